```python
import jax, jax.numpy as jnp
from jax import lax
import numpy as np


D_MODEL = 1024
BATCH = 16
SEQ = 2048
DEPTH = 4

GRID_W = 64
HEAD_DIM = 64
N_Q_HEADS = 8
N_KV_HEADS = 2
Q_PER_KV = N_Q_HEADS // N_KV_HEADS
ATTN_WIDTH = N_Q_HEADS * HEAD_DIM
KV_WIDTH = N_KV_HEADS * HEAD_DIM
ROPE_THETA = 10000.0
Q_BLOCK = 128
CONV_WIDTH = D_MODEL // 2
POOL_WINDOWS = (2, 4, 8, 16)
N_POOL_GROUPS = 4
POOL_WIDTH = D_MODEL // 2
POOL_GROUP = POOL_WIDTH // N_POOL_GROUPS
SGU_WIDTH = D_MODEL // 2
N_SGU_GROUPS = 4
SGU_GROUP = SGU_WIDTH // N_SGU_GROUPS
SGU_CHUNK = 128
D_FF = 2816
EPS = 1e-6
EVEN_IN = 3 * CONV_WIDTH + ATTN_WIDTH + 2 * KV_WIDTH
EVEN_SPLITS = (CONV_WIDTH, 2 * CONV_WIDTH, 3 * CONV_WIDTH,
               3 * CONV_WIDTH + ATTN_WIDTH, 3 * CONV_WIDTH + ATTN_WIDTH + KV_WIDTH)
EVEN_MIX = CONV_WIDTH + ATTN_WIDTH
ODD_IN = POOL_WIDTH + 2 * SGU_WIDTH
ODD_SPLITS = (POOL_WIDTH, POOL_WIDTH + SGU_WIDTH)
ODD_MIX = POOL_WIDTH + SGU_WIDTH

kernel_name = 'hybrid_conv_gqa_pool_sgu_macaron_encoder'


def rms_norm(x, g):
    xf = x.astype(jnp.float32)
    y = xf * lax.rsqrt(jnp.mean(xf * xf, axis=-1, keepdims=True) + EPS)
    return (y * g.astype(jnp.float32)).astype(x.dtype)


def swiglu(x, w_in, w_out):
    g, u = jnp.split(x @ w_in, 2, axis=-1)
    return (jax.nn.silu(g) * u) @ w_out


def axial_rope_tables(seq):
    rows = seq // GRID_W
    r_idx, c_idx = jnp.meshgrid(jnp.arange(rows), jnp.arange(GRID_W), indexing='ij')
    r_idx = r_idx.reshape(-1).astype(jnp.float32)
    c_idx = c_idx.reshape(-1).astype(jnp.float32)
    n_freq = HEAD_DIM // 4
    inv = ROPE_THETA ** (-jnp.arange(n_freq, dtype=jnp.float32) / n_freq)
    ang = jnp.concatenate([r_idx[:, None] * inv, c_idx[:, None] * inv], axis=-1)
    return jnp.cos(ang), jnp.sin(ang)


def apply_rope(x, cos, sin):
    b, s, h, d = x.shape
    xf = x.astype(jnp.float32).reshape(b, s, h, d // 2, 2)
    x0, x1 = xf[..., 0], xf[..., 1]
    c = cos[None, :, None, :]
    sn = sin[None, :, None, :]
    out = jnp.stack([x0 * c - x1 * sn, x0 * sn + x1 * c], axis=-1)
    return out.reshape(b, s, h, d).astype(x.dtype)


def blocked_gqa(q, k, v):
    b, s, _, _ = q.shape
    nblk = s // Q_BLOCK
    qb = q.reshape(b, nblk, Q_BLOCK, N_KV_HEADS, Q_PER_KV, HEAD_DIM).transpose(1, 0, 2, 3, 4, 5)
    scale = HEAD_DIM ** -0.5

    def one_block(qi):
        sc = jnp.einsum('bqkgd,bskd->bkgqs', qi, k).astype(jnp.float32) * scale
        p = jax.nn.softmax(sc, axis=-1).astype(v.dtype)
        return jnp.einsum('bkgqs,bskd->bqkgd', p, v)

    o = lax.map(one_block, qb)
    return o.transpose(1, 0, 2, 3, 4, 5).reshape(b, s, ATTN_WIDTH)


def short_conv(h, w):
    hp = jnp.pad(h, ((0, 0), (1, 1), (0, 0)))
    return hp[:, :-2] * w[0] + hp[:, 1:-1] * w[1] + hp[:, 2:] * w[2]


def even_mixer(h, w_in, conv_w, q_g, k_g, w_out, cos, sin):
    b, s, _ = h.shape
    gate_b, gate_c, hc, q, k, v = jnp.split(h @ w_in, list(EVEN_SPLITS), axis=-1)
    a_out = gate_b * short_conv(gate_c * hc, conv_w)
    q = apply_rope(rms_norm(q.reshape(b, s, N_Q_HEADS, HEAD_DIM), q_g), cos, sin)
    k = apply_rope(rms_norm(k.reshape(b, s, N_KV_HEADS, HEAD_DIM), k_g), cos, sin)
    v = v.reshape(b, s, N_KV_HEADS, HEAD_DIM)
    b_out = blocked_gqa(q, k, v)
    return jnp.concatenate([a_out, b_out], axis=-1) @ w_out


def multiscale_pool(p):
    b, s, _ = p.shape
    pf = p.astype(jnp.float32)
    cs = jnp.concatenate([jnp.zeros((b, 1, POOL_WIDTH), jnp.float32), jnp.cumsum(pf, axis=1)], axis=1)
    t = jnp.arange(s)
    outs = []
    for gi, w in enumerate(POOL_WINDOWS):
        r = w // 2
        lo = jnp.maximum(t - r, 0)
        hi = jnp.minimum(t + r, s - 1)
        sl = slice(gi * POOL_GROUP, (gi + 1) * POOL_GROUP)
        csg = cs[:, :, sl]
        win = csg[:, hi + 1] - csg[:, lo]
        cnt = (hi - lo + 1).astype(jnp.float32)[None, :, None]
        outs.append(win / cnt - pf[:, :, sl])
    return jnp.concatenate(outs, axis=-1).astype(p.dtype)


def chunked_sgu(u, v, norm_g, w_s, b_s):
    b, s, _ = u.shape
    v = rms_norm(v, norm_g)
    n = s // SGU_CHUNK
    vc = v.reshape(b, n, SGU_CHUNK, N_SGU_GROUPS, SGU_GROUP)
    mixed = jnp.einsum('gpq,bnqgc->bnpgc', w_s, vc) + b_s.T[None, None, :, :, None]
    return u * mixed.reshape(b, s, SGU_WIDTH)


def odd_mixer(h, w_in, pool_w, pool_scale, sgu_norm, sgu_w, sgu_b, w_out):
    b, s, _ = h.shape
    p, u, v = jnp.split(h @ w_in, list(ODD_SPLITS), axis=-1)
    pooled = multiscale_pool(p).reshape(b, s, N_POOL_GROUPS, POOL_GROUP)
    c_out = jnp.einsum('bsgc,gcd->bsgd', pooled, pool_w).reshape(b, s, POOL_WIDTH) * pool_scale
    d_out = chunked_sgu(jax.nn.gelu(u), jax.nn.gelu(v), sgu_norm, sgu_w, sgu_b)
    return jnp.concatenate([c_out, d_out], axis=-1) @ w_out


def _fwd_setup_inputs(seed: int = 0) -> dict:
    key = jax.random.key(seed)
    ks = jax.random.split(key, 24)
    n_even = (DEPTH + 1) // 2
    n_odd = DEPTH // 2
    f32 = jnp.float32

    def nrm(k, shape, scale):
        return jax.random.normal(k, shape, f32) * scale

    def gain(k, shape):
        return 1.0 + 0.02 * jax.random.normal(k, shape, f32)

    return {
        'x': jax.random.normal(ks[0], (BATCH, SEQ, D_MODEL), f32),
        'ffn1_norm': gain(ks[1], (DEPTH, D_MODEL)),
        'ffn1_w_in': nrm(ks[2], (DEPTH, D_MODEL, 2 * D_FF), D_MODEL ** -0.5),
        'ffn1_w_out': nrm(ks[3], (DEPTH, D_FF, D_MODEL), D_FF ** -0.5),
        'mix_norm': gain(ks[4], (DEPTH, D_MODEL)),
        'ffn2_norm': gain(ks[5], (DEPTH, D_MODEL)),
        'ffn2_w_in': nrm(ks[6], (DEPTH, D_MODEL, 2 * D_FF), D_MODEL ** -0.5),
        'ffn2_w_out': nrm(ks[7], (DEPTH, D_FF, D_MODEL), D_FF ** -0.5),
        'ev_w_in': nrm(ks[8], (n_even, D_MODEL, EVEN_IN), D_MODEL ** -0.5),
        'ev_conv_w': nrm(ks[9], (n_even, 3, CONV_WIDTH), 3 ** -0.5),
        'ev_q_norm': gain(ks[10], (n_even, HEAD_DIM)),
        'ev_k_norm': gain(ks[11], (n_even, HEAD_DIM)),
        'ev_w_out': nrm(ks[12], (n_even, EVEN_MIX, D_MODEL), EVEN_MIX ** -0.5),
        'od_w_in': nrm(ks[13], (n_odd, D_MODEL, ODD_IN), D_MODEL ** -0.5),
        'od_pool_w': nrm(ks[14], (n_odd, N_POOL_GROUPS, POOL_GROUP, POOL_GROUP), POOL_GROUP ** -0.5),
        'od_pool_scale': 1.0 + 0.1 * jax.random.normal(ks[15], (n_odd, POOL_WIDTH), f32),
        'od_sgu_norm': gain(ks[16], (n_odd, SGU_WIDTH)),
        'od_sgu_w': nrm(ks[17], (n_odd, N_SGU_GROUPS, SGU_CHUNK, SGU_CHUNK), SGU_CHUNK ** -0.5),
        'od_sgu_b': 1.0 + 0.01 * jax.random.normal(ks[18], (n_odd, N_SGU_GROUPS, SGU_CHUNK), f32),
        'od_w_out': nrm(ks[19], (n_odd, ODD_MIX, D_MODEL), ODD_MIX ** -0.5),
        'final_norm': gain(ks[20], (D_MODEL,)),
    }


def _fwd_reference(x, ffn1_norm, ffn1_w_in, ffn1_w_out, mix_norm, ffn2_norm, ffn2_w_in, ffn2_w_out,
              ev_w_in, ev_conv_w, ev_q_norm, ev_k_norm, ev_w_out,
              od_w_in, od_pool_w, od_pool_scale, od_sgu_norm, od_sgu_w, od_sgu_b, od_w_out,
              final_norm):
    s = x.shape[1]
    cos, sin = axial_rope_tables(s)
    for layer in range(DEPTH):
        x = x + 0.5 * swiglu(rms_norm(x, ffn1_norm[layer]), ffn1_w_in[layer], ffn1_w_out[layer])
        h = rms_norm(x, mix_norm[layer])
        j = layer // 2
        if layer % 2 == 0:
            x = x + even_mixer(h, ev_w_in[j], ev_conv_w[j], ev_q_norm[j], ev_k_norm[j], ev_w_out[j], cos, sin)
        else:
            x = x + odd_mixer(h, od_w_in[j], od_pool_w[j], od_pool_scale[j], od_sgu_norm[j],
                              od_sgu_w[j], od_sgu_b[j], od_w_out[j])
        x = x + 0.5 * swiglu(rms_norm(x, ffn2_norm[layer]), ffn2_w_in[layer], ffn2_w_out[layer])
    return rms_norm(x, final_norm)


import jax as _jax
import jax.numpy as _jnp

TWIN_FORMAT = 'train_step'
FWD_PARAMS = ['x', 'ffn1_norm', 'ffn1_w_in', 'ffn1_w_out', 'mix_norm', 'ffn2_norm', 'ffn2_w_in', 'ffn2_w_out', 'ev_w_in', 'ev_conv_w', 'ev_q_norm', 'ev_k_norm', 'ev_w_out', 'od_w_in', 'od_pool_w', 'od_pool_scale', 'od_sgu_norm', 'od_sgu_w', 'od_sgu_b', 'od_w_out', 'final_norm']
TWIN_WEIGHTS = ['ffn1_norm', 'ffn1_w_in', 'ffn1_w_out', 'mix_norm', 'ffn2_norm', 'ffn2_w_in', 'ffn2_w_out', 'ev_w_in', 'ev_conv_w', 'ev_q_norm', 'ev_k_norm', 'ev_w_out', 'od_w_in', 'od_pool_w', 'od_pool_scale', 'od_sgu_norm', 'od_sgu_w', 'od_sgu_b', 'od_w_out', 'final_norm']
TWIN_DIFF_INPUT = 'x'
TWIN_INPUTS = ['x', 'ffn1_norm', 'ffn1_w_in', 'ffn1_w_out', 'mix_norm', 'ffn2_norm', 'ffn2_w_in', 'ffn2_w_out', 'ev_w_in', 'ev_conv_w', 'ev_q_norm', 'ev_k_norm', 'ev_w_out', 'od_w_in', 'od_pool_w', 'od_pool_scale', 'od_sgu_norm', 'od_sgu_w', 'od_sgu_b', 'od_w_out', 'final_norm', 'loss_target', 'm_ffn1_norm', 'm_ffn1_w_in', 'm_ffn1_w_out', 'm_mix_norm', 'm_ffn2_norm', 'm_ffn2_w_in', 'm_ffn2_w_out', 'm_ev_w_in', 'm_ev_conv_w', 'm_ev_q_norm', 'm_ev_k_norm', 'm_ev_w_out', 'm_od_w_in', 'm_od_pool_w', 'm_od_pool_scale', 'm_od_sgu_norm', 'm_od_sgu_w', 'm_od_sgu_b', 'm_od_w_out', 'm_final_norm', 'v_ffn1_norm', 'v_ffn1_w_in', 'v_ffn1_w_out', 'v_mix_norm', 'v_ffn2_norm', 'v_ffn2_w_in', 'v_ffn2_w_out', 'v_ev_w_in', 'v_ev_conv_w', 'v_ev_q_norm', 'v_ev_k_norm', 'v_ev_w_out', 'v_od_w_in', 'v_od_pool_w', 'v_od_pool_scale', 'v_od_sgu_norm', 'v_od_sgu_w', 'v_od_sgu_b', 'v_od_w_out', 'v_final_norm']
TWIN_OUTPUTS = ['loss', 'grad_x', 'grad_ffn1_norm', 'grad_ffn1_w_in', 'grad_ffn1_w_out', 'grad_mix_norm', 'grad_ffn2_norm', 'grad_ffn2_w_in', 'grad_ffn2_w_out', 'grad_ev_w_in', 'grad_ev_conv_w', 'grad_ev_q_norm', 'grad_ev_k_norm', 'grad_ev_w_out', 'grad_od_w_in', 'grad_od_pool_w', 'grad_od_pool_scale', 'grad_od_sgu_norm', 'grad_od_sgu_w', 'grad_od_sgu_b', 'grad_od_w_out', 'grad_final_norm', 'delta_ffn1_norm', 'delta_ffn1_w_in', 'delta_ffn1_w_out', 'delta_mix_norm', 'delta_ffn2_norm', 'delta_ffn2_w_in', 'delta_ffn2_w_out', 'delta_ev_w_in', 'delta_ev_conv_w', 'delta_ev_q_norm', 'delta_ev_k_norm', 'delta_ev_w_out', 'delta_od_w_in', 'delta_od_pool_w', 'delta_od_pool_scale', 'delta_od_sgu_norm', 'delta_od_sgu_w', 'delta_od_sgu_b', 'delta_od_w_out', 'delta_final_norm', 'new_m_ffn1_norm', 'new_m_ffn1_w_in', 'new_m_ffn1_w_out', 'new_m_mix_norm', 'new_m_ffn2_norm', 'new_m_ffn2_w_in', 'new_m_ffn2_w_out', 'new_m_ev_w_in', 'new_m_ev_conv_w', 'new_m_ev_q_norm', 'new_m_ev_k_norm', 'new_m_ev_w_out', 'new_m_od_w_in', 'new_m_od_pool_w', 'new_m_od_pool_scale', 'new_m_od_sgu_norm', 'new_m_od_sgu_w', 'new_m_od_sgu_b', 'new_m_od_w_out', 'new_m_final_norm', 'new_v_ffn1_norm', 'new_v_ffn1_w_in', 'new_v_ffn1_w_out', 'new_v_mix_norm', 'new_v_ffn2_norm', 'new_v_ffn2_w_in', 'new_v_ffn2_w_out', 'new_v_ev_w_in', 'new_v_ev_conv_w', 'new_v_ev_q_norm', 'new_v_ev_k_norm', 'new_v_ev_w_out', 'new_v_od_w_in', 'new_v_od_pool_w', 'new_v_od_pool_scale', 'new_v_od_sgu_norm', 'new_v_od_sgu_w', 'new_v_od_sgu_b', 'new_v_od_w_out', 'new_v_final_norm']
TWIN_LEAF_KINDS = {'loss': 'loss', 'grad_x': 'grad_x', 'grad_ffn1_norm': 'grad_w', 'grad_ffn1_w_in': 'grad_w', 'grad_ffn1_w_out': 'grad_w', 'grad_mix_norm': 'grad_w', 'grad_ffn2_norm': 'grad_w', 'grad_ffn2_w_in': 'grad_w', 'grad_ffn2_w_out': 'grad_w', 'grad_ev_w_in': 'grad_w', 'grad_ev_conv_w': 'grad_w', 'grad_ev_q_norm': 'grad_w', 'grad_ev_k_norm': 'grad_w', 'grad_ev_w_out': 'grad_w', 'grad_od_w_in': 'grad_w', 'grad_od_pool_w': 'grad_w', 'grad_od_pool_scale': 'grad_w', 'grad_od_sgu_norm': 'grad_w', 'grad_od_sgu_w': 'grad_w', 'grad_od_sgu_b': 'grad_w', 'grad_od_w_out': 'grad_w', 'grad_final_norm': 'grad_w', 'delta_ffn1_norm': 'delta_w', 'delta_ffn1_w_in': 'delta_w', 'delta_ffn1_w_out': 'delta_w', 'delta_mix_norm': 'delta_w', 'delta_ffn2_norm': 'delta_w', 'delta_ffn2_w_in': 'delta_w', 'delta_ffn2_w_out': 'delta_w', 'delta_ev_w_in': 'delta_w', 'delta_ev_conv_w': 'delta_w', 'delta_ev_q_norm': 'delta_w', 'delta_ev_k_norm': 'delta_w', 'delta_ev_w_out': 'delta_w', 'delta_od_w_in': 'delta_w', 'delta_od_pool_w': 'delta_w', 'delta_od_pool_scale': 'delta_w', 'delta_od_sgu_norm': 'delta_w', 'delta_od_sgu_w': 'delta_w', 'delta_od_sgu_b': 'delta_w', 'delta_od_w_out': 'delta_w', 'delta_final_norm': 'delta_w', 'new_m_ffn1_norm': 'new_m', 'new_m_ffn1_w_in': 'new_m', 'new_m_ffn1_w_out': 'new_m', 'new_m_mix_norm': 'new_m', 'new_m_ffn2_norm': 'new_m', 'new_m_ffn2_w_in': 'new_m', 'new_m_ffn2_w_out': 'new_m', 'new_m_ev_w_in': 'new_m', 'new_m_ev_conv_w': 'new_m', 'new_m_ev_q_norm': 'new_m', 'new_m_ev_k_norm': 'new_m', 'new_m_ev_w_out': 'new_m', 'new_m_od_w_in': 'new_m', 'new_m_od_pool_w': 'new_m', 'new_m_od_pool_scale': 'new_m', 'new_m_od_sgu_norm': 'new_m', 'new_m_od_sgu_w': 'new_m', 'new_m_od_sgu_b': 'new_m', 'new_m_od_w_out': 'new_m', 'new_m_final_norm': 'new_m', 'new_v_ffn1_norm': 'new_v', 'new_v_ffn1_w_in': 'new_v', 'new_v_ffn1_w_out': 'new_v', 'new_v_mix_norm': 'new_v', 'new_v_ffn2_norm': 'new_v', 'new_v_ffn2_w_in': 'new_v', 'new_v_ffn2_w_out': 'new_v', 'new_v_ev_w_in': 'new_v', 'new_v_ev_conv_w': 'new_v', 'new_v_ev_q_norm': 'new_v', 'new_v_ev_k_norm': 'new_v', 'new_v_ev_w_out': 'new_v', 'new_v_od_w_in': 'new_v', 'new_v_od_pool_w': 'new_v', 'new_v_od_pool_scale': 'new_v', 'new_v_od_sgu_norm': 'new_v', 'new_v_od_sgu_w': 'new_v', 'new_v_od_sgu_b': 'new_v', 'new_v_od_w_out': 'new_v', 'new_v_final_norm': 'new_v'}


def _forward(args):
    return _fwd_reference(*[args[k] for k in FWD_PARAMS])


def _output_shape():
    out = _jax.eval_shape(lambda: _forward(_fwd_setup_inputs(0)))
    return out.shape, out.dtype

N_MICROBATCH = 1
ADAM_LR = 0.001
ADAM_B1 = 0.9
ADAM_B2 = 0.999
ADAM_EPS = 1e-08
ADAM_WD = 0.01
ADAM_STEP = 10
PER_EXAMPLE_BATCH_AXIS = {'x': 0, 'loss_target': 0}
SHARED_INPUTS = []
_WEIGHT_DTYPES = {'ffn1_norm': _jnp.float32, 'ffn1_w_in': _jnp.float32, 'ffn1_w_out': _jnp.float32, 'mix_norm': _jnp.float32, 'ffn2_norm': _jnp.float32, 'ffn2_w_in': _jnp.float32, 'ffn2_w_out': _jnp.float32, 'ev_w_in': _jnp.float32, 'ev_conv_w': _jnp.float32, 'ev_q_norm': _jnp.float32, 'ev_k_norm': _jnp.float32, 'ev_w_out': _jnp.float32, 'od_w_in': _jnp.float32, 'od_pool_w': _jnp.float32, 'od_pool_scale': _jnp.float32, 'od_sgu_norm': _jnp.float32, 'od_sgu_w': _jnp.float32, 'od_sgu_b': _jnp.float32, 'od_w_out': _jnp.float32, 'final_norm': _jnp.float32}
MOMENT_SCALE = {'ffn1_norm': 8.256088e-02, 'ffn1_w_in': 3.557366e-02, 'ffn1_w_out': 5.798185e-02, 'mix_norm': 1.607372e-01, 'ffn2_norm': 5.647253e-02, 'ffn2_w_in': 2.416271e-02, 'ffn2_w_out': 3.938405e-02, 'ev_w_in': 1.303088e-01, 'ev_conv_w': 1.560698e-01, 'ev_q_norm': 4.086447e-02, 'ev_k_norm': 4.225186e-02, 'ev_w_out': 1.119124e-01, 'od_w_in': 9.868089e-02, 'od_pool_w': 1.058943e-01, 'od_pool_scale': 1.123917e-01, 'od_sgu_norm': 8.228657e-02, 'od_sgu_w': 7.508626e-02, 'od_sgu_b': 7.521638e-02, 'od_w_out': 1.060333e-01, 'final_norm': 3.208057e+01}


def _to_microbatches(a, axis):
    t = _jnp.moveaxis(a, axis, 0)
    t = t.reshape((N_MICROBATCH, t.shape[0] // N_MICROBATCH) + t.shape[1:])
    return _jnp.moveaxis(t, 1, axis + 1)


def setup_inputs(seed: int = 0) -> dict:
    inp = _fwd_setup_inputs(seed)
    key = _jax.random.fold_in(_jax.random.key(seed), 7919)
    shape, _ = _output_shape()
    out = dict(inp)
    out["loss_target"] = _jax.random.normal(_jax.random.fold_in(key, 0), shape, _jnp.float32)
    for i, name in enumerate(TWIN_WEIGHTS):
        w = inp[name].astype(_jnp.float32)
        if MOMENT_SCALE is None:
            s = _jnp.sqrt(_jnp.mean(_jnp.square(w)) + 1e-30)
        else:
            s = MOMENT_SCALE[name]
        km, kv = _jax.random.split(_jax.random.fold_in(key, i + 1))
        out[name] = w
        out["m_" + name] = s * _jax.random.normal(km, w.shape, _jnp.float32)
        out["v_" + name] = (s * s) * _jax.random.uniform(kv, w.shape, _jnp.float32, 0.5, 1.5)
    if N_MICROBATCH > 1:
        for name, axis in PER_EXAMPLE_BATCH_AXIS.items():
            out[name] = _to_microbatches(out[name], axis)
    return {'x': out['x'], 'ffn1_norm': out['ffn1_norm'], 'ffn1_w_in': out['ffn1_w_in'], 'ffn1_w_out': out['ffn1_w_out'], 'mix_norm': out['mix_norm'], 'ffn2_norm': out['ffn2_norm'], 'ffn2_w_in': out['ffn2_w_in'], 'ffn2_w_out': out['ffn2_w_out'], 'ev_w_in': out['ev_w_in'], 'ev_conv_w': out['ev_conv_w'], 'ev_q_norm': out['ev_q_norm'], 'ev_k_norm': out['ev_k_norm'], 'ev_w_out': out['ev_w_out'], 'od_w_in': out['od_w_in'], 'od_pool_w': out['od_pool_w'], 'od_pool_scale': out['od_pool_scale'], 'od_sgu_norm': out['od_sgu_norm'], 'od_sgu_w': out['od_sgu_w'], 'od_sgu_b': out['od_sgu_b'], 'od_w_out': out['od_w_out'], 'final_norm': out['final_norm'], 'loss_target': out['loss_target'], 'm_ffn1_norm': out['m_ffn1_norm'], 'm_ffn1_w_in': out['m_ffn1_w_in'], 'm_ffn1_w_out': out['m_ffn1_w_out'], 'm_mix_norm': out['m_mix_norm'], 'm_ffn2_norm': out['m_ffn2_norm'], 'm_ffn2_w_in': out['m_ffn2_w_in'], 'm_ffn2_w_out': out['m_ffn2_w_out'], 'm_ev_w_in': out['m_ev_w_in'], 'm_ev_conv_w': out['m_ev_conv_w'], 'm_ev_q_norm': out['m_ev_q_norm'], 'm_ev_k_norm': out['m_ev_k_norm'], 'm_ev_w_out': out['m_ev_w_out'], 'm_od_w_in': out['m_od_w_in'], 'm_od_pool_w': out['m_od_pool_w'], 'm_od_pool_scale': out['m_od_pool_scale'], 'm_od_sgu_norm': out['m_od_sgu_norm'], 'm_od_sgu_w': out['m_od_sgu_w'], 'm_od_sgu_b': out['m_od_sgu_b'], 'm_od_w_out': out['m_od_w_out'], 'm_final_norm': out['m_final_norm'], 'v_ffn1_norm': out['v_ffn1_norm'], 'v_ffn1_w_in': out['v_ffn1_w_in'], 'v_ffn1_w_out': out['v_ffn1_w_out'], 'v_mix_norm': out['v_mix_norm'], 'v_ffn2_norm': out['v_ffn2_norm'], 'v_ffn2_w_in': out['v_ffn2_w_in'], 'v_ffn2_w_out': out['v_ffn2_w_out'], 'v_ev_w_in': out['v_ev_w_in'], 'v_ev_conv_w': out['v_ev_conv_w'], 'v_ev_q_norm': out['v_ev_q_norm'], 'v_ev_k_norm': out['v_ev_k_norm'], 'v_ev_w_out': out['v_ev_w_out'], 'v_od_w_in': out['v_od_w_in'], 'v_od_pool_w': out['v_od_pool_w'], 'v_od_pool_scale': out['v_od_pool_scale'], 'v_od_sgu_norm': out['v_od_sgu_norm'], 'v_od_sgu_w': out['v_od_sgu_w'], 'v_od_sgu_b': out['v_od_sgu_b'], 'v_od_w_out': out['v_od_w_out'], 'v_final_norm': out['v_final_norm']}


def _loss(weights, diff, rest, loss_target):
    with _jax.named_scope("forward"):
        args = {**rest, TWIN_DIFF_INPUT: diff, **{k: w.astype(_WEIGHT_DTYPES[k]) for k, w in weights.items()}}
        y = _forward(args)
    with _jax.named_scope("loss_head"):
        err = _jnp.square(y.astype(_jnp.float32) - loss_target)
        return 0.5 * _jnp.sum(_jnp.mean(err, axis=-1)) if err.ndim else 0.5 * err


def _adamw(w, g, m, v):
    m = ADAM_B1 * m + (1.0 - ADAM_B1) * g
    v = ADAM_B2 * v + (1.0 - ADAM_B2) * _jnp.square(g)
    m_hat = m / (1.0 - ADAM_B1 ** ADAM_STEP)
    v_hat = v / (1.0 - ADAM_B2 ** ADAM_STEP)
    delta = -ADAM_LR * (m_hat / (_jnp.sqrt(v_hat) + ADAM_EPS) + ADAM_WD * w)
    return delta, m, v


def reference(x, ffn1_norm, ffn1_w_in, ffn1_w_out, mix_norm, ffn2_norm, ffn2_w_in, ffn2_w_out, ev_w_in, ev_conv_w, ev_q_norm, ev_k_norm, ev_w_out, od_w_in, od_pool_w, od_pool_scale, od_sgu_norm, od_sgu_w, od_sgu_b, od_w_out, final_norm, loss_target, m_ffn1_norm, m_ffn1_w_in, m_ffn1_w_out, m_mix_norm, m_ffn2_norm, m_ffn2_w_in, m_ffn2_w_out, m_ev_w_in, m_ev_conv_w, m_ev_q_norm, m_ev_k_norm, m_ev_w_out, m_od_w_in, m_od_pool_w, m_od_pool_scale, m_od_sgu_norm, m_od_sgu_w, m_od_sgu_b, m_od_w_out, m_final_norm, v_ffn1_norm, v_ffn1_w_in, v_ffn1_w_out, v_mix_norm, v_ffn2_norm, v_ffn2_w_in, v_ffn2_w_out, v_ev_w_in, v_ev_conv_w, v_ev_q_norm, v_ev_k_norm, v_ev_w_out, v_od_w_in, v_od_pool_w, v_od_pool_scale, v_od_sgu_norm, v_od_sgu_w, v_od_sgu_b, v_od_w_out, v_final_norm):
    given = dict(x=x, ffn1_norm=ffn1_norm, ffn1_w_in=ffn1_w_in, ffn1_w_out=ffn1_w_out, mix_norm=mix_norm, ffn2_norm=ffn2_norm, ffn2_w_in=ffn2_w_in, ffn2_w_out=ffn2_w_out, ev_w_in=ev_w_in, ev_conv_w=ev_conv_w, ev_q_norm=ev_q_norm, ev_k_norm=ev_k_norm, ev_w_out=ev_w_out, od_w_in=od_w_in, od_pool_w=od_pool_w, od_pool_scale=od_pool_scale, od_sgu_norm=od_sgu_norm, od_sgu_w=od_sgu_w, od_sgu_b=od_sgu_b, od_w_out=od_w_out, final_norm=final_norm, loss_target=loss_target, m_ffn1_norm=m_ffn1_norm, m_ffn1_w_in=m_ffn1_w_in, m_ffn1_w_out=m_ffn1_w_out, m_mix_norm=m_mix_norm, m_ffn2_norm=m_ffn2_norm, m_ffn2_w_in=m_ffn2_w_in, m_ffn2_w_out=m_ffn2_w_out, m_ev_w_in=m_ev_w_in, m_ev_conv_w=m_ev_conv_w, m_ev_q_norm=m_ev_q_norm, m_ev_k_norm=m_ev_k_norm, m_ev_w_out=m_ev_w_out, m_od_w_in=m_od_w_in, m_od_pool_w=m_od_pool_w, m_od_pool_scale=m_od_pool_scale, m_od_sgu_norm=m_od_sgu_norm, m_od_sgu_w=m_od_sgu_w, m_od_sgu_b=m_od_sgu_b, m_od_w_out=m_od_w_out, m_final_norm=m_final_norm, v_ffn1_norm=v_ffn1_norm, v_ffn1_w_in=v_ffn1_w_in, v_ffn1_w_out=v_ffn1_w_out, v_mix_norm=v_mix_norm, v_ffn2_norm=v_ffn2_norm, v_ffn2_w_in=v_ffn2_w_in, v_ffn2_w_out=v_ffn2_w_out, v_ev_w_in=v_ev_w_in, v_ev_conv_w=v_ev_conv_w, v_ev_q_norm=v_ev_q_norm, v_ev_k_norm=v_ev_k_norm, v_ev_w_out=v_ev_w_out, v_od_w_in=v_od_w_in, v_od_pool_w=v_od_pool_w, v_od_pool_scale=v_od_pool_scale, v_od_sgu_norm=v_od_sgu_norm, v_od_sgu_w=v_od_sgu_w, v_od_sgu_b=v_od_sgu_b, v_od_w_out=v_od_w_out, v_final_norm=v_final_norm)
    weights = {n: given[n] for n in TWIN_WEIGHTS}
    shared = {n: given[n] for n in SHARED_INPUTS}
    per_example = {n: given[n] for n in ['x']}
    grad_fn = _jax.value_and_grad(_loss, argnums=(0, 1))

    def one_microbatch(ex, loss_target):
        ex = dict(ex)
        diff = ex.pop(TWIN_DIFF_INPUT)
        return grad_fn(weights, diff, {**shared, **ex}, loss_target)

    if N_MICROBATCH == 1:
        loss, (grad_w, grad_x) = one_microbatch(per_example, given["loss_target"])
    else:
        def body(carry, xs):
            loss_sum, grad_sum = carry
            l_k, (gw_k, gx_k) = one_microbatch(xs[0], xs[1])
            with _jax.named_scope("update"):
                return (loss_sum + l_k, _jax.tree.map(_jnp.add, grad_sum, gw_k)), gx_k

        init = (_jnp.zeros((), _jnp.float32), _jax.tree.map(_jnp.zeros_like, weights))
        (loss, grad_w), grad_x = _jax.lax.scan(body, init, (per_example, given["loss_target"]))
    with _jax.named_scope("update"):
        delta_w, new_m, new_v = {}, {}, {}
        for n in TWIN_WEIGHTS:
            delta_w[n], new_m[n], new_v[n] = _adamw(weights[n], grad_w[n], given["m_" + n], given["v_" + n])
    return (loss, grad_x, *[grad_w[n] for n in TWIN_WEIGHTS], *[delta_w[n] for n in TWIN_WEIGHTS],
            *[new_m[n] for n in TWIN_WEIGHTS], *[new_v[n] for n in TWIN_WEIGHTS])
```

```python
import jax
import jax.numpy as jnp
from jax import lax
from jax.experimental import pallas as pl
from jax.experimental.pallas import tpu as pltpu

F32 = jnp.float32
_MXU = jnp.bfloat16
_ACT = jnp.bfloat16

D_MODEL = 1024
GRID_W = 64
HEAD_DIM = 64
N_Q_HEADS = 8
N_KV_HEADS = 2
Q_PER_KV = N_Q_HEADS // N_KV_HEADS
ATTN_WIDTH = N_Q_HEADS * HEAD_DIM
KV_WIDTH = N_KV_HEADS * HEAD_DIM
ROPE_THETA = 10000.0
CONV_WIDTH = D_MODEL // 2
POOL_RADII = (1, 2, 4, 8)
POOL_GROUP = 128
SGU_GROUP = 128
SGU_CHUNK = 128
N_GROUPS = 4
HALF = D_MODEL // 2
EPS = 1e-6
HALO = 8
LANES = 128
N_CHIPS = 4
N_DEV = 8

ADAM_LR = 0.001
ADAM_B1 = 0.9
ADAM_B2 = 0.999
ADAM_EPS = 1e-08
ADAM_WD = 0.01
ADAM_STEP = 10

_VMEM_LIMIT = 56 * 2 ** 20
_MESH = pl.DeviceIdType.MESH
_ANY = pl.BlockSpec(memory_space=pl.ANY)
_VMEM = pl.BlockSpec(memory_space=pltpu.VMEM)

_DN = {
    "nn": (((1,), (0,)), ((), ())),
    "nt": (((1,), (1,)), ((), ())),
    "tn": (((0,), (0,)), ((), ())),
}


def _params(*sem):
    return pltpu.CompilerParams(dimension_semantics=sem, vmem_limit_bytes=_VMEM_LIMIT)


def _tile(n, cap):
    best = None
    d = LANES
    while d <= min(n, cap):
        if n % d == 0:
            best = d
        d += LANES
    return best if best is not None else n


def _dot(a, b, mode="nn"):
    return lax.dot_general(a.astype(_MXU), b.astype(_MXU), _DN[mode], preferred_element_type=F32)


def _cat(*vals):
    vals = [v.astype(_MXU) for v in vals]
    return vals[0] if len(vals) == 1 else jnp.concatenate(vals, axis=1)


def _sigmoid(g):
    return 1.0 / (1.0 + jnp.exp(-g))


def _swiglu(g, u):
    g = g.astype(F32)
    return (g * _sigmoid(g)) * u.astype(F32)


_GELU_C = 0.7978845608028654


def _gelu(x):
    return 0.5 * x * (1.0 + jnp.tanh(_GELU_C * (x + 0.044715 * (x * x * x))))


def _gelu_grad(x):
    t = jnp.tanh(_GELU_C * (x + 0.044715 * (x * x * x)))
    return 0.5 * (1.0 + t) + 0.5 * x * (1.0 - t * t) * (_GELU_C * (1.0 + 3.0 * 0.044715 * (x * x)))


def _mm(name, grid, mode, a_ops, b_ops, e_ops, out_shape, out_specs, acc_shape, a_fn=_cat, b_fn=_cat, epi=None):
    ni, nj, nk = grid
    na, nb, ne = len(a_ops), len(b_ops), len(e_ops)
    multi = isinstance(out_shape, (list, tuple))
    no = len(out_shape) if multi else 1

    def body(*refs):
        a_refs = refs[:na]
        b_refs = refs[na:na + nb]
        e_refs = refs[na + nb:na + nb + ne]
        o_refs = refs[na + nb + ne:na + nb + ne + no]
        a = a_fn(*[r[...] for r in a_refs])
        b = b_fn(*[r[...] for r in b_refs])
        p = _dot(a, b, mode)

        def finish(acc):
            if epi is None:
                o_refs[0][...] = acc.astype(o_refs[0].dtype)
            else:
                epi(acc, [r[...] for r in e_refs], o_refs)

        if nk == 1:
            finish(p)
        else:
            acc_ref = refs[-1]
            k = pl.program_id(2)

            @pl.when(k == 0)
            def _():
                acc_ref[...] = p

            @pl.when(k > 0)
            def _():
                acc_ref[...] += p

            @pl.when(k == nk - 1)
            def _():
                finish(acc_ref[...])

    ops = list(a_ops) + list(b_ops) + list(e_ops)
    return pl.pallas_call(
        body, name=name, grid=grid,
        in_specs=[s for _, s in ops],
        out_specs=out_specs, out_shape=out_shape,
        scratch_shapes=[pltpu.VMEM(acc_shape, F32)] if nk > 1 else [],
        compiler_params=_params("parallel", "parallel", "arbitrary"),
    )(*[a for a, _ in ops])


def _rows(t):
    return _tile(t, 512)


def _rmsnorm_fwd(name, x, gain):
    t, d = x.shape
    tr = _rows(t)

    def body(x_ref, g_ref, h_ref):
        xf = x_ref[...]
        r = lax.rsqrt(jnp.mean(xf * xf, axis=-1, keepdims=True) + EPS)
        h_ref[...] = ((xf * r) * g_ref[...]).astype(h_ref.dtype)

    return pl.pallas_call(
        body, name=name, grid=(t // tr,),
        in_specs=[pl.BlockSpec((tr, d), lambda i: (i, 0)), pl.BlockSpec((1, d), lambda i: (0, 0))],
        out_specs=pl.BlockSpec((tr, d), lambda i: (i, 0)),
        out_shape=jax.ShapeDtypeStruct((t, d), _ACT),
        compiler_params=_params("parallel"),
    )(x, gain.reshape(1, d))


def _rmsnorm_bwd(name, dh, x, gain, dres):
    t, d = x.shape
    tr = _rows(t)

    def body(dh_ref, x_ref, g_ref, dres_ref, dx_ref, dg_ref):
        i = pl.program_id(0)
        xf = x_ref[...]
        r = lax.rsqrt(jnp.mean(xf * xf, axis=-1, keepdims=True) + EPS)
        xhat = xf * r
        dy = dh_ref[...].astype(F32)
        dgx = dy * g_ref[...]
        m = jnp.mean(dgx * xhat, axis=-1, keepdims=True)
        dx_ref[...] = dres_ref[...] + r * (dgx - xhat * m)
        part = jnp.sum(dy * xhat, axis=0, keepdims=True)

        @pl.when(i == 0)
        def _():
            dg_ref[...] = part

        @pl.when(i > 0)
        def _():
            dg_ref[...] += part

    row = pl.BlockSpec((tr, d), lambda i: (i, 0))
    vec = pl.BlockSpec((1, d), lambda i: (0, 0))
    dx, dg = pl.pallas_call(
        body, name=name, grid=(t // tr,),
        in_specs=[row, row, vec, row],
        out_specs=(row, vec),
        out_shape=(jax.ShapeDtypeStruct((t, d), F32), jax.ShapeDtypeStruct((1, d), F32)),
        compiler_params=_params("arbitrary"),
    )(dh, x, gain.reshape(1, d), dres)
    return dx, dg.reshape(d)


def _final_loss(name, x, gain, target):
    t, d = x.shape
    tr = _rows(t)

    def body(x_ref, g_ref, t_ref, dx_ref, dg_ref, loss_ref):
        i = pl.program_id(0)
        xf = x_ref[...]
        r = lax.rsqrt(jnp.mean(xf * xf, axis=-1, keepdims=True) + EPS)
        xhat = xf * r
        g = g_ref[...]
        err = xhat * g - t_ref[...]
        lpart = 0.5 * jnp.sum(jnp.mean(err * err, axis=-1, keepdims=True), axis=0, keepdims=True)
        dy = err * (1.0 / d)
        dgx = dy * g
        m = jnp.mean(dgx * xhat, axis=-1, keepdims=True)
        dx_ref[...] = r * (dgx - xhat * m)
        part = jnp.sum(dy * xhat, axis=0, keepdims=True)
        lrow = jnp.broadcast_to(lpart, (1, LANES))

        @pl.when(i == 0)
        def _():
            dg_ref[...] = part
            loss_ref[...] = lrow

        @pl.when(i > 0)
        def _():
            dg_ref[...] += part
            loss_ref[...] += lrow

    row = pl.BlockSpec((tr, d), lambda i: (i, 0))
    vec = pl.BlockSpec((1, d), lambda i: (0, 0))
    dx, dg, loss = pl.pallas_call(
        body, name=name, grid=(t // tr,),
        in_specs=[row, vec, row],
        out_specs=(row, vec, pl.BlockSpec((1, LANES), lambda i: (0, 0))),
        out_shape=(jax.ShapeDtypeStruct((t, d), F32), jax.ShapeDtypeStruct((1, d), F32),
                   jax.ShapeDtypeStruct((1, LANES), F32)),
        compiler_params=_params("arbitrary"),
    )(x, gain.reshape(1, d), target)
    return loss[0, 0], dx, dg.reshape(d)


def _ffn_fwd(tag, x, gain, w_in4, w_out):
    t, d = x.shape
    fs = w_in4.shape[2]
    f = 2 * fs
    tm = _tile(t, 512)
    h = _rmsnorm_fwd(tag + "_norm", x, gain)
    gu = _mm(
        tag + "_in", (t // tm, N_CHIPS, 1), "nn",
        [(h, pl.BlockSpec((tm, d), lambda i, j, k: (i, 0)))],
        [(w_in4, pl.BlockSpec((None, d, fs), lambda i, j, k: (j, 0, 0)))], [],
        jax.ShapeDtypeStruct((2, t, f), _ACT),
        pl.BlockSpec((None, tm, fs), lambda i, j, k: (j // 2, i, j % 2)), None)
    tm2 = _tile(t, 256)

    def epi(acc, e, o):
        o[0][...] = e[0] + 0.5 * acc

    x_out = _mm(
        tag + "_out", (t // tm2, 1, 1), "nn",
        [(gu, pl.BlockSpec((None, tm2, f), lambda i, j, k: (0, i, 0))),
         (gu, pl.BlockSpec((None, tm2, f), lambda i, j, k: (1, i, 0)))],
        [(w_out, pl.BlockSpec((f, d), lambda i, j, k: (0, 0)))],
        [(x, pl.BlockSpec((tm2, d), lambda i, j, k: (i, 0)))],
        jax.ShapeDtypeStruct((t, d), F32),
        pl.BlockSpec((tm2, d), lambda i, j, k: (i, 0)), None,
        a_fn=_swiglu, epi=epi)
    return x_out, (x, h, gu)


def _ffn_bwd(tag, dxo, saved, gain, w_in4, w_out):
    x, h, gu = saved
    t, d = x.shape
    fs = w_in4.shape[2]
    f = 2 * fs
    tm = _tile(t, 512)
    tk = _tile(t, 512)

    def epi_act(acc, e, o):
        g = e[0].astype(F32)
        u = e[1].astype(F32)
        da = 0.5 * acc
        sig = _sigmoid(g)
        o[0][0] = (da * u * (sig * (1.0 + g * (1.0 - sig)))).astype(o[0].dtype)
        o[0][1] = (da * (g * sig)).astype(o[0].dtype)

    dgu = _mm(
        tag + "_dact", (t // tm, 2, 1), "nt",
        [(dxo, pl.BlockSpec((tm, d), lambda i, j, k: (i, 0)))],
        [(w_out, pl.BlockSpec((fs, d), lambda i, j, k: (j, 0)))],
        [(gu, pl.BlockSpec((None, tm, fs), lambda i, j, k: (0, i, j))),
         (gu, pl.BlockSpec((None, tm, fs), lambda i, j, k: (1, i, j)))],
        jax.ShapeDtypeStruct((2, t, f), _ACT),
        pl.BlockSpec((2, tm, fs), lambda i, j, k: (0, i, j)), None, epi=epi_act)

    def epi_half(acc, e, o):
        o[0][...] = (0.5 * acc).astype(o[0].dtype)

    dw_out = _mm(
        tag + "_dwout", (2, 1, t // tk), "tn",
        [(gu, pl.BlockSpec((None, tk, fs), lambda i, j, k: (0, k, i))),
         (gu, pl.BlockSpec((None, tk, fs), lambda i, j, k: (1, k, i)))],
        [(dxo, pl.BlockSpec((tk, d), lambda i, j, k: (k, 0)))], [],
        jax.ShapeDtypeStruct((f, d), _ACT),
        pl.BlockSpec((fs, d), lambda i, j, k: (i, 0)), (fs, d),
        a_fn=_swiglu, epi=epi_half)
    dh = _mm(
        tag + "_dh", (t // tm, 1, N_CHIPS), "nt",
        [(dgu, pl.BlockSpec((None, tm, fs), lambda i, j, k: (k // 2, i, k % 2)))],
        [(w_in4, pl.BlockSpec((None, d, fs), lambda i, j, k: (k, 0, 0)))], [],
        jax.ShapeDtypeStruct((t, d), F32),
        pl.BlockSpec((tm, d), lambda i, j, k: (i, 0)), (tm, d))
    dw_in4 = _mm(
        tag + "_dwin", (1, N_CHIPS, t // tk), "tn",
        [(h, pl.BlockSpec((tk, d), lambda i, j, k: (k, 0)))],
        [(dgu, pl.BlockSpec((None, tk, fs), lambda i, j, k: (j // 2, k, j % 2)))], [],
        jax.ShapeDtypeStruct((N_CHIPS, d, fs), _ACT),
        pl.BlockSpec((None, d, fs), lambda i, j, k: (j, 0, 0)), (d, fs))
    dx, dgain = _rmsnorm_bwd(tag + "_dnorm", dh, x, gain, dxo)
    return dx, dgain, dw_in4, dw_out


def _proj_in(tag, h, w_in):
    t, d = h.shape
    n = w_in.shape[1]
    tm = _tile(t, 512)
    return _mm(
        tag + "_in", (t // tm, 1, 1), "nn",
        [(h, pl.BlockSpec((tm, d), lambda i, j, k: (i, 0)))],
        [(w_in, pl.BlockSpec((d, n), lambda i, j, k: (0, 0)))], [],
        jax.ShapeDtypeStruct((t, n), _ACT),
        pl.BlockSpec((tm, n), lambda i, j, k: (i, 0)), None)


def _proj_out(tag, x, parts, w_out):
    t, d = x.shape
    tm = _tile(t, 512)

    def epi(acc, e, o):
        o[0][...] = e[0] + acc

    return _mm(
        tag + "_out", (t // tm, 1, 1), "nn",
        [(p, pl.BlockSpec((tm, p.shape[1]), lambda i, j, k: (i, 0))) for p in parts],
        [(w_out, pl.BlockSpec(w_out.shape, lambda i, j, k: (0, 0)))],
        [(x, pl.BlockSpec((tm, d), lambda i, j, k: (i, 0)))],
        jax.ShapeDtypeStruct((t, d), F32),
        pl.BlockSpec((tm, d), lambda i, j, k: (i, 0)), None, epi=epi)


def _proj_out_bwd(tag, dxo, parts, w_out):
    t, d = dxo.shape
    mix = w_out.shape[0]
    tm = _tile(t, 512)
    tk = _tile(t, 512)
    d_mix = _mm(
        tag + "_dmix", (t // tm, 1, 1), "nt",
        [(dxo, pl.BlockSpec((tm, d), lambda i, j, k: (i, 0)))],
        [(w_out, pl.BlockSpec((mix, d), lambda i, j, k: (0, 0)))], [],
        jax.ShapeDtypeStruct((t, mix), F32),
        pl.BlockSpec((tm, mix), lambda i, j, k: (i, 0)), None)
    dw_out = _mm(
        tag + "_dwout", (1, 1, t // tk), "tn",
        [(p, pl.BlockSpec((tk, p.shape[1]), lambda i, j, k: (k, 0))) for p in parts],
        [(dxo, pl.BlockSpec((tk, d), lambda i, j, k: (k, 0)))], [],
        jax.ShapeDtypeStruct((mix, d), _ACT),
        pl.BlockSpec((mix, d), lambda i, j, k: (0, 0)), (mix, d))
    return d_mix, dw_out


def _proj_in_bwd(tag, h, dparts, w_in):
    t, d = h.shape
    n = w_in.shape[1]
    tm = _tile(t, 512)
    tk = _tile(t, 512)
    dh = _mm(
        tag + "_dh", (t // tm, 1, 1), "nt",
        [(p, pl.BlockSpec((tm, p.shape[1]), lambda i, j, k: (i, 0))) for p in dparts],
        [(w_in, pl.BlockSpec((d, n), lambda i, j, k: (0, 0)))], [],
        jax.ShapeDtypeStruct((t, d), F32),
        pl.BlockSpec((tm, d), lambda i, j, k: (i, 0)), None)
    dw_in = _mm(
        tag + "_dwin", (1, 1, t // tk), "tn",
        [(h, pl.BlockSpec((tk, d), lambda i, j, k: (k, 0)))],
        [(p, pl.BlockSpec((tk, p.shape[1]), lambda i, j, k: (k, 0))) for p in dparts], [],
        jax.ShapeDtypeStruct((d, n), _ACT),
        pl.BlockSpec((d, n), lambda i, j, k: (0, 0)), (d, n))
    return dh, dw_in


def _shifted(pad_ref, val, s):
    pad_ref[pl.ds(HALO, s), :] = val
    return pad_ref[pl.ds(HALO - 1, s), :], pad_ref[pl.ds(HALO + 1, s), :]


def _zero_halo(pad_ref, s):
    z = jnp.zeros((HALO, pad_ref.shape[1]), F32)
    pad_ref[pl.ds(0, HALO), :] = z
    pad_ref[pl.ds(HALO + s, HALO), :] = z


def _conv_fwd(tag, proj, conv_w, nb, s):
    t = proj.shape[0]
    ncb = CONV_WIDTH // LANES

    def body(gb_ref, gc_ref, hc_ref, w_ref, a_ref, pad_ref):
        _zero_halo(pad_ref, s)
        cg = gc_ref[...].astype(F32) * hc_ref[...].astype(F32)
        prev, nxt = _shifted(pad_ref, cg, s)
        w = w_ref[...]
        conv = prev * w[0:1, :] + cg * w[1:2, :] + nxt * w[2:3, :]
        a_ref[...] = (gb_ref[...].astype(F32) * conv).astype(a_ref.dtype)

    def col(off):
        return pl.BlockSpec((s, LANES), lambda b, c: (b, off + c))

    return pl.pallas_call(
        body, name=tag + "_conv", grid=(nb, ncb),
        in_specs=[col(0), col(ncb), col(2 * ncb), pl.BlockSpec((3, LANES), lambda b, c: (0, c))],
        out_specs=col(0),
        out_shape=jax.ShapeDtypeStruct((t, CONV_WIDTH), _ACT),
        scratch_shapes=[pltpu.VMEM((s + 2 * HALO, LANES), F32)],
        compiler_params=_params("parallel", "parallel"),
    )(proj, proj, proj, conv_w)


def _conv_bwd(tag, proj, conv_w, d_mix, nb, s):
    t = proj.shape[0]
    ncb = CONV_WIDTH // LANES

    def body(gb_ref, gc_ref, hc_ref, w_ref, da_ref, dgb_ref, dgc_ref, dhc_ref, dw_ref, pad_ref):
        b = pl.program_id(1)
        _zero_halo(pad_ref, s)
        gb = gb_ref[...].astype(F32)
        gc = gc_ref[...].astype(F32)
        hc = hc_ref[...].astype(F32)
        w = w_ref[...]
        da = da_ref[...]
        cg = gc * hc
        prev, nxt = _shifted(pad_ref, cg, s)
        conv = prev * w[0:1, :] + cg * w[1:2, :] + nxt * w[2:3, :]
        dgb_ref[...] = (da * conv).astype(dgb_ref.dtype)
        dconv = da * gb
        dw = jnp.concatenate([
            jnp.sum(dconv * prev, axis=0, keepdims=True),
            jnp.sum(dconv * cg, axis=0, keepdims=True),
            jnp.sum(dconv * nxt, axis=0, keepdims=True)], axis=0)
        dprev, dnxt = _shifted(pad_ref, dconv, s)
        dcg = dnxt * w[0:1, :] + dconv * w[1:2, :] + dprev * w[2:3, :]
        dgc_ref[...] = (dcg * hc).astype(dgc_ref.dtype)
        dhc_ref[...] = (dcg * gc).astype(dhc_ref.dtype)

        @pl.when(b == 0)
        def _():
            dw_ref[...] = dw

        @pl.when(b > 0)
        def _():
            dw_ref[...] += dw

    def col(off):
        return pl.BlockSpec((s, LANES), lambda c, b: (b, off + c))

    wspec = pl.BlockSpec((3, LANES), lambda c, b: (0, c))
    act = jax.ShapeDtypeStruct((t, CONV_WIDTH), _ACT)
    return pl.pallas_call(
        body, name=tag + "_dconv", grid=(ncb, nb),
        in_specs=[col(0), col(ncb), col(2 * ncb), wspec, col(0)],
        out_specs=(col(0), col(0), col(0), wspec),
        out_shape=(act, act, act, jax.ShapeDtypeStruct((3, CONV_WIDTH), F32)),
        scratch_shapes=[pltpu.VMEM((s + 2 * HALO, LANES), F32)],
        compiler_params=_params("parallel", "arbitrary"),
    )(proj, proj, proj, conv_w, d_mix)


def _rope_tables(s):
    rows = s // GRID_W
    r_idx, c_idx = jnp.meshgrid(jnp.arange(rows), jnp.arange(GRID_W), indexing="ij")
    r_idx = r_idx.reshape(-1).astype(F32)
    c_idx = c_idx.reshape(-1).astype(F32)
    n_freq = HEAD_DIM // 4
    inv = ROPE_THETA ** (-jnp.arange(n_freq, dtype=F32) / n_freq)
    ang = jnp.concatenate([r_idx[:, None] * inv, c_idx[:, None] * inv], axis=-1)
    cos = jnp.repeat(jnp.cos(ang), 2, axis=1)
    sin = jnp.repeat(jnp.sin(ang), 2, axis=1)
    sign = jnp.where(jnp.arange(HEAD_DIM) % 2 == 0, -1.0, 1.0).astype(F32)
    return jnp.tile(cos, (1, LANES // HEAD_DIM)), jnp.tile(sin * sign, (1, LANES // HEAD_DIM))


def _head_ones():
    i = jnp.arange(LANES) // HEAD_DIM
    return (i[:, None] == i[None, :]).astype(jnp.bfloat16)


def _head_sum(v, ones):
    outs = []
    for j in range(v.shape[1] // LANES):
        c = v[:, j * LANES:(j + 1) * LANES]
        hi = c.astype(jnp.bfloat16)
        lo = (c - hi.astype(F32)).astype(jnp.bfloat16)
        outs.append(jnp.dot(hi, ones, preferred_element_type=F32) + jnp.dot(lo, ones, preferred_element_type=F32))
    return outs[0] if len(outs) == 1 else jnp.concatenate(outs, axis=1)


def _pair_swap(v):
    outs = []
    for j in range(v.shape[1] // LANES):
        c = v[:, j * LANES:(j + 1) * LANES]
        lane = lax.broadcasted_iota(jnp.int32, c.shape, 1)
        outs.append(jnp.where(lane % 2 == 0, pltpu.roll(c, LANES - 1, 1), pltpu.roll(c, 1, 1)))
    return outs[0] if len(outs) == 1 else jnp.concatenate(outs, axis=1)


def _wide(tab, width):
    return tab if width == LANES else jnp.concatenate([tab] * (width // LANES), axis=1)


def _qk_fwd(tag, proj, q_gain, k_gain, cos, sin, nb, s):
    t = proj.shape[0]
    tr = _tile(s, 512)
    ns = s // tr
    q_off = 3 * CONV_WIDTH // ATTN_WIDTH
    k_off = (3 * CONV_WIDTH + ATTN_WIDTH) // KV_WIDTH

    def body(q_ref, k_ref, qg_ref, kg_ref, cos_ref, sin_ref, ones_ref, qo_ref, ko_ref):
        ones = ones_ref[...]
        for src, g_ref, dst in ((q_ref, qg_ref, qo_ref), (k_ref, kg_ref, ko_ref)):
            v = src[...].astype(F32)
            w = v.shape[1]
            r = lax.rsqrt(_head_sum(v * v, ones) * (1.0 / HEAD_DIM) + EPS)
            vn = (v * r) * g_ref[...]
            dst[...] = (vn * _wide(cos_ref[...], w) + _pair_swap(vn) * _wide(sin_ref[...], w)).astype(dst.dtype)

    tab = pl.BlockSpec((tr, LANES), lambda i: (i % ns, 0))
    return pl.pallas_call(
        body, name=tag + "_qk", grid=(t // tr,),
        in_specs=[pl.BlockSpec((tr, ATTN_WIDTH), lambda i: (i, q_off)),
                  pl.BlockSpec((tr, KV_WIDTH), lambda i: (i, k_off)),
                  pl.BlockSpec((1, ATTN_WIDTH), lambda i: (0, 0)),
                  pl.BlockSpec((1, KV_WIDTH), lambda i: (0, 0)),
                  tab, tab, pl.BlockSpec((LANES, LANES), lambda i: (0, 0))],
        out_specs=(pl.BlockSpec((tr, ATTN_WIDTH), lambda i: (i, 0)),
                   pl.BlockSpec((tr, KV_WIDTH), lambda i: (i, 0))),
        out_shape=(jax.ShapeDtypeStruct((t, ATTN_WIDTH), _ACT), jax.ShapeDtypeStruct((t, KV_WIDTH), _ACT)),
        compiler_params=_params("parallel"),
    )(proj, proj, jnp.tile(q_gain, N_Q_HEADS).reshape(1, ATTN_WIDTH),
      jnp.tile(k_gain, N_KV_HEADS).reshape(1, KV_WIDTH), cos, sin, _head_ones())


def _qk_bwd(tag, proj, q_gain, k_gain, cos, sin, dq_rot, dk_rot, nb, s):
    t = proj.shape[0]
    tr = _tile(s, 512)
    ns = s // tr
    q_off = 3 * CONV_WIDTH // ATTN_WIDTH
    k_off = (3 * CONV_WIDTH + ATTN_WIDTH) // KV_WIDTH

    def body(q_ref, k_ref, qg_ref, kg_ref, cos_ref, sin_ref, ones_ref, dqr_ref, dkr_ref,
             dq_ref, dk_ref, dqg_ref, dkg_ref):
        i = pl.program_id(0)
        ones = ones_ref[...]
        for src, g_ref, dr_ref, dst, dg_ref in ((q_ref, qg_ref, dqr_ref, dq_ref, dqg_ref),
                                                (k_ref, kg_ref, dkr_ref, dk_ref, dkg_ref)):
            v = src[...].astype(F32)
            w = v.shape[1]
            r = lax.rsqrt(_head_sum(v * v, ones) * (1.0 / HEAD_DIM) + EPS)
            xhat = v * r
            dr = dr_ref[...]
            dvn = dr * _wide(cos_ref[...], w) + _pair_swap(dr * _wide(sin_ref[...], w))
            dgx = dvn * g_ref[...]
            m = _head_sum(dgx * xhat, ones) * (1.0 / HEAD_DIM)
            dst[...] = (r * (dgx - xhat * m)).astype(dst.dtype)
            part = jnp.sum(dvn * xhat, axis=0, keepdims=True)
            fold = part[:, 0:HEAD_DIM]
            for hh in range(1, w // HEAD_DIM):
                fold = fold + part[:, hh * HEAD_DIM:(hh + 1) * HEAD_DIM]

            @pl.when(i == 0)
            def _():
                dg_ref[...] = fold

            @pl.when(i > 0)
            def _():
                dg_ref[...] += fold

    tab = pl.BlockSpec((tr, LANES), lambda i: (i % ns, 0))
    qrow = pl.BlockSpec((tr, ATTN_WIDTH), lambda i: (i, 0))
    krow = pl.BlockSpec((tr, KV_WIDTH), lambda i: (i, 0))
    gvec = pl.BlockSpec((1, HEAD_DIM), lambda i: (0, 0))
    dq, dk, dqg, dkg = pl.pallas_call(
        body, name=tag + "_dqk", grid=(t // tr,),
        in_specs=[pl.BlockSpec((tr, ATTN_WIDTH), lambda i: (i, q_off)),
                  pl.BlockSpec((tr, KV_WIDTH), lambda i: (i, k_off)),
                  pl.BlockSpec((1, ATTN_WIDTH), lambda i: (0, 0)),
                  pl.BlockSpec((1, KV_WIDTH), lambda i: (0, 0)),
                  tab, tab, pl.BlockSpec((LANES, LANES), lambda i: (0, 0)), qrow, krow],
        out_specs=(qrow, krow, gvec, gvec),
        out_shape=(jax.ShapeDtypeStruct((t, ATTN_WIDTH), _ACT), jax.ShapeDtypeStruct((t, KV_WIDTH), _ACT),
                   jax.ShapeDtypeStruct((1, HEAD_DIM), F32), jax.ShapeDtypeStruct((1, HEAD_DIM), F32)),
        compiler_params=_params("arbitrary"),
    )(proj, proj, jnp.tile(q_gain, N_Q_HEADS).reshape(1, ATTN_WIDTH),
      jnp.tile(k_gain, N_KV_HEADS).reshape(1, KV_WIDTH), cos, sin, _head_ones(), dq_rot, dk_rot)
    return dq, dk, dqg.reshape(HEAD_DIM), dkg.reshape(HEAD_DIM)


def _head(v, h):
    return v[:, h * HEAD_DIM:(h + 1) * HEAD_DIM]


def _attn_fwd(tag, q, k, proj, nb, s):
    t = q.shape[0]
    tq = _tile(s, 256)
    nq = s // tq
    v_off = (3 * CONV_WIDTH + ATTN_WIDTH + KV_WIDTH) // KV_WIDTH
    scale = HEAD_DIM ** -0.5

    def body(q_ref, k_ref, v_ref, o_ref, lse_ref):
        qv = q_ref[...]
        kv = k_ref[...]
        vv = v_ref[...]
        for h in range(N_Q_HEADS):
            j = h // Q_PER_KV
            sc = _dot(_head(qv, h), _head(kv, j), "nt") * scale
            m = jnp.max(sc, axis=-1, keepdims=True)
            e = jnp.exp(sc - m)
            l = jnp.sum(e, axis=-1, keepdims=True)
            o = _dot(e, _head(vv, j)) * (1.0 / l)
            o_ref[:, h * HEAD_DIM:(h + 1) * HEAD_DIM] = o.astype(o_ref.dtype)
            lse_ref[:, h:h + 1] = m + jnp.log(l)

    return pl.pallas_call(
        body, name=tag + "_attn", grid=(nb, nq),
        in_specs=[pl.BlockSpec((tq, ATTN_WIDTH), lambda b, i: (b * nq + i, 0)),
                  pl.BlockSpec((s, KV_WIDTH), lambda b, i: (b, 0)),
                  pl.BlockSpec((s, KV_WIDTH), lambda b, i: (b, v_off))],
        out_specs=(pl.BlockSpec((tq, ATTN_WIDTH), lambda b, i: (b * nq + i, 0)),
                   pl.BlockSpec((tq, N_Q_HEADS), lambda b, i: (b * nq + i, 0))),
        out_shape=(jax.ShapeDtypeStruct((t, ATTN_WIDTH), _ACT), jax.ShapeDtypeStruct((t, N_Q_HEADS), F32)),
        compiler_params=_params("parallel", "parallel"),
    )(q, k, proj)


def _attn_bwd(tag, q, k, proj, o, lse, d_mix, nb, s):
    t = q.shape[0]
    tq = _tile(s, 256)
    nq = s // tq
    v_off = (3 * CONV_WIDTH + ATTN_WIDTH + KV_WIDTH) // KV_WIDTH
    scale = HEAD_DIM ** -0.5

    def body(q_ref, k_ref, v_ref, o_ref, lse_ref, do_ref, dq_ref, dk_ref, dv_ref):
        i = pl.program_id(1)

        @pl.when(i == 0)
        def _():
            dk_ref[...] = jnp.zeros_like(dk_ref)
            dv_ref[...] = jnp.zeros_like(dv_ref)

        qv = q_ref[...]
        kv = k_ref[...]
        vv = v_ref[...]
        ov = o_ref[...].astype(F32)
        dov = do_ref[...]
        lse = lse_ref[...]
        for h in range(N_Q_HEADS):
            j = h // Q_PER_KV
            cols = slice(j * HEAD_DIM, (j + 1) * HEAD_DIM)
            qh = _head(qv, h)
            kj = _head(kv, j)
            doh = _head(dov, h)
            sc = _dot(qh, kj, "nt") * scale
            p = jnp.exp(sc - lse[:, h:h + 1])
            dp = _dot(doh, _head(vv, j), "nt")
            delta = jnp.sum(doh * _head(ov, h), axis=-1, keepdims=True)
            ds = p * (dp - delta) * scale
            dv_ref[:, cols] += _dot(p, doh, "tn")
            dk_ref[:, cols] += _dot(ds, qh, "tn")
            dq_ref[:, h * HEAD_DIM:(h + 1) * HEAD_DIM] = _dot(ds, kj)

    qrow = pl.BlockSpec((tq, ATTN_WIDTH), lambda b, i: (b * nq + i, 0))
    kvrow = pl.BlockSpec((s, KV_WIDTH), lambda b, i: (b, 0))
    return pl.pallas_call(
        body, name=tag + "_dattn", grid=(nb, nq),
        in_specs=[qrow, kvrow, pl.BlockSpec((s, KV_WIDTH), lambda b, i: (b, v_off)), qrow,
                  pl.BlockSpec((tq, N_Q_HEADS), lambda b, i: (b * nq + i, 0)),
                  pl.BlockSpec((tq, ATTN_WIDTH), lambda b, i: (b * nq + i, 1))],
        out_specs=(qrow, kvrow, kvrow),
        out_shape=(jax.ShapeDtypeStruct((t, ATTN_WIDTH), F32), jax.ShapeDtypeStruct((t, KV_WIDTH), F32),
                   jax.ShapeDtypeStruct((t, KV_WIDTH), F32)),
        compiler_params=_params("parallel", "arbitrary"),
    )(q, k, proj, o, lse, d_mix)


def _even_fwd(tag, x, p, cos, sin, nb, s):
    h = _rmsnorm_fwd(tag + "_norm", x, p["norm"])
    proj = _proj_in(tag, h, p["w_in"])
    a = _conv_fwd(tag, proj, p["conv_w"], nb, s)
    q, k = _qk_fwd(tag, proj, p["q_gain"], p["k_gain"], cos, sin, nb, s)
    o, lse = _attn_fwd(tag, q, k, proj, nb, s)
    x_out = _proj_out(tag, x, [a, o], p["w_out"])
    return x_out, (x, h, proj, a, q, k, o, lse)


def _even_bwd(tag, dxo, saved, p, cos, sin, nb, s):
    x, h, proj, a, q, k, o, lse = saved
    d_mix, dw_out = _proj_out_bwd(tag, dxo, [a, o], p["w_out"])
    dgb, dgc, dhc, dconv_w = _conv_bwd(tag, proj, p["conv_w"], d_mix, nb, s)
    dq_rot, dk_rot, dv = _attn_bwd(tag, q, k, proj, o, lse, d_mix, nb, s)
    dq, dk, dq_gain, dk_gain = _qk_bwd(tag, proj, p["q_gain"], p["k_gain"], cos, sin, dq_rot, dk_rot, nb, s)
    dh, dw_in = _proj_in_bwd(tag, h, [dgb, dgc, dhc, dq, dk, dv], p["w_in"])
    dx, dnorm = _rmsnorm_bwd(tag + "_dnorm", dh, x, p["norm"], dxo)
    grads = dict(norm=dnorm, w_in=dw_in, w_out=dw_out, conv_w=dconv_w, q_gain=dq_gain, k_gain=dk_gain)
    return dx, grads


def _window(pad_ref, val, r, s):
    pad_ref[pl.ds(HALO, s), :] = val
    acc = val
    for d in range(1, r + 1):
        acc = acc + pad_ref[pl.ds(HALO - d, s), :] + pad_ref[pl.ds(HALO + d, s), :]
    return acc


def _count(r, s):
    t = lax.broadcasted_iota(jnp.int32, (s, 1), 0)
    return (jnp.minimum(t + r, s - 1) - jnp.maximum(t - r, 0) + 1).astype(F32)


def _sgu_chunk(u_ref, v_ref, norm, ws_ref, bt, rows):
    uu = u_ref[rows, :].astype(F32)
    vv = v_ref[rows, :].astype(F32)
    gu = _gelu(uu)
    gv = _gelu(vv)
    r = lax.rsqrt(jnp.mean(gv * gv, axis=-1, keepdims=True) + EPS)
    xhat = gv * r
    vn = xhat * norm
    mixed = []
    for g in range(N_GROUPS):
        cols = slice(g * SGU_GROUP, (g + 1) * SGU_GROUP)
        mixed.append(_dot(ws_ref[g], vn[:, cols]) + bt[:, g:g + 1])
    return uu, vv, gu, r, xhat, vn, mixed


def _odd_core_fwd(tag, proj, p, nb, s):
    t = proj.shape[0]
    nchunk = s // SGU_CHUNK

    def body(p_ref, u_ref, v_ref, pw_ref, ps_ref, sn_ref, ws_ref, bt_ref, mix_ref, pad_ref):
        _zero_halo(pad_ref, s)
        for g, r in enumerate(POOL_RADII):
            cols = slice(g * POOL_GROUP, (g + 1) * POOL_GROUP)
            pg = p_ref[:, cols].astype(F32)
            pooled = _window(pad_ref, pg, r, s) / _count(r, s) - pg
            mix_ref[:, cols] = (_dot(pooled, pw_ref[g]) * ps_ref[:, cols]).astype(mix_ref.dtype)
        norm = sn_ref[...]
        bt = bt_ref[...]

        def chunk(n, carry):
            rows = pl.ds(pl.multiple_of(n * SGU_CHUNK, SGU_CHUNK), SGU_CHUNK)
            _, _, gu, _, _, _, mixed = _sgu_chunk(u_ref, v_ref, norm, ws_ref, bt, rows)
            for g in range(N_GROUPS):
                cols = slice(g * SGU_GROUP, (g + 1) * SGU_GROUP)
                mix_ref[rows, HALF + g * SGU_GROUP:HALF + (g + 1) * SGU_GROUP] = (
                    gu[:, cols] * mixed[g]).astype(mix_ref.dtype)
            return carry

        lax.fori_loop(0, nchunk, chunk, 0)

    def col(j):
        return pl.BlockSpec((s, HALF), lambda b: (b, j))

    def whole(a):
        return pl.BlockSpec(a.shape, lambda b: (0,) * a.ndim)

    consts = [p["pool_w"], p["pool_scale"].reshape(1, HALF), p["sgu_norm"].reshape(1, HALF),
              p["sgu_w"], p["sgu_b"].T]
    return pl.pallas_call(
        body, name=tag + "_core", grid=(nb,),
        in_specs=[col(0), col(1), col(2)] + [whole(a) for a in consts],
        out_specs=pl.BlockSpec((s, D_MODEL), lambda b: (b, 0)),
        out_shape=jax.ShapeDtypeStruct((t, D_MODEL), _ACT),
        scratch_shapes=[pltpu.VMEM((s + 2 * HALO, POOL_GROUP), F32)],
        compiler_params=_params("parallel"),
    )(proj, proj, proj, *consts)


def _odd_core_bwd(tag, proj, p, d_mix, nb, s):
    t = proj.shape[0]
    nchunk = s // SGU_CHUNK

    def body(p_ref, u_ref, v_ref, pw_ref, ps_ref, sn_ref, ws_ref, bt_ref, dm_ref,
             dproj_ref, dpw_ref, dps_ref, dsn_ref, dws_ref, dbt_ref, pad_ref):
        b = pl.program_id(0)

        @pl.when(b == 0)
        def _():
            dpw_ref[...] = jnp.zeros_like(dpw_ref)
            dps_ref[...] = jnp.zeros_like(dps_ref)
            dsn_ref[...] = jnp.zeros_like(dsn_ref)
            dws_ref[...] = jnp.zeros_like(dws_ref)
            dbt_ref[...] = jnp.zeros_like(dbt_ref)

        _zero_halo(pad_ref, s)
        for g, r in enumerate(POOL_RADII):
            cols = slice(g * POOL_GROUP, (g + 1) * POOL_GROUP)
            pg = p_ref[:, cols].astype(F32)
            cnt = _count(r, s)
            pooled = _window(pad_ref, pg, r, s) / cnt - pg
            c_pre = _dot(pooled, pw_ref[g])
            dc = dm_ref[:, cols]
            dps_ref[:, cols] += jnp.sum(dc * c_pre, axis=0, keepdims=True)
            dcp = dc * ps_ref[:, cols]
            dpw_ref[g] += _dot(pooled, dcp, "tn")
            dpooled = _dot(dcp, pw_ref[g], "nt")
            dproj_ref[:, cols] = (_window(pad_ref, dpooled / cnt, r, s) - dpooled).astype(dproj_ref.dtype)
        norm = sn_ref[...]
        bt = bt_ref[...]

        def chunk(n, carry):
            rows = pl.ds(pl.multiple_of(n * SGU_CHUNK, SGU_CHUNK), SGU_CHUNK)
            uu, vv, gu, r, xhat, vn, mixed = _sgu_chunk(u_ref, v_ref, norm, ws_ref, bt, rows)
            dd = dm_ref[rows, HALF:D_MODEL]
            dgu, dvn = [], []
            for g in range(N_GROUPS):
                cols = slice(g * SGU_GROUP, (g + 1) * SGU_GROUP)
                dgu.append(dd[:, cols] * mixed[g])
                dmx = dd[:, cols] * gu[:, cols]
                dbt_ref[:, g:g + 1] += jnp.sum(dmx, axis=-1, keepdims=True)
                dws_ref[g] += _dot(dmx, vn[:, cols], "nt")
                dvn.append(_dot(ws_ref[g], dmx, "tn"))
            dgu = jnp.concatenate(dgu, axis=1)
            dvn = jnp.concatenate(dvn, axis=1)
            dsn_ref[...] += jnp.sum(dvn * xhat, axis=0, keepdims=True)
            dgx = dvn * norm
            m = jnp.mean(dgx * xhat, axis=-1, keepdims=True)
            dgv = r * (dgx - xhat * m)
            dproj_ref[rows, HALF:2 * HALF] = (dgu * _gelu_grad(uu)).astype(dproj_ref.dtype)
            dproj_ref[rows, 2 * HALF:3 * HALF] = (dgv * _gelu_grad(vv)).astype(dproj_ref.dtype)
            return carry

        lax.fori_loop(0, nchunk, chunk, 0)

    def col(j):
        return pl.BlockSpec((s, HALF), lambda b: (b, j))

    def whole(a):
        return pl.BlockSpec(a.shape, lambda b: (0,) * a.ndim)

    consts = [p["pool_w"], p["pool_scale"].reshape(1, HALF), p["sgu_norm"].reshape(1, HALF),
              p["sgu_w"], p["sgu_b"].T]
    gshapes = [jax.ShapeDtypeStruct(a.shape, F32) for a in consts]
    dproj, dpw, dps, dsn, dws, dbt = pl.pallas_call(
        body, name=tag + "_dcore", grid=(nb,),
        in_specs=[col(0), col(1), col(2)] + [whole(a) for a in consts]
        + [pl.BlockSpec((s, D_MODEL), lambda b: (b, 0))],
        out_specs=[pl.BlockSpec((s, 3 * HALF), lambda b: (b, 0))] + [whole(a) for a in consts],
        out_shape=[jax.ShapeDtypeStruct((t, 3 * HALF), _ACT)] + gshapes,
        scratch_shapes=[pltpu.VMEM((s + 2 * HALO, POOL_GROUP), F32)],
        compiler_params=_params("arbitrary"),
    )(proj, proj, proj, *consts, d_mix)
    return dproj, dict(pool_w=dpw, pool_scale=dps.reshape(HALF), sgu_norm=dsn.reshape(HALF), sgu_w=dws, sgu_b=dbt.T)


def _odd_fwd(tag, x, p, nb, s):
    h = _rmsnorm_fwd(tag + "_norm", x, p["norm"])
    proj = _proj_in(tag, h, p["w_in"])
    mix = _odd_core_fwd(tag, proj, p, nb, s)
    x_out = _proj_out(tag, x, [mix], p["w_out"])
    return x_out, (x, h, proj, mix)


def _odd_bwd(tag, dxo, saved, p, nb, s):
    x, h, proj, mix = saved
    d_mix, dw_out = _proj_out_bwd(tag, dxo, [mix], p["w_out"])
    dproj, grads = _odd_core_bwd(tag, proj, p, d_mix, nb, s)
    dh, dw_in = _proj_in_bwd(tag, h, [dproj], p["w_in"])
    dx, dnorm = _rmsnorm_bwd(tag + "_dnorm", dh, x, p["norm"], dxo)
    grads.update(norm=dnorm, w_in=dw_in, w_out=dw_out)
    return dx, grads


def _local_step(x3, target3, w):
    nb, s, d = x3.shape
    t = nb * s
    x = x3.reshape(t, d)
    target = target3.reshape(t, d)
    depth = len(w["mix"])
    cos, sin = _rope_tables(s)
    saved = []
    for l in range(depth):
        x, s1 = _ffn_fwd(f"l{l}_ffn1", x, w["ffn1"][l]["norm"], w["ffn1"][l]["w_in4"], w["ffn1"][l]["w_out"])
        if l % 2 == 0:
            x, s2 = _even_fwd(f"l{l}_ev", x, w["mix"][l], cos, sin, nb, s)
        else:
            x, s2 = _odd_fwd(f"l{l}_od", x, w["mix"][l], nb, s)
        x, s3 = _ffn_fwd(f"l{l}_ffn2", x, w["ffn2"][l]["norm"], w["ffn2"][l]["w_in4"], w["ffn2"][l]["w_out"])
        saved.append((s1, s2, s3))
    loss, dx, dfinal = _final_loss("final_loss", x, w["final_norm"], target)
    g = dict(ffn1=[None] * depth, ffn2=[None] * depth, mix=[None] * depth, final_norm=dfinal)
    for l in reversed(range(depth)):
        s1, s2, s3 = saved[l]
        dx, dn, dwi, dwo = _ffn_bwd(f"l{l}_ffn2", dx, s3, w["ffn2"][l]["norm"], w["ffn2"][l]["w_in4"],
                                    w["ffn2"][l]["w_out"])
        g["ffn2"][l] = dict(norm=dn, w_in4=dwi, w_out=dwo)
        if l % 2 == 0:
            dx, g["mix"][l] = _even_bwd(f"l{l}_ev", dx, s2, w["mix"][l], cos, sin, nb, s)
        else:
            dx, g["mix"][l] = _odd_bwd(f"l{l}_od", dx, s2, w["mix"][l], nb, s)
        dx, dn, dwi, dwo = _ffn_bwd(f"l{l}_ffn1", dx, s1, w["ffn1"][l]["norm"], w["ffn1"][l]["w_in4"],
                                    w["ffn1"][l]["w_out"])
        g["ffn1"][l] = dict(norm=dn, w_in4=dwi, w_out=dwo)
    return loss, dx.reshape(nb, s, d), g


_HBM = pl.BlockSpec(memory_space=pltpu.HBM)


def _place():
    x, y, c = lax.axis_index("x"), lax.axis_index("y"), lax.axis_index("c")
    chips = [(1 - x, y), (x, 1 - y), (1 - x, 1 - y)]
    return x, y, c, chips


def _remote(src, dst, send_sem, recv_sem, to):
    return pltpu.make_async_remote_copy(src_ref=src, dst_ref=dst, send_sem=send_sem, recv_sem=recv_sem,
                                        device_id=to, device_id_type=_MESH)


def _gather_shards(arrs, small):
    n = len(arrs)

    def body(*refs):
        ins, sm_in = refs[:n], refs[n]
        outs, sm_out = refs[n + 1:2 * n + 1], refs[2 * n + 1]
        send, recv, loc = refs[2 * n + 2:]
        x, y, c, chips = _place()
        k = 2 * x + y
        sib = (x, y, 1 - c)
        started, local = [], []
        for a in range(n + 1):
            src, dst = (ins[a], outs[a]) if a < n else (sm_in, sm_out)
            cp = pltpu.make_async_copy(src, dst.at[k], loc.at[a])
            cp.start()
            local.append(cp)
            if a < n:
                h = src.shape[0] // 2
                mine = pl.ds(c * h, h)
                src_part, dst_part = src.at[mine], dst.at[k, mine]
            else:
                src_part, dst_part = src, dst.at[k]
            for j, chip in enumerate(chips):
                cp = _remote(src_part, dst_part, send.at[a, j], recv.at[a, j], (*chip, c))
                cp.start()
                started.append(cp)
        for a in range(n):
            h = ins[a].shape[0] // 2
            mine = pl.ds(c * h, h)
            for j, (px, py) in enumerate(chips):
                landed = outs[a].at[2 * px + py, mine]
                _remote(landed, landed, send.at[a, j], recv.at[a, j], (px, py, c)).wait_recv()
                cp = _remote(landed, landed, send.at[a, 3 + j], recv.at[a, 3 + j], sib)
                cp.start()
                started.append(cp)
        for a in range(n):
            h = ins[a].shape[0] // 2
            other = pl.ds((1 - c) * h, h)
            for j, (px, py) in enumerate(chips):
                passed = outs[a].at[2 * px + py, other]
                _remote(passed, passed, send.at[a, 3 + j], recv.at[a, 3 + j], sib).wait_recv()
        for j, (px, py) in enumerate(chips):
            landed = sm_out.at[2 * px + py]
            _remote(landed, landed, send.at[n, j], recv.at[n, j], (px, py, c)).wait_recv()
        for cp in started:
            cp.wait_send()
        for cp in local:
            cp.wait()

    outs = pl.pallas_call(
        body, name="gather_shards",
        in_specs=[_HBM] * (n + 1), out_specs=[_HBM] * (n + 1),
        out_shape=[jax.ShapeDtypeStruct((N_CHIPS,) + a.shape, a.dtype) for a in list(arrs) + [small]],
        scratch_shapes=[pltpu.SemaphoreType.DMA((n + 1, 6)), pltpu.SemaphoreType.DMA((n + 1, 6)),
                        pltpu.SemaphoreType.DMA((n + 1,))],
    )(*arrs, small)
    return outs[:n], outs[n]


def _swap_halves(grads):
    n = len(grads)

    def body(*refs):
        ins, outs = refs[:n], refs[n:2 * n]
        send, recv = refs[2 * n:]
        x, y, c, _ = _place()
        sib = (x, y, 1 - c)
        cps = []
        for a in range(n):
            h = ins[a].shape[1] // 2
            cp = _remote(ins[a].at[:, pl.ds((1 - c) * h, h)], outs[a], send.at[a], recv.at[a], sib)
            cp.start()
            cps.append(cp)
        for cp in cps:
            cp.wait()

    return pl.pallas_call(
        body, name="swap_halves",
        in_specs=[_HBM] * n, out_specs=[_HBM] * n,
        out_shape=[jax.ShapeDtypeStruct((g.shape[0], g.shape[1] // 2) + g.shape[2:], g.dtype) for g in grads],
        scratch_shapes=[pltpu.SemaphoreType.DMA((n,)), pltpu.SemaphoreType.DMA((n,))],
    )(*grads)


def _scatter_chips(parts):
    n = len(parts)

    def body(*refs):
        ins, outs = refs[:n], refs[n:2 * n]
        send, recv, loc = refs[2 * n:]
        x, y, c, chips = _place()
        k = 2 * x + y
        cps = []
        for a in range(n):
            cp = pltpu.make_async_copy(ins[a].at[k], outs[a].at[k], loc.at[a])
            cp.start()
            cps.append(cp)
            for j, (px, py) in enumerate(chips):
                cp = _remote(ins[a].at[2 * px + py], outs[a].at[k], send.at[a, j], recv.at[a, j], (px, py, c))
                cp.start()
                cps.append(cp)
        for a in range(n):
            for j, (px, py) in enumerate(chips):
                landed = outs[a].at[2 * px + py]
                _remote(landed, landed, send.at[a, j], recv.at[a, j], (px, py, c)).wait_recv()
        for a in range(n):
            cps[4 * a].wait()
            for j in range(3):
                cps[4 * a + 1 + j].wait_send()

    return pl.pallas_call(
        body, name="scatter_chips",
        in_specs=[_HBM] * n, out_specs=[_HBM] * n,
        out_shape=[jax.ShapeDtypeStruct(p.shape, p.dtype) for p in parts],
        scratch_shapes=[pltpu.SemaphoreType.DMA((n, 3)), pltpu.SemaphoreType.DMA((n, 3)),
                        pltpu.SemaphoreType.DMA((n,))],
    )(*parts)


def _join_halves(halves):
    n = len(halves)

    def body(*refs):
        ins, outs = refs[:n], refs[n:2 * n]
        send, recv, loc = refs[2 * n:]
        x, y, c, _ = _place()
        sib = (x, y, 1 - c)
        cps = []
        for a in range(n):
            h = ins[a].shape[0]
            mine = outs[a].at[pl.ds(c * h, h)]
            cp = pltpu.make_async_copy(ins[a], mine, loc.at[a])
            cp.start()
            cps.append(cp)
            cp = _remote(ins[a], mine, send.at[a], recv.at[a], sib)
            cp.start()
            cps.append(cp)
        for a in range(n):
            h = ins[a].shape[0]
            theirs = outs[a].at[pl.ds((1 - c) * h, h)]
            _remote(theirs, theirs, send.at[a], recv.at[a], sib).wait_recv()
        for a in range(n):
            cps[2 * a].wait()
            cps[2 * a + 1].wait_send()

    return pl.pallas_call(
        body, name="join_halves",
        in_specs=[_HBM] * n, out_specs=[_HBM] * n,
        out_shape=[jax.ShapeDtypeStruct((2 * p.shape[0],) + p.shape[1:], p.dtype) for p in halves],
        scratch_shapes=[pltpu.SemaphoreType.DMA((n,)), pltpu.SemaphoreType.DMA((n,)),
                        pltpu.SemaphoreType.DMA((n,))],
    )(*halves)


def _allreduce_small(buf):
    rows = buf.shape[0]

    def body(in_ref, out_ref, land_ref, send, recv):
        x, y, c, _ = _place()
        me = 4 * x + 2 * y + c
        land_ref[me] = in_ref[...]
        peers = []
        for r in range(1, N_DEV):
            peers.append((1 - x if r & 4 else x, 1 - y if r & 2 else y, 1 - c if r & 1 else c))
        cps = []
        for r, peer in enumerate(peers):
            cp = _remote(in_ref, land_ref.at[me], send.at[r], recv.at[r], peer)
            cp.start()
            cps.append(cp)
        for r, (px, py, pc) in enumerate(peers):
            landed = land_ref.at[4 * px + 2 * py + pc]
            _remote(landed, landed, send.at[r], recv.at[r], (px, py, pc)).wait_recv()
        for cp in cps:
            cp.wait_send()
        acc = land_ref[0]
        for d in range(1, N_DEV):
            acc = acc + land_ref[d]
        out_ref[...] = acc

    return pl.pallas_call(
        body, name="allreduce_small",
        in_specs=[_VMEM], out_specs=_VMEM,
        out_shape=jax.ShapeDtypeStruct(buf.shape, F32),
        scratch_shapes=[pltpu.VMEM((N_DEV, rows, LANES), F32), pltpu.SemaphoreType.DMA((N_DEV - 1,)),
                        pltpu.SemaphoreType.DMA((N_DEV - 1,))],
        compiler_params=pltpu.CompilerParams(vmem_limit_bytes=_VMEM_LIMIT),
    )(buf)


def _div_tile(n, cap, mult):
    best = None
    for d in range(mult, min(n, cap) + 1, mult):
        if n % d == 0:
            best = d
    return best if best is not None else n


def _add_sibling(name, grad, got, c):
    nk, h, r, cc = got.shape
    tr = _div_tile(r, 512, 16)

    def body(c_ref, g_ref, o_ref, s_ref):
        s_ref[...] = (g_ref[...].astype(F32) + o_ref[...].astype(F32)).astype(s_ref.dtype)

    blk = (None, None, tr, cc)
    return pl.pallas_call(
        body, name=name,
        grid_spec=pltpu.PrefetchScalarGridSpec(
            num_scalar_prefetch=1, grid=(nk, h, r // tr),
            in_specs=[pl.BlockSpec(blk, lambda i, l, q, c_ref: (i, c_ref[0] * h + l, q, 0)),
                      pl.BlockSpec(blk, lambda i, l, q, c_ref: (i, l, q, 0))],
            out_specs=pl.BlockSpec(blk, lambda i, l, q, c_ref: (i, l, q, 0))),
        out_shape=jax.ShapeDtypeStruct(got.shape, got.dtype),
        compiler_params=_params("parallel", "parallel", "parallel"),
    )(c, grad, got)


def _add_chips(name, got):
    nk, h, r, cc = got.shape
    tr = _div_tile(r, 512, 16)

    def body(*refs):
        acc = refs[0][...].astype(F32)
        for q in range(1, nk):
            acc = acc + refs[q][...].astype(F32)
        refs[nk][...] = acc

    def part(q):
        return pl.BlockSpec((None, None, tr, cc), lambda l, i: (q, l, i, 0))

    return pl.pallas_call(
        body, name=name, grid=(h, r // tr),
        in_specs=[part(q) for q in range(nk)],
        out_specs=pl.BlockSpec((None, tr, cc), lambda l, i: (l, i, 0)),
        out_shape=jax.ShapeDtypeStruct((h, r, cc), F32),
        compiler_params=_params("parallel", "parallel"),
    )(*([got] * nk))


def _adamw(name, w, g, m, v):
    shape = w.shape
    cols = shape[-1]
    rows = w.size // cols
    tr = rows if rows * cols <= 2 ** 18 else _div_tile(rows, max(8, 2 ** 18 // cols), 8)
    c1 = 1.0 - ADAM_B1 ** ADAM_STEP
    c2 = 1.0 - ADAM_B2 ** ADAM_STEP

    def body(w_ref, g_ref, m_ref, v_ref, d_ref, mo_ref, vo_ref):
        gg = g_ref[...]
        mn = ADAM_B1 * m_ref[...] + (1.0 - ADAM_B1) * gg
        vn = ADAM_B2 * v_ref[...] + (1.0 - ADAM_B2) * (gg * gg)
        d_ref[...] = -ADAM_LR * ((mn / c1) / (jnp.sqrt(vn / c2) + ADAM_EPS) + ADAM_WD * w_ref[...])
        mo_ref[...] = mn
        vo_ref[...] = vn

    blk = pl.BlockSpec((tr, cols), lambda i: (i, 0))
    sds = jax.ShapeDtypeStruct((rows, cols), F32)
    outs = pl.pallas_call(
        body, name=name, grid=(rows // tr,),
        in_specs=[blk] * 4, out_specs=(blk,) * 3, out_shape=(sds,) * 3,
        compiler_params=_params("parallel"),
    )(*[a.reshape(rows, cols) for a in (w, g, m, v)])
    return [o.reshape(shape) for o in outs]


_WEIGHTS = ["ffn1_norm", "ffn1_w_in", "ffn1_w_out", "mix_norm", "ffn2_norm", "ffn2_w_in", "ffn2_w_out",
            "ev_w_in", "ev_conv_w", "ev_q_norm", "ev_k_norm", "ev_w_out", "od_w_in", "od_pool_w",
            "od_pool_scale", "od_sgu_norm", "od_sgu_w", "od_sgu_b", "od_w_out", "final_norm"]
_BIG = ["ffn1_w_in", "ffn1_w_out", "ffn2_w_in", "ffn2_w_out", "ev_w_in", "ev_w_out", "od_w_in", "od_w_out"]
_COL_SHARDED = ("ev_w_in", "od_w_in")
_SMALL_SHARDED = ["ev_conv_w", "od_pool_scale", "od_sgu_norm"]


def _pad_rows(a, mult=8):
    pad = (-a.shape[0]) % mult
    return a if pad == 0 else jnp.concatenate([a, jnp.zeros((pad,) + a.shape[1:], a.dtype)], axis=0)


def _join_cols(g):
    return g.transpose(1, 0, 2).reshape(g.shape[1], N_CHIPS * g.shape[2])


def _split_cols(w):
    return w.reshape(w.shape[0], N_CHIPS, w.shape[1] // N_CHIPS).transpose(1, 0, 2)


def kernel(x, ffn1_norm, ffn1_w_in, ffn1_w_out, mix_norm, ffn2_norm, ffn2_w_in, ffn2_w_out, ev_w_in, ev_conv_w,
           ev_q_norm, ev_k_norm, ev_w_out, od_w_in, od_pool_w, od_pool_scale, od_sgu_norm, od_sgu_w, od_sgu_b,
           od_w_out, final_norm, loss_target, m_ffn1_norm, m_ffn1_w_in, m_ffn1_w_out, m_mix_norm, m_ffn2_norm,
           m_ffn2_w_in, m_ffn2_w_out, m_ev_w_in, m_ev_conv_w, m_ev_q_norm, m_ev_k_norm, m_ev_w_out, m_od_w_in,
           m_od_pool_w, m_od_pool_scale, m_od_sgu_norm, m_od_sgu_w, m_od_sgu_b, m_od_w_out, m_final_norm, v_ffn1_norm,
           v_ffn1_w_in, v_ffn1_w_out, v_mix_norm, v_ffn2_norm, v_ffn2_w_in, v_ffn2_w_out, v_ev_w_in, v_ev_conv_w,
           v_ev_q_norm, v_ev_k_norm, v_ev_w_out, v_od_w_in, v_od_pool_w, v_od_pool_scale, v_od_sgu_norm, v_od_sgu_w,
           v_od_sgu_b, v_od_w_out, v_final_norm):
    return _step(x, ffn1_norm, ffn1_w_in, ffn1_w_out, mix_norm, ffn2_norm, ffn2_w_in, ffn2_w_out, ev_w_in, ev_conv_w,
                 ev_q_norm, ev_k_norm, ev_w_out, od_w_in, od_pool_w, od_pool_scale, od_sgu_norm, od_sgu_w, od_sgu_b,
                 od_w_out, final_norm, loss_target, m_ffn1_norm, m_ffn1_w_in, m_ffn1_w_out, m_mix_norm, m_ffn2_norm,
                 m_ffn2_w_in, m_ffn2_w_out, m_ev_w_in, m_ev_conv_w, m_ev_q_norm, m_ev_k_norm, m_ev_w_out, m_od_w_in,
                 m_od_pool_w, m_od_pool_scale, m_od_sgu_norm, m_od_sgu_w, m_od_sgu_b, m_od_w_out, m_final_norm,
                 v_ffn1_norm, v_ffn1_w_in, v_ffn1_w_out, v_mix_norm, v_ffn2_norm, v_ffn2_w_in, v_ffn2_w_out,
                 v_ev_w_in, v_ev_conv_w, v_ev_q_norm, v_ev_k_norm, v_ev_w_out, v_od_w_in, v_od_pool_w,
                 v_od_pool_scale, v_od_sgu_norm, v_od_sgu_w, v_od_sgu_b, v_od_w_out, v_final_norm)


def _step(*args):
    nw = len(_WEIGHTS)
    x = args[0]
    w = dict(zip(_WEIGHTS, args[1:1 + nw]))
    target = args[1 + nw]
    m = dict(zip(_WEIGHTS, args[2 + nw:2 + 2 * nw]))
    v = dict(zip(_WEIGHTS, args[2 + 2 * nw:2 + 3 * nw]))
    depth = w["ffn1_norm"].shape[0]
    n_even, n_odd = w["ev_w_in"].shape[0], w["od_w_in"].shape[0]
    chip = 2 * lax.axis_index("x") + lax.axis_index("y")
    core = lax.axis_index("c")

    small_rows = [w["ev_conv_w"].reshape(3 * n_even, LANES), w["od_pool_scale"], w["od_sgu_norm"]]
    gathered, small = _gather_shards([w[n].astype(_ACT) for n in _BIG], _pad_rows(jnp.concatenate(small_rows, axis=0)))
    gw = dict(zip(_BIG, gathered))
    conv_w = small[:, :3 * n_even].reshape(N_CHIPS, n_even, 3, LANES).transpose(1, 2, 0, 3).reshape(n_even, 3, CONV_WIDTH)
    pool_scale = small[:, 3 * n_even:3 * n_even + n_odd].transpose(1, 0, 2).reshape(n_odd, HALF)
    sgu_norm = small[:, 3 * n_even + n_odd:3 * n_even + 2 * n_odd].transpose(1, 0, 2).reshape(n_odd, HALF)

    def rows_of(name, l):
        g = gw[name][:, l]
        return g.reshape(N_CHIPS * g.shape[1], g.shape[2])

    full = dict(ffn1=[], ffn2=[], mix=[], final_norm=w["final_norm"])
    for l in range(depth):
        for nm in ("ffn1", "ffn2"):
            full[nm].append(dict(norm=w[nm + "_norm"][l], w_in4=gw[nm + "_w_in"][:, l], w_out=rows_of(nm + "_w_out", l)))
        j = l // 2
        if l % 2 == 0:
            full["mix"].append(dict(norm=w["mix_norm"][l], w_in=_join_cols(gw["ev_w_in"][:, j]), conv_w=conv_w[j],
                                    q_gain=w["ev_q_norm"][j], k_gain=w["ev_k_norm"][j], w_out=rows_of("ev_w_out", j)))
        else:
            full["mix"].append(dict(norm=w["mix_norm"][l], w_in=_join_cols(gw["od_w_in"][:, j]), pool_w=w["od_pool_w"][j],
                                    pool_scale=pool_scale[j], sgu_norm=sgu_norm[j], sgu_w=w["od_sgu_w"][j],
                                    sgu_b=w["od_sgu_b"][j], w_out=rows_of("od_w_out", j)))

    loss_part, grad_x, g = _local_step(x, target, full)
    loss = lax.psum(loss_part, ("x", "y", "c"))

    def by_chip(dw):
        return dw.reshape(N_CHIPS, dw.shape[0] // N_CHIPS, dw.shape[1])

    ev = [l for l in range(depth) if l % 2 == 0]
    od = [l for l in range(depth) if l % 2 == 1]
    local = {
        "ffn1_w_in": [g["ffn1"][l]["w_in4"] for l in range(depth)],
        "ffn1_w_out": [by_chip(g["ffn1"][l]["w_out"]) for l in range(depth)],
        "ffn2_w_in": [g["ffn2"][l]["w_in4"] for l in range(depth)],
        "ffn2_w_out": [by_chip(g["ffn2"][l]["w_out"]) for l in range(depth)],
        "ev_w_in": [_split_cols(g["mix"][l]["w_in"]) for l in ev],
        "ev_w_out": [by_chip(g["mix"][l]["w_out"]) for l in ev],
        "od_w_in": [_split_cols(g["mix"][l]["w_in"]) for l in od],
        "od_w_out": [by_chip(g["mix"][l]["w_out"]) for l in od],
    }
    stacked = [jnp.stack(local[n], axis=1) for n in _BIG]
    from_sibling = _swap_halves(stacked)
    core_arr = core.astype(jnp.int32).reshape(1)
    halves = [_add_sibling("add_sibling_" + n, s, o, core_arr) for n, s, o in zip(_BIG, stacked, from_sibling)]
    from_chips = _scatter_chips(halves)
    reduced = [_add_chips("add_chips_" + n, p) for n, p in zip(_BIG, from_chips)]
    grads = dict(zip(_BIG, _join_halves(reduced)))

    small_grads = {
        "ffn1_norm": jnp.stack([g["ffn1"][l]["norm"] for l in range(depth)]),
        "mix_norm": jnp.stack([g["mix"][l]["norm"] for l in range(depth)]),
        "ffn2_norm": jnp.stack([g["ffn2"][l]["norm"] for l in range(depth)]),
        "final_norm": g["final_norm"],
        "ev_conv_w": jnp.stack([g["mix"][l]["conv_w"] for l in ev]),
        "ev_q_norm": jnp.stack([g["mix"][l]["q_gain"] for l in ev]),
        "ev_k_norm": jnp.stack([g["mix"][l]["k_gain"] for l in ev]),
        "od_pool_w": jnp.stack([g["mix"][l]["pool_w"] for l in od]),
        "od_pool_scale": jnp.stack([g["mix"][l]["pool_scale"] for l in od]),
        "od_sgu_norm": jnp.stack([g["mix"][l]["sgu_norm"] for l in od]),
        "od_sgu_w": jnp.stack([g["mix"][l]["sgu_w"] for l in od]),
        "od_sgu_b": jnp.stack([g["mix"][l]["sgu_b"] for l in od]),
    }
    names = list(small_grads)
    flat = jnp.concatenate([small_grads[n].reshape(-1) for n in names])
    total = flat.shape[0]
    flat = jnp.concatenate([flat, jnp.zeros((-total) % (8 * LANES), F32)])
    summed = _allreduce_small(flat.reshape(-1, LANES)).reshape(-1)
    off = 0
    for n in names:
        size = small_grads[n].size
        full_grad = summed[off:off + size].reshape(small_grads[n].shape)
        off += size
        if n in _SMALL_SHARDED:
            full_grad = lax.dynamic_slice_in_dim(full_grad, chip * LANES, LANES, axis=full_grad.ndim - 1)
        grads[n] = full_grad

    deltas, new_m, new_v = [], [], []
    for n in _WEIGHTS:
        d_n, m_n, v_n = _adamw("adamw_" + n, w[n], grads[n], m[n], v[n])
        deltas.append(d_n)
        new_m.append(m_n)
        new_v.append(v_n)
    return (loss, grad_x, *[grads[n] for n in _WEIGHTS], *deltas, *new_m, *new_v)
```

```python
import jax
import jax.numpy as jnp
from jax import lax
from jax.experimental import pallas as pl
from jax.experimental.pallas import tpu as pltpu

F32 = jnp.float32
_MXU = jnp.bfloat16
_ACT = jnp.bfloat16

D_MODEL = 1024
GRID_W = 64
HEAD_DIM = 64
N_Q_HEADS = 8
N_KV_HEADS = 2
Q_PER_KV = N_Q_HEADS // N_KV_HEADS
ATTN_WIDTH = N_Q_HEADS * HEAD_DIM
KV_WIDTH = N_KV_HEADS * HEAD_DIM
ROPE_THETA = 10000.0
CONV_WIDTH = D_MODEL // 2
POOL_RADII = (1, 2, 4, 8)
POOL_GROUP = 128
SGU_GROUP = 128
SGU_CHUNK = 128
N_GROUPS = 4
HALF = D_MODEL // 2
EPS = 1e-6
HALO = 8
LANES = 128
N_CHIPS = 4
N_DEV = 8

ADAM_LR = 0.001
ADAM_B1 = 0.9
ADAM_B2 = 0.999
ADAM_EPS = 1e-08
ADAM_WD = 0.01
ADAM_STEP = 10

_VMEM_LIMIT = 56 * 2 ** 20
_MESH = pl.DeviceIdType.MESH
_ANY = pl.BlockSpec(memory_space=pl.ANY)
_VMEM = pl.BlockSpec(memory_space=pltpu.VMEM)

_DN = {
    "nn": (((1,), (0,)), ((), ())),
    "nt": (((1,), (1,)), ((), ())),
    "tn": (((0,), (0,)), ((), ())),
}


def _params(*sem):
    return pltpu.CompilerParams(dimension_semantics=sem, vmem_limit_bytes=_VMEM_LIMIT)


def _tile(n, cap):
    best = None
    d = LANES
    while d <= min(n, cap):
        if n % d == 0:
            best = d
        d += LANES
    return best if best is not None else n


def _dot(a, b, mode="nn"):
    return lax.dot_general(a.astype(_MXU), b.astype(_MXU), _DN[mode], preferred_element_type=F32)


def _cat(*vals):
    vals = [v.astype(_MXU) for v in vals]
    return vals[0] if len(vals) == 1 else jnp.concatenate(vals, axis=1)


def _sigmoid(g):
    return 1.0 / (1.0 + jnp.exp(-g))


def _swiglu(g, u):
    g = g.astype(F32)
    return (g * _sigmoid(g)) * u.astype(F32)


_GELU_C = 0.7978845608028654


def _gelu(x):
    return 0.5 * x * (1.0 + jnp.tanh(_GELU_C * (x + 0.044715 * (x * x * x))))


def _gelu_grad(x):
    t = jnp.tanh(_GELU_C * (x + 0.044715 * (x * x * x)))
    return 0.5 * (1.0 + t) + 0.5 * x * (1.0 - t * t) * (_GELU_C * (1.0 + 3.0 * 0.044715 * (x * x)))


def _mm(name, grid, mode, a_ops, b_ops, e_ops, out_shape, out_specs, acc_shape, a_fn=_cat, b_fn=_cat, epi=None):
    ni, nj, nk = grid
    na, nb, ne = len(a_ops), len(b_ops), len(e_ops)
    multi = isinstance(out_shape, (list, tuple))
    no = len(out_shape) if multi else 1

    def body(*refs):
        a_refs = refs[:na]
        b_refs = refs[na:na + nb]
        e_refs = refs[na + nb:na + nb + ne]
        o_refs = refs[na + nb + ne:na + nb + ne + no]
        a = a_fn(*[r[...] for r in a_refs])
        b = b_fn(*[r[...] for r in b_refs])
        p = _dot(a, b, mode)

        def finish(acc):
            if epi is None:
                o_refs[0][...] = acc.astype(o_refs[0].dtype)
            else:
                epi(acc, [r[...] for r in e_refs], o_refs)

        if nk == 1:
            finish(p)
        else:
            acc_ref = refs[-1]
            k = pl.program_id(2)

            @pl.when(k == 0)
            def _():
                acc_ref[...] = p

            @pl.when(k > 0)
            def _():
                acc_ref[...] += p

            @pl.when(k == nk - 1)
            def _():
                finish(acc_ref[...])

    ops = list(a_ops) + list(b_ops) + list(e_ops)
    return pl.pallas_call(
        body, name=name, grid=grid,
        in_specs=[s for _, s in ops],
        out_specs=out_specs, out_shape=out_shape,
        scratch_shapes=[pltpu.VMEM(acc_shape, F32)] if nk > 1 else [],
        compiler_params=_params("parallel", "parallel", "arbitrary"),
    )(*[a for a, _ in ops])


def _rows(t):
    return _tile(t, 512)


def _rmsnorm_fwd(name, x, gain):
    t, d = x.shape
    tr = _rows(t)

    def body(x_ref, g_ref, h_ref):
        xf = x_ref[...]
        r = lax.rsqrt(jnp.mean(xf * xf, axis=-1, keepdims=True) + EPS)
        h_ref[...] = ((xf * r) * g_ref[...]).astype(h_ref.dtype)

    return pl.pallas_call(
        body, name=name, grid=(t // tr,),
        in_specs=[pl.BlockSpec((tr, d), lambda i: (i, 0)), pl.BlockSpec((1, d), lambda i: (0, 0))],
        out_specs=pl.BlockSpec((tr, d), lambda i: (i, 0)),
        out_shape=jax.ShapeDtypeStruct((t, d), _ACT),
        compiler_params=_params("parallel"),
    )(x, gain.reshape(1, d))


def _rmsnorm_bwd(name, dh, x, gain, dres):
    t, d = x.shape
    tr = _rows(t)

    def body(dh_ref, x_ref, g_ref, dres_ref, dx_ref, dg_ref):
        i = pl.program_id(0)
        xf = x_ref[...]
        r = lax.rsqrt(jnp.mean(xf * xf, axis=-1, keepdims=True) + EPS)
        xhat = xf * r
        dy = dh_ref[...].astype(F32)
        dgx = dy * g_ref[...]
        m = jnp.mean(dgx * xhat, axis=-1, keepdims=True)
        dx_ref[...] = dres_ref[...] + r * (dgx - xhat * m)
        part = jnp.sum(dy * xhat, axis=0, keepdims=True)

        @pl.when(i == 0)
        def _():
            dg_ref[...] = part

        @pl.when(i > 0)
        def _():
            dg_ref[...] += part

    row = pl.BlockSpec((tr, d), lambda i: (i, 0))
    vec = pl.BlockSpec((1, d), lambda i: (0, 0))
    dx, dg = pl.pallas_call(
        body, name=name, grid=(t // tr,),
        in_specs=[row, row, vec, row],
        out_specs=(row, vec),
        out_shape=(jax.ShapeDtypeStruct((t, d), F32), jax.ShapeDtypeStruct((1, d), F32)),
        compiler_params=_params("arbitrary"),
    )(dh, x, gain.reshape(1, d), dres)
    return dx, dg.reshape(d)


def _final_loss(name, x, gain, target):
    t, d = x.shape
    tr = _rows(t)

    def body(x_ref, g_ref, t_ref, dx_ref, dg_ref, loss_ref):
        i = pl.program_id(0)
        xf = x_ref[...]
        r = lax.rsqrt(jnp.mean(xf * xf, axis=-1, keepdims=True) + EPS)
        xhat = xf * r
        g = g_ref[...]
        err = xhat * g - t_ref[...]
        lpart = 0.5 * jnp.sum(jnp.mean(err * err, axis=-1, keepdims=True), axis=0, keepdims=True)
        dy = err * (1.0 / d)
        dgx = dy * g
        m = jnp.mean(dgx * xhat, axis=-1, keepdims=True)
        dx_ref[...] = r * (dgx - xhat * m)
        part = jnp.sum(dy * xhat, axis=0, keepdims=True)
        lrow = jnp.broadcast_to(lpart, (1, LANES))

        @pl.when(i == 0)
        def _():
            dg_ref[...] = part
            loss_ref[...] = lrow

        @pl.when(i > 0)
        def _():
            dg_ref[...] += part
            loss_ref[...] += lrow

    row = pl.BlockSpec((tr, d), lambda i: (i, 0))
    vec = pl.BlockSpec((1, d), lambda i: (0, 0))
    dx, dg, loss = pl.pallas_call(
        body, name=name, grid=(t // tr,),
        in_specs=[row, vec, row],
        out_specs=(row, vec, pl.BlockSpec((1, LANES), lambda i: (0, 0))),
        out_shape=(jax.ShapeDtypeStruct((t, d), F32), jax.ShapeDtypeStruct((1, d), F32),
                   jax.ShapeDtypeStruct((1, LANES), F32)),
        compiler_params=_params("arbitrary"),
    )(x, gain.reshape(1, d), target)
    return loss[0, 0], dx, dg.reshape(d)


def _ffn_fwd(tag, x, gain, w_in4, w_out):
    t, d = x.shape
    fs = w_in4.shape[2]
    f = 2 * fs
    tm = _tile(t, 512)
    h = _rmsnorm_fwd(tag + "_norm", x, gain)
    gu = _mm(
        tag + "_in", (t // tm, N_CHIPS, 1), "nn",
        [(h, pl.BlockSpec((tm, d), lambda i, j, k: (i, 0)))],
        [(w_in4, pl.BlockSpec((None, d, fs), lambda i, j, k: (j, 0, 0)))], [],
        jax.ShapeDtypeStruct((2, t, f), _ACT),
        pl.BlockSpec((None, tm, fs), lambda i, j, k: (j // 2, i, j % 2)), None)
    tm2 = _tile(t, 256)

    def epi(acc, e, o):
        o[0][...] = e[0] + 0.5 * acc

    x_out = _mm(
        tag + "_out", (t // tm2, 1, 1), "nn",
        [(gu, pl.BlockSpec((None, tm2, f), lambda i, j, k: (0, i, 0))),
         (gu, pl.BlockSpec((None, tm2, f), lambda i, j, k: (1, i, 0)))],
        [(w_out, pl.BlockSpec((f, d), lambda i, j, k: (0, 0)))],
        [(x, pl.BlockSpec((tm2, d), lambda i, j, k: (i, 0)))],
        jax.ShapeDtypeStruct((t, d), F32),
        pl.BlockSpec((tm2, d), lambda i, j, k: (i, 0)), None,
        a_fn=_swiglu, epi=epi)
    return x_out, (x, h, gu)


def _ffn_bwd(tag, dxo, saved, gain, w_in4, w_out):
    x, h, gu = saved
    t, d = x.shape
    fs = w_in4.shape[2]
    f = 2 * fs
    tm = _tile(t, 512)
    tk = _tile(t, 512)

    def epi_act(acc, e, o):
        g = e[0].astype(F32)
        u = e[1].astype(F32)
        da = 0.5 * acc
        sig = _sigmoid(g)
        o[0][0] = (da * u * (sig * (1.0 + g * (1.0 - sig)))).astype(o[0].dtype)
        o[0][1] = (da * (g * sig)).astype(o[0].dtype)

    dgu = _mm(
        tag + "_dact", (t // tm, 2, 1), "nt",
        [(dxo, pl.BlockSpec((tm, d), lambda i, j, k: (i, 0)))],
        [(w_out, pl.BlockSpec((fs, d), lambda i, j, k: (j, 0)))],
        [(gu, pl.BlockSpec((None, tm, fs), lambda i, j, k: (0, i, j))),
         (gu, pl.BlockSpec((None, tm, fs), lambda i, j, k: (1, i, j)))],
        jax.ShapeDtypeStruct((2, t, f), _ACT),
        pl.BlockSpec((2, tm, fs), lambda i, j, k: (0, i, j)), None, epi=epi_act)

    def epi_half(acc, e, o):
        o[0][...] = (0.5 * acc).astype(o[0].dtype)

    dw_out = _mm(
        tag + "_dwout", (2, 1, t // tk), "tn",
        [(gu, pl.BlockSpec((None, tk, fs), lambda i, j, k: (0, k, i))),
         (gu, pl.BlockSpec((None, tk, fs), lambda i, j, k: (1, k, i)))],
        [(dxo, pl.BlockSpec((tk, d), lambda i, j, k: (k, 0)))], [],
        jax.ShapeDtypeStruct((f, d), _ACT),
        pl.BlockSpec((fs, d), lambda i, j, k: (i, 0)), (fs, d),
        a_fn=_swiglu, epi=epi_half)
    dh = _mm(
        tag + "_dh", (t // tm, 1, N_CHIPS), "nt",
        [(dgu, pl.BlockSpec((None, tm, fs), lambda i, j, k: (k // 2, i, k % 2)))],
        [(w_in4, pl.BlockSpec((None, d, fs), lambda i, j, k: (k, 0, 0)))], [],
        jax.ShapeDtypeStruct((t, d), F32),
        pl.BlockSpec((tm, d), lambda i, j, k: (i, 0)), (tm, d))
    dw_in4 = _mm(
        tag + "_dwin", (1, N_CHIPS, t // tk), "tn",
        [(h, pl.BlockSpec((tk, d), lambda i, j, k: (k, 0)))],
        [(dgu, pl.BlockSpec((None, tk, fs), lambda i, j, k: (j // 2, k, j % 2)))], [],
        jax.ShapeDtypeStruct((N_CHIPS, d, fs), _ACT),
        pl.BlockSpec((None, d, fs), lambda i, j, k: (j, 0, 0)), (d, fs))
    dx, dgain = _rmsnorm_bwd(tag + "_dnorm", dh, x, gain, dxo)
    return dx, dgain, dw_in4, dw_out


def _proj_in(tag, h, w_in):
    t, d = h.shape
    n = w_in.shape[1]
    tm = _tile(t, 512)
    return _mm(
        tag + "_in", (t // tm, 1, 1), "nn",
        [(h, pl.BlockSpec((tm, d), lambda i, j, k: (i, 0)))],
        [(w_in, pl.BlockSpec((d, n), lambda i, j, k: (0, 0)))], [],
        jax.ShapeDtypeStruct((t, n), _ACT),
        pl.BlockSpec((tm, n), lambda i, j, k: (i, 0)), None)


def _proj_out(tag, x, parts, w_out):
    t, d = x.shape
    tm = _tile(t, 512)

    def epi(acc, e, o):
        o[0][...] = e[0] + acc

    return _mm(
        tag + "_out", (t // tm, 1, 1), "nn",
        [(p, pl.BlockSpec((tm, p.shape[1]), lambda i, j, k: (i, 0))) for p in parts],
        [(w_out, pl.BlockSpec(w_out.shape, lambda i, j, k: (0, 0)))],
        [(x, pl.BlockSpec((tm, d), lambda i, j, k: (i, 0)))],
        jax.ShapeDtypeStruct((t, d), F32),
        pl.BlockSpec((tm, d), lambda i, j, k: (i, 0)), None, epi=epi)


def _proj_out_bwd(tag, dxo, parts, w_out):
    t, d = dxo.shape
    mix = w_out.shape[0]
    tm = _tile(t, 512)
    tk = _tile(t, 512)
    d_mix = _mm(
        tag + "_dmix", (t // tm, 1, 1), "nt",
        [(dxo, pl.BlockSpec((tm, d), lambda i, j, k: (i, 0)))],
        [(w_out, pl.BlockSpec((mix, d), lambda i, j, k: (0, 0)))], [],
        jax.ShapeDtypeStruct((t, mix), F32),
        pl.BlockSpec((tm, mix), lambda i, j, k: (i, 0)), None)
    dw_out = _mm(
        tag + "_dwout", (1, 1, t // tk), "tn",
        [(p, pl.BlockSpec((tk, p.shape[1]), lambda i, j, k: (k, 0))) for p in parts],
        [(dxo, pl.BlockSpec((tk, d), lambda i, j, k: (k, 0)))], [],
        jax.ShapeDtypeStruct((mix, d), _ACT),
        pl.BlockSpec((mix, d), lambda i, j, k: (0, 0)), (mix, d))
    return d_mix, dw_out


def _proj_in_bwd(tag, h, dparts, w_in):
    t, d = h.shape
    n = w_in.shape[1]
    tm = _tile(t, 512)
    tk = _tile(t, 512)
    dh = _mm(
        tag + "_dh", (t // tm, 1, 1), "nt",
        [(p, pl.BlockSpec((tm, p.shape[1]), lambda i, j, k: (i, 0))) for p in dparts],
        [(w_in, pl.BlockSpec((d, n), lambda i, j, k: (0, 0)))], [],
        jax.ShapeDtypeStruct((t, d), F32),
        pl.BlockSpec((tm, d), lambda i, j, k: (i, 0)), None)
    dw_in = _mm(
        tag + "_dwin", (1, 1, t // tk), "tn",
        [(h, pl.BlockSpec((tk, d), lambda i, j, k: (k, 0)))],
        [(p, pl.BlockSpec((tk, p.shape[1]), lambda i, j, k: (k, 0))) for p in dparts], [],
        jax.ShapeDtypeStruct((d, n), _ACT),
        pl.BlockSpec((d, n), lambda i, j, k: (0, 0)), (d, n))
    return dh, dw_in


def _shifted(pad_ref, val, s):
    pad_ref[pl.ds(HALO, s), :] = val
    return pad_ref[pl.ds(HALO - 1, s), :], pad_ref[pl.ds(HALO + 1, s), :]


def _zero_halo(pad_ref, s):
    z = jnp.zeros((HALO, pad_ref.shape[1]), F32)
    pad_ref[pl.ds(0, HALO), :] = z
    pad_ref[pl.ds(HALO + s, HALO), :] = z


def _conv_fwd(tag, proj, conv_w, nb, s):
    t = proj.shape[0]
    ncb = CONV_WIDTH // LANES

    def body(gb_ref, gc_ref, hc_ref, w_ref, a_ref, pad_ref):
        _zero_halo(pad_ref, s)
        cg = gc_ref[...].astype(F32) * hc_ref[...].astype(F32)
        prev, nxt = _shifted(pad_ref, cg, s)
        w = w_ref[...]
        conv = prev * w[0:1, :] + cg * w[1:2, :] + nxt * w[2:3, :]
        a_ref[...] = (gb_ref[...].astype(F32) * conv).astype(a_ref.dtype)

    def col(off):
        return pl.BlockSpec((s, LANES), lambda b, c: (b, off + c))

    return pl.pallas_call(
        body, name=tag + "_conv", grid=(nb, ncb),
        in_specs=[col(0), col(ncb), col(2 * ncb), pl.BlockSpec((3, LANES), lambda b, c: (0, c))],
        out_specs=col(0),
        out_shape=jax.ShapeDtypeStruct((t, CONV_WIDTH), _ACT),
        scratch_shapes=[pltpu.VMEM((s + 2 * HALO, LANES), F32)],
        compiler_params=_params("parallel", "parallel"),
    )(proj, proj, proj, conv_w)


def _conv_bwd(tag, proj, conv_w, d_mix, nb, s):
    t = proj.shape[0]
    ncb = CONV_WIDTH // LANES

    def body(gb_ref, gc_ref, hc_ref, w_ref, da_ref, dgb_ref, dgc_ref, dhc_ref, dw_ref, pad_ref):
        b = pl.program_id(1)
        _zero_halo(pad_ref, s)
        gb = gb_ref[...].astype(F32)
        gc = gc_ref[...].astype(F32)
        hc = hc_ref[...].astype(F32)
        w = w_ref[...]
        da = da_ref[...]
        cg = gc * hc
        prev, nxt = _shifted(pad_ref, cg, s)
        conv = prev * w[0:1, :] + cg * w[1:2, :] + nxt * w[2:3, :]
        dgb_ref[...] = (da * conv).astype(dgb_ref.dtype)
        dconv = da * gb
        dw = jnp.concatenate([
            jnp.sum(dconv * prev, axis=0, keepdims=True),
            jnp.sum(dconv * cg, axis=0, keepdims=True),
            jnp.sum(dconv * nxt, axis=0, keepdims=True)], axis=0)
        dprev, dnxt = _shifted(pad_ref, dconv, s)
        dcg = dnxt * w[0:1, :] + dconv * w[1:2, :] + dprev * w[2:3, :]
        dgc_ref[...] = (dcg * hc).astype(dgc_ref.dtype)
        dhc_ref[...] = (dcg * gc).astype(dhc_ref.dtype)

        @pl.when(b == 0)
        def _():
            dw_ref[...] = dw

        @pl.when(b > 0)
        def _():
            dw_ref[...] += dw

    def col(off):
        return pl.BlockSpec((s, LANES), lambda c, b: (b, off + c))

    wspec = pl.BlockSpec((3, LANES), lambda c, b: (0, c))
    act = jax.ShapeDtypeStruct((t, CONV_WIDTH), _ACT)
    return pl.pallas_call(
        body, name=tag + "_dconv", grid=(ncb, nb),
        in_specs=[col(0), col(ncb), col(2 * ncb), wspec, col(0)],
        out_specs=(col(0), col(0), col(0), wspec),
        out_shape=(act, act, act, jax.ShapeDtypeStruct((3, CONV_WIDTH), F32)),
        scratch_shapes=[pltpu.VMEM((s + 2 * HALO, LANES), F32)],
        compiler_params=_params("parallel", "arbitrary"),
    )(proj, proj, proj, conv_w, d_mix)


def _rope_tables(s):
    rows = s // GRID_W
    r_idx, c_idx = jnp.meshgrid(jnp.arange(rows), jnp.arange(GRID_W), indexing="ij")
    r_idx = r_idx.reshape(-1).astype(F32)
    c_idx = c_idx.reshape(-1).astype(F32)
    n_freq = HEAD_DIM // 4
    inv = ROPE_THETA ** (-jnp.arange(n_freq, dtype=F32) / n_freq)
    ang = jnp.concatenate([r_idx[:, None] * inv, c_idx[:, None] * inv], axis=-1)
    cos = jnp.repeat(jnp.cos(ang), 2, axis=1)
    sin = jnp.repeat(jnp.sin(ang), 2, axis=1)
    sign = jnp.where(jnp.arange(HEAD_DIM) % 2 == 0, -1.0, 1.0).astype(F32)
    return jnp.tile(cos, (1, LANES // HEAD_DIM)), jnp.tile(sin * sign, (1, LANES // HEAD_DIM))


def _head_ones():
    i = jnp.arange(LANES) // HEAD_DIM
    return (i[:, None] == i[None, :]).astype(jnp.bfloat16)


def _head_sum(v, ones):
    outs = []
    for j in range(v.shape[1] // LANES):
        c = v[:, j * LANES:(j + 1) * LANES]
        hi = c.astype(jnp.bfloat16)
        lo = (c - hi.astype(F32)).astype(jnp.bfloat16)
        outs.append(jnp.dot(hi, ones, preferred_element_type=F32) + jnp.dot(lo, ones, preferred_element_type=F32))
    return outs[0] if len(outs) == 1 else jnp.concatenate(outs, axis=1)


def _pair_swap(v):
    outs = []
    for j in range(v.shape[1] // LANES):
        c = v[:, j * LANES:(j + 1) * LANES]
        lane = lax.broadcasted_iota(jnp.int32, c.shape, 1)
        outs.append(jnp.where(lane % 2 == 0, pltpu.roll(c, LANES - 1, 1), pltpu.roll(c, 1, 1)))
    return outs[0] if len(outs) == 1 else jnp.concatenate(outs, axis=1)


def _wide(tab, width):
    return tab if width == LANES else jnp.concatenate([tab] * (width // LANES), axis=1)


def _qk_fwd(tag, proj, q_gain, k_gain, cos, sin, nb, s):
    t = proj.shape[0]
    tr = _tile(s, 512)
    ns = s // tr
    q_off = 3 * CONV_WIDTH // ATTN_WIDTH
    k_off = (3 * CONV_WIDTH + ATTN_WIDTH) // KV_WIDTH

    def body(q_ref, k_ref, qg_ref, kg_ref, cos_ref, sin_ref, ones_ref, qo_ref, ko_ref):
        ones = ones_ref[...]
        for src, g_ref, dst in ((q_ref, qg_ref, qo_ref), (k_ref, kg_ref, ko_ref)):
            v = src[...].astype(F32)
            w = v.shape[1]
            r = lax.rsqrt(_head_sum(v * v, ones) * (1.0 / HEAD_DIM) + EPS)
            vn = (v * r) * g_ref[...]
            dst[...] = (vn * _wide(cos_ref[...], w) + _pair_swap(vn) * _wide(sin_ref[...], w)).astype(dst.dtype)

    tab = pl.BlockSpec((tr, LANES), lambda i: (i % ns, 0))
    return pl.pallas_call(
        body, name=tag + "_qk", grid=(t // tr,),
        in_specs=[pl.BlockSpec((tr, ATTN_WIDTH), lambda i: (i, q_off)),
                  pl.BlockSpec((tr, KV_WIDTH), lambda i: (i, k_off)),
                  pl.BlockSpec((1, ATTN_WIDTH), lambda i: (0, 0)),
                  pl.BlockSpec((1, KV_WIDTH), lambda i: (0, 0)),
                  tab, tab, pl.BlockSpec((LANES, LANES), lambda i: (0, 0))],
        out_specs=(pl.BlockSpec((tr, ATTN_WIDTH), lambda i: (i, 0)),
                   pl.BlockSpec((tr, KV_WIDTH), lambda i: (i, 0))),
        out_shape=(jax.ShapeDtypeStruct((t, ATTN_WIDTH), _ACT), jax.ShapeDtypeStruct((t, KV_WIDTH), _ACT)),
        compiler_params=_params("parallel"),
    )(proj, proj, jnp.tile(q_gain, N_Q_HEADS).reshape(1, ATTN_WIDTH),
      jnp.tile(k_gain, N_KV_HEADS).reshape(1, KV_WIDTH), cos, sin, _head_ones())


def _qk_bwd(tag, proj, q_gain, k_gain, cos, sin, dq_rot, dk_rot, nb, s):
    t = proj.shape[0]
    tr = _tile(s, 512)
    ns = s // tr
    q_off = 3 * CONV_WIDTH // ATTN_WIDTH
    k_off = (3 * CONV_WIDTH + ATTN_WIDTH) // KV_WIDTH

    def body(q_ref, k_ref, qg_ref, kg_ref, cos_ref, sin_ref, ones_ref, dqr_ref, dkr_ref,
             dq_ref, dk_ref, dqg_ref, dkg_ref):
        i = pl.program_id(0)
        ones = ones_ref[...]
        for src, g_ref, dr_ref, dst, dg_ref in ((q_ref, qg_ref, dqr_ref, dq_ref, dqg_ref),
                                                (k_ref, kg_ref, dkr_ref, dk_ref, dkg_ref)):
            v = src[...].astype(F32)
            w = v.shape[1]
            r = lax.rsqrt(_head_sum(v * v, ones) * (1.0 / HEAD_DIM) + EPS)
            xhat = v * r
            dr = dr_ref[...]
            dvn = dr * _wide(cos_ref[...], w) + _pair_swap(dr * _wide(sin_ref[...], w))
            dgx = dvn * g_ref[...]
            m = _head_sum(dgx * xhat, ones) * (1.0 / HEAD_DIM)
            dst[...] = (r * (dgx - xhat * m)).astype(dst.dtype)
            part = jnp.sum(dvn * xhat, axis=0, keepdims=True)
            fold = part[:, 0:HEAD_DIM]
            for hh in range(1, w // HEAD_DIM):
                fold = fold + part[:, hh * HEAD_DIM:(hh + 1) * HEAD_DIM]

            @pl.when(i == 0)
            def _():
                dg_ref[...] = fold

            @pl.when(i > 0)
            def _():
                dg_ref[...] += fold

    tab = pl.BlockSpec((tr, LANES), lambda i: (i % ns, 0))
    qrow = pl.BlockSpec((tr, ATTN_WIDTH), lambda i: (i, 0))
    krow = pl.BlockSpec((tr, KV_WIDTH), lambda i: (i, 0))
    gvec = pl.BlockSpec((1, HEAD_DIM), lambda i: (0, 0))
    dq, dk, dqg, dkg = pl.pallas_call(
        body, name=tag + "_dqk", grid=(t // tr,),
        in_specs=[pl.BlockSpec((tr, ATTN_WIDTH), lambda i: (i, q_off)),
                  pl.BlockSpec((tr, KV_WIDTH), lambda i: (i, k_off)),
                  pl.BlockSpec((1, ATTN_WIDTH), lambda i: (0, 0)),
                  pl.BlockSpec((1, KV_WIDTH), lambda i: (0, 0)),
                  tab, tab, pl.BlockSpec((LANES, LANES), lambda i: (0, 0)), qrow, krow],
        out_specs=(qrow, krow, gvec, gvec),
        out_shape=(jax.ShapeDtypeStruct((t, ATTN_WIDTH), _ACT), jax.ShapeDtypeStruct((t, KV_WIDTH), _ACT),
                   jax.ShapeDtypeStruct((1, HEAD_DIM), F32), jax.ShapeDtypeStruct((1, HEAD_DIM), F32)),
        compiler_params=_params("arbitrary"),
    )(proj, proj, jnp.tile(q_gain, N_Q_HEADS).reshape(1, ATTN_WIDTH),
      jnp.tile(k_gain, N_KV_HEADS).reshape(1, KV_WIDTH), cos, sin, _head_ones(), dq_rot, dk_rot)
    return dq, dk, dqg.reshape(HEAD_DIM), dkg.reshape(HEAD_DIM)


def _head(v, h):
    return v[:, h * HEAD_DIM:(h + 1) * HEAD_DIM]


def _attn_fwd(tag, q, k, proj, nb, s):
    t = q.shape[0]
    tq = _tile(s, 256)
    nq = s // tq
    v_off = (3 * CONV_WIDTH + ATTN_WIDTH + KV_WIDTH) // KV_WIDTH
    scale = HEAD_DIM ** -0.5

    def body(q_ref, k_ref, v_ref, o_ref, lse_ref):
        qv = q_ref[...]
        kv = k_ref[...]
        vv = v_ref[...]
        for h in range(N_Q_HEADS):
            j = h // Q_PER_KV
            sc = _dot(_head(qv, h), _head(kv, j), "nt") * scale
            m = jnp.max(sc, axis=-1, keepdims=True)
            e = jnp.exp(sc - m)
            l = jnp.sum(e, axis=-1, keepdims=True)
            o = _dot(e, _head(vv, j)) * (1.0 / l)
            o_ref[:, h * HEAD_DIM:(h + 1) * HEAD_DIM] = o.astype(o_ref.dtype)
            lse_ref[:, h:h + 1] = m + jnp.log(l)

    return pl.pallas_call(
        body, name=tag + "_attn", grid=(nb, nq),
        in_specs=[pl.BlockSpec((tq, ATTN_WIDTH), lambda b, i: (b * nq + i, 0)),
                  pl.BlockSpec((s, KV_WIDTH), lambda b, i: (b, 0)),
                  pl.BlockSpec((s, KV_WIDTH), lambda b, i: (b, v_off))],
        out_specs=(pl.BlockSpec((tq, ATTN_WIDTH), lambda b, i: (b * nq + i, 0)),
                   pl.BlockSpec((tq, N_Q_HEADS), lambda b, i: (b * nq + i, 0))),
        out_shape=(jax.ShapeDtypeStruct((t, ATTN_WIDTH), _ACT), jax.ShapeDtypeStruct((t, N_Q_HEADS), F32)),
        compiler_params=_params("parallel", "parallel"),
    )(q, k, proj)


def _attn_bwd(tag, q, k, proj, o, lse, d_mix, nb, s):
    t = q.shape[0]
    tq = _tile(s, 256)
    nq = s // tq
    v_off = (3 * CONV_WIDTH + ATTN_WIDTH + KV_WIDTH) // KV_WIDTH
    scale = HEAD_DIM ** -0.5

    def body(q_ref, k_ref, v_ref, o_ref, lse_ref, do_ref, dq_ref, dk_ref, dv_ref):
        i = pl.program_id(1)

        @pl.when(i == 0)
        def _():
            dk_ref[...] = jnp.zeros_like(dk_ref)
            dv_ref[...] = jnp.zeros_like(dv_ref)

        qv = q_ref[...]
        kv = k_ref[...]
        vv = v_ref[...]
        ov = o_ref[...].astype(F32)
        dov = do_ref[...]
        lse = lse_ref[...]
        for h in range(N_Q_HEADS):
            j = h // Q_PER_KV
            cols = slice(j * HEAD_DIM, (j + 1) * HEAD_DIM)
            qh = _head(qv, h)
            kj = _head(kv, j)
            doh = _head(dov, h)
            sc = _dot(qh, kj, "nt") * scale
            p = jnp.exp(sc - lse[:, h:h + 1])
            dp = _dot(doh, _head(vv, j), "nt")
            delta = jnp.sum(doh * _head(ov, h), axis=-1, keepdims=True)
            ds = p * (dp - delta) * scale
            dv_ref[:, cols] += _dot(p, doh, "tn")
            dk_ref[:, cols] += _dot(ds, qh, "tn")
            dq_ref[:, h * HEAD_DIM:(h + 1) * HEAD_DIM] = _dot(ds, kj)

    qrow = pl.BlockSpec((tq, ATTN_WIDTH), lambda b, i: (b * nq + i, 0))
    kvrow = pl.BlockSpec((s, KV_WIDTH), lambda b, i: (b, 0))
    return pl.pallas_call(
        body, name=tag + "_dattn", grid=(nb, nq),
        in_specs=[qrow, kvrow, pl.BlockSpec((s, KV_WIDTH), lambda b, i: (b, v_off)), qrow,
                  pl.BlockSpec((tq, N_Q_HEADS), lambda b, i: (b * nq + i, 0)),
                  pl.BlockSpec((tq, ATTN_WIDTH), lambda b, i: (b * nq + i, 1))],
        out_specs=(qrow, kvrow, kvrow),
        out_shape=(jax.ShapeDtypeStruct((t, ATTN_WIDTH), F32), jax.ShapeDtypeStruct((t, KV_WIDTH), F32),
                   jax.ShapeDtypeStruct((t, KV_WIDTH), F32)),
        compiler_params=_params("parallel", "arbitrary"),
    )(q, k, proj, o, lse, d_mix)


def _even_fwd(tag, x, p, cos, sin, nb, s):
    h = _rmsnorm_fwd(tag + "_norm", x, p["norm"])
    proj = _proj_in(tag, h, p["w_in"])
    a = _conv_fwd(tag, proj, p["conv_w"], nb, s)
    q, k = _qk_fwd(tag, proj, p["q_gain"], p["k_gain"], cos, sin, nb, s)
    o, lse = _attn_fwd(tag, q, k, proj, nb, s)
    x_out = _proj_out(tag, x, [a, o], p["w_out"])
    return x_out, (x, h, proj, a, q, k, o, lse)


def _even_bwd(tag, dxo, saved, p, cos, sin, nb, s):
    x, h, proj, a, q, k, o, lse = saved
    d_mix, dw_out = _proj_out_bwd(tag, dxo, [a, o], p["w_out"])
    dgb, dgc, dhc, dconv_w = _conv_bwd(tag, proj, p["conv_w"], d_mix, nb, s)
    dq_rot, dk_rot, dv = _attn_bwd(tag, q, k, proj, o, lse, d_mix, nb, s)
    dq, dk, dq_gain, dk_gain = _qk_bwd(tag, proj, p["q_gain"], p["k_gain"], cos, sin, dq_rot, dk_rot, nb, s)
    dh, dw_in = _proj_in_bwd(tag, h, [dgb, dgc, dhc, dq, dk, dv], p["w_in"])
    dx, dnorm = _rmsnorm_bwd(tag + "_dnorm", dh, x, p["norm"], dxo)
    grads = dict(norm=dnorm, w_in=dw_in, w_out=dw_out, conv_w=dconv_w, q_gain=dq_gain, k_gain=dk_gain)
    return dx, grads


def _window(pad_ref, val, r, s):
    pad_ref[pl.ds(HALO, s), :] = val
    acc = val
    for d in range(1, r + 1):
        acc = acc + pad_ref[pl.ds(HALO - d, s), :] + pad_ref[pl.ds(HALO + d, s), :]
    return acc


def _count(r, s):
    t = lax.broadcasted_iota(jnp.int32, (s, 1), 0)
    return (jnp.minimum(t + r, s - 1) - jnp.maximum(t - r, 0) + 1).astype(F32)


def _sgu_chunk(u_ref, v_ref, norm, ws_ref, bt, rows):
    uu = u_ref[rows, :].astype(F32)
    vv = v_ref[rows, :].astype(F32)
    gu = _gelu(uu)
    gv = _gelu(vv)
    r = lax.rsqrt(jnp.mean(gv * gv, axis=-1, keepdims=True) + EPS)
    xhat = gv * r
    vn = xhat * norm
    mixed = []
    for g in range(N_GROUPS):
        cols = slice(g * SGU_GROUP, (g + 1) * SGU_GROUP)
        mixed.append(_dot(ws_ref[g], vn[:, cols]) + bt[:, g:g + 1])
    return uu, vv, gu, r, xhat, vn, mixed


def _odd_core_fwd(tag, proj, p, nb, s):
    t = proj.shape[0]
    nchunk = s // SGU_CHUNK

    def body(p_ref, u_ref, v_ref, pw_ref, ps_ref, sn_ref, ws_ref, bt_ref, mix_ref, pad_ref):
        _zero_halo(pad_ref, s)
        for g, r in enumerate(POOL_RADII):
            cols = slice(g * POOL_GROUP, (g + 1) * POOL_GROUP)
            pg = p_ref[:, cols].astype(F32)
            pooled = _window(pad_ref, pg, r, s) / _count(r, s) - pg
            mix_ref[:, cols] = (_dot(pooled, pw_ref[g]) * ps_ref[:, cols]).astype(mix_ref.dtype)
        norm = sn_ref[...]
        bt = bt_ref[...]

        def chunk(n, carry):
            rows = pl.ds(pl.multiple_of(n * SGU_CHUNK, SGU_CHUNK), SGU_CHUNK)
            _, _, gu, _, _, _, mixed = _sgu_chunk(u_ref, v_ref, norm, ws_ref, bt, rows)
            for g in range(N_GROUPS):
                cols = slice(g * SGU_GROUP, (g + 1) * SGU_GROUP)
                mix_ref[rows, HALF + g * SGU_GROUP:HALF + (g + 1) * SGU_GROUP] = (
                    gu[:, cols] * mixed[g]).astype(mix_ref.dtype)
            return carry

        lax.fori_loop(0, nchunk, chunk, 0)

    def col(j):
        return pl.BlockSpec((s, HALF), lambda b: (b, j))

    def whole(a):
        return pl.BlockSpec(a.shape, lambda b: (0,) * a.ndim)

    consts = [p["pool_w"], p["pool_scale"].reshape(1, HALF), p["sgu_norm"].reshape(1, HALF),
              p["sgu_w"], p["sgu_b"].T]
    return pl.pallas_call(
        body, name=tag + "_core", grid=(nb,),
        in_specs=[col(0), col(1), col(2)] + [whole(a) for a in consts],
        out_specs=pl.BlockSpec((s, D_MODEL), lambda b: (b, 0)),
        out_shape=jax.ShapeDtypeStruct((t, D_MODEL), _ACT),
        scratch_shapes=[pltpu.VMEM((s + 2 * HALO, POOL_GROUP), F32)],
        compiler_params=_params("parallel"),
    )(proj, proj, proj, *consts)


def _odd_core_bwd(tag, proj, p, d_mix, nb, s):
    t = proj.shape[0]
    nchunk = s // SGU_CHUNK

    def body(p_ref, u_ref, v_ref, pw_ref, ps_ref, sn_ref, ws_ref, bt_ref, dm_ref,
             dproj_ref, dpw_ref, dps_ref, dsn_ref, dws_ref, dbt_ref, pad_ref):
        b = pl.program_id(0)

        @pl.when(b == 0)
        def _():
            dpw_ref[...] = jnp.zeros_like(dpw_ref)
            dps_ref[...] = jnp.zeros_like(dps_ref)
            dsn_ref[...] = jnp.zeros_like(dsn_ref)
            dws_ref[...] = jnp.zeros_like(dws_ref)
            dbt_ref[...] = jnp.zeros_like(dbt_ref)

        _zero_halo(pad_ref, s)
        for g, r in enumerate(POOL_RADII):
            cols = slice(g * POOL_GROUP, (g + 1) * POOL_GROUP)
            pg = p_ref[:, cols].astype(F32)
            cnt = _count(r, s)
            pooled = _window(pad_ref, pg, r, s) / cnt - pg
            c_pre = _dot(pooled, pw_ref[g])
            dc = dm_ref[:, cols]
            dps_ref[:, cols] += jnp.sum(dc * c_pre, axis=0, keepdims=True)
            dcp = dc * ps_ref[:, cols]
            dpw_ref[g] += _dot(pooled, dcp, "tn")
            dpooled = _dot(dcp, pw_ref[g], "nt")
            dproj_ref[:, cols] = (_window(pad_ref, dpooled / cnt, r, s) - dpooled).astype(dproj_ref.dtype)
        norm = sn_ref[...]
        bt = bt_ref[...]

        def chunk(n, carry):
            rows = pl.ds(pl.multiple_of(n * SGU_CHUNK, SGU_CHUNK), SGU_CHUNK)
            uu, vv, gu, r, xhat, vn, mixed = _sgu_chunk(u_ref, v_ref, norm, ws_ref, bt, rows)
            dd = dm_ref[rows, HALF:D_MODEL]
            dgu, dvn = [], []
            for g in range(N_GROUPS):
                cols = slice(g * SGU_GROUP, (g + 1) * SGU_GROUP)
                dgu.append(dd[:, cols] * mixed[g])
                dmx = dd[:, cols] * gu[:, cols]
                dbt_ref[:, g:g + 1] += jnp.sum(dmx, axis=-1, keepdims=True)
                dws_ref[g] += _dot(dmx, vn[:, cols], "nt")
                dvn.append(_dot(ws_ref[g], dmx, "tn"))
            dgu = jnp.concatenate(dgu, axis=1)
            dvn = jnp.concatenate(dvn, axis=1)
            dsn_ref[...] += jnp.sum(dvn * xhat, axis=0, keepdims=True)
            dgx = dvn * norm
            m = jnp.mean(dgx * xhat, axis=-1, keepdims=True)
            dgv = r * (dgx - xhat * m)
            dproj_ref[rows, HALF:2 * HALF] = (dgu * _gelu_grad(uu)).astype(dproj_ref.dtype)
            dproj_ref[rows, 2 * HALF:3 * HALF] = (dgv * _gelu_grad(vv)).astype(dproj_ref.dtype)
            return carry

        lax.fori_loop(0, nchunk, chunk, 0)

    def col(j):
        return pl.BlockSpec((s, HALF), lambda b: (b, j))

    def whole(a):
        return pl.BlockSpec(a.shape, lambda b: (0,) * a.ndim)

    consts = [p["pool_w"], p["pool_scale"].reshape(1, HALF), p["sgu_norm"].reshape(1, HALF),
              p["sgu_w"], p["sgu_b"].T]
    gshapes = [jax.ShapeDtypeStruct(a.shape, F32) for a in consts]
    dproj, dpw, dps, dsn, dws, dbt = pl.pallas_call(
        body, name=tag + "_dcore", grid=(nb,),
        in_specs=[col(0), col(1), col(2)] + [whole(a) for a in consts]
        + [pl.BlockSpec((s, D_MODEL), lambda b: (b, 0))],
        out_specs=[pl.BlockSpec((s, 3 * HALF), lambda b: (b, 0))] + [whole(a) for a in consts],
        out_shape=[jax.ShapeDtypeStruct((t, 3 * HALF), _ACT)] + gshapes,
        scratch_shapes=[pltpu.VMEM((s + 2 * HALO, POOL_GROUP), F32)],
        compiler_params=_params("arbitrary"),
    )(proj, proj, proj, *consts, d_mix)
    return dproj, dict(pool_w=dpw, pool_scale=dps.reshape(HALF), sgu_norm=dsn.reshape(HALF), sgu_w=dws, sgu_b=dbt.T)


def _odd_fwd(tag, x, p, nb, s):
    h = _rmsnorm_fwd(tag + "_norm", x, p["norm"])
    proj = _proj_in(tag, h, p["w_in"])
    mix = _odd_core_fwd(tag, proj, p, nb, s)
    x_out = _proj_out(tag, x, [mix], p["w_out"])
    return x_out, (x, h, proj, mix)


def _odd_bwd(tag, dxo, saved, p, nb, s):
    x, h, proj, mix = saved
    d_mix, dw_out = _proj_out_bwd(tag, dxo, [mix], p["w_out"])
    dproj, grads = _odd_core_bwd(tag, proj, p, d_mix, nb, s)
    dh, dw_in = _proj_in_bwd(tag, h, [dproj], p["w_in"])
    dx, dnorm = _rmsnorm_bwd(tag + "_dnorm", dh, x, p["norm"], dxo)
    grads.update(norm=dnorm, w_in=dw_in, w_out=dw_out)
    return dx, grads


def _local_step(x3, target3, w):
    nb, s, d = x3.shape
    t = nb * s
    x = x3.reshape(t, d)
    target = target3.reshape(t, d)
    depth = len(w["mix"])
    cos, sin = _rope_tables(s)
    saved = []
    for l in range(depth):
        x, s1 = _ffn_fwd(f"l{l}_ffn1", x, w["ffn1"][l]["norm"], w["ffn1"][l]["w_in4"], w["ffn1"][l]["w_out"])
        if l % 2 == 0:
            x, s2 = _even_fwd(f"l{l}_ev", x, w["mix"][l], cos, sin, nb, s)
        else:
            x, s2 = _odd_fwd(f"l{l}_od", x, w["mix"][l], nb, s)
        x, s3 = _ffn_fwd(f"l{l}_ffn2", x, w["ffn2"][l]["norm"], w["ffn2"][l]["w_in4"], w["ffn2"][l]["w_out"])
        saved.append((s1, s2, s3))
    loss, dx, dfinal = _final_loss("final_loss", x, w["final_norm"], target)
    g = dict(ffn1=[None] * depth, ffn2=[None] * depth, mix=[None] * depth, final_norm=dfinal)
    for l in reversed(range(depth)):
        s1, s2, s3 = saved[l]
        dx, dn, dwi, dwo = _ffn_bwd(f"l{l}_ffn2", dx, s3, w["ffn2"][l]["norm"], w["ffn2"][l]["w_in4"],
                                    w["ffn2"][l]["w_out"])
        g["ffn2"][l] = dict(norm=dn, w_in4=dwi, w_out=dwo)
        if l % 2 == 0:
            dx, g["mix"][l] = _even_bwd(f"l{l}_ev", dx, s2, w["mix"][l], cos, sin, nb, s)
        else:
            dx, g["mix"][l] = _odd_bwd(f"l{l}_od", dx, s2, w["mix"][l], nb, s)
        dx, dn, dwi, dwo = _ffn_bwd(f"l{l}_ffn1", dx, s1, w["ffn1"][l]["norm"], w["ffn1"][l]["w_in4"],
                                    w["ffn1"][l]["w_out"])
        g["ffn1"][l] = dict(norm=dn, w_in4=dwi, w_out=dwo)
    return loss, dx.reshape(nb, s, d), g


_HBM = pl.BlockSpec(memory_space=pltpu.HBM)


def _place():
    x, y, c = lax.axis_index("x"), lax.axis_index("y"), lax.axis_index("c")
    chips = [(1 - x, y), (x, 1 - y), (1 - x, 1 - y)]
    return x, y, c, chips


def _remote(src, dst, send_sem, recv_sem, to):
    return pltpu.make_async_remote_copy(src_ref=src, dst_ref=dst, send_sem=send_sem, recv_sem=recv_sem,
                                        device_id=to, device_id_type=_MESH)


def _gather_shards(arrs, small):
    n = len(arrs)
    own = 6

    def body(*refs):
        ins, sm_in = refs[:n], refs[n]
        outs, sm_out = refs[n + 1:2 * n + 1], refs[2 * n + 1]
        send, recv = refs[2 * n + 2:]
        x, y, c, chips = _place()
        k = 2 * x + y
        sib = (x, y, 1 - c)
        started = []
        for a in range(n + 1):
            src, dst = (ins[a], outs[a]) if a < n else (sm_in, sm_out)
            cp = _remote(src, dst.at[k], send.at[a, own], recv.at[a, own], sib)
            cp.start()
            started.append(cp)
            if a < n:
                h = src.shape[0] // 2
                mine = pl.ds(c * h, h)
                src_part, dst_part = src.at[mine], dst.at[k, mine]
            else:
                src_part, dst_part = src, dst.at[k]
            for j, chip in enumerate(chips):
                cp = _remote(src_part, dst_part, send.at[a, j], recv.at[a, j], (*chip, c))
                cp.start()
                started.append(cp)
        for a in range(n):
            h = ins[a].shape[0] // 2
            mine = pl.ds(c * h, h)
            for j, (px, py) in enumerate(chips):
                landed = outs[a].at[2 * px + py, mine]
                _remote(landed, landed, send.at[a, j], recv.at[a, j], (px, py, c)).wait_recv()
                cp = _remote(landed, landed, send.at[a, 3 + j], recv.at[a, 3 + j], sib)
                cp.start()
                started.append(cp)
        for a in range(n):
            h = ins[a].shape[0] // 2
            other = pl.ds((1 - c) * h, h)
            for j, (px, py) in enumerate(chips):
                passed = outs[a].at[2 * px + py, other]
                _remote(passed, passed, send.at[a, 3 + j], recv.at[a, 3 + j], sib).wait_recv()
        for j, (px, py) in enumerate(chips):
            landed = sm_out.at[2 * px + py]
            _remote(landed, landed, send.at[n, j], recv.at[n, j], (px, py, c)).wait_recv()
        for a in range(n + 1):
            filled = (outs[a] if a < n else sm_out).at[k]
            _remote(filled, filled, send.at[a, own], recv.at[a, own], sib).wait_recv()
        for cp in started:
            cp.wait_send()

    outs = pl.pallas_call(
        body, name="gather_shards",
        in_specs=[_HBM] * (n + 1), out_specs=[_HBM] * (n + 1),
        out_shape=[jax.ShapeDtypeStruct((N_CHIPS,) + a.shape, a.dtype) for a in list(arrs) + [small]],
        scratch_shapes=[pltpu.SemaphoreType.DMA((n + 1, 7)), pltpu.SemaphoreType.DMA((n + 1, 7))],
    )(*arrs, small)
    return outs[:n], outs[n]


def _swap_halves(grads):
    n = len(grads)

    def body(*refs):
        ins, outs = refs[:n], refs[n:2 * n]
        send, recv = refs[2 * n:]
        x, y, c, _ = _place()
        sib = (x, y, 1 - c)
        cps = []
        for a in range(n):
            h = ins[a].shape[1] // 2
            cp = _remote(ins[a].at[:, pl.ds((1 - c) * h, h)], outs[a], send.at[a], recv.at[a], sib)
            cp.start()
            cps.append(cp)
        for cp in cps:
            cp.wait()

    return pl.pallas_call(
        body, name="swap_halves",
        in_specs=[_HBM] * n, out_specs=[_HBM] * n,
        out_shape=[jax.ShapeDtypeStruct((g.shape[0], g.shape[1] // 2) + g.shape[2:], g.dtype) for g in grads],
        scratch_shapes=[pltpu.SemaphoreType.DMA((n,)), pltpu.SemaphoreType.DMA((n,))],
    )(*grads)


def _scatter_chips(parts):
    n = len(parts)

    def body(*refs):
        ins, outs = refs[:n], refs[n:2 * n]
        send, recv = refs[2 * n:]
        x, y, c, chips = _place()
        k = 2 * x + y
        cps = []
        for a in range(n):
            for j, (px, py) in enumerate(chips):
                cp = _remote(ins[a].at[2 * px + py], outs[a].at[k], send.at[a, j], recv.at[a, j], (px, py, c))
                cp.start()
                cps.append(cp)
        for a in range(n):
            for j, (px, py) in enumerate(chips):
                landed = outs[a].at[2 * px + py]
                _remote(landed, landed, send.at[a, j], recv.at[a, j], (px, py, c)).wait_recv()
        for cp in cps:
            cp.wait_send()

    return pl.pallas_call(
        body, name="scatter_chips",
        in_specs=[_HBM] * n, out_specs=[_HBM] * n,
        out_shape=[jax.ShapeDtypeStruct(p.shape, p.dtype) for p in parts],
        scratch_shapes=[pltpu.SemaphoreType.DMA((n, 3)), pltpu.SemaphoreType.DMA((n, 3))],
    )(*parts)


def _join_halves(fulls):
    n = len(fulls)

    def body(*refs):
        ins, outs = refs[:n], refs[n:2 * n]
        send, recv = refs[2 * n:]
        x, y, c, _ = _place()
        sib = (x, y, 1 - c)
        cps = []
        for a in range(n):
            h = ins[a].shape[0] // 2
            mine = pl.ds(c * h, h)
            cp = _remote(ins[a].at[mine], outs[a].at[mine], send.at[a], recv.at[a], sib)
            cp.start()
            cps.append(cp)
        for a in range(n):
            h = ins[a].shape[0] // 2
            theirs = outs[a].at[pl.ds((1 - c) * h, h)]
            _remote(theirs, theirs, send.at[a], recv.at[a], sib).wait_recv()
        for cp in cps:
            cp.wait_send()

    return pl.pallas_call(
        body, name="join_halves",
        in_specs=[_HBM] * n, out_specs=[_HBM] * n,
        out_shape=[jax.ShapeDtypeStruct(p.shape, p.dtype) for p in fulls],
        input_output_aliases={a: a for a in range(n)},
        scratch_shapes=[pltpu.SemaphoreType.DMA((n,)), pltpu.SemaphoreType.DMA((n,))],
    )(*fulls)


def _allreduce_small(buf):
    rows = buf.shape[0]

    def body(in_ref, out_ref, land_ref, send, recv):
        x, y, c, _ = _place()
        me = 4 * x + 2 * y + c
        land_ref[me] = in_ref[...]
        peers = []
        for r in range(1, N_DEV):
            peers.append((1 - x if r & 4 else x, 1 - y if r & 2 else y, 1 - c if r & 1 else c))
        cps = []
        for r, peer in enumerate(peers):
            cp = _remote(in_ref, land_ref.at[me], send.at[r], recv.at[r], peer)
            cp.start()
            cps.append(cp)
        for r, (px, py, pc) in enumerate(peers):
            landed = land_ref.at[4 * px + 2 * py + pc]
            _remote(landed, landed, send.at[r], recv.at[r], (px, py, pc)).wait_recv()
        for cp in cps:
            cp.wait_send()
        acc = land_ref[0]
        for d in range(1, N_DEV):
            acc = acc + land_ref[d]
        out_ref[...] = acc

    return pl.pallas_call(
        body, name="allreduce_small",
        in_specs=[_VMEM], out_specs=_VMEM,
        out_shape=jax.ShapeDtypeStruct(buf.shape, F32),
        scratch_shapes=[pltpu.VMEM((N_DEV, rows, LANES), F32), pltpu.SemaphoreType.DMA((N_DEV - 1,)),
                        pltpu.SemaphoreType.DMA((N_DEV - 1,))],
        compiler_params=pltpu.CompilerParams(vmem_limit_bytes=_VMEM_LIMIT),
    )(buf)


def _div_tile(n, cap, mult):
    best = None
    for d in range(mult, min(n, cap) + 1, mult):
        if n % d == 0:
            best = d
    return best if best is not None else n


def _add_sibling(name, grad, got, c):
    nk, h, r, cc = got.shape
    tr = _div_tile(r, 512, 16)

    def body(c_ref, g_ref, o_ref, s_ref):
        s_ref[...] = (g_ref[...].astype(F32) + o_ref[...].astype(F32)).astype(s_ref.dtype)

    blk = (None, None, tr, cc)
    return pl.pallas_call(
        body, name=name,
        grid_spec=pltpu.PrefetchScalarGridSpec(
            num_scalar_prefetch=1, grid=(nk, h, r // tr),
            in_specs=[pl.BlockSpec(blk, lambda i, l, q, c_ref: (i, c_ref[0] * h + l, q, 0)),
                      pl.BlockSpec(blk, lambda i, l, q, c_ref: (i, l, q, 0))],
            out_specs=pl.BlockSpec(blk, lambda i, l, q, c_ref: (i, l, q, 0))),
        out_shape=jax.ShapeDtypeStruct(got.shape, got.dtype),
        compiler_params=_params("parallel", "parallel", "parallel"),
    )(c, grad, got)


def _add_chips(name, mine, got, place):
    nk, h, r, cc = got.shape
    tr = _div_tile(r, 512, 16)

    def body(*refs):
        acc = refs[1][...].astype(F32)
        for q in range(1, nk):
            acc = acc + refs[1 + q][...].astype(F32)
        refs[1 + nk][...] = acc

    def part(q):
        return pl.BlockSpec((None, None, tr, cc), lambda l, i, p_ref: ((p_ref[0] + q) % nk, l, i, 0))

    return pl.pallas_call(
        body, name=name,
        grid_spec=pltpu.PrefetchScalarGridSpec(
            num_scalar_prefetch=1, grid=(h, r // tr),
            in_specs=[part(q) for q in range(nk)],
            out_specs=pl.BlockSpec((None, tr, cc), lambda l, i, p_ref: (p_ref[1] * h + l, i, 0))),
        out_shape=jax.ShapeDtypeStruct((2 * h, r, cc), F32),
        compiler_params=_params("parallel", "parallel"),
    )(place, mine, *([got] * (nk - 1)))


def _adamw(name, w, g, m, v):
    shape = w.shape
    cols = shape[-1]
    rows = w.size // cols
    tr = rows if rows * cols <= 2 ** 18 else _div_tile(rows, max(8, 2 ** 18 // cols), 8)
    c1 = 1.0 - ADAM_B1 ** ADAM_STEP
    c2 = 1.0 - ADAM_B2 ** ADAM_STEP

    def body(w_ref, g_ref, m_ref, v_ref, d_ref, mo_ref, vo_ref):
        gg = g_ref[...]
        mn = ADAM_B1 * m_ref[...] + (1.0 - ADAM_B1) * gg
        vn = ADAM_B2 * v_ref[...] + (1.0 - ADAM_B2) * (gg * gg)
        d_ref[...] = -ADAM_LR * ((mn / c1) / (jnp.sqrt(vn / c2) + ADAM_EPS) + ADAM_WD * w_ref[...])
        mo_ref[...] = mn
        vo_ref[...] = vn

    blk = pl.BlockSpec((tr, cols), lambda i: (i, 0))
    sds = jax.ShapeDtypeStruct((rows, cols), F32)
    outs = pl.pallas_call(
        body, name=name, grid=(rows // tr,),
        in_specs=[blk] * 4, out_specs=(blk,) * 3, out_shape=(sds,) * 3,
        compiler_params=_params("parallel"),
    )(*[a.reshape(rows, cols) for a in (w, g, m, v)])
    return [o.reshape(shape) for o in outs]


_WEIGHTS = ["ffn1_norm", "ffn1_w_in", "ffn1_w_out", "mix_norm", "ffn2_norm", "ffn2_w_in", "ffn2_w_out",
            "ev_w_in", "ev_conv_w", "ev_q_norm", "ev_k_norm", "ev_w_out", "od_w_in", "od_pool_w",
            "od_pool_scale", "od_sgu_norm", "od_sgu_w", "od_sgu_b", "od_w_out", "final_norm"]
_BIG = ["ffn1_w_in", "ffn1_w_out", "ffn2_w_in", "ffn2_w_out", "ev_w_in", "ev_w_out", "od_w_in", "od_w_out"]
_COL_SHARDED = ("ev_w_in", "od_w_in")
_SMALL_SHARDED = ["ev_conv_w", "od_pool_scale", "od_sgu_norm"]


def _pad_rows(a, mult=8):
    pad = (-a.shape[0]) % mult
    return a if pad == 0 else jnp.concatenate([a, jnp.zeros((pad,) + a.shape[1:], a.dtype)], axis=0)


def _join_cols(g):
    return g.transpose(1, 0, 2).reshape(g.shape[1], N_CHIPS * g.shape[2])


def _split_cols(w):
    return w.reshape(w.shape[0], N_CHIPS, w.shape[1] // N_CHIPS).transpose(1, 0, 2)


def kernel(x, ffn1_norm, ffn1_w_in, ffn1_w_out, mix_norm, ffn2_norm, ffn2_w_in, ffn2_w_out, ev_w_in, ev_conv_w,
           ev_q_norm, ev_k_norm, ev_w_out, od_w_in, od_pool_w, od_pool_scale, od_sgu_norm, od_sgu_w, od_sgu_b,
           od_w_out, final_norm, loss_target, m_ffn1_norm, m_ffn1_w_in, m_ffn1_w_out, m_mix_norm, m_ffn2_norm,
           m_ffn2_w_in, m_ffn2_w_out, m_ev_w_in, m_ev_conv_w, m_ev_q_norm, m_ev_k_norm, m_ev_w_out, m_od_w_in,
           m_od_pool_w, m_od_pool_scale, m_od_sgu_norm, m_od_sgu_w, m_od_sgu_b, m_od_w_out, m_final_norm, v_ffn1_norm,
           v_ffn1_w_in, v_ffn1_w_out, v_mix_norm, v_ffn2_norm, v_ffn2_w_in, v_ffn2_w_out, v_ev_w_in, v_ev_conv_w,
           v_ev_q_norm, v_ev_k_norm, v_ev_w_out, v_od_w_in, v_od_pool_w, v_od_pool_scale, v_od_sgu_norm, v_od_sgu_w,
           v_od_sgu_b, v_od_w_out, v_final_norm):
    return _step(x, ffn1_norm, ffn1_w_in, ffn1_w_out, mix_norm, ffn2_norm, ffn2_w_in, ffn2_w_out, ev_w_in, ev_conv_w,
                 ev_q_norm, ev_k_norm, ev_w_out, od_w_in, od_pool_w, od_pool_scale, od_sgu_norm, od_sgu_w, od_sgu_b,
                 od_w_out, final_norm, loss_target, m_ffn1_norm, m_ffn1_w_in, m_ffn1_w_out, m_mix_norm, m_ffn2_norm,
                 m_ffn2_w_in, m_ffn2_w_out, m_ev_w_in, m_ev_conv_w, m_ev_q_norm, m_ev_k_norm, m_ev_w_out, m_od_w_in,
                 m_od_pool_w, m_od_pool_scale, m_od_sgu_norm, m_od_sgu_w, m_od_sgu_b, m_od_w_out, m_final_norm,
                 v_ffn1_norm, v_ffn1_w_in, v_ffn1_w_out, v_mix_norm, v_ffn2_norm, v_ffn2_w_in, v_ffn2_w_out,
                 v_ev_w_in, v_ev_conv_w, v_ev_q_norm, v_ev_k_norm, v_ev_w_out, v_od_w_in, v_od_pool_w,
                 v_od_pool_scale, v_od_sgu_norm, v_od_sgu_w, v_od_sgu_b, v_od_w_out, v_final_norm)


def _step(*args):
    nw = len(_WEIGHTS)
    x = args[0]
    w = dict(zip(_WEIGHTS, args[1:1 + nw]))
    target = args[1 + nw]
    m = dict(zip(_WEIGHTS, args[2 + nw:2 + 2 * nw]))
    v = dict(zip(_WEIGHTS, args[2 + 2 * nw:2 + 3 * nw]))
    depth = w["ffn1_norm"].shape[0]
    n_even, n_odd = w["ev_w_in"].shape[0], w["od_w_in"].shape[0]
    chip = 2 * lax.axis_index("x") + lax.axis_index("y")
    core = lax.axis_index("c")

    small_rows = [w["ev_conv_w"].reshape(3 * n_even, LANES), w["od_pool_scale"], w["od_sgu_norm"]]
    gathered, small = _gather_shards([w[n].astype(_ACT) for n in _BIG], _pad_rows(jnp.concatenate(small_rows, axis=0)))
    gw = dict(zip(_BIG, gathered))
    conv_w = small[:, :3 * n_even].reshape(N_CHIPS, n_even, 3, LANES).transpose(1, 2, 0, 3).reshape(n_even, 3, CONV_WIDTH)
    pool_scale = small[:, 3 * n_even:3 * n_even + n_odd].transpose(1, 0, 2).reshape(n_odd, HALF)
    sgu_norm = small[:, 3 * n_even + n_odd:3 * n_even + 2 * n_odd].transpose(1, 0, 2).reshape(n_odd, HALF)

    def rows_of(name, l):
        g = gw[name][:, l]
        return g.reshape(N_CHIPS * g.shape[1], g.shape[2])

    full = dict(ffn1=[], ffn2=[], mix=[], final_norm=w["final_norm"])
    for l in range(depth):
        for nm in ("ffn1", "ffn2"):
            full[nm].append(dict(norm=w[nm + "_norm"][l], w_in4=gw[nm + "_w_in"][:, l], w_out=rows_of(nm + "_w_out", l)))
        j = l // 2
        if l % 2 == 0:
            full["mix"].append(dict(norm=w["mix_norm"][l], w_in=_join_cols(gw["ev_w_in"][:, j]), conv_w=conv_w[j],
                                    q_gain=w["ev_q_norm"][j], k_gain=w["ev_k_norm"][j], w_out=rows_of("ev_w_out", j)))
        else:
            full["mix"].append(dict(norm=w["mix_norm"][l], w_in=_join_cols(gw["od_w_in"][:, j]), pool_w=w["od_pool_w"][j],
                                    pool_scale=pool_scale[j], sgu_norm=sgu_norm[j], sgu_w=w["od_sgu_w"][j],
                                    sgu_b=w["od_sgu_b"][j], w_out=rows_of("od_w_out", j)))

    loss_part, grad_x, g = _local_step(x, target, full)
    loss = lax.psum(loss_part, ("x", "y", "c"))

    def by_chip(dw):
        return dw.reshape(N_CHIPS, dw.shape[0] // N_CHIPS, dw.shape[1])

    ev = [l for l in range(depth) if l % 2 == 0]
    od = [l for l in range(depth) if l % 2 == 1]
    local = {
        "ffn1_w_in": [g["ffn1"][l]["w_in4"] for l in range(depth)],
        "ffn1_w_out": [by_chip(g["ffn1"][l]["w_out"]) for l in range(depth)],
        "ffn2_w_in": [g["ffn2"][l]["w_in4"] for l in range(depth)],
        "ffn2_w_out": [by_chip(g["ffn2"][l]["w_out"]) for l in range(depth)],
        "ev_w_in": [_split_cols(g["mix"][l]["w_in"]) for l in ev],
        "ev_w_out": [by_chip(g["mix"][l]["w_out"]) for l in ev],
        "od_w_in": [_split_cols(g["mix"][l]["w_in"]) for l in od],
        "od_w_out": [by_chip(g["mix"][l]["w_out"]) for l in od],
    }
    stacked = [jnp.stack(local[n], axis=1) for n in _BIG]
    from_sibling = _swap_halves(stacked)
    core_arr = core.astype(jnp.int32).reshape(1)
    halves = [_add_sibling("add_sibling_" + n, s, o, core_arr) for n, s, o in zip(_BIG, stacked, from_sibling)]
    from_chips = _scatter_chips(halves)
    place = jnp.stack([chip, core]).astype(jnp.int32)
    reduced = [_add_chips("add_chips_" + n, s, o, place) for n, s, o in zip(_BIG, halves, from_chips)]
    grads = dict(zip(_BIG, _join_halves(reduced)))

    small_grads = {
        "ffn1_norm": jnp.stack([g["ffn1"][l]["norm"] for l in range(depth)]),
        "mix_norm": jnp.stack([g["mix"][l]["norm"] for l in range(depth)]),
        "ffn2_norm": jnp.stack([g["ffn2"][l]["norm"] for l in range(depth)]),
        "final_norm": g["final_norm"],
        "ev_conv_w": jnp.stack([g["mix"][l]["conv_w"] for l in ev]),
        "ev_q_norm": jnp.stack([g["mix"][l]["q_gain"] for l in ev]),
        "ev_k_norm": jnp.stack([g["mix"][l]["k_gain"] for l in ev]),
        "od_pool_w": jnp.stack([g["mix"][l]["pool_w"] for l in od]),
        "od_pool_scale": jnp.stack([g["mix"][l]["pool_scale"] for l in od]),
        "od_sgu_norm": jnp.stack([g["mix"][l]["sgu_norm"] for l in od]),
        "od_sgu_w": jnp.stack([g["mix"][l]["sgu_w"] for l in od]),
        "od_sgu_b": jnp.stack([g["mix"][l]["sgu_b"] for l in od]),
    }
    names = list(small_grads)
    flat = jnp.concatenate([small_grads[n].reshape(-1) for n in names])
    total = flat.shape[0]
    flat = jnp.concatenate([flat, jnp.zeros((-total) % (8 * LANES), F32)])
    summed = _allreduce_small(flat.reshape(-1, LANES)).reshape(-1)
    off = 0
    for n in names:
        size = small_grads[n].size
        full_grad = summed[off:off + size].reshape(small_grads[n].shape)
        off += size
        if n in _SMALL_SHARDED:
            full_grad = lax.dynamic_slice_in_dim(full_grad, chip * LANES, LANES, axis=full_grad.ndim - 1)
        grads[n] = full_grad

    deltas, new_m, new_v = [], [], []
    for n in _WEIGHTS:
        d_n, m_n, v_n = _adamw("adamw_" + n, w[n], grads[n], m[n], v[n])
        deltas.append(d_n)
        new_m.append(m_n)
        new_v.append(v_n)
    return (loss, grad_x, *[grads[n] for n in _WEIGHTS], *deltas, *new_m, *new_v)
```

```python
import jax
import jax.numpy as jnp
from jax import lax
from jax.experimental import pallas as pl
from jax.experimental.pallas import tpu as pltpu

F32 = jnp.float32
_MXU = jnp.bfloat16
_ACT = jnp.bfloat16

D_MODEL = 1024
GRID_W = 64
HEAD_DIM = 64
N_Q_HEADS = 8
N_KV_HEADS = 2
Q_PER_KV = N_Q_HEADS // N_KV_HEADS
ATTN_WIDTH = N_Q_HEADS * HEAD_DIM
KV_WIDTH = N_KV_HEADS * HEAD_DIM
ROPE_THETA = 10000.0
CONV_WIDTH = D_MODEL // 2
POOL_RADII = (1, 2, 4, 8)
POOL_GROUP = 128
SGU_GROUP = 128
SGU_CHUNK = 128
N_GROUPS = 4
HALF = D_MODEL // 2
EPS = 1e-6
HALO = 8
LANES = 128
N_CHIPS = 4
N_DEV = 8

ADAM_LR = 0.001
ADAM_B1 = 0.9
ADAM_B2 = 0.999
ADAM_EPS = 1e-08
ADAM_WD = 0.01
ADAM_STEP = 10

_VMEM_LIMIT = 56 * 2 ** 20
_MESH = pl.DeviceIdType.MESH
_ANY = pl.BlockSpec(memory_space=pl.ANY)
_VMEM = pl.BlockSpec(memory_space=pltpu.VMEM)

_DN = {
    "nn": (((1,), (0,)), ((), ())),
    "nt": (((1,), (1,)), ((), ())),
    "tn": (((0,), (0,)), ((), ())),
}


def _params(*sem):
    return pltpu.CompilerParams(dimension_semantics=sem, vmem_limit_bytes=_VMEM_LIMIT)


def _tile(n, cap):
    best = None
    d = LANES
    while d <= min(n, cap):
        if n % d == 0:
            best = d
        d += LANES
    return best if best is not None else n


def _dot(a, b, mode="nn"):
    return lax.dot_general(a.astype(_MXU), b.astype(_MXU), _DN[mode], preferred_element_type=F32)


def _cat(*vals):
    vals = [v.astype(_MXU) for v in vals]
    return vals[0] if len(vals) == 1 else jnp.concatenate(vals, axis=1)


def _sigmoid(g):
    return 1.0 / (1.0 + jnp.exp(-g))


def _swiglu(g, u):
    g = g.astype(F32)
    return (g * _sigmoid(g)) * u.astype(F32)


_GELU_C = 0.7978845608028654


def _gelu(x):
    return 0.5 * x * (1.0 + jnp.tanh(_GELU_C * (x + 0.044715 * (x * x * x))))


def _gelu_grad(x):
    t = jnp.tanh(_GELU_C * (x + 0.044715 * (x * x * x)))
    return 0.5 * (1.0 + t) + 0.5 * x * (1.0 - t * t) * (_GELU_C * (1.0 + 3.0 * 0.044715 * (x * x)))


def _mm(name, grid, mode, a_ops, b_ops, e_ops, out_shape, out_specs, acc_shape, a_fn=_cat, b_fn=_cat, epi=None):
    ni, nj, nk = grid
    na, nb, ne = len(a_ops), len(b_ops), len(e_ops)
    multi = isinstance(out_shape, (list, tuple))
    no = len(out_shape) if multi else 1

    def body(*refs):
        a_refs = refs[:na]
        b_refs = refs[na:na + nb]
        e_refs = refs[na + nb:na + nb + ne]
        o_refs = refs[na + nb + ne:na + nb + ne + no]
        a = a_fn(*[r[...] for r in a_refs])
        b = b_fn(*[r[...] for r in b_refs])
        p = _dot(a, b, mode)

        def finish(acc):
            if epi is None:
                o_refs[0][...] = acc.astype(o_refs[0].dtype)
            else:
                epi(acc, [r[...] for r in e_refs], o_refs)

        if nk == 1:
            finish(p)
        else:
            acc_ref = refs[-1]
            k = pl.program_id(2)

            @pl.when(k == 0)
            def _():
                acc_ref[...] = p

            @pl.when(k > 0)
            def _():
                acc_ref[...] += p

            @pl.when(k == nk - 1)
            def _():
                finish(acc_ref[...])

    ops = list(a_ops) + list(b_ops) + list(e_ops)
    return pl.pallas_call(
        body, name=name, grid=grid,
        in_specs=[s for _, s in ops],
        out_specs=out_specs, out_shape=out_shape,
        scratch_shapes=[pltpu.VMEM(acc_shape, F32)] if nk > 1 else [],
        compiler_params=_params("parallel", "parallel", "arbitrary"),
    )(*[a for a, _ in ops])


def _rows(t):
    return _tile(t, 512)


def _rmsnorm_fwd(name, x, gain):
    t, d = x.shape
    tr = _rows(t)

    def body(x_ref, g_ref, h_ref):
        xf = x_ref[...]
        r = lax.rsqrt(jnp.mean(xf * xf, axis=-1, keepdims=True) + EPS)
        h_ref[...] = ((xf * r) * g_ref[...]).astype(h_ref.dtype)

    return pl.pallas_call(
        body, name=name, grid=(t // tr,),
        in_specs=[pl.BlockSpec((tr, d), lambda i: (i, 0)), pl.BlockSpec((1, d), lambda i: (0, 0))],
        out_specs=pl.BlockSpec((tr, d), lambda i: (i, 0)),
        out_shape=jax.ShapeDtypeStruct((t, d), _ACT),
        compiler_params=_params("parallel"),
    )(x, gain.reshape(1, d))


def _rmsnorm_bwd(name, dh, x, gain, dres):
    t, d = x.shape
    tr = _rows(t)

    def body(dh_ref, x_ref, g_ref, dres_ref, dx_ref, dg_ref):
        i = pl.program_id(0)
        xf = x_ref[...]
        r = lax.rsqrt(jnp.mean(xf * xf, axis=-1, keepdims=True) + EPS)
        xhat = xf * r
        dy = dh_ref[...].astype(F32)
        dgx = dy * g_ref[...]
        m = jnp.mean(dgx * xhat, axis=-1, keepdims=True)
        dx_ref[...] = dres_ref[...] + r * (dgx - xhat * m)
        part = jnp.sum(dy * xhat, axis=0, keepdims=True)

        @pl.when(i == 0)
        def _():
            dg_ref[...] = part

        @pl.when(i > 0)
        def _():
            dg_ref[...] += part

    row = pl.BlockSpec((tr, d), lambda i: (i, 0))
    vec = pl.BlockSpec((1, d), lambda i: (0, 0))
    dx, dg = pl.pallas_call(
        body, name=name, grid=(t // tr,),
        in_specs=[row, row, vec, row],
        out_specs=(row, vec),
        out_shape=(jax.ShapeDtypeStruct((t, d), F32), jax.ShapeDtypeStruct((1, d), F32)),
        compiler_params=_params("arbitrary"),
    )(dh, x, gain.reshape(1, d), dres)
    return dx, dg.reshape(d)


def _final_loss(name, x, gain, target):
    t, d = x.shape
    tr = _rows(t)

    def body(x_ref, g_ref, t_ref, dx_ref, dg_ref, loss_ref):
        i = pl.program_id(0)
        xf = x_ref[...]
        r = lax.rsqrt(jnp.mean(xf * xf, axis=-1, keepdims=True) + EPS)
        xhat = xf * r
        g = g_ref[...]
        err = xhat * g - t_ref[...]
        lpart = 0.5 * jnp.sum(jnp.mean(err * err, axis=-1, keepdims=True), axis=0, keepdims=True)
        dy = err * (1.0 / d)
        dgx = dy * g
        m = jnp.mean(dgx * xhat, axis=-1, keepdims=True)
        dx_ref[...] = r * (dgx - xhat * m)
        part = jnp.sum(dy * xhat, axis=0, keepdims=True)
        lrow = jnp.broadcast_to(lpart, (1, LANES))

        @pl.when(i == 0)
        def _():
            dg_ref[...] = part
            loss_ref[...] = lrow

        @pl.when(i > 0)
        def _():
            dg_ref[...] += part
            loss_ref[...] += lrow

    row = pl.BlockSpec((tr, d), lambda i: (i, 0))
    vec = pl.BlockSpec((1, d), lambda i: (0, 0))
    dx, dg, loss = pl.pallas_call(
        body, name=name, grid=(t // tr,),
        in_specs=[row, vec, row],
        out_specs=(row, vec, pl.BlockSpec((1, LANES), lambda i: (0, 0))),
        out_shape=(jax.ShapeDtypeStruct((t, d), F32), jax.ShapeDtypeStruct((1, d), F32),
                   jax.ShapeDtypeStruct((1, LANES), F32)),
        compiler_params=_params("arbitrary"),
    )(x, gain.reshape(1, d), target)
    return loss[0, 0], dx, dg.reshape(d)


def _ffn_fwd(tag, x, gain, w_in4, w_out):
    t, d = x.shape
    fs = w_in4.shape[2]
    f = 2 * fs
    tm = _tile(t, 512)
    h = _rmsnorm_fwd(tag + "_norm", x, gain)
    gu = _mm(
        tag + "_in", (t // tm, N_CHIPS, 1), "nn",
        [(h, pl.BlockSpec((tm, d), lambda i, j, k: (i, 0)))],
        [(w_in4, pl.BlockSpec((None, d, fs), lambda i, j, k: (j, 0, 0)))], [],
        jax.ShapeDtypeStruct((2, t, f), _ACT),
        pl.BlockSpec((None, tm, fs), lambda i, j, k: (j // 2, i, j % 2)), None)
    tm2 = _tile(t, 256)

    def epi(acc, e, o):
        o[0][...] = e[0] + 0.5 * acc

    x_out = _mm(
        tag + "_out", (t // tm2, 1, 1), "nn",
        [(gu, pl.BlockSpec((None, tm2, f), lambda i, j, k: (0, i, 0))),
         (gu, pl.BlockSpec((None, tm2, f), lambda i, j, k: (1, i, 0)))],
        [(w_out, pl.BlockSpec((f, d), lambda i, j, k: (0, 0)))],
        [(x, pl.BlockSpec((tm2, d), lambda i, j, k: (i, 0)))],
        jax.ShapeDtypeStruct((t, d), F32),
        pl.BlockSpec((tm2, d), lambda i, j, k: (i, 0)), None,
        a_fn=_swiglu, epi=epi)
    return x_out, (x, h, gu)


def _ffn_bwd(tag, dxo, saved, gain, w_in4, w_out):
    x, h, gu = saved
    t, d = x.shape
    fs = w_in4.shape[2]
    f = 2 * fs
    tm = _tile(t, 512)
    tk = _tile(t, 512)

    def epi_act(acc, e, o):
        g = e[0].astype(F32)
        u = e[1].astype(F32)
        da = 0.5 * acc
        sig = _sigmoid(g)
        o[0][0] = (da * u * (sig * (1.0 + g * (1.0 - sig)))).astype(o[0].dtype)
        o[0][1] = (da * (g * sig)).astype(o[0].dtype)

    dgu = _mm(
        tag + "_dact", (t // tm, 2, 1), "nt",
        [(dxo, pl.BlockSpec((tm, d), lambda i, j, k: (i, 0)))],
        [(w_out, pl.BlockSpec((fs, d), lambda i, j, k: (j, 0)))],
        [(gu, pl.BlockSpec((None, tm, fs), lambda i, j, k: (0, i, j))),
         (gu, pl.BlockSpec((None, tm, fs), lambda i, j, k: (1, i, j)))],
        jax.ShapeDtypeStruct((2, t, f), _ACT),
        pl.BlockSpec((2, tm, fs), lambda i, j, k: (0, i, j)), None, epi=epi_act)

    def epi_half(acc, e, o):
        o[0][...] = (0.5 * acc).astype(o[0].dtype)

    dw_out = _mm(
        tag + "_dwout", (2, 1, t // tk), "tn",
        [(gu, pl.BlockSpec((None, tk, fs), lambda i, j, k: (0, k, i))),
         (gu, pl.BlockSpec((None, tk, fs), lambda i, j, k: (1, k, i)))],
        [(dxo, pl.BlockSpec((tk, d), lambda i, j, k: (k, 0)))], [],
        jax.ShapeDtypeStruct((f, d), _ACT),
        pl.BlockSpec((fs, d), lambda i, j, k: (i, 0)), (fs, d),
        a_fn=_swiglu, epi=epi_half)
    dh = _mm(
        tag + "_dh", (t // tm, 1, N_CHIPS), "nt",
        [(dgu, pl.BlockSpec((None, tm, fs), lambda i, j, k: (k // 2, i, k % 2)))],
        [(w_in4, pl.BlockSpec((None, d, fs), lambda i, j, k: (k, 0, 0)))], [],
        jax.ShapeDtypeStruct((t, d), F32),
        pl.BlockSpec((tm, d), lambda i, j, k: (i, 0)), (tm, d))
    dw_in4 = _mm(
        tag + "_dwin", (1, N_CHIPS, t // tk), "tn",
        [(h, pl.BlockSpec((tk, d), lambda i, j, k: (k, 0)))],
        [(dgu, pl.BlockSpec((None, tk, fs), lambda i, j, k: (j // 2, k, j % 2)))], [],
        jax.ShapeDtypeStruct((N_CHIPS, d, fs), _ACT),
        pl.BlockSpec((None, d, fs), lambda i, j, k: (j, 0, 0)), (d, fs))
    dx, dgain = _rmsnorm_bwd(tag + "_dnorm", dh, x, gain, dxo)
    return dx, dgain, dw_in4, dw_out


def _proj_in(tag, h, w_in):
    t, d = h.shape
    n = w_in.shape[1]
    tm = _tile(t, 512)
    return _mm(
        tag + "_in", (t // tm, 1, 1), "nn",
        [(h, pl.BlockSpec((tm, d), lambda i, j, k: (i, 0)))],
        [(w_in, pl.BlockSpec((d, n), lambda i, j, k: (0, 0)))], [],
        jax.ShapeDtypeStruct((t, n), _ACT),
        pl.BlockSpec((tm, n), lambda i, j, k: (i, 0)), None)


def _proj_out(tag, x, parts, w_out):
    t, d = x.shape
    tm = _tile(t, 512)

    def epi(acc, e, o):
        o[0][...] = e[0] + acc

    return _mm(
        tag + "_out", (t // tm, 1, 1), "nn",
        [(p, pl.BlockSpec((tm, p.shape[1]), lambda i, j, k: (i, 0))) for p in parts],
        [(w_out, pl.BlockSpec(w_out.shape, lambda i, j, k: (0, 0)))],
        [(x, pl.BlockSpec((tm, d), lambda i, j, k: (i, 0)))],
        jax.ShapeDtypeStruct((t, d), F32),
        pl.BlockSpec((tm, d), lambda i, j, k: (i, 0)), None, epi=epi)


def _proj_out_bwd(tag, dxo, parts, w_out):
    t, d = dxo.shape
    mix = w_out.shape[0]
    tm = _tile(t, 512)
    tk = _tile(t, 512)
    d_mix = _mm(
        tag + "_dmix", (t // tm, 1, 1), "nt",
        [(dxo, pl.BlockSpec((tm, d), lambda i, j, k: (i, 0)))],
        [(w_out, pl.BlockSpec((mix, d), lambda i, j, k: (0, 0)))], [],
        jax.ShapeDtypeStruct((t, mix), F32),
        pl.BlockSpec((tm, mix), lambda i, j, k: (i, 0)), None)
    dw_out = _mm(
        tag + "_dwout", (1, 1, t // tk), "tn",
        [(p, pl.BlockSpec((tk, p.shape[1]), lambda i, j, k: (k, 0))) for p in parts],
        [(dxo, pl.BlockSpec((tk, d), lambda i, j, k: (k, 0)))], [],
        jax.ShapeDtypeStruct((mix, d), _ACT),
        pl.BlockSpec((mix, d), lambda i, j, k: (0, 0)), (mix, d))
    return d_mix, dw_out


def _proj_in_bwd(tag, h, dparts, w_in):
    t, d = h.shape
    n = w_in.shape[1]
    tm = _tile(t, 512)
    tk = _tile(t, 512)
    dh = _mm(
        tag + "_dh", (t // tm, 1, 1), "nt",
        [(p, pl.BlockSpec((tm, p.shape[1]), lambda i, j, k: (i, 0))) for p in dparts],
        [(w_in, pl.BlockSpec((d, n), lambda i, j, k: (0, 0)))], [],
        jax.ShapeDtypeStruct((t, d), F32),
        pl.BlockSpec((tm, d), lambda i, j, k: (i, 0)), None)
    dw_in = _mm(
        tag + "_dwin", (1, 1, t // tk), "tn",
        [(h, pl.BlockSpec((tk, d), lambda i, j, k: (k, 0)))],
        [(p, pl.BlockSpec((tk, p.shape[1]), lambda i, j, k: (k, 0))) for p in dparts], [],
        jax.ShapeDtypeStruct((d, n), _ACT),
        pl.BlockSpec((d, n), lambda i, j, k: (0, 0)), (d, n))
    return dh, dw_in


def _shifted(pad_ref, val, s):
    pad_ref[pl.ds(HALO, s), :] = val
    return pad_ref[pl.ds(HALO - 1, s), :], pad_ref[pl.ds(HALO + 1, s), :]


def _zero_halo(pad_ref, s):
    z = jnp.zeros((HALO, pad_ref.shape[1]), F32)
    pad_ref[pl.ds(0, HALO), :] = z
    pad_ref[pl.ds(HALO + s, HALO), :] = z


def _conv_fwd(tag, proj, conv_w, nb, s):
    t = proj.shape[0]
    ncb = CONV_WIDTH // LANES

    def body(gb_ref, gc_ref, hc_ref, w_ref, a_ref, pad_ref):
        _zero_halo(pad_ref, s)
        cg = gc_ref[...].astype(F32) * hc_ref[...].astype(F32)
        prev, nxt = _shifted(pad_ref, cg, s)
        w = w_ref[...]
        conv = prev * w[0:1, :] + cg * w[1:2, :] + nxt * w[2:3, :]
        a_ref[...] = (gb_ref[...].astype(F32) * conv).astype(a_ref.dtype)

    def col(off):
        return pl.BlockSpec((s, LANES), lambda b, c: (b, off + c))

    return pl.pallas_call(
        body, name=tag + "_conv", grid=(nb, ncb),
        in_specs=[col(0), col(ncb), col(2 * ncb), pl.BlockSpec((3, LANES), lambda b, c: (0, c))],
        out_specs=col(0),
        out_shape=jax.ShapeDtypeStruct((t, CONV_WIDTH), _ACT),
        scratch_shapes=[pltpu.VMEM((s + 2 * HALO, LANES), F32)],
        compiler_params=_params("parallel", "parallel"),
    )(proj, proj, proj, conv_w)


def _conv_bwd(tag, proj, conv_w, d_mix, nb, s):
    t = proj.shape[0]
    ncb = CONV_WIDTH // LANES

    def body(gb_ref, gc_ref, hc_ref, w_ref, da_ref, dgb_ref, dgc_ref, dhc_ref, dw_ref, pad_ref):
        b = pl.program_id(1)
        _zero_halo(pad_ref, s)
        gb = gb_ref[...].astype(F32)
        gc = gc_ref[...].astype(F32)
        hc = hc_ref[...].astype(F32)
        w = w_ref[...]
        da = da_ref[...]
        cg = gc * hc
        prev, nxt = _shifted(pad_ref, cg, s)
        conv = prev * w[0:1, :] + cg * w[1:2, :] + nxt * w[2:3, :]
        dgb_ref[...] = (da * conv).astype(dgb_ref.dtype)
        dconv = da * gb
        dw = jnp.concatenate([
            jnp.sum(dconv * prev, axis=0, keepdims=True),
            jnp.sum(dconv * cg, axis=0, keepdims=True),
            jnp.sum(dconv * nxt, axis=0, keepdims=True)], axis=0)
        dprev, dnxt = _shifted(pad_ref, dconv, s)
        dcg = dnxt * w[0:1, :] + dconv * w[1:2, :] + dprev * w[2:3, :]
        dgc_ref[...] = (dcg * hc).astype(dgc_ref.dtype)
        dhc_ref[...] = (dcg * gc).astype(dhc_ref.dtype)

        @pl.when(b == 0)
        def _():
            dw_ref[...] = dw

        @pl.when(b > 0)
        def _():
            dw_ref[...] += dw

    def col(off):
        return pl.BlockSpec((s, LANES), lambda c, b: (b, off + c))

    wspec = pl.BlockSpec((3, LANES), lambda c, b: (0, c))
    act = jax.ShapeDtypeStruct((t, CONV_WIDTH), _ACT)
    return pl.pallas_call(
        body, name=tag + "_dconv", grid=(ncb, nb),
        in_specs=[col(0), col(ncb), col(2 * ncb), wspec, col(0)],
        out_specs=(col(0), col(0), col(0), wspec),
        out_shape=(act, act, act, jax.ShapeDtypeStruct((3, CONV_WIDTH), F32)),
        scratch_shapes=[pltpu.VMEM((s + 2 * HALO, LANES), F32)],
        compiler_params=_params("parallel", "arbitrary"),
    )(proj, proj, proj, conv_w, d_mix)


def _rope_tables(s):
    rows = s // GRID_W
    r_idx, c_idx = jnp.meshgrid(jnp.arange(rows), jnp.arange(GRID_W), indexing="ij")
    r_idx = r_idx.reshape(-1).astype(F32)
    c_idx = c_idx.reshape(-1).astype(F32)
    n_freq = HEAD_DIM // 4
    inv = ROPE_THETA ** (-jnp.arange(n_freq, dtype=F32) / n_freq)
    ang = jnp.concatenate([r_idx[:, None] * inv, c_idx[:, None] * inv], axis=-1)
    cos = jnp.repeat(jnp.cos(ang), 2, axis=1)
    sin = jnp.repeat(jnp.sin(ang), 2, axis=1)
    sign = jnp.where(jnp.arange(HEAD_DIM) % 2 == 0, -1.0, 1.0).astype(F32)
    return jnp.tile(cos, (1, LANES // HEAD_DIM)), jnp.tile(sin * sign, (1, LANES // HEAD_DIM))


def _head_ones():
    i = jnp.arange(LANES) // HEAD_DIM
    return (i[:, None] == i[None, :]).astype(jnp.bfloat16)


def _head_sum(v, ones):
    outs = []
    for j in range(v.shape[1] // LANES):
        c = v[:, j * LANES:(j + 1) * LANES]
        hi = c.astype(jnp.bfloat16)
        lo = (c - hi.astype(F32)).astype(jnp.bfloat16)
        outs.append(jnp.dot(hi, ones, preferred_element_type=F32) + jnp.dot(lo, ones, preferred_element_type=F32))
    return outs[0] if len(outs) == 1 else jnp.concatenate(outs, axis=1)


def _pair_swap(v):
    outs = []
    for j in range(v.shape[1] // LANES):
        c = v[:, j * LANES:(j + 1) * LANES]
        lane = lax.broadcasted_iota(jnp.int32, c.shape, 1)
        outs.append(jnp.where(lane % 2 == 0, pltpu.roll(c, LANES - 1, 1), pltpu.roll(c, 1, 1)))
    return outs[0] if len(outs) == 1 else jnp.concatenate(outs, axis=1)


def _wide(tab, width):
    return tab if width == LANES else jnp.concatenate([tab] * (width // LANES), axis=1)


def _qk_fwd(tag, proj, q_gain, k_gain, cos, sin, nb, s):
    t = proj.shape[0]
    tr = _tile(s, 512)
    ns = s // tr
    q_off = 3 * CONV_WIDTH // ATTN_WIDTH
    k_off = (3 * CONV_WIDTH + ATTN_WIDTH) // KV_WIDTH

    def body(q_ref, k_ref, qg_ref, kg_ref, cos_ref, sin_ref, ones_ref, qo_ref, ko_ref):
        ones = ones_ref[...]
        for src, g_ref, dst in ((q_ref, qg_ref, qo_ref), (k_ref, kg_ref, ko_ref)):
            v = src[...].astype(F32)
            w = v.shape[1]
            r = lax.rsqrt(_head_sum(v * v, ones) * (1.0 / HEAD_DIM) + EPS)
            vn = (v * r) * g_ref[...]
            dst[...] = (vn * _wide(cos_ref[...], w) + _pair_swap(vn) * _wide(sin_ref[...], w)).astype(dst.dtype)

    tab = pl.BlockSpec((tr, LANES), lambda i: (i % ns, 0))
    return pl.pallas_call(
        body, name=tag + "_qk", grid=(t // tr,),
        in_specs=[pl.BlockSpec((tr, ATTN_WIDTH), lambda i: (i, q_off)),
                  pl.BlockSpec((tr, KV_WIDTH), lambda i: (i, k_off)),
                  pl.BlockSpec((1, ATTN_WIDTH), lambda i: (0, 0)),
                  pl.BlockSpec((1, KV_WIDTH), lambda i: (0, 0)),
                  tab, tab, pl.BlockSpec((LANES, LANES), lambda i: (0, 0))],
        out_specs=(pl.BlockSpec((tr, ATTN_WIDTH), lambda i: (i, 0)),
                   pl.BlockSpec((tr, KV_WIDTH), lambda i: (i, 0))),
        out_shape=(jax.ShapeDtypeStruct((t, ATTN_WIDTH), _ACT), jax.ShapeDtypeStruct((t, KV_WIDTH), _ACT)),
        compiler_params=_params("parallel"),
    )(proj, proj, jnp.tile(q_gain, N_Q_HEADS).reshape(1, ATTN_WIDTH),
      jnp.tile(k_gain, N_KV_HEADS).reshape(1, KV_WIDTH), cos, sin, _head_ones())


def _qk_bwd(tag, proj, q_gain, k_gain, cos, sin, dq_rot, dk_rot, nb, s):
    t = proj.shape[0]
    tr = _tile(s, 512)
    ns = s // tr
    q_off = 3 * CONV_WIDTH // ATTN_WIDTH
    k_off = (3 * CONV_WIDTH + ATTN_WIDTH) // KV_WIDTH

    def body(q_ref, k_ref, qg_ref, kg_ref, cos_ref, sin_ref, ones_ref, dqr_ref, dkr_ref,
             dq_ref, dk_ref, dqg_ref, dkg_ref):
        i = pl.program_id(0)
        ones = ones_ref[...]
        for src, g_ref, dr_ref, dst, dg_ref in ((q_ref, qg_ref, dqr_ref, dq_ref, dqg_ref),
                                                (k_ref, kg_ref, dkr_ref, dk_ref, dkg_ref)):
            v = src[...].astype(F32)
            w = v.shape[1]
            r = lax.rsqrt(_head_sum(v * v, ones) * (1.0 / HEAD_DIM) + EPS)
            xhat = v * r
            dr = dr_ref[...]
            dvn = dr * _wide(cos_ref[...], w) + _pair_swap(dr * _wide(sin_ref[...], w))
            dgx = dvn * g_ref[...]
            m = _head_sum(dgx * xhat, ones) * (1.0 / HEAD_DIM)
            dst[...] = (r * (dgx - xhat * m)).astype(dst.dtype)
            part = jnp.sum(dvn * xhat, axis=0, keepdims=True)
            fold = part[:, 0:HEAD_DIM]
            for hh in range(1, w // HEAD_DIM):
                fold = fold + part[:, hh * HEAD_DIM:(hh + 1) * HEAD_DIM]

            @pl.when(i == 0)
            def _():
                dg_ref[...] = fold

            @pl.when(i > 0)
            def _():
                dg_ref[...] += fold

    tab = pl.BlockSpec((tr, LANES), lambda i: (i % ns, 0))
    qrow = pl.BlockSpec((tr, ATTN_WIDTH), lambda i: (i, 0))
    krow = pl.BlockSpec((tr, KV_WIDTH), lambda i: (i, 0))
    gvec = pl.BlockSpec((1, HEAD_DIM), lambda i: (0, 0))
    dq, dk, dqg, dkg = pl.pallas_call(
        body, name=tag + "_dqk", grid=(t // tr,),
        in_specs=[pl.BlockSpec((tr, ATTN_WIDTH), lambda i: (i, q_off)),
                  pl.BlockSpec((tr, KV_WIDTH), lambda i: (i, k_off)),
                  pl.BlockSpec((1, ATTN_WIDTH), lambda i: (0, 0)),
                  pl.BlockSpec((1, KV_WIDTH), lambda i: (0, 0)),
                  tab, tab, pl.BlockSpec((LANES, LANES), lambda i: (0, 0)), qrow, krow],
        out_specs=(qrow, krow, gvec, gvec),
        out_shape=(jax.ShapeDtypeStruct((t, ATTN_WIDTH), _ACT), jax.ShapeDtypeStruct((t, KV_WIDTH), _ACT),
                   jax.ShapeDtypeStruct((1, HEAD_DIM), F32), jax.ShapeDtypeStruct((1, HEAD_DIM), F32)),
        compiler_params=_params("arbitrary"),
    )(proj, proj, jnp.tile(q_gain, N_Q_HEADS).reshape(1, ATTN_WIDTH),
      jnp.tile(k_gain, N_KV_HEADS).reshape(1, KV_WIDTH), cos, sin, _head_ones(), dq_rot, dk_rot)
    return dq, dk, dqg.reshape(HEAD_DIM), dkg.reshape(HEAD_DIM)


def _head(v, h):
    return v[:, h * HEAD_DIM:(h + 1) * HEAD_DIM]


def _attn_fwd(tag, q, k, proj, nb, s):
    t = q.shape[0]
    tq = _tile(s, 256)
    nq = s // tq
    v_off = (3 * CONV_WIDTH + ATTN_WIDTH + KV_WIDTH) // KV_WIDTH
    scale = HEAD_DIM ** -0.5

    def body(q_ref, k_ref, v_ref, o_ref, lse_ref):
        qv = q_ref[...]
        kv = k_ref[...]
        vv = v_ref[...]
        for h in range(N_Q_HEADS):
            j = h // Q_PER_KV
            sc = _dot(_head(qv, h), _head(kv, j), "nt") * scale
            m = jnp.max(sc, axis=-1, keepdims=True)
            e = jnp.exp(sc - m)
            l = jnp.sum(e, axis=-1, keepdims=True)
            o = _dot(e, _head(vv, j)) * (1.0 / l)
            o_ref[:, h * HEAD_DIM:(h + 1) * HEAD_DIM] = o.astype(o_ref.dtype)
            lse_ref[:, h:h + 1] = m + jnp.log(l)

    return pl.pallas_call(
        body, name=tag + "_attn", grid=(nb, nq),
        in_specs=[pl.BlockSpec((tq, ATTN_WIDTH), lambda b, i: (b * nq + i, 0)),
                  pl.BlockSpec((s, KV_WIDTH), lambda b, i: (b, 0)),
                  pl.BlockSpec((s, KV_WIDTH), lambda b, i: (b, v_off))],
        out_specs=(pl.BlockSpec((tq, ATTN_WIDTH), lambda b, i: (b * nq + i, 0)),
                   pl.BlockSpec((tq, N_Q_HEADS), lambda b, i: (b * nq + i, 0))),
        out_shape=(jax.ShapeDtypeStruct((t, ATTN_WIDTH), _ACT), jax.ShapeDtypeStruct((t, N_Q_HEADS), F32)),
        compiler_params=_params("parallel", "parallel"),
    )(q, k, proj)


def _attn_bwd(tag, q, k, proj, o, lse, d_mix, nb, s):
    t = q.shape[0]
    tq = _tile(s, 256)
    nq = s // tq
    v_off = (3 * CONV_WIDTH + ATTN_WIDTH + KV_WIDTH) // KV_WIDTH
    scale = HEAD_DIM ** -0.5

    def body(q_ref, k_ref, v_ref, o_ref, lse_ref, do_ref, dq_ref, dk_ref, dv_ref):
        i = pl.program_id(1)

        @pl.when(i == 0)
        def _():
            dk_ref[...] = jnp.zeros_like(dk_ref)
            dv_ref[...] = jnp.zeros_like(dv_ref)

        qv = q_ref[...]
        kv = k_ref[...]
        vv = v_ref[...]
        ov = o_ref[...].astype(F32)
        dov = do_ref[...]
        lse = lse_ref[...]
        for h in range(N_Q_HEADS):
            j = h // Q_PER_KV
            cols = slice(j * HEAD_DIM, (j + 1) * HEAD_DIM)
            qh = _head(qv, h)
            kj = _head(kv, j)
            doh = _head(dov, h)
            sc = _dot(qh, kj, "nt") * scale
            p = jnp.exp(sc - lse[:, h:h + 1])
            dp = _dot(doh, _head(vv, j), "nt")
            delta = jnp.sum(doh * _head(ov, h), axis=-1, keepdims=True)
            ds = p * (dp - delta) * scale
            dv_ref[:, cols] += _dot(p, doh, "tn")
            dk_ref[:, cols] += _dot(ds, qh, "tn")
            dq_ref[:, h * HEAD_DIM:(h + 1) * HEAD_DIM] = _dot(ds, kj)

    qrow = pl.BlockSpec((tq, ATTN_WIDTH), lambda b, i: (b * nq + i, 0))
    kvrow = pl.BlockSpec((s, KV_WIDTH), lambda b, i: (b, 0))
    return pl.pallas_call(
        body, name=tag + "_dattn", grid=(nb, nq),
        in_specs=[qrow, kvrow, pl.BlockSpec((s, KV_WIDTH), lambda b, i: (b, v_off)), qrow,
                  pl.BlockSpec((tq, N_Q_HEADS), lambda b, i: (b * nq + i, 0)),
                  pl.BlockSpec((tq, ATTN_WIDTH), lambda b, i: (b * nq + i, 1))],
        out_specs=(qrow, kvrow, kvrow),
        out_shape=(jax.ShapeDtypeStruct((t, ATTN_WIDTH), F32), jax.ShapeDtypeStruct((t, KV_WIDTH), F32),
                   jax.ShapeDtypeStruct((t, KV_WIDTH), F32)),
        compiler_params=_params("parallel", "arbitrary"),
    )(q, k, proj, o, lse, d_mix)


def _even_fwd(tag, x, p, cos, sin, nb, s):
    h = _rmsnorm_fwd(tag + "_norm", x, p["norm"])
    proj = _proj_in(tag, h, p["w_in"])
    a = _conv_fwd(tag, proj, p["conv_w"], nb, s)
    q, k = _qk_fwd(tag, proj, p["q_gain"], p["k_gain"], cos, sin, nb, s)
    o, lse = _attn_fwd(tag, q, k, proj, nb, s)
    x_out = _proj_out(tag, x, [a, o], p["w_out"])
    return x_out, (x, h, proj, a, q, k, o, lse)


def _even_bwd(tag, dxo, saved, p, cos, sin, nb, s):
    x, h, proj, a, q, k, o, lse = saved
    d_mix, dw_out = _proj_out_bwd(tag, dxo, [a, o], p["w_out"])
    dgb, dgc, dhc, dconv_w = _conv_bwd(tag, proj, p["conv_w"], d_mix, nb, s)
    dq_rot, dk_rot, dv = _attn_bwd(tag, q, k, proj, o, lse, d_mix, nb, s)
    dq, dk, dq_gain, dk_gain = _qk_bwd(tag, proj, p["q_gain"], p["k_gain"], cos, sin, dq_rot, dk_rot, nb, s)
    dh, dw_in = _proj_in_bwd(tag, h, [dgb, dgc, dhc, dq, dk, dv], p["w_in"])
    dx, dnorm = _rmsnorm_bwd(tag + "_dnorm", dh, x, p["norm"], dxo)
    grads = dict(norm=dnorm, w_in=dw_in, w_out=dw_out, conv_w=dconv_w, q_gain=dq_gain, k_gain=dk_gain)
    return dx, grads


def _window(pad_ref, val, r, s):
    pad_ref[pl.ds(HALO, s), :] = val
    acc = val
    for d in range(1, r + 1):
        acc = acc + pad_ref[pl.ds(HALO - d, s), :] + pad_ref[pl.ds(HALO + d, s), :]
    return acc


def _count(r, s):
    t = lax.broadcasted_iota(jnp.int32, (s, 1), 0)
    return (jnp.minimum(t + r, s - 1) - jnp.maximum(t - r, 0) + 1).astype(F32)


def _sgu_chunk(u_ref, v_ref, norm, ws_ref, bt, rows):
    uu = u_ref[rows, :].astype(F32)
    vv = v_ref[rows, :].astype(F32)
    gu = _gelu(uu)
    gv = _gelu(vv)
    r = lax.rsqrt(jnp.mean(gv * gv, axis=-1, keepdims=True) + EPS)
    xhat = gv * r
    vn = xhat * norm
    mixed = []
    for g in range(N_GROUPS):
        cols = slice(g * SGU_GROUP, (g + 1) * SGU_GROUP)
        mixed.append(_dot(ws_ref[g], vn[:, cols]) + bt[:, g:g + 1])
    return uu, vv, gu, r, xhat, vn, mixed


def _odd_core_fwd(tag, proj, p, nb, s):
    t = proj.shape[0]
    nchunk = s // SGU_CHUNK

    def body(p_ref, u_ref, v_ref, pw_ref, ps_ref, sn_ref, ws_ref, bt_ref, mix_ref, pad_ref):
        _zero_halo(pad_ref, s)
        for g, r in enumerate(POOL_RADII):
            cols = slice(g * POOL_GROUP, (g + 1) * POOL_GROUP)
            pg = p_ref[:, cols].astype(F32)
            pooled = _window(pad_ref, pg, r, s) / _count(r, s) - pg
            mix_ref[:, cols] = (_dot(pooled, pw_ref[g]) * ps_ref[:, cols]).astype(mix_ref.dtype)
        norm = sn_ref[...]
        bt = bt_ref[...]

        def chunk(n, carry):
            rows = pl.ds(pl.multiple_of(n * SGU_CHUNK, SGU_CHUNK), SGU_CHUNK)
            _, _, gu, _, _, _, mixed = _sgu_chunk(u_ref, v_ref, norm, ws_ref, bt, rows)
            for g in range(N_GROUPS):
                cols = slice(g * SGU_GROUP, (g + 1) * SGU_GROUP)
                mix_ref[rows, HALF + g * SGU_GROUP:HALF + (g + 1) * SGU_GROUP] = (
                    gu[:, cols] * mixed[g]).astype(mix_ref.dtype)
            return carry

        lax.fori_loop(0, nchunk, chunk, 0)

    def col(j):
        return pl.BlockSpec((s, HALF), lambda b: (b, j))

    def whole(a):
        return pl.BlockSpec(a.shape, lambda b: (0,) * a.ndim)

    consts = [p["pool_w"], p["pool_scale"].reshape(1, HALF), p["sgu_norm"].reshape(1, HALF),
              p["sgu_w"], p["sgu_b"].T]
    return pl.pallas_call(
        body, name=tag + "_core", grid=(nb,),
        in_specs=[col(0), col(1), col(2)] + [whole(a) for a in consts],
        out_specs=pl.BlockSpec((s, D_MODEL), lambda b: (b, 0)),
        out_shape=jax.ShapeDtypeStruct((t, D_MODEL), _ACT),
        scratch_shapes=[pltpu.VMEM((s + 2 * HALO, POOL_GROUP), F32)],
        compiler_params=_params("parallel"),
    )(proj, proj, proj, *consts)


def _odd_core_bwd(tag, proj, p, d_mix, nb, s):
    t = proj.shape[0]
    nchunk = s // SGU_CHUNK

    def body(p_ref, u_ref, v_ref, pw_ref, ps_ref, sn_ref, ws_ref, bt_ref, dm_ref,
             dproj_ref, dpw_ref, dps_ref, dsn_ref, dws_ref, dbt_ref, pad_ref):
        b = pl.program_id(0)

        @pl.when(b == 0)
        def _():
            dpw_ref[...] = jnp.zeros_like(dpw_ref)
            dps_ref[...] = jnp.zeros_like(dps_ref)
            dsn_ref[...] = jnp.zeros_like(dsn_ref)
            dws_ref[...] = jnp.zeros_like(dws_ref)
            dbt_ref[...] = jnp.zeros_like(dbt_ref)

        _zero_halo(pad_ref, s)
        for g, r in enumerate(POOL_RADII):
            cols = slice(g * POOL_GROUP, (g + 1) * POOL_GROUP)
            pg = p_ref[:, cols].astype(F32)
            cnt = _count(r, s)
            pooled = _window(pad_ref, pg, r, s) / cnt - pg
            c_pre = _dot(pooled, pw_ref[g])
            dc = dm_ref[:, cols]
            dps_ref[:, cols] += jnp.sum(dc * c_pre, axis=0, keepdims=True)
            dcp = dc * ps_ref[:, cols]
            dpw_ref[g] += _dot(pooled, dcp, "tn")
            dpooled = _dot(dcp, pw_ref[g], "nt")
            dproj_ref[:, cols] = (_window(pad_ref, dpooled / cnt, r, s) - dpooled).astype(dproj_ref.dtype)
        norm = sn_ref[...]
        bt = bt_ref[...]

        def chunk(n, carry):
            rows = pl.ds(pl.multiple_of(n * SGU_CHUNK, SGU_CHUNK), SGU_CHUNK)
            uu, vv, gu, r, xhat, vn, mixed = _sgu_chunk(u_ref, v_ref, norm, ws_ref, bt, rows)
            dd = dm_ref[rows, HALF:D_MODEL]
            dgu, dvn = [], []
            for g in range(N_GROUPS):
                cols = slice(g * SGU_GROUP, (g + 1) * SGU_GROUP)
                dgu.append(dd[:, cols] * mixed[g])
                dmx = dd[:, cols] * gu[:, cols]
                dbt_ref[:, g:g + 1] += jnp.sum(dmx, axis=-1, keepdims=True)
                dws_ref[g] += _dot(dmx, vn[:, cols], "nt")
                dvn.append(_dot(ws_ref[g], dmx, "tn"))
            dgu = jnp.concatenate(dgu, axis=1)
            dvn = jnp.concatenate(dvn, axis=1)
            dsn_ref[...] += jnp.sum(dvn * xhat, axis=0, keepdims=True)
            dgx = dvn * norm
            m = jnp.mean(dgx * xhat, axis=-1, keepdims=True)
            dgv = r * (dgx - xhat * m)
            dproj_ref[rows, HALF:2 * HALF] = (dgu * _gelu_grad(uu)).astype(dproj_ref.dtype)
            dproj_ref[rows, 2 * HALF:3 * HALF] = (dgv * _gelu_grad(vv)).astype(dproj_ref.dtype)
            return carry

        lax.fori_loop(0, nchunk, chunk, 0)

    def col(j):
        return pl.BlockSpec((s, HALF), lambda b: (b, j))

    def whole(a):
        return pl.BlockSpec(a.shape, lambda b: (0,) * a.ndim)

    consts = [p["pool_w"], p["pool_scale"].reshape(1, HALF), p["sgu_norm"].reshape(1, HALF),
              p["sgu_w"], p["sgu_b"].T]
    gshapes = [jax.ShapeDtypeStruct(a.shape, F32) for a in consts]
    dproj, dpw, dps, dsn, dws, dbt = pl.pallas_call(
        body, name=tag + "_dcore", grid=(nb,),
        in_specs=[col(0), col(1), col(2)] + [whole(a) for a in consts]
        + [pl.BlockSpec((s, D_MODEL), lambda b: (b, 0))],
        out_specs=[pl.BlockSpec((s, 3 * HALF), lambda b: (b, 0))] + [whole(a) for a in consts],
        out_shape=[jax.ShapeDtypeStruct((t, 3 * HALF), _ACT)] + gshapes,
        scratch_shapes=[pltpu.VMEM((s + 2 * HALO, POOL_GROUP), F32)],
        compiler_params=_params("arbitrary"),
    )(proj, proj, proj, *consts, d_mix)
    return dproj, dict(pool_w=dpw, pool_scale=dps.reshape(HALF), sgu_norm=dsn.reshape(HALF), sgu_w=dws, sgu_b=dbt.T)


def _odd_fwd(tag, x, p, nb, s):
    h = _rmsnorm_fwd(tag + "_norm", x, p["norm"])
    proj = _proj_in(tag, h, p["w_in"])
    mix = _odd_core_fwd(tag, proj, p, nb, s)
    x_out = _proj_out(tag, x, [mix], p["w_out"])
    return x_out, (x, h, proj, mix)


def _odd_bwd(tag, dxo, saved, p, nb, s):
    x, h, proj, mix = saved
    d_mix, dw_out = _proj_out_bwd(tag, dxo, [mix], p["w_out"])
    dproj, grads = _odd_core_bwd(tag, proj, p, d_mix, nb, s)
    dh, dw_in = _proj_in_bwd(tag, h, [dproj], p["w_in"])
    dx, dnorm = _rmsnorm_bwd(tag + "_dnorm", dh, x, p["norm"], dxo)
    grads.update(norm=dnorm, w_in=dw_in, w_out=dw_out)
    return dx, grads


def _local_step(x3, target3, depth, weights_of, final_norm, grads_done):
    nb, s, d = x3.shape
    t = nb * s
    x = x3.reshape(t, d)
    target = target3.reshape(t, d)
    cos, sin = _rope_tables(s)
    saved, ws = [], []
    for l in range(depth):
        w = weights_of(l, x)
        x, s1 = _ffn_fwd(f"l{l}_ffn1", x, w["ffn1"]["norm"], w["ffn1"]["w_in4"], w["ffn1"]["w_out"])
        if l % 2 == 0:
            x, s2 = _even_fwd(f"l{l}_ev", x, w["mix"], cos, sin, nb, s)
        else:
            x, s2 = _odd_fwd(f"l{l}_od", x, w["mix"], nb, s)
        x, s3 = _ffn_fwd(f"l{l}_ffn2", x, w["ffn2"]["norm"], w["ffn2"]["w_in4"], w["ffn2"]["w_out"])
        saved.append((s1, s2, s3))
        ws.append(w)
    loss, dx, dfinal = _final_loss("final_loss", x, final_norm, target)
    token = 0.0
    for l in reversed(range(depth)):
        s1, s2, s3 = saved[l]
        w = ws[l]
        dx, dn, dwi, dwo = _ffn_bwd(f"l{l}_ffn2", dx, s3, w["ffn2"]["norm"] + token, w["ffn2"]["w_in4"], w["ffn2"]["w_out"])
        g2 = dict(norm=dn, w_in4=dwi, w_out=dwo)
        if l % 2 == 0:
            dx, gm = _even_bwd(f"l{l}_ev", dx, s2, w["mix"], cos, sin, nb, s)
        else:
            dx, gm = _odd_bwd(f"l{l}_od", dx, s2, w["mix"], nb, s)
        dx, dn, dwi, dwo = _ffn_bwd(f"l{l}_ffn1", dx, s1, w["ffn1"]["norm"], w["ffn1"]["w_in4"], w["ffn1"]["w_out"])
        token = grads_done(l, dict(ffn1=dict(norm=dn, w_in4=dwi, w_out=dwo), mix=gm, ffn2=g2), dx)
    return loss, dx.reshape(nb, s, d), dfinal


_HBM = pl.BlockSpec(memory_space=pltpu.HBM)


def _place():
    x, y, c = lax.axis_index("x"), lax.axis_index("y"), lax.axis_index("c")
    chips = [(1 - x, y), (x, 1 - y), (1 - x, 1 - y)]
    return x, y, c, chips


def _remote(src, dst, send_sem, recv_sem, to):
    return pltpu.make_async_remote_copy(src_ref=src, dst_ref=dst, send_sem=send_sem, recv_sem=recv_sem,
                                        device_id=to, device_id_type=_MESH)


def _gather_shards(arrs, small):
    n = len(arrs)
    own = 6

    def body(*refs):
        ins, sm_in = refs[:n], refs[n]
        outs, sm_out = refs[n + 1:2 * n + 1], refs[2 * n + 1]
        send, recv = refs[2 * n + 2:]
        x, y, c, chips = _place()
        k = 2 * x + y
        sib = (x, y, 1 - c)
        started = []
        for a in range(n + 1):
            src, dst = (ins[a], outs[a]) if a < n else (sm_in, sm_out)
            cp = _remote(src, dst.at[k], send.at[a, own], recv.at[a, own], sib)
            cp.start()
            started.append(cp)
            if a < n:
                h = src.shape[0] // 2
                mine = pl.ds(c * h, h)
                src_part, dst_part = src.at[mine], dst.at[k, mine]
            else:
                src_part, dst_part = src, dst.at[k]
            for j, chip in enumerate(chips):
                cp = _remote(src_part, dst_part, send.at[a, j], recv.at[a, j], (*chip, c))
                cp.start()
                started.append(cp)
        for a in range(n):
            h = ins[a].shape[0] // 2
            mine = pl.ds(c * h, h)
            for j, (px, py) in enumerate(chips):
                landed = outs[a].at[2 * px + py, mine]
                _remote(landed, landed, send.at[a, j], recv.at[a, j], (px, py, c)).wait_recv()
                cp = _remote(landed, landed, send.at[a, 3 + j], recv.at[a, 3 + j], sib)
                cp.start()
                started.append(cp)
        for a in range(n):
            h = ins[a].shape[0] // 2
            other = pl.ds((1 - c) * h, h)
            for j, (px, py) in enumerate(chips):
                passed = outs[a].at[2 * px + py, other]
                _remote(passed, passed, send.at[a, 3 + j], recv.at[a, 3 + j], sib).wait_recv()
        for j, (px, py) in enumerate(chips):
            landed = sm_out.at[2 * px + py]
            _remote(landed, landed, send.at[n, j], recv.at[n, j], (px, py, c)).wait_recv()
        for a in range(n + 1):
            filled = (outs[a] if a < n else sm_out).at[k]
            _remote(filled, filled, send.at[a, own], recv.at[a, own], sib).wait_recv()
        for cp in started:
            cp.wait_send()

    outs = pl.pallas_call(
        body, name="gather_shards",
        in_specs=[_HBM] * (n + 1), out_specs=[_HBM] * (n + 1),
        out_shape=[jax.ShapeDtypeStruct((N_CHIPS,) + a.shape, a.dtype) for a in list(arrs) + [small]],
        scratch_shapes=[pltpu.SemaphoreType.DMA((n + 1, 7)), pltpu.SemaphoreType.DMA((n + 1, 7))],
    )(*arrs, small)
    return outs[:n], outs[n]


def _swap_halves(name, grads):
    n = len(grads)

    def body(*refs):
        ins, outs = refs[:n], refs[n:2 * n]
        send, recv = refs[2 * n:]
        x, y, c, _ = _place()
        sib = (x, y, 1 - c)
        cps = []
        for a in range(n):
            h = ins[a].shape[1] // 2
            cp = _remote(ins[a].at[:, pl.ds((1 - c) * h, h)], outs[a], send.at[a], recv.at[a], sib)
            cp.start()
            cps.append(cp)
        for cp in cps:
            cp.wait()

    return pl.pallas_call(
        body, name=name,
        in_specs=[_HBM] * n, out_specs=[_HBM] * n,
        out_shape=[jax.ShapeDtypeStruct((g.shape[0], g.shape[1] // 2) + g.shape[2:], g.dtype) for g in grads],
        scratch_shapes=[pltpu.SemaphoreType.DMA((n,)), pltpu.SemaphoreType.DMA((n,))],
    )(*grads)


_SEM = pl.BlockSpec(memory_space=pltpu.SEMAPHORE)
_EFFECT = pltpu.SideEffectType.DATAFLOW_SIDE_EFFECTING


def _gather_plan(src, land, k, c, chips, sib):
    plan = [(src, land.at[k], (px, py, c), land.at[2 * px + py]) for px, py in chips]
    return plan + [(src, land.at[k], sib, land.at[k])]


def _scatter_plan(src, land, k, c, chips, sib):
    return [(src.at[2 * px + py], land.at[k], (px, py, c), land.at[2 * px + py]) for px, py in chips]


def _split_start(name, plan, ncopy, srcs, after):
    n = len(srcs)
    lands = [pltpu.with_memory_space_constraint(lax.empty((N_CHIPS,) + a.shape[-2:], a.dtype), pltpu.HBM) for a in srcs]

    def body(*refs):
        src_refs, land_refs = refs[1:1 + n], refs[1 + n:1 + 2 * n]
        send, recv, token = refs[1 + 2 * n], refs[2 + 2 * n], refs[-1]
        x, y, c, chips = _place()
        for i in range(n):
            for j, (src, dst, peer, _) in enumerate(plan(src_refs[i], land_refs[i], 2 * x + y, c, chips, (x, y, 1 - c))):
                _remote(src, dst, send.at[i * ncopy + j], recv.at[i * ncopy + j], peer).start()
        token[...] = jnp.zeros_like(token)

    outs = pl.pallas_call(
        body, name=name,
        in_specs=[_ANY] + [_HBM] * (2 * n),
        out_specs=[_SEM, _SEM] + [_HBM] * (2 * n) + [_VMEM],
        out_shape=[pltpu.SemaphoreType.DMA((n * ncopy,)), pltpu.SemaphoreType.DMA((n * ncopy,))]
        + [pltpu.HBM(a.shape, a.dtype) for a in list(srcs) + lands] + [jax.ShapeDtypeStruct((8, LANES), F32)],
        input_output_aliases={1 + i: 2 + i for i in range(2 * n)},
        compiler_params=pltpu.CompilerParams(has_side_effects=_EFFECT),
    )(after, *[pltpu.with_memory_space_constraint(a, pltpu.HBM) for a in srcs], *lands)
    return outs[0], outs[1], outs[2:2 + n], outs[2 + n:2 + 2 * n], outs[-1]


def _split_wait(name, plan, started, after):
    send, recv, srcs, lands = started
    n = len(srcs)
    ncopy = send.shape[0] // n

    def body(*refs):
        src_refs, land_refs = refs[:n], refs[n:2 * n]
        send, recv = refs[2 * n], refs[2 * n + 1]
        x, y, c, chips = _place()
        for i in range(n):
            for j, (src, _, peer, landed) in enumerate(plan(src_refs[i], land_refs[i], 2 * x + y, c, chips, (x, y, 1 - c))):
                cp = _remote(src, landed, send.at[i * ncopy + j], recv.at[i * ncopy + j], peer)
                cp.wait_send()
                cp.wait_recv()

    outs = pl.pallas_call(
        body, name=name,
        in_specs=[_HBM] * (2 * n) + [_SEM, _SEM, _ANY],
        out_specs=[_HBM] * (2 * n),
        out_shape=[pltpu.HBM(a.shape, a.dtype) for a in list(srcs) + list(lands)],
        input_output_aliases={i: i for i in range(2 * n)},
        compiler_params=pltpu.CompilerParams(has_side_effects=_EFFECT),
    )(*srcs, *lands, send, recv, after)
    return outs[:n], outs[n:]


def _join_halves(name, bufs, layers):
    n = len(bufs)

    def body(*refs):
        ins, outs = refs[:n], refs[n:2 * n]
        send, recv = refs[2 * n:]
        x, y, c, _ = _place()
        sib = (x, y, 1 - c)
        cps = []
        for a in range(n):
            h = ins[a].shape[1] // 2
            mine = pl.ds(c * h, h)
            cp = _remote(ins[a].at[layers[a], mine], outs[a].at[layers[a], mine], send.at[a], recv.at[a], sib)
            cp.start()
            cps.append(cp)
        for a in range(n):
            h = ins[a].shape[1] // 2
            theirs = outs[a].at[layers[a], pl.ds((1 - c) * h, h)]
            _remote(theirs, theirs, send.at[a], recv.at[a], sib).wait_recv()
        for cp in cps:
            cp.wait_send()

    return pl.pallas_call(
        body, name=name,
        in_specs=[_HBM] * n, out_specs=[_HBM] * n,
        out_shape=[jax.ShapeDtypeStruct(p.shape, p.dtype) for p in bufs],
        input_output_aliases={a: a for a in range(n)},
        scratch_shapes=[pltpu.SemaphoreType.DMA((n,)), pltpu.SemaphoreType.DMA((n,))],
    )(*bufs)


def _allreduce_small(buf):
    rows = buf.shape[0]

    def body(in_ref, out_ref, land_ref, send, recv):
        x, y, c, _ = _place()
        me = 4 * x + 2 * y + c
        land_ref[me] = in_ref[...]
        peers = []
        for r in range(1, N_DEV):
            peers.append((1 - x if r & 4 else x, 1 - y if r & 2 else y, 1 - c if r & 1 else c))
        cps = []
        for r, peer in enumerate(peers):
            cp = _remote(in_ref, land_ref.at[me], send.at[r], recv.at[r], peer)
            cp.start()
            cps.append(cp)
        for r, (px, py, pc) in enumerate(peers):
            landed = land_ref.at[4 * px + 2 * py + pc]
            _remote(landed, landed, send.at[r], recv.at[r], (px, py, pc)).wait_recv()
        for cp in cps:
            cp.wait_send()
        acc = land_ref[0]
        for d in range(1, N_DEV):
            acc = acc + land_ref[d]
        out_ref[...] = acc

    return pl.pallas_call(
        body, name="allreduce_small",
        in_specs=[_VMEM], out_specs=_VMEM,
        out_shape=jax.ShapeDtypeStruct(buf.shape, F32),
        scratch_shapes=[pltpu.VMEM((N_DEV, rows, LANES), F32), pltpu.SemaphoreType.DMA((N_DEV - 1,)),
                        pltpu.SemaphoreType.DMA((N_DEV - 1,))],
        compiler_params=pltpu.CompilerParams(vmem_limit_bytes=_VMEM_LIMIT),
    )(buf)


def _div_tile(n, cap, mult):
    best = None
    for d in range(mult, min(n, cap) + 1, mult):
        if n % d == 0:
            best = d
    return best if best is not None else n


def _add_sibling(name, grad, got, c):
    nk, hr, cc = got.shape
    tr = _div_tile(hr, 512, 16)
    nt = hr // tr

    def body(c_ref, g_ref, o_ref, s_ref):
        s_ref[...] = (g_ref[...].astype(F32) + o_ref[...].astype(F32)).astype(s_ref.dtype)

    blk = (None, tr, cc)
    return pl.pallas_call(
        body, name=name,
        grid_spec=pltpu.PrefetchScalarGridSpec(
            num_scalar_prefetch=1, grid=(nk, nt),
            in_specs=[pl.BlockSpec(blk, lambda i, q, c_ref: (i, c_ref[0] * nt + q, 0)),
                      pl.BlockSpec(blk, lambda i, q, c_ref: (i, q, 0))],
            out_specs=pl.BlockSpec(blk, lambda i, q, c_ref: (i, q, 0))),
        out_shape=jax.ShapeDtypeStruct(got.shape, got.dtype),
        compiler_params=_params("parallel", "parallel"),
    )(c, grad, got)


def _add_chips(name, mine, got, place, buf, l):
    nk, hr, cc = got.shape
    tr = _div_tile(hr, 512, 16)
    nt = hr // tr

    def body(*refs):
        acc = refs[1][...].astype(F32)
        for q in range(1, nk):
            acc = acc + refs[1 + q][...].astype(F32)
        refs[2 + nk][...] = acc

    def part(q):
        return pl.BlockSpec((None, tr, cc), lambda i, p_ref: ((p_ref[0] + q) % nk, i, 0))

    return pl.pallas_call(
        body, name=name,
        grid_spec=pltpu.PrefetchScalarGridSpec(
            num_scalar_prefetch=1, grid=(nt,),
            in_specs=[part(q) for q in range(nk)] + [_ANY],
            out_specs=pl.BlockSpec((None, tr, cc), lambda i, p_ref: (l, p_ref[1] * nt + i, 0))),
        out_shape=jax.ShapeDtypeStruct(buf.shape, F32),
        input_output_aliases={1 + nk: 0},
        compiler_params=_params("parallel"),
    )(place, mine, *([got] * (nk - 1)), buf)


def _adamw(name, w, g, m, v):
    shape = w.shape
    cols = shape[-1]
    rows = w.size // cols
    tr = rows if rows * cols <= 2 ** 18 else _div_tile(rows, max(8, 2 ** 18 // cols), 8)
    c1 = 1.0 - ADAM_B1 ** ADAM_STEP
    c2 = 1.0 - ADAM_B2 ** ADAM_STEP

    def body(w_ref, g_ref, m_ref, v_ref, d_ref, mo_ref, vo_ref):
        gg = g_ref[...]
        mn = ADAM_B1 * m_ref[...] + (1.0 - ADAM_B1) * gg
        vn = ADAM_B2 * v_ref[...] + (1.0 - ADAM_B2) * (gg * gg)
        d_ref[...] = -ADAM_LR * ((mn / c1) / (jnp.sqrt(vn / c2) + ADAM_EPS) + ADAM_WD * w_ref[...])
        mo_ref[...] = mn
        vo_ref[...] = vn

    blk = pl.BlockSpec((tr, cols), lambda i: (i, 0))
    sds = jax.ShapeDtypeStruct((rows, cols), F32)
    outs = pl.pallas_call(
        body, name=name, grid=(rows // tr,),
        in_specs=[blk] * 4, out_specs=(blk,) * 3, out_shape=(sds,) * 3,
        compiler_params=_params("parallel"),
    )(*[a.reshape(rows, cols) for a in (w, g, m, v)])
    return [o.reshape(shape) for o in outs]


_WEIGHTS = ["ffn1_norm", "ffn1_w_in", "ffn1_w_out", "mix_norm", "ffn2_norm", "ffn2_w_in", "ffn2_w_out",
            "ev_w_in", "ev_conv_w", "ev_q_norm", "ev_k_norm", "ev_w_out", "od_w_in", "od_pool_w",
            "od_pool_scale", "od_sgu_norm", "od_sgu_w", "od_sgu_b", "od_w_out", "final_norm"]
_BIG = ["ffn1_w_in", "ffn1_w_out", "ffn2_w_in", "ffn2_w_out", "ev_w_in", "ev_w_out", "od_w_in", "od_w_out"]
_SMALL_SHARDED = ["ev_conv_w", "od_pool_scale", "od_sgu_norm"]


def _pad_rows(a, mult=8):
    pad = (-a.shape[0]) % mult
    return a if pad == 0 else jnp.concatenate([a, jnp.zeros((pad,) + a.shape[1:], a.dtype)], axis=0)


def _join_cols(g):
    return g.transpose(1, 0, 2).reshape(g.shape[1], N_CHIPS * g.shape[2])


def _split_cols(w):
    return w.reshape(w.shape[0], N_CHIPS, w.shape[1] // N_CHIPS).transpose(1, 0, 2)


def kernel(x, ffn1_norm, ffn1_w_in, ffn1_w_out, mix_norm, ffn2_norm, ffn2_w_in, ffn2_w_out, ev_w_in, ev_conv_w,
           ev_q_norm, ev_k_norm, ev_w_out, od_w_in, od_pool_w, od_pool_scale, od_sgu_norm, od_sgu_w, od_sgu_b,
           od_w_out, final_norm, loss_target, m_ffn1_norm, m_ffn1_w_in, m_ffn1_w_out, m_mix_norm, m_ffn2_norm,
           m_ffn2_w_in, m_ffn2_w_out, m_ev_w_in, m_ev_conv_w, m_ev_q_norm, m_ev_k_norm, m_ev_w_out, m_od_w_in,
           m_od_pool_w, m_od_pool_scale, m_od_sgu_norm, m_od_sgu_w, m_od_sgu_b, m_od_w_out, m_final_norm, v_ffn1_norm,
           v_ffn1_w_in, v_ffn1_w_out, v_mix_norm, v_ffn2_norm, v_ffn2_w_in, v_ffn2_w_out, v_ev_w_in, v_ev_conv_w,
           v_ev_q_norm, v_ev_k_norm, v_ev_w_out, v_od_w_in, v_od_pool_w, v_od_pool_scale, v_od_sgu_norm, v_od_sgu_w,
           v_od_sgu_b, v_od_w_out, v_final_norm):
    return _step(x, ffn1_norm, ffn1_w_in, ffn1_w_out, mix_norm, ffn2_norm, ffn2_w_in, ffn2_w_out, ev_w_in, ev_conv_w,
                 ev_q_norm, ev_k_norm, ev_w_out, od_w_in, od_pool_w, od_pool_scale, od_sgu_norm, od_sgu_w, od_sgu_b,
                 od_w_out, final_norm, loss_target, m_ffn1_norm, m_ffn1_w_in, m_ffn1_w_out, m_mix_norm, m_ffn2_norm,
                 m_ffn2_w_in, m_ffn2_w_out, m_ev_w_in, m_ev_conv_w, m_ev_q_norm, m_ev_k_norm, m_ev_w_out, m_od_w_in,
                 m_od_pool_w, m_od_pool_scale, m_od_sgu_norm, m_od_sgu_w, m_od_sgu_b, m_od_w_out, m_final_norm,
                 v_ffn1_norm, v_ffn1_w_in, v_ffn1_w_out, v_mix_norm, v_ffn2_norm, v_ffn2_w_in, v_ffn2_w_out,
                 v_ev_w_in, v_ev_conv_w, v_ev_q_norm, v_ev_k_norm, v_ev_w_out, v_od_w_in, v_od_pool_w,
                 v_od_pool_scale, v_od_sgu_norm, v_od_sgu_w, v_od_sgu_b, v_od_w_out, v_final_norm)


def _step(*args):
    nw = len(_WEIGHTS)
    x = args[0]
    w = dict(zip(_WEIGHTS, args[1:1 + nw]))
    target = args[1 + nw]
    m = dict(zip(_WEIGHTS, args[2 + nw:2 + 2 * nw]))
    v = dict(zip(_WEIGHTS, args[2 + 2 * nw:2 + 3 * nw]))
    depth = w["ffn1_norm"].shape[0]
    n_even, n_odd = w["ev_w_in"].shape[0], w["od_w_in"].shape[0]
    chip = 2 * lax.axis_index("x") + lax.axis_index("y")
    place = jnp.stack([chip, lax.axis_index("c")]).astype(jnp.int32)
    core = place[1:2]

    def sharded(l):
        pre, j = ("ev", l // 2) if l % 2 == 0 else ("od", l // 2)
        return [("ffn1_w_in", l), ("ffn1_w_out", l), (pre + "_w_in", j), (pre + "_w_out", j),
                ("ffn2_w_in", l), ("ffn2_w_out", l)]

    def shards(l):
        return [w[n][i].astype(_ACT) for n, i in sharded(l)]

    small_rows = [w["ev_conv_w"].reshape(3 * n_even, LANES), w["od_pool_scale"], w["od_sgu_norm"]]
    first, small = _gather_shards(shards(0), _pad_rows(jnp.concatenate(small_rows, axis=0)))
    conv_w = small[:, :3 * n_even].reshape(N_CHIPS, n_even, 3, LANES).transpose(1, 2, 0, 3).reshape(n_even, 3, CONV_WIDTH)
    pool_scale = small[:, 3 * n_even:3 * n_even + n_odd].transpose(1, 0, 2).reshape(n_odd, HALF)
    sgu_norm = small[:, 3 * n_even + n_odd:3 * n_even + 2 * n_odd].transpose(1, 0, 2).reshape(n_odd, HALF)
    gathering, after = {}, small
    for l in range(1, depth):
        gathering[l] = _split_start(f"gather_start{l}", _gather_plan, N_CHIPS, shards(l), after)
        after = gathering[l][4]
    started = after[0, 0] if depth > 1 else 0.0

    def rows(g):
        return g.reshape(N_CHIPS * g.shape[1], g.shape[2])

    def weights_of(l, x_in):
        if l == 0:
            got, zero = first, started
        else:
            got, zero = _split_wait(f"gather_wait{l}", _gather_plan, gathering[l][:4], x_in)[1], 0.0
        j = l // 2
        if l % 2 == 0:
            mix = dict(conv_w=conv_w[j], q_gain=w["ev_q_norm"][j], k_gain=w["ev_k_norm"][j])
        else:
            mix = dict(pool_w=w["od_pool_w"][j], pool_scale=pool_scale[j], sgu_norm=sgu_norm[j],
                       sgu_w=w["od_sgu_w"][j], sgu_b=w["od_sgu_b"][j])
        mix.update(norm=w["mix_norm"][l], w_in=_join_cols(got[2]), w_out=rows(got[3]))
        return dict(ffn1=dict(norm=w["ffn1_norm"][l] + zero, w_in4=got[0], w_out=rows(got[1])), mix=mix,
                    ffn2=dict(norm=w["ffn2_norm"][l], w_in4=got[4], w_out=rows(got[5])))

    def by_chip(dw):
        return dw.reshape(N_CHIPS, dw.shape[0] // N_CHIPS, dw.shape[1])

    bufs = {n: lax.empty(w[n].shape, F32) for n in _BIG}
    small_grads = {n: [None] * w[n].shape[0] for n in _WEIGHTS if n not in _BIG and n != "final_norm"}
    scattering = []

    def finish_scatter(after):
        l, started = scattering.pop()
        halves, got = _split_wait(f"scatter_wait{l}", _scatter_plan, started[:4], after)
        names = sharded(l)
        for i, (n, j) in enumerate(names):
            bufs[n] = _add_chips(f"add_chips{l}_{n}", halves[i], got[i], place, bufs[n], j)
        joined = _join_halves(f"join_halves{l}", [bufs[n] for n, _ in names], [j for _, j in names])
        for (n, _), b in zip(names, joined):
            bufs[n] = b

    def grads_done(l, g, dx):
        if scattering:
            finish_scatter(dx)
        local = [g["ffn1"]["w_in4"], by_chip(g["ffn1"]["w_out"]), _split_cols(g["mix"]["w_in"]),
                 by_chip(g["mix"]["w_out"]), g["ffn2"]["w_in4"], by_chip(g["ffn2"]["w_out"])]
        from_sibling = _swap_halves(f"swap_halves{l}", local)
        halves = [_add_sibling(f"add_sibling{l}_{n}", a, b, core)
                  for (n, _), a, b in zip(sharded(l), local, from_sibling)]
        scattering.append((l, _split_start(f"scatter_start{l}", _scatter_plan, N_CHIPS - 1, halves, dx)))
        j = l // 2
        small_grads["ffn1_norm"][l] = g["ffn1"]["norm"]
        small_grads["mix_norm"][l] = g["mix"]["norm"]
        small_grads["ffn2_norm"][l] = g["ffn2"]["norm"]
        renamed = (dict(conv_w="ev_conv_w", q_gain="ev_q_norm", k_gain="ev_k_norm") if l % 2 == 0 else
                   dict(pool_w="od_pool_w", pool_scale="od_pool_scale", sgu_norm="od_sgu_norm", sgu_w="od_sgu_w",
                        sgu_b="od_sgu_b"))
        for key, n in renamed.items():
            small_grads[n][j] = g["mix"][key]
        return scattering[-1][1][4][0, 0]

    loss_part, grad_x, dfinal = _local_step(x, target, depth, weights_of, w["final_norm"], grads_done)
    loss = lax.psum(loss_part, ("x", "y", "c"))

    small_grads = {n: jnp.stack(parts) for n, parts in small_grads.items()}
    small_grads["final_norm"] = dfinal
    names = list(small_grads)
    flat = jnp.concatenate([small_grads[n].reshape(-1) for n in names])
    total = flat.shape[0]
    flat = jnp.concatenate([flat, jnp.zeros((-total) % (8 * LANES), F32)])
    summed = _allreduce_small(flat.reshape(-1, LANES)).reshape(-1)
    finish_scatter(summed)
    grads = dict(bufs)
    off = 0
    for n in names:
        size = small_grads[n].size
        full_grad = summed[off:off + size].reshape(small_grads[n].shape)
        off += size
        if n in _SMALL_SHARDED:
            full_grad = lax.dynamic_slice_in_dim(full_grad, chip * LANES, LANES, axis=full_grad.ndim - 1)
        grads[n] = full_grad

    deltas, new_m, new_v = [], [], []
    for n in _WEIGHTS:
        d_n, m_n, v_n = _adamw("adamw_" + n, w[n], grads[n], m[n], v[n])
        deltas.append(d_n)
        new_m.append(m_n)
        new_v.append(v_n)
    return (loss, grad_x, *[grads[n] for n in _WEIGHTS], *deltas, *new_m, *new_v)
```

```python
import jax
import jax.numpy as jnp
from jax import lax
from jax.experimental import pallas as pl
from jax.experimental.pallas import tpu as pltpu

F32 = jnp.float32
_MXU = jnp.bfloat16
_ACT = jnp.bfloat16

D_MODEL = 1024
GRID_W = 64
HEAD_DIM = 64
N_Q_HEADS = 8
N_KV_HEADS = 2
Q_PER_KV = N_Q_HEADS // N_KV_HEADS
ATTN_WIDTH = N_Q_HEADS * HEAD_DIM
KV_WIDTH = N_KV_HEADS * HEAD_DIM
ROPE_THETA = 10000.0
CONV_WIDTH = D_MODEL // 2
POOL_RADII = (1, 2, 4, 8)
POOL_GROUP = 128
SGU_GROUP = 128
SGU_CHUNK = 128
N_GROUPS = 4
HALF = D_MODEL // 2
EPS = 1e-6
HALO = 8
LANES = 128
N_CHIPS = 4
N_DEV = 8

ADAM_LR = 0.001
ADAM_B1 = 0.9
ADAM_B2 = 0.999
ADAM_EPS = 1e-08
ADAM_WD = 0.01
ADAM_STEP = 10

_VMEM_LIMIT = 56 * 2 ** 20
_MESH = pl.DeviceIdType.MESH
_ANY = pl.BlockSpec(memory_space=pl.ANY)
_VMEM = pl.BlockSpec(memory_space=pltpu.VMEM)

_DN = {
    "nn": (((1,), (0,)), ((), ())),
    "nt": (((1,), (1,)), ((), ())),
    "tn": (((0,), (0,)), ((), ())),
}


def _params(*sem):
    return pltpu.CompilerParams(dimension_semantics=sem, vmem_limit_bytes=_VMEM_LIMIT)


def _tile(n, cap):
    best = None
    d = LANES
    while d <= min(n, cap):
        if n % d == 0:
            best = d
        d += LANES
    return best if best is not None else n


def _dot(a, b, mode="nn"):
    return lax.dot_general(a.astype(_MXU), b.astype(_MXU), _DN[mode], preferred_element_type=F32)


def _cat(*vals):
    vals = [v.astype(_MXU) for v in vals]
    return vals[0] if len(vals) == 1 else jnp.concatenate(vals, axis=1)


def _sigmoid(g):
    return 1.0 / (1.0 + jnp.exp(-g))


def _swiglu(g, u):
    g = g.astype(F32)
    return (g * _sigmoid(g)) * u.astype(F32)


_GELU_C = 0.7978845608028654


def _gelu(x):
    return 0.5 * x * (1.0 + jnp.tanh(_GELU_C * (x + 0.044715 * (x * x * x))))


def _gelu_grad(x):
    t = jnp.tanh(_GELU_C * (x + 0.044715 * (x * x * x)))
    return 0.5 * (1.0 + t) + 0.5 * x * (1.0 - t * t) * (_GELU_C * (1.0 + 3.0 * 0.044715 * (x * x)))


def _mm(name, grid, mode, a_ops, b_ops, e_ops, out_shape, out_specs, acc_shape, a_fn=_cat, b_fn=_cat, epi=None,
        n_outer=False, m_carried=False):
    ni, nj, nk = grid
    na, nb, ne = len(a_ops), len(b_ops), len(e_ops)
    multi = isinstance(out_shape, (list, tuple))
    no = len(out_shape) if multi else 1

    def body(*refs):
        a_refs = refs[:na]
        b_refs = refs[na:na + nb]
        e_refs = refs[na + nb:na + nb + ne]
        o_refs = refs[na + nb + ne:na + nb + ne + no]
        a = a_fn(*[r[...] for r in a_refs])
        b = b_fn(*[r[...] for r in b_refs])
        p = _dot(a, b, mode)

        def finish(acc):
            if epi is None:
                o_refs[0][...] = acc.astype(o_refs[0].dtype)
            else:
                epi(acc, [r[...] for r in e_refs], o_refs)

        if nk == 1:
            finish(p)
        else:
            acc_ref = refs[-1]
            k = pl.program_id(2)

            @pl.when(k == 0)
            def _():
                acc_ref[...] = p

            @pl.when((k > 0) & (k < nk - 1))
            def _():
                acc_ref[...] += p

            @pl.when(k == nk - 1)
            def _():
                finish(acc_ref[...] + p)

    ops = list(a_ops) + list(b_ops) + list(e_ops)
    if n_outer:
        def flip(spec):
            return pl.BlockSpec(spec.block_shape, lambda j, i, k, f=spec.index_map: f(i, j, k))

        grid = (nj, ni, nk)
        ops = [(a, flip(s)) for a, s in ops]
        out_specs = [flip(s) for s in out_specs] if multi else flip(out_specs)
    return pl.pallas_call(
        body, name=name, grid=grid,
        in_specs=[s for _, s in ops],
        out_specs=out_specs, out_shape=out_shape,
        scratch_shapes=[pltpu.VMEM(acc_shape, F32)] if nk > 1 else [],
        compiler_params=_params(*(("arbitrary",) * 3 if m_carried else ("parallel", "parallel", "arbitrary"))),
    )(*[a for a, _ in ops])


def _norm_bwd_epi(acc, e, o):
    xf, dres, g = e
    r = lax.rsqrt(jnp.mean(xf * xf, axis=-1, keepdims=True) + EPS)
    xhat = xf * r
    dgx = acc * g
    m = jnp.mean(dgx * xhat, axis=-1, keepdims=True)
    o[0][...] = dres + r * (dgx - xhat * m)
    part = jnp.sum(acc * xhat, axis=0, keepdims=True)
    i = pl.program_id(0)

    @pl.when(i == 0)
    def _():
        o[1][...] = part

    @pl.when(i > 0)
    def _():
        o[1][...] += part


def _norm_bwd_ops(x, dres, gain, tm):
    t, d = x.shape
    row = pl.BlockSpec((tm, d), lambda i, j, k: (i, 0))
    vec = pl.BlockSpec((1, d), lambda i, j, k: (0, 0))
    return ([(x, row), (dres, row), (gain.reshape(1, d), vec)],
            [jax.ShapeDtypeStruct((t, d), F32), jax.ShapeDtypeStruct((1, d), F32)], [row, vec])


def _rows(t):
    return _tile(t, 512)


def _rmsnorm_fwd(name, x, gain):
    t, d = x.shape
    tr = _rows(t)

    def body(x_ref, g_ref, h_ref):
        xf = x_ref[...]
        r = lax.rsqrt(jnp.mean(xf * xf, axis=-1, keepdims=True) + EPS)
        h_ref[...] = ((xf * r) * g_ref[...]).astype(h_ref.dtype)

    return pl.pallas_call(
        body, name=name, grid=(t // tr,),
        in_specs=[pl.BlockSpec((tr, d), lambda i: (i, 0)), pl.BlockSpec((1, d), lambda i: (0, 0))],
        out_specs=pl.BlockSpec((tr, d), lambda i: (i, 0)),
        out_shape=jax.ShapeDtypeStruct((t, d), _ACT),
        compiler_params=_params("parallel"),
    )(x, gain.reshape(1, d))


def _rmsnorm_bwd(name, dh, x, gain, dres):
    t, d = x.shape
    tr = _rows(t)

    def body(dh_ref, x_ref, g_ref, dres_ref, dx_ref, dg_ref):
        i = pl.program_id(0)
        xf = x_ref[...]
        r = lax.rsqrt(jnp.mean(xf * xf, axis=-1, keepdims=True) + EPS)
        xhat = xf * r
        dy = dh_ref[...].astype(F32)
        dgx = dy * g_ref[...]
        m = jnp.mean(dgx * xhat, axis=-1, keepdims=True)
        dx_ref[...] = dres_ref[...] + r * (dgx - xhat * m)
        part = jnp.sum(dy * xhat, axis=0, keepdims=True)

        @pl.when(i == 0)
        def _():
            dg_ref[...] = part

        @pl.when(i > 0)
        def _():
            dg_ref[...] += part

    row = pl.BlockSpec((tr, d), lambda i: (i, 0))
    vec = pl.BlockSpec((1, d), lambda i: (0, 0))
    dx, dg = pl.pallas_call(
        body, name=name, grid=(t // tr,),
        in_specs=[row, row, vec, row],
        out_specs=(row, vec),
        out_shape=(jax.ShapeDtypeStruct((t, d), F32), jax.ShapeDtypeStruct((1, d), F32)),
        compiler_params=_params("arbitrary"),
    )(dh, x, gain.reshape(1, d), dres)
    return dx, dg.reshape(d)


def _final_loss(name, x, gain, target):
    t, d = x.shape
    tr = _rows(t)

    def body(x_ref, g_ref, t_ref, dx_ref, dg_ref, loss_ref):
        i = pl.program_id(0)
        xf = x_ref[...]
        r = lax.rsqrt(jnp.mean(xf * xf, axis=-1, keepdims=True) + EPS)
        xhat = xf * r
        g = g_ref[...]
        err = xhat * g - t_ref[...]
        lpart = 0.5 * jnp.sum(jnp.mean(err * err, axis=-1, keepdims=True), axis=0, keepdims=True)
        dy = err * (1.0 / d)
        dgx = dy * g
        m = jnp.mean(dgx * xhat, axis=-1, keepdims=True)
        dx_ref[...] = r * (dgx - xhat * m)
        part = jnp.sum(dy * xhat, axis=0, keepdims=True)
        lrow = jnp.broadcast_to(lpart, (1, LANES))

        @pl.when(i == 0)
        def _():
            dg_ref[...] = part
            loss_ref[...] = lrow

        @pl.when(i > 0)
        def _():
            dg_ref[...] += part
            loss_ref[...] += lrow

    row = pl.BlockSpec((tr, d), lambda i: (i, 0))
    vec = pl.BlockSpec((1, d), lambda i: (0, 0))
    dx, dg, loss = pl.pallas_call(
        body, name=name, grid=(t // tr,),
        in_specs=[row, vec, row],
        out_specs=(row, vec, pl.BlockSpec((1, LANES), lambda i: (0, 0))),
        out_shape=(jax.ShapeDtypeStruct((t, d), F32), jax.ShapeDtypeStruct((1, d), F32),
                   jax.ShapeDtypeStruct((1, LANES), F32)),
        compiler_params=_params("arbitrary"),
    )(x, gain.reshape(1, d), target)
    return loss[0, 0], dx, dg.reshape(d)


_FFN_TILES = dict(in_tm=1024, in_n_outer=False, out_tm=512, dact_tm=512, dwout_tk=1024, dh_tm=512, dwin_tk=2048)


def _ffn_tiles(layer, which):
    return _FFN_TILES


def _ffn_fwd(tag, x, gain, w_in4, w_out, cfg):
    t, d = x.shape
    fs = w_in4.shape[2]
    f = 2 * fs
    tm = _tile(t, cfg["in_tm"])
    h = _rmsnorm_fwd(tag + "_norm", x, gain)
    gu = _mm(
        tag + "_in", (t // tm, N_CHIPS, 1), "nn",
        [(h, pl.BlockSpec((tm, d), lambda i, j, k: (i, 0)))],
        [(w_in4, pl.BlockSpec((None, d, fs), lambda i, j, k: (j, 0, 0)))], [],
        jax.ShapeDtypeStruct((2, t, f), _ACT),
        pl.BlockSpec((None, tm, fs), lambda i, j, k: (j // 2, i, j % 2)), None, n_outer=cfg["in_n_outer"])
    tm2 = _tile(t, cfg["out_tm"])

    def epi(acc, e, o):
        o[0][...] = e[0] + 0.5 * acc

    x_out = _mm(
        tag + "_out", (t // tm2, 1, 1), "nn",
        [(gu, pl.BlockSpec((None, tm2, f), lambda i, j, k: (0, i, 0))),
         (gu, pl.BlockSpec((None, tm2, f), lambda i, j, k: (1, i, 0)))],
        [(w_out, pl.BlockSpec((f, d), lambda i, j, k: (0, 0)))],
        [(x, pl.BlockSpec((tm2, d), lambda i, j, k: (i, 0)))],
        jax.ShapeDtypeStruct((t, d), F32),
        pl.BlockSpec((tm2, d), lambda i, j, k: (i, 0)), None,
        a_fn=_swiglu, epi=epi)
    return x_out, (x, h, gu)


def _ffn_bwd(tag, dxo, saved, gain, w_in4, w_out, cfg):
    x, h, gu = saved
    t, d = x.shape
    fs = w_in4.shape[2]
    f = 2 * fs
    tm = _tile(t, cfg["dact_tm"])
    tk = _tile(t, cfg["dwout_tk"])

    def epi_act(acc, e, o):
        g = e[0].astype(F32)
        u = e[1].astype(F32)
        da = 0.5 * acc
        sig = _sigmoid(g)
        o[0][0] = (da * u * (sig * (1.0 + g * (1.0 - sig)))).astype(o[0].dtype)
        o[0][1] = (da * (g * sig)).astype(o[0].dtype)

    dgu = _mm(
        tag + "_dact", (t // tm, 2, 1), "nt",
        [(dxo, pl.BlockSpec((tm, d), lambda i, j, k: (i, 0)))],
        [(w_out, pl.BlockSpec((fs, d), lambda i, j, k: (j, 0)))],
        [(gu, pl.BlockSpec((None, tm, fs), lambda i, j, k: (0, i, j))),
         (gu, pl.BlockSpec((None, tm, fs), lambda i, j, k: (1, i, j)))],
        jax.ShapeDtypeStruct((2, t, f), _ACT),
        pl.BlockSpec((2, tm, fs), lambda i, j, k: (0, i, j)), None, epi=epi_act)

    def epi_half(acc, e, o):
        o[0][...] = (0.5 * acc).astype(o[0].dtype)

    dw_out = _mm(
        tag + "_dwout", (2, 1, t // tk), "tn",
        [(gu, pl.BlockSpec((None, tk, fs), lambda i, j, k: (0, k, i))),
         (gu, pl.BlockSpec((None, tk, fs), lambda i, j, k: (1, k, i)))],
        [(dxo, pl.BlockSpec((tk, d), lambda i, j, k: (k, 0)))], [],
        jax.ShapeDtypeStruct((f, d), _ACT),
        pl.BlockSpec((fs, d), lambda i, j, k: (i, 0)), (fs, d),
        a_fn=_swiglu, epi=epi_half)
    tm = _tile(t, cfg["dh_tm"])
    e_ops, shapes, specs = _norm_bwd_ops(x, dxo, gain, tm)
    dx, dgain = _mm(
        tag + "_dh", (t // tm, 1, 2), "nt",
        [(dgu, pl.BlockSpec((None, tm, f), lambda i, j, k: (k, i, 0)))],
        [(w_in4, pl.BlockSpec((2, d, fs), lambda i, j, k: (k, 0, 0)))], e_ops, shapes, specs, (tm, d),
        b_fn=lambda b: jnp.concatenate([b[0], b[1]], axis=1), epi=_norm_bwd_epi, m_carried=True)
    tk = _tile(t, cfg["dwin_tk"])
    dw_in4 = _mm(
        tag + "_dwin", (1, N_CHIPS, t // tk), "tn",
        [(h, pl.BlockSpec((tk, d), lambda i, j, k: (k, 0)))],
        [(dgu, pl.BlockSpec((None, tk, fs), lambda i, j, k: (j // 2, k, j % 2)))], [],
        jax.ShapeDtypeStruct((N_CHIPS, d, fs), _ACT),
        pl.BlockSpec((None, d, fs), lambda i, j, k: (j, 0, 0)), (d, fs))
    return dx, dgain.reshape(d), dw_in4, dw_out


_MIX_TILES = dict(tm=1024, dwout_tk=2048, dwin_tk=1024)


def _proj_in(tag, h, w_in):
    t, d = h.shape
    n = w_in.shape[1]
    tm = _tile(t, _MIX_TILES["tm"])
    return _mm(
        tag + "_in", (t // tm, 1, 1), "nn",
        [(h, pl.BlockSpec((tm, d), lambda i, j, k: (i, 0)))],
        [(w_in, pl.BlockSpec((d, n), lambda i, j, k: (0, 0)))], [],
        jax.ShapeDtypeStruct((t, n), _ACT),
        pl.BlockSpec((tm, n), lambda i, j, k: (i, 0)), None)


def _proj_out(tag, x, parts, w_out):
    t, d = x.shape
    tm = _tile(t, _MIX_TILES["tm"])

    def epi(acc, e, o):
        o[0][...] = e[0] + acc

    return _mm(
        tag + "_out", (t // tm, 1, 1), "nn",
        [(p, pl.BlockSpec((tm, p.shape[1]), lambda i, j, k: (i, 0))) for p in parts],
        [(w_out, pl.BlockSpec(w_out.shape, lambda i, j, k: (0, 0)))],
        [(x, pl.BlockSpec((tm, d), lambda i, j, k: (i, 0)))],
        jax.ShapeDtypeStruct((t, d), F32),
        pl.BlockSpec((tm, d), lambda i, j, k: (i, 0)), None, epi=epi)


def _proj_out_bwd(tag, dxo, parts, w_out):
    t, d = dxo.shape
    mix = w_out.shape[0]
    tm = _tile(t, _MIX_TILES["tm"])
    tk = _tile(t, _MIX_TILES["dwout_tk"])
    d_mix = _mm(
        tag + "_dmix", (t // tm, 1, 1), "nt",
        [(dxo, pl.BlockSpec((tm, d), lambda i, j, k: (i, 0)))],
        [(w_out, pl.BlockSpec((mix, d), lambda i, j, k: (0, 0)))], [],
        jax.ShapeDtypeStruct((t, mix), F32),
        pl.BlockSpec((tm, mix), lambda i, j, k: (i, 0)), None)
    dw_out = _mm(
        tag + "_dwout", (1, 1, t // tk), "tn",
        [(p, pl.BlockSpec((tk, p.shape[1]), lambda i, j, k: (k, 0))) for p in parts],
        [(dxo, pl.BlockSpec((tk, d), lambda i, j, k: (k, 0)))], [],
        jax.ShapeDtypeStruct((mix, d), _ACT),
        pl.BlockSpec((mix, d), lambda i, j, k: (0, 0)), (mix, d))
    return d_mix, dw_out


def _proj_in_bwd(tag, h, dparts, w_in, x, dres, gain):
    t, d = h.shape
    n = w_in.shape[1]
    tm = _tile(t, _MIX_TILES["tm"])
    tk = _tile(t, _MIX_TILES["dwin_tk"])
    e_ops, shapes, specs = _norm_bwd_ops(x, dres, gain, tm)
    dx, dgain = _mm(
        tag + "_dh", (t // tm, 1, 1), "nt",
        [(p, pl.BlockSpec((tm, p.shape[1]), lambda i, j, k: (i, 0))) for p in dparts],
        [(w_in, pl.BlockSpec((d, n), lambda i, j, k: (0, 0)))], e_ops, shapes, specs, None,
        epi=_norm_bwd_epi, m_carried=True)
    dw_in = _mm(
        tag + "_dwin", (1, 1, t // tk), "tn",
        [(h, pl.BlockSpec((tk, d), lambda i, j, k: (k, 0)))],
        [(p, pl.BlockSpec((tk, p.shape[1]), lambda i, j, k: (k, 0))) for p in dparts], [],
        jax.ShapeDtypeStruct((d, n), _ACT),
        pl.BlockSpec((d, n), lambda i, j, k: (0, 0)), (d, n))
    return dx, dgain.reshape(d), dw_in


def _shifted(pad_ref, val, s):
    pad_ref[pl.ds(HALO, s), :] = val
    return pad_ref[pl.ds(HALO - 1, s), :], pad_ref[pl.ds(HALO + 1, s), :]


def _zero_halo(pad_ref, s):
    z = jnp.zeros((HALO, pad_ref.shape[1]), F32)
    pad_ref[pl.ds(0, HALO), :] = z
    pad_ref[pl.ds(HALO + s, HALO), :] = z


def _conv_fwd(tag, proj, conv_w, nb, s):
    t = proj.shape[0]
    ncb = CONV_WIDTH // LANES

    def body(gb_ref, gc_ref, hc_ref, w_ref, a_ref, pad_ref):
        _zero_halo(pad_ref, s)
        cg = gc_ref[...].astype(F32) * hc_ref[...].astype(F32)
        prev, nxt = _shifted(pad_ref, cg, s)
        w = w_ref[...]
        conv = prev * w[0:1, :] + cg * w[1:2, :] + nxt * w[2:3, :]
        a_ref[...] = (gb_ref[...].astype(F32) * conv).astype(a_ref.dtype)

    def col(off):
        return pl.BlockSpec((s, LANES), lambda b, c: (b, off + c))

    return pl.pallas_call(
        body, name=tag + "_conv", grid=(nb, ncb),
        in_specs=[col(0), col(ncb), col(2 * ncb), pl.BlockSpec((3, LANES), lambda b, c: (0, c))],
        out_specs=col(0),
        out_shape=jax.ShapeDtypeStruct((t, CONV_WIDTH), _ACT),
        scratch_shapes=[pltpu.VMEM((s + 2 * HALO, LANES), F32)],
        compiler_params=_params("parallel", "parallel"),
    )(proj, proj, proj, conv_w)


def _conv_bwd(tag, proj, conv_w, d_mix, nb, s):
    t = proj.shape[0]
    ncb = CONV_WIDTH // LANES

    def body(gb_ref, gc_ref, hc_ref, w_ref, da_ref, dgb_ref, dgc_ref, dhc_ref, dw_ref, pad_ref):
        b = pl.program_id(1)
        _zero_halo(pad_ref, s)
        gb = gb_ref[...].astype(F32)
        gc = gc_ref[...].astype(F32)
        hc = hc_ref[...].astype(F32)
        w = w_ref[...]
        da = da_ref[...]
        cg = gc * hc
        prev, nxt = _shifted(pad_ref, cg, s)
        conv = prev * w[0:1, :] + cg * w[1:2, :] + nxt * w[2:3, :]
        dgb_ref[...] = (da * conv).astype(dgb_ref.dtype)
        dconv = da * gb
        dw = jnp.concatenate([
            jnp.sum(dconv * prev, axis=0, keepdims=True),
            jnp.sum(dconv * cg, axis=0, keepdims=True),
            jnp.sum(dconv * nxt, axis=0, keepdims=True)], axis=0)
        dprev, dnxt = _shifted(pad_ref, dconv, s)
        dcg = dnxt * w[0:1, :] + dconv * w[1:2, :] + dprev * w[2:3, :]
        dgc_ref[...] = (dcg * hc).astype(dgc_ref.dtype)
        dhc_ref[...] = (dcg * gc).astype(dhc_ref.dtype)

        @pl.when(b == 0)
        def _():
            dw_ref[...] = dw

        @pl.when(b > 0)
        def _():
            dw_ref[...] += dw

    def col(off):
        return pl.BlockSpec((s, LANES), lambda c, b: (b, off + c))

    wspec = pl.BlockSpec((3, LANES), lambda c, b: (0, c))
    act = jax.ShapeDtypeStruct((t, CONV_WIDTH), _ACT)
    return pl.pallas_call(
        body, name=tag + "_dconv", grid=(ncb, nb),
        in_specs=[col(0), col(ncb), col(2 * ncb), wspec, col(0)],
        out_specs=(col(0), col(0), col(0), wspec),
        out_shape=(act, act, act, jax.ShapeDtypeStruct((3, CONV_WIDTH), F32)),
        scratch_shapes=[pltpu.VMEM((s + 2 * HALO, LANES), F32)],
        compiler_params=_params("parallel", "arbitrary"),
    )(proj, proj, proj, conv_w, d_mix)


def _rope_tables(s):
    rows = s // GRID_W
    r_idx, c_idx = jnp.meshgrid(jnp.arange(rows), jnp.arange(GRID_W), indexing="ij")
    r_idx = r_idx.reshape(-1).astype(F32)
    c_idx = c_idx.reshape(-1).astype(F32)
    n_freq = HEAD_DIM // 4
    inv = ROPE_THETA ** (-jnp.arange(n_freq, dtype=F32) / n_freq)
    ang = jnp.concatenate([r_idx[:, None] * inv, c_idx[:, None] * inv], axis=-1)
    cos = jnp.repeat(jnp.cos(ang), 2, axis=1)
    sin = jnp.repeat(jnp.sin(ang), 2, axis=1)
    sign = jnp.where(jnp.arange(HEAD_DIM) % 2 == 0, -1.0, 1.0).astype(F32)
    return jnp.tile(cos, (1, LANES // HEAD_DIM)), jnp.tile(sin * sign, (1, LANES // HEAD_DIM))


def _head_ones():
    i = jnp.arange(LANES) // HEAD_DIM
    return (i[:, None] == i[None, :]).astype(jnp.bfloat16)


def _head_sum(v, ones):
    outs = []
    for j in range(v.shape[1] // LANES):
        c = v[:, j * LANES:(j + 1) * LANES]
        hi = c.astype(jnp.bfloat16)
        lo = (c - hi.astype(F32)).astype(jnp.bfloat16)
        outs.append(jnp.dot(hi, ones, preferred_element_type=F32) + jnp.dot(lo, ones, preferred_element_type=F32))
    return outs[0] if len(outs) == 1 else jnp.concatenate(outs, axis=1)


def _pair_swap(v):
    outs = []
    for j in range(v.shape[1] // LANES):
        c = v[:, j * LANES:(j + 1) * LANES]
        lane = lax.broadcasted_iota(jnp.int32, c.shape, 1)
        outs.append(jnp.where(lane % 2 == 0, pltpu.roll(c, LANES - 1, 1), pltpu.roll(c, 1, 1)))
    return outs[0] if len(outs) == 1 else jnp.concatenate(outs, axis=1)


def _wide(tab, width):
    return tab if width == LANES else jnp.concatenate([tab] * (width // LANES), axis=1)


def _qk_fwd(tag, proj, q_gain, k_gain, cos, sin, nb, s):
    t = proj.shape[0]
    tr = _tile(s, 512)
    ns = s // tr
    q_off = 3 * CONV_WIDTH // ATTN_WIDTH
    k_off = (3 * CONV_WIDTH + ATTN_WIDTH) // KV_WIDTH

    def body(q_ref, k_ref, qg_ref, kg_ref, cos_ref, sin_ref, ones_ref, qo_ref, ko_ref):
        ones = ones_ref[...]
        for src, g_ref, dst in ((q_ref, qg_ref, qo_ref), (k_ref, kg_ref, ko_ref)):
            v = src[...].astype(F32)
            w = v.shape[1]
            r = lax.rsqrt(_head_sum(v * v, ones) * (1.0 / HEAD_DIM) + EPS)
            vn = (v * r) * g_ref[...]
            dst[...] = (vn * _wide(cos_ref[...], w) + _pair_swap(vn) * _wide(sin_ref[...], w)).astype(dst.dtype)

    tab = pl.BlockSpec((tr, LANES), lambda i: (i % ns, 0))
    return pl.pallas_call(
        body, name=tag + "_qk", grid=(t // tr,),
        in_specs=[pl.BlockSpec((tr, ATTN_WIDTH), lambda i: (i, q_off)),
                  pl.BlockSpec((tr, KV_WIDTH), lambda i: (i, k_off)),
                  pl.BlockSpec((1, ATTN_WIDTH), lambda i: (0, 0)),
                  pl.BlockSpec((1, KV_WIDTH), lambda i: (0, 0)),
                  tab, tab, pl.BlockSpec((LANES, LANES), lambda i: (0, 0))],
        out_specs=(pl.BlockSpec((tr, ATTN_WIDTH), lambda i: (i, 0)),
                   pl.BlockSpec((tr, KV_WIDTH), lambda i: (i, 0))),
        out_shape=(jax.ShapeDtypeStruct((t, ATTN_WIDTH), _ACT), jax.ShapeDtypeStruct((t, KV_WIDTH), _ACT)),
        compiler_params=_params("parallel"),
    )(proj, proj, jnp.tile(q_gain, N_Q_HEADS).reshape(1, ATTN_WIDTH),
      jnp.tile(k_gain, N_KV_HEADS).reshape(1, KV_WIDTH), cos, sin, _head_ones())


def _qk_bwd(tag, proj, q_gain, k_gain, cos, sin, dq_rot, dk_rot, nb, s):
    t = proj.shape[0]
    tr = _tile(s, 512)
    ns = s // tr
    q_off = 3 * CONV_WIDTH // ATTN_WIDTH
    k_off = (3 * CONV_WIDTH + ATTN_WIDTH) // KV_WIDTH

    def body(q_ref, k_ref, qg_ref, kg_ref, cos_ref, sin_ref, ones_ref, dqr_ref, dkr_ref,
             dq_ref, dk_ref, dqg_ref, dkg_ref):
        i = pl.program_id(0)
        ones = ones_ref[...]
        for src, g_ref, dr_ref, dst, dg_ref in ((q_ref, qg_ref, dqr_ref, dq_ref, dqg_ref),
                                                (k_ref, kg_ref, dkr_ref, dk_ref, dkg_ref)):
            v = src[...].astype(F32)
            w = v.shape[1]
            r = lax.rsqrt(_head_sum(v * v, ones) * (1.0 / HEAD_DIM) + EPS)
            xhat = v * r
            dr = dr_ref[...]
            dvn = dr * _wide(cos_ref[...], w) + _pair_swap(dr * _wide(sin_ref[...], w))
            dgx = dvn * g_ref[...]
            m = _head_sum(dgx * xhat, ones) * (1.0 / HEAD_DIM)
            dst[...] = (r * (dgx - xhat * m)).astype(dst.dtype)
            part = jnp.sum(dvn * xhat, axis=0, keepdims=True)
            fold = part[:, 0:HEAD_DIM]
            for hh in range(1, w // HEAD_DIM):
                fold = fold + part[:, hh * HEAD_DIM:(hh + 1) * HEAD_DIM]

            @pl.when(i == 0)
            def _():
                dg_ref[...] = fold

            @pl.when(i > 0)
            def _():
                dg_ref[...] += fold

    tab = pl.BlockSpec((tr, LANES), lambda i: (i % ns, 0))
    qrow = pl.BlockSpec((tr, ATTN_WIDTH), lambda i: (i, 0))
    krow = pl.BlockSpec((tr, KV_WIDTH), lambda i: (i, 0))
    gvec = pl.BlockSpec((1, HEAD_DIM), lambda i: (0, 0))
    dq, dk, dqg, dkg = pl.pallas_call(
        body, name=tag + "_dqk", grid=(t // tr,),
        in_specs=[pl.BlockSpec((tr, ATTN_WIDTH), lambda i: (i, q_off)),
                  pl.BlockSpec((tr, KV_WIDTH), lambda i: (i, k_off)),
                  pl.BlockSpec((1, ATTN_WIDTH), lambda i: (0, 0)),
                  pl.BlockSpec((1, KV_WIDTH), lambda i: (0, 0)),
                  tab, tab, pl.BlockSpec((LANES, LANES), lambda i: (0, 0)), qrow, krow],
        out_specs=(qrow, krow, gvec, gvec),
        out_shape=(jax.ShapeDtypeStruct((t, ATTN_WIDTH), _ACT), jax.ShapeDtypeStruct((t, KV_WIDTH), _ACT),
                   jax.ShapeDtypeStruct((1, HEAD_DIM), F32), jax.ShapeDtypeStruct((1, HEAD_DIM), F32)),
        compiler_params=_params("arbitrary"),
    )(proj, proj, jnp.tile(q_gain, N_Q_HEADS).reshape(1, ATTN_WIDTH),
      jnp.tile(k_gain, N_KV_HEADS).reshape(1, KV_WIDTH), cos, sin, _head_ones(), dq_rot, dk_rot)
    return dq, dk, dqg.reshape(HEAD_DIM), dkg.reshape(HEAD_DIM)


def _head(v, h):
    return v[:, h * HEAD_DIM:(h + 1) * HEAD_DIM]


def _attn_fwd(tag, q, k, proj, nb, s):
    t = q.shape[0]
    tq = _tile(s, 256)
    nq = s // tq
    v_off = (3 * CONV_WIDTH + ATTN_WIDTH + KV_WIDTH) // KV_WIDTH
    scale = HEAD_DIM ** -0.5

    def body(q_ref, k_ref, v_ref, o_ref, lse_ref):
        qv = q_ref[...]
        kv = k_ref[...]
        vv = v_ref[...]
        for h in range(N_Q_HEADS):
            j = h // Q_PER_KV
            sc = _dot(_head(qv, h), _head(kv, j), "nt") * scale
            m = jnp.max(sc, axis=-1, keepdims=True)
            e = jnp.exp(sc - m)
            l = jnp.sum(e, axis=-1, keepdims=True)
            o = _dot(e, _head(vv, j)) * (1.0 / l)
            o_ref[:, h * HEAD_DIM:(h + 1) * HEAD_DIM] = o.astype(o_ref.dtype)
            lse_ref[:, h:h + 1] = m + jnp.log(l)

    return pl.pallas_call(
        body, name=tag + "_attn", grid=(nb, nq),
        in_specs=[pl.BlockSpec((tq, ATTN_WIDTH), lambda b, i: (b * nq + i, 0)),
                  pl.BlockSpec((s, KV_WIDTH), lambda b, i: (b, 0)),
                  pl.BlockSpec((s, KV_WIDTH), lambda b, i: (b, v_off))],
        out_specs=(pl.BlockSpec((tq, ATTN_WIDTH), lambda b, i: (b * nq + i, 0)),
                   pl.BlockSpec((tq, N_Q_HEADS), lambda b, i: (b * nq + i, 0))),
        out_shape=(jax.ShapeDtypeStruct((t, ATTN_WIDTH), _ACT), jax.ShapeDtypeStruct((t, N_Q_HEADS), F32)),
        compiler_params=_params("parallel", "parallel"),
    )(q, k, proj)


def _attn_bwd(tag, q, k, proj, o, lse, d_mix, nb, s):
    t = q.shape[0]
    tq = _tile(s, 256)
    nq = s // tq
    v_off = (3 * CONV_WIDTH + ATTN_WIDTH + KV_WIDTH) // KV_WIDTH
    scale = HEAD_DIM ** -0.5

    def body(q_ref, k_ref, v_ref, o_ref, lse_ref, do_ref, dq_ref, dk_ref, dv_ref):
        i = pl.program_id(1)

        @pl.when(i == 0)
        def _():
            dk_ref[...] = jnp.zeros_like(dk_ref)
            dv_ref[...] = jnp.zeros_like(dv_ref)

        qv = q_ref[...]
        kv = k_ref[...]
        vv = v_ref[...]
        ov = o_ref[...].astype(F32)
        dov = do_ref[...]
        lse = lse_ref[...]
        for h in range(N_Q_HEADS):
            j = h // Q_PER_KV
            cols = slice(j * HEAD_DIM, (j + 1) * HEAD_DIM)
            qh = _head(qv, h)
            kj = _head(kv, j)
            doh = _head(dov, h)
            sc = _dot(qh, kj, "nt") * scale
            p = jnp.exp(sc - lse[:, h:h + 1])
            dp = _dot(doh, _head(vv, j), "nt")
            delta = jnp.sum(doh * _head(ov, h), axis=-1, keepdims=True)
            ds = p * (dp - delta) * scale
            dv_ref[:, cols] += _dot(p, doh, "tn")
            dk_ref[:, cols] += _dot(ds, qh, "tn")
            dq_ref[:, h * HEAD_DIM:(h + 1) * HEAD_DIM] = _dot(ds, kj)

    qrow = pl.BlockSpec((tq, ATTN_WIDTH), lambda b, i: (b * nq + i, 0))
    kvrow = pl.BlockSpec((s, KV_WIDTH), lambda b, i: (b, 0))
    return pl.pallas_call(
        body, name=tag + "_dattn", grid=(nb, nq),
        in_specs=[qrow, kvrow, pl.BlockSpec((s, KV_WIDTH), lambda b, i: (b, v_off)), qrow,
                  pl.BlockSpec((tq, N_Q_HEADS), lambda b, i: (b * nq + i, 0)),
                  pl.BlockSpec((tq, ATTN_WIDTH), lambda b, i: (b * nq + i, 1))],
        out_specs=(qrow, kvrow, kvrow),
        out_shape=(jax.ShapeDtypeStruct((t, ATTN_WIDTH), F32), jax.ShapeDtypeStruct((t, KV_WIDTH), F32),
                   jax.ShapeDtypeStruct((t, KV_WIDTH), F32)),
        compiler_params=_params("parallel", "arbitrary"),
    )(q, k, proj, o, lse, d_mix)


def _even_fwd(tag, x, p, cos, sin, nb, s):
    h = _rmsnorm_fwd(tag + "_norm", x, p["norm"])
    proj = _proj_in(tag, h, p["w_in"])
    a = _conv_fwd(tag, proj, p["conv_w"], nb, s)
    q, k = _qk_fwd(tag, proj, p["q_gain"], p["k_gain"], cos, sin, nb, s)
    o, lse = _attn_fwd(tag, q, k, proj, nb, s)
    x_out = _proj_out(tag, x, [a, o], p["w_out"])
    return x_out, (x, h, proj, a, q, k, o, lse)


def _even_bwd(tag, dxo, saved, p, cos, sin, nb, s):
    x, h, proj, a, q, k, o, lse = saved
    d_mix, dw_out = _proj_out_bwd(tag, dxo, [a, o], p["w_out"])
    dgb, dgc, dhc, dconv_w = _conv_bwd(tag, proj, p["conv_w"], d_mix, nb, s)
    dq_rot, dk_rot, dv = _attn_bwd(tag, q, k, proj, o, lse, d_mix, nb, s)
    dq, dk, dq_gain, dk_gain = _qk_bwd(tag, proj, p["q_gain"], p["k_gain"], cos, sin, dq_rot, dk_rot, nb, s)
    dx, dnorm, dw_in = _proj_in_bwd(tag, h, [dgb, dgc, dhc, dq, dk, dv], p["w_in"], x, dxo, p["norm"])
    grads = dict(norm=dnorm, w_in=dw_in, w_out=dw_out, conv_w=dconv_w, q_gain=dq_gain, k_gain=dk_gain)
    return dx, grads


def _window(pad_ref, val, r, s):
    pad_ref[pl.ds(HALO, s), :] = val
    acc = val
    for d in range(1, r + 1):
        acc = acc + pad_ref[pl.ds(HALO - d, s), :] + pad_ref[pl.ds(HALO + d, s), :]
    return acc


def _count(r, s):
    t = lax.broadcasted_iota(jnp.int32, (s, 1), 0)
    return (jnp.minimum(t + r, s - 1) - jnp.maximum(t - r, 0) + 1).astype(F32)


def _sgu_chunk(u_ref, v_ref, norm, ws_ref, bt, rows):
    uu = u_ref[rows, :].astype(F32)
    vv = v_ref[rows, :].astype(F32)
    gu = _gelu(uu)
    gv = _gelu(vv)
    r = lax.rsqrt(jnp.mean(gv * gv, axis=-1, keepdims=True) + EPS)
    xhat = gv * r
    vn = xhat * norm
    mixed = []
    for g in range(N_GROUPS):
        cols = slice(g * SGU_GROUP, (g + 1) * SGU_GROUP)
        mixed.append(_dot(ws_ref[g], vn[:, cols]) + bt[:, g:g + 1])
    return uu, vv, gu, r, xhat, vn, mixed


def _odd_core_fwd(tag, proj, p, nb, s):
    t = proj.shape[0]
    nchunk = s // SGU_CHUNK

    def body(p_ref, u_ref, v_ref, pw_ref, ps_ref, sn_ref, ws_ref, bt_ref, mix_ref, pad_ref):
        _zero_halo(pad_ref, s)
        for g, r in enumerate(POOL_RADII):
            cols = slice(g * POOL_GROUP, (g + 1) * POOL_GROUP)
            pg = p_ref[:, cols].astype(F32)
            pooled = _window(pad_ref, pg, r, s) / _count(r, s) - pg
            mix_ref[:, cols] = (_dot(pooled, pw_ref[g]) * ps_ref[:, cols]).astype(mix_ref.dtype)
        norm = sn_ref[...]
        bt = bt_ref[...]

        def chunk(n, carry):
            rows = pl.ds(pl.multiple_of(n * SGU_CHUNK, SGU_CHUNK), SGU_CHUNK)
            _, _, gu, _, _, _, mixed = _sgu_chunk(u_ref, v_ref, norm, ws_ref, bt, rows)
            for g in range(N_GROUPS):
                cols = slice(g * SGU_GROUP, (g + 1) * SGU_GROUP)
                mix_ref[rows, HALF + g * SGU_GROUP:HALF + (g + 1) * SGU_GROUP] = (
                    gu[:, cols] * mixed[g]).astype(mix_ref.dtype)
            return carry

        lax.fori_loop(0, nchunk, chunk, 0)

    def col(j):
        return pl.BlockSpec((s, HALF), lambda b: (b, j))

    def whole(a):
        return pl.BlockSpec(a.shape, lambda b: (0,) * a.ndim)

    consts = [p["pool_w"], p["pool_scale"].reshape(1, HALF), p["sgu_norm"].reshape(1, HALF),
              p["sgu_w"], p["sgu_b"].T]
    return pl.pallas_call(
        body, name=tag + "_core", grid=(nb,),
        in_specs=[col(0), col(1), col(2)] + [whole(a) for a in consts],
        out_specs=pl.BlockSpec((s, D_MODEL), lambda b: (b, 0)),
        out_shape=jax.ShapeDtypeStruct((t, D_MODEL), _ACT),
        scratch_shapes=[pltpu.VMEM((s + 2 * HALO, POOL_GROUP), F32)],
        compiler_params=_params("parallel"),
    )(proj, proj, proj, *consts)


def _odd_core_bwd(tag, proj, p, d_mix, nb, s):
    t = proj.shape[0]
    nchunk = s // SGU_CHUNK

    def body(p_ref, u_ref, v_ref, pw_ref, ps_ref, sn_ref, ws_ref, bt_ref, dm_ref,
             dproj_ref, dpw_ref, dps_ref, dsn_ref, dws_ref, dbt_ref, pad_ref):
        b = pl.program_id(0)

        @pl.when(b == 0)
        def _():
            dpw_ref[...] = jnp.zeros_like(dpw_ref)
            dps_ref[...] = jnp.zeros_like(dps_ref)
            dsn_ref[...] = jnp.zeros_like(dsn_ref)
            dws_ref[...] = jnp.zeros_like(dws_ref)
            dbt_ref[...] = jnp.zeros_like(dbt_ref)

        _zero_halo(pad_ref, s)
        for g, r in enumerate(POOL_RADII):
            cols = slice(g * POOL_GROUP, (g + 1) * POOL_GROUP)
            pg = p_ref[:, cols].astype(F32)
            cnt = _count(r, s)
            pooled = _window(pad_ref, pg, r, s) / cnt - pg
            c_pre = _dot(pooled, pw_ref[g])
            dc = dm_ref[:, cols]
            dps_ref[:, cols] += jnp.sum(dc * c_pre, axis=0, keepdims=True)
            dcp = dc * ps_ref[:, cols]
            dpw_ref[g] += _dot(pooled, dcp, "tn")
            dpooled = _dot(dcp, pw_ref[g], "nt")
            dproj_ref[:, cols] = (_window(pad_ref, dpooled / cnt, r, s) - dpooled).astype(dproj_ref.dtype)
        norm = sn_ref[...]
        bt = bt_ref[...]

        def chunk(n, carry):
            rows = pl.ds(pl.multiple_of(n * SGU_CHUNK, SGU_CHUNK), SGU_CHUNK)
            uu, vv, gu, r, xhat, vn, mixed = _sgu_chunk(u_ref, v_ref, norm, ws_ref, bt, rows)
            dd = dm_ref[rows, HALF:D_MODEL]
            dgu, dvn = [], []
            for g in range(N_GROUPS):
                cols = slice(g * SGU_GROUP, (g + 1) * SGU_GROUP)
                dgu.append(dd[:, cols] * mixed[g])
                dmx = dd[:, cols] * gu[:, cols]
                dbt_ref[:, g:g + 1] += jnp.sum(dmx, axis=-1, keepdims=True)
                dws_ref[g] += _dot(dmx, vn[:, cols], "nt")
                dvn.append(_dot(ws_ref[g], dmx, "tn"))
            dgu = jnp.concatenate(dgu, axis=1)
            dvn = jnp.concatenate(dvn, axis=1)
            dsn_ref[...] += jnp.sum(dvn * xhat, axis=0, keepdims=True)
            dgx = dvn * norm
            m = jnp.mean(dgx * xhat, axis=-1, keepdims=True)
            dgv = r * (dgx - xhat * m)
            dproj_ref[rows, HALF:2 * HALF] = (dgu * _gelu_grad(uu)).astype(dproj_ref.dtype)
            dproj_ref[rows, 2 * HALF:3 * HALF] = (dgv * _gelu_grad(vv)).astype(dproj_ref.dtype)
            return carry

        lax.fori_loop(0, nchunk, chunk, 0)

    def col(j):
        return pl.BlockSpec((s, HALF), lambda b: (b, j))

    def whole(a):
        return pl.BlockSpec(a.shape, lambda b: (0,) * a.ndim)

    consts = [p["pool_w"], p["pool_scale"].reshape(1, HALF), p["sgu_norm"].reshape(1, HALF),
              p["sgu_w"], p["sgu_b"].T]
    gshapes = [jax.ShapeDtypeStruct(a.shape, F32) for a in consts]
    dproj, dpw, dps, dsn, dws, dbt = pl.pallas_call(
        body, name=tag + "_dcore", grid=(nb,),
        in_specs=[col(0), col(1), col(2)] + [whole(a) for a in consts]
        + [pl.BlockSpec((s, D_MODEL), lambda b: (b, 0))],
        out_specs=[pl.BlockSpec((s, 3 * HALF), lambda b: (b, 0))] + [whole(a) for a in consts],
        out_shape=[jax.ShapeDtypeStruct((t, 3 * HALF), _ACT)] + gshapes,
        scratch_shapes=[pltpu.VMEM((s + 2 * HALO, POOL_GROUP), F32)],
        compiler_params=_params("arbitrary"),
    )(proj, proj, proj, *consts, d_mix)
    return dproj, dict(pool_w=dpw, pool_scale=dps.reshape(HALF), sgu_norm=dsn.reshape(HALF), sgu_w=dws, sgu_b=dbt.T)


def _odd_fwd(tag, x, p, nb, s):
    h = _rmsnorm_fwd(tag + "_norm", x, p["norm"])
    proj = _proj_in(tag, h, p["w_in"])
    mix = _odd_core_fwd(tag, proj, p, nb, s)
    x_out = _proj_out(tag, x, [mix], p["w_out"])
    return x_out, (x, h, proj, mix)


def _odd_bwd(tag, dxo, saved, p, nb, s):
    x, h, proj, mix = saved
    d_mix, dw_out = _proj_out_bwd(tag, dxo, [mix], p["w_out"])
    dproj, grads = _odd_core_bwd(tag, proj, p, d_mix, nb, s)
    dx, dnorm, dw_in = _proj_in_bwd(tag, h, [dproj], p["w_in"], x, dxo, p["norm"])
    grads.update(norm=dnorm, w_in=dw_in, w_out=dw_out)
    return dx, grads


def _local_step(x3, target3, depth, weights_of, final_norm, grads_done):
    nb, s, d = x3.shape
    t = nb * s
    x = x3.reshape(t, d)
    target = target3.reshape(t, d)
    cos, sin = _rope_tables(s)
    saved, ws = [], []
    for l in range(depth):
        w1 = weights_of(l, "ffn1", x)
        x, s1 = _ffn_fwd(f"l{l}_ffn1", x, w1["norm"], w1["w_in4"], w1["w_out"], _ffn_tiles(l, 1))
        wm = weights_of(l, "mix", x)
        if l % 2 == 0:
            x, s2 = _even_fwd(f"l{l}_ev", x, wm, cos, sin, nb, s)
        else:
            x, s2 = _odd_fwd(f"l{l}_od", x, wm, nb, s)
        w2 = weights_of(l, "ffn2", x)
        x, s3 = _ffn_fwd(f"l{l}_ffn2", x, w2["norm"], w2["w_in4"], w2["w_out"], _ffn_tiles(l, 2))
        saved.append((s1, s2, s3))
        ws.append((w1, wm, w2))
    loss, dx, dfinal = _final_loss("final_loss", x, final_norm, target)
    zero = 0.0
    for l in reversed(range(depth)):
        s1, s2, s3 = saved[l]
        w1, wm, w2 = ws[l]
        dx, dn, dwi, dwo = _ffn_bwd(f"l{l}_ffn2", dx, s3, w2["norm"] + zero, w2["w_in4"], w2["w_out"], _ffn_tiles(l, 2))
        zero = grads_done(l, "ffn2", dict(norm=dn, w_in4=dwi, w_out=dwo), dx)
        wm = dict(wm, norm=wm["norm"] + zero)
        if l % 2 == 0:
            dx, gm = _even_bwd(f"l{l}_ev", dx, s2, wm, cos, sin, nb, s)
        else:
            dx, gm = _odd_bwd(f"l{l}_od", dx, s2, wm, nb, s)
        zero = grads_done(l, "mix", gm, dx)
        dx, dn, dwi, dwo = _ffn_bwd(f"l{l}_ffn1", dx, s1, w1["norm"] + zero, w1["w_in4"], w1["w_out"], _ffn_tiles(l, 1))
        zero = grads_done(l, "ffn1", dict(norm=dn, w_in4=dwi, w_out=dwo), dx)
    return loss, dx.reshape(nb, s, d), dfinal


_HBM = pl.BlockSpec(memory_space=pltpu.HBM)


def _place():
    x, y, c = lax.axis_index("x"), lax.axis_index("y"), lax.axis_index("c")
    chips = [(1 - x, y), (x, 1 - y), (1 - x, 1 - y)]
    return x, y, c, chips


def _remote(src, dst, send_sem, recv_sem, to):
    return pltpu.make_async_remote_copy(src_ref=src, dst_ref=dst, send_sem=send_sem, recv_sem=recv_sem,
                                        device_id=to, device_id_type=_MESH)


def _gather_shards(arrs, small):
    n = len(arrs)
    own = 6

    def body(*refs):
        ins, sm_in = refs[:n], refs[n]
        outs, sm_out = refs[n + 1:2 * n + 1], refs[2 * n + 1]
        send, recv = refs[2 * n + 2:]
        x, y, c, chips = _place()
        k = 2 * x + y
        sib = (x, y, 1 - c)
        started = []
        for a in range(n + 1):
            src, dst = (ins[a], outs[a]) if a < n else (sm_in, sm_out)
            cp = _remote(src, dst.at[k], send.at[a, own], recv.at[a, own], sib)
            cp.start()
            started.append(cp)
            if a < n:
                h = src.shape[0] // 2
                mine = pl.ds(c * h, h)
                src_part, dst_part = src.at[mine], dst.at[k, mine]
            else:
                src_part, dst_part = src, dst.at[k]
            for j, chip in enumerate(chips):
                cp = _remote(src_part, dst_part, send.at[a, j], recv.at[a, j], (*chip, c))
                cp.start()
                started.append(cp)
        for a in range(n):
            h = ins[a].shape[0] // 2
            mine = pl.ds(c * h, h)
            for j, (px, py) in enumerate(chips):
                landed = outs[a].at[2 * px + py, mine]
                _remote(landed, landed, send.at[a, j], recv.at[a, j], (px, py, c)).wait_recv()
                cp = _remote(landed, landed, send.at[a, 3 + j], recv.at[a, 3 + j], sib)
                cp.start()
                started.append(cp)
        for a in range(n):
            h = ins[a].shape[0] // 2
            other = pl.ds((1 - c) * h, h)
            for j, (px, py) in enumerate(chips):
                passed = outs[a].at[2 * px + py, other]
                _remote(passed, passed, send.at[a, 3 + j], recv.at[a, 3 + j], sib).wait_recv()
        for j, (px, py) in enumerate(chips):
            landed = sm_out.at[2 * px + py]
            _remote(landed, landed, send.at[n, j], recv.at[n, j], (px, py, c)).wait_recv()
        for a in range(n + 1):
            filled = (outs[a] if a < n else sm_out).at[k]
            _remote(filled, filled, send.at[a, own], recv.at[a, own], sib).wait_recv()
        for cp in started:
            cp.wait_send()

    outs = pl.pallas_call(
        body, name="gather_shards",
        in_specs=[_HBM] * (n + 1), out_specs=[_HBM] * (n + 1),
        out_shape=[jax.ShapeDtypeStruct((N_CHIPS,) + a.shape, a.dtype) for a in list(arrs) + [small]],
        scratch_shapes=[pltpu.SemaphoreType.DMA((n + 1, 7)), pltpu.SemaphoreType.DMA((n + 1, 7))],
    )(*arrs, small)
    return outs[:n], outs[n]


def _swap_halves(name, grads):
    n = len(grads)

    def body(*refs):
        ins, outs = refs[:n], refs[n:2 * n]
        send, recv = refs[2 * n:]
        x, y, c, _ = _place()
        sib = (x, y, 1 - c)
        cps = []
        for a in range(n):
            h = ins[a].shape[1] // 2
            cp = _remote(ins[a].at[:, pl.ds((1 - c) * h, h)], outs[a], send.at[a], recv.at[a], sib)
            cp.start()
            cps.append(cp)
        for cp in cps:
            cp.wait()

    return pl.pallas_call(
        body, name=name,
        in_specs=[_HBM] * n, out_specs=[_HBM] * n,
        out_shape=[jax.ShapeDtypeStruct((g.shape[0], g.shape[1] // 2) + g.shape[2:], g.dtype) for g in grads],
        scratch_shapes=[pltpu.SemaphoreType.DMA((n,)), pltpu.SemaphoreType.DMA((n,))],
    )(*grads)


_SEM = pl.BlockSpec(memory_space=pltpu.SEMAPHORE)
_EFFECT = pltpu.SideEffectType.DATAFLOW_SIDE_EFFECTING


def _gather_plan(src, land, k, c, chips, sib):
    plan = [(src, land.at[k], (px, py, c), land.at[2 * px + py]) for px, py in chips]
    return plan + [(src, land.at[k], sib, land.at[k])]


def _scatter_plan(src, land, k, c, chips, sib):
    return [(src.at[2 * px + py], land.at[k], (px, py, c), land.at[2 * px + py]) for px, py in chips]


def _split_start(name, plan, ncopy, srcs, after):
    n = len(srcs)
    lands = [pltpu.with_memory_space_constraint(lax.empty((N_CHIPS,) + a.shape[-2:], a.dtype), pltpu.HBM) for a in srcs]

    def body(*refs):
        src_refs, land_refs = refs[1:1 + n], refs[1 + n:1 + 2 * n]
        send, recv, token = refs[1 + 2 * n], refs[2 + 2 * n], refs[-1]
        x, y, c, chips = _place()
        for i in range(n):
            for j, (src, dst, peer, _) in enumerate(plan(src_refs[i], land_refs[i], 2 * x + y, c, chips, (x, y, 1 - c))):
                _remote(src, dst, send.at[i * ncopy + j], recv.at[i * ncopy + j], peer).start()
        token[...] = jnp.zeros_like(token)

    outs = pl.pallas_call(
        body, name=name,
        in_specs=[_ANY] + [_HBM] * (2 * n),
        out_specs=[_SEM, _SEM] + [_HBM] * (2 * n) + [_VMEM],
        out_shape=[pltpu.SemaphoreType.DMA((n * ncopy,)), pltpu.SemaphoreType.DMA((n * ncopy,))]
        + [pltpu.HBM(a.shape, a.dtype) for a in list(srcs) + lands] + [jax.ShapeDtypeStruct((8, LANES), F32)],
        input_output_aliases={1 + i: 2 + i for i in range(2 * n)},
        compiler_params=pltpu.CompilerParams(has_side_effects=_EFFECT),
    )(after, *[pltpu.with_memory_space_constraint(a, pltpu.HBM) for a in srcs], *lands)
    return outs[0], outs[1], outs[2:2 + n], outs[2 + n:2 + 2 * n], outs[-1]


def _split_wait(name, plan, started, after):
    send, recv, srcs, lands = started
    n = len(srcs)
    ncopy = send.shape[0] // n

    def body(*refs):
        src_refs, land_refs = refs[:n], refs[n:2 * n]
        send, recv = refs[2 * n], refs[2 * n + 1]
        x, y, c, chips = _place()
        for i in range(n):
            for j, (src, _, peer, landed) in enumerate(plan(src_refs[i], land_refs[i], 2 * x + y, c, chips, (x, y, 1 - c))):
                cp = _remote(src, landed, send.at[i * ncopy + j], recv.at[i * ncopy + j], peer)
                cp.wait_send()
                cp.wait_recv()

    outs = pl.pallas_call(
        body, name=name,
        in_specs=[_HBM] * (2 * n) + [_SEM, _SEM, _ANY],
        out_specs=[_HBM] * (2 * n),
        out_shape=[pltpu.HBM(a.shape, a.dtype) for a in list(srcs) + list(lands)],
        input_output_aliases={i: i for i in range(2 * n)},
        compiler_params=pltpu.CompilerParams(has_side_effects=_EFFECT),
    )(*srcs, *lands, send, recv, after)
    return outs[:n], outs[n:]


def _join_halves(name, bufs, layers):
    n = len(bufs)

    def body(*refs):
        ins, outs = refs[:n], refs[n:2 * n]
        send, recv = refs[2 * n:]
        x, y, c, _ = _place()
        sib = (x, y, 1 - c)
        cps = []
        for a in range(n):
            h = ins[a].shape[1] // 2
            mine = pl.ds(c * h, h)
            cp = _remote(ins[a].at[layers[a], mine], outs[a].at[layers[a], mine], send.at[a], recv.at[a], sib)
            cp.start()
            cps.append(cp)
        for a in range(n):
            h = ins[a].shape[1] // 2
            theirs = outs[a].at[layers[a], pl.ds((1 - c) * h, h)]
            _remote(theirs, theirs, send.at[a], recv.at[a], sib).wait_recv()
        for cp in cps:
            cp.wait_send()

    return pl.pallas_call(
        body, name=name,
        in_specs=[_HBM] * n, out_specs=[_HBM] * n,
        out_shape=[jax.ShapeDtypeStruct(p.shape, p.dtype) for p in bufs],
        input_output_aliases={a: a for a in range(n)},
        scratch_shapes=[pltpu.SemaphoreType.DMA((n,)), pltpu.SemaphoreType.DMA((n,))],
    )(*bufs)


def _allreduce_small(buf):
    rows = buf.shape[0]

    def body(in_ref, out_ref, land_ref, send, recv):
        x, y, c, _ = _place()
        me = 4 * x + 2 * y + c
        land_ref[me] = in_ref[...]
        peers = []
        for r in range(1, N_DEV):
            peers.append((1 - x if r & 4 else x, 1 - y if r & 2 else y, 1 - c if r & 1 else c))
        cps = []
        for r, peer in enumerate(peers):
            cp = _remote(in_ref, land_ref.at[me], send.at[r], recv.at[r], peer)
            cp.start()
            cps.append(cp)
        for r, (px, py, pc) in enumerate(peers):
            landed = land_ref.at[4 * px + 2 * py + pc]
            _remote(landed, landed, send.at[r], recv.at[r], (px, py, pc)).wait_recv()
        for cp in cps:
            cp.wait_send()
        acc = land_ref[0]
        for d in range(1, N_DEV):
            acc = acc + land_ref[d]
        out_ref[...] = acc

    return pl.pallas_call(
        body, name="allreduce_small",
        in_specs=[_VMEM], out_specs=_VMEM,
        out_shape=jax.ShapeDtypeStruct(buf.shape, F32),
        scratch_shapes=[pltpu.VMEM((N_DEV, rows, LANES), F32), pltpu.SemaphoreType.DMA((N_DEV - 1,)),
                        pltpu.SemaphoreType.DMA((N_DEV - 1,))],
        compiler_params=pltpu.CompilerParams(vmem_limit_bytes=_VMEM_LIMIT),
    )(buf)


def _div_tile(n, cap, mult):
    best = None
    for d in range(mult, min(n, cap) + 1, mult):
        if n % d == 0:
            best = d
    return best if best is not None else n


def _add_sibling(name, grad, got, c):
    nk, hr, cc = got.shape
    tr = _div_tile(hr, 512, 16)
    nt = hr // tr

    def body(c_ref, g_ref, o_ref, s_ref):
        s_ref[...] = (g_ref[...].astype(F32) + o_ref[...].astype(F32)).astype(s_ref.dtype)

    blk = (None, tr, cc)
    return pl.pallas_call(
        body, name=name,
        grid_spec=pltpu.PrefetchScalarGridSpec(
            num_scalar_prefetch=1, grid=(nk, nt),
            in_specs=[pl.BlockSpec(blk, lambda i, q, c_ref: (i, c_ref[0] * nt + q, 0)),
                      pl.BlockSpec(blk, lambda i, q, c_ref: (i, q, 0))],
            out_specs=pl.BlockSpec(blk, lambda i, q, c_ref: (i, q, 0))),
        out_shape=jax.ShapeDtypeStruct(got.shape, got.dtype),
        compiler_params=_params("parallel", "parallel"),
    )(c, grad, got)


def _add_chips(name, mine, got, place, buf, l):
    nk, hr, cc = got.shape
    tr = _div_tile(hr, 512, 16)
    nt = hr // tr

    def body(*refs):
        acc = refs[1][...].astype(F32)
        for q in range(1, nk):
            acc = acc + refs[1 + q][...].astype(F32)
        refs[2 + nk][...] = acc

    def part(q):
        return pl.BlockSpec((None, tr, cc), lambda i, p_ref: ((p_ref[0] + q) % nk, i, 0))

    return pl.pallas_call(
        body, name=name,
        grid_spec=pltpu.PrefetchScalarGridSpec(
            num_scalar_prefetch=1, grid=(nt,),
            in_specs=[part(q) for q in range(nk)] + [_ANY],
            out_specs=pl.BlockSpec((None, tr, cc), lambda i, p_ref: (l, p_ref[1] * nt + i, 0))),
        out_shape=jax.ShapeDtypeStruct(buf.shape, F32),
        input_output_aliases={1 + nk: 0},
        compiler_params=_params("parallel"),
    )(place, mine, *([got] * (nk - 1)), buf)


def _adamw(name, w, g, m, v):
    shape = w.shape
    cols = shape[-1]
    rows = w.size // cols
    tr = rows if rows * cols <= 2 ** 18 else _div_tile(rows, max(8, 2 ** 18 // cols), 8)
    c1 = 1.0 - ADAM_B1 ** ADAM_STEP
    c2 = 1.0 - ADAM_B2 ** ADAM_STEP

    def body(w_ref, g_ref, m_ref, v_ref, d_ref, mo_ref, vo_ref):
        gg = g_ref[...]
        mn = ADAM_B1 * m_ref[...] + (1.0 - ADAM_B1) * gg
        vn = ADAM_B2 * v_ref[...] + (1.0 - ADAM_B2) * (gg * gg)
        d_ref[...] = -ADAM_LR * ((mn / c1) / (jnp.sqrt(vn / c2) + ADAM_EPS) + ADAM_WD * w_ref[...])
        mo_ref[...] = mn
        vo_ref[...] = vn

    blk = pl.BlockSpec((tr, cols), lambda i: (i, 0))
    sds = jax.ShapeDtypeStruct((rows, cols), F32)
    outs = pl.pallas_call(
        body, name=name, grid=(rows // tr,),
        in_specs=[blk] * 4, out_specs=(blk,) * 3, out_shape=(sds,) * 3,
        compiler_params=_params("parallel"),
    )(*[a.reshape(rows, cols) for a in (w, g, m, v)])
    return [o.reshape(shape) for o in outs]


_WEIGHTS = ["ffn1_norm", "ffn1_w_in", "ffn1_w_out", "mix_norm", "ffn2_norm", "ffn2_w_in", "ffn2_w_out",
            "ev_w_in", "ev_conv_w", "ev_q_norm", "ev_k_norm", "ev_w_out", "od_w_in", "od_pool_w",
            "od_pool_scale", "od_sgu_norm", "od_sgu_w", "od_sgu_b", "od_w_out", "final_norm"]
_BIG = ["ffn1_w_in", "ffn1_w_out", "ffn2_w_in", "ffn2_w_out", "ev_w_in", "ev_w_out", "od_w_in", "od_w_out"]
_SMALL_SHARDED = ["ev_conv_w", "od_pool_scale", "od_sgu_norm"]


def _pad_rows(a, mult=8):
    pad = (-a.shape[0]) % mult
    return a if pad == 0 else jnp.concatenate([a, jnp.zeros((pad,) + a.shape[1:], a.dtype)], axis=0)


def _join_cols(g):
    return g.transpose(1, 0, 2).reshape(g.shape[1], N_CHIPS * g.shape[2])


def _split_cols(w):
    return w.reshape(w.shape[0], N_CHIPS, w.shape[1] // N_CHIPS).transpose(1, 0, 2)


def kernel(x, ffn1_norm, ffn1_w_in, ffn1_w_out, mix_norm, ffn2_norm, ffn2_w_in, ffn2_w_out, ev_w_in, ev_conv_w,
           ev_q_norm, ev_k_norm, ev_w_out, od_w_in, od_pool_w, od_pool_scale, od_sgu_norm, od_sgu_w, od_sgu_b,
           od_w_out, final_norm, loss_target, m_ffn1_norm, m_ffn1_w_in, m_ffn1_w_out, m_mix_norm, m_ffn2_norm,
           m_ffn2_w_in, m_ffn2_w_out, m_ev_w_in, m_ev_conv_w, m_ev_q_norm, m_ev_k_norm, m_ev_w_out, m_od_w_in,
           m_od_pool_w, m_od_pool_scale, m_od_sgu_norm, m_od_sgu_w, m_od_sgu_b, m_od_w_out, m_final_norm, v_ffn1_norm,
           v_ffn1_w_in, v_ffn1_w_out, v_mix_norm, v_ffn2_norm, v_ffn2_w_in, v_ffn2_w_out, v_ev_w_in, v_ev_conv_w,
           v_ev_q_norm, v_ev_k_norm, v_ev_w_out, v_od_w_in, v_od_pool_w, v_od_pool_scale, v_od_sgu_norm, v_od_sgu_w,
           v_od_sgu_b, v_od_w_out, v_final_norm):
    return _step(x, ffn1_norm, ffn1_w_in, ffn1_w_out, mix_norm, ffn2_norm, ffn2_w_in, ffn2_w_out, ev_w_in, ev_conv_w,
                 ev_q_norm, ev_k_norm, ev_w_out, od_w_in, od_pool_w, od_pool_scale, od_sgu_norm, od_sgu_w, od_sgu_b,
                 od_w_out, final_norm, loss_target, m_ffn1_norm, m_ffn1_w_in, m_ffn1_w_out, m_mix_norm, m_ffn2_norm,
                 m_ffn2_w_in, m_ffn2_w_out, m_ev_w_in, m_ev_conv_w, m_ev_q_norm, m_ev_k_norm, m_ev_w_out, m_od_w_in,
                 m_od_pool_w, m_od_pool_scale, m_od_sgu_norm, m_od_sgu_w, m_od_sgu_b, m_od_w_out, m_final_norm,
                 v_ffn1_norm, v_ffn1_w_in, v_ffn1_w_out, v_mix_norm, v_ffn2_norm, v_ffn2_w_in, v_ffn2_w_out,
                 v_ev_w_in, v_ev_conv_w, v_ev_q_norm, v_ev_k_norm, v_ev_w_out, v_od_w_in, v_od_pool_w,
                 v_od_pool_scale, v_od_sgu_norm, v_od_sgu_w, v_od_sgu_b, v_od_w_out, v_final_norm)


def _step(*args):
    nw = len(_WEIGHTS)
    x = args[0]
    w = dict(zip(_WEIGHTS, args[1:1 + nw]))
    target = args[1 + nw]
    m = dict(zip(_WEIGHTS, args[2 + nw:2 + 2 * nw]))
    v = dict(zip(_WEIGHTS, args[2 + 2 * nw:2 + 3 * nw]))
    depth = w["ffn1_norm"].shape[0]
    n_even, n_odd = w["ev_w_in"].shape[0], w["od_w_in"].shape[0]
    chip = 2 * lax.axis_index("x") + lax.axis_index("y")
    place = jnp.stack([chip, lax.axis_index("c")]).astype(jnp.int32)
    core = place[1:2]

    def sharded(l, block):
        if block == "mix":
            block = "ev" if l % 2 == 0 else "od"
            return [(block + "_w_in", l // 2), (block + "_w_out", l // 2)]
        return [(block + "_w_in", l), (block + "_w_out", l)]

    def shards(group):
        return [w[n][i].astype(_ACT) for l, block in group for n, i in sharded(l, block)]

    small_rows = [w["ev_conv_w"].reshape(3 * n_even, LANES), w["od_pool_scale"], w["od_sgu_norm"]]
    first, small = _gather_shards(shards([(0, "ffn1"), (0, "mix")]), _pad_rows(jnp.concatenate(small_rows, axis=0)))
    conv_w = small[:, :3 * n_even].reshape(N_CHIPS, n_even, 3, LANES).transpose(1, 2, 0, 3).reshape(n_even, 3, CONV_WIDTH)
    pool_scale = small[:, 3 * n_even:3 * n_even + n_odd].transpose(1, 0, 2).reshape(n_odd, HALF)
    sgu_norm = small[:, 3 * n_even + n_odd:3 * n_even + 2 * n_odd].transpose(1, 0, 2).reshape(n_odd, HALF)
    later = [[(0, "ffn2")]] + [[(l, "ffn1"), (l, "mix"), (l, "ffn2")] for l in range(1, depth)]
    gathering, after = [], small
    for i, group in enumerate(later):
        gathering.append(_split_start(f"gather_start{i}", _gather_plan, N_CHIPS, shards(group), after))
        after = gathering[-1][4]
    gathered = {(0, "ffn1"): first[0:2], (0, "mix"): first[2:4]}

    def rows(g):
        return g.reshape(N_CHIPS * g.shape[1], g.shape[2])

    def weights_of(l, block, x_in):
        zero = after[0, 0] if (l, block) == (0, "ffn1") else 0.0
        if (l, block) not in gathered:
            i = next(i for i, group in enumerate(later) if (l, block) in group)
            got = _split_wait(f"gather_wait{i}", _gather_plan, gathering[i][:4], x_in)[1]
            for n, key in enumerate(later[i]):
                gathered[key] = got[2 * n:2 * n + 2]
        w_in, w_out = gathered[(l, block)]
        if block != "mix":
            return dict(norm=w[block + "_norm"][l] + zero, w_in4=w_in, w_out=rows(w_out))
        j = l // 2
        if l % 2 == 0:
            mix = dict(conv_w=conv_w[j], q_gain=w["ev_q_norm"][j], k_gain=w["ev_k_norm"][j])
        else:
            mix = dict(pool_w=w["od_pool_w"][j], pool_scale=pool_scale[j], sgu_norm=sgu_norm[j],
                       sgu_w=w["od_sgu_w"][j], sgu_b=w["od_sgu_b"][j])
        return dict(mix, norm=w["mix_norm"][l], w_in=_join_cols(w_in), w_out=rows(w_out))

    def by_chip(dw):
        return dw.reshape(N_CHIPS, dw.shape[0] // N_CHIPS, dw.shape[1])

    bufs = {n: lax.empty(w[n].shape, F32) for n in _BIG}
    small_grads = {n: [None] * w[n].shape[0] for n in _WEIGHTS if n not in _BIG and n != "final_norm"}
    scattering, group = [], []

    def finish_scatter(after):
        tag, names, started = scattering.pop()
        halves, got = _split_wait(f"scatter_wait{tag}", _scatter_plan, started[:4], after)
        for i, (n, j) in enumerate(names):
            bufs[n] = _add_chips(f"add_chips{tag}_{n}", halves[i], got[i], place, bufs[n], j)
        joined = _join_halves(f"join_halves{tag}", [bufs[n] for n, _ in names], [j for _, j in names])
        for (n, _), b in zip(names, joined):
            bufs[n] = b

    def grads_done(l, block, g, dx):
        j = l // 2
        if block == "mix":
            local = [_split_cols(g["w_in"]), by_chip(g["w_out"])]
            small_grads["mix_norm"][l] = g["norm"]
            renamed = (dict(conv_w="ev_conv_w", q_gain="ev_q_norm", k_gain="ev_k_norm") if l % 2 == 0 else
                       dict(pool_w="od_pool_w", pool_scale="od_pool_scale", sgu_norm="od_sgu_norm", sgu_w="od_sgu_w",
                            sgu_b="od_sgu_b"))
            for key, n in renamed.items():
                small_grads[n][j] = g[key]
        else:
            local = [g["w_in4"], by_chip(g["w_out"])]
            small_grads[block + "_norm"][l] = g["norm"]
        group.extend(zip(sharded(l, block), local))
        if block == "ffn2" or (block == "mix" and l > 0):
            return 0.0
        if scattering:
            finish_scatter(dx)
        tag = f"{l}_{block}"
        names, local = [n for n, _ in group], [a for _, a in group]
        group.clear()
        from_sibling = _swap_halves(f"swap_halves{tag}", local)
        halves = [_add_sibling(f"add_sibling{tag}_{n}", a, b, core) for (n, _), a, b in zip(names, local, from_sibling)]
        scattering.append((tag, names, _split_start(f"scatter_start{tag}", _scatter_plan, N_CHIPS - 1, halves, dx)))
        return scattering[-1][2][4][0, 0]

    loss_part, grad_x, dfinal = _local_step(x, target, depth, weights_of, w["final_norm"], grads_done)
    loss = lax.psum(loss_part, ("x", "y", "c"))

    small_grads = {n: jnp.stack(parts) for n, parts in small_grads.items()}
    small_grads["final_norm"] = dfinal
    names = list(small_grads)
    flat = jnp.concatenate([small_grads[n].reshape(-1) for n in names])
    total = flat.shape[0]
    flat = jnp.concatenate([flat, jnp.zeros((-total) % (8 * LANES), F32)])
    summed = _allreduce_small(flat.reshape(-1, LANES)).reshape(-1)
    finish_scatter(summed)
    grads = dict(bufs)
    off = 0
    for n in names:
        size = small_grads[n].size
        full_grad = summed[off:off + size].reshape(small_grads[n].shape)
        off += size
        if n in _SMALL_SHARDED:
            full_grad = lax.dynamic_slice_in_dim(full_grad, chip * LANES, LANES, axis=full_grad.ndim - 1)
        grads[n] = full_grad

    deltas, new_m, new_v = [], [], []
    for n in _WEIGHTS:
        d_n, m_n, v_n = _adamw("adamw_" + n, w[n], grads[n], m[n], v[n])
        deltas.append(d_n)
        new_m.append(m_n)
        new_v.append(v_n)
    return (loss, grad_x, *[grads[n] for n in _WEIGHTS], *deltas, *new_m, *new_v)
```

```python
import jax
import jax.numpy as jnp
from jax import lax
from jax.experimental import pallas as pl
from jax.experimental.pallas import tpu as pltpu

F32 = jnp.float32
_MXU = jnp.bfloat16
_ACT = jnp.bfloat16

D_MODEL = 1024
GRID_W = 64
HEAD_DIM = 64
N_Q_HEADS = 8
N_KV_HEADS = 2
Q_PER_KV = N_Q_HEADS // N_KV_HEADS
ATTN_WIDTH = N_Q_HEADS * HEAD_DIM
KV_WIDTH = N_KV_HEADS * HEAD_DIM
ROPE_THETA = 10000.0
CONV_WIDTH = D_MODEL // 2
POOL_RADII = (1, 2, 4, 8)
POOL_GROUP = 128
SGU_GROUP = 128
SGU_CHUNK = 128
N_GROUPS = 4
HALF = D_MODEL // 2
EPS = 1e-6
HALO = 8
LANES = 128
N_CHIPS = 4
N_DEV = 8

ADAM_LR = 0.001
ADAM_B1 = 0.9
ADAM_B2 = 0.999
ADAM_EPS = 1e-08
ADAM_WD = 0.01
ADAM_STEP = 10

_VMEM_LIMIT = 56 * 2 ** 20
_MESH = pl.DeviceIdType.MESH
_ANY = pl.BlockSpec(memory_space=pl.ANY)
_VMEM = pl.BlockSpec(memory_space=pltpu.VMEM)

_DN = {
    "nn": (((1,), (0,)), ((), ())),
    "nt": (((1,), (1,)), ((), ())),
    "tn": (((0,), (0,)), ((), ())),
}


def _params(*sem):
    return pltpu.CompilerParams(dimension_semantics=sem, vmem_limit_bytes=_VMEM_LIMIT)


def _tile(n, cap):
    best = None
    d = LANES
    while d <= min(n, cap):
        if n % d == 0:
            best = d
        d += LANES
    return best if best is not None else n


def _dot(a, b, mode="nn"):
    return lax.dot_general(a.astype(_MXU), b.astype(_MXU), _DN[mode], preferred_element_type=F32)


def _cat(*vals):
    vals = [v.astype(_MXU) for v in vals]
    return vals[0] if len(vals) == 1 else jnp.concatenate(vals, axis=1)


def _sigmoid(g):
    return 1.0 / (1.0 + jnp.exp(-g))


def _swiglu(g, u):
    g = g.astype(F32)
    return (g * _sigmoid(g)) * u.astype(F32)


_GELU_C = 0.7978845608028654


def _gelu(x):
    return 0.5 * x * (1.0 + jnp.tanh(_GELU_C * (x + 0.044715 * (x * x * x))))


def _gelu_grad(x):
    t = jnp.tanh(_GELU_C * (x + 0.044715 * (x * x * x)))
    return 0.5 * (1.0 + t) + 0.5 * x * (1.0 - t * t) * (_GELU_C * (1.0 + 3.0 * 0.044715 * (x * x)))


def _mm(name, grid, mode, a_ops, b_ops, e_ops, out_shape, out_specs, acc_shape, a_fn=_cat, b_fn=_cat, epi=None,
        n_outer=False, m_carried=False):
    ni, nj, nk = grid
    na, nb, ne = len(a_ops), len(b_ops), len(e_ops)
    multi = isinstance(out_shape, (list, tuple))
    no = len(out_shape) if multi else 1

    def body(*refs):
        a_refs = refs[:na]
        b_refs = refs[na:na + nb]
        e_refs = refs[na + nb:na + nb + ne]
        o_refs = refs[na + nb + ne:na + nb + ne + no]
        a = a_fn(*[r[...] for r in a_refs])
        b = b_fn(*[r[...] for r in b_refs])
        p = _dot(a, b, mode)

        def finish(acc):
            if epi is None:
                o_refs[0][...] = acc.astype(o_refs[0].dtype)
            else:
                epi(acc, [r[...] for r in e_refs], o_refs)

        if nk == 1:
            finish(p)
        else:
            acc_ref = refs[-1]
            k = pl.program_id(2)

            @pl.when(k == 0)
            def _():
                acc_ref[...] = p

            @pl.when((k > 0) & (k < nk - 1))
            def _():
                acc_ref[...] += p

            @pl.when(k == nk - 1)
            def _():
                finish(acc_ref[...] + p)

    ops = list(a_ops) + list(b_ops) + list(e_ops)
    if n_outer:
        def flip(spec):
            return pl.BlockSpec(spec.block_shape, lambda j, i, k, f=spec.index_map: f(i, j, k))

        grid = (nj, ni, nk)
        ops = [(a, flip(s)) for a, s in ops]
        out_specs = [flip(s) for s in out_specs] if multi else flip(out_specs)
    return pl.pallas_call(
        body, name=name, grid=grid,
        in_specs=[s for _, s in ops],
        out_specs=out_specs, out_shape=out_shape,
        scratch_shapes=[pltpu.VMEM(acc_shape, F32)] if nk > 1 else [],
        compiler_params=_params(*(("arbitrary",) * 3 if m_carried else ("parallel", "parallel", "arbitrary"))),
    )(*[a for a, _ in ops])


def _norm_bwd_epi(acc, e, o):
    xf, dres, g = e
    r = lax.rsqrt(jnp.mean(xf * xf, axis=-1, keepdims=True) + EPS)
    xhat = xf * r
    dgx = acc * g
    m = jnp.mean(dgx * xhat, axis=-1, keepdims=True)
    o[0][...] = dres + r * (dgx - xhat * m)
    part = jnp.sum(acc * xhat, axis=0, keepdims=True)
    i = pl.program_id(0)

    @pl.when(i == 0)
    def _():
        o[1][...] = part

    @pl.when(i > 0)
    def _():
        o[1][...] += part


def _norm_bwd_ops(x, dres, gain, tm):
    t, d = x.shape
    row = pl.BlockSpec((tm, d), lambda i, j, k: (i, 0))
    vec = pl.BlockSpec((1, d), lambda i, j, k: (0, 0))
    return ([(x, row), (dres, row), (gain.reshape(1, d), vec)],
            [jax.ShapeDtypeStruct((t, d), F32), jax.ShapeDtypeStruct((1, d), F32)], [row, vec])


def _rows(t):
    return _tile(t, 512)


def _rmsnorm_fwd(name, x, gain):
    t, d = x.shape
    tr = _rows(t)

    def body(x_ref, g_ref, h_ref):
        xf = x_ref[...]
        r = lax.rsqrt(jnp.mean(xf * xf, axis=-1, keepdims=True) + EPS)
        h_ref[...] = ((xf * r) * g_ref[...]).astype(h_ref.dtype)

    return pl.pallas_call(
        body, name=name, grid=(t // tr,),
        in_specs=[pl.BlockSpec((tr, d), lambda i: (i, 0)), pl.BlockSpec((1, d), lambda i: (0, 0))],
        out_specs=pl.BlockSpec((tr, d), lambda i: (i, 0)),
        out_shape=jax.ShapeDtypeStruct((t, d), _ACT),
        compiler_params=_params("parallel"),
    )(x, gain.reshape(1, d))


def _rmsnorm_bwd(name, dh, x, gain, dres):
    t, d = x.shape
    tr = _rows(t)

    def body(dh_ref, x_ref, g_ref, dres_ref, dx_ref, dg_ref):
        i = pl.program_id(0)
        xf = x_ref[...]
        r = lax.rsqrt(jnp.mean(xf * xf, axis=-1, keepdims=True) + EPS)
        xhat = xf * r
        dy = dh_ref[...].astype(F32)
        dgx = dy * g_ref[...]
        m = jnp.mean(dgx * xhat, axis=-1, keepdims=True)
        dx_ref[...] = dres_ref[...] + r * (dgx - xhat * m)
        part = jnp.sum(dy * xhat, axis=0, keepdims=True)

        @pl.when(i == 0)
        def _():
            dg_ref[...] = part

        @pl.when(i > 0)
        def _():
            dg_ref[...] += part

    row = pl.BlockSpec((tr, d), lambda i: (i, 0))
    vec = pl.BlockSpec((1, d), lambda i: (0, 0))
    dx, dg = pl.pallas_call(
        body, name=name, grid=(t // tr,),
        in_specs=[row, row, vec, row],
        out_specs=(row, vec),
        out_shape=(jax.ShapeDtypeStruct((t, d), F32), jax.ShapeDtypeStruct((1, d), F32)),
        compiler_params=_params("arbitrary"),
    )(dh, x, gain.reshape(1, d), dres)
    return dx, dg.reshape(d)


def _final_loss(name, x, gain, target):
    t, d = x.shape
    tr = _rows(t)

    def body(x_ref, g_ref, t_ref, dx_ref, dg_ref, loss_ref):
        i = pl.program_id(0)
        xf = x_ref[...]
        r = lax.rsqrt(jnp.mean(xf * xf, axis=-1, keepdims=True) + EPS)
        xhat = xf * r
        g = g_ref[...]
        err = xhat * g - t_ref[...]
        lpart = 0.5 * jnp.sum(jnp.mean(err * err, axis=-1, keepdims=True), axis=0, keepdims=True)
        dy = err * (1.0 / d)
        dgx = dy * g
        m = jnp.mean(dgx * xhat, axis=-1, keepdims=True)
        dx_ref[...] = r * (dgx - xhat * m)
        part = jnp.sum(dy * xhat, axis=0, keepdims=True)
        lrow = jnp.broadcast_to(lpart, (1, LANES))

        @pl.when(i == 0)
        def _():
            dg_ref[...] = part
            loss_ref[...] = lrow

        @pl.when(i > 0)
        def _():
            dg_ref[...] += part
            loss_ref[...] += lrow

    row = pl.BlockSpec((tr, d), lambda i: (i, 0))
    vec = pl.BlockSpec((1, d), lambda i: (0, 0))
    dx, dg, loss = pl.pallas_call(
        body, name=name, grid=(t // tr,),
        in_specs=[row, vec, row],
        out_specs=(row, vec, pl.BlockSpec((1, LANES), lambda i: (0, 0))),
        out_shape=(jax.ShapeDtypeStruct((t, d), F32), jax.ShapeDtypeStruct((1, d), F32),
                   jax.ShapeDtypeStruct((1, LANES), F32)),
        compiler_params=_params("arbitrary"),
    )(x, gain.reshape(1, d), target)
    return loss[0, 0], dx, dg.reshape(d)


_FFN_TILES = dict(in_tm=1024, in_n_outer=False, out_tm=512, dact_tm=512, dwout_tk=1024, dh_tm=512, dwin_tk=2048)


def _ffn_tiles(layer, which):
    return _FFN_TILES


def _ffn_fwd(tag, x, gain, w_in4, w_out, cfg):
    t, d = x.shape
    fs = w_in4.shape[2]
    f = 2 * fs
    tm = _tile(t, cfg["in_tm"])
    h = _rmsnorm_fwd(tag + "_norm", x, gain)
    gu = _mm(
        tag + "_in", (t // tm, N_CHIPS, 1), "nn",
        [(h, pl.BlockSpec((tm, d), lambda i, j, k: (i, 0)))],
        [(w_in4, pl.BlockSpec((None, d, fs), lambda i, j, k: (j, 0, 0)))], [],
        jax.ShapeDtypeStruct((2, t, f), _ACT),
        pl.BlockSpec((None, tm, fs), lambda i, j, k: (j // 2, i, j % 2)), None, n_outer=cfg["in_n_outer"])
    tm2 = _tile(t, cfg["out_tm"])

    def epi(acc, e, o):
        o[0][...] = e[0] + 0.5 * acc

    x_out = _mm(
        tag + "_out", (t // tm2, 1, 1), "nn",
        [(gu, pl.BlockSpec((None, tm2, f), lambda i, j, k: (0, i, 0))),
         (gu, pl.BlockSpec((None, tm2, f), lambda i, j, k: (1, i, 0)))],
        [(w_out, pl.BlockSpec((f, d), lambda i, j, k: (0, 0)))],
        [(x, pl.BlockSpec((tm2, d), lambda i, j, k: (i, 0)))],
        jax.ShapeDtypeStruct((t, d), F32),
        pl.BlockSpec((tm2, d), lambda i, j, k: (i, 0)), None,
        a_fn=_swiglu, epi=epi)
    return x_out, (x, h, gu)


def _ffn_bwd(tag, dxo, saved, gain, w_in4, w_out, cfg):
    x, h, gu = saved
    t, d = x.shape
    fs = w_in4.shape[2]
    f = 2 * fs
    tm = _tile(t, cfg["dact_tm"])
    tk = _tile(t, cfg["dwout_tk"])

    def epi_act(acc, e, o):
        g = e[0].astype(F32)
        u = e[1].astype(F32)
        da = 0.5 * acc
        sig = _sigmoid(g)
        o[0][0] = (da * u * (sig * (1.0 + g * (1.0 - sig)))).astype(o[0].dtype)
        o[0][1] = (da * (g * sig)).astype(o[0].dtype)

    dgu = _mm(
        tag + "_dact", (t // tm, 2, 1), "nt",
        [(dxo, pl.BlockSpec((tm, d), lambda i, j, k: (i, 0)))],
        [(w_out, pl.BlockSpec((fs, d), lambda i, j, k: (j, 0)))],
        [(gu, pl.BlockSpec((None, tm, fs), lambda i, j, k: (0, i, j))),
         (gu, pl.BlockSpec((None, tm, fs), lambda i, j, k: (1, i, j)))],
        jax.ShapeDtypeStruct((2, t, f), _ACT),
        pl.BlockSpec((2, tm, fs), lambda i, j, k: (0, i, j)), None, epi=epi_act)

    def epi_half(acc, e, o):
        o[0][...] = (0.5 * acc).astype(o[0].dtype)

    dw_out = _mm(
        tag + "_dwout", (2, 1, t // tk), "tn",
        [(gu, pl.BlockSpec((None, tk, fs), lambda i, j, k: (0, k, i))),
         (gu, pl.BlockSpec((None, tk, fs), lambda i, j, k: (1, k, i)))],
        [(dxo, pl.BlockSpec((tk, d), lambda i, j, k: (k, 0)))], [],
        jax.ShapeDtypeStruct((f, d), _ACT),
        pl.BlockSpec((fs, d), lambda i, j, k: (i, 0)), (fs, d),
        a_fn=_swiglu, epi=epi_half)
    tm = _tile(t, cfg["dh_tm"])
    e_ops, shapes, specs = _norm_bwd_ops(x, dxo, gain, tm)
    dx, dgain = _mm(
        tag + "_dh", (t // tm, 1, 2), "nt",
        [(dgu, pl.BlockSpec((None, tm, f), lambda i, j, k: (k, i, 0)))],
        [(w_in4, pl.BlockSpec((2, d, fs), lambda i, j, k: (k, 0, 0)))], e_ops, shapes, specs, (tm, d),
        b_fn=lambda b: jnp.concatenate([b[0], b[1]], axis=1), epi=_norm_bwd_epi, m_carried=True)
    tk = _tile(t, cfg["dwin_tk"])
    dw_in4 = _mm(
        tag + "_dwin", (1, N_CHIPS, t // tk), "tn",
        [(h, pl.BlockSpec((tk, d), lambda i, j, k: (k, 0)))],
        [(dgu, pl.BlockSpec((None, tk, fs), lambda i, j, k: (j // 2, k, j % 2)))], [],
        jax.ShapeDtypeStruct((N_CHIPS, d, fs), _ACT),
        pl.BlockSpec((None, d, fs), lambda i, j, k: (j, 0, 0)), (d, fs))
    return dx, dgain.reshape(d), dw_in4, dw_out


_MIX_TILES = dict(tm=1024, dwout_tk=2048, dwin_tk=1024)


def _proj_in(tag, h, w_in):
    t, d = h.shape
    n = w_in.shape[1]
    tm = _tile(t, _MIX_TILES["tm"])
    return _mm(
        tag + "_in", (t // tm, 1, 1), "nn",
        [(h, pl.BlockSpec((tm, d), lambda i, j, k: (i, 0)))],
        [(w_in, pl.BlockSpec((d, n), lambda i, j, k: (0, 0)))], [],
        jax.ShapeDtypeStruct((t, n), _ACT),
        pl.BlockSpec((tm, n), lambda i, j, k: (i, 0)), None)


def _proj_out(tag, x, parts, w_out):
    t, d = x.shape
    tm = _tile(t, _MIX_TILES["tm"])

    def epi(acc, e, o):
        o[0][...] = e[0] + acc

    return _mm(
        tag + "_out", (t // tm, 1, 1), "nn",
        [(p, pl.BlockSpec((tm, p.shape[1]), lambda i, j, k: (i, 0))) for p in parts],
        [(w_out, pl.BlockSpec(w_out.shape, lambda i, j, k: (0, 0)))],
        [(x, pl.BlockSpec((tm, d), lambda i, j, k: (i, 0)))],
        jax.ShapeDtypeStruct((t, d), F32),
        pl.BlockSpec((tm, d), lambda i, j, k: (i, 0)), None, epi=epi)


def _proj_out_bwd(tag, dxo, parts, w_out):
    t, d = dxo.shape
    mix = w_out.shape[0]
    tm = _tile(t, _MIX_TILES["tm"])
    tk = _tile(t, _MIX_TILES["dwout_tk"])
    d_mix = _mm(
        tag + "_dmix", (t // tm, 1, 1), "nt",
        [(dxo, pl.BlockSpec((tm, d), lambda i, j, k: (i, 0)))],
        [(w_out, pl.BlockSpec((mix, d), lambda i, j, k: (0, 0)))], [],
        jax.ShapeDtypeStruct((t, mix), F32),
        pl.BlockSpec((tm, mix), lambda i, j, k: (i, 0)), None)
    dw_out = _mm(
        tag + "_dwout", (1, 1, t // tk), "tn",
        [(p, pl.BlockSpec((tk, p.shape[1]), lambda i, j, k: (k, 0))) for p in parts],
        [(dxo, pl.BlockSpec((tk, d), lambda i, j, k: (k, 0)))], [],
        jax.ShapeDtypeStruct((mix, d), _ACT),
        pl.BlockSpec((mix, d), lambda i, j, k: (0, 0)), (mix, d))
    return d_mix, dw_out


def _proj_in_bwd(tag, h, dparts, w_in, x, dres, gain):
    t, d = h.shape
    n = w_in.shape[1]
    tm = _tile(t, _MIX_TILES["tm"])
    tk = _tile(t, _MIX_TILES["dwin_tk"])
    e_ops, shapes, specs = _norm_bwd_ops(x, dres, gain, tm)
    dx, dgain = _mm(
        tag + "_dh", (t // tm, 1, 1), "nt",
        [(p, pl.BlockSpec((tm, p.shape[1]), lambda i, j, k: (i, 0))) for p in dparts],
        [(w_in, pl.BlockSpec((d, n), lambda i, j, k: (0, 0)))], e_ops, shapes, specs, None,
        epi=_norm_bwd_epi, m_carried=True)
    dw_in = _mm(
        tag + "_dwin", (1, 1, t // tk), "tn",
        [(h, pl.BlockSpec((tk, d), lambda i, j, k: (k, 0)))],
        [(p, pl.BlockSpec((tk, p.shape[1]), lambda i, j, k: (k, 0))) for p in dparts], [],
        jax.ShapeDtypeStruct((d, n), _ACT),
        pl.BlockSpec((d, n), lambda i, j, k: (0, 0)), (d, n))
    return dx, dgain.reshape(d), dw_in


def _shifted(pad_ref, val, s):
    pad_ref[pl.ds(HALO, s), :] = val
    return pad_ref[pl.ds(HALO - 1, s), :], pad_ref[pl.ds(HALO + 1, s), :]


def _zero_halo(pad_ref, s):
    z = jnp.zeros((HALO, pad_ref.shape[1]), F32)
    pad_ref[pl.ds(0, HALO), :] = z
    pad_ref[pl.ds(HALO + s, HALO), :] = z


def _conv_fwd(tag, proj, conv_w, nb, s):
    t = proj.shape[0]
    ncb = CONV_WIDTH // LANES

    def body(gb_ref, gc_ref, hc_ref, w_ref, a_ref, pad_ref):
        _zero_halo(pad_ref, s)
        cg = gc_ref[...].astype(F32) * hc_ref[...].astype(F32)
        prev, nxt = _shifted(pad_ref, cg, s)
        w = w_ref[...]
        conv = prev * w[0:1, :] + cg * w[1:2, :] + nxt * w[2:3, :]
        a_ref[...] = (gb_ref[...].astype(F32) * conv).astype(a_ref.dtype)

    def col(off):
        return pl.BlockSpec((s, LANES), lambda b, c: (b, off + c))

    return pl.pallas_call(
        body, name=tag + "_conv", grid=(nb, ncb),
        in_specs=[col(0), col(ncb), col(2 * ncb), pl.BlockSpec((3, LANES), lambda b, c: (0, c))],
        out_specs=col(0),
        out_shape=jax.ShapeDtypeStruct((t, CONV_WIDTH), _ACT),
        scratch_shapes=[pltpu.VMEM((s + 2 * HALO, LANES), F32)],
        compiler_params=_params("parallel", "parallel"),
    )(proj, proj, proj, conv_w)


def _conv_bwd(tag, proj, conv_w, d_mix, nb, s):
    t = proj.shape[0]
    ncb = CONV_WIDTH // LANES

    def body(gb_ref, gc_ref, hc_ref, w_ref, da_ref, dgb_ref, dgc_ref, dhc_ref, dw_ref, pad_ref):
        b = pl.program_id(1)
        _zero_halo(pad_ref, s)
        gb = gb_ref[...].astype(F32)
        gc = gc_ref[...].astype(F32)
        hc = hc_ref[...].astype(F32)
        w = w_ref[...]
        da = da_ref[...]
        cg = gc * hc
        prev, nxt = _shifted(pad_ref, cg, s)
        conv = prev * w[0:1, :] + cg * w[1:2, :] + nxt * w[2:3, :]
        dgb_ref[...] = (da * conv).astype(dgb_ref.dtype)
        dconv = da * gb
        dw = jnp.concatenate([
            jnp.sum(dconv * prev, axis=0, keepdims=True),
            jnp.sum(dconv * cg, axis=0, keepdims=True),
            jnp.sum(dconv * nxt, axis=0, keepdims=True)], axis=0)
        dprev, dnxt = _shifted(pad_ref, dconv, s)
        dcg = dnxt * w[0:1, :] + dconv * w[1:2, :] + dprev * w[2:3, :]
        dgc_ref[...] = (dcg * hc).astype(dgc_ref.dtype)
        dhc_ref[...] = (dcg * gc).astype(dhc_ref.dtype)

        @pl.when(b == 0)
        def _():
            dw_ref[...] = dw

        @pl.when(b > 0)
        def _():
            dw_ref[...] += dw

    def col(off):
        return pl.BlockSpec((s, LANES), lambda c, b: (b, off + c))

    wspec = pl.BlockSpec((3, LANES), lambda c, b: (0, c))
    act = jax.ShapeDtypeStruct((t, CONV_WIDTH), _ACT)
    return pl.pallas_call(
        body, name=tag + "_dconv", grid=(ncb, nb),
        in_specs=[col(0), col(ncb), col(2 * ncb), wspec, col(0)],
        out_specs=(col(0), col(0), col(0), wspec),
        out_shape=(act, act, act, jax.ShapeDtypeStruct((3, CONV_WIDTH), F32)),
        scratch_shapes=[pltpu.VMEM((s + 2 * HALO, LANES), F32)],
        compiler_params=_params("parallel", "arbitrary"),
    )(proj, proj, proj, conv_w, d_mix)


def _rope_tables(s):
    rows = s // GRID_W
    r_idx, c_idx = jnp.meshgrid(jnp.arange(rows), jnp.arange(GRID_W), indexing="ij")
    r_idx = r_idx.reshape(-1).astype(F32)
    c_idx = c_idx.reshape(-1).astype(F32)
    n_freq = HEAD_DIM // 4
    inv = ROPE_THETA ** (-jnp.arange(n_freq, dtype=F32) / n_freq)
    ang = jnp.concatenate([r_idx[:, None] * inv, c_idx[:, None] * inv], axis=-1)
    cos = jnp.repeat(jnp.cos(ang), 2, axis=1)
    sin = jnp.repeat(jnp.sin(ang), 2, axis=1)
    sign = jnp.where(jnp.arange(HEAD_DIM) % 2 == 0, -1.0, 1.0).astype(F32)
    return jnp.tile(cos, (1, LANES // HEAD_DIM)), jnp.tile(sin * sign, (1, LANES // HEAD_DIM))


def _head_ones():
    i = jnp.arange(LANES) // HEAD_DIM
    return (i[:, None] == i[None, :]).astype(jnp.bfloat16)


def _head_sum(v, ones):
    outs = []
    for j in range(v.shape[1] // LANES):
        c = v[:, j * LANES:(j + 1) * LANES]
        hi = c.astype(jnp.bfloat16)
        lo = (c - hi.astype(F32)).astype(jnp.bfloat16)
        outs.append(jnp.dot(hi, ones, preferred_element_type=F32) + jnp.dot(lo, ones, preferred_element_type=F32))
    return outs[0] if len(outs) == 1 else jnp.concatenate(outs, axis=1)


def _pair_swap(v):
    outs = []
    for j in range(v.shape[1] // LANES):
        c = v[:, j * LANES:(j + 1) * LANES]
        lane = lax.broadcasted_iota(jnp.int32, c.shape, 1)
        outs.append(jnp.where(lane % 2 == 0, pltpu.roll(c, LANES - 1, 1), pltpu.roll(c, 1, 1)))
    return outs[0] if len(outs) == 1 else jnp.concatenate(outs, axis=1)


def _wide(tab, width):
    return tab if width == LANES else jnp.concatenate([tab] * (width // LANES), axis=1)


def _qk_fwd(tag, proj, q_gain, k_gain, cos, sin, nb, s):
    t = proj.shape[0]
    tr = _tile(s, 512)
    ns = s // tr
    q_off = 3 * CONV_WIDTH // ATTN_WIDTH
    k_off = (3 * CONV_WIDTH + ATTN_WIDTH) // KV_WIDTH

    def body(q_ref, k_ref, qg_ref, kg_ref, cos_ref, sin_ref, ones_ref, qo_ref, ko_ref):
        ones = ones_ref[...]
        for src, g_ref, dst in ((q_ref, qg_ref, qo_ref), (k_ref, kg_ref, ko_ref)):
            v = src[...].astype(F32)
            w = v.shape[1]
            r = lax.rsqrt(_head_sum(v * v, ones) * (1.0 / HEAD_DIM) + EPS)
            vn = (v * r) * g_ref[...]
            dst[...] = (vn * _wide(cos_ref[...], w) + _pair_swap(vn) * _wide(sin_ref[...], w)).astype(dst.dtype)

    tab = pl.BlockSpec((tr, LANES), lambda i: (i % ns, 0))
    return pl.pallas_call(
        body, name=tag + "_qk", grid=(t // tr,),
        in_specs=[pl.BlockSpec((tr, ATTN_WIDTH), lambda i: (i, q_off)),
                  pl.BlockSpec((tr, KV_WIDTH), lambda i: (i, k_off)),
                  pl.BlockSpec((1, ATTN_WIDTH), lambda i: (0, 0)),
                  pl.BlockSpec((1, KV_WIDTH), lambda i: (0, 0)),
                  tab, tab, pl.BlockSpec((LANES, LANES), lambda i: (0, 0))],
        out_specs=(pl.BlockSpec((tr, ATTN_WIDTH), lambda i: (i, 0)),
                   pl.BlockSpec((tr, KV_WIDTH), lambda i: (i, 0))),
        out_shape=(jax.ShapeDtypeStruct((t, ATTN_WIDTH), _ACT), jax.ShapeDtypeStruct((t, KV_WIDTH), _ACT)),
        compiler_params=_params("parallel"),
    )(proj, proj, jnp.tile(q_gain, N_Q_HEADS).reshape(1, ATTN_WIDTH),
      jnp.tile(k_gain, N_KV_HEADS).reshape(1, KV_WIDTH), cos, sin, _head_ones())


def _qk_bwd(tag, proj, q_gain, k_gain, cos, sin, dq_rot, dk_rot, nb, s):
    t = proj.shape[0]
    tr = _tile(s, 512)
    ns = s // tr
    q_off = 3 * CONV_WIDTH // ATTN_WIDTH
    k_off = (3 * CONV_WIDTH + ATTN_WIDTH) // KV_WIDTH

    def body(q_ref, k_ref, qg_ref, kg_ref, cos_ref, sin_ref, ones_ref, dqr_ref, dkr_ref,
             dq_ref, dk_ref, dqg_ref, dkg_ref):
        i = pl.program_id(0)
        ones = ones_ref[...]
        for src, g_ref, dr_ref, dst, dg_ref in ((q_ref, qg_ref, dqr_ref, dq_ref, dqg_ref),
                                                (k_ref, kg_ref, dkr_ref, dk_ref, dkg_ref)):
            v = src[...].astype(F32)
            w = v.shape[1]
            r = lax.rsqrt(_head_sum(v * v, ones) * (1.0 / HEAD_DIM) + EPS)
            xhat = v * r
            dr = dr_ref[...]
            dvn = dr * _wide(cos_ref[...], w) + _pair_swap(dr * _wide(sin_ref[...], w))
            dgx = dvn * g_ref[...]
            m = _head_sum(dgx * xhat, ones) * (1.0 / HEAD_DIM)
            dst[...] = (r * (dgx - xhat * m)).astype(dst.dtype)
            part = jnp.sum(dvn * xhat, axis=0, keepdims=True)
            fold = part[:, 0:HEAD_DIM]
            for hh in range(1, w // HEAD_DIM):
                fold = fold + part[:, hh * HEAD_DIM:(hh + 1) * HEAD_DIM]

            @pl.when(i == 0)
            def _():
                dg_ref[...] = fold

            @pl.when(i > 0)
            def _():
                dg_ref[...] += fold

    tab = pl.BlockSpec((tr, LANES), lambda i: (i % ns, 0))
    qrow = pl.BlockSpec((tr, ATTN_WIDTH), lambda i: (i, 0))
    krow = pl.BlockSpec((tr, KV_WIDTH), lambda i: (i, 0))
    gvec = pl.BlockSpec((1, HEAD_DIM), lambda i: (0, 0))
    dq, dk, dqg, dkg = pl.pallas_call(
        body, name=tag + "_dqk", grid=(t // tr,),
        in_specs=[pl.BlockSpec((tr, ATTN_WIDTH), lambda i: (i, q_off)),
                  pl.BlockSpec((tr, KV_WIDTH), lambda i: (i, k_off)),
                  pl.BlockSpec((1, ATTN_WIDTH), lambda i: (0, 0)),
                  pl.BlockSpec((1, KV_WIDTH), lambda i: (0, 0)),
                  tab, tab, pl.BlockSpec((LANES, LANES), lambda i: (0, 0)), qrow, krow],
        out_specs=(qrow, krow, gvec, gvec),
        out_shape=(jax.ShapeDtypeStruct((t, ATTN_WIDTH), _ACT), jax.ShapeDtypeStruct((t, KV_WIDTH), _ACT),
                   jax.ShapeDtypeStruct((1, HEAD_DIM), F32), jax.ShapeDtypeStruct((1, HEAD_DIM), F32)),
        compiler_params=_params("arbitrary"),
    )(proj, proj, jnp.tile(q_gain, N_Q_HEADS).reshape(1, ATTN_WIDTH),
      jnp.tile(k_gain, N_KV_HEADS).reshape(1, KV_WIDTH), cos, sin, _head_ones(), dq_rot, dk_rot)
    return dq, dk, dqg.reshape(HEAD_DIM), dkg.reshape(HEAD_DIM)


def _head(v, h):
    return v[:, h * HEAD_DIM:(h + 1) * HEAD_DIM]


def _attn_fwd(tag, q, k, proj, nb, s):
    t = q.shape[0]
    tq = _tile(s, 256)
    nq = s // tq
    v_off = (3 * CONV_WIDTH + ATTN_WIDTH + KV_WIDTH) // KV_WIDTH
    scale = HEAD_DIM ** -0.5

    def body(q_ref, k_ref, v_ref, o_ref, lse_ref):
        qv = q_ref[...]
        kv = k_ref[...]
        vv = v_ref[...]
        for h in range(N_Q_HEADS):
            j = h // Q_PER_KV
            sc = _dot(_head(qv, h), _head(kv, j), "nt") * scale
            m = jnp.max(sc, axis=-1, keepdims=True)
            e = jnp.exp(sc - m)
            l = jnp.sum(e, axis=-1, keepdims=True)
            o = _dot(e, _head(vv, j)) * (1.0 / l)
            o_ref[:, h * HEAD_DIM:(h + 1) * HEAD_DIM] = o.astype(o_ref.dtype)
            lse_ref[:, h:h + 1] = m + jnp.log(l)

    return pl.pallas_call(
        body, name=tag + "_attn", grid=(nb, nq),
        in_specs=[pl.BlockSpec((tq, ATTN_WIDTH), lambda b, i: (b * nq + i, 0)),
                  pl.BlockSpec((s, KV_WIDTH), lambda b, i: (b, 0)),
                  pl.BlockSpec((s, KV_WIDTH), lambda b, i: (b, v_off))],
        out_specs=(pl.BlockSpec((tq, ATTN_WIDTH), lambda b, i: (b * nq + i, 0)),
                   pl.BlockSpec((tq, N_Q_HEADS), lambda b, i: (b * nq + i, 0))),
        out_shape=(jax.ShapeDtypeStruct((t, ATTN_WIDTH), _ACT), jax.ShapeDtypeStruct((t, N_Q_HEADS), F32)),
        compiler_params=_params("parallel", "parallel"),
    )(q, k, proj)


def _attn_bwd(tag, q, k, proj, o, lse, d_mix, nb, s):
    t = q.shape[0]
    tq = _tile(s, 256)
    nq = s // tq
    v_off = (3 * CONV_WIDTH + ATTN_WIDTH + KV_WIDTH) // KV_WIDTH
    scale = HEAD_DIM ** -0.5

    def body(q_ref, k_ref, v_ref, o_ref, lse_ref, do_ref, dq_ref, dk_ref, dv_ref):
        i = pl.program_id(1)

        @pl.when(i == 0)
        def _():
            dk_ref[...] = jnp.zeros_like(dk_ref)
            dv_ref[...] = jnp.zeros_like(dv_ref)

        qv = q_ref[...]
        kv = k_ref[...]
        vv = v_ref[...]
        ov = o_ref[...].astype(F32)
        dov = do_ref[...]
        lse = lse_ref[...]
        for h in range(N_Q_HEADS):
            j = h // Q_PER_KV
            cols = slice(j * HEAD_DIM, (j + 1) * HEAD_DIM)
            qh = _head(qv, h)
            kj = _head(kv, j)
            doh = _head(dov, h)
            sc = _dot(qh, kj, "nt") * scale
            p = jnp.exp(sc - lse[:, h:h + 1])
            dp = _dot(doh, _head(vv, j), "nt")
            delta = jnp.sum(doh * _head(ov, h), axis=-1, keepdims=True)
            ds = p * (dp - delta) * scale
            dv_ref[:, cols] += _dot(p, doh, "tn")
            dk_ref[:, cols] += _dot(ds, qh, "tn")
            dq_ref[:, h * HEAD_DIM:(h + 1) * HEAD_DIM] = _dot(ds, kj)

    qrow = pl.BlockSpec((tq, ATTN_WIDTH), lambda b, i: (b * nq + i, 0))
    kvrow = pl.BlockSpec((s, KV_WIDTH), lambda b, i: (b, 0))
    return pl.pallas_call(
        body, name=tag + "_dattn", grid=(nb, nq),
        in_specs=[qrow, kvrow, pl.BlockSpec((s, KV_WIDTH), lambda b, i: (b, v_off)), qrow,
                  pl.BlockSpec((tq, N_Q_HEADS), lambda b, i: (b * nq + i, 0)),
                  pl.BlockSpec((tq, ATTN_WIDTH), lambda b, i: (b * nq + i, 1))],
        out_specs=(qrow, kvrow, kvrow),
        out_shape=(jax.ShapeDtypeStruct((t, ATTN_WIDTH), F32), jax.ShapeDtypeStruct((t, KV_WIDTH), F32),
                   jax.ShapeDtypeStruct((t, KV_WIDTH), F32)),
        compiler_params=_params("parallel", "arbitrary"),
    )(q, k, proj, o, lse, d_mix)


def _even_fwd(tag, x, p, cos, sin, nb, s):
    h = _rmsnorm_fwd(tag + "_norm", x, p["norm"])
    proj = _proj_in(tag, h, p["w_in"])
    a = _conv_fwd(tag, proj, p["conv_w"], nb, s)
    q, k = _qk_fwd(tag, proj, p["q_gain"], p["k_gain"], cos, sin, nb, s)
    o, lse = _attn_fwd(tag, q, k, proj, nb, s)
    x_out = _proj_out(tag, x, [a, o], p["w_out"])
    return x_out, (x, h, proj, a, q, k, o, lse)


def _even_bwd(tag, dxo, saved, p, cos, sin, nb, s):
    x, h, proj, a, q, k, o, lse = saved
    d_mix, dw_out = _proj_out_bwd(tag, dxo, [a, o], p["w_out"])
    dgb, dgc, dhc, dconv_w = _conv_bwd(tag, proj, p["conv_w"], d_mix, nb, s)
    dq_rot, dk_rot, dv = _attn_bwd(tag, q, k, proj, o, lse, d_mix, nb, s)
    dq, dk, dq_gain, dk_gain = _qk_bwd(tag, proj, p["q_gain"], p["k_gain"], cos, sin, dq_rot, dk_rot, nb, s)
    dx, dnorm, dw_in = _proj_in_bwd(tag, h, [dgb, dgc, dhc, dq, dk, dv], p["w_in"], x, dxo, p["norm"])
    grads = dict(norm=dnorm, w_in=dw_in, w_out=dw_out, conv_w=dconv_w, q_gain=dq_gain, k_gain=dk_gain)
    return dx, grads


def _window(pad_ref, val, r, s):
    pad_ref[pl.ds(HALO, s), :] = val
    acc = val
    for d in range(1, r + 1):
        acc = acc + pad_ref[pl.ds(HALO - d, s), :] + pad_ref[pl.ds(HALO + d, s), :]
    return acc


def _count(r, s):
    t = lax.broadcasted_iota(jnp.int32, (s, 1), 0)
    return (jnp.minimum(t + r, s - 1) - jnp.maximum(t - r, 0) + 1).astype(F32)


def _sgu_chunk(u_ref, v_ref, norm, ws_ref, bt, rows):
    uu = u_ref[rows, :].astype(F32)
    vv = v_ref[rows, :].astype(F32)
    gu = _gelu(uu)
    gv = _gelu(vv)
    r = lax.rsqrt(jnp.mean(gv * gv, axis=-1, keepdims=True) + EPS)
    xhat = gv * r
    vn = xhat * norm
    mixed = []
    for g in range(N_GROUPS):
        cols = slice(g * SGU_GROUP, (g + 1) * SGU_GROUP)
        mixed.append(_dot(ws_ref[g], vn[:, cols]) + bt[:, g:g + 1])
    return uu, vv, gu, r, xhat, vn, mixed


def _odd_core_fwd(tag, proj, p, nb, s):
    t = proj.shape[0]
    nchunk = s // SGU_CHUNK

    def body(p_ref, u_ref, v_ref, pw_ref, ps_ref, sn_ref, ws_ref, bt_ref, mix_ref, pad_ref):
        _zero_halo(pad_ref, s)
        for g, r in enumerate(POOL_RADII):
            cols = slice(g * POOL_GROUP, (g + 1) * POOL_GROUP)
            pg = p_ref[:, cols].astype(F32)
            pooled = _window(pad_ref, pg, r, s) / _count(r, s) - pg
            mix_ref[:, cols] = (_dot(pooled, pw_ref[g]) * ps_ref[:, cols]).astype(mix_ref.dtype)
        norm = sn_ref[...]
        bt = bt_ref[...]

        def chunk(n, carry):
            rows = pl.ds(pl.multiple_of(n * SGU_CHUNK, SGU_CHUNK), SGU_CHUNK)
            _, _, gu, _, _, _, mixed = _sgu_chunk(u_ref, v_ref, norm, ws_ref, bt, rows)
            for g in range(N_GROUPS):
                cols = slice(g * SGU_GROUP, (g + 1) * SGU_GROUP)
                mix_ref[rows, HALF + g * SGU_GROUP:HALF + (g + 1) * SGU_GROUP] = (
                    gu[:, cols] * mixed[g]).astype(mix_ref.dtype)
            return carry

        lax.fori_loop(0, nchunk, chunk, 0)

    def col(j):
        return pl.BlockSpec((s, HALF), lambda b: (b, j))

    def whole(a):
        return pl.BlockSpec(a.shape, lambda b: (0,) * a.ndim)

    consts = [p["pool_w"], p["pool_scale"].reshape(1, HALF), p["sgu_norm"].reshape(1, HALF),
              p["sgu_w"], p["sgu_b"].T]
    return pl.pallas_call(
        body, name=tag + "_core", grid=(nb,),
        in_specs=[col(0), col(1), col(2)] + [whole(a) for a in consts],
        out_specs=pl.BlockSpec((s, D_MODEL), lambda b: (b, 0)),
        out_shape=jax.ShapeDtypeStruct((t, D_MODEL), _ACT),
        scratch_shapes=[pltpu.VMEM((s + 2 * HALO, POOL_GROUP), F32)],
        compiler_params=_params("parallel"),
    )(proj, proj, proj, *consts)


def _odd_core_bwd(tag, proj, p, d_mix, nb, s):
    t = proj.shape[0]
    nchunk = s // SGU_CHUNK

    def body(p_ref, u_ref, v_ref, pw_ref, ps_ref, sn_ref, ws_ref, bt_ref, dm_ref,
             dproj_ref, dpw_ref, dps_ref, dsn_ref, dws_ref, dbt_ref, pad_ref):
        b = pl.program_id(0)

        @pl.when(b == 0)
        def _():
            dpw_ref[...] = jnp.zeros_like(dpw_ref)
            dps_ref[...] = jnp.zeros_like(dps_ref)
            dsn_ref[...] = jnp.zeros_like(dsn_ref)
            dws_ref[...] = jnp.zeros_like(dws_ref)
            dbt_ref[...] = jnp.zeros_like(dbt_ref)

        _zero_halo(pad_ref, s)
        for g, r in enumerate(POOL_RADII):
            cols = slice(g * POOL_GROUP, (g + 1) * POOL_GROUP)
            pg = p_ref[:, cols].astype(F32)
            cnt = _count(r, s)
            pooled = _window(pad_ref, pg, r, s) / cnt - pg
            c_pre = _dot(pooled, pw_ref[g])
            dc = dm_ref[:, cols]
            dps_ref[:, cols] += jnp.sum(dc * c_pre, axis=0, keepdims=True)
            dcp = dc * ps_ref[:, cols]
            dpw_ref[g] += _dot(pooled, dcp, "tn")
            dpooled = _dot(dcp, pw_ref[g], "nt")
            dproj_ref[:, cols] = (_window(pad_ref, dpooled / cnt, r, s) - dpooled).astype(dproj_ref.dtype)
        norm = sn_ref[...]
        bt = bt_ref[...]

        def chunk(n, carry):
            rows = pl.ds(pl.multiple_of(n * SGU_CHUNK, SGU_CHUNK), SGU_CHUNK)
            uu, vv, gu, r, xhat, vn, mixed = _sgu_chunk(u_ref, v_ref, norm, ws_ref, bt, rows)
            dd = dm_ref[rows, HALF:D_MODEL]
            dgu, dvn = [], []
            for g in range(N_GROUPS):
                cols = slice(g * SGU_GROUP, (g + 1) * SGU_GROUP)
                dgu.append(dd[:, cols] * mixed[g])
                dmx = dd[:, cols] * gu[:, cols]
                dbt_ref[:, g:g + 1] += jnp.sum(dmx, axis=-1, keepdims=True)
                dws_ref[g] += _dot(dmx, vn[:, cols], "nt")
                dvn.append(_dot(ws_ref[g], dmx, "tn"))
            dgu = jnp.concatenate(dgu, axis=1)
            dvn = jnp.concatenate(dvn, axis=1)
            dsn_ref[...] += jnp.sum(dvn * xhat, axis=0, keepdims=True)
            dgx = dvn * norm
            m = jnp.mean(dgx * xhat, axis=-1, keepdims=True)
            dgv = r * (dgx - xhat * m)
            dproj_ref[rows, HALF:2 * HALF] = (dgu * _gelu_grad(uu)).astype(dproj_ref.dtype)
            dproj_ref[rows, 2 * HALF:3 * HALF] = (dgv * _gelu_grad(vv)).astype(dproj_ref.dtype)
            return carry

        lax.fori_loop(0, nchunk, chunk, 0)

    def col(j):
        return pl.BlockSpec((s, HALF), lambda b: (b, j))

    def whole(a):
        return pl.BlockSpec(a.shape, lambda b: (0,) * a.ndim)

    consts = [p["pool_w"], p["pool_scale"].reshape(1, HALF), p["sgu_norm"].reshape(1, HALF),
              p["sgu_w"], p["sgu_b"].T]
    gshapes = [jax.ShapeDtypeStruct(a.shape, F32) for a in consts]
    dproj, dpw, dps, dsn, dws, dbt = pl.pallas_call(
        body, name=tag + "_dcore", grid=(nb,),
        in_specs=[col(0), col(1), col(2)] + [whole(a) for a in consts]
        + [pl.BlockSpec((s, D_MODEL), lambda b: (b, 0))],
        out_specs=[pl.BlockSpec((s, 3 * HALF), lambda b: (b, 0))] + [whole(a) for a in consts],
        out_shape=[jax.ShapeDtypeStruct((t, 3 * HALF), _ACT)] + gshapes,
        scratch_shapes=[pltpu.VMEM((s + 2 * HALO, POOL_GROUP), F32)],
        compiler_params=_params("arbitrary"),
    )(proj, proj, proj, *consts, d_mix)
    return dproj, dict(pool_w=dpw, pool_scale=dps.reshape(HALF), sgu_norm=dsn.reshape(HALF), sgu_w=dws, sgu_b=dbt.T)


def _odd_fwd(tag, x, p, nb, s):
    h = _rmsnorm_fwd(tag + "_norm", x, p["norm"])
    proj = _proj_in(tag, h, p["w_in"])
    mix = _odd_core_fwd(tag, proj, p, nb, s)
    x_out = _proj_out(tag, x, [mix], p["w_out"])
    return x_out, (x, h, proj, mix)


def _odd_bwd(tag, dxo, saved, p, nb, s):
    x, h, proj, mix = saved
    d_mix, dw_out = _proj_out_bwd(tag, dxo, [mix], p["w_out"])
    dproj, grads = _odd_core_bwd(tag, proj, p, d_mix, nb, s)
    dx, dnorm, dw_in = _proj_in_bwd(tag, h, [dproj], p["w_in"], x, dxo, p["norm"])
    grads.update(norm=dnorm, w_in=dw_in, w_out=dw_out)
    return dx, grads


def _local_step(x3, target3, depth, weights_of, final_norm, grads_done):
    nb, s, d = x3.shape
    t = nb * s
    x = x3.reshape(t, d)
    target = target3.reshape(t, d)
    cos, sin = _rope_tables(s)
    saved, ws = [], []
    for l in range(depth):
        w1 = weights_of(l, "ffn1", x)
        x, s1 = _ffn_fwd(f"l{l}_ffn1", x, w1["norm"], w1["w_in4"], w1["w_out"], _ffn_tiles(l, 1))
        wm = weights_of(l, "mix", x)
        if l % 2 == 0:
            x, s2 = _even_fwd(f"l{l}_ev", x, wm, cos, sin, nb, s)
        else:
            x, s2 = _odd_fwd(f"l{l}_od", x, wm, nb, s)
        w2 = weights_of(l, "ffn2", x)
        x, s3 = _ffn_fwd(f"l{l}_ffn2", x, w2["norm"], w2["w_in4"], w2["w_out"], _ffn_tiles(l, 2))
        saved.append((s1, s2, s3))
        ws.append((w1, wm, w2))
    loss, dx, dfinal = _final_loss("final_loss", x, final_norm, target)
    zero = 0.0
    for l in reversed(range(depth)):
        s1, s2, s3 = saved[l]
        w1, wm, w2 = ws[l]
        dx, dn, dwi, dwo = _ffn_bwd(f"l{l}_ffn2", dx, s3, w2["norm"] + zero, w2["w_in4"], w2["w_out"], _ffn_tiles(l, 2))
        zero = grads_done(l, "ffn2", dict(norm=dn, w_in4=dwi, w_out=dwo), dx)
        wm = dict(wm, norm=wm["norm"] + zero)
        if l % 2 == 0:
            dx, gm = _even_bwd(f"l{l}_ev", dx, s2, wm, cos, sin, nb, s)
        else:
            dx, gm = _odd_bwd(f"l{l}_od", dx, s2, wm, nb, s)
        zero = grads_done(l, "mix", gm, dx)
        dx, dn, dwi, dwo = _ffn_bwd(f"l{l}_ffn1", dx, s1, w1["norm"] + zero, w1["w_in4"], w1["w_out"], _ffn_tiles(l, 1))
        zero = grads_done(l, "ffn1", dict(norm=dn, w_in4=dwi, w_out=dwo), dx)
    return loss, dx.reshape(nb, s, d), dfinal


_HBM = pl.BlockSpec(memory_space=pltpu.HBM)


def _place():
    x, y, c = lax.axis_index("x"), lax.axis_index("y"), lax.axis_index("c")
    chips = [(1 - x, y), (x, 1 - y), (1 - x, 1 - y)]
    return x, y, c, chips


def _remote(src, dst, send_sem, recv_sem, to):
    return pltpu.make_async_remote_copy(src_ref=src, dst_ref=dst, send_sem=send_sem, recv_sem=recv_sem,
                                        device_id=to, device_id_type=_MESH)


def _gather_shards(arrs, small):
    n = len(arrs)
    own = 6

    def body(*refs):
        ins, sm_in = refs[:n], refs[n]
        outs, sm_out = refs[n + 1:2 * n + 1], refs[2 * n + 1]
        send, recv = refs[2 * n + 2:]
        x, y, c, chips = _place()
        k = 2 * x + y
        sib = (x, y, 1 - c)
        started = []
        for a in range(n + 1):
            src, dst = (ins[a], outs[a]) if a < n else (sm_in, sm_out)
            cp = _remote(src, dst.at[k], send.at[a, own], recv.at[a, own], sib)
            cp.start()
            started.append(cp)
            if a < n:
                h = src.shape[0] // 2
                mine = pl.ds(c * h, h)
                src_part, dst_part = src.at[mine], dst.at[k, mine]
            else:
                src_part, dst_part = src, dst.at[k]
            for j, chip in enumerate(chips):
                cp = _remote(src_part, dst_part, send.at[a, j], recv.at[a, j], (*chip, c))
                cp.start()
                started.append(cp)
        for a in range(n):
            h = ins[a].shape[0] // 2
            mine = pl.ds(c * h, h)
            for j, (px, py) in enumerate(chips):
                landed = outs[a].at[2 * px + py, mine]
                _remote(landed, landed, send.at[a, j], recv.at[a, j], (px, py, c)).wait_recv()
                cp = _remote(landed, landed, send.at[a, 3 + j], recv.at[a, 3 + j], sib)
                cp.start()
                started.append(cp)
        for a in range(n):
            h = ins[a].shape[0] // 2
            other = pl.ds((1 - c) * h, h)
            for j, (px, py) in enumerate(chips):
                passed = outs[a].at[2 * px + py, other]
                _remote(passed, passed, send.at[a, 3 + j], recv.at[a, 3 + j], sib).wait_recv()
        for j, (px, py) in enumerate(chips):
            landed = sm_out.at[2 * px + py]
            _remote(landed, landed, send.at[n, j], recv.at[n, j], (px, py, c)).wait_recv()
        for a in range(n + 1):
            filled = (outs[a] if a < n else sm_out).at[k]
            _remote(filled, filled, send.at[a, own], recv.at[a, own], sib).wait_recv()
        for cp in started:
            cp.wait_send()

    outs = pl.pallas_call(
        body, name="gather_shards",
        in_specs=[_HBM] * (n + 1), out_specs=[_HBM] * (n + 1),
        out_shape=[jax.ShapeDtypeStruct((N_CHIPS,) + a.shape, a.dtype) for a in list(arrs) + [small]],
        scratch_shapes=[pltpu.SemaphoreType.DMA((n + 1, 7)), pltpu.SemaphoreType.DMA((n + 1, 7))],
    )(*arrs, small)
    return outs[:n], outs[n]


def _swap_halves(name, grads):
    n = len(grads)

    def body(*refs):
        ins, outs = refs[:n], refs[n:2 * n]
        send, recv = refs[2 * n:]
        x, y, c, _ = _place()
        sib = (x, y, 1 - c)
        cps = []
        for a in range(n):
            h = ins[a].shape[1] // 2
            cp = _remote(ins[a].at[:, pl.ds((1 - c) * h, h)], outs[a], send.at[a], recv.at[a], sib)
            cp.start()
            cps.append(cp)
        for cp in cps:
            cp.wait()

    return pl.pallas_call(
        body, name=name,
        in_specs=[_HBM] * n, out_specs=[_HBM] * n,
        out_shape=[jax.ShapeDtypeStruct((g.shape[0], g.shape[1] // 2) + g.shape[2:], g.dtype) for g in grads],
        scratch_shapes=[pltpu.SemaphoreType.DMA((n,)), pltpu.SemaphoreType.DMA((n,))],
    )(*grads)


_SEM = pl.BlockSpec(memory_space=pltpu.SEMAPHORE)
_EFFECT = pltpu.SideEffectType.DATAFLOW_SIDE_EFFECTING


def _gather_plan(src, land, k, c, chips, sib):
    mine = pl.ds(c * (src.shape[0] // 2), src.shape[0] // 2)
    plan = [(src.at[mine], land.at[k, mine], (px, py, c), land.at[2 * px + py, mine]) for px, py in chips]
    return plan + [(src, land.at[k], sib, land.at[k])]


def _scatter_plan(src, land, k, c, chips, sib):
    return [(src.at[2 * px + py], land.at[k], (px, py, c), land.at[2 * px + py]) for px, py in chips]


def _split_start(name, plan, ncopy, srcs, after):
    n = len(srcs)
    lands = [pltpu.with_memory_space_constraint(lax.empty((N_CHIPS,) + a.shape[-2:], a.dtype), pltpu.HBM) for a in srcs]

    def body(*refs):
        src_refs, land_refs = refs[1:1 + n], refs[1 + n:1 + 2 * n]
        send, recv, token = refs[1 + 2 * n], refs[2 + 2 * n], refs[-1]
        x, y, c, chips = _place()
        for i in range(n):
            for j, (src, dst, peer, _) in enumerate(plan(src_refs[i], land_refs[i], 2 * x + y, c, chips, (x, y, 1 - c))):
                _remote(src, dst, send.at[i * ncopy + j], recv.at[i * ncopy + j], peer).start()
        token[...] = jnp.zeros_like(token)

    outs = pl.pallas_call(
        body, name=name,
        in_specs=[_ANY] + [_HBM] * (2 * n),
        out_specs=[_SEM, _SEM] + [_HBM] * (2 * n) + [_VMEM],
        out_shape=[pltpu.SemaphoreType.DMA((n * ncopy,)), pltpu.SemaphoreType.DMA((n * ncopy,))]
        + [pltpu.HBM(a.shape, a.dtype) for a in list(srcs) + lands] + [jax.ShapeDtypeStruct((8, LANES), F32)],
        input_output_aliases={1 + i: 2 + i for i in range(2 * n)},
        compiler_params=pltpu.CompilerParams(has_side_effects=_EFFECT),
    )(after, *[pltpu.with_memory_space_constraint(a, pltpu.HBM) for a in srcs], *lands)
    return outs[0], outs[1], outs[2:2 + n], outs[2 + n:2 + 2 * n], outs[-1]


def _split_wait(name, plan, started, after):
    send, recv, srcs, lands = started
    n = len(srcs)
    ncopy = send.shape[0] // n

    def body(*refs):
        src_refs, land_refs = refs[:n], refs[n:2 * n]
        send, recv = refs[2 * n], refs[2 * n + 1]
        x, y, c, chips = _place()
        for i in range(n):
            for j, (src, _, peer, landed) in enumerate(plan(src_refs[i], land_refs[i], 2 * x + y, c, chips, (x, y, 1 - c))):
                cp = _remote(src, landed, send.at[i * ncopy + j], recv.at[i * ncopy + j], peer)
                cp.wait_send()
                cp.wait_recv()

    outs = pl.pallas_call(
        body, name=name,
        in_specs=[_HBM] * (2 * n) + [_SEM, _SEM, _ANY],
        out_specs=[_HBM] * (2 * n),
        out_shape=[pltpu.HBM(a.shape, a.dtype) for a in list(srcs) + list(lands)],
        input_output_aliases={i: i for i in range(2 * n)},
        compiler_params=pltpu.CompilerParams(has_side_effects=_EFFECT),
    )(*srcs, *lands, send, recv, after)
    return outs[:n], outs[n:]


def _pass_halves(name, lands):
    n = len(lands)

    def body(*refs):
        ins, outs = refs[:n], refs[n:2 * n]
        send, recv = refs[2 * n:]
        x, y, c, chips = _place()
        sib = (x, y, 1 - c)
        cps = []
        for a in range(n):
            h = ins[a].shape[1] // 2
            for j, (px, py) in enumerate(chips):
                rows = (2 * px + py, pl.ds(c * h, h))
                cp = _remote(ins[a].at[rows], outs[a].at[rows], send.at[a, j], recv.at[a, j], sib)
                cp.start()
                cps.append(cp)
        for a in range(n):
            h = ins[a].shape[1] // 2
            for j, (px, py) in enumerate(chips):
                theirs = outs[a].at[2 * px + py, pl.ds((1 - c) * h, h)]
                _remote(theirs, theirs, send.at[a, j], recv.at[a, j], sib).wait_recv()
        for cp in cps:
            cp.wait_send()

    return pl.pallas_call(
        body, name=name,
        in_specs=[_HBM] * n, out_specs=[_HBM] * n,
        out_shape=[jax.ShapeDtypeStruct(p.shape, p.dtype) for p in lands],
        input_output_aliases={a: a for a in range(n)},
        scratch_shapes=[pltpu.SemaphoreType.DMA((n, 3)), pltpu.SemaphoreType.DMA((n, 3))],
    )(*lands)


def _join_halves(name, bufs, layers):
    n = len(bufs)

    def body(*refs):
        ins, outs = refs[:n], refs[n:2 * n]
        send, recv = refs[2 * n:]
        x, y, c, _ = _place()
        sib = (x, y, 1 - c)
        cps = []
        for a in range(n):
            h = ins[a].shape[1] // 2
            mine = pl.ds(c * h, h)
            cp = _remote(ins[a].at[layers[a], mine], outs[a].at[layers[a], mine], send.at[a], recv.at[a], sib)
            cp.start()
            cps.append(cp)
        for a in range(n):
            h = ins[a].shape[1] // 2
            theirs = outs[a].at[layers[a], pl.ds((1 - c) * h, h)]
            _remote(theirs, theirs, send.at[a], recv.at[a], sib).wait_recv()
        for cp in cps:
            cp.wait_send()

    return pl.pallas_call(
        body, name=name,
        in_specs=[_HBM] * n, out_specs=[_HBM] * n,
        out_shape=[jax.ShapeDtypeStruct(p.shape, p.dtype) for p in bufs],
        input_output_aliases={a: a for a in range(n)},
        scratch_shapes=[pltpu.SemaphoreType.DMA((n,)), pltpu.SemaphoreType.DMA((n,))],
    )(*bufs)


def _allreduce_small(buf, after):
    rows = buf.shape[0]

    def body(in_ref, after_ref, out_ref, land_ref, send, recv):
        x, y, c, _ = _place()
        me = 4 * x + 2 * y + c
        land_ref[me] = in_ref[...]
        peers = []
        for r in range(1, N_DEV):
            peers.append((1 - x if r & 4 else x, 1 - y if r & 2 else y, 1 - c if r & 1 else c))
        cps = []
        for r, peer in enumerate(peers):
            cp = _remote(in_ref, land_ref.at[me], send.at[r], recv.at[r], peer)
            cp.start()
            cps.append(cp)
        for r, (px, py, pc) in enumerate(peers):
            landed = land_ref.at[4 * px + 2 * py + pc]
            _remote(landed, landed, send.at[r], recv.at[r], (px, py, pc)).wait_recv()
        for cp in cps:
            cp.wait_send()
        acc = land_ref[0]
        for d in range(1, N_DEV):
            acc = acc + land_ref[d]
        out_ref[...] = acc

    return pl.pallas_call(
        body, name="allreduce_small",
        in_specs=[_VMEM, _ANY], out_specs=_VMEM,
        out_shape=jax.ShapeDtypeStruct(buf.shape, F32),
        scratch_shapes=[pltpu.VMEM((N_DEV, rows, LANES), F32), pltpu.SemaphoreType.DMA((N_DEV - 1,)),
                        pltpu.SemaphoreType.DMA((N_DEV - 1,))],
        compiler_params=pltpu.CompilerParams(vmem_limit_bytes=_VMEM_LIMIT),
    )(buf, after)


def _div_tile(n, cap, mult):
    best = None
    for d in range(mult, min(n, cap) + 1, mult):
        if n % d == 0:
            best = d
    return best if best is not None else n


def _add_sibling(name, grad, got, c):
    nk, hr, cc = got.shape
    tr = _div_tile(hr, 512, 16)
    nt = hr // tr

    def body(c_ref, g_ref, o_ref, s_ref):
        s_ref[...] = (g_ref[...].astype(F32) + o_ref[...].astype(F32)).astype(s_ref.dtype)

    blk = (None, tr, cc)
    return pl.pallas_call(
        body, name=name,
        grid_spec=pltpu.PrefetchScalarGridSpec(
            num_scalar_prefetch=1, grid=(nk, nt),
            in_specs=[pl.BlockSpec(blk, lambda i, q, c_ref: (i, c_ref[0] * nt + q, 0)),
                      pl.BlockSpec(blk, lambda i, q, c_ref: (i, q, 0))],
            out_specs=pl.BlockSpec(blk, lambda i, q, c_ref: (i, q, 0))),
        out_shape=jax.ShapeDtypeStruct(got.shape, got.dtype),
        compiler_params=_params("parallel", "parallel"),
    )(c, grad, got)


def _add_chips(name, mine, got, place, buf, l):
    nk, hr, cc = got.shape
    tr = _div_tile(hr, 512, 16)
    nt = hr // tr

    def body(*refs):
        acc = refs[1][...].astype(F32)
        for q in range(1, nk):
            acc = acc + refs[1 + q][...].astype(F32)
        refs[2 + nk][...] = acc

    def part(q):
        return pl.BlockSpec((None, tr, cc), lambda i, p_ref: ((p_ref[0] + q) % nk, i, 0))

    return pl.pallas_call(
        body, name=name,
        grid_spec=pltpu.PrefetchScalarGridSpec(
            num_scalar_prefetch=1, grid=(nt,),
            in_specs=[part(q) for q in range(nk)] + [_ANY],
            out_specs=pl.BlockSpec((None, tr, cc), lambda i, p_ref: (l, p_ref[1] * nt + i, 0))),
        out_shape=jax.ShapeDtypeStruct(buf.shape, F32),
        input_output_aliases={1 + nk: 0},
        compiler_params=_params("parallel"),
    )(place, mine, *([got] * (nk - 1)), buf)


def _adamw(name, w, g, m, v, after=None):
    shape = w.shape
    cols = shape[-1]
    rows = w.size // cols
    tr = rows if rows * cols <= 2 ** 18 else _div_tile(rows, max(8, 2 ** 18 // cols), 8)
    c1 = 1.0 - ADAM_B1 ** ADAM_STEP
    c2 = 1.0 - ADAM_B2 ** ADAM_STEP
    extra = [] if after is None else [after]

    def body(*refs):
        w_ref, g_ref, m_ref, v_ref = refs[:4]
        d_ref, mo_ref, vo_ref, go_ref = refs[4 + len(extra):]
        gg = g_ref[...]
        mn = ADAM_B1 * m_ref[...] + (1.0 - ADAM_B1) * gg
        vn = ADAM_B2 * v_ref[...] + (1.0 - ADAM_B2) * (gg * gg)
        d_ref[...] = -ADAM_LR * ((mn / c1) / (jnp.sqrt(vn / c2) + ADAM_EPS) + ADAM_WD * w_ref[...])
        mo_ref[...] = mn
        vo_ref[...] = vn
        go_ref[...] = gg

    blk = pl.BlockSpec((tr, cols), lambda i: (i, 0))
    sds = jax.ShapeDtypeStruct((rows, cols), F32)
    outs = pl.pallas_call(
        body, name=name, grid=(rows // tr,),
        in_specs=[blk] * 4 + [_ANY] * len(extra), out_specs=(blk,) * 4, out_shape=(sds,) * 4,
        compiler_params=_params("parallel"),
    )(*[a.reshape(rows, cols) for a in (w, g, m, v)], *extra)
    return [o.reshape(shape) for o in outs]


_WEIGHTS = ["ffn1_norm", "ffn1_w_in", "ffn1_w_out", "mix_norm", "ffn2_norm", "ffn2_w_in", "ffn2_w_out",
            "ev_w_in", "ev_conv_w", "ev_q_norm", "ev_k_norm", "ev_w_out", "od_w_in", "od_pool_w",
            "od_pool_scale", "od_sgu_norm", "od_sgu_w", "od_sgu_b", "od_w_out", "final_norm"]
_BIG = ["ffn1_w_in", "ffn1_w_out", "ffn2_w_in", "ffn2_w_out", "ev_w_in", "ev_w_out", "od_w_in", "od_w_out"]
_SMALL_SHARDED = ["ev_conv_w", "od_pool_scale", "od_sgu_norm"]


def _pad_rows(a, mult=8):
    pad = (-a.shape[0]) % mult
    return a if pad == 0 else jnp.concatenate([a, jnp.zeros((pad,) + a.shape[1:], a.dtype)], axis=0)


def _join_cols(g):
    return g.transpose(1, 0, 2).reshape(g.shape[1], N_CHIPS * g.shape[2])


def _split_cols(w):
    return w.reshape(w.shape[0], N_CHIPS, w.shape[1] // N_CHIPS).transpose(1, 0, 2)


def kernel(x, ffn1_norm, ffn1_w_in, ffn1_w_out, mix_norm, ffn2_norm, ffn2_w_in, ffn2_w_out, ev_w_in, ev_conv_w,
           ev_q_norm, ev_k_norm, ev_w_out, od_w_in, od_pool_w, od_pool_scale, od_sgu_norm, od_sgu_w, od_sgu_b,
           od_w_out, final_norm, loss_target, m_ffn1_norm, m_ffn1_w_in, m_ffn1_w_out, m_mix_norm, m_ffn2_norm,
           m_ffn2_w_in, m_ffn2_w_out, m_ev_w_in, m_ev_conv_w, m_ev_q_norm, m_ev_k_norm, m_ev_w_out, m_od_w_in,
           m_od_pool_w, m_od_pool_scale, m_od_sgu_norm, m_od_sgu_w, m_od_sgu_b, m_od_w_out, m_final_norm, v_ffn1_norm,
           v_ffn1_w_in, v_ffn1_w_out, v_mix_norm, v_ffn2_norm, v_ffn2_w_in, v_ffn2_w_out, v_ev_w_in, v_ev_conv_w,
           v_ev_q_norm, v_ev_k_norm, v_ev_w_out, v_od_w_in, v_od_pool_w, v_od_pool_scale, v_od_sgu_norm, v_od_sgu_w,
           v_od_sgu_b, v_od_w_out, v_final_norm):
    return _step(x, ffn1_norm, ffn1_w_in, ffn1_w_out, mix_norm, ffn2_norm, ffn2_w_in, ffn2_w_out, ev_w_in, ev_conv_w,
                 ev_q_norm, ev_k_norm, ev_w_out, od_w_in, od_pool_w, od_pool_scale, od_sgu_norm, od_sgu_w, od_sgu_b,
                 od_w_out, final_norm, loss_target, m_ffn1_norm, m_ffn1_w_in, m_ffn1_w_out, m_mix_norm, m_ffn2_norm,
                 m_ffn2_w_in, m_ffn2_w_out, m_ev_w_in, m_ev_conv_w, m_ev_q_norm, m_ev_k_norm, m_ev_w_out, m_od_w_in,
                 m_od_pool_w, m_od_pool_scale, m_od_sgu_norm, m_od_sgu_w, m_od_sgu_b, m_od_w_out, m_final_norm,
                 v_ffn1_norm, v_ffn1_w_in, v_ffn1_w_out, v_mix_norm, v_ffn2_norm, v_ffn2_w_in, v_ffn2_w_out,
                 v_ev_w_in, v_ev_conv_w, v_ev_q_norm, v_ev_k_norm, v_ev_w_out, v_od_w_in, v_od_pool_w,
                 v_od_pool_scale, v_od_sgu_norm, v_od_sgu_w, v_od_sgu_b, v_od_w_out, v_final_norm)


def _step(*args):
    nw = len(_WEIGHTS)
    x = args[0]
    w = dict(zip(_WEIGHTS, args[1:1 + nw]))
    target = args[1 + nw]
    m = dict(zip(_WEIGHTS, args[2 + nw:2 + 2 * nw]))
    v = dict(zip(_WEIGHTS, args[2 + 2 * nw:2 + 3 * nw]))
    depth = w["ffn1_norm"].shape[0]
    n_even, n_odd = w["ev_w_in"].shape[0], w["od_w_in"].shape[0]
    chip = 2 * lax.axis_index("x") + lax.axis_index("y")
    place = jnp.stack([chip, lax.axis_index("c")]).astype(jnp.int32)
    core = place[1:2]

    def sharded(l, block):
        if block == "mix":
            block = "ev" if l % 2 == 0 else "od"
            return [(block + "_w_in", l // 2), (block + "_w_out", l // 2)]
        return [(block + "_w_in", l), (block + "_w_out", l)]

    def shards(group):
        return [w[n][i].astype(_ACT) for l, block in group for n, i in sharded(l, block)]

    small_rows = [w["ev_conv_w"].reshape(3 * n_even, LANES), w["od_pool_scale"], w["od_sgu_norm"]]
    first, small = _gather_shards(shards([(0, "ffn1"), (0, "mix")]), _pad_rows(jnp.concatenate(small_rows, axis=0)))
    conv_w = small[:, :3 * n_even].reshape(N_CHIPS, n_even, 3, LANES).transpose(1, 2, 0, 3).reshape(n_even, 3, CONV_WIDTH)
    pool_scale = small[:, 3 * n_even:3 * n_even + n_odd].transpose(1, 0, 2).reshape(n_odd, HALF)
    sgu_norm = small[:, 3 * n_even + n_odd:3 * n_even + 2 * n_odd].transpose(1, 0, 2).reshape(n_odd, HALF)
    later = [[(0, "ffn2")]] + [[(l, "ffn1"), (l, "mix"), (l, "ffn2")] for l in range(1, depth)]
    gathering, after = [], small
    for i, group in enumerate(later):
        gathering.append(_split_start(f"gather_start{i}", _gather_plan, N_CHIPS, shards(group), after))
        after = gathering[-1][4]
    gathered = {(0, "ffn1"): first[0:2], (0, "mix"): first[2:4]}

    def rows(g):
        return g.reshape(N_CHIPS * g.shape[1], g.shape[2])

    def weights_of(l, block, x_in):
        zero = after[0, 0] if (l, block) == (0, "ffn1") else 0.0
        if (l, block) not in gathered:
            i = next(i for i, group in enumerate(later) if (l, block) in group)
            got = _pass_halves(f"pass_halves{i}", _split_wait(f"gather_wait{i}", _gather_plan, gathering[i][:4], x_in)[1])
            for n, key in enumerate(later[i]):
                gathered[key] = got[2 * n:2 * n + 2]
        w_in, w_out = gathered[(l, block)]
        if block != "mix":
            return dict(norm=w[block + "_norm"][l] + zero, w_in4=w_in, w_out=rows(w_out))
        j = l // 2
        if l % 2 == 0:
            mix = dict(conv_w=conv_w[j], q_gain=w["ev_q_norm"][j], k_gain=w["ev_k_norm"][j])
        else:
            mix = dict(pool_w=w["od_pool_w"][j], pool_scale=pool_scale[j], sgu_norm=sgu_norm[j],
                       sgu_w=w["od_sgu_w"][j], sgu_b=w["od_sgu_b"][j])
        return dict(mix, norm=w["mix_norm"][l], w_in=_join_cols(w_in), w_out=rows(w_out))

    def by_chip(dw):
        return dw.reshape(N_CHIPS, dw.shape[0] // N_CHIPS, dw.shape[1])

    bufs = {n: lax.empty(w[n].shape, F32) for n in _BIG}
    small_grads = {n: [None] * w[n].shape[0] for n in _WEIGHTS if n not in _BIG and n != "final_norm"}
    scattering, group = [], []

    def finish_scatter(after):
        tag, names, started = scattering.pop()
        halves, got = _split_wait(f"scatter_wait{tag}", _scatter_plan, started[:4], after)
        for i, (n, j) in enumerate(names):
            bufs[n] = _add_chips(f"add_chips{tag}_{n}", halves[i], got[i], place, bufs[n], j)
        joined = _join_halves(f"join_halves{tag}", [bufs[n] for n, _ in names], [j for _, j in names])
        for (n, _), b in zip(names, joined):
            bufs[n] = b

    def grads_done(l, block, g, dx):
        j = l // 2
        if block == "mix":
            local = [_split_cols(g["w_in"]), by_chip(g["w_out"])]
            small_grads["mix_norm"][l] = g["norm"]
            renamed = (dict(conv_w="ev_conv_w", q_gain="ev_q_norm", k_gain="ev_k_norm") if l % 2 == 0 else
                       dict(pool_w="od_pool_w", pool_scale="od_pool_scale", sgu_norm="od_sgu_norm", sgu_w="od_sgu_w",
                            sgu_b="od_sgu_b"))
            for key, n in renamed.items():
                small_grads[n][j] = g[key]
        else:
            local = [g["w_in4"], by_chip(g["w_out"])]
            small_grads[block + "_norm"][l] = g["norm"]
        group.extend(zip(sharded(l, block), local))
        if block == "ffn2" or (block == "mix" and l > 0):
            return 0.0
        if scattering:
            finish_scatter(dx)
        tag = f"{l}_{block}"
        names, local = [n for n, _ in group], [a for _, a in group]
        group.clear()
        from_sibling = _swap_halves(f"swap_halves{tag}", local)
        halves = [_add_sibling(f"add_sibling{tag}_{n}", a, b, core) for (n, _), a, b in zip(names, local, from_sibling)]
        scattering.append((tag, names, _split_start(f"scatter_start{tag}", _scatter_plan, N_CHIPS - 1, halves, dx)))
        return scattering[-1][2][4][0, 0]

    loss_part, grad_x, dfinal = _local_step(x, target, depth, weights_of, w["final_norm"], grads_done)
    loss = lax.psum(loss_part, ("x", "y", "c"))

    updates = {}
    behind = scattering[-1][2][4]
    for n in _BIG:
        if not n.startswith("ffn1"):
            updates[n] = _adamw("adamw_" + n, w[n], bufs[n], m[n], v[n], behind)
            behind = updates[n][1]
    small_grads = {n: jnp.stack(parts) for n, parts in small_grads.items()}
    small_grads["final_norm"] = dfinal
    names = list(small_grads)
    flat = jnp.concatenate([small_grads[n].reshape(-1) for n in names])
    total = flat.shape[0]
    flat = jnp.concatenate([flat, jnp.zeros((-total) % (8 * LANES), F32)])
    summed = _allreduce_small(flat.reshape(-1, LANES), behind).reshape(-1)
    finish_scatter(summed)
    grads = dict(bufs)
    off = 0
    for n in names:
        size = small_grads[n].size
        full_grad = summed[off:off + size].reshape(small_grads[n].shape)
        off += size
        if n in _SMALL_SHARDED:
            full_grad = lax.dynamic_slice_in_dim(full_grad, chip * LANES, LANES, axis=full_grad.ndim - 1)
        grads[n] = full_grad
    for n in _WEIGHTS:
        if n not in updates:
            updates[n] = _adamw("adamw_" + n, w[n], grads[n], m[n], v[n])
    return (loss, grad_x, *[updates[n][3] for n in _WEIGHTS], *[updates[n][0] for n in _WEIGHTS],
            *[updates[n][1] for n in _WEIGHTS], *[updates[n][2] for n in _WEIGHTS])
```

```python
import jax
import jax.numpy as jnp
from jax import lax
from jax.experimental import pallas as pl
from jax.experimental.pallas import tpu as pltpu

F32 = jnp.float32
_MXU = jnp.bfloat16
_ACT = jnp.bfloat16

D_MODEL = 1024
GRID_W = 64
HEAD_DIM = 64
N_Q_HEADS = 8
N_KV_HEADS = 2
Q_PER_KV = N_Q_HEADS // N_KV_HEADS
ATTN_WIDTH = N_Q_HEADS * HEAD_DIM
KV_WIDTH = N_KV_HEADS * HEAD_DIM
ROPE_THETA = 10000.0
CONV_WIDTH = D_MODEL // 2
POOL_RADII = (1, 2, 4, 8)
POOL_GROUP = 128
SGU_GROUP = 128
SGU_CHUNK = 128
N_GROUPS = 4
HALF = D_MODEL // 2
EPS = 1e-6
HALO = 8
LANES = 128
N_CHIPS = 4
N_DEV = 8

ADAM_LR = 0.001
ADAM_B1 = 0.9
ADAM_B2 = 0.999
ADAM_EPS = 1e-08
ADAM_WD = 0.01
ADAM_STEP = 10

_VMEM_LIMIT = 56 * 2 ** 20
_MESH = pl.DeviceIdType.MESH
_ANY = pl.BlockSpec(memory_space=pl.ANY)
_VMEM = pl.BlockSpec(memory_space=pltpu.VMEM)

_DN = {
    "nn": (((1,), (0,)), ((), ())),
    "nt": (((1,), (1,)), ((), ())),
    "tn": (((0,), (0,)), ((), ())),
}


def _params(*sem):
    return pltpu.CompilerParams(dimension_semantics=sem, vmem_limit_bytes=_VMEM_LIMIT)


def _tile(n, cap):
    best = None
    d = LANES
    while d <= min(n, cap):
        if n % d == 0:
            best = d
        d += LANES
    return best if best is not None else n


def _dot(a, b, mode="nn"):
    return lax.dot_general(a.astype(_MXU), b.astype(_MXU), _DN[mode], preferred_element_type=F32)


def _cat(*vals):
    vals = [v.astype(_MXU) for v in vals]
    return vals[0] if len(vals) == 1 else jnp.concatenate(vals, axis=1)


def _sigmoid(g):
    return 1.0 / (1.0 + jnp.exp(-g))


def _norm_rows(x, g):
    r = lax.rsqrt(jnp.mean(x * x, axis=-1, keepdims=True) + EPS)
    return (x * r) * g


def _swiglu(g, u):
    return (g * _sigmoid(g)) * u


_GELU_C = 0.7978845608028654


def _gelu(x):
    return 0.5 * x * (1.0 + jnp.tanh(_GELU_C * (x + 0.044715 * (x * x * x))))


def _gelu_grad(x):
    t = jnp.tanh(_GELU_C * (x + 0.044715 * (x * x * x)))
    return 0.5 * (1.0 + t) + 0.5 * x * (1.0 - t * t) * (_GELU_C * (1.0 + 3.0 * 0.044715 * (x * x)))


def _mm(name, grid, mode, a_ops, b_ops, e_ops, out_shape, out_specs, acc_shape, a_fn=_cat, b_fn=_cat, epi=None,
        n_outer=False, m_carried=False):
    ni, nj, nk = grid
    na, nb, ne = len(a_ops), len(b_ops), len(e_ops)
    multi = isinstance(out_shape, (list, tuple))
    no = len(out_shape) if multi else 1

    def body(*refs):
        a_refs = refs[:na]
        b_refs = refs[na:na + nb]
        e_refs = refs[na + nb:na + nb + ne]
        o_refs = refs[na + nb + ne:na + nb + ne + no]
        a = a_fn(*[r[...] for r in a_refs])
        b = b_fn(*[r[...] for r in b_refs])
        p = _dot(a, b, mode)

        def finish(acc):
            if epi is None:
                o_refs[0][...] = acc.astype(o_refs[0].dtype)
            else:
                epi(acc, [r[...] for r in e_refs], o_refs)

        if nk == 1:
            finish(p)
        else:
            acc_ref = refs[-1]
            k = pl.program_id(2)

            @pl.when(k == 0)
            def _():
                acc_ref[...] = p

            @pl.when((k > 0) & (k < nk - 1))
            def _():
                acc_ref[...] += p

            @pl.when(k == nk - 1)
            def _():
                finish(acc_ref[...] + p)

    ops = list(a_ops) + list(b_ops) + list(e_ops)
    if n_outer:
        def flip(spec):
            return pl.BlockSpec(spec.block_shape, lambda j, i, k, f=spec.index_map: f(i, j, k))

        grid = (nj, ni, nk)
        ops = [(a, flip(s)) for a, s in ops]
        out_specs = [flip(s) for s in out_specs] if multi else flip(out_specs)
    return pl.pallas_call(
        body, name=name, grid=grid,
        in_specs=[s for _, s in ops],
        out_specs=out_specs, out_shape=out_shape,
        scratch_shapes=[pltpu.VMEM(acc_shape, F32)] if nk > 1 else [],
        compiler_params=_params(*(("arbitrary",) * 3 if m_carried else ("parallel", "parallel", "arbitrary"))),
    )(*[a for a, _ in ops])


def _norm_bwd_epi(acc, e, o):
    xf, dres, g = e
    r = lax.rsqrt(jnp.mean(xf * xf, axis=-1, keepdims=True) + EPS)
    xhat = xf * r
    dgx = acc * g
    m = jnp.mean(dgx * xhat, axis=-1, keepdims=True)
    o[0][...] = dres + r * (dgx - xhat * m)
    part = jnp.sum(acc * xhat, axis=0, keepdims=True)
    i = pl.program_id(0)

    @pl.when(i == 0)
    def _():
        o[1][...] = part

    @pl.when(i > 0)
    def _():
        o[1][...] += part


def _norm_bwd_ops(x, dres, gain, tm):
    t, d = x.shape
    row = pl.BlockSpec((tm, d), lambda i, j, k: (i, 0))
    vec = pl.BlockSpec((1, d), lambda i, j, k: (0, 0))
    return ([(x, row), (dres, row), (gain.reshape(1, d), vec)],
            [jax.ShapeDtypeStruct((t, d), F32), jax.ShapeDtypeStruct((1, d), F32)], [row, vec])


def _rows(t):
    return _tile(t, 512)


def _rmsnorm_fwd(name, x, gain):
    t, d = x.shape
    tr = _rows(t)

    def body(x_ref, g_ref, h_ref):
        xf = x_ref[...]
        r = lax.rsqrt(jnp.mean(xf * xf, axis=-1, keepdims=True) + EPS)
        h_ref[...] = ((xf * r) * g_ref[...]).astype(h_ref.dtype)

    return pl.pallas_call(
        body, name=name, grid=(t // tr,),
        in_specs=[pl.BlockSpec((tr, d), lambda i: (i, 0)), pl.BlockSpec((1, d), lambda i: (0, 0))],
        out_specs=pl.BlockSpec((tr, d), lambda i: (i, 0)),
        out_shape=jax.ShapeDtypeStruct((t, d), _ACT),
        compiler_params=_params("parallel"),
    )(x, gain.reshape(1, d))


def _rmsnorm_bwd(name, dh, x, gain, dres):
    t, d = x.shape
    tr = _rows(t)

    def body(dh_ref, x_ref, g_ref, dres_ref, dx_ref, dg_ref):
        i = pl.program_id(0)
        xf = x_ref[...]
        r = lax.rsqrt(jnp.mean(xf * xf, axis=-1, keepdims=True) + EPS)
        xhat = xf * r
        dy = dh_ref[...].astype(F32)
        dgx = dy * g_ref[...]
        m = jnp.mean(dgx * xhat, axis=-1, keepdims=True)
        dx_ref[...] = dres_ref[...] + r * (dgx - xhat * m)
        part = jnp.sum(dy * xhat, axis=0, keepdims=True)

        @pl.when(i == 0)
        def _():
            dg_ref[...] = part

        @pl.when(i > 0)
        def _():
            dg_ref[...] += part

    row = pl.BlockSpec((tr, d), lambda i: (i, 0))
    vec = pl.BlockSpec((1, d), lambda i: (0, 0))
    dx, dg = pl.pallas_call(
        body, name=name, grid=(t // tr,),
        in_specs=[row, row, vec, row],
        out_specs=(row, vec),
        out_shape=(jax.ShapeDtypeStruct((t, d), F32), jax.ShapeDtypeStruct((1, d), F32)),
        compiler_params=_params("arbitrary"),
    )(dh, x, gain.reshape(1, d), dres)
    return dx, dg.reshape(d)


def _final_loss(name, x, gain, target):
    t, d = x.shape
    tr = _rows(t)

    def body(x_ref, g_ref, t_ref, dx_ref, dg_ref, loss_ref):
        i = pl.program_id(0)
        xf = x_ref[...]
        r = lax.rsqrt(jnp.mean(xf * xf, axis=-1, keepdims=True) + EPS)
        xhat = xf * r
        g = g_ref[...]
        err = xhat * g - t_ref[...]
        lpart = 0.5 * jnp.sum(jnp.mean(err * err, axis=-1, keepdims=True), axis=0, keepdims=True)
        dy = err * (1.0 / d)
        dgx = dy * g
        m = jnp.mean(dgx * xhat, axis=-1, keepdims=True)
        dx_ref[...] = r * (dgx - xhat * m)
        part = jnp.sum(dy * xhat, axis=0, keepdims=True)
        lrow = jnp.broadcast_to(lpart, (1, LANES))

        @pl.when(i == 0)
        def _():
            dg_ref[...] = part
            loss_ref[...] = lrow

        @pl.when(i > 0)
        def _():
            dg_ref[...] += part
            loss_ref[...] += lrow

    row = pl.BlockSpec((tr, d), lambda i: (i, 0))
    vec = pl.BlockSpec((1, d), lambda i: (0, 0))
    dx, dg, loss = pl.pallas_call(
        body, name=name, grid=(t // tr,),
        in_specs=[row, vec, row],
        out_specs=(row, vec, pl.BlockSpec((1, LANES), lambda i: (0, 0))),
        out_shape=(jax.ShapeDtypeStruct((t, d), F32), jax.ShapeDtypeStruct((1, d), F32),
                   jax.ShapeDtypeStruct((1, LANES), F32)),
        compiler_params=_params("arbitrary"),
    )(x, gain.reshape(1, d), target)
    return loss[0, 0], dx, dg.reshape(d)


_FFN_TILES = dict(in_tm=1024, in_n_outer=False, out_tm=512, dact_tm=512, dwout_tk=1024, dh_tm=512, dwin_tk=2048)


def _ffn_tiles(layer, which):
    return _FFN_TILES


def _ffn_fwd(tag, x, gain, w_in4, w_out, cfg):
    t, d = x.shape
    fs = w_in4.shape[2]
    f = 2 * fs
    tm = _tile(t, cfg["in_tm"])
    gain = gain.reshape(1, d)
    gu = _mm(
        tag + "_in", (t // tm, N_CHIPS, 1), "nn",
        [(x, pl.BlockSpec((tm, d), lambda i, j, k: (i, 0))), (gain, pl.BlockSpec((1, d), lambda i, j, k: (0, 0)))],
        [(w_in4, pl.BlockSpec((None, d, fs), lambda i, j, k: (j, 0, 0)))], [],
        jax.ShapeDtypeStruct((2, t, f), _ACT),
        pl.BlockSpec((None, tm, fs), lambda i, j, k: (j // 2, i, j % 2)), None, a_fn=_norm_rows,
        n_outer=cfg["in_n_outer"])
    tm2 = _tile(t, cfg["out_tm"])

    def epi(acc, e, o):
        o[0][...] = e[0] + 0.5 * acc

    x_out = _mm(
        tag + "_out", (t // tm2, 1, 1), "nn",
        [(gu, pl.BlockSpec((None, tm2, f), lambda i, j, k: (0, i, 0))),
         (gu, pl.BlockSpec((None, tm2, f), lambda i, j, k: (1, i, 0)))],
        [(w_out, pl.BlockSpec((f, d), lambda i, j, k: (0, 0)))],
        [(x, pl.BlockSpec((tm2, d), lambda i, j, k: (i, 0)))],
        jax.ShapeDtypeStruct((t, d), F32),
        pl.BlockSpec((tm2, d), lambda i, j, k: (i, 0)), None,
        a_fn=_swiglu, epi=epi)
    return x_out, (x, gu)


def _ffn_bwd(tag, dxo, saved, gain, w_in4, w_out, cfg):
    x, gu = saved
    t, d = x.shape
    fs = w_in4.shape[2]
    f = 2 * fs
    tm = _tile(t, cfg["dact_tm"])
    tk = _tile(t, cfg["dwout_tk"])

    def epi_act(acc, e, o):
        g, u = e
        da = (0.5 * acc).astype(g.dtype)
        sig = _sigmoid(g)
        silu = g * sig
        o[0][0] = (da * u * (sig + silu * (1.0 - sig))).astype(o[0].dtype)
        o[0][1] = (da * silu).astype(o[0].dtype)

    dgu = _mm(
        tag + "_dact", (t // tm, 2, 1), "nt",
        [(dxo, pl.BlockSpec((tm, d), lambda i, j, k: (i, 0)))],
        [(w_out, pl.BlockSpec((fs, d), lambda i, j, k: (j, 0)))],
        [(gu, pl.BlockSpec((None, tm, fs), lambda i, j, k: (0, i, j))),
         (gu, pl.BlockSpec((None, tm, fs), lambda i, j, k: (1, i, j)))],
        jax.ShapeDtypeStruct((2, t, f), _ACT),
        pl.BlockSpec((2, tm, fs), lambda i, j, k: (0, i, j)), None, epi=epi_act)

    def epi_half(acc, e, o):
        o[0][...] = (0.5 * acc).astype(o[0].dtype)

    dw_out = _mm(
        tag + "_dwout", (2, 1, t // tk), "tn",
        [(gu, pl.BlockSpec((None, tk, fs), lambda i, j, k: (0, k, i))),
         (gu, pl.BlockSpec((None, tk, fs), lambda i, j, k: (1, k, i)))],
        [(dxo, pl.BlockSpec((tk, d), lambda i, j, k: (k, 0)))], [],
        jax.ShapeDtypeStruct((f, d), _ACT),
        pl.BlockSpec((fs, d), lambda i, j, k: (i, 0)), (fs, d),
        a_fn=_swiglu, epi=epi_half)
    tm = _tile(t, cfg["dh_tm"])
    e_ops, shapes, specs = _norm_bwd_ops(x, dxo, gain, tm)
    dx, dgain = _mm(
        tag + "_dh", (t // tm, 1, 2), "nt",
        [(dgu, pl.BlockSpec((None, tm, f), lambda i, j, k: (k, i, 0)))],
        [(w_in4, pl.BlockSpec((2, d, fs), lambda i, j, k: (k, 0, 0)))], e_ops, shapes, specs, (tm, d),
        b_fn=lambda b: jnp.concatenate([b[0], b[1]], axis=1), epi=_norm_bwd_epi, m_carried=True)
    tk = _tile(t, cfg["dwin_tk"])
    dw_in4 = _mm(
        tag + "_dwin", (1, N_CHIPS, t // tk), "tn",
        [(x, pl.BlockSpec((tk, d), lambda i, j, k: (k, 0))),
         (gain.reshape(1, d), pl.BlockSpec((1, d), lambda i, j, k: (0, 0)))],
        [(dgu, pl.BlockSpec((None, tk, fs), lambda i, j, k: (j // 2, k, j % 2)))], [],
        jax.ShapeDtypeStruct((N_CHIPS, d, fs), _ACT),
        pl.BlockSpec((None, d, fs), lambda i, j, k: (j, 0, 0)), (d, fs), a_fn=_norm_rows)
    return dx, dgain.reshape(d), dw_in4, dw_out


_MIX_TILES = dict(tm=1024, dwout_tk=2048, dwin_tk=1024)


def _proj_in(tag, h, w_in):
    t, d = h.shape
    n = w_in.shape[1]
    tm = _tile(t, _MIX_TILES["tm"])
    return _mm(
        tag + "_in", (t // tm, 1, 1), "nn",
        [(h, pl.BlockSpec((tm, d), lambda i, j, k: (i, 0)))],
        [(w_in, pl.BlockSpec((d, n), lambda i, j, k: (0, 0)))], [],
        jax.ShapeDtypeStruct((t, n), _ACT),
        pl.BlockSpec((tm, n), lambda i, j, k: (i, 0)), None)


def _proj_out(tag, x, parts, w_out):
    t, d = x.shape
    tm = _tile(t, _MIX_TILES["tm"])

    def epi(acc, e, o):
        o[0][...] = e[0] + acc

    return _mm(
        tag + "_out", (t // tm, 1, 1), "nn",
        [(p, pl.BlockSpec((tm, p.shape[1]), lambda i, j, k: (i, 0))) for p in parts],
        [(w_out, pl.BlockSpec(w_out.shape, lambda i, j, k: (0, 0)))],
        [(x, pl.BlockSpec((tm, d), lambda i, j, k: (i, 0)))],
        jax.ShapeDtypeStruct((t, d), F32),
        pl.BlockSpec((tm, d), lambda i, j, k: (i, 0)), None, epi=epi)


def _proj_out_bwd(tag, dxo, parts, w_out):
    t, d = dxo.shape
    mix = w_out.shape[0]
    tm = _tile(t, _MIX_TILES["tm"])
    tk = _tile(t, _MIX_TILES["dwout_tk"])
    d_mix = _mm(
        tag + "_dmix", (t // tm, 1, 1), "nt",
        [(dxo, pl.BlockSpec((tm, d), lambda i, j, k: (i, 0)))],
        [(w_out, pl.BlockSpec((mix, d), lambda i, j, k: (0, 0)))], [],
        jax.ShapeDtypeStruct((t, mix), F32),
        pl.BlockSpec((tm, mix), lambda i, j, k: (i, 0)), None)
    dw_out = _mm(
        tag + "_dwout", (1, 1, t // tk), "tn",
        [(p, pl.BlockSpec((tk, p.shape[1]), lambda i, j, k: (k, 0))) for p in parts],
        [(dxo, pl.BlockSpec((tk, d), lambda i, j, k: (k, 0)))], [],
        jax.ShapeDtypeStruct((mix, d), _ACT),
        pl.BlockSpec((mix, d), lambda i, j, k: (0, 0)), (mix, d))
    return d_mix, dw_out


def _proj_in_bwd(tag, h, dparts, w_in, x, dres, gain):
    t, d = h.shape
    n = w_in.shape[1]
    tm = _tile(t, _MIX_TILES["tm"])
    tk = _tile(t, _MIX_TILES["dwin_tk"])
    e_ops, shapes, specs = _norm_bwd_ops(x, dres, gain, tm)
    dx, dgain = _mm(
        tag + "_dh", (t // tm, 1, 1), "nt",
        [(p, pl.BlockSpec((tm, p.shape[1]), lambda i, j, k: (i, 0))) for p in dparts],
        [(w_in, pl.BlockSpec((d, n), lambda i, j, k: (0, 0)))], e_ops, shapes, specs, None,
        epi=_norm_bwd_epi, m_carried=True)
    dw_in = _mm(
        tag + "_dwin", (1, 1, t // tk), "tn",
        [(h, pl.BlockSpec((tk, d), lambda i, j, k: (k, 0)))],
        [(p, pl.BlockSpec((tk, p.shape[1]), lambda i, j, k: (k, 0))) for p in dparts], [],
        jax.ShapeDtypeStruct((d, n), _ACT),
        pl.BlockSpec((d, n), lambda i, j, k: (0, 0)), (d, n))
    return dx, dgain.reshape(d), dw_in


def _shifted(pad_ref, val, s):
    pad_ref[pl.ds(HALO, s), :] = val
    return pad_ref[pl.ds(HALO - 1, s), :], pad_ref[pl.ds(HALO + 1, s), :]


def _zero_halo(pad_ref, s):
    z = jnp.zeros((HALO, pad_ref.shape[1]), F32)
    pad_ref[pl.ds(0, HALO), :] = z
    pad_ref[pl.ds(HALO + s, HALO), :] = z


def _conv_fwd(tag, proj, conv_w, nb, s):
    t = proj.shape[0]
    ncb = CONV_WIDTH // LANES

    def body(gb_ref, gc_ref, hc_ref, w_ref, a_ref, pad_ref):
        _zero_halo(pad_ref, s)
        cg = gc_ref[...].astype(F32) * hc_ref[...].astype(F32)
        prev, nxt = _shifted(pad_ref, cg, s)
        w = w_ref[...]
        conv = prev * w[0:1, :] + cg * w[1:2, :] + nxt * w[2:3, :]
        a_ref[...] = (gb_ref[...].astype(F32) * conv).astype(a_ref.dtype)

    def col(off):
        return pl.BlockSpec((s, LANES), lambda b, c: (b, off + c))

    return pl.pallas_call(
        body, name=tag + "_conv", grid=(nb, ncb),
        in_specs=[col(0), col(ncb), col(2 * ncb), pl.BlockSpec((3, LANES), lambda b, c: (0, c))],
        out_specs=col(0),
        out_shape=jax.ShapeDtypeStruct((t, CONV_WIDTH), _ACT),
        scratch_shapes=[pltpu.VMEM((s + 2 * HALO, LANES), F32)],
        compiler_params=_params("parallel", "parallel"),
    )(proj, proj, proj, conv_w)


def _conv_bwd(tag, proj, conv_w, d_mix, nb, s):
    t = proj.shape[0]
    ncb = CONV_WIDTH // LANES

    def body(gb_ref, gc_ref, hc_ref, w_ref, da_ref, dgb_ref, dgc_ref, dhc_ref, dw_ref, pad_ref):
        b = pl.program_id(1)
        _zero_halo(pad_ref, s)
        gb = gb_ref[...].astype(F32)
        gc = gc_ref[...].astype(F32)
        hc = hc_ref[...].astype(F32)
        w = w_ref[...]
        da = da_ref[...]
        cg = gc * hc
        prev, nxt = _shifted(pad_ref, cg, s)
        conv = prev * w[0:1, :] + cg * w[1:2, :] + nxt * w[2:3, :]
        dgb_ref[...] = (da * conv).astype(dgb_ref.dtype)
        dconv = da * gb
        dw = jnp.concatenate([
            jnp.sum(dconv * prev, axis=0, keepdims=True),
            jnp.sum(dconv * cg, axis=0, keepdims=True),
            jnp.sum(dconv * nxt, axis=0, keepdims=True)], axis=0)
        dprev, dnxt = _shifted(pad_ref, dconv, s)
        dcg = dnxt * w[0:1, :] + dconv * w[1:2, :] + dprev * w[2:3, :]
        dgc_ref[...] = (dcg * hc).astype(dgc_ref.dtype)
        dhc_ref[...] = (dcg * gc).astype(dhc_ref.dtype)

        @pl.when(b == 0)
        def _():
            dw_ref[...] = dw

        @pl.when(b > 0)
        def _():
            dw_ref[...] += dw

    def col(off):
        return pl.BlockSpec((s, LANES), lambda c, b: (b, off + c))

    wspec = pl.BlockSpec((3, LANES), lambda c, b: (0, c))
    act = jax.ShapeDtypeStruct((t, CONV_WIDTH), _ACT)
    return pl.pallas_call(
        body, name=tag + "_dconv", grid=(ncb, nb),
        in_specs=[col(0), col(ncb), col(2 * ncb), wspec, col(0)],
        out_specs=(col(0), col(0), col(0), wspec),
        out_shape=(act, act, act, jax.ShapeDtypeStruct((3, CONV_WIDTH), F32)),
        scratch_shapes=[pltpu.VMEM((s + 2 * HALO, LANES), F32)],
        compiler_params=_params("parallel", "arbitrary"),
    )(proj, proj, proj, conv_w, d_mix)


def _rope_tables(s):
    rows = s // GRID_W
    r_idx, c_idx = jnp.meshgrid(jnp.arange(rows), jnp.arange(GRID_W), indexing="ij")
    r_idx = r_idx.reshape(-1).astype(F32)
    c_idx = c_idx.reshape(-1).astype(F32)
    n_freq = HEAD_DIM // 4
    inv = ROPE_THETA ** (-jnp.arange(n_freq, dtype=F32) / n_freq)
    ang = jnp.concatenate([r_idx[:, None] * inv, c_idx[:, None] * inv], axis=-1)
    cos = jnp.repeat(jnp.cos(ang), 2, axis=1)
    sin = jnp.repeat(jnp.sin(ang), 2, axis=1)
    sign = jnp.where(jnp.arange(HEAD_DIM) % 2 == 0, -1.0, 1.0).astype(F32)
    return jnp.tile(cos, (1, LANES // HEAD_DIM)), jnp.tile(sin * sign, (1, LANES // HEAD_DIM))


def _head_ones():
    i = jnp.arange(LANES) // HEAD_DIM
    return (i[:, None] == i[None, :]).astype(jnp.bfloat16)


def _head_sum(v, ones):
    outs = []
    for j in range(v.shape[1] // LANES):
        c = v[:, j * LANES:(j + 1) * LANES]
        hi = c.astype(jnp.bfloat16)
        lo = (c - hi.astype(F32)).astype(jnp.bfloat16)
        outs.append(jnp.dot(hi, ones, preferred_element_type=F32) + jnp.dot(lo, ones, preferred_element_type=F32))
    return outs[0] if len(outs) == 1 else jnp.concatenate(outs, axis=1)


def _pair_swap(v):
    outs = []
    for j in range(v.shape[1] // LANES):
        c = v[:, j * LANES:(j + 1) * LANES]
        lane = lax.broadcasted_iota(jnp.int32, c.shape, 1)
        outs.append(jnp.where(lane % 2 == 0, pltpu.roll(c, LANES - 1, 1), pltpu.roll(c, 1, 1)))
    return outs[0] if len(outs) == 1 else jnp.concatenate(outs, axis=1)


def _wide(tab, width):
    return tab if width == LANES else jnp.concatenate([tab] * (width // LANES), axis=1)


_QK_SCALE = HEAD_DIM ** -0.5


def _qk_fwd(tag, proj, q_gain, k_gain, cos, sin, nb, s):
    t = proj.shape[0]
    tr = _tile(s, 512)
    ns = s // tr
    q_off = 3 * CONV_WIDTH // ATTN_WIDTH
    k_off = (3 * CONV_WIDTH + ATTN_WIDTH) // KV_WIDTH

    def body(q_ref, k_ref, qg_ref, kg_ref, cos_ref, sin_ref, ones_ref, qo_ref, ko_ref):
        ones = ones_ref[...]
        for src, g_ref, dst, mult in ((q_ref, qg_ref, qo_ref, _QK_SCALE), (k_ref, kg_ref, ko_ref, 1.0)):
            v = src[...].astype(F32)
            w = v.shape[1]
            r = lax.rsqrt(_head_sum(v * v, ones) * (1.0 / HEAD_DIM) + EPS)
            vn = (v * r) * g_ref[...]
            rot = vn * _wide(cos_ref[...], w) + _pair_swap(vn) * _wide(sin_ref[...], w)
            dst[...] = (rot * mult).astype(dst.dtype)

    tab = pl.BlockSpec((tr, LANES), lambda i: (i % ns, 0))
    return pl.pallas_call(
        body, name=tag + "_qk", grid=(t // tr,),
        in_specs=[pl.BlockSpec((tr, ATTN_WIDTH), lambda i: (i, q_off)),
                  pl.BlockSpec((tr, KV_WIDTH), lambda i: (i, k_off)),
                  pl.BlockSpec((1, ATTN_WIDTH), lambda i: (0, 0)),
                  pl.BlockSpec((1, KV_WIDTH), lambda i: (0, 0)),
                  tab, tab, pl.BlockSpec((LANES, LANES), lambda i: (0, 0))],
        out_specs=(pl.BlockSpec((tr, ATTN_WIDTH), lambda i: (i, 0)),
                   pl.BlockSpec((tr, KV_WIDTH), lambda i: (i, 0))),
        out_shape=(jax.ShapeDtypeStruct((t, ATTN_WIDTH), _ACT), jax.ShapeDtypeStruct((t, KV_WIDTH), _ACT)),
        compiler_params=_params("parallel"),
    )(proj, proj, jnp.tile(q_gain, N_Q_HEADS).reshape(1, ATTN_WIDTH),
      jnp.tile(k_gain, N_KV_HEADS).reshape(1, KV_WIDTH), cos, sin, _head_ones())


def _qk_bwd(tag, proj, q_gain, k_gain, cos, sin, dq_rot, dk_rot, nb, s):
    t = proj.shape[0]
    tr = _tile(s, 512)
    ns = s // tr
    q_off = 3 * CONV_WIDTH // ATTN_WIDTH
    k_off = (3 * CONV_WIDTH + ATTN_WIDTH) // KV_WIDTH

    def body(q_ref, k_ref, qg_ref, kg_ref, cos_ref, sin_ref, ones_ref, dqr_ref, dkr_ref,
             dq_ref, dk_ref, dqg_ref, dkg_ref):
        i = pl.program_id(0)
        ones = ones_ref[...]
        for src, g_ref, dr_ref, dst, dg_ref, mult in ((q_ref, qg_ref, dqr_ref, dq_ref, dqg_ref, _QK_SCALE),
                                                      (k_ref, kg_ref, dkr_ref, dk_ref, dkg_ref, 1.0)):
            v = src[...].astype(F32)
            w = v.shape[1]
            r = lax.rsqrt(_head_sum(v * v, ones) * (1.0 / HEAD_DIM) + EPS)
            xhat = v * r
            dr = dr_ref[...] * mult
            dvn = dr * _wide(cos_ref[...], w) + _pair_swap(dr * _wide(sin_ref[...], w))
            dgx = dvn * g_ref[...]
            m = _head_sum(dgx * xhat, ones) * (1.0 / HEAD_DIM)
            dst[...] = (r * (dgx - xhat * m)).astype(dst.dtype)
            part = jnp.sum(dvn * xhat, axis=0, keepdims=True)
            fold = part[:, 0:HEAD_DIM]
            for hh in range(1, w // HEAD_DIM):
                fold = fold + part[:, hh * HEAD_DIM:(hh + 1) * HEAD_DIM]

            @pl.when(i == 0)
            def _():
                dg_ref[...] = fold

            @pl.when(i > 0)
            def _():
                dg_ref[...] += fold

    tab = pl.BlockSpec((tr, LANES), lambda i: (i % ns, 0))
    qrow = pl.BlockSpec((tr, ATTN_WIDTH), lambda i: (i, 0))
    krow = pl.BlockSpec((tr, KV_WIDTH), lambda i: (i, 0))
    gvec = pl.BlockSpec((1, HEAD_DIM), lambda i: (0, 0))
    dq, dk, dqg, dkg = pl.pallas_call(
        body, name=tag + "_dqk", grid=(t // tr,),
        in_specs=[pl.BlockSpec((tr, ATTN_WIDTH), lambda i: (i, q_off)),
                  pl.BlockSpec((tr, KV_WIDTH), lambda i: (i, k_off)),
                  pl.BlockSpec((1, ATTN_WIDTH), lambda i: (0, 0)),
                  pl.BlockSpec((1, KV_WIDTH), lambda i: (0, 0)),
                  tab, tab, pl.BlockSpec((LANES, LANES), lambda i: (0, 0)), qrow, krow],
        out_specs=(qrow, krow, gvec, gvec),
        out_shape=(jax.ShapeDtypeStruct((t, ATTN_WIDTH), _ACT), jax.ShapeDtypeStruct((t, KV_WIDTH), _ACT),
                   jax.ShapeDtypeStruct((1, HEAD_DIM), F32), jax.ShapeDtypeStruct((1, HEAD_DIM), F32)),
        compiler_params=_params("arbitrary"),
    )(proj, proj, jnp.tile(q_gain, N_Q_HEADS).reshape(1, ATTN_WIDTH),
      jnp.tile(k_gain, N_KV_HEADS).reshape(1, KV_WIDTH), cos, sin, _head_ones(), dq_rot, dk_rot)
    return dq, dk, dqg.reshape(HEAD_DIM), dkg.reshape(HEAD_DIM)


def _head(v, h):
    return v[:, h * HEAD_DIM:(h + 1) * HEAD_DIM]


def _attn_fwd(tag, q, k, proj, nb, s):
    t = q.shape[0]
    tq = _tile(s, 256)
    nq = s // tq
    v_off = (3 * CONV_WIDTH + ATTN_WIDTH + KV_WIDTH) // KV_WIDTH

    def body(q_ref, k_ref, v_ref, o_ref, lse_ref):
        qv = q_ref[...]
        kv = k_ref[...]
        vv = v_ref[...]
        for h in range(N_Q_HEADS):
            j = h // Q_PER_KV
            sc = _dot(_head(qv, h), _head(kv, j), "nt")
            m = jnp.max(sc, axis=-1, keepdims=True)
            e = jnp.exp(sc - m)
            l = jnp.sum(e, axis=-1, keepdims=True)
            o = _dot(e, _head(vv, j)) * (1.0 / l)
            o_ref[:, h * HEAD_DIM:(h + 1) * HEAD_DIM] = o.astype(o_ref.dtype)
            lse_ref[:, h:h + 1] = m + jnp.log(l)

    return pl.pallas_call(
        body, name=tag + "_attn", grid=(nb, nq),
        in_specs=[pl.BlockSpec((tq, ATTN_WIDTH), lambda b, i: (b * nq + i, 0)),
                  pl.BlockSpec((s, KV_WIDTH), lambda b, i: (b, 0)),
                  pl.BlockSpec((s, KV_WIDTH), lambda b, i: (b, v_off))],
        out_specs=(pl.BlockSpec((tq, ATTN_WIDTH), lambda b, i: (b * nq + i, 0)),
                   pl.BlockSpec((tq, N_Q_HEADS), lambda b, i: (b * nq + i, 0))),
        out_shape=(jax.ShapeDtypeStruct((t, ATTN_WIDTH), _ACT), jax.ShapeDtypeStruct((t, N_Q_HEADS), F32)),
        compiler_params=_params("parallel", "parallel"),
    )(q, k, proj)


def _attn_bwd(tag, q, k, proj, o, lse, d_mix, nb, s):
    t = q.shape[0]
    tq = _tile(s, 256)
    nq = s // tq
    v_off = (3 * CONV_WIDTH + ATTN_WIDTH + KV_WIDTH) // KV_WIDTH

    def body(q_ref, k_ref, v_ref, o_ref, lse_ref, do_ref, dq_ref, dk_ref, dv_ref):
        i = pl.program_id(1)

        @pl.when(i == 0)
        def _():
            dk_ref[...] = jnp.zeros_like(dk_ref)
            dv_ref[...] = jnp.zeros_like(dv_ref)

        qv = q_ref[...]
        kv = k_ref[...]
        vv = v_ref[...]
        ov = o_ref[...].astype(F32)
        dov = do_ref[...]
        lse = lse_ref[...]
        for h in range(N_Q_HEADS):
            j = h // Q_PER_KV
            cols = slice(j * HEAD_DIM, (j + 1) * HEAD_DIM)
            qh = _head(qv, h)
            kj = _head(kv, j)
            doh = _head(dov, h)
            sc = _dot(qh, kj, "nt")
            p = jnp.exp(sc - lse[:, h:h + 1])
            dp = _dot(doh, _head(vv, j), "nt")
            delta = jnp.sum(doh * _head(ov, h), axis=-1, keepdims=True)
            ds = p * (dp - delta)
            dv_ref[:, cols] += _dot(p, doh, "tn")
            dk_ref[:, cols] += _dot(ds, qh, "tn")
            dq_ref[:, h * HEAD_DIM:(h + 1) * HEAD_DIM] = _dot(ds, kj)

    qrow = pl.BlockSpec((tq, ATTN_WIDTH), lambda b, i: (b * nq + i, 0))
    kvrow = pl.BlockSpec((s, KV_WIDTH), lambda b, i: (b, 0))
    return pl.pallas_call(
        body, name=tag + "_dattn", grid=(nb, nq),
        in_specs=[qrow, kvrow, pl.BlockSpec((s, KV_WIDTH), lambda b, i: (b, v_off)), qrow,
                  pl.BlockSpec((tq, N_Q_HEADS), lambda b, i: (b * nq + i, 0)),
                  pl.BlockSpec((tq, ATTN_WIDTH), lambda b, i: (b * nq + i, 1))],
        out_specs=(qrow, kvrow, kvrow),
        out_shape=(jax.ShapeDtypeStruct((t, ATTN_WIDTH), F32), jax.ShapeDtypeStruct((t, KV_WIDTH), F32),
                   jax.ShapeDtypeStruct((t, KV_WIDTH), F32)),
        compiler_params=_params("parallel", "arbitrary"),
    )(q, k, proj, o, lse, d_mix)


def _even_fwd(tag, x, p, cos, sin, nb, s):
    h = _rmsnorm_fwd(tag + "_norm", x, p["norm"])
    proj = _proj_in(tag, h, p["w_in"])
    a = _conv_fwd(tag, proj, p["conv_w"], nb, s)
    q, k = _qk_fwd(tag, proj, p["q_gain"], p["k_gain"], cos, sin, nb, s)
    o, lse = _attn_fwd(tag, q, k, proj, nb, s)
    x_out = _proj_out(tag, x, [a, o], p["w_out"])
    return x_out, (x, h, proj, a, q, k, o, lse)


def _even_bwd(tag, dxo, saved, p, cos, sin, nb, s):
    x, h, proj, a, q, k, o, lse = saved
    d_mix, dw_out = _proj_out_bwd(tag, dxo, [a, o], p["w_out"])
    dgb, dgc, dhc, dconv_w = _conv_bwd(tag, proj, p["conv_w"], d_mix, nb, s)
    dq_rot, dk_rot, dv = _attn_bwd(tag, q, k, proj, o, lse, d_mix, nb, s)
    dq, dk, dq_gain, dk_gain = _qk_bwd(tag, proj, p["q_gain"], p["k_gain"], cos, sin, dq_rot, dk_rot, nb, s)
    dx, dnorm, dw_in = _proj_in_bwd(tag, h, [dgb, dgc, dhc, dq, dk, dv], p["w_in"], x, dxo, p["norm"])
    grads = dict(norm=dnorm, w_in=dw_in, w_out=dw_out, conv_w=dconv_w, q_gain=dq_gain, k_gain=dk_gain)
    return dx, grads


def _window(pad_ref, val, r, s):
    pad_ref[pl.ds(HALO, s), :] = val
    acc = val
    for d in range(1, r + 1):
        acc = acc + pad_ref[pl.ds(HALO - d, s), :] + pad_ref[pl.ds(HALO + d, s), :]
    return acc


def _count(r, s):
    t = lax.broadcasted_iota(jnp.int32, (s, 1), 0)
    return (jnp.minimum(t + r, s - 1) - jnp.maximum(t - r, 0) + 1).astype(F32)


def _sgu_chunk(u_ref, v_ref, norm, ws_ref, bt, rows):
    uu = u_ref[rows, :].astype(F32)
    vv = v_ref[rows, :].astype(F32)
    gu = _gelu(uu)
    gv = _gelu(vv)
    r = lax.rsqrt(jnp.mean(gv * gv, axis=-1, keepdims=True) + EPS)
    xhat = gv * r
    vn = xhat * norm
    mixed = []
    for g in range(N_GROUPS):
        cols = slice(g * SGU_GROUP, (g + 1) * SGU_GROUP)
        mixed.append(_dot(ws_ref[g], vn[:, cols]) + bt[:, g:g + 1])
    return uu, vv, gu, r, xhat, vn, mixed


def _odd_core_fwd(tag, proj, p, nb, s):
    t = proj.shape[0]
    nchunk = s // SGU_CHUNK

    def body(p_ref, u_ref, v_ref, pw_ref, ps_ref, sn_ref, ws_ref, bt_ref, mix_ref, pad_ref):
        _zero_halo(pad_ref, s)
        for g, r in enumerate(POOL_RADII):
            cols = slice(g * POOL_GROUP, (g + 1) * POOL_GROUP)
            pg = p_ref[:, cols].astype(F32)
            pooled = _window(pad_ref, pg, r, s) / _count(r, s) - pg
            mix_ref[:, cols] = (_dot(pooled, pw_ref[g]) * ps_ref[:, cols]).astype(mix_ref.dtype)
        norm = sn_ref[...]
        bt = bt_ref[...]

        def chunk(n, carry):
            rows = pl.ds(pl.multiple_of(n * SGU_CHUNK, SGU_CHUNK), SGU_CHUNK)
            _, _, gu, _, _, _, mixed = _sgu_chunk(u_ref, v_ref, norm, ws_ref, bt, rows)
            for g in range(N_GROUPS):
                cols = slice(g * SGU_GROUP, (g + 1) * SGU_GROUP)
                mix_ref[rows, HALF + g * SGU_GROUP:HALF + (g + 1) * SGU_GROUP] = (
                    gu[:, cols] * mixed[g]).astype(mix_ref.dtype)
            return carry

        lax.fori_loop(0, nchunk, chunk, 0)

    def col(j):
        return pl.BlockSpec((s, HALF), lambda b: (b, j))

    def whole(a):
        return pl.BlockSpec(a.shape, lambda b: (0,) * a.ndim)

    consts = [p["pool_w"], p["pool_scale"].reshape(1, HALF), p["sgu_norm"].reshape(1, HALF),
              p["sgu_w"], p["sgu_b"].T]
    return pl.pallas_call(
        body, name=tag + "_core", grid=(nb,),
        in_specs=[col(0), col(1), col(2)] + [whole(a) for a in consts],
        out_specs=pl.BlockSpec((s, D_MODEL), lambda b: (b, 0)),
        out_shape=jax.ShapeDtypeStruct((t, D_MODEL), _ACT),
        scratch_shapes=[pltpu.VMEM((s + 2 * HALO, POOL_GROUP), F32)],
        compiler_params=_params("parallel"),
    )(proj, proj, proj, *consts)


def _odd_core_bwd(tag, proj, p, d_mix, nb, s):
    t = proj.shape[0]
    nchunk = s // SGU_CHUNK

    def body(p_ref, u_ref, v_ref, pw_ref, ps_ref, sn_ref, ws_ref, bt_ref, dm_ref,
             dproj_ref, dpw_ref, dps_ref, dsn_ref, dws_ref, dbt_ref, pad_ref):
        b = pl.program_id(0)

        @pl.when(b == 0)
        def _():
            dpw_ref[...] = jnp.zeros_like(dpw_ref)
            dps_ref[...] = jnp.zeros_like(dps_ref)
            dsn_ref[...] = jnp.zeros_like(dsn_ref)
            dws_ref[...] = jnp.zeros_like(dws_ref)
            dbt_ref[...] = jnp.zeros_like(dbt_ref)

        _zero_halo(pad_ref, s)
        for g, r in enumerate(POOL_RADII):
            cols = slice(g * POOL_GROUP, (g + 1) * POOL_GROUP)
            pg = p_ref[:, cols].astype(F32)
            cnt = _count(r, s)
            pooled = _window(pad_ref, pg, r, s) / cnt - pg
            c_pre = _dot(pooled, pw_ref[g])
            dc = dm_ref[:, cols]
            dps_ref[:, cols] += jnp.sum(dc * c_pre, axis=0, keepdims=True)
            dcp = dc * ps_ref[:, cols]
            dpw_ref[g] += _dot(pooled, dcp, "tn")
            dpooled = _dot(dcp, pw_ref[g], "nt")
            dproj_ref[:, cols] = (_window(pad_ref, dpooled / cnt, r, s) - dpooled).astype(dproj_ref.dtype)
        norm = sn_ref[...]
        bt = bt_ref[...]

        def chunk(n, carry):
            rows = pl.ds(pl.multiple_of(n * SGU_CHUNK, SGU_CHUNK), SGU_CHUNK)
            uu, vv, gu, r, xhat, vn, mixed = _sgu_chunk(u_ref, v_ref, norm, ws_ref, bt, rows)
            dd = dm_ref[rows, HALF:D_MODEL]
            dgu, dvn = [], []
            for g in range(N_GROUPS):
                cols = slice(g * SGU_GROUP, (g + 1) * SGU_GROUP)
                dgu.append(dd[:, cols] * mixed[g])
                dmx = dd[:, cols] * gu[:, cols]
                dbt_ref[:, g:g + 1] += jnp.sum(dmx, axis=-1, keepdims=True)
                dws_ref[g] += _dot(dmx, vn[:, cols], "nt")
                dvn.append(_dot(ws_ref[g], dmx, "tn"))
            dgu = jnp.concatenate(dgu, axis=1)
            dvn = jnp.concatenate(dvn, axis=1)
            dsn_ref[...] += jnp.sum(dvn * xhat, axis=0, keepdims=True)
            dgx = dvn * norm
            m = jnp.mean(dgx * xhat, axis=-1, keepdims=True)
            dgv = r * (dgx - xhat * m)
            dproj_ref[rows, HALF:2 * HALF] = (dgu * _gelu_grad(uu)).astype(dproj_ref.dtype)
            dproj_ref[rows, 2 * HALF:3 * HALF] = (dgv * _gelu_grad(vv)).astype(dproj_ref.dtype)
            return carry

        lax.fori_loop(0, nchunk, chunk, 0)

    def col(j):
        return pl.BlockSpec((s, HALF), lambda b: (b, j))

    def whole(a):
        return pl.BlockSpec(a.shape, lambda b: (0,) * a.ndim)

    consts = [p["pool_w"], p["pool_scale"].reshape(1, HALF), p["sgu_norm"].reshape(1, HALF),
              p["sgu_w"], p["sgu_b"].T]
    gshapes = [jax.ShapeDtypeStruct(a.shape, F32) for a in consts]
    dproj, dpw, dps, dsn, dws, dbt = pl.pallas_call(
        body, name=tag + "_dcore", grid=(nb,),
        in_specs=[col(0), col(1), col(2)] + [whole(a) for a in consts]
        + [pl.BlockSpec((s, D_MODEL), lambda b: (b, 0))],
        out_specs=[pl.BlockSpec((s, 3 * HALF), lambda b: (b, 0))] + [whole(a) for a in consts],
        out_shape=[jax.ShapeDtypeStruct((t, 3 * HALF), _ACT)] + gshapes,
        scratch_shapes=[pltpu.VMEM((s + 2 * HALO, POOL_GROUP), F32)],
        compiler_params=_params("arbitrary"),
    )(proj, proj, proj, *consts, d_mix)
    return dproj, dict(pool_w=dpw, pool_scale=dps.reshape(HALF), sgu_norm=dsn.reshape(HALF), sgu_w=dws, sgu_b=dbt.T)


def _odd_fwd(tag, x, p, nb, s):
    h = _rmsnorm_fwd(tag + "_norm", x, p["norm"])
    proj = _proj_in(tag, h, p["w_in"])
    mix = _odd_core_fwd(tag, proj, p, nb, s)
    x_out = _proj_out(tag, x, [mix], p["w_out"])
    return x_out, (x, h, proj, mix)


def _odd_bwd(tag, dxo, saved, p, nb, s):
    x, h, proj, mix = saved
    d_mix, dw_out = _proj_out_bwd(tag, dxo, [mix], p["w_out"])
    dproj, grads = _odd_core_bwd(tag, proj, p, d_mix, nb, s)
    dx, dnorm, dw_in = _proj_in_bwd(tag, h, [dproj], p["w_in"], x, dxo, p["norm"])
    grads.update(norm=dnorm, w_in=dw_in, w_out=dw_out)
    return dx, grads


def _local_step(x3, target3, depth, weights_of, final_norm, grads_done):
    nb, s, d = x3.shape
    t = nb * s
    x = x3.reshape(t, d)
    target = target3.reshape(t, d)
    cos, sin = _rope_tables(s)
    saved, ws = [], []
    for l in range(depth):
        w1 = weights_of(l, "ffn1", x)
        x, s1 = _ffn_fwd(f"l{l}_ffn1", x, w1["norm"], w1["w_in4"], w1["w_out"], _ffn_tiles(l, 1))
        wm = weights_of(l, "mix", x)
        if l % 2 == 0:
            x, s2 = _even_fwd(f"l{l}_ev", x, wm, cos, sin, nb, s)
        else:
            x, s2 = _odd_fwd(f"l{l}_od", x, wm, nb, s)
        w2 = weights_of(l, "ffn2", x)
        x, s3 = _ffn_fwd(f"l{l}_ffn2", x, w2["norm"], w2["w_in4"], w2["w_out"], _ffn_tiles(l, 2))
        saved.append((s1, s2, s3))
        ws.append((w1, wm, w2))
    loss, dx, dfinal = _final_loss("final_loss", x, final_norm, target)
    zero = 0.0
    for l in reversed(range(depth)):
        s1, s2, s3 = saved[l]
        w1, wm, w2 = ws[l]
        dx, dn, dwi, dwo = _ffn_bwd(f"l{l}_ffn2", dx, s3, w2["norm"] + zero, w2["w_in4"], w2["w_out"], _ffn_tiles(l, 2))
        zero = grads_done(l, "ffn2", dict(norm=dn, w_in4=dwi, w_out=dwo), dx)
        wm = dict(wm, norm=wm["norm"] + zero)
        if l % 2 == 0:
            dx, gm = _even_bwd(f"l{l}_ev", dx, s2, wm, cos, sin, nb, s)
        else:
            dx, gm = _odd_bwd(f"l{l}_od", dx, s2, wm, nb, s)
        zero = grads_done(l, "mix", gm, dx)
        dx, dn, dwi, dwo = _ffn_bwd(f"l{l}_ffn1", dx, s1, w1["norm"] + zero, w1["w_in4"], w1["w_out"], _ffn_tiles(l, 1))
        zero = grads_done(l, "ffn1", dict(norm=dn, w_in4=dwi, w_out=dwo), dx)
    return loss, dx.reshape(nb, s, d), dfinal


_HBM = pl.BlockSpec(memory_space=pltpu.HBM)


def _place():
    x, y, c = lax.axis_index("x"), lax.axis_index("y"), lax.axis_index("c")
    chips = [(1 - x, y), (x, 1 - y), (1 - x, 1 - y)]
    return x, y, c, chips


def _remote(src, dst, send_sem, recv_sem, to):
    return pltpu.make_async_remote_copy(src_ref=src, dst_ref=dst, send_sem=send_sem, recv_sem=recv_sem,
                                        device_id=to, device_id_type=_MESH)


def _gather_shards(arrs, small):
    n = len(arrs)
    own = 6

    def body(*refs):
        ins, sm_in = refs[:n], refs[n]
        outs, sm_out = refs[n + 1:2 * n + 1], refs[2 * n + 1]
        send, recv = refs[2 * n + 2:]
        x, y, c, chips = _place()
        k = 2 * x + y
        sib = (x, y, 1 - c)
        started = []
        for a in range(n + 1):
            src, dst = (ins[a], outs[a]) if a < n else (sm_in, sm_out)
            cp = _remote(src, dst.at[k], send.at[a, own], recv.at[a, own], sib)
            cp.start()
            started.append(cp)
            if a < n:
                h = src.shape[0] // 2
                mine = pl.ds(c * h, h)
                src_part, dst_part = src.at[mine], dst.at[k, mine]
            else:
                src_part, dst_part = src, dst.at[k]
            for j, chip in enumerate(chips):
                cp = _remote(src_part, dst_part, send.at[a, j], recv.at[a, j], (*chip, c))
                cp.start()
                started.append(cp)
        for a in range(n):
            h = ins[a].shape[0] // 2
            mine = pl.ds(c * h, h)
            for j, (px, py) in enumerate(chips):
                landed = outs[a].at[2 * px + py, mine]
                _remote(landed, landed, send.at[a, j], recv.at[a, j], (px, py, c)).wait_recv()
                cp = _remote(landed, landed, send.at[a, 3 + j], recv.at[a, 3 + j], sib)
                cp.start()
                started.append(cp)
        for a in range(n):
            h = ins[a].shape[0] // 2
            other = pl.ds((1 - c) * h, h)
            for j, (px, py) in enumerate(chips):
                passed = outs[a].at[2 * px + py, other]
                _remote(passed, passed, send.at[a, 3 + j], recv.at[a, 3 + j], sib).wait_recv()
        for j, (px, py) in enumerate(chips):
            landed = sm_out.at[2 * px + py]
            _remote(landed, landed, send.at[n, j], recv.at[n, j], (px, py, c)).wait_recv()
        for a in range(n + 1):
            filled = (outs[a] if a < n else sm_out).at[k]
            _remote(filled, filled, send.at[a, own], recv.at[a, own], sib).wait_recv()
        for cp in started:
            cp.wait_send()

    outs = pl.pallas_call(
        body, name="gather_shards",
        in_specs=[_HBM] * (n + 1), out_specs=[_HBM] * (n + 1),
        out_shape=[jax.ShapeDtypeStruct((N_CHIPS,) + a.shape, a.dtype) for a in list(arrs) + [small]],
        scratch_shapes=[pltpu.SemaphoreType.DMA((n + 1, 7)), pltpu.SemaphoreType.DMA((n + 1, 7))],
    )(*arrs, small)
    return outs[:n], outs[n]


def _swap_halves(name, grads):
    n = len(grads)

    def body(*refs):
        ins, outs = refs[:n], refs[n:2 * n]
        send, recv = refs[2 * n:]
        x, y, c, _ = _place()
        sib = (x, y, 1 - c)
        cps = []
        for a in range(n):
            h = ins[a].shape[1] // 2
            cp = _remote(ins[a].at[:, pl.ds((1 - c) * h, h)], outs[a], send.at[a], recv.at[a], sib)
            cp.start()
            cps.append(cp)
        for cp in cps:
            cp.wait()

    return pl.pallas_call(
        body, name=name,
        in_specs=[_HBM] * n, out_specs=[_HBM] * n,
        out_shape=[jax.ShapeDtypeStruct((g.shape[0], g.shape[1] // 2) + g.shape[2:], g.dtype) for g in grads],
        scratch_shapes=[pltpu.SemaphoreType.DMA((n,)), pltpu.SemaphoreType.DMA((n,))],
    )(*grads)


_SEM = pl.BlockSpec(memory_space=pltpu.SEMAPHORE)
_EFFECT = pltpu.SideEffectType.DATAFLOW_SIDE_EFFECTING


def _gather_plan(src, land, k, c, chips, sib):
    mine = pl.ds(c * (src.shape[0] // 2), src.shape[0] // 2)
    plan = [(src.at[mine], land.at[k, mine], (px, py, c), land.at[2 * px + py, mine]) for px, py in chips]
    return plan + [(src, land.at[k], sib, land.at[k])]


def _scatter_plan(src, land, k, c, chips, sib):
    return [(src.at[2 * px + py], land.at[k], (px, py, c), land.at[2 * px + py]) for px, py in chips]


def _split_start(name, plan, ncopy, srcs, after):
    n = len(srcs)
    lands = [pltpu.with_memory_space_constraint(lax.empty((N_CHIPS,) + a.shape[-2:], a.dtype), pltpu.HBM) for a in srcs]

    def body(*refs):
        src_refs, land_refs = refs[1:1 + n], refs[1 + n:1 + 2 * n]
        send, recv, token = refs[1 + 2 * n], refs[2 + 2 * n], refs[-1]
        x, y, c, chips = _place()
        for i in range(n):
            for j, (src, dst, peer, _) in enumerate(plan(src_refs[i], land_refs[i], 2 * x + y, c, chips, (x, y, 1 - c))):
                _remote(src, dst, send.at[i * ncopy + j], recv.at[i * ncopy + j], peer).start()
        token[...] = jnp.zeros_like(token)

    outs = pl.pallas_call(
        body, name=name,
        in_specs=[_ANY] + [_HBM] * (2 * n),
        out_specs=[_SEM, _SEM] + [_HBM] * (2 * n) + [_VMEM],
        out_shape=[pltpu.SemaphoreType.DMA((n * ncopy,)), pltpu.SemaphoreType.DMA((n * ncopy,))]
        + [pltpu.HBM(a.shape, a.dtype) for a in list(srcs) + lands] + [jax.ShapeDtypeStruct((8, LANES), F32)],
        input_output_aliases={1 + i: 2 + i for i in range(2 * n)},
        compiler_params=pltpu.CompilerParams(has_side_effects=_EFFECT),
    )(after, *[pltpu.with_memory_space_constraint(a, pltpu.HBM) for a in srcs], *lands)
    return outs[0], outs[1], outs[2:2 + n], outs[2 + n:2 + 2 * n], outs[-1]


def _split_wait(name, plan, started, after):
    send, recv, srcs, lands = started
    n = len(srcs)
    ncopy = send.shape[0] // n

    def body(*refs):
        src_refs, land_refs = refs[:n], refs[n:2 * n]
        send, recv = refs[2 * n], refs[2 * n + 1]
        x, y, c, chips = _place()
        for i in range(n):
            for j, (src, _, peer, landed) in enumerate(plan(src_refs[i], land_refs[i], 2 * x + y, c, chips, (x, y, 1 - c))):
                cp = _remote(src, landed, send.at[i * ncopy + j], recv.at[i * ncopy + j], peer)
                cp.wait_send()
                cp.wait_recv()

    outs = pl.pallas_call(
        body, name=name,
        in_specs=[_HBM] * (2 * n) + [_SEM, _SEM, _ANY],
        out_specs=[_HBM] * (2 * n),
        out_shape=[pltpu.HBM(a.shape, a.dtype) for a in list(srcs) + list(lands)],
        input_output_aliases={i: i for i in range(2 * n)},
        compiler_params=pltpu.CompilerParams(has_side_effects=_EFFECT),
    )(*srcs, *lands, send, recv, after)
    return outs[:n], outs[n:]


def _pass_halves(name, lands):
    n = len(lands)

    def body(*refs):
        ins, outs = refs[:n], refs[n:2 * n]
        send, recv = refs[2 * n:]
        x, y, c, chips = _place()
        sib = (x, y, 1 - c)
        cps = []
        for a in range(n):
            h = ins[a].shape[1] // 2
            for j, (px, py) in enumerate(chips):
                rows = (2 * px + py, pl.ds(c * h, h))
                cp = _remote(ins[a].at[rows], outs[a].at[rows], send.at[a, j], recv.at[a, j], sib)
                cp.start()
                cps.append(cp)
        for a in range(n):
            h = ins[a].shape[1] // 2
            for j, (px, py) in enumerate(chips):
                theirs = outs[a].at[2 * px + py, pl.ds((1 - c) * h, h)]
                _remote(theirs, theirs, send.at[a, j], recv.at[a, j], sib).wait_recv()
        for cp in cps:
            cp.wait_send()

    return pl.pallas_call(
        body, name=name,
        in_specs=[_HBM] * n, out_specs=[_HBM] * n,
        out_shape=[jax.ShapeDtypeStruct(p.shape, p.dtype) for p in lands],
        input_output_aliases={a: a for a in range(n)},
        scratch_shapes=[pltpu.SemaphoreType.DMA((n, 3)), pltpu.SemaphoreType.DMA((n, 3))],
    )(*lands)


def _join_halves(name, bufs, layers):
    n = len(bufs)

    def body(*refs):
        ins, outs = refs[:n], refs[n:2 * n]
        send, recv = refs[2 * n:]
        x, y, c, _ = _place()
        sib = (x, y, 1 - c)
        cps = []
        for a in range(n):
            h = ins[a].shape[1] // 2
            mine = pl.ds(c * h, h)
            cp = _remote(ins[a].at[layers[a], mine], outs[a].at[layers[a], mine], send.at[a], recv.at[a], sib)
            cp.start()
            cps.append(cp)
        for a in range(n):
            h = ins[a].shape[1] // 2
            theirs = outs[a].at[layers[a], pl.ds((1 - c) * h, h)]
            _remote(theirs, theirs, send.at[a], recv.at[a], sib).wait_recv()
        for cp in cps:
            cp.wait_send()

    return pl.pallas_call(
        body, name=name,
        in_specs=[_HBM] * n, out_specs=[_HBM] * n,
        out_shape=[jax.ShapeDtypeStruct(p.shape, p.dtype) for p in bufs],
        input_output_aliases={a: a for a in range(n)},
        scratch_shapes=[pltpu.SemaphoreType.DMA((n,)), pltpu.SemaphoreType.DMA((n,))],
    )(*bufs)


def _allreduce_small(buf, after):
    rows = buf.shape[0]

    def body(in_ref, after_ref, out_ref, land_ref, send, recv):
        x, y, c, _ = _place()
        me = 4 * x + 2 * y + c
        land_ref[me] = in_ref[...]
        peers = []
        for r in range(1, N_DEV):
            peers.append((1 - x if r & 4 else x, 1 - y if r & 2 else y, 1 - c if r & 1 else c))
        cps = []
        for r, peer in enumerate(peers):
            cp = _remote(in_ref, land_ref.at[me], send.at[r], recv.at[r], peer)
            cp.start()
            cps.append(cp)
        for r, (px, py, pc) in enumerate(peers):
            landed = land_ref.at[4 * px + 2 * py + pc]
            _remote(landed, landed, send.at[r], recv.at[r], (px, py, pc)).wait_recv()
        for cp in cps:
            cp.wait_send()
        acc = land_ref[0]
        for d in range(1, N_DEV):
            acc = acc + land_ref[d]
        out_ref[...] = acc

    return pl.pallas_call(
        body, name="allreduce_small",
        in_specs=[_VMEM, _ANY], out_specs=_VMEM,
        out_shape=jax.ShapeDtypeStruct(buf.shape, F32),
        scratch_shapes=[pltpu.VMEM((N_DEV, rows, LANES), F32), pltpu.SemaphoreType.DMA((N_DEV - 1,)),
                        pltpu.SemaphoreType.DMA((N_DEV - 1,))],
        compiler_params=pltpu.CompilerParams(vmem_limit_bytes=_VMEM_LIMIT),
    )(buf, after)


def _div_tile(n, cap, mult):
    best = None
    for d in range(mult, min(n, cap) + 1, mult):
        if n % d == 0:
            best = d
    return best if best is not None else n


def _add_sibling(name, grad, got, c):
    nk, hr, cc = got.shape
    tr = _div_tile(hr, 512, 16)
    nt = hr // tr

    def body(c_ref, g_ref, o_ref, s_ref):
        s_ref[...] = (g_ref[...].astype(F32) + o_ref[...].astype(F32)).astype(s_ref.dtype)

    blk = (None, tr, cc)
    return pl.pallas_call(
        body, name=name,
        grid_spec=pltpu.PrefetchScalarGridSpec(
            num_scalar_prefetch=1, grid=(nk, nt),
            in_specs=[pl.BlockSpec(blk, lambda i, q, c_ref: (i, c_ref[0] * nt + q, 0)),
                      pl.BlockSpec(blk, lambda i, q, c_ref: (i, q, 0))],
            out_specs=pl.BlockSpec(blk, lambda i, q, c_ref: (i, q, 0))),
        out_shape=jax.ShapeDtypeStruct(got.shape, got.dtype),
        compiler_params=_params("parallel", "parallel"),
    )(c, grad, got)


def _add_chips(name, mine, got, place, buf, l):
    nk, hr, cc = got.shape
    tr = _div_tile(hr, 512, 16)
    nt = hr // tr

    def body(*refs):
        acc = refs[1][...].astype(F32)
        for q in range(1, nk):
            acc = acc + refs[1 + q][...].astype(F32)
        refs[2 + nk][...] = acc

    def part(q):
        return pl.BlockSpec((None, tr, cc), lambda i, p_ref: ((p_ref[0] + q) % nk, i, 0))

    return pl.pallas_call(
        body, name=name,
        grid_spec=pltpu.PrefetchScalarGridSpec(
            num_scalar_prefetch=1, grid=(nt,),
            in_specs=[part(q) for q in range(nk)] + [_ANY],
            out_specs=pl.BlockSpec((None, tr, cc), lambda i, p_ref: (l, p_ref[1] * nt + i, 0))),
        out_shape=jax.ShapeDtypeStruct(buf.shape, F32),
        input_output_aliases={1 + nk: 0},
        compiler_params=_params("parallel"),
    )(place, mine, *([got] * (nk - 1)), buf)


def _adamw(name, w, g, m, v, after=None):
    shape = w.shape
    cols = shape[-1]
    rows = w.size // cols
    tr = rows if rows * cols <= 2 ** 18 else _div_tile(rows, max(8, 2 ** 18 // cols), 8)
    c1 = 1.0 - ADAM_B1 ** ADAM_STEP
    c2 = 1.0 - ADAM_B2 ** ADAM_STEP
    extra = [] if after is None else [after]

    def body(*refs):
        w_ref, g_ref, m_ref, v_ref = refs[:4]
        d_ref, mo_ref, vo_ref, go_ref = refs[4 + len(extra):]
        gg = g_ref[...]
        mn = ADAM_B1 * m_ref[...] + (1.0 - ADAM_B1) * gg
        vn = ADAM_B2 * v_ref[...] + (1.0 - ADAM_B2) * (gg * gg)
        d_ref[...] = -ADAM_LR * ((mn / c1) / (jnp.sqrt(vn / c2) + ADAM_EPS) + ADAM_WD * w_ref[...])
        mo_ref[...] = mn
        vo_ref[...] = vn
        go_ref[...] = gg

    blk = pl.BlockSpec((tr, cols), lambda i: (i, 0))
    sds = jax.ShapeDtypeStruct((rows, cols), F32)
    outs = pl.pallas_call(
        body, name=name, grid=(rows // tr,),
        in_specs=[blk] * 4 + [_ANY] * len(extra), out_specs=(blk,) * 4, out_shape=(sds,) * 4,
        compiler_params=_params("parallel"),
    )(*[a.reshape(rows, cols) for a in (w, g, m, v)], *extra)
    return [o.reshape(shape) for o in outs]


_WEIGHTS = ["ffn1_norm", "ffn1_w_in", "ffn1_w_out", "mix_norm", "ffn2_norm", "ffn2_w_in", "ffn2_w_out",
            "ev_w_in", "ev_conv_w", "ev_q_norm", "ev_k_norm", "ev_w_out", "od_w_in", "od_pool_w",
            "od_pool_scale", "od_sgu_norm", "od_sgu_w", "od_sgu_b", "od_w_out", "final_norm"]
_BIG = ["ffn1_w_in", "ffn1_w_out", "ffn2_w_in", "ffn2_w_out", "ev_w_in", "ev_w_out", "od_w_in", "od_w_out"]
_SMALL_SHARDED = ["ev_conv_w", "od_pool_scale", "od_sgu_norm"]


def _pad_rows(a, mult=8):
    pad = (-a.shape[0]) % mult
    return a if pad == 0 else jnp.concatenate([a, jnp.zeros((pad,) + a.shape[1:], a.dtype)], axis=0)


def _join_cols(g):
    return g.transpose(1, 0, 2).reshape(g.shape[1], N_CHIPS * g.shape[2])


def _split_cols(w):
    return w.reshape(w.shape[0], N_CHIPS, w.shape[1] // N_CHIPS).transpose(1, 0, 2)


def kernel(x, ffn1_norm, ffn1_w_in, ffn1_w_out, mix_norm, ffn2_norm, ffn2_w_in, ffn2_w_out, ev_w_in, ev_conv_w,
           ev_q_norm, ev_k_norm, ev_w_out, od_w_in, od_pool_w, od_pool_scale, od_sgu_norm, od_sgu_w, od_sgu_b,
           od_w_out, final_norm, loss_target, m_ffn1_norm, m_ffn1_w_in, m_ffn1_w_out, m_mix_norm, m_ffn2_norm,
           m_ffn2_w_in, m_ffn2_w_out, m_ev_w_in, m_ev_conv_w, m_ev_q_norm, m_ev_k_norm, m_ev_w_out, m_od_w_in,
           m_od_pool_w, m_od_pool_scale, m_od_sgu_norm, m_od_sgu_w, m_od_sgu_b, m_od_w_out, m_final_norm, v_ffn1_norm,
           v_ffn1_w_in, v_ffn1_w_out, v_mix_norm, v_ffn2_norm, v_ffn2_w_in, v_ffn2_w_out, v_ev_w_in, v_ev_conv_w,
           v_ev_q_norm, v_ev_k_norm, v_ev_w_out, v_od_w_in, v_od_pool_w, v_od_pool_scale, v_od_sgu_norm, v_od_sgu_w,
           v_od_sgu_b, v_od_w_out, v_final_norm):
    return _step(x, ffn1_norm, ffn1_w_in, ffn1_w_out, mix_norm, ffn2_norm, ffn2_w_in, ffn2_w_out, ev_w_in, ev_conv_w,
                 ev_q_norm, ev_k_norm, ev_w_out, od_w_in, od_pool_w, od_pool_scale, od_sgu_norm, od_sgu_w, od_sgu_b,
                 od_w_out, final_norm, loss_target, m_ffn1_norm, m_ffn1_w_in, m_ffn1_w_out, m_mix_norm, m_ffn2_norm,
                 m_ffn2_w_in, m_ffn2_w_out, m_ev_w_in, m_ev_conv_w, m_ev_q_norm, m_ev_k_norm, m_ev_w_out, m_od_w_in,
                 m_od_pool_w, m_od_pool_scale, m_od_sgu_norm, m_od_sgu_w, m_od_sgu_b, m_od_w_out, m_final_norm,
                 v_ffn1_norm, v_ffn1_w_in, v_ffn1_w_out, v_mix_norm, v_ffn2_norm, v_ffn2_w_in, v_ffn2_w_out,
                 v_ev_w_in, v_ev_conv_w, v_ev_q_norm, v_ev_k_norm, v_ev_w_out, v_od_w_in, v_od_pool_w,
                 v_od_pool_scale, v_od_sgu_norm, v_od_sgu_w, v_od_sgu_b, v_od_w_out, v_final_norm)


def _step(*args):
    nw = len(_WEIGHTS)
    x = args[0]
    w = dict(zip(_WEIGHTS, args[1:1 + nw]))
    target = args[1 + nw]
    m = dict(zip(_WEIGHTS, args[2 + nw:2 + 2 * nw]))
    v = dict(zip(_WEIGHTS, args[2 + 2 * nw:2 + 3 * nw]))
    depth = w["ffn1_norm"].shape[0]
    n_even, n_odd = w["ev_w_in"].shape[0], w["od_w_in"].shape[0]
    chip = 2 * lax.axis_index("x") + lax.axis_index("y")
    place = jnp.stack([chip, lax.axis_index("c")]).astype(jnp.int32)
    core = place[1:2]

    def sharded(l, block):
        if block == "mix":
            block = "ev" if l % 2 == 0 else "od"
            return [(block + "_w_in", l // 2), (block + "_w_out", l // 2)]
        return [(block + "_w_in", l), (block + "_w_out", l)]

    def shards(group):
        return [w[n][i].astype(_ACT) for l, block in group for n, i in sharded(l, block)]

    small_rows = [w["ev_conv_w"].reshape(3 * n_even, LANES), w["od_pool_scale"], w["od_sgu_norm"]]
    first, small = _gather_shards(shards([(0, "ffn1")]), _pad_rows(jnp.concatenate(small_rows, axis=0)))
    conv_w = small[:, :3 * n_even].reshape(N_CHIPS, n_even, 3, LANES).transpose(1, 2, 0, 3).reshape(n_even, 3, CONV_WIDTH)
    pool_scale = small[:, 3 * n_even:3 * n_even + n_odd].transpose(1, 0, 2).reshape(n_odd, HALF)
    sgu_norm = small[:, 3 * n_even + n_odd:3 * n_even + 2 * n_odd].transpose(1, 0, 2).reshape(n_odd, HALF)
    later = [[(0, "mix"), (0, "ffn2")]] + [[(l, "ffn1"), (l, "mix"), (l, "ffn2")] for l in range(1, depth)]
    gathering, after = [], small
    for i, group in enumerate(later):
        gathering.append(_split_start(f"gather_start{i}", _gather_plan, N_CHIPS, shards(group), after))
        after = gathering[-1][4]
    gathered = {(0, "ffn1"): first}

    def rows(g):
        return g.reshape(N_CHIPS * g.shape[1], g.shape[2])

    def weights_of(l, block, x_in):
        zero = after[0, 0] if (l, block) == (0, "ffn1") else 0.0
        if (l, block) not in gathered:
            i = next(i for i, group in enumerate(later) if (l, block) in group)
            got = _pass_halves(f"pass_halves{i}", _split_wait(f"gather_wait{i}", _gather_plan, gathering[i][:4], x_in)[1])
            for n, key in enumerate(later[i]):
                gathered[key] = got[2 * n:2 * n + 2]
        w_in, w_out = gathered[(l, block)]
        if block != "mix":
            return dict(norm=w[block + "_norm"][l] + zero, w_in4=w_in, w_out=rows(w_out))
        j = l // 2
        if l % 2 == 0:
            mix = dict(conv_w=conv_w[j], q_gain=w["ev_q_norm"][j], k_gain=w["ev_k_norm"][j])
        else:
            mix = dict(pool_w=w["od_pool_w"][j], pool_scale=pool_scale[j], sgu_norm=sgu_norm[j],
                       sgu_w=w["od_sgu_w"][j], sgu_b=w["od_sgu_b"][j])
        return dict(mix, norm=w["mix_norm"][l], w_in=_join_cols(w_in), w_out=rows(w_out))

    def by_chip(dw):
        return dw.reshape(N_CHIPS, dw.shape[0] // N_CHIPS, dw.shape[1])

    bufs = {n: lax.empty(w[n].shape, F32) for n in _BIG}
    small_grads = {n: [None] * w[n].shape[0] for n in _WEIGHTS if n not in _BIG and n != "final_norm"}
    scattering, group = [], []

    def finish_scatter(after):
        tag, names, started = scattering.pop()
        halves, got = _split_wait(f"scatter_wait{tag}", _scatter_plan, started[:4], after)
        for i, (n, j) in enumerate(names):
            bufs[n] = _add_chips(f"add_chips{tag}_{n}", halves[i], got[i], place, bufs[n], j)
        joined = _join_halves(f"join_halves{tag}", [bufs[n] for n, _ in names], [j for _, j in names])
        for (n, _), b in zip(names, joined):
            bufs[n] = b

    def grads_done(l, block, g, dx):
        j = l // 2
        if block == "mix":
            local = [_split_cols(g["w_in"]), by_chip(g["w_out"])]
            small_grads["mix_norm"][l] = g["norm"]
            renamed = (dict(conv_w="ev_conv_w", q_gain="ev_q_norm", k_gain="ev_k_norm") if l % 2 == 0 else
                       dict(pool_w="od_pool_w", pool_scale="od_pool_scale", sgu_norm="od_sgu_norm", sgu_w="od_sgu_w",
                            sgu_b="od_sgu_b"))
            for key, n in renamed.items():
                small_grads[n][j] = g[key]
        else:
            local = [g["w_in4"], by_chip(g["w_out"])]
            small_grads[block + "_norm"][l] = g["norm"]
        group.extend(zip(sharded(l, block), local))
        if block == "ffn2" or (block == "mix" and l > 0):
            return 0.0
        if scattering:
            finish_scatter(dx)
        tag = f"{l}_{block}"
        names, local = [n for n, _ in group], [a for _, a in group]
        group.clear()
        from_sibling = _swap_halves(f"swap_halves{tag}", local)
        halves = [_add_sibling(f"add_sibling{tag}_{n}", a, b, core) for (n, _), a, b in zip(names, local, from_sibling)]
        scattering.append((tag, names, _split_start(f"scatter_start{tag}", _scatter_plan, N_CHIPS - 1, halves, dx)))
        return scattering[-1][2][4][0, 0]

    loss_part, grad_x, dfinal = _local_step(x, target, depth, weights_of, w["final_norm"], grads_done)
    loss = lax.psum(loss_part, ("x", "y", "c"))

    updates = {}
    behind = scattering[-1][2][4]
    for n in _BIG:
        if not n.startswith("ffn1"):
            updates[n] = _adamw("adamw_" + n, w[n], bufs[n], m[n], v[n], behind)
            behind = updates[n][1]
    small_grads = {n: jnp.stack(parts) for n, parts in small_grads.items()}
    small_grads["final_norm"] = dfinal
    names = list(small_grads)
    flat = jnp.concatenate([small_grads[n].reshape(-1) for n in names])
    total = flat.shape[0]
    flat = jnp.concatenate([flat, jnp.zeros((-total) % (8 * LANES), F32)])
    summed = _allreduce_small(flat.reshape(-1, LANES), behind).reshape(-1)
    finish_scatter(summed)
    grads = dict(bufs)
    off = 0
    for n in names:
        size = small_grads[n].size
        full_grad = summed[off:off + size].reshape(small_grads[n].shape)
        off += size
        if n in _SMALL_SHARDED:
            full_grad = lax.dynamic_slice_in_dim(full_grad, chip * LANES, LANES, axis=full_grad.ndim - 1)
        grads[n] = full_grad
    for n in _WEIGHTS:
        if n not in updates:
            updates[n] = _adamw("adamw_" + n, w[n], grads[n], m[n], v[n])
    return (loss, grad_x, *[updates[n][3] for n in _WEIGHTS], *[updates[n][0] for n in _WEIGHTS],
            *[updates[n][1] for n in _WEIGHTS], *[updates[n][2] for n in _WEIGHTS])
```

```python
import jax
import jax.numpy as jnp
from jax import lax
from jax.experimental import pallas as pl
from jax.experimental.pallas import tpu as pltpu

F32 = jnp.float32
_MXU = jnp.bfloat16
_ACT = jnp.bfloat16

D_MODEL = 1024
GRID_W = 64
HEAD_DIM = 64
N_Q_HEADS = 8
N_KV_HEADS = 2
Q_PER_KV = N_Q_HEADS // N_KV_HEADS
ATTN_WIDTH = N_Q_HEADS * HEAD_DIM
KV_WIDTH = N_KV_HEADS * HEAD_DIM
ROPE_THETA = 10000.0
CONV_WIDTH = D_MODEL // 2
POOL_RADII = (1, 2, 4, 8)
POOL_GROUP = 128
SGU_GROUP = 128
SGU_CHUNK = 128
N_GROUPS = 4
HALF = D_MODEL // 2
EPS = 1e-6
HALO = 8
LANES = 128
N_CHIPS = 4
N_DEV = 8

ADAM_LR = 0.001
ADAM_B1 = 0.9
ADAM_B2 = 0.999
ADAM_EPS = 1e-08
ADAM_WD = 0.01
ADAM_STEP = 10

_VMEM_LIMIT = 56 * 2 ** 20
_MESH = pl.DeviceIdType.MESH
_ANY = pl.BlockSpec(memory_space=pl.ANY)
_VMEM = pl.BlockSpec(memory_space=pltpu.VMEM)

_DN = {
    "nn": (((1,), (0,)), ((), ())),
    "nt": (((1,), (1,)), ((), ())),
    "tn": (((0,), (0,)), ((), ())),
}


def _params(*sem):
    return pltpu.CompilerParams(dimension_semantics=sem, vmem_limit_bytes=_VMEM_LIMIT)


def _tile(n, cap):
    best = None
    d = LANES
    while d <= min(n, cap):
        if n % d == 0:
            best = d
        d += LANES
    return best if best is not None else n


def _dot(a, b, mode="nn"):
    return lax.dot_general(a.astype(_MXU), b.astype(_MXU), _DN[mode], preferred_element_type=F32)


def _cat(*vals):
    vals = [v.astype(_MXU) for v in vals]
    return vals[0] if len(vals) == 1 else jnp.concatenate(vals, axis=1)


def _sigmoid(g):
    return 1.0 / (1.0 + jnp.exp(-g))


def _norm_rows(x, g):
    r = lax.rsqrt(jnp.mean(x * x, axis=-1, keepdims=True) + EPS)
    return (x * r) * g


def _swiglu(g, u):
    return (g * _sigmoid(g)) * u


_GELU_C = 0.7978845608028654


def _gelu(x):
    return 0.5 * x * (1.0 + jnp.tanh(_GELU_C * (x + 0.044715 * (x * x * x))))


def _gelu_grad(x):
    t = jnp.tanh(_GELU_C * (x + 0.044715 * (x * x * x)))
    return 0.5 * (1.0 + t) + 0.5 * x * (1.0 - t * t) * (_GELU_C * (1.0 + 3.0 * 0.044715 * (x * x)))


def _mm(name, grid, mode, a_ops, b_ops, e_ops, out_shape, out_specs, acc_shape, a_fn=_cat, b_fn=_cat, epi=None,
        n_outer=False, m_carried=False, b_pick=None):
    ni, nj, nk = grid
    na, nb, ne = len(a_ops), len(b_ops), len(e_ops)
    multi = isinstance(out_shape, (list, tuple))
    no = len(out_shape) if multi else 1

    def body(*refs):
        a_refs = refs[:na]
        b_refs = refs[na:na + nb]
        e_refs = refs[na + nb:na + nb + ne]
        o_refs = refs[na + nb + ne:na + nb + ne + no]
        a = a_fn(*[r[...] for r in a_refs])
        if b_pick is None:
            b = b_fn(*[r[...] for r in b_refs])
        else:
            b = b_pick(b_refs, pl.program_id(1), pl.program_id(2))
        p = _dot(a, b, mode)

        def finish(acc):
            if epi is None:
                o_refs[0][...] = acc.astype(o_refs[0].dtype)
            else:
                epi(acc, [r[...] for r in e_refs], o_refs)

        if nk == 1:
            finish(p)
        else:
            acc_ref = refs[-1]
            k = pl.program_id(2)

            @pl.when(k == 0)
            def _():
                acc_ref[...] = p

            @pl.when((k > 0) & (k < nk - 1))
            def _():
                acc_ref[...] += p

            @pl.when(k == nk - 1)
            def _():
                finish(acc_ref[...] + p)

    ops = list(a_ops) + list(b_ops) + list(e_ops)
    if n_outer:
        def flip(spec):
            return pl.BlockSpec(spec.block_shape, lambda j, i, k, f=spec.index_map: f(i, j, k))

        grid = (nj, ni, nk)
        ops = [(a, flip(s)) for a, s in ops]
        out_specs = [flip(s) for s in out_specs] if multi else flip(out_specs)
    return pl.pallas_call(
        body, name=name, grid=grid,
        in_specs=[s for _, s in ops],
        out_specs=out_specs, out_shape=out_shape,
        scratch_shapes=[pltpu.VMEM(acc_shape, F32)] if nk > 1 else [],
        compiler_params=_params(*(("arbitrary",) * 3 if m_carried else ("parallel", "parallel", "arbitrary"))),
    )(*[a for a, _ in ops])


def _whole(a):
    return pl.BlockSpec(a.shape, lambda i, j, k: (0,) * a.ndim, pipeline_mode=pl.Buffered(1))


def _norm_bwd_epi(acc, e, o):
    xf, dres, g = e
    r = lax.rsqrt(jnp.mean(xf * xf, axis=-1, keepdims=True) + EPS)
    xhat = xf * r
    dgx = acc * g
    m = jnp.mean(dgx * xhat, axis=-1, keepdims=True)
    o[0][...] = dres + r * (dgx - xhat * m)
    part = jnp.sum(acc * xhat, axis=0, keepdims=True)
    i = pl.program_id(0)

    @pl.when(i == 0)
    def _():
        o[1][...] = part

    @pl.when(i > 0)
    def _():
        o[1][...] += part


def _norm_bwd_ops(x, dres, gain, tm):
    t, d = x.shape
    row = pl.BlockSpec((tm, d), lambda i, j, k: (i, 0))
    vec = pl.BlockSpec((1, d), lambda i, j, k: (0, 0))
    return ([(x, row), (dres, row), (gain.reshape(1, d), vec)],
            [jax.ShapeDtypeStruct((t, d), F32), jax.ShapeDtypeStruct((1, d), F32)], [row, vec])


def _rows(t):
    return _tile(t, 512)


def _rmsnorm_fwd(name, x, gain):
    t, d = x.shape
    tr = _rows(t)

    def body(x_ref, g_ref, h_ref):
        xf = x_ref[...]
        r = lax.rsqrt(jnp.mean(xf * xf, axis=-1, keepdims=True) + EPS)
        h_ref[...] = ((xf * r) * g_ref[...]).astype(h_ref.dtype)

    return pl.pallas_call(
        body, name=name, grid=(t // tr,),
        in_specs=[pl.BlockSpec((tr, d), lambda i: (i, 0)), pl.BlockSpec((1, d), lambda i: (0, 0))],
        out_specs=pl.BlockSpec((tr, d), lambda i: (i, 0)),
        out_shape=jax.ShapeDtypeStruct((t, d), _ACT),
        compiler_params=_params("parallel"),
    )(x, gain.reshape(1, d))


def _rmsnorm_bwd(name, dh, x, gain, dres):
    t, d = x.shape
    tr = _rows(t)

    def body(dh_ref, x_ref, g_ref, dres_ref, dx_ref, dg_ref):
        i = pl.program_id(0)
        xf = x_ref[...]
        r = lax.rsqrt(jnp.mean(xf * xf, axis=-1, keepdims=True) + EPS)
        xhat = xf * r
        dy = dh_ref[...].astype(F32)
        dgx = dy * g_ref[...]
        m = jnp.mean(dgx * xhat, axis=-1, keepdims=True)
        dx_ref[...] = dres_ref[...] + r * (dgx - xhat * m)
        part = jnp.sum(dy * xhat, axis=0, keepdims=True)

        @pl.when(i == 0)
        def _():
            dg_ref[...] = part

        @pl.when(i > 0)
        def _():
            dg_ref[...] += part

    row = pl.BlockSpec((tr, d), lambda i: (i, 0))
    vec = pl.BlockSpec((1, d), lambda i: (0, 0))
    dx, dg = pl.pallas_call(
        body, name=name, grid=(t // tr,),
        in_specs=[row, row, vec, row],
        out_specs=(row, vec),
        out_shape=(jax.ShapeDtypeStruct((t, d), F32), jax.ShapeDtypeStruct((1, d), F32)),
        compiler_params=_params("arbitrary"),
    )(dh, x, gain.reshape(1, d), dres)
    return dx, dg.reshape(d)


def _final_loss(name, x, gain, target):
    t, d = x.shape
    tr = _rows(t)

    def body(x_ref, g_ref, t_ref, dx_ref, dg_ref, loss_ref):
        i = pl.program_id(0)
        xf = x_ref[...]
        r = lax.rsqrt(jnp.mean(xf * xf, axis=-1, keepdims=True) + EPS)
        xhat = xf * r
        g = g_ref[...]
        err = xhat * g - t_ref[...]
        lpart = 0.5 * jnp.sum(jnp.mean(err * err, axis=-1, keepdims=True), axis=0, keepdims=True)
        dy = err * (1.0 / d)
        dgx = dy * g
        m = jnp.mean(dgx * xhat, axis=-1, keepdims=True)
        dx_ref[...] = r * (dgx - xhat * m)
        part = jnp.sum(dy * xhat, axis=0, keepdims=True)
        lrow = jnp.broadcast_to(lpart, (1, LANES))

        @pl.when(i == 0)
        def _():
            dg_ref[...] = part
            loss_ref[...] = lrow

        @pl.when(i > 0)
        def _():
            dg_ref[...] += part
            loss_ref[...] += lrow

    row = pl.BlockSpec((tr, d), lambda i: (i, 0))
    vec = pl.BlockSpec((1, d), lambda i: (0, 0))
    dx, dg, loss = pl.pallas_call(
        body, name=name, grid=(t // tr,),
        in_specs=[row, vec, row],
        out_specs=(row, vec, pl.BlockSpec((1, LANES), lambda i: (0, 0))),
        out_shape=(jax.ShapeDtypeStruct((t, d), F32), jax.ShapeDtypeStruct((1, d), F32),
                   jax.ShapeDtypeStruct((1, LANES), F32)),
        compiler_params=_params("arbitrary"),
    )(x, gain.reshape(1, d), target)
    return loss[0, 0], dx, dg.reshape(d)


_FFN_TILES = dict(in_tm=1024, in_n_outer=False, out_tm=512, dact_tm=512, dwout_tk=1024, dh_tm=512, dwin_tk=2048)


def _ffn_tiles(layer, which):
    return _FFN_TILES


def _ffn_fwd(tag, x, gain, w_in4, w_out, cfg):
    t, d = x.shape
    fs = w_in4.shape[2]
    f = 2 * fs
    tm = _tile(t, cfg["in_tm"])
    gain = gain.reshape(1, d)
    gu = _mm(
        tag + "_in", (t // tm, N_CHIPS, 1), "nn",
        [(x, pl.BlockSpec((tm, d), lambda i, j, k: (i, 0))), (gain, pl.BlockSpec((1, d), lambda i, j, k: (0, 0)))],
        [(w_in4, _whole(w_in4))], [],
        jax.ShapeDtypeStruct((2, t, f), _ACT),
        pl.BlockSpec((None, tm, fs), lambda i, j, k: (j // 2, i, j % 2)), None, a_fn=_norm_rows,
        b_pick=lambda b, j, k: b[0][j])
    tm2 = _tile(t, cfg["out_tm"])

    def epi(acc, e, o):
        o[0][...] = e[0] + 0.5 * acc

    x_out = _mm(
        tag + "_out", (t // tm2, 1, 1), "nn",
        [(gu, pl.BlockSpec((None, tm2, f), lambda i, j, k: (0, i, 0))),
         (gu, pl.BlockSpec((None, tm2, f), lambda i, j, k: (1, i, 0)))],
        [(w_out, pl.BlockSpec((f, d), lambda i, j, k: (0, 0)))],
        [(x, pl.BlockSpec((tm2, d), lambda i, j, k: (i, 0)))],
        jax.ShapeDtypeStruct((t, d), F32),
        pl.BlockSpec((tm2, d), lambda i, j, k: (i, 0)), None,
        a_fn=_swiglu, epi=epi)
    return x_out, (x, gu)


def _ffn_bwd(tag, dxo, saved, gain, w_in4, w_out, cfg):
    x, gu = saved
    t, d = x.shape
    fs = w_in4.shape[2]
    f = 2 * fs
    tm = _tile(t, cfg["dact_tm"])
    tk = _tile(t, cfg["dwout_tk"])

    def epi_act(acc, e, o):
        g, u = e
        da = (0.5 * acc).astype(g.dtype)
        sig = _sigmoid(g)
        silu = g * sig
        o[0][0] = (da * u * (sig + silu * (1.0 - sig))).astype(o[0].dtype)
        o[0][1] = (da * silu).astype(o[0].dtype)

    dgu = _mm(
        tag + "_dact", (t // tm, 2, 1), "nt",
        [(dxo, pl.BlockSpec((tm, d), lambda i, j, k: (i, 0)))],
        [(w_out, _whole(w_out))],
        [(gu, pl.BlockSpec((None, tm, fs), lambda i, j, k: (0, i, j))),
         (gu, pl.BlockSpec((None, tm, fs), lambda i, j, k: (1, i, j)))],
        jax.ShapeDtypeStruct((2, t, f), _ACT),
        pl.BlockSpec((2, tm, fs), lambda i, j, k: (0, i, j)), None, epi=epi_act,
        b_pick=lambda b, j, k: b[0][pl.ds(pl.multiple_of(j * fs, LANES), fs), :])

    def epi_half(acc, e, o):
        o[0][...] = (0.5 * acc).astype(o[0].dtype)

    dw_out = _mm(
        tag + "_dwout", (2, 1, t // tk), "tn",
        [(gu, pl.BlockSpec((None, tk, fs), lambda i, j, k: (0, k, i))),
         (gu, pl.BlockSpec((None, tk, fs), lambda i, j, k: (1, k, i)))],
        [(dxo, pl.BlockSpec((tk, d), lambda i, j, k: (k, 0)))], [],
        jax.ShapeDtypeStruct((f, d), _ACT),
        pl.BlockSpec((fs, d), lambda i, j, k: (i, 0)), (fs, d),
        a_fn=_swiglu, epi=epi_half)
    tm = _tile(t, cfg["dh_tm"])
    e_ops, shapes, specs = _norm_bwd_ops(x, dxo, gain, tm)
    dx, dgain = _mm(
        tag + "_dh", (t // tm, 1, 2), "nt",
        [(dgu, pl.BlockSpec((None, tm, f), lambda i, j, k: (k, i, 0)))],
        [(w_in4, _whole(w_in4))], e_ops, shapes, specs, (tm, d), epi=_norm_bwd_epi, m_carried=True,
        b_pick=lambda b, j, k: jnp.concatenate([b[0][2 * k], b[0][2 * k + 1]], axis=1))
    tk = _tile(t, cfg["dwin_tk"])
    dw_in4 = _mm(
        tag + "_dwin", (1, N_CHIPS, t // tk), "tn",
        [(x, pl.BlockSpec((tk, d), lambda i, j, k: (k, 0))),
         (gain.reshape(1, d), pl.BlockSpec((1, d), lambda i, j, k: (0, 0)))],
        [(dgu, pl.BlockSpec((None, tk, fs), lambda i, j, k: (j // 2, k, j % 2)))], [],
        jax.ShapeDtypeStruct((N_CHIPS, d, fs), _ACT),
        pl.BlockSpec((None, d, fs), lambda i, j, k: (j, 0, 0)), (d, fs), a_fn=_norm_rows)
    return dx, dgain.reshape(d), dw_in4, dw_out


_MIX_TILES = dict(tm=1024, dwout_tk=2048, dwin_tk=1024)


def _proj_in(tag, h, w_in):
    t, d = h.shape
    n = w_in.shape[1]
    tm = _tile(t, _MIX_TILES["tm"])
    return _mm(
        tag + "_in", (t // tm, 1, 1), "nn",
        [(h, pl.BlockSpec((tm, d), lambda i, j, k: (i, 0)))],
        [(w_in, pl.BlockSpec((d, n), lambda i, j, k: (0, 0)))], [],
        jax.ShapeDtypeStruct((t, n), _ACT),
        pl.BlockSpec((tm, n), lambda i, j, k: (i, 0)), None)


def _proj_out(tag, x, parts, w_out):
    t, d = x.shape
    tm = _tile(t, _MIX_TILES["tm"])

    def epi(acc, e, o):
        o[0][...] = e[0] + acc

    return _mm(
        tag + "_out", (t // tm, 1, 1), "nn",
        [(p, pl.BlockSpec((tm, p.shape[1]), lambda i, j, k: (i, 0))) for p in parts],
        [(w_out, pl.BlockSpec(w_out.shape, lambda i, j, k: (0, 0)))],
        [(x, pl.BlockSpec((tm, d), lambda i, j, k: (i, 0)))],
        jax.ShapeDtypeStruct((t, d), F32),
        pl.BlockSpec((tm, d), lambda i, j, k: (i, 0)), None, epi=epi)


def _proj_out_bwd(tag, dxo, parts, w_out):
    t, d = dxo.shape
    mix = w_out.shape[0]
    tm = _tile(t, _MIX_TILES["tm"])
    tk = _tile(t, _MIX_TILES["dwout_tk"])
    d_mix = _mm(
        tag + "_dmix", (t // tm, 1, 1), "nt",
        [(dxo, pl.BlockSpec((tm, d), lambda i, j, k: (i, 0)))],
        [(w_out, pl.BlockSpec((mix, d), lambda i, j, k: (0, 0)))], [],
        jax.ShapeDtypeStruct((t, mix), F32),
        pl.BlockSpec((tm, mix), lambda i, j, k: (i, 0)), None)
    dw_out = _mm(
        tag + "_dwout", (1, 1, t // tk), "tn",
        [(p, pl.BlockSpec((tk, p.shape[1]), lambda i, j, k: (k, 0))) for p in parts],
        [(dxo, pl.BlockSpec((tk, d), lambda i, j, k: (k, 0)))], [],
        jax.ShapeDtypeStruct((mix, d), _ACT),
        pl.BlockSpec((mix, d), lambda i, j, k: (0, 0)), (mix, d))
    return d_mix, dw_out


def _proj_in_bwd(tag, h, dparts, w_in, x, dres, gain):
    t, d = h.shape
    n = w_in.shape[1]
    tm = _tile(t, _MIX_TILES["tm"])
    tk = _tile(t, _MIX_TILES["dwin_tk"])
    e_ops, shapes, specs = _norm_bwd_ops(x, dres, gain, tm)
    dx, dgain = _mm(
        tag + "_dh", (t // tm, 1, 1), "nt",
        [(p, pl.BlockSpec((tm, p.shape[1]), lambda i, j, k: (i, 0))) for p in dparts],
        [(w_in, pl.BlockSpec((d, n), lambda i, j, k: (0, 0)))], e_ops, shapes, specs, None,
        epi=_norm_bwd_epi, m_carried=True)
    dw_in = _mm(
        tag + "_dwin", (1, 1, t // tk), "tn",
        [(h, pl.BlockSpec((tk, d), lambda i, j, k: (k, 0)))],
        [(p, pl.BlockSpec((tk, p.shape[1]), lambda i, j, k: (k, 0))) for p in dparts], [],
        jax.ShapeDtypeStruct((d, n), _ACT),
        pl.BlockSpec((d, n), lambda i, j, k: (0, 0)), (d, n))
    return dx, dgain.reshape(d), dw_in


def _shifted(pad_ref, val, s):
    pad_ref[pl.ds(HALO, s), :] = val
    return pad_ref[pl.ds(HALO - 1, s), :], pad_ref[pl.ds(HALO + 1, s), :]


def _zero_halo(pad_ref, s):
    z = jnp.zeros((HALO, pad_ref.shape[1]), F32)
    pad_ref[pl.ds(0, HALO), :] = z
    pad_ref[pl.ds(HALO + s, HALO), :] = z


def _conv_fwd(tag, proj, conv_w, nb, s):
    t = proj.shape[0]
    ncb = CONV_WIDTH // LANES

    def body(gb_ref, gc_ref, hc_ref, w_ref, a_ref, pad_ref):
        _zero_halo(pad_ref, s)
        cg = gc_ref[...].astype(F32) * hc_ref[...].astype(F32)
        prev, nxt = _shifted(pad_ref, cg, s)
        w = w_ref[...]
        conv = prev * w[0:1, :] + cg * w[1:2, :] + nxt * w[2:3, :]
        a_ref[...] = (gb_ref[...].astype(F32) * conv).astype(a_ref.dtype)

    def col(off):
        return pl.BlockSpec((s, LANES), lambda b, c: (b, off + c))

    return pl.pallas_call(
        body, name=tag + "_conv", grid=(nb, ncb),
        in_specs=[col(0), col(ncb), col(2 * ncb), pl.BlockSpec((3, LANES), lambda b, c: (0, c))],
        out_specs=col(0),
        out_shape=jax.ShapeDtypeStruct((t, CONV_WIDTH), _ACT),
        scratch_shapes=[pltpu.VMEM((s + 2 * HALO, LANES), F32)],
        compiler_params=_params("parallel", "parallel"),
    )(proj, proj, proj, conv_w)


def _conv_bwd(tag, proj, conv_w, d_mix, nb, s):
    t = proj.shape[0]
    ncb = CONV_WIDTH // LANES

    def body(gb_ref, gc_ref, hc_ref, w_ref, da_ref, dgb_ref, dgc_ref, dhc_ref, dw_ref, pad_ref):
        b = pl.program_id(1)
        _zero_halo(pad_ref, s)
        gb = gb_ref[...].astype(F32)
        gc = gc_ref[...].astype(F32)
        hc = hc_ref[...].astype(F32)
        w = w_ref[...]
        da = da_ref[...]
        cg = gc * hc
        prev, nxt = _shifted(pad_ref, cg, s)
        conv = prev * w[0:1, :] + cg * w[1:2, :] + nxt * w[2:3, :]
        dgb_ref[...] = (da * conv).astype(dgb_ref.dtype)
        dconv = da * gb
        dw = jnp.concatenate([
            jnp.sum(dconv * prev, axis=0, keepdims=True),
            jnp.sum(dconv * cg, axis=0, keepdims=True),
            jnp.sum(dconv * nxt, axis=0, keepdims=True)], axis=0)
        dprev, dnxt = _shifted(pad_ref, dconv, s)
        dcg = dnxt * w[0:1, :] + dconv * w[1:2, :] + dprev * w[2:3, :]
        dgc_ref[...] = (dcg * hc).astype(dgc_ref.dtype)
        dhc_ref[...] = (dcg * gc).astype(dhc_ref.dtype)

        @pl.when(b == 0)
        def _():
            dw_ref[...] = dw

        @pl.when(b > 0)
        def _():
            dw_ref[...] += dw

    def col(off):
        return pl.BlockSpec((s, LANES), lambda c, b: (b, off + c))

    wspec = pl.BlockSpec((3, LANES), lambda c, b: (0, c))
    act = jax.ShapeDtypeStruct((t, CONV_WIDTH), _ACT)
    return pl.pallas_call(
        body, name=tag + "_dconv", grid=(ncb, nb),
        in_specs=[col(0), col(ncb), col(2 * ncb), wspec, col(0)],
        out_specs=(col(0), col(0), col(0), wspec),
        out_shape=(act, act, act, jax.ShapeDtypeStruct((3, CONV_WIDTH), F32)),
        scratch_shapes=[pltpu.VMEM((s + 2 * HALO, LANES), F32)],
        compiler_params=_params("parallel", "arbitrary"),
    )(proj, proj, proj, conv_w, d_mix)


def _rope_tables(s):
    rows = s // GRID_W
    r_idx, c_idx = jnp.meshgrid(jnp.arange(rows), jnp.arange(GRID_W), indexing="ij")
    r_idx = r_idx.reshape(-1).astype(F32)
    c_idx = c_idx.reshape(-1).astype(F32)
    n_freq = HEAD_DIM // 4
    inv = ROPE_THETA ** (-jnp.arange(n_freq, dtype=F32) / n_freq)
    ang = jnp.concatenate([r_idx[:, None] * inv, c_idx[:, None] * inv], axis=-1)
    cos = jnp.repeat(jnp.cos(ang), 2, axis=1)
    sin = jnp.repeat(jnp.sin(ang), 2, axis=1)
    sign = jnp.where(jnp.arange(HEAD_DIM) % 2 == 0, -1.0, 1.0).astype(F32)
    return jnp.tile(cos, (1, LANES // HEAD_DIM)), jnp.tile(sin * sign, (1, LANES // HEAD_DIM))


def _head_ones():
    i = jnp.arange(LANES) // HEAD_DIM
    return (i[:, None] == i[None, :]).astype(jnp.bfloat16)


def _head_sum(v, ones):
    outs = []
    for j in range(v.shape[1] // LANES):
        c = v[:, j * LANES:(j + 1) * LANES]
        hi = c.astype(jnp.bfloat16)
        lo = (c - hi.astype(F32)).astype(jnp.bfloat16)
        outs.append(jnp.dot(hi, ones, preferred_element_type=F32) + jnp.dot(lo, ones, preferred_element_type=F32))
    return outs[0] if len(outs) == 1 else jnp.concatenate(outs, axis=1)


def _pair_swap(v):
    outs = []
    for j in range(v.shape[1] // LANES):
        c = v[:, j * LANES:(j + 1) * LANES]
        lane = lax.broadcasted_iota(jnp.int32, c.shape, 1)
        outs.append(jnp.where(lane % 2 == 0, pltpu.roll(c, LANES - 1, 1), pltpu.roll(c, 1, 1)))
    return outs[0] if len(outs) == 1 else jnp.concatenate(outs, axis=1)


def _wide(tab, width):
    return tab if width == LANES else jnp.concatenate([tab] * (width // LANES), axis=1)


_QK_SCALE = HEAD_DIM ** -0.5


def _qk_fwd(tag, proj, q_gain, k_gain, cos, sin, nb, s):
    t = proj.shape[0]
    tr = _tile(s, 512)
    ns = s // tr
    q_off = 3 * CONV_WIDTH // ATTN_WIDTH
    k_off = (3 * CONV_WIDTH + ATTN_WIDTH) // KV_WIDTH

    def body(q_ref, k_ref, qg_ref, kg_ref, cos_ref, sin_ref, ones_ref, qo_ref, ko_ref):
        ones = ones_ref[...]
        for src, g_ref, dst, mult in ((q_ref, qg_ref, qo_ref, _QK_SCALE), (k_ref, kg_ref, ko_ref, 1.0)):
            v = src[...].astype(F32)
            w = v.shape[1]
            r = lax.rsqrt(_head_sum(v * v, ones) * (1.0 / HEAD_DIM) + EPS)
            vn = (v * r) * g_ref[...]
            rot = vn * _wide(cos_ref[...], w) + _pair_swap(vn) * _wide(sin_ref[...], w)
            dst[...] = (rot * mult).astype(dst.dtype)

    tab = pl.BlockSpec((tr, LANES), lambda i: (i % ns, 0))
    return pl.pallas_call(
        body, name=tag + "_qk", grid=(t // tr,),
        in_specs=[pl.BlockSpec((tr, ATTN_WIDTH), lambda i: (i, q_off)),
                  pl.BlockSpec((tr, KV_WIDTH), lambda i: (i, k_off)),
                  pl.BlockSpec((1, ATTN_WIDTH), lambda i: (0, 0)),
                  pl.BlockSpec((1, KV_WIDTH), lambda i: (0, 0)),
                  tab, tab, pl.BlockSpec((LANES, LANES), lambda i: (0, 0))],
        out_specs=(pl.BlockSpec((tr, ATTN_WIDTH), lambda i: (i, 0)),
                   pl.BlockSpec((tr, KV_WIDTH), lambda i: (i, 0))),
        out_shape=(jax.ShapeDtypeStruct((t, ATTN_WIDTH), _ACT), jax.ShapeDtypeStruct((t, KV_WIDTH), _ACT)),
        compiler_params=_params("parallel"),
    )(proj, proj, jnp.tile(q_gain, N_Q_HEADS).reshape(1, ATTN_WIDTH),
      jnp.tile(k_gain, N_KV_HEADS).reshape(1, KV_WIDTH), cos, sin, _head_ones())


def _qk_bwd(tag, proj, q_gain, k_gain, cos, sin, dq_rot, dk_rot, nb, s):
    t = proj.shape[0]
    tr = _tile(s, 512)
    ns = s // tr
    q_off = 3 * CONV_WIDTH // ATTN_WIDTH
    k_off = (3 * CONV_WIDTH + ATTN_WIDTH) // KV_WIDTH

    def body(q_ref, k_ref, qg_ref, kg_ref, cos_ref, sin_ref, ones_ref, dqr_ref, dkr_ref,
             dq_ref, dk_ref, dqg_ref, dkg_ref):
        i = pl.program_id(0)
        ones = ones_ref[...]
        for src, g_ref, dr_ref, dst, dg_ref, mult in ((q_ref, qg_ref, dqr_ref, dq_ref, dqg_ref, _QK_SCALE),
                                                      (k_ref, kg_ref, dkr_ref, dk_ref, dkg_ref, 1.0)):
            v = src[...].astype(F32)
            w = v.shape[1]
            r = lax.rsqrt(_head_sum(v * v, ones) * (1.0 / HEAD_DIM) + EPS)
            xhat = v * r
            dr = dr_ref[...] * mult
            dvn = dr * _wide(cos_ref[...], w) + _pair_swap(dr * _wide(sin_ref[...], w))
            dgx = dvn * g_ref[...]
            m = _head_sum(dgx * xhat, ones) * (1.0 / HEAD_DIM)
            dst[...] = (r * (dgx - xhat * m)).astype(dst.dtype)
            part = jnp.sum(dvn * xhat, axis=0, keepdims=True)
            fold = part[:, 0:HEAD_DIM]
            for hh in range(1, w // HEAD_DIM):
                fold = fold + part[:, hh * HEAD_DIM:(hh + 1) * HEAD_DIM]

            @pl.when(i == 0)
            def _():
                dg_ref[...] = fold

            @pl.when(i > 0)
            def _():
                dg_ref[...] += fold

    tab = pl.BlockSpec((tr, LANES), lambda i: (i % ns, 0))
    qrow = pl.BlockSpec((tr, ATTN_WIDTH), lambda i: (i, 0))
    krow = pl.BlockSpec((tr, KV_WIDTH), lambda i: (i, 0))
    gvec = pl.BlockSpec((1, HEAD_DIM), lambda i: (0, 0))
    dq, dk, dqg, dkg = pl.pallas_call(
        body, name=tag + "_dqk", grid=(t // tr,),
        in_specs=[pl.BlockSpec((tr, ATTN_WIDTH), lambda i: (i, q_off)),
                  pl.BlockSpec((tr, KV_WIDTH), lambda i: (i, k_off)),
                  pl.BlockSpec((1, ATTN_WIDTH), lambda i: (0, 0)),
                  pl.BlockSpec((1, KV_WIDTH), lambda i: (0, 0)),
                  tab, tab, pl.BlockSpec((LANES, LANES), lambda i: (0, 0)), qrow, krow],
        out_specs=(qrow, krow, gvec, gvec),
        out_shape=(jax.ShapeDtypeStruct((t, ATTN_WIDTH), _ACT), jax.ShapeDtypeStruct((t, KV_WIDTH), _ACT),
                   jax.ShapeDtypeStruct((1, HEAD_DIM), F32), jax.ShapeDtypeStruct((1, HEAD_DIM), F32)),
        compiler_params=_params("arbitrary"),
    )(proj, proj, jnp.tile(q_gain, N_Q_HEADS).reshape(1, ATTN_WIDTH),
      jnp.tile(k_gain, N_KV_HEADS).reshape(1, KV_WIDTH), cos, sin, _head_ones(), dq_rot, dk_rot)
    return dq, dk, dqg.reshape(HEAD_DIM), dkg.reshape(HEAD_DIM)


def _head(v, h):
    return v[:, h * HEAD_DIM:(h + 1) * HEAD_DIM]


def _attn_fwd(tag, q, k, proj, nb, s):
    t = q.shape[0]
    tq = _tile(s, 256)
    nq = s // tq
    v_off = (3 * CONV_WIDTH + ATTN_WIDTH + KV_WIDTH) // KV_WIDTH

    def body(q_ref, k_ref, v_ref, o_ref, lse_ref):
        qv = q_ref[...]
        kv = k_ref[...]
        vv = v_ref[...]
        for h in range(N_Q_HEADS):
            j = h // Q_PER_KV
            sc = _dot(_head(qv, h), _head(kv, j), "nt")
            m = jnp.max(sc, axis=-1, keepdims=True)
            e = jnp.exp(sc - m)
            l = jnp.sum(e, axis=-1, keepdims=True)
            o = _dot(e, _head(vv, j)) * (1.0 / l)
            o_ref[:, h * HEAD_DIM:(h + 1) * HEAD_DIM] = o.astype(o_ref.dtype)
            lse_ref[:, h:h + 1] = m + jnp.log(l)

    return pl.pallas_call(
        body, name=tag + "_attn", grid=(nb, nq),
        in_specs=[pl.BlockSpec((tq, ATTN_WIDTH), lambda b, i: (b * nq + i, 0)),
                  pl.BlockSpec((s, KV_WIDTH), lambda b, i: (b, 0)),
                  pl.BlockSpec((s, KV_WIDTH), lambda b, i: (b, v_off))],
        out_specs=(pl.BlockSpec((tq, ATTN_WIDTH), lambda b, i: (b * nq + i, 0)),
                   pl.BlockSpec((tq, N_Q_HEADS), lambda b, i: (b * nq + i, 0))),
        out_shape=(jax.ShapeDtypeStruct((t, ATTN_WIDTH), _ACT), jax.ShapeDtypeStruct((t, N_Q_HEADS), F32)),
        compiler_params=_params("parallel", "parallel"),
    )(q, k, proj)


def _attn_bwd(tag, q, k, proj, o, lse, d_mix, nb, s):
    t = q.shape[0]
    tq = _tile(s, 256)
    nq = s // tq
    v_off = (3 * CONV_WIDTH + ATTN_WIDTH + KV_WIDTH) // KV_WIDTH

    def body(q_ref, k_ref, v_ref, o_ref, lse_ref, do_ref, dq_ref, dk_ref, dv_ref):
        i = pl.program_id(1)

        @pl.when(i == 0)
        def _():
            dk_ref[...] = jnp.zeros_like(dk_ref)
            dv_ref[...] = jnp.zeros_like(dv_ref)

        qv = q_ref[...]
        kv = k_ref[...]
        vv = v_ref[...]
        ov = o_ref[...].astype(F32)
        dov = do_ref[...]
        lse = lse_ref[...]
        for h in range(N_Q_HEADS):
            j = h // Q_PER_KV
            cols = slice(j * HEAD_DIM, (j + 1) * HEAD_DIM)
            qh = _head(qv, h)
            kj = _head(kv, j)
            doh = _head(dov, h)
            sc = _dot(qh, kj, "nt")
            p = jnp.exp(sc - lse[:, h:h + 1])
            dp = _dot(doh, _head(vv, j), "nt")
            delta = jnp.sum(doh * _head(ov, h), axis=-1, keepdims=True)
            ds = p * (dp - delta)
            dv_ref[:, cols] += _dot(p, doh, "tn")
            dk_ref[:, cols] += _dot(ds, qh, "tn")
            dq_ref[:, h * HEAD_DIM:(h + 1) * HEAD_DIM] = _dot(ds, kj)

    qrow = pl.BlockSpec((tq, ATTN_WIDTH), lambda b, i: (b * nq + i, 0))
    kvrow = pl.BlockSpec((s, KV_WIDTH), lambda b, i: (b, 0))
    return pl.pallas_call(
        body, name=tag + "_dattn", grid=(nb, nq),
        in_specs=[qrow, kvrow, pl.BlockSpec((s, KV_WIDTH), lambda b, i: (b, v_off)), qrow,
                  pl.BlockSpec((tq, N_Q_HEADS), lambda b, i: (b * nq + i, 0)),
                  pl.BlockSpec((tq, ATTN_WIDTH), lambda b, i: (b * nq + i, 1))],
        out_specs=(qrow, kvrow, kvrow),
        out_shape=(jax.ShapeDtypeStruct((t, ATTN_WIDTH), F32), jax.ShapeDtypeStruct((t, KV_WIDTH), F32),
                   jax.ShapeDtypeStruct((t, KV_WIDTH), F32)),
        compiler_params=_params("parallel", "arbitrary"),
    )(q, k, proj, o, lse, d_mix)


def _even_fwd(tag, x, p, cos, sin, nb, s):
    h = _rmsnorm_fwd(tag + "_norm", x, p["norm"])
    proj = _proj_in(tag, h, p["w_in"])
    a = _conv_fwd(tag, proj, p["conv_w"], nb, s)
    q, k = _qk_fwd(tag, proj, p["q_gain"], p["k_gain"], cos, sin, nb, s)
    o, lse = _attn_fwd(tag, q, k, proj, nb, s)
    x_out = _proj_out(tag, x, [a, o], p["w_out"])
    return x_out, (x, h, proj, a, q, k, o, lse)


def _even_bwd(tag, dxo, saved, p, cos, sin, nb, s):
    x, h, proj, a, q, k, o, lse = saved
    d_mix, dw_out = _proj_out_bwd(tag, dxo, [a, o], p["w_out"])
    dgb, dgc, dhc, dconv_w = _conv_bwd(tag, proj, p["conv_w"], d_mix, nb, s)
    dq_rot, dk_rot, dv = _attn_bwd(tag, q, k, proj, o, lse, d_mix, nb, s)
    dq, dk, dq_gain, dk_gain = _qk_bwd(tag, proj, p["q_gain"], p["k_gain"], cos, sin, dq_rot, dk_rot, nb, s)
    dx, dnorm, dw_in = _proj_in_bwd(tag, h, [dgb, dgc, dhc, dq, dk, dv], p["w_in"], x, dxo, p["norm"])
    grads = dict(norm=dnorm, w_in=dw_in, w_out=dw_out, conv_w=dconv_w, q_gain=dq_gain, k_gain=dk_gain)
    return dx, grads


def _window(pad_ref, val, r, s):
    pad_ref[pl.ds(HALO, s), :] = val
    acc = val
    for d in range(1, r + 1):
        acc = acc + pad_ref[pl.ds(HALO - d, s), :] + pad_ref[pl.ds(HALO + d, s), :]
    return acc


def _count(r, s):
    t = lax.broadcasted_iota(jnp.int32, (s, 1), 0)
    return (jnp.minimum(t + r, s - 1) - jnp.maximum(t - r, 0) + 1).astype(F32)


def _sgu_chunk(u_ref, v_ref, norm, ws_ref, bt, rows):
    uu = u_ref[rows, :].astype(F32)
    vv = v_ref[rows, :].astype(F32)
    gu = _gelu(uu)
    gv = _gelu(vv)
    r = lax.rsqrt(jnp.mean(gv * gv, axis=-1, keepdims=True) + EPS)
    xhat = gv * r
    vn = xhat * norm
    mixed = []
    for g in range(N_GROUPS):
        cols = slice(g * SGU_GROUP, (g + 1) * SGU_GROUP)
        mixed.append(_dot(ws_ref[g], vn[:, cols]) + bt[:, g:g + 1])
    return uu, vv, gu, r, xhat, vn, mixed


def _odd_core_fwd(tag, proj, p, nb, s):
    t = proj.shape[0]
    nchunk = s // SGU_CHUNK

    def body(p_ref, u_ref, v_ref, pw_ref, ps_ref, sn_ref, ws_ref, bt_ref, mix_ref, pad_ref):
        _zero_halo(pad_ref, s)
        for g, r in enumerate(POOL_RADII):
            cols = slice(g * POOL_GROUP, (g + 1) * POOL_GROUP)
            pg = p_ref[:, cols].astype(F32)
            pooled = _window(pad_ref, pg, r, s) / _count(r, s) - pg
            mix_ref[:, cols] = (_dot(pooled, pw_ref[g]) * ps_ref[:, cols]).astype(mix_ref.dtype)
        norm = sn_ref[...]
        bt = bt_ref[...]

        def chunk(n, carry):
            rows = pl.ds(pl.multiple_of(n * SGU_CHUNK, SGU_CHUNK), SGU_CHUNK)
            _, _, gu, _, _, _, mixed = _sgu_chunk(u_ref, v_ref, norm, ws_ref, bt, rows)
            for g in range(N_GROUPS):
                cols = slice(g * SGU_GROUP, (g + 1) * SGU_GROUP)
                mix_ref[rows, HALF + g * SGU_GROUP:HALF + (g + 1) * SGU_GROUP] = (
                    gu[:, cols] * mixed[g]).astype(mix_ref.dtype)
            return carry

        lax.fori_loop(0, nchunk, chunk, 0)

    def col(j):
        return pl.BlockSpec((s, HALF), lambda b: (b, j))

    def whole(a):
        return pl.BlockSpec(a.shape, lambda b: (0,) * a.ndim)

    consts = [p["pool_w"], p["pool_scale"].reshape(1, HALF), p["sgu_norm"].reshape(1, HALF),
              p["sgu_w"], p["sgu_b"].T]
    return pl.pallas_call(
        body, name=tag + "_core", grid=(nb,),
        in_specs=[col(0), col(1), col(2)] + [whole(a) for a in consts],
        out_specs=pl.BlockSpec((s, D_MODEL), lambda b: (b, 0)),
        out_shape=jax.ShapeDtypeStruct((t, D_MODEL), _ACT),
        scratch_shapes=[pltpu.VMEM((s + 2 * HALO, POOL_GROUP), F32)],
        compiler_params=_params("parallel"),
    )(proj, proj, proj, *consts)


def _odd_core_bwd(tag, proj, p, d_mix, nb, s):
    t = proj.shape[0]
    nchunk = s // SGU_CHUNK

    def body(p_ref, u_ref, v_ref, pw_ref, ps_ref, sn_ref, ws_ref, bt_ref, dm_ref,
             dproj_ref, dpw_ref, dps_ref, dsn_ref, dws_ref, dbt_ref, pad_ref):
        b = pl.program_id(0)

        @pl.when(b == 0)
        def _():
            dpw_ref[...] = jnp.zeros_like(dpw_ref)
            dps_ref[...] = jnp.zeros_like(dps_ref)
            dsn_ref[...] = jnp.zeros_like(dsn_ref)
            dws_ref[...] = jnp.zeros_like(dws_ref)
            dbt_ref[...] = jnp.zeros_like(dbt_ref)

        _zero_halo(pad_ref, s)
        for g, r in enumerate(POOL_RADII):
            cols = slice(g * POOL_GROUP, (g + 1) * POOL_GROUP)
            pg = p_ref[:, cols].astype(F32)
            cnt = _count(r, s)
            pooled = _window(pad_ref, pg, r, s) / cnt - pg
            c_pre = _dot(pooled, pw_ref[g])
            dc = dm_ref[:, cols]
            dps_ref[:, cols] += jnp.sum(dc * c_pre, axis=0, keepdims=True)
            dcp = dc * ps_ref[:, cols]
            dpw_ref[g] += _dot(pooled, dcp, "tn")
            dpooled = _dot(dcp, pw_ref[g], "nt")
            dproj_ref[:, cols] = (_window(pad_ref, dpooled / cnt, r, s) - dpooled).astype(dproj_ref.dtype)
        norm = sn_ref[...]
        bt = bt_ref[...]

        def chunk(n, carry):
            rows = pl.ds(pl.multiple_of(n * SGU_CHUNK, SGU_CHUNK), SGU_CHUNK)
            uu, vv, gu, r, xhat, vn, mixed = _sgu_chunk(u_ref, v_ref, norm, ws_ref, bt, rows)
            dd = dm_ref[rows, HALF:D_MODEL]
            dgu, dvn = [], []
            for g in range(N_GROUPS):
                cols = slice(g * SGU_GROUP, (g + 1) * SGU_GROUP)
                dgu.append(dd[:, cols] * mixed[g])
                dmx = dd[:, cols] * gu[:, cols]
                dbt_ref[:, g:g + 1] += jnp.sum(dmx, axis=-1, keepdims=True)
                dws_ref[g] += _dot(dmx, vn[:, cols], "nt")
                dvn.append(_dot(ws_ref[g], dmx, "tn"))
            dgu = jnp.concatenate(dgu, axis=1)
            dvn = jnp.concatenate(dvn, axis=1)
            dsn_ref[...] += jnp.sum(dvn * xhat, axis=0, keepdims=True)
            dgx = dvn * norm
            m = jnp.mean(dgx * xhat, axis=-1, keepdims=True)
            dgv = r * (dgx - xhat * m)
            dproj_ref[rows, HALF:2 * HALF] = (dgu * _gelu_grad(uu)).astype(dproj_ref.dtype)
            dproj_ref[rows, 2 * HALF:3 * HALF] = (dgv * _gelu_grad(vv)).astype(dproj_ref.dtype)
            return carry

        lax.fori_loop(0, nchunk, chunk, 0)

    def col(j):
        return pl.BlockSpec((s, HALF), lambda b: (b, j))

    def whole(a):
        return pl.BlockSpec(a.shape, lambda b: (0,) * a.ndim)

    consts = [p["pool_w"], p["pool_scale"].reshape(1, HALF), p["sgu_norm"].reshape(1, HALF),
              p["sgu_w"], p["sgu_b"].T]
    gshapes = [jax.ShapeDtypeStruct(a.shape, F32) for a in consts]
    dproj, dpw, dps, dsn, dws, dbt = pl.pallas_call(
        body, name=tag + "_dcore", grid=(nb,),
        in_specs=[col(0), col(1), col(2)] + [whole(a) for a in consts]
        + [pl.BlockSpec((s, D_MODEL), lambda b: (b, 0))],
        out_specs=[pl.BlockSpec((s, 3 * HALF), lambda b: (b, 0))] + [whole(a) for a in consts],
        out_shape=[jax.ShapeDtypeStruct((t, 3 * HALF), _ACT)] + gshapes,
        scratch_shapes=[pltpu.VMEM((s + 2 * HALO, POOL_GROUP), F32)],
        compiler_params=_params("arbitrary"),
    )(proj, proj, proj, *consts, d_mix)
    return dproj, dict(pool_w=dpw, pool_scale=dps.reshape(HALF), sgu_norm=dsn.reshape(HALF), sgu_w=dws, sgu_b=dbt.T)


def _odd_fwd(tag, x, p, nb, s):
    h = _rmsnorm_fwd(tag + "_norm", x, p["norm"])
    proj = _proj_in(tag, h, p["w_in"])
    mix = _odd_core_fwd(tag, proj, p, nb, s)
    x_out = _proj_out(tag, x, [mix], p["w_out"])
    return x_out, (x, h, proj, mix)


def _odd_bwd(tag, dxo, saved, p, nb, s):
    x, h, proj, mix = saved
    d_mix, dw_out = _proj_out_bwd(tag, dxo, [mix], p["w_out"])
    dproj, grads = _odd_core_bwd(tag, proj, p, d_mix, nb, s)
    dx, dnorm, dw_in = _proj_in_bwd(tag, h, [dproj], p["w_in"], x, dxo, p["norm"])
    grads.update(norm=dnorm, w_in=dw_in, w_out=dw_out)
    return dx, grads


def _local_step(x3, target3, depth, weights_of, final_norm, grads_done):
    nb, s, d = x3.shape
    t = nb * s
    x = x3.reshape(t, d)
    target = target3.reshape(t, d)
    cos, sin = _rope_tables(s)
    saved, ws = [], []
    for l in range(depth):
        w1 = weights_of(l, "ffn1", x)
        x, s1 = _ffn_fwd(f"l{l}_ffn1", x, w1["norm"], w1["w_in4"], w1["w_out"], _ffn_tiles(l, 1))
        wm = weights_of(l, "mix", x)
        if l % 2 == 0:
            x, s2 = _even_fwd(f"l{l}_ev", x, wm, cos, sin, nb, s)
        else:
            x, s2 = _odd_fwd(f"l{l}_od", x, wm, nb, s)
        w2 = weights_of(l, "ffn2", x)
        x, s3 = _ffn_fwd(f"l{l}_ffn2", x, w2["norm"], w2["w_in4"], w2["w_out"], _ffn_tiles(l, 2))
        saved.append((s1, s2, s3))
        ws.append((w1, wm, w2))
    loss, dx, dfinal = _final_loss("final_loss", x, final_norm, target)
    zero = 0.0
    for l in reversed(range(depth)):
        s1, s2, s3 = saved[l]
        w1, wm, w2 = ws[l]
        dx, dn, dwi, dwo = _ffn_bwd(f"l{l}_ffn2", dx, s3, w2["norm"] + zero, w2["w_in4"], w2["w_out"], _ffn_tiles(l, 2))
        zero = grads_done(l, "ffn2", dict(norm=dn, w_in4=dwi, w_out=dwo), dx)
        wm = dict(wm, norm=wm["norm"] + zero)
        if l % 2 == 0:
            dx, gm = _even_bwd(f"l{l}_ev", dx, s2, wm, cos, sin, nb, s)
        else:
            dx, gm = _odd_bwd(f"l{l}_od", dx, s2, wm, nb, s)
        zero = grads_done(l, "mix", gm, dx)
        dx, dn, dwi, dwo = _ffn_bwd(f"l{l}_ffn1", dx, s1, w1["norm"] + zero, w1["w_in4"], w1["w_out"], _ffn_tiles(l, 1))
        zero = grads_done(l, "ffn1", dict(norm=dn, w_in4=dwi, w_out=dwo), dx)
    return loss, dx.reshape(nb, s, d), dfinal


_HBM = pl.BlockSpec(memory_space=pltpu.HBM)


def _place():
    x, y, c = lax.axis_index("x"), lax.axis_index("y"), lax.axis_index("c")
    chips = [(1 - x, y), (x, 1 - y), (1 - x, 1 - y)]
    return x, y, c, chips


def _remote(src, dst, send_sem, recv_sem, to):
    return pltpu.make_async_remote_copy(src_ref=src, dst_ref=dst, send_sem=send_sem, recv_sem=recv_sem,
                                        device_id=to, device_id_type=_MESH)


def _gather_shards(arrs, small):
    n = len(arrs)
    own = 6

    def body(*refs):
        ins, sm_in = refs[:n], refs[n]
        outs, sm_out = refs[n + 1:2 * n + 1], refs[2 * n + 1]
        send, recv = refs[2 * n + 2:]
        x, y, c, chips = _place()
        k = 2 * x + y
        sib = (x, y, 1 - c)
        started = []
        for a in range(n + 1):
            src, dst = (ins[a], outs[a]) if a < n else (sm_in, sm_out)
            cp = _remote(src, dst.at[k], send.at[a, own], recv.at[a, own], sib)
            cp.start()
            started.append(cp)
            if a < n:
                h = src.shape[0] // 2
                mine = pl.ds(c * h, h)
                src_part, dst_part = src.at[mine], dst.at[k, mine]
            else:
                src_part, dst_part = src, dst.at[k]
            for j, chip in enumerate(chips):
                cp = _remote(src_part, dst_part, send.at[a, j], recv.at[a, j], (*chip, c))
                cp.start()
                started.append(cp)
        for a in range(n):
            h = ins[a].shape[0] // 2
            mine = pl.ds(c * h, h)
            for j, (px, py) in enumerate(chips):
                landed = outs[a].at[2 * px + py, mine]
                _remote(landed, landed, send.at[a, j], recv.at[a, j], (px, py, c)).wait_recv()
                cp = _remote(landed, landed, send.at[a, 3 + j], recv.at[a, 3 + j], sib)
                cp.start()
                started.append(cp)
        for a in range(n):
            h = ins[a].shape[0] // 2
            other = pl.ds((1 - c) * h, h)
            for j, (px, py) in enumerate(chips):
                passed = outs[a].at[2 * px + py, other]
                _remote(passed, passed, send.at[a, 3 + j], recv.at[a, 3 + j], sib).wait_recv()
        for j, (px, py) in enumerate(chips):
            landed = sm_out.at[2 * px + py]
            _remote(landed, landed, send.at[n, j], recv.at[n, j], (px, py, c)).wait_recv()
        for a in range(n + 1):
            filled = (outs[a] if a < n else sm_out).at[k]
            _remote(filled, filled, send.at[a, own], recv.at[a, own], sib).wait_recv()
        for cp in started:
            cp.wait_send()

    outs = pl.pallas_call(
        body, name="gather_shards",
        in_specs=[_HBM] * (n + 1), out_specs=[_HBM] * (n + 1),
        out_shape=[jax.ShapeDtypeStruct((N_CHIPS,) + a.shape, a.dtype) for a in list(arrs) + [small]],
        scratch_shapes=[pltpu.SemaphoreType.DMA((n + 1, 7)), pltpu.SemaphoreType.DMA((n + 1, 7))],
    )(*arrs, small)
    return outs[:n], outs[n]


def _swap_halves(name, grads):
    n = len(grads)

    def body(*refs):
        ins, outs = refs[:n], refs[n:2 * n]
        send, recv = refs[2 * n:]
        x, y, c, _ = _place()
        sib = (x, y, 1 - c)
        cps = []
        for a in range(n):
            h = ins[a].shape[1] // 2
            cp = _remote(ins[a].at[:, pl.ds((1 - c) * h, h)], outs[a], send.at[a], recv.at[a], sib)
            cp.start()
            cps.append(cp)
        for cp in cps:
            cp.wait()

    return pl.pallas_call(
        body, name=name,
        in_specs=[_HBM] * n, out_specs=[_HBM] * n,
        out_shape=[jax.ShapeDtypeStruct((g.shape[0], g.shape[1] // 2) + g.shape[2:], g.dtype) for g in grads],
        scratch_shapes=[pltpu.SemaphoreType.DMA((n,)), pltpu.SemaphoreType.DMA((n,))],
    )(*grads)


_SEM = pl.BlockSpec(memory_space=pltpu.SEMAPHORE)
_EFFECT = pltpu.SideEffectType.DATAFLOW_SIDE_EFFECTING


def _gather_plan(src, land, k, c, chips, sib):
    mine = pl.ds(c * (src.shape[0] // 2), src.shape[0] // 2)
    plan = [(src.at[mine], land.at[k, mine], (px, py, c), land.at[2 * px + py, mine]) for px, py in chips]
    return plan + [(src, land.at[k], sib, land.at[k])]


def _scatter_plan(src, land, k, c, chips, sib):
    return [(src.at[2 * px + py], land.at[k], (px, py, c), land.at[2 * px + py]) for px, py in chips]


def _split_start(name, plan, ncopy, srcs, after):
    n = len(srcs)
    lands = [pltpu.with_memory_space_constraint(lax.empty((N_CHIPS,) + a.shape[-2:], a.dtype), pltpu.HBM) for a in srcs]

    def body(*refs):
        src_refs, land_refs = refs[1:1 + n], refs[1 + n:1 + 2 * n]
        send, recv, token = refs[1 + 2 * n], refs[2 + 2 * n], refs[-1]
        x, y, c, chips = _place()
        for i in range(n):
            for j, (src, dst, peer, _) in enumerate(plan(src_refs[i], land_refs[i], 2 * x + y, c, chips, (x, y, 1 - c))):
                _remote(src, dst, send.at[i * ncopy + j], recv.at[i * ncopy + j], peer).start()
        token[...] = jnp.zeros_like(token)

    outs = pl.pallas_call(
        body, name=name,
        in_specs=[_ANY] + [_HBM] * (2 * n),
        out_specs=[_SEM, _SEM] + [_HBM] * (2 * n) + [_VMEM],
        out_shape=[pltpu.SemaphoreType.DMA((n * ncopy,)), pltpu.SemaphoreType.DMA((n * ncopy,))]
        + [pltpu.HBM(a.shape, a.dtype) for a in list(srcs) + lands] + [jax.ShapeDtypeStruct((8, LANES), F32)],
        input_output_aliases={1 + i: 2 + i for i in range(2 * n)},
        compiler_params=pltpu.CompilerParams(has_side_effects=_EFFECT),
    )(after, *[pltpu.with_memory_space_constraint(a, pltpu.HBM) for a in srcs], *lands)
    return outs[0], outs[1], outs[2:2 + n], outs[2 + n:2 + 2 * n], outs[-1]


def _split_wait(name, plan, started, after):
    send, recv, srcs, lands = started
    n = len(srcs)
    ncopy = send.shape[0] // n

    def body(*refs):
        src_refs, land_refs = refs[:n], refs[n:2 * n]
        send, recv = refs[2 * n], refs[2 * n + 1]
        x, y, c, chips = _place()
        for i in range(n):
            for j, (src, _, peer, landed) in enumerate(plan(src_refs[i], land_refs[i], 2 * x + y, c, chips, (x, y, 1 - c))):
                cp = _remote(src, landed, send.at[i * ncopy + j], recv.at[i * ncopy + j], peer)
                cp.wait_send()
                cp.wait_recv()

    outs = pl.pallas_call(
        body, name=name,
        in_specs=[_HBM] * (2 * n) + [_SEM, _SEM, _ANY],
        out_specs=[_HBM] * (2 * n),
        out_shape=[pltpu.HBM(a.shape, a.dtype) for a in list(srcs) + list(lands)],
        input_output_aliases={i: i for i in range(2 * n)},
        compiler_params=pltpu.CompilerParams(has_side_effects=_EFFECT),
    )(*srcs, *lands, send, recv, after)
    return outs[:n], outs[n:]


def _pass_halves(name, lands):
    n = len(lands)

    def body(*refs):
        ins, outs = refs[:n], refs[n:2 * n]
        send, recv = refs[2 * n:]
        x, y, c, chips = _place()
        sib = (x, y, 1 - c)
        cps = []
        for a in range(n):
            h = ins[a].shape[1] // 2
            for j, (px, py) in enumerate(chips):
                rows = (2 * px + py, pl.ds(c * h, h))
                cp = _remote(ins[a].at[rows], outs[a].at[rows], send.at[a, j], recv.at[a, j], sib)
                cp.start()
                cps.append(cp)
        for a in range(n):
            h = ins[a].shape[1] // 2
            for j, (px, py) in enumerate(chips):
                theirs = outs[a].at[2 * px + py, pl.ds((1 - c) * h, h)]
                _remote(theirs, theirs, send.at[a, j], recv.at[a, j], sib).wait_recv()
        for cp in cps:
            cp.wait_send()

    return pl.pallas_call(
        body, name=name,
        in_specs=[_HBM] * n, out_specs=[_HBM] * n,
        out_shape=[jax.ShapeDtypeStruct(p.shape, p.dtype) for p in lands],
        input_output_aliases={a: a for a in range(n)},
        scratch_shapes=[pltpu.SemaphoreType.DMA((n, 3)), pltpu.SemaphoreType.DMA((n, 3))],
    )(*lands)


def _join_halves(name, bufs, layers):
    n = len(bufs)

    def body(*refs):
        ins, outs = refs[:n], refs[n:2 * n]
        send, recv = refs[2 * n:]
        x, y, c, _ = _place()
        sib = (x, y, 1 - c)
        cps = []
        for a in range(n):
            h = ins[a].shape[1] // 2
            mine = pl.ds(c * h, h)
            cp = _remote(ins[a].at[layers[a], mine], outs[a].at[layers[a], mine], send.at[a], recv.at[a], sib)
            cp.start()
            cps.append(cp)
        for a in range(n):
            h = ins[a].shape[1] // 2
            theirs = outs[a].at[layers[a], pl.ds((1 - c) * h, h)]
            _remote(theirs, theirs, send.at[a], recv.at[a], sib).wait_recv()
        for cp in cps:
            cp.wait_send()

    return pl.pallas_call(
        body, name=name,
        in_specs=[_HBM] * n, out_specs=[_HBM] * n,
        out_shape=[jax.ShapeDtypeStruct(p.shape, p.dtype) for p in bufs],
        input_output_aliases={a: a for a in range(n)},
        scratch_shapes=[pltpu.SemaphoreType.DMA((n,)), pltpu.SemaphoreType.DMA((n,))],
    )(*bufs)


def _allreduce_small(buf, after):
    rows = buf.shape[0]

    def body(in_ref, after_ref, out_ref, land_ref, send, recv):
        x, y, c, _ = _place()
        me = 4 * x + 2 * y + c
        land_ref[me] = in_ref[...]
        peers = []
        for r in range(1, N_DEV):
            peers.append((1 - x if r & 4 else x, 1 - y if r & 2 else y, 1 - c if r & 1 else c))
        cps = []
        for r, peer in enumerate(peers):
            cp = _remote(in_ref, land_ref.at[me], send.at[r], recv.at[r], peer)
            cp.start()
            cps.append(cp)
        for r, (px, py, pc) in enumerate(peers):
            landed = land_ref.at[4 * px + 2 * py + pc]
            _remote(landed, landed, send.at[r], recv.at[r], (px, py, pc)).wait_recv()
        for cp in cps:
            cp.wait_send()
        acc = land_ref[0]
        for d in range(1, N_DEV):
            acc = acc + land_ref[d]
        out_ref[...] = acc

    return pl.pallas_call(
        body, name="allreduce_small",
        in_specs=[_VMEM, _ANY], out_specs=_VMEM,
        out_shape=jax.ShapeDtypeStruct(buf.shape, F32),
        scratch_shapes=[pltpu.VMEM((N_DEV, rows, LANES), F32), pltpu.SemaphoreType.DMA((N_DEV - 1,)),
                        pltpu.SemaphoreType.DMA((N_DEV - 1,))],
        compiler_params=pltpu.CompilerParams(vmem_limit_bytes=_VMEM_LIMIT),
    )(buf, after)


def _div_tile(n, cap, mult):
    best = None
    for d in range(mult, min(n, cap) + 1, mult):
        if n % d == 0:
            best = d
    return best if best is not None else n


def _add_sibling(name, grad, got, c):
    nk, hr, cc = got.shape
    tr = _div_tile(hr, 512, 16)
    nt = hr // tr

    def body(c_ref, g_ref, o_ref, s_ref):
        s_ref[...] = (g_ref[...].astype(F32) + o_ref[...].astype(F32)).astype(s_ref.dtype)

    blk = (None, tr, cc)
    return pl.pallas_call(
        body, name=name,
        grid_spec=pltpu.PrefetchScalarGridSpec(
            num_scalar_prefetch=1, grid=(nk, nt),
            in_specs=[pl.BlockSpec(blk, lambda i, q, c_ref: (i, c_ref[0] * nt + q, 0)),
                      pl.BlockSpec(blk, lambda i, q, c_ref: (i, q, 0))],
            out_specs=pl.BlockSpec(blk, lambda i, q, c_ref: (i, q, 0))),
        out_shape=jax.ShapeDtypeStruct(got.shape, got.dtype),
        compiler_params=_params("parallel", "parallel"),
    )(c, grad, got)


def _add_chips(name, mine, got, place, buf, l):
    nk, hr, cc = got.shape
    tr = _div_tile(hr, 512, 16)
    nt = hr // tr

    def body(*refs):
        acc = refs[1][...].astype(F32)
        for q in range(1, nk):
            acc = acc + refs[1 + q][...].astype(F32)
        refs[2 + nk][...] = acc

    def part(q):
        return pl.BlockSpec((None, tr, cc), lambda i, p_ref: ((p_ref[0] + q) % nk, i, 0))

    return pl.pallas_call(
        body, name=name,
        grid_spec=pltpu.PrefetchScalarGridSpec(
            num_scalar_prefetch=1, grid=(nt,),
            in_specs=[part(q) for q in range(nk)] + [_ANY],
            out_specs=pl.BlockSpec((None, tr, cc), lambda i, p_ref: (l, p_ref[1] * nt + i, 0))),
        out_shape=jax.ShapeDtypeStruct(buf.shape, F32),
        input_output_aliases={1 + nk: 0},
        compiler_params=_params("parallel"),
    )(place, mine, *([got] * (nk - 1)), buf)


def _adamw(name, w, g, m, v, after=None):
    shape = w.shape
    cols = shape[-1]
    rows = w.size // cols
    tr = rows if rows * cols <= 2 ** 18 else _div_tile(rows, max(8, 2 ** 18 // cols), 8)
    c1 = 1.0 - ADAM_B1 ** ADAM_STEP
    c2 = 1.0 - ADAM_B2 ** ADAM_STEP
    extra = [] if after is None else [after]

    def body(*refs):
        w_ref, g_ref, m_ref, v_ref = refs[:4]
        d_ref, mo_ref, vo_ref, go_ref = refs[4 + len(extra):]
        gg = g_ref[...]
        mn = ADAM_B1 * m_ref[...] + (1.0 - ADAM_B1) * gg
        vn = ADAM_B2 * v_ref[...] + (1.0 - ADAM_B2) * (gg * gg)
        d_ref[...] = -ADAM_LR * ((mn / c1) / (jnp.sqrt(vn / c2) + ADAM_EPS) + ADAM_WD * w_ref[...])
        mo_ref[...] = mn
        vo_ref[...] = vn
        go_ref[...] = gg

    blk = pl.BlockSpec((tr, cols), lambda i: (i, 0))
    sds = jax.ShapeDtypeStruct((rows, cols), F32)
    outs = pl.pallas_call(
        body, name=name, grid=(rows // tr,),
        in_specs=[blk] * 4 + [_ANY] * len(extra), out_specs=(blk,) * 4, out_shape=(sds,) * 4,
        compiler_params=_params("parallel"),
    )(*[a.reshape(rows, cols) for a in (w, g, m, v)], *extra)
    return [o.reshape(shape) for o in outs]


_WEIGHTS = ["ffn1_norm", "ffn1_w_in", "ffn1_w_out", "mix_norm", "ffn2_norm", "ffn2_w_in", "ffn2_w_out",
            "ev_w_in", "ev_conv_w", "ev_q_norm", "ev_k_norm", "ev_w_out", "od_w_in", "od_pool_w",
            "od_pool_scale", "od_sgu_norm", "od_sgu_w", "od_sgu_b", "od_w_out", "final_norm"]
_BIG = ["ffn1_w_in", "ffn1_w_out", "ffn2_w_in", "ffn2_w_out", "ev_w_in", "ev_w_out", "od_w_in", "od_w_out"]
_SMALL_SHARDED = ["ev_conv_w", "od_pool_scale", "od_sgu_norm"]


def _pad_rows(a, mult=8):
    pad = (-a.shape[0]) % mult
    return a if pad == 0 else jnp.concatenate([a, jnp.zeros((pad,) + a.shape[1:], a.dtype)], axis=0)


def _join_cols(g):
    return g.transpose(1, 0, 2).reshape(g.shape[1], N_CHIPS * g.shape[2])


def _split_cols(w):
    return w.reshape(w.shape[0], N_CHIPS, w.shape[1] // N_CHIPS).transpose(1, 0, 2)


def kernel(x, ffn1_norm, ffn1_w_in, ffn1_w_out, mix_norm, ffn2_norm, ffn2_w_in, ffn2_w_out, ev_w_in, ev_conv_w,
           ev_q_norm, ev_k_norm, ev_w_out, od_w_in, od_pool_w, od_pool_scale, od_sgu_norm, od_sgu_w, od_sgu_b,
           od_w_out, final_norm, loss_target, m_ffn1_norm, m_ffn1_w_in, m_ffn1_w_out, m_mix_norm, m_ffn2_norm,
           m_ffn2_w_in, m_ffn2_w_out, m_ev_w_in, m_ev_conv_w, m_ev_q_norm, m_ev_k_norm, m_ev_w_out, m_od_w_in,
           m_od_pool_w, m_od_pool_scale, m_od_sgu_norm, m_od_sgu_w, m_od_sgu_b, m_od_w_out, m_final_norm, v_ffn1_norm,
           v_ffn1_w_in, v_ffn1_w_out, v_mix_norm, v_ffn2_norm, v_ffn2_w_in, v_ffn2_w_out, v_ev_w_in, v_ev_conv_w,
           v_ev_q_norm, v_ev_k_norm, v_ev_w_out, v_od_w_in, v_od_pool_w, v_od_pool_scale, v_od_sgu_norm, v_od_sgu_w,
           v_od_sgu_b, v_od_w_out, v_final_norm):
    return _step(x, ffn1_norm, ffn1_w_in, ffn1_w_out, mix_norm, ffn2_norm, ffn2_w_in, ffn2_w_out, ev_w_in, ev_conv_w,
                 ev_q_norm, ev_k_norm, ev_w_out, od_w_in, od_pool_w, od_pool_scale, od_sgu_norm, od_sgu_w, od_sgu_b,
                 od_w_out, final_norm, loss_target, m_ffn1_norm, m_ffn1_w_in, m_ffn1_w_out, m_mix_norm, m_ffn2_norm,
                 m_ffn2_w_in, m_ffn2_w_out, m_ev_w_in, m_ev_conv_w, m_ev_q_norm, m_ev_k_norm, m_ev_w_out, m_od_w_in,
                 m_od_pool_w, m_od_pool_scale, m_od_sgu_norm, m_od_sgu_w, m_od_sgu_b, m_od_w_out, m_final_norm,
                 v_ffn1_norm, v_ffn1_w_in, v_ffn1_w_out, v_mix_norm, v_ffn2_norm, v_ffn2_w_in, v_ffn2_w_out,
                 v_ev_w_in, v_ev_conv_w, v_ev_q_norm, v_ev_k_norm, v_ev_w_out, v_od_w_in, v_od_pool_w,
                 v_od_pool_scale, v_od_sgu_norm, v_od_sgu_w, v_od_sgu_b, v_od_w_out, v_final_norm)


def _step(*args):
    nw = len(_WEIGHTS)
    x = args[0]
    w = dict(zip(_WEIGHTS, args[1:1 + nw]))
    target = args[1 + nw]
    m = dict(zip(_WEIGHTS, args[2 + nw:2 + 2 * nw]))
    v = dict(zip(_WEIGHTS, args[2 + 2 * nw:2 + 3 * nw]))
    depth = w["ffn1_norm"].shape[0]
    n_even, n_odd = w["ev_w_in"].shape[0], w["od_w_in"].shape[0]
    chip = 2 * lax.axis_index("x") + lax.axis_index("y")
    place = jnp.stack([chip, lax.axis_index("c")]).astype(jnp.int32)
    core = place[1:2]

    def sharded(l, block):
        if block == "mix":
            block = "ev" if l % 2 == 0 else "od"
            return [(block + "_w_in", l // 2), (block + "_w_out", l // 2)]
        return [(block + "_w_in", l), (block + "_w_out", l)]

    def shards(group):
        return [w[n][i].astype(_ACT) for l, block in group for n, i in sharded(l, block)]

    small_rows = [w["ev_conv_w"].reshape(3 * n_even, LANES), w["od_pool_scale"], w["od_sgu_norm"]]
    first, small = _gather_shards(shards([(0, "ffn1")]), _pad_rows(jnp.concatenate(small_rows, axis=0)))
    conv_w = small[:, :3 * n_even].reshape(N_CHIPS, n_even, 3, LANES).transpose(1, 2, 0, 3).reshape(n_even, 3, CONV_WIDTH)
    pool_scale = small[:, 3 * n_even:3 * n_even + n_odd].transpose(1, 0, 2).reshape(n_odd, HALF)
    sgu_norm = small[:, 3 * n_even + n_odd:3 * n_even + 2 * n_odd].transpose(1, 0, 2).reshape(n_odd, HALF)
    later = [[(0, "mix"), (0, "ffn2")]] + [[(l, "ffn1"), (l, "mix"), (l, "ffn2")] for l in range(1, depth)]
    gathering, after = [], small
    for i, group in enumerate(later):
        gathering.append(_split_start(f"gather_start{i}", _gather_plan, N_CHIPS, shards(group), after))
        after = gathering[-1][4]
    gathered = {(0, "ffn1"): first}

    def rows(g):
        return g.reshape(N_CHIPS * g.shape[1], g.shape[2])

    def weights_of(l, block, x_in):
        zero = after[0, 0] if (l, block) == (0, "ffn1") else 0.0
        if (l, block) not in gathered:
            i = next(i for i, group in enumerate(later) if (l, block) in group)
            got = _pass_halves(f"pass_halves{i}", _split_wait(f"gather_wait{i}", _gather_plan, gathering[i][:4], x_in)[1])
            for n, key in enumerate(later[i]):
                gathered[key] = got[2 * n:2 * n + 2]
        w_in, w_out = gathered[(l, block)]
        if block != "mix":
            return dict(norm=w[block + "_norm"][l] + zero, w_in4=w_in, w_out=rows(w_out))
        j = l // 2
        if l % 2 == 0:
            mix = dict(conv_w=conv_w[j], q_gain=w["ev_q_norm"][j], k_gain=w["ev_k_norm"][j])
        else:
            mix = dict(pool_w=w["od_pool_w"][j], pool_scale=pool_scale[j], sgu_norm=sgu_norm[j],
                       sgu_w=w["od_sgu_w"][j], sgu_b=w["od_sgu_b"][j])
        return dict(mix, norm=w["mix_norm"][l], w_in=_join_cols(w_in), w_out=rows(w_out))

    def by_chip(dw):
        return dw.reshape(N_CHIPS, dw.shape[0] // N_CHIPS, dw.shape[1])

    bufs = {n: lax.empty(w[n].shape, F32) for n in _BIG}
    small_grads = {n: [None] * w[n].shape[0] for n in _WEIGHTS if n not in _BIG and n != "final_norm"}
    scattering, group = [], []

    def finish_scatter(after):
        tag, names, started = scattering.pop()
        halves, got = _split_wait(f"scatter_wait{tag}", _scatter_plan, started[:4], after)
        for i, (n, j) in enumerate(names):
            bufs[n] = _add_chips(f"add_chips{tag}_{n}", halves[i], got[i], place, bufs[n], j)
        joined = _join_halves(f"join_halves{tag}", [bufs[n] for n, _ in names], [j for _, j in names])
        for (n, _), b in zip(names, joined):
            bufs[n] = b

    def grads_done(l, block, g, dx):
        j = l // 2
        if block == "mix":
            local = [_split_cols(g["w_in"]), by_chip(g["w_out"])]
            small_grads["mix_norm"][l] = g["norm"]
            renamed = (dict(conv_w="ev_conv_w", q_gain="ev_q_norm", k_gain="ev_k_norm") if l % 2 == 0 else
                       dict(pool_w="od_pool_w", pool_scale="od_pool_scale", sgu_norm="od_sgu_norm", sgu_w="od_sgu_w",
                            sgu_b="od_sgu_b"))
            for key, n in renamed.items():
                small_grads[n][j] = g[key]
        else:
            local = [g["w_in4"], by_chip(g["w_out"])]
            small_grads[block + "_norm"][l] = g["norm"]
        group.extend(zip(sharded(l, block), local))
        if block == "ffn2" or (block == "mix" and l > 0):
            return 0.0
        if scattering:
            finish_scatter(dx)
        tag = f"{l}_{block}"
        names, local = [n for n, _ in group], [a for _, a in group]
        group.clear()
        from_sibling = _swap_halves(f"swap_halves{tag}", local)
        halves = [_add_sibling(f"add_sibling{tag}_{n}", a, b, core) for (n, _), a, b in zip(names, local, from_sibling)]
        scattering.append((tag, names, _split_start(f"scatter_start{tag}", _scatter_plan, N_CHIPS - 1, halves, dx)))
        return scattering[-1][2][4][0, 0]

    loss_part, grad_x, dfinal = _local_step(x, target, depth, weights_of, w["final_norm"], grads_done)
    loss = lax.psum(loss_part, ("x", "y", "c"))

    updates = {}
    behind = scattering[-1][2][4]
    for n in _BIG:
        if not n.startswith("ffn1"):
            updates[n] = _adamw("adamw_" + n, w[n], bufs[n], m[n], v[n], behind)
            behind = updates[n][1]
    small_grads = {n: jnp.stack(parts) for n, parts in small_grads.items()}
    small_grads["final_norm"] = dfinal
    names = list(small_grads)
    flat = jnp.concatenate([small_grads[n].reshape(-1) for n in names])
    total = flat.shape[0]
    flat = jnp.concatenate([flat, jnp.zeros((-total) % (8 * LANES), F32)])
    summed = _allreduce_small(flat.reshape(-1, LANES), behind).reshape(-1)
    finish_scatter(summed)
    grads = dict(bufs)
    off = 0
    for n in names:
        size = small_grads[n].size
        full_grad = summed[off:off + size].reshape(small_grads[n].shape)
        off += size
        if n in _SMALL_SHARDED:
            full_grad = lax.dynamic_slice_in_dim(full_grad, chip * LANES, LANES, axis=full_grad.ndim - 1)
        grads[n] = full_grad
    for n in _WEIGHTS:
        if n not in updates:
            updates[n] = _adamw("adamw_" + n, w[n], grads[n], m[n], v[n])
    return (loss, grad_x, *[updates[n][3] for n in _WEIGHTS], *[updates[n][0] for n in _WEIGHTS],
            *[updates[n][1] for n in _WEIGHTS], *[updates[n][2] for n in _WEIGHTS])
```

```python
import jax
import jax.numpy as jnp
from jax import lax
from jax.experimental import pallas as pl
from jax.experimental.pallas import tpu as pltpu

F32 = jnp.float32
_MXU = jnp.bfloat16
_ACT = jnp.bfloat16

D_MODEL = 1024
GRID_W = 64
HEAD_DIM = 64
N_Q_HEADS = 8
N_KV_HEADS = 2
Q_PER_KV = N_Q_HEADS // N_KV_HEADS
ATTN_WIDTH = N_Q_HEADS * HEAD_DIM
KV_WIDTH = N_KV_HEADS * HEAD_DIM
ROPE_THETA = 10000.0
CONV_WIDTH = D_MODEL // 2
POOL_RADII = (1, 2, 4, 8)
POOL_GROUP = 128
SGU_GROUP = 128
SGU_CHUNK = 128
N_GROUPS = 4
HALF = D_MODEL // 2
EPS = 1e-6
HALO = 8
LANES = 128
N_CHIPS = 4
N_DEV = 8

ADAM_LR = 0.001
ADAM_B1 = 0.9
ADAM_B2 = 0.999
ADAM_EPS = 1e-08
ADAM_WD = 0.01
ADAM_STEP = 10

_VMEM_LIMIT = 56 * 2 ** 20
_MESH = pl.DeviceIdType.MESH
_ANY = pl.BlockSpec(memory_space=pl.ANY)
_VMEM = pl.BlockSpec(memory_space=pltpu.VMEM)

_DN = {
    "nn": (((1,), (0,)), ((), ())),
    "nt": (((1,), (1,)), ((), ())),
    "tn": (((0,), (0,)), ((), ())),
}


def _params(*sem):
    return pltpu.CompilerParams(dimension_semantics=sem, vmem_limit_bytes=_VMEM_LIMIT)


def _tile(n, cap):
    best = None
    d = LANES
    while d <= min(n, cap):
        if n % d == 0:
            best = d
        d += LANES
    return best if best is not None else n


def _dot(a, b, mode="nn"):
    return lax.dot_general(a.astype(_MXU), b.astype(_MXU), _DN[mode], preferred_element_type=F32)


def _cat(*vals):
    vals = [v.astype(_MXU) for v in vals]
    return vals[0] if len(vals) == 1 else jnp.concatenate(vals, axis=1)


def _sigmoid(g):
    return 1.0 / (1.0 + jnp.exp(-g))


def _norm_rows(x, g):
    r = lax.rsqrt(jnp.mean(x * x, axis=-1, keepdims=True) + EPS)
    return (x * r) * g


def _swiglu(g, u):
    return (g * _sigmoid(g)) * u


_GELU_C = 0.7978845608028654


def _gelu(x):
    return 0.5 * x * (1.0 + jnp.tanh(_GELU_C * (x + 0.044715 * (x * x * x))))


def _gelu_grad(x):
    t = jnp.tanh(_GELU_C * (x + 0.044715 * (x * x * x)))
    return 0.5 * (1.0 + t) + 0.5 * x * (1.0 - t * t) * (_GELU_C * (1.0 + 3.0 * 0.044715 * (x * x)))


def _mm(name, grid, mode, a_ops, b_ops, e_ops, out_shape, out_specs, acc_shape, a_fn=_cat, b_fn=_cat, epi=None,
        n_outer=False, m_carried=False, b_pick=None):
    ni, nj, nk = grid
    na, nb, ne = len(a_ops), len(b_ops), len(e_ops)
    multi = isinstance(out_shape, (list, tuple))
    no = len(out_shape) if multi else 1

    def body(*refs):
        a_refs = refs[:na]
        b_refs = refs[na:na + nb]
        e_refs = refs[na + nb:na + nb + ne]
        o_refs = refs[na + nb + ne:na + nb + ne + no]
        a = a_fn(*[r[...] for r in a_refs])
        if b_pick is None:
            b = b_fn(*[r[...] for r in b_refs])
        else:
            b = b_pick(b_refs, pl.program_id(1), pl.program_id(2))
        p = _dot(a, b, mode)

        def finish(acc):
            if epi is None:
                o_refs[0][...] = acc.astype(o_refs[0].dtype)
            else:
                epi(acc, [r[...] for r in e_refs], o_refs)

        if nk == 1:
            finish(p)
        else:
            acc_ref = refs[-1]
            k = pl.program_id(2)

            @pl.when(k == 0)
            def _():
                acc_ref[...] = p

            @pl.when((k > 0) & (k < nk - 1))
            def _():
                acc_ref[...] += p

            @pl.when(k == nk - 1)
            def _():
                finish(acc_ref[...] + p)

    ops = list(a_ops) + list(b_ops) + list(e_ops)
    if n_outer:
        def flip(spec):
            return pl.BlockSpec(spec.block_shape, lambda j, i, k, f=spec.index_map: f(i, j, k))

        grid = (nj, ni, nk)
        ops = [(a, flip(s)) for a, s in ops]
        out_specs = [flip(s) for s in out_specs] if multi else flip(out_specs)
    return pl.pallas_call(
        body, name=name, grid=grid,
        in_specs=[s for _, s in ops],
        out_specs=out_specs, out_shape=out_shape,
        scratch_shapes=[pltpu.VMEM(acc_shape, F32)] if nk > 1 else [],
        compiler_params=_params(*(("arbitrary",) * 3 if m_carried else ("parallel", "parallel", "arbitrary"))),
    )(*[a for a, _ in ops])


def _whole(a):
    return pl.BlockSpec(a.shape, lambda i, j, k: (0,) * a.ndim, pipeline_mode=pl.Buffered(1))


def _norm_bwd_epi(acc, e, o):
    xf, dres, g = e
    r = lax.rsqrt(jnp.mean(xf * xf, axis=-1, keepdims=True) + EPS)
    xhat = xf * r
    dgx = acc * g
    m = jnp.mean(dgx * xhat, axis=-1, keepdims=True)
    o[0][...] = dres + r * (dgx - xhat * m)
    part = jnp.sum(acc * xhat, axis=0, keepdims=True)
    i = pl.program_id(0)

    @pl.when(i == 0)
    def _():
        o[1][...] = part

    @pl.when(i > 0)
    def _():
        o[1][...] += part


def _norm_bwd_ops(x, dres, gain, tm):
    t, d = x.shape
    row = pl.BlockSpec((tm, d), lambda i, j, k: (i, 0))
    vec = pl.BlockSpec((1, d), lambda i, j, k: (0, 0))
    return ([(x, row), (dres, row), (gain.reshape(1, d), vec)],
            [jax.ShapeDtypeStruct((t, d), F32), jax.ShapeDtypeStruct((1, d), F32)], [row, vec])


def _rows(t):
    return _tile(t, 512)


def _rmsnorm_fwd(name, x, gain):
    t, d = x.shape
    tr = _rows(t)

    def body(x_ref, g_ref, h_ref):
        xf = x_ref[...]
        r = lax.rsqrt(jnp.mean(xf * xf, axis=-1, keepdims=True) + EPS)
        h_ref[...] = ((xf * r) * g_ref[...]).astype(h_ref.dtype)

    return pl.pallas_call(
        body, name=name, grid=(t // tr,),
        in_specs=[pl.BlockSpec((tr, d), lambda i: (i, 0)), pl.BlockSpec((1, d), lambda i: (0, 0))],
        out_specs=pl.BlockSpec((tr, d), lambda i: (i, 0)),
        out_shape=jax.ShapeDtypeStruct((t, d), _ACT),
        compiler_params=_params("parallel"),
    )(x, gain.reshape(1, d))


def _rmsnorm_bwd(name, dh, x, gain, dres):
    t, d = x.shape
    tr = _rows(t)

    def body(dh_ref, x_ref, g_ref, dres_ref, dx_ref, dg_ref):
        i = pl.program_id(0)
        xf = x_ref[...]
        r = lax.rsqrt(jnp.mean(xf * xf, axis=-1, keepdims=True) + EPS)
        xhat = xf * r
        dy = dh_ref[...].astype(F32)
        dgx = dy * g_ref[...]
        m = jnp.mean(dgx * xhat, axis=-1, keepdims=True)
        dx_ref[...] = dres_ref[...] + r * (dgx - xhat * m)
        part = jnp.sum(dy * xhat, axis=0, keepdims=True)

        @pl.when(i == 0)
        def _():
            dg_ref[...] = part

        @pl.when(i > 0)
        def _():
            dg_ref[...] += part

    row = pl.BlockSpec((tr, d), lambda i: (i, 0))
    vec = pl.BlockSpec((1, d), lambda i: (0, 0))
    dx, dg = pl.pallas_call(
        body, name=name, grid=(t // tr,),
        in_specs=[row, row, vec, row],
        out_specs=(row, vec),
        out_shape=(jax.ShapeDtypeStruct((t, d), F32), jax.ShapeDtypeStruct((1, d), F32)),
        compiler_params=_params("arbitrary"),
    )(dh, x, gain.reshape(1, d), dres)
    return dx, dg.reshape(d)


def _final_loss(name, x, gain, target):
    t, d = x.shape
    tr = _rows(t)

    def body(x_ref, g_ref, t_ref, dx_ref, dg_ref, loss_ref):
        i = pl.program_id(0)
        xf = x_ref[...]
        r = lax.rsqrt(jnp.mean(xf * xf, axis=-1, keepdims=True) + EPS)
        xhat = xf * r
        g = g_ref[...]
        err = xhat * g - t_ref[...]
        lpart = 0.5 * jnp.sum(jnp.mean(err * err, axis=-1, keepdims=True), axis=0, keepdims=True)
        dy = err * (1.0 / d)
        dgx = dy * g
        m = jnp.mean(dgx * xhat, axis=-1, keepdims=True)
        dx_ref[...] = r * (dgx - xhat * m)
        part = jnp.sum(dy * xhat, axis=0, keepdims=True)
        lrow = jnp.broadcast_to(lpart, (1, LANES))

        @pl.when(i == 0)
        def _():
            dg_ref[...] = part
            loss_ref[...] = lrow

        @pl.when(i > 0)
        def _():
            dg_ref[...] += part
            loss_ref[...] += lrow

    row = pl.BlockSpec((tr, d), lambda i: (i, 0))
    vec = pl.BlockSpec((1, d), lambda i: (0, 0))
    dx, dg, loss = pl.pallas_call(
        body, name=name, grid=(t // tr,),
        in_specs=[row, vec, row],
        out_specs=(row, vec, pl.BlockSpec((1, LANES), lambda i: (0, 0))),
        out_shape=(jax.ShapeDtypeStruct((t, d), F32), jax.ShapeDtypeStruct((1, d), F32),
                   jax.ShapeDtypeStruct((1, LANES), F32)),
        compiler_params=_params("arbitrary"),
    )(x, gain.reshape(1, d), target)
    return loss[0, 0], dx, dg.reshape(d)


_FFN_TILES = dict(in_tm=1024, in_n_outer=False, out_tm=512, dact_tm=512, dwout_tk=1024, dh_tm=512, dwin_tk=2048)


def _ffn_tiles(layer, which):
    return _FFN_TILES


def _ffn_fwd(tag, x, gain, w_in4, w_out, cfg):
    t, d = x.shape
    fs = w_in4.shape[2]
    f = 2 * fs
    tm = _tile(t, cfg["in_tm"])
    gain = gain.reshape(1, d)
    gu = _mm(
        tag + "_in", (t // tm, N_CHIPS, 1), "nn",
        [(x, pl.BlockSpec((tm, d), lambda i, j, k: (i, 0))), (gain, pl.BlockSpec((1, d), lambda i, j, k: (0, 0)))],
        [(w_in4, _whole(w_in4))], [],
        jax.ShapeDtypeStruct((2, t, f), _ACT),
        pl.BlockSpec((None, tm, fs), lambda i, j, k: (j // 2, i, j % 2)), None, a_fn=_norm_rows,
        b_pick=lambda b, j, k: b[0][j])
    tm2 = _tile(t, cfg["out_tm"])

    def epi(acc, e, o):
        o[0][...] = e[0] + 0.5 * acc

    x_out = _mm(
        tag + "_out", (t // tm2, 1, 1), "nn",
        [(gu, pl.BlockSpec((None, tm2, f), lambda i, j, k: (0, i, 0))),
         (gu, pl.BlockSpec((None, tm2, f), lambda i, j, k: (1, i, 0)))],
        [(w_out, pl.BlockSpec((f, d), lambda i, j, k: (0, 0)))],
        [(x, pl.BlockSpec((tm2, d), lambda i, j, k: (i, 0)))],
        jax.ShapeDtypeStruct((t, d), F32),
        pl.BlockSpec((tm2, d), lambda i, j, k: (i, 0)), None,
        a_fn=_swiglu, epi=epi)
    return x_out, (x, gu)


def _ffn_bwd(tag, dxo, saved, gain, w_in4, w_out, cfg):
    x, gu = saved
    t, d = x.shape
    fs = w_in4.shape[2]
    f = 2 * fs
    tm = _tile(t, cfg["dact_tm"])
    tk = _tile(t, cfg["dwout_tk"])

    def epi_act(acc, e, o):
        g, u = e
        da = (0.5 * acc).astype(g.dtype)
        sig = _sigmoid(g)
        silu = g * sig
        o[0][0] = (da * u * (sig + silu * (1.0 - sig))).astype(o[0].dtype)
        o[0][1] = (da * silu).astype(o[0].dtype)

    dgu = _mm(
        tag + "_dact", (t // tm, 2, 1), "nt",
        [(dxo, pl.BlockSpec((tm, d), lambda i, j, k: (i, 0)))],
        [(w_out, _whole(w_out))],
        [(gu, pl.BlockSpec((None, tm, fs), lambda i, j, k: (0, i, j))),
         (gu, pl.BlockSpec((None, tm, fs), lambda i, j, k: (1, i, j)))],
        jax.ShapeDtypeStruct((2, t, f), _ACT),
        pl.BlockSpec((2, tm, fs), lambda i, j, k: (0, i, j)), None, epi=epi_act,
        b_pick=lambda b, j, k: b[0][pl.ds(pl.multiple_of(j * fs, LANES), fs), :])

    def epi_half(acc, e, o):
        o[0][...] = (0.5 * acc).astype(o[0].dtype)

    dw_out = _mm(
        tag + "_dwout", (2, 1, t // tk), "tn",
        [(gu, pl.BlockSpec((None, tk, fs), lambda i, j, k: (0, k, i))),
         (gu, pl.BlockSpec((None, tk, fs), lambda i, j, k: (1, k, i)))],
        [(dxo, pl.BlockSpec((tk, d), lambda i, j, k: (k, 0)))], [],
        jax.ShapeDtypeStruct((f, d), _ACT),
        pl.BlockSpec((fs, d), lambda i, j, k: (i, 0)), (fs, d),
        a_fn=_swiglu, epi=epi_half)
    tm = _tile(t, cfg["dh_tm"])
    e_ops, shapes, specs = _norm_bwd_ops(x, dxo, gain, tm)
    dx, dgain = _mm(
        tag + "_dh", (t // tm, 1, 2), "nt",
        [(dgu, pl.BlockSpec((None, tm, f), lambda i, j, k: (k, i, 0)))],
        [(w_in4, _whole(w_in4))], e_ops, shapes, specs, (tm, d), epi=_norm_bwd_epi, m_carried=True,
        b_pick=lambda b, j, k: jnp.concatenate([b[0][2 * k], b[0][2 * k + 1]], axis=1))
    tk = _tile(t, cfg["dwin_tk"])
    dw_in4 = _mm(
        tag + "_dwin", (1, N_CHIPS, t // tk), "tn",
        [(x, pl.BlockSpec((tk, d), lambda i, j, k: (k, 0))),
         (gain.reshape(1, d), pl.BlockSpec((1, d), lambda i, j, k: (0, 0)))],
        [(dgu, pl.BlockSpec((None, tk, fs), lambda i, j, k: (j // 2, k, j % 2)))], [],
        jax.ShapeDtypeStruct((N_CHIPS, d, fs), _ACT),
        pl.BlockSpec((None, d, fs), lambda i, j, k: (j, 0, 0)), (d, fs), a_fn=_norm_rows)
    return dx, dgain.reshape(d), dw_in4, dw_out


_MIX_TILES = dict(tm=1024, dwout_tk=2048, dwin_tk=1024)


def _proj_in(tag, h, w_in):
    t, d = h.shape
    n = w_in.shape[1]
    tm = _tile(t, _MIX_TILES["tm"])
    return _mm(
        tag + "_in", (t // tm, 1, 1), "nn",
        [(h, pl.BlockSpec((tm, d), lambda i, j, k: (i, 0)))],
        [(w_in, pl.BlockSpec((d, n), lambda i, j, k: (0, 0)))], [],
        jax.ShapeDtypeStruct((t, n), _ACT),
        pl.BlockSpec((tm, n), lambda i, j, k: (i, 0)), None)


def _proj_out(tag, x, parts, w_out):
    t, d = x.shape
    tm = _tile(t, _MIX_TILES["tm"])

    def epi(acc, e, o):
        o[0][...] = e[0] + acc

    return _mm(
        tag + "_out", (t // tm, 1, 1), "nn",
        [(p, pl.BlockSpec((tm, p.shape[1]), lambda i, j, k: (i, 0))) for p in parts],
        [(w_out, pl.BlockSpec(w_out.shape, lambda i, j, k: (0, 0)))],
        [(x, pl.BlockSpec((tm, d), lambda i, j, k: (i, 0)))],
        jax.ShapeDtypeStruct((t, d), F32),
        pl.BlockSpec((tm, d), lambda i, j, k: (i, 0)), None, epi=epi)


def _proj_out_bwd(tag, dxo, parts, w_out):
    t, d = dxo.shape
    mix = w_out.shape[0]
    tm = _tile(t, _MIX_TILES["tm"])
    tk = _tile(t, _MIX_TILES["dwout_tk"])
    d_mix = _mm(
        tag + "_dmix", (t // tm, 1, 1), "nt",
        [(dxo, pl.BlockSpec((tm, d), lambda i, j, k: (i, 0)))],
        [(w_out, pl.BlockSpec((mix, d), lambda i, j, k: (0, 0)))], [],
        jax.ShapeDtypeStruct((t, mix), F32),
        pl.BlockSpec((tm, mix), lambda i, j, k: (i, 0)), None)
    dw_out = _mm(
        tag + "_dwout", (1, 1, t // tk), "tn",
        [(p, pl.BlockSpec((tk, p.shape[1]), lambda i, j, k: (k, 0))) for p in parts],
        [(dxo, pl.BlockSpec((tk, d), lambda i, j, k: (k, 0)))], [],
        jax.ShapeDtypeStruct((mix, d), _ACT),
        pl.BlockSpec((mix, d), lambda i, j, k: (0, 0)), (mix, d))
    return d_mix, dw_out


def _proj_in_bwd(tag, h, dparts, w_in, x, dres, gain):
    t, d = h.shape
    n = w_in.shape[1]
    tm = _tile(t, _MIX_TILES["tm"])
    tk = _tile(t, _MIX_TILES["dwin_tk"])
    e_ops, shapes, specs = _norm_bwd_ops(x, dres, gain, tm)
    dx, dgain = _mm(
        tag + "_dh", (t // tm, 1, 1), "nt",
        [(p, pl.BlockSpec((tm, p.shape[1]), lambda i, j, k: (i, 0))) for p in dparts],
        [(w_in, pl.BlockSpec((d, n), lambda i, j, k: (0, 0)))], e_ops, shapes, specs, None,
        epi=_norm_bwd_epi, m_carried=True)
    dw_in = _mm(
        tag + "_dwin", (1, 1, t // tk), "tn",
        [(h, pl.BlockSpec((tk, d), lambda i, j, k: (k, 0)))],
        [(p, pl.BlockSpec((tk, p.shape[1]), lambda i, j, k: (k, 0))) for p in dparts], [],
        jax.ShapeDtypeStruct((d, n), _ACT),
        pl.BlockSpec((d, n), lambda i, j, k: (0, 0)), (d, n))
    return dx, dgain.reshape(d), dw_in


def _shifted(pad_ref, val, s):
    pad_ref[pl.ds(HALO, s), :] = val
    return pad_ref[pl.ds(HALO - 1, s), :], pad_ref[pl.ds(HALO + 1, s), :]


def _zero_halo(pad_ref, s):
    z = jnp.zeros((HALO, pad_ref.shape[1]), F32)
    pad_ref[pl.ds(0, HALO), :] = z
    pad_ref[pl.ds(HALO + s, HALO), :] = z


def _conv_fwd(tag, proj, conv_w, nb, s):
    t = proj.shape[0]
    ncb = CONV_WIDTH // LANES

    def body(gb_ref, gc_ref, hc_ref, w_ref, a_ref, pad_ref):
        _zero_halo(pad_ref, s)
        cg = gc_ref[...].astype(F32) * hc_ref[...].astype(F32)
        prev, nxt = _shifted(pad_ref, cg, s)
        w = w_ref[...]
        conv = prev * w[0:1, :] + cg * w[1:2, :] + nxt * w[2:3, :]
        a_ref[...] = (gb_ref[...].astype(F32) * conv).astype(a_ref.dtype)

    def col(off):
        return pl.BlockSpec((s, LANES), lambda b, c: (b, off + c))

    return pl.pallas_call(
        body, name=tag + "_conv", grid=(nb, ncb),
        in_specs=[col(0), col(ncb), col(2 * ncb), pl.BlockSpec((3, LANES), lambda b, c: (0, c))],
        out_specs=col(0),
        out_shape=jax.ShapeDtypeStruct((t, CONV_WIDTH), _ACT),
        scratch_shapes=[pltpu.VMEM((s + 2 * HALO, LANES), F32)],
        compiler_params=_params("parallel", "parallel"),
    )(proj, proj, proj, conv_w)


def _conv_bwd(tag, proj, conv_w, d_mix, nb, s):
    t = proj.shape[0]
    ncb = CONV_WIDTH // LANES

    def body(gb_ref, gc_ref, hc_ref, w_ref, da_ref, dgb_ref, dgc_ref, dhc_ref, dw_ref, pad_ref):
        b = pl.program_id(1)
        _zero_halo(pad_ref, s)
        gb = gb_ref[...].astype(F32)
        gc = gc_ref[...].astype(F32)
        hc = hc_ref[...].astype(F32)
        w = w_ref[...]
        da = da_ref[...]
        cg = gc * hc
        prev, nxt = _shifted(pad_ref, cg, s)
        conv = prev * w[0:1, :] + cg * w[1:2, :] + nxt * w[2:3, :]
        dgb_ref[...] = (da * conv).astype(dgb_ref.dtype)
        dconv = da * gb
        dw = jnp.concatenate([
            jnp.sum(dconv * prev, axis=0, keepdims=True),
            jnp.sum(dconv * cg, axis=0, keepdims=True),
            jnp.sum(dconv * nxt, axis=0, keepdims=True)], axis=0)
        dprev, dnxt = _shifted(pad_ref, dconv, s)
        dcg = dnxt * w[0:1, :] + dconv * w[1:2, :] + dprev * w[2:3, :]
        dgc_ref[...] = (dcg * hc).astype(dgc_ref.dtype)
        dhc_ref[...] = (dcg * gc).astype(dhc_ref.dtype)

        @pl.when(b == 0)
        def _():
            dw_ref[...] = dw

        @pl.when(b > 0)
        def _():
            dw_ref[...] += dw

    def col(off):
        return pl.BlockSpec((s, LANES), lambda c, b: (b, off + c))

    wspec = pl.BlockSpec((3, LANES), lambda c, b: (0, c))
    act = jax.ShapeDtypeStruct((t, CONV_WIDTH), _ACT)
    return pl.pallas_call(
        body, name=tag + "_dconv", grid=(ncb, nb),
        in_specs=[col(0), col(ncb), col(2 * ncb), wspec, col(0)],
        out_specs=(col(0), col(0), col(0), wspec),
        out_shape=(act, act, act, jax.ShapeDtypeStruct((3, CONV_WIDTH), F32)),
        scratch_shapes=[pltpu.VMEM((s + 2 * HALO, LANES), F32)],
        compiler_params=_params("parallel", "arbitrary"),
    )(proj, proj, proj, conv_w, d_mix)


def _rope_tables(s):
    rows = s // GRID_W
    r_idx, c_idx = jnp.meshgrid(jnp.arange(rows), jnp.arange(GRID_W), indexing="ij")
    r_idx = r_idx.reshape(-1).astype(F32)
    c_idx = c_idx.reshape(-1).astype(F32)
    n_freq = HEAD_DIM // 4
    inv = ROPE_THETA ** (-jnp.arange(n_freq, dtype=F32) / n_freq)
    ang = jnp.concatenate([r_idx[:, None] * inv, c_idx[:, None] * inv], axis=-1)
    cos = jnp.repeat(jnp.cos(ang), 2, axis=1)
    sin = jnp.repeat(jnp.sin(ang), 2, axis=1)
    sign = jnp.where(jnp.arange(HEAD_DIM) % 2 == 0, -1.0, 1.0).astype(F32)
    return jnp.tile(cos, (1, LANES // HEAD_DIM)), jnp.tile(sin * sign, (1, LANES // HEAD_DIM))


def _head_ones():
    i = jnp.arange(LANES) // HEAD_DIM
    return (i[:, None] == i[None, :]).astype(jnp.bfloat16)


def _head_sum(v, ones):
    outs = []
    for j in range(v.shape[1] // LANES):
        c = v[:, j * LANES:(j + 1) * LANES]
        hi = c.astype(jnp.bfloat16)
        lo = (c - hi.astype(F32)).astype(jnp.bfloat16)
        outs.append(jnp.dot(hi, ones, preferred_element_type=F32) + jnp.dot(lo, ones, preferred_element_type=F32))
    return outs[0] if len(outs) == 1 else jnp.concatenate(outs, axis=1)


def _pair_swap(v):
    outs = []
    for j in range(v.shape[1] // LANES):
        c = v[:, j * LANES:(j + 1) * LANES]
        lane = lax.broadcasted_iota(jnp.int32, c.shape, 1)
        outs.append(jnp.where(lane % 2 == 0, pltpu.roll(c, LANES - 1, 1), pltpu.roll(c, 1, 1)))
    return outs[0] if len(outs) == 1 else jnp.concatenate(outs, axis=1)


def _wide(tab, width):
    return tab if width == LANES else jnp.concatenate([tab] * (width // LANES), axis=1)


_QK_SCALE = HEAD_DIM ** -0.5


def _qk_fwd(tag, proj, q_gain, k_gain, cos, sin, nb, s):
    t = proj.shape[0]
    tr = _tile(s, 512)
    ns = s // tr
    q_off = 3 * CONV_WIDTH // ATTN_WIDTH
    k_off = (3 * CONV_WIDTH + ATTN_WIDTH) // KV_WIDTH

    def body(q_ref, k_ref, qg_ref, kg_ref, cos_ref, sin_ref, ones_ref, qo_ref, ko_ref):
        ones = ones_ref[...]
        for src, g_ref, dst, mult in ((q_ref, qg_ref, qo_ref, _QK_SCALE), (k_ref, kg_ref, ko_ref, 1.0)):
            v = src[...].astype(F32)
            w = v.shape[1]
            r = lax.rsqrt(_head_sum(v * v, ones) * (1.0 / HEAD_DIM) + EPS)
            vn = (v * r) * g_ref[...]
            rot = vn * _wide(cos_ref[...], w) + _pair_swap(vn) * _wide(sin_ref[...], w)
            dst[...] = (rot * mult).astype(dst.dtype)

    tab = pl.BlockSpec((tr, LANES), lambda i: (i % ns, 0))
    return pl.pallas_call(
        body, name=tag + "_qk", grid=(t // tr,),
        in_specs=[pl.BlockSpec((tr, ATTN_WIDTH), lambda i: (i, q_off)),
                  pl.BlockSpec((tr, KV_WIDTH), lambda i: (i, k_off)),
                  pl.BlockSpec((1, ATTN_WIDTH), lambda i: (0, 0)),
                  pl.BlockSpec((1, KV_WIDTH), lambda i: (0, 0)),
                  tab, tab, pl.BlockSpec((LANES, LANES), lambda i: (0, 0))],
        out_specs=(pl.BlockSpec((tr, ATTN_WIDTH), lambda i: (i, 0)),
                   pl.BlockSpec((tr, KV_WIDTH), lambda i: (i, 0))),
        out_shape=(jax.ShapeDtypeStruct((t, ATTN_WIDTH), _ACT), jax.ShapeDtypeStruct((t, KV_WIDTH), _ACT)),
        compiler_params=_params("parallel"),
    )(proj, proj, jnp.tile(q_gain, N_Q_HEADS).reshape(1, ATTN_WIDTH),
      jnp.tile(k_gain, N_KV_HEADS).reshape(1, KV_WIDTH), cos, sin, _head_ones())


def _qk_bwd(tag, proj, q_gain, k_gain, cos, sin, dq_rot, dk_rot, nb, s):
    t = proj.shape[0]
    tr = _tile(s, 512)
    ns = s // tr
    q_off = 3 * CONV_WIDTH // ATTN_WIDTH
    k_off = (3 * CONV_WIDTH + ATTN_WIDTH) // KV_WIDTH

    def body(q_ref, k_ref, qg_ref, kg_ref, cos_ref, sin_ref, ones_ref, dqr_ref, dkr_ref,
             dq_ref, dk_ref, dqg_ref, dkg_ref):
        i = pl.program_id(0)
        ones = ones_ref[...]
        for src, g_ref, dr_ref, dst, dg_ref, mult in ((q_ref, qg_ref, dqr_ref, dq_ref, dqg_ref, _QK_SCALE),
                                                      (k_ref, kg_ref, dkr_ref, dk_ref, dkg_ref, 1.0)):
            v = src[...].astype(F32)
            w = v.shape[1]
            r = lax.rsqrt(_head_sum(v * v, ones) * (1.0 / HEAD_DIM) + EPS)
            xhat = v * r
            dr = dr_ref[...] * mult
            dvn = dr * _wide(cos_ref[...], w) + _pair_swap(dr * _wide(sin_ref[...], w))
            dgx = dvn * g_ref[...]
            m = _head_sum(dgx * xhat, ones) * (1.0 / HEAD_DIM)
            dst[...] = (r * (dgx - xhat * m)).astype(dst.dtype)
            part = jnp.sum(dvn * xhat, axis=0, keepdims=True)
            fold = part[:, 0:HEAD_DIM]
            for hh in range(1, w // HEAD_DIM):
                fold = fold + part[:, hh * HEAD_DIM:(hh + 1) * HEAD_DIM]

            @pl.when(i == 0)
            def _():
                dg_ref[...] = fold

            @pl.when(i > 0)
            def _():
                dg_ref[...] += fold

    tab = pl.BlockSpec((tr, LANES), lambda i: (i % ns, 0))
    qrow = pl.BlockSpec((tr, ATTN_WIDTH), lambda i: (i, 0))
    krow = pl.BlockSpec((tr, KV_WIDTH), lambda i: (i, 0))
    gvec = pl.BlockSpec((1, HEAD_DIM), lambda i: (0, 0))
    dq, dk, dqg, dkg = pl.pallas_call(
        body, name=tag + "_dqk", grid=(t // tr,),
        in_specs=[pl.BlockSpec((tr, ATTN_WIDTH), lambda i: (i, q_off)),
                  pl.BlockSpec((tr, KV_WIDTH), lambda i: (i, k_off)),
                  pl.BlockSpec((1, ATTN_WIDTH), lambda i: (0, 0)),
                  pl.BlockSpec((1, KV_WIDTH), lambda i: (0, 0)),
                  tab, tab, pl.BlockSpec((LANES, LANES), lambda i: (0, 0)), qrow, krow],
        out_specs=(qrow, krow, gvec, gvec),
        out_shape=(jax.ShapeDtypeStruct((t, ATTN_WIDTH), _ACT), jax.ShapeDtypeStruct((t, KV_WIDTH), _ACT),
                   jax.ShapeDtypeStruct((1, HEAD_DIM), F32), jax.ShapeDtypeStruct((1, HEAD_DIM), F32)),
        compiler_params=_params("arbitrary"),
    )(proj, proj, jnp.tile(q_gain, N_Q_HEADS).reshape(1, ATTN_WIDTH),
      jnp.tile(k_gain, N_KV_HEADS).reshape(1, KV_WIDTH), cos, sin, _head_ones(), dq_rot, dk_rot)
    return dq, dk, dqg.reshape(HEAD_DIM), dkg.reshape(HEAD_DIM)


def _head(v, h):
    return v[:, h * HEAD_DIM:(h + 1) * HEAD_DIM]


def _attn_fwd(tag, q, k, proj, nb, s):
    t = q.shape[0]
    tq = _tile(s, 256)
    nq = s // tq
    v_off = (3 * CONV_WIDTH + ATTN_WIDTH + KV_WIDTH) // KV_WIDTH

    def body(q_ref, k_ref, v_ref, o_ref, lse_ref):
        qv = q_ref[...]
        kv = k_ref[...]
        vv = v_ref[...]
        for h in range(N_Q_HEADS):
            j = h // Q_PER_KV
            sc = _dot(_head(qv, h), _head(kv, j), "nt")
            m = jnp.max(sc, axis=-1, keepdims=True)
            e = jnp.exp(sc - m)
            l = jnp.sum(e, axis=-1, keepdims=True)
            o = _dot(e, _head(vv, j)) * (1.0 / l)
            o_ref[:, h * HEAD_DIM:(h + 1) * HEAD_DIM] = o.astype(o_ref.dtype)
            lse_ref[:, h:h + 1] = m + jnp.log(l)

    return pl.pallas_call(
        body, name=tag + "_attn", grid=(nb, nq),
        in_specs=[pl.BlockSpec((tq, ATTN_WIDTH), lambda b, i: (b * nq + i, 0)),
                  pl.BlockSpec((s, KV_WIDTH), lambda b, i: (b, 0)),
                  pl.BlockSpec((s, KV_WIDTH), lambda b, i: (b, v_off))],
        out_specs=(pl.BlockSpec((tq, ATTN_WIDTH), lambda b, i: (b * nq + i, 0)),
                   pl.BlockSpec((tq, N_Q_HEADS), lambda b, i: (b * nq + i, 0))),
        out_shape=(jax.ShapeDtypeStruct((t, ATTN_WIDTH), _ACT), jax.ShapeDtypeStruct((t, N_Q_HEADS), F32)),
        compiler_params=_params("parallel", "parallel"),
    )(q, k, proj)


def _attn_bwd(tag, q, k, proj, o, lse, d_mix, nb, s):
    t = q.shape[0]
    tq = _tile(s, 256)
    nq = s // tq
    v_off = (3 * CONV_WIDTH + ATTN_WIDTH + KV_WIDTH) // KV_WIDTH

    def body(q_ref, k_ref, v_ref, o_ref, lse_ref, do_ref, dq_ref, dk_ref, dv_ref):
        i = pl.program_id(1)

        @pl.when(i == 0)
        def _():
            dk_ref[...] = jnp.zeros_like(dk_ref)
            dv_ref[...] = jnp.zeros_like(dv_ref)

        qv = q_ref[...]
        kv = k_ref[...]
        vv = v_ref[...]
        ov = o_ref[...].astype(F32)
        dov = do_ref[...]
        lse = lse_ref[...]
        for h in range(N_Q_HEADS):
            j = h // Q_PER_KV
            cols = slice(j * HEAD_DIM, (j + 1) * HEAD_DIM)
            qh = _head(qv, h)
            kj = _head(kv, j)
            doh = _head(dov, h)
            sc = _dot(qh, kj, "nt")
            p = jnp.exp(sc - lse[:, h:h + 1])
            dp = _dot(doh, _head(vv, j), "nt")
            delta = jnp.sum(doh * _head(ov, h), axis=-1, keepdims=True)
            ds = p * (dp - delta)
            dv_ref[:, cols] += _dot(p, doh, "tn")
            dk_ref[:, cols] += _dot(ds, qh, "tn")
            dq_ref[:, h * HEAD_DIM:(h + 1) * HEAD_DIM] = _dot(ds, kj)

    qrow = pl.BlockSpec((tq, ATTN_WIDTH), lambda b, i: (b * nq + i, 0))
    kvrow = pl.BlockSpec((s, KV_WIDTH), lambda b, i: (b, 0))
    return pl.pallas_call(
        body, name=tag + "_dattn", grid=(nb, nq),
        in_specs=[qrow, kvrow, pl.BlockSpec((s, KV_WIDTH), lambda b, i: (b, v_off)), qrow,
                  pl.BlockSpec((tq, N_Q_HEADS), lambda b, i: (b * nq + i, 0)),
                  pl.BlockSpec((tq, ATTN_WIDTH), lambda b, i: (b * nq + i, 1))],
        out_specs=(qrow, kvrow, kvrow),
        out_shape=(jax.ShapeDtypeStruct((t, ATTN_WIDTH), F32), jax.ShapeDtypeStruct((t, KV_WIDTH), F32),
                   jax.ShapeDtypeStruct((t, KV_WIDTH), F32)),
        compiler_params=_params("parallel", "arbitrary"),
    )(q, k, proj, o, lse, d_mix)


def _even_fwd(tag, x, p, cos, sin, nb, s):
    h = _rmsnorm_fwd(tag + "_norm", x, p["norm"])
    proj = _proj_in(tag, h, p["w_in"])
    a = _conv_fwd(tag, proj, p["conv_w"], nb, s)
    q, k = _qk_fwd(tag, proj, p["q_gain"], p["k_gain"], cos, sin, nb, s)
    o, lse = _attn_fwd(tag, q, k, proj, nb, s)
    x_out = _proj_out(tag, x, [a, o], p["w_out"])
    return x_out, (x, h, proj, a, q, k, o, lse)


def _even_bwd(tag, dxo, saved, p, cos, sin, nb, s):
    x, h, proj, a, q, k, o, lse = saved
    d_mix, dw_out = _proj_out_bwd(tag, dxo, [a, o], p["w_out"])
    dgb, dgc, dhc, dconv_w = _conv_bwd(tag, proj, p["conv_w"], d_mix, nb, s)
    dq_rot, dk_rot, dv = _attn_bwd(tag, q, k, proj, o, lse, d_mix, nb, s)
    dq, dk, dq_gain, dk_gain = _qk_bwd(tag, proj, p["q_gain"], p["k_gain"], cos, sin, dq_rot, dk_rot, nb, s)
    dx, dnorm, dw_in = _proj_in_bwd(tag, h, [dgb, dgc, dhc, dq, dk, dv], p["w_in"], x, dxo, p["norm"])
    grads = dict(norm=dnorm, w_in=dw_in, w_out=dw_out, conv_w=dconv_w, q_gain=dq_gain, k_gain=dk_gain)
    return dx, grads


def _window(pad_ref, val, r, s):
    pad_ref[pl.ds(HALO, s), :] = val
    acc = val
    for d in range(1, r + 1):
        acc = acc + pad_ref[pl.ds(HALO - d, s), :] + pad_ref[pl.ds(HALO + d, s), :]
    return acc


def _count(r, s):
    t = lax.broadcasted_iota(jnp.int32, (s, 1), 0)
    return (jnp.minimum(t + r, s - 1) - jnp.maximum(t - r, 0) + 1).astype(F32)


def _sgu_chunk(u_ref, v_ref, norm, ws_ref, bt, rows):
    uu = u_ref[rows, :].astype(F32)
    vv = v_ref[rows, :].astype(F32)
    gu = _gelu(uu)
    gv = _gelu(vv)
    r = lax.rsqrt(jnp.mean(gv * gv, axis=-1, keepdims=True) + EPS)
    xhat = gv * r
    vn = xhat * norm
    mixed = []
    for g in range(N_GROUPS):
        cols = slice(g * SGU_GROUP, (g + 1) * SGU_GROUP)
        mixed.append(_dot(ws_ref[g], vn[:, cols]) + bt[:, g:g + 1])
    return uu, vv, gu, r, xhat, vn, mixed


def _odd_core_fwd(tag, proj, p, nb, s):
    t = proj.shape[0]
    nchunk = s // SGU_CHUNK

    def body(p_ref, u_ref, v_ref, pw_ref, ps_ref, sn_ref, ws_ref, bt_ref, mix_ref, pad_ref):
        _zero_halo(pad_ref, s)
        for g, r in enumerate(POOL_RADII):
            cols = slice(g * POOL_GROUP, (g + 1) * POOL_GROUP)
            pg = p_ref[:, cols].astype(F32)
            pooled = _window(pad_ref, pg, r, s) / _count(r, s) - pg
            mix_ref[:, cols] = (_dot(pooled, pw_ref[g]) * ps_ref[:, cols]).astype(mix_ref.dtype)
        norm = sn_ref[...]
        bt = bt_ref[...]

        def chunk(n, carry):
            rows = pl.ds(pl.multiple_of(n * SGU_CHUNK, SGU_CHUNK), SGU_CHUNK)
            _, _, gu, _, _, _, mixed = _sgu_chunk(u_ref, v_ref, norm, ws_ref, bt, rows)
            for g in range(N_GROUPS):
                cols = slice(g * SGU_GROUP, (g + 1) * SGU_GROUP)
                mix_ref[rows, HALF + g * SGU_GROUP:HALF + (g + 1) * SGU_GROUP] = (
                    gu[:, cols] * mixed[g]).astype(mix_ref.dtype)
            return carry

        lax.fori_loop(0, nchunk, chunk, 0)

    def col(j):
        return pl.BlockSpec((s, HALF), lambda b: (b, j))

    def whole(a):
        return pl.BlockSpec(a.shape, lambda b: (0,) * a.ndim)

    consts = [p["pool_w"], p["pool_scale"].reshape(1, HALF), p["sgu_norm"].reshape(1, HALF),
              p["sgu_w"], p["sgu_b"].T]
    return pl.pallas_call(
        body, name=tag + "_core", grid=(nb,),
        in_specs=[col(0), col(1), col(2)] + [whole(a) for a in consts],
        out_specs=pl.BlockSpec((s, D_MODEL), lambda b: (b, 0)),
        out_shape=jax.ShapeDtypeStruct((t, D_MODEL), _ACT),
        scratch_shapes=[pltpu.VMEM((s + 2 * HALO, POOL_GROUP), F32)],
        compiler_params=_params("parallel"),
    )(proj, proj, proj, *consts)


def _odd_core_bwd(tag, proj, p, d_mix, nb, s):
    t = proj.shape[0]
    nchunk = s // SGU_CHUNK

    def body(p_ref, u_ref, v_ref, pw_ref, ps_ref, sn_ref, ws_ref, bt_ref, dm_ref,
             dproj_ref, dpw_ref, dps_ref, dsn_ref, dws_ref, dbt_ref, pad_ref):
        b = pl.program_id(0)

        @pl.when(b == 0)
        def _():
            dpw_ref[...] = jnp.zeros_like(dpw_ref)
            dps_ref[...] = jnp.zeros_like(dps_ref)
            dsn_ref[...] = jnp.zeros_like(dsn_ref)
            dws_ref[...] = jnp.zeros_like(dws_ref)
            dbt_ref[...] = jnp.zeros_like(dbt_ref)

        _zero_halo(pad_ref, s)
        for g, r in enumerate(POOL_RADII):
            cols = slice(g * POOL_GROUP, (g + 1) * POOL_GROUP)
            pg = p_ref[:, cols].astype(F32)
            cnt = _count(r, s)
            pooled = _window(pad_ref, pg, r, s) / cnt - pg
            c_pre = _dot(pooled, pw_ref[g])
            dc = dm_ref[:, cols]
            dps_ref[:, cols] += jnp.sum(dc * c_pre, axis=0, keepdims=True)
            dcp = dc * ps_ref[:, cols]
            dpw_ref[g] += _dot(pooled, dcp, "tn")
            dpooled = _dot(dcp, pw_ref[g], "nt")
            dproj_ref[:, cols] = (_window(pad_ref, dpooled / cnt, r, s) - dpooled).astype(dproj_ref.dtype)
        norm = sn_ref[...]
        bt = bt_ref[...]

        def chunk(n, carry):
            rows = pl.ds(pl.multiple_of(n * SGU_CHUNK, SGU_CHUNK), SGU_CHUNK)
            uu, vv, gu, r, xhat, vn, mixed = _sgu_chunk(u_ref, v_ref, norm, ws_ref, bt, rows)
            dd = dm_ref[rows, HALF:D_MODEL]
            dgu, dvn = [], []
            for g in range(N_GROUPS):
                cols = slice(g * SGU_GROUP, (g + 1) * SGU_GROUP)
                dgu.append(dd[:, cols] * mixed[g])
                dmx = dd[:, cols] * gu[:, cols]
                dbt_ref[:, g:g + 1] += jnp.sum(dmx, axis=-1, keepdims=True)
                dws_ref[g] += _dot(dmx, vn[:, cols], "nt")
                dvn.append(_dot(ws_ref[g], dmx, "tn"))
            dgu = jnp.concatenate(dgu, axis=1)
            dvn = jnp.concatenate(dvn, axis=1)
            dsn_ref[...] += jnp.sum(dvn * xhat, axis=0, keepdims=True)
            dgx = dvn * norm
            m = jnp.mean(dgx * xhat, axis=-1, keepdims=True)
            dgv = r * (dgx - xhat * m)
            dproj_ref[rows, HALF:2 * HALF] = (dgu * _gelu_grad(uu)).astype(dproj_ref.dtype)
            dproj_ref[rows, 2 * HALF:3 * HALF] = (dgv * _gelu_grad(vv)).astype(dproj_ref.dtype)
            return carry

        lax.fori_loop(0, nchunk, chunk, 0)

    def col(j):
        return pl.BlockSpec((s, HALF), lambda b: (b, j))

    def whole(a):
        return pl.BlockSpec(a.shape, lambda b: (0,) * a.ndim)

    consts = [p["pool_w"], p["pool_scale"].reshape(1, HALF), p["sgu_norm"].reshape(1, HALF),
              p["sgu_w"], p["sgu_b"].T]
    gshapes = [jax.ShapeDtypeStruct(a.shape, F32) for a in consts]
    dproj, dpw, dps, dsn, dws, dbt = pl.pallas_call(
        body, name=tag + "_dcore", grid=(nb,),
        in_specs=[col(0), col(1), col(2)] + [whole(a) for a in consts]
        + [pl.BlockSpec((s, D_MODEL), lambda b: (b, 0))],
        out_specs=[pl.BlockSpec((s, 3 * HALF), lambda b: (b, 0))] + [whole(a) for a in consts],
        out_shape=[jax.ShapeDtypeStruct((t, 3 * HALF), _ACT)] + gshapes,
        scratch_shapes=[pltpu.VMEM((s + 2 * HALO, POOL_GROUP), F32)],
        compiler_params=_params("arbitrary"),
    )(proj, proj, proj, *consts, d_mix)
    return dproj, dict(pool_w=dpw, pool_scale=dps.reshape(HALF), sgu_norm=dsn.reshape(HALF), sgu_w=dws, sgu_b=dbt.T)


def _odd_fwd(tag, x, p, nb, s):
    h = _rmsnorm_fwd(tag + "_norm", x, p["norm"])
    proj = _proj_in(tag, h, p["w_in"])
    mix = _odd_core_fwd(tag, proj, p, nb, s)
    x_out = _proj_out(tag, x, [mix], p["w_out"])
    return x_out, (x, h, proj, mix)


def _odd_bwd(tag, dxo, saved, p, nb, s):
    x, h, proj, mix = saved
    d_mix, dw_out = _proj_out_bwd(tag, dxo, [mix], p["w_out"])
    dproj, grads = _odd_core_bwd(tag, proj, p, d_mix, nb, s)
    dx, dnorm, dw_in = _proj_in_bwd(tag, h, [dproj], p["w_in"], x, dxo, p["norm"])
    grads.update(norm=dnorm, w_in=dw_in, w_out=dw_out)
    return dx, grads


def _local_step(x3, target3, depth, weights_of, final_norm, grads_done):
    nb, s, d = x3.shape
    t = nb * s
    x = x3.reshape(t, d)
    target = target3.reshape(t, d)
    cos, sin = _rope_tables(s)
    saved, ws = [], []
    for l in range(depth):
        w1 = weights_of(l, "ffn1", x)
        x, s1 = _ffn_fwd(f"l{l}_ffn1", x, w1["norm"], w1["w_in4"], w1["w_out"], _ffn_tiles(l, 1))
        wm = weights_of(l, "mix", x)
        if l % 2 == 0:
            x, s2 = _even_fwd(f"l{l}_ev", x, wm, cos, sin, nb, s)
        else:
            x, s2 = _odd_fwd(f"l{l}_od", x, wm, nb, s)
        w2 = weights_of(l, "ffn2", x)
        x, s3 = _ffn_fwd(f"l{l}_ffn2", x, w2["norm"], w2["w_in4"], w2["w_out"], _ffn_tiles(l, 2))
        saved.append((s1, s2, s3))
        ws.append((w1, wm, w2))
    loss, dx, dfinal = _final_loss("final_loss", x, final_norm, target)
    zero = 0.0
    for l in reversed(range(depth)):
        s1, s2, s3 = saved[l]
        w1, wm, w2 = ws[l]
        dx, dn, dwi, dwo = _ffn_bwd(f"l{l}_ffn2", dx, s3, w2["norm"] + zero, w2["w_in4"], w2["w_out"], _ffn_tiles(l, 2))
        zero = grads_done(l, "ffn2", dict(norm=dn, w_in4=dwi, w_out=dwo), dx)
        wm = dict(wm, norm=wm["norm"] + zero)
        if l % 2 == 0:
            dx, gm = _even_bwd(f"l{l}_ev", dx, s2, wm, cos, sin, nb, s)
        else:
            dx, gm = _odd_bwd(f"l{l}_od", dx, s2, wm, nb, s)
        zero = grads_done(l, "mix", gm, dx)
        dx, dn, dwi, dwo = _ffn_bwd(f"l{l}_ffn1", dx, s1, w1["norm"] + zero, w1["w_in4"], w1["w_out"], _ffn_tiles(l, 1))
        zero = grads_done(l, "ffn1", dict(norm=dn, w_in4=dwi, w_out=dwo), dx)
    return loss, dx.reshape(nb, s, d), dfinal


_HBM = pl.BlockSpec(memory_space=pltpu.HBM)


def _place():
    x, y, c = lax.axis_index("x"), lax.axis_index("y"), lax.axis_index("c")
    chips = [(1 - x, y), (x, 1 - y), (1 - x, 1 - y)]
    return x, y, c, chips


def _remote(src, dst, send_sem, recv_sem, to):
    return pltpu.make_async_remote_copy(src_ref=src, dst_ref=dst, send_sem=send_sem, recv_sem=recv_sem,
                                        device_id=to, device_id_type=_MESH)


def _gather_shards(arrs, small):
    n = len(arrs)
    own = 6

    def body(*refs):
        ins, sm_in = refs[:n], refs[n]
        outs, sm_out = refs[n + 1:2 * n + 1], refs[2 * n + 1]
        send, recv = refs[2 * n + 2:]
        x, y, c, chips = _place()
        k = 2 * x + y
        sib = (x, y, 1 - c)
        started = []
        for a in range(n + 1):
            src, dst = (ins[a], outs[a]) if a < n else (sm_in, sm_out)
            cp = _remote(src, dst.at[k], send.at[a, own], recv.at[a, own], sib)
            cp.start()
            started.append(cp)
            if a < n:
                h = src.shape[0] // 2
                mine = pl.ds(c * h, h)
                src_part, dst_part = src.at[mine], dst.at[k, mine]
            else:
                src_part, dst_part = src, dst.at[k]
            for j, chip in enumerate(chips):
                cp = _remote(src_part, dst_part, send.at[a, j], recv.at[a, j], (*chip, c))
                cp.start()
                started.append(cp)
        for a in range(n):
            h = ins[a].shape[0] // 2
            mine = pl.ds(c * h, h)
            for j, (px, py) in enumerate(chips):
                landed = outs[a].at[2 * px + py, mine]
                _remote(landed, landed, send.at[a, j], recv.at[a, j], (px, py, c)).wait_recv()
                cp = _remote(landed, landed, send.at[a, 3 + j], recv.at[a, 3 + j], sib)
                cp.start()
                started.append(cp)
        for a in range(n):
            h = ins[a].shape[0] // 2
            other = pl.ds((1 - c) * h, h)
            for j, (px, py) in enumerate(chips):
                passed = outs[a].at[2 * px + py, other]
                _remote(passed, passed, send.at[a, 3 + j], recv.at[a, 3 + j], sib).wait_recv()
        for j, (px, py) in enumerate(chips):
            landed = sm_out.at[2 * px + py]
            _remote(landed, landed, send.at[n, j], recv.at[n, j], (px, py, c)).wait_recv()
        for a in range(n + 1):
            filled = (outs[a] if a < n else sm_out).at[k]
            _remote(filled, filled, send.at[a, own], recv.at[a, own], sib).wait_recv()
        for cp in started:
            cp.wait_send()

    outs = pl.pallas_call(
        body, name="gather_shards",
        in_specs=[_HBM] * (n + 1), out_specs=[_HBM] * (n + 1),
        out_shape=[jax.ShapeDtypeStruct((N_CHIPS,) + a.shape, a.dtype) for a in list(arrs) + [small]],
        scratch_shapes=[pltpu.SemaphoreType.DMA((n + 1, 7)), pltpu.SemaphoreType.DMA((n + 1, 7))],
    )(*arrs, small)
    return outs[:n], outs[n]


_SEM = pl.BlockSpec(memory_space=pltpu.SEMAPHORE)
_EFFECT = pltpu.SideEffectType.DATAFLOW_SIDE_EFFECTING


def _gather_plan(i, src, land, k, c, chips, sib):
    mine = pl.ds(c * (src.shape[0] // 2), src.shape[0] // 2)
    plan = [(src.at[mine], land.at[k, mine], (px, py, c), land.at[2 * px + py, mine]) for px, py in chips]
    return plan + [(src, land.at[k], sib, land.at[k])]


def _pass_plan(i, src, land, k, c, chips, sib):
    h = src.shape[1] // 2
    mine, theirs = pl.ds(c * h, h), pl.ds((1 - c) * h, h)
    return [(src.at[2 * px + py, mine], land.at[2 * px + py, mine], sib, land.at[2 * px + py, theirs]) for px, py in chips]


def _swap_plan(i, src, land, k, c, chips, sib):
    h = src.shape[1] // 2
    return [(src.at[:, pl.ds((1 - c) * h, h)], land, sib, land)]


def _scatter_plan(i, src, land, k, c, chips, sib):
    return [(src.at[2 * px + py], land.at[k], (px, py, c), land.at[2 * px + py]) for px, py in chips]


def _join_plan(layers):
    def plan(i, src, land, k, c, chips, sib):
        h = src.shape[1] // 2
        mine, theirs = pl.ds(c * h, h), pl.ds((1 - c) * h, h)
        return [(src.at[layers[i], mine], land.at[layers[i], mine], sib, land.at[layers[i], theirs])]
    return plan


def _split_start(name, plan, ncopy, srcs, after, land_shapes=None):
    n = len(srcs)
    if land_shapes is None:
        land_shapes = [(N_CHIPS,) + a.shape[-2:] for a in srcs]
    in_place = land_shapes == "self"
    lands = [] if in_place else [pltpu.with_memory_space_constraint(lax.empty(shape, a.dtype), pltpu.HBM)
                                 for shape, a in zip(land_shapes, srcs)]
    nbuf = n + len(lands)

    def body(*refs):
        src_refs = refs[1:1 + n]
        land_refs = src_refs if in_place else refs[1 + n:1 + nbuf]
        send, recv, token = refs[1 + nbuf], refs[2 + nbuf], refs[-1]
        x, y, c, chips = _place()
        for i in range(n):
            for j, (src, dst, peer, _) in enumerate(plan(i, src_refs[i], land_refs[i], 2 * x + y, c, chips, (x, y, 1 - c))):
                _remote(src, dst, send.at[i * ncopy + j], recv.at[i * ncopy + j], peer).start()
        token[...] = jnp.zeros_like(token)

    outs = pl.pallas_call(
        body, name=name,
        in_specs=[_ANY] + [_HBM] * nbuf,
        out_specs=[_SEM, _SEM] + [_HBM] * nbuf + [_VMEM],
        out_shape=[pltpu.SemaphoreType.DMA((n * ncopy,)), pltpu.SemaphoreType.DMA((n * ncopy,))]
        + [pltpu.HBM(a.shape, a.dtype) for a in list(srcs) + lands] + [jax.ShapeDtypeStruct((8, LANES), F32)],
        input_output_aliases={1 + i: 2 + i for i in range(nbuf)},
        compiler_params=pltpu.CompilerParams(has_side_effects=_EFFECT),
    )(after, *[pltpu.with_memory_space_constraint(a, pltpu.HBM) for a in srcs], *lands)
    return outs[0], outs[1], outs[2:2 + n], None if in_place else outs[2 + n:2 + nbuf], outs[-1]


def _split_wait(name, plan, started, after):
    send, recv, srcs, lands = started
    n = len(srcs)
    ncopy = send.shape[0] // n
    in_place = lands is None
    bufs = list(srcs) + ([] if in_place else list(lands))
    nbuf = len(bufs)

    def body(*refs):
        src_refs = refs[:n]
        land_refs = src_refs if in_place else refs[n:nbuf]
        send, recv = refs[nbuf], refs[nbuf + 1]
        x, y, c, chips = _place()
        for i in range(n):
            for j, (src, _, peer, landed) in enumerate(plan(i, src_refs[i], land_refs[i], 2 * x + y, c, chips, (x, y, 1 - c))):
                cp = _remote(src, landed, send.at[i * ncopy + j], recv.at[i * ncopy + j], peer)
                cp.wait_send()
                cp.wait_recv()

    outs = pl.pallas_call(
        body, name=name,
        in_specs=[_HBM] * nbuf + [_SEM, _SEM, _ANY],
        out_specs=[_HBM] * nbuf,
        out_shape=[pltpu.HBM(a.shape, a.dtype) for a in bufs],
        input_output_aliases={i: i for i in range(nbuf)},
        compiler_params=pltpu.CompilerParams(has_side_effects=_EFFECT),
    )(*bufs, send, recv, after)
    return outs[:n], outs[:n] if in_place else outs[n:]


def _allreduce_small(buf, after):
    rows = buf.shape[0]
    piece = rows // N_DEV

    def body(in_ref, after_ref, out_ref, land_ref, send, recv):
        x, y, c, _ = _place()
        me = 4 * x + 2 * y + c
        peers = [(1 - x if r & 4 else x, 1 - y if r & 2 else y, 1 - c if r & 1 else c) for r in range(1, N_DEV)]

        def rows_of(dev):
            return pl.ds(pl.multiple_of(dev * piece, 8), piece)

        first, second = [], []
        for r, (px, py, pc) in enumerate(peers):
            cp = _remote(in_ref.at[rows_of(4 * px + 2 * py + pc)], land_ref.at[me], send.at[0, r], recv.at[0, r], (px, py, pc))
            cp.start()
            first.append(cp)
        land_ref[me] = in_ref[rows_of(me), :]
        for r, (px, py, pc) in enumerate(peers):
            landed = land_ref.at[4 * px + 2 * py + pc]
            _remote(landed, landed, send.at[0, r], recv.at[0, r], (px, py, pc)).wait_recv()
        acc = land_ref[0]
        for d in range(1, N_DEV):
            acc = acc + land_ref[d]
        out_ref[rows_of(me), :] = acc
        for r, peer in enumerate(peers):
            cp = _remote(out_ref.at[rows_of(me)], out_ref.at[rows_of(me)], send.at[1, r], recv.at[1, r], peer)
            cp.start()
            second.append(cp)
        for r, (px, py, pc) in enumerate(peers):
            landed = out_ref.at[rows_of(4 * px + 2 * py + pc)]
            _remote(landed, landed, send.at[1, r], recv.at[1, r], (px, py, pc)).wait_recv()
        for cp in first + second:
            cp.wait_send()

    return pl.pallas_call(
        body, name="allreduce_small",
        in_specs=[_VMEM, _ANY], out_specs=_VMEM,
        out_shape=jax.ShapeDtypeStruct(buf.shape, F32),
        scratch_shapes=[pltpu.VMEM((N_DEV, piece, LANES), F32), pltpu.SemaphoreType.DMA((2, N_DEV - 1)),
                        pltpu.SemaphoreType.DMA((2, N_DEV - 1))],
        compiler_params=pltpu.CompilerParams(vmem_limit_bytes=_VMEM_LIMIT),
    )(buf, after)


def _div_tile(n, cap, mult):
    best = None
    for d in range(mult, min(n, cap) + 1, mult):
        if n % d == 0:
            best = d
    return best if best is not None else n


def _add_sibling(name, grad, got, c):
    nk, hr, cc = got.shape
    tr = _div_tile(hr, 512, 16)
    nt = hr // tr

    def body(c_ref, g_ref, o_ref, s_ref):
        s_ref[...] = (g_ref[...].astype(F32) + o_ref[...].astype(F32)).astype(s_ref.dtype)

    blk = (None, tr, cc)
    return pl.pallas_call(
        body, name=name,
        grid_spec=pltpu.PrefetchScalarGridSpec(
            num_scalar_prefetch=1, grid=(nk, nt),
            in_specs=[pl.BlockSpec(blk, lambda i, q, c_ref: (i, c_ref[0] * nt + q, 0)),
                      pl.BlockSpec(blk, lambda i, q, c_ref: (i, q, 0))],
            out_specs=pl.BlockSpec(blk, lambda i, q, c_ref: (i, q, 0))),
        out_shape=jax.ShapeDtypeStruct(got.shape, got.dtype),
        compiler_params=_params("parallel", "parallel"),
    )(c, grad, got)


def _add_chips(name, mine, got, place, buf, l):
    nk, hr, cc = got.shape
    tr = _div_tile(hr, 512, 16)
    nt = hr // tr

    def body(*refs):
        acc = refs[1][...].astype(F32)
        for q in range(1, nk):
            acc = acc + refs[1 + q][...].astype(F32)
        refs[2 + nk][...] = acc

    def part(q):
        return pl.BlockSpec((None, tr, cc), lambda i, p_ref: ((p_ref[0] + q) % nk, i, 0))

    return pl.pallas_call(
        body, name=name,
        grid_spec=pltpu.PrefetchScalarGridSpec(
            num_scalar_prefetch=1, grid=(nt,),
            in_specs=[part(q) for q in range(nk)] + [_ANY],
            out_specs=pl.BlockSpec((None, tr, cc), lambda i, p_ref: (l, p_ref[1] * nt + i, 0))),
        out_shape=jax.ShapeDtypeStruct(buf.shape, F32),
        input_output_aliases={1 + nk: 0},
        compiler_params=_params("parallel"),
    )(place, mine, *([got] * (nk - 1)), buf)


def _adamw(name, w, g, m, v, after=None):
    shape = w.shape
    cols = shape[-1]
    rows = w.size // cols
    tr = rows if rows * cols <= 2 ** 18 else _div_tile(rows, max(8, 2 ** 18 // cols), 8)
    c1 = 1.0 - ADAM_B1 ** ADAM_STEP
    c2 = 1.0 - ADAM_B2 ** ADAM_STEP
    extra = [] if after is None else [after]

    def body(*refs):
        w_ref, g_ref, m_ref, v_ref = refs[:4]
        d_ref, mo_ref, vo_ref, go_ref = refs[4 + len(extra):]
        gg = g_ref[...]
        mn = ADAM_B1 * m_ref[...] + (1.0 - ADAM_B1) * gg
        vn = ADAM_B2 * v_ref[...] + (1.0 - ADAM_B2) * (gg * gg)
        d_ref[...] = -ADAM_LR * ((mn / c1) / (jnp.sqrt(vn / c2) + ADAM_EPS) + ADAM_WD * w_ref[...])
        mo_ref[...] = mn
        vo_ref[...] = vn
        go_ref[...] = gg

    blk = pl.BlockSpec((tr, cols), lambda i: (i, 0))
    sds = jax.ShapeDtypeStruct((rows, cols), F32)
    outs = pl.pallas_call(
        body, name=name, grid=(rows // tr,),
        in_specs=[blk] * 4 + [_ANY] * len(extra), out_specs=(blk,) * 4, out_shape=(sds,) * 4,
        compiler_params=_params("parallel"),
    )(*[a.reshape(rows, cols) for a in (w, g, m, v)], *extra)
    return [o.reshape(shape) for o in outs]


_WEIGHTS = ["ffn1_norm", "ffn1_w_in", "ffn1_w_out", "mix_norm", "ffn2_norm", "ffn2_w_in", "ffn2_w_out",
            "ev_w_in", "ev_conv_w", "ev_q_norm", "ev_k_norm", "ev_w_out", "od_w_in", "od_pool_w",
            "od_pool_scale", "od_sgu_norm", "od_sgu_w", "od_sgu_b", "od_w_out", "final_norm"]
_BIG = ["ffn1_w_in", "ffn1_w_out", "ffn2_w_in", "ffn2_w_out", "ev_w_in", "ev_w_out", "od_w_in", "od_w_out"]
_SMALL_SHARDED = ["ev_conv_w", "od_pool_scale", "od_sgu_norm"]


def _pad_rows(a, mult=8):
    pad = (-a.shape[0]) % mult
    return a if pad == 0 else jnp.concatenate([a, jnp.zeros((pad,) + a.shape[1:], a.dtype)], axis=0)


def _join_cols(g):
    return g.transpose(1, 0, 2).reshape(g.shape[1], N_CHIPS * g.shape[2])


def _split_cols(w):
    return w.reshape(w.shape[0], N_CHIPS, w.shape[1] // N_CHIPS).transpose(1, 0, 2)


def kernel(x, ffn1_norm, ffn1_w_in, ffn1_w_out, mix_norm, ffn2_norm, ffn2_w_in, ffn2_w_out, ev_w_in, ev_conv_w,
           ev_q_norm, ev_k_norm, ev_w_out, od_w_in, od_pool_w, od_pool_scale, od_sgu_norm, od_sgu_w, od_sgu_b,
           od_w_out, final_norm, loss_target, m_ffn1_norm, m_ffn1_w_in, m_ffn1_w_out, m_mix_norm, m_ffn2_norm,
           m_ffn2_w_in, m_ffn2_w_out, m_ev_w_in, m_ev_conv_w, m_ev_q_norm, m_ev_k_norm, m_ev_w_out, m_od_w_in,
           m_od_pool_w, m_od_pool_scale, m_od_sgu_norm, m_od_sgu_w, m_od_sgu_b, m_od_w_out, m_final_norm, v_ffn1_norm,
           v_ffn1_w_in, v_ffn1_w_out, v_mix_norm, v_ffn2_norm, v_ffn2_w_in, v_ffn2_w_out, v_ev_w_in, v_ev_conv_w,
           v_ev_q_norm, v_ev_k_norm, v_ev_w_out, v_od_w_in, v_od_pool_w, v_od_pool_scale, v_od_sgu_norm, v_od_sgu_w,
           v_od_sgu_b, v_od_w_out, v_final_norm):
    return _step(x, ffn1_norm, ffn1_w_in, ffn1_w_out, mix_norm, ffn2_norm, ffn2_w_in, ffn2_w_out, ev_w_in, ev_conv_w,
                 ev_q_norm, ev_k_norm, ev_w_out, od_w_in, od_pool_w, od_pool_scale, od_sgu_norm, od_sgu_w, od_sgu_b,
                 od_w_out, final_norm, loss_target, m_ffn1_norm, m_ffn1_w_in, m_ffn1_w_out, m_mix_norm, m_ffn2_norm,
                 m_ffn2_w_in, m_ffn2_w_out, m_ev_w_in, m_ev_conv_w, m_ev_q_norm, m_ev_k_norm, m_ev_w_out, m_od_w_in,
                 m_od_pool_w, m_od_pool_scale, m_od_sgu_norm, m_od_sgu_w, m_od_sgu_b, m_od_w_out, m_final_norm,
                 v_ffn1_norm, v_ffn1_w_in, v_ffn1_w_out, v_mix_norm, v_ffn2_norm, v_ffn2_w_in, v_ffn2_w_out,
                 v_ev_w_in, v_ev_conv_w, v_ev_q_norm, v_ev_k_norm, v_ev_w_out, v_od_w_in, v_od_pool_w,
                 v_od_pool_scale, v_od_sgu_norm, v_od_sgu_w, v_od_sgu_b, v_od_w_out, v_final_norm)


def _step(*args):
    nw = len(_WEIGHTS)
    x = args[0]
    w = dict(zip(_WEIGHTS, args[1:1 + nw]))
    target = args[1 + nw]
    m = dict(zip(_WEIGHTS, args[2 + nw:2 + 2 * nw]))
    v = dict(zip(_WEIGHTS, args[2 + 2 * nw:2 + 3 * nw]))
    depth = w["ffn1_norm"].shape[0]
    n_even, n_odd = w["ev_w_in"].shape[0], w["od_w_in"].shape[0]
    chip = 2 * lax.axis_index("x") + lax.axis_index("y")
    place = jnp.stack([chip, lax.axis_index("c")]).astype(jnp.int32)
    core = place[1:2]

    def sharded(l, block):
        if block == "mix":
            block = "ev" if l % 2 == 0 else "od"
            return [(block + "_w_in", l // 2), (block + "_w_out", l // 2)]
        return [(block + "_w_in", l), (block + "_w_out", l)]

    def shards(group):
        return [w[n][i].astype(_ACT) for l, block in group for n, i in sharded(l, block)]

    small_rows = [w["ev_conv_w"].reshape(3 * n_even, LANES), w["od_pool_scale"], w["od_sgu_norm"]]
    first, small = _gather_shards(shards([(0, "ffn1")]), _pad_rows(jnp.concatenate(small_rows, axis=0)))
    conv_w = small[:, :3 * n_even].reshape(N_CHIPS, n_even, 3, LANES).transpose(1, 2, 0, 3).reshape(n_even, 3, CONV_WIDTH)
    pool_scale = small[:, 3 * n_even:3 * n_even + n_odd].transpose(1, 0, 2).reshape(n_odd, HALF)
    sgu_norm = small[:, 3 * n_even + n_odd:3 * n_even + 2 * n_odd].transpose(1, 0, 2).reshape(n_odd, HALF)
    later = [[(0, "mix"), (0, "ffn2")]] + [[(l, "ffn1"), (l, "mix"), (l, "ffn2")] for l in range(1, depth)]
    gathering, after = [], small
    for i, group in enumerate(later):
        gathering.append(_split_start(f"gather_start{i}", _gather_plan, N_CHIPS, shards(group), after))
        after = gathering[-1][4]
    gathered = {(0, "ffn1"): first}

    def rows(g):
        return g.reshape(N_CHIPS * g.shape[1], g.shape[2])

    passing = {}

    def fetch(i, x_in):
        got = _split_wait(f"gather_wait{i}", _gather_plan, gathering[i][:4], x_in)[1]
        passing[i] = _split_start(f"pass_start{i}", _pass_plan, N_CHIPS - 1, got, x_in, "self")
        return passing[i][4][0, 0]

    def weights_of(l, block, x_in):
        zero = after[0, 0] if (l, block) == (0, "ffn1") else 0.0
        if (l, block) not in gathered:
            i = next(i for i, group in enumerate(later) if (l, block) in group)
            if i not in passing:
                zero = zero + fetch(i, x_in)
            got = _split_wait(f"pass_wait{i}", _pass_plan, passing.pop(i)[:4], x_in)[0]
            for n, key in enumerate(later[i]):
                gathered[key] = got[2 * n:2 * n + 2]
        if block == "ffn2" and l + 1 < depth:
            zero = zero + fetch(next(i for i, group in enumerate(later) if (l + 1, "ffn1") in group), x_in)
        w_in, w_out = gathered[(l, block)]
        if block != "mix":
            return dict(norm=w[block + "_norm"][l] + zero, w_in4=w_in, w_out=rows(w_out))
        j = l // 2
        if l % 2 == 0:
            mix = dict(conv_w=conv_w[j], q_gain=w["ev_q_norm"][j], k_gain=w["ev_k_norm"][j])
        else:
            mix = dict(pool_w=w["od_pool_w"][j], pool_scale=pool_scale[j], sgu_norm=sgu_norm[j],
                       sgu_w=w["od_sgu_w"][j], sgu_b=w["od_sgu_b"][j])
        return dict(mix, norm=w["mix_norm"][l] + zero, w_in=_join_cols(w_in), w_out=rows(w_out))

    def by_chip(dw):
        return dw.reshape(N_CHIPS, dw.shape[0] // N_CHIPS, dw.shape[1])

    bufs = {n: lax.empty(w[n].shape, F32) for n in _BIG}
    small_grads = {n: [None] * w[n].shape[0] for n in _WEIGHTS if n not in _BIG and n != "final_norm"}
    swapping, scattering, joining, group = [], [], [], []

    def finish_swap(after):
        tag, names, started = swapping.pop()
        local, from_sibling = _split_wait(f"swap_wait{tag}", _swap_plan, started[:4], after)
        halves = [_add_sibling(f"add_sibling{tag}_{n}", a, b, core) for (n, _), a, b in zip(names, local, from_sibling)]
        scattering.append((tag, names, _split_start(f"scatter_start{tag}", _scatter_plan, N_CHIPS - 1, halves, after)))
        return scattering[-1][2][4][0, 0]

    def finish_scatter(after):
        tag, names, started = scattering.pop(0)
        halves, got = _split_wait(f"scatter_wait{tag}", _scatter_plan, started[:4], after)
        for i, (n, j) in enumerate(names):
            bufs[n] = _add_chips(f"add_chips{tag}_{n}", halves[i], got[i], place, bufs[n], j)
        layers = [j for _, j in names]
        started = _split_start(f"join_start{tag}", _join_plan(layers), 1, [bufs[n] for n, _ in names], after, "self")
        for (n, _), b in zip(names, started[2]):
            bufs[n] = b
        joining.append((tag, names, layers, started))
        return started[4]

    def finish_join(after):
        tag, names, layers, started = joining.pop(0)
        joined = _split_wait(f"join_wait{tag}", _join_plan(layers), (started[0], started[1], [bufs[n] for n, _ in names], None),
                             after)[0]
        for (n, _), b in zip(names, joined):
            bufs[n] = b

    def grads_done(l, block, g, dx):
        j = l // 2
        if block == "mix":
            local = [_split_cols(g["w_in"]), by_chip(g["w_out"])]
            small_grads["mix_norm"][l] = g["norm"]
            renamed = (dict(conv_w="ev_conv_w", q_gain="ev_q_norm", k_gain="ev_k_norm") if l % 2 == 0 else
                       dict(pool_w="od_pool_w", pool_scale="od_pool_scale", sgu_norm="od_sgu_norm", sgu_w="od_sgu_w",
                            sgu_b="od_sgu_b"))
            for key, n in renamed.items():
                small_grads[n][j] = g[key]
        else:
            local = [g["w_in4"], by_chip(g["w_out"])]
            small_grads[block + "_norm"][l] = g["norm"]
        group.extend(zip(sharded(l, block), local))
        zero = 0.0
        if swapping:
            if scattering:
                zero = zero + finish_scatter(dx)[0, 0]
            zero = zero + finish_swap(dx)
        if block == "ffn2" or (block == "mix" and l > 0):
            return zero
        tag = f"{l}_{block}"
        names, local = [n for n, _ in group], [a for _, a in group]
        group.clear()
        shapes = [(N_CHIPS, a.shape[1] // 2, a.shape[2]) for a in local]
        swapping.append((tag, names, _split_start(f"swap_start{tag}", _swap_plan, 1, local, dx, shapes)))
        return zero + swapping[-1][2][4][0, 0]

    loss_part, grad_x, dfinal = _local_step(x, target, depth, weights_of, w["final_norm"], grads_done)
    loss = lax.psum(loss_part, ("x", "y", "c"))

    small_grads = {n: jnp.stack(parts) for n, parts in small_grads.items()}
    small_grads["final_norm"] = dfinal
    names = list(small_grads)
    flat = jnp.concatenate([small_grads[n].reshape(-1) for n in names])
    total = flat.shape[0]
    flat = jnp.concatenate([flat, jnp.zeros((-total) % (N_DEV * 8 * LANES), F32)])
    summed = _allreduce_small(flat.reshape(-1, LANES), grad_x).reshape(-1)
    grads, off = {}, 0
    for n in names:
        size = small_grads[n].size
        full_grad = summed[off:off + size].reshape(small_grads[n].shape)
        off += size
        if n in _SMALL_SHARDED:
            full_grad = lax.dynamic_slice_in_dim(full_grad, chip * LANES, LANES, axis=full_grad.ndim - 1)
        grads[n] = full_grad
    updates = {}

    def update(n, after):
        updates[n] = _adamw("adamw_" + n, w[n], grads[n] if n in grads else bufs[n], m[n], v[n], after)
        return updates[n][1]

    finish_swap(summed)
    behind = scattering[-1][2][4]
    while len(joining) > 0:
        finish_join(behind)
    behind = finish_scatter(behind)
    for n in ("od_w_in", "od_w_out"):
        behind = update(n, behind)
    finish_join(behind)
    for n in ("ffn2_w_in", "ffn2_w_out", "ev_w_in", "ev_w_out"):
        behind = update(n, behind)
    behind = finish_scatter(behind)
    for n in _WEIGHTS:
        if n in grads:
            behind = update(n, behind)
    finish_join(behind)
    for n in ("ffn1_w_in", "ffn1_w_out"):
        behind = update(n, behind)
    return (loss, grad_x, *[updates[n][3] for n in _WEIGHTS], *[updates[n][0] for n in _WEIGHTS],
            *[updates[n][1] for n in _WEIGHTS], *[updates[n][2] for n in _WEIGHTS])
```

```python
import jax
import jax.numpy as jnp
from jax import lax
from jax.experimental import pallas as pl
from jax.experimental.pallas import tpu as pltpu

F32 = jnp.float32
_MXU = jnp.bfloat16
_ACT = jnp.bfloat16

D_MODEL = 1024
GRID_W = 64
HEAD_DIM = 64
N_Q_HEADS = 8
N_KV_HEADS = 2
Q_PER_KV = N_Q_HEADS // N_KV_HEADS
ATTN_WIDTH = N_Q_HEADS * HEAD_DIM
KV_WIDTH = N_KV_HEADS * HEAD_DIM
ROPE_THETA = 10000.0
CONV_WIDTH = D_MODEL // 2
POOL_RADII = (1, 2, 4, 8)
POOL_GROUP = 128
SGU_GROUP = 128
SGU_CHUNK = 128
N_GROUPS = 4
HALF = D_MODEL // 2
EPS = 1e-6
HALO = 8
LANES = 128
N_CHIPS = 4
N_DEV = 8

ADAM_LR = 0.001
ADAM_B1 = 0.9
ADAM_B2 = 0.999
ADAM_EPS = 1e-08
ADAM_WD = 0.01
ADAM_STEP = 10

_VMEM_LIMIT = 56 * 2 ** 20
_MESH = pl.DeviceIdType.MESH
_ANY = pl.BlockSpec(memory_space=pl.ANY)
_VMEM = pl.BlockSpec(memory_space=pltpu.VMEM)

_DN = {
    "nn": (((1,), (0,)), ((), ())),
    "nt": (((1,), (1,)), ((), ())),
    "tn": (((0,), (0,)), ((), ())),
}


def _params(*sem):
    return pltpu.CompilerParams(dimension_semantics=sem, vmem_limit_bytes=_VMEM_LIMIT)


def _tile(n, cap):
    best = None
    d = LANES
    while d <= min(n, cap):
        if n % d == 0:
            best = d
        d += LANES
    return best if best is not None else n


def _dot(a, b, mode="nn"):
    return lax.dot_general(a.astype(_MXU), b.astype(_MXU), _DN[mode], preferred_element_type=F32)


def _cat(*vals):
    vals = [v.astype(_MXU) for v in vals]
    return vals[0] if len(vals) == 1 else jnp.concatenate(vals, axis=1)


def _sigmoid(g):
    return 1.0 / (1.0 + jnp.exp(-g))


def _norm_rows(x, g):
    r = lax.rsqrt(jnp.mean(x * x, axis=-1, keepdims=True) + EPS)
    return (x * r) * g


def _swiglu(g, u):
    return (g * _sigmoid(g)) * u


_GELU_C = 0.7978845608028654


def _gelu(x):
    return 0.5 * x * (1.0 + jnp.tanh(_GELU_C * (x + 0.044715 * (x * x * x))))


def _gelu_grad(x):
    t = jnp.tanh(_GELU_C * (x + 0.044715 * (x * x * x)))
    return 0.5 * (1.0 + t) + 0.5 * x * (1.0 - t * t) * (_GELU_C * (1.0 + 3.0 * 0.044715 * (x * x)))


def _mm(name, grid, mode, a_ops, b_ops, e_ops, out_shape, out_specs, acc_shape, a_fn=_cat, b_fn=_cat, epi=None,
        n_outer=False, m_carried=False, b_pick=None):
    ni, nj, nk = grid
    na, nb, ne = len(a_ops), len(b_ops), len(e_ops)
    multi = isinstance(out_shape, (list, tuple))
    no = len(out_shape) if multi else 1

    def body(*refs):
        a_refs = refs[:na]
        b_refs = refs[na:na + nb]
        e_refs = refs[na + nb:na + nb + ne]
        o_refs = refs[na + nb + ne:na + nb + ne + no]
        a = a_fn(*[r[...] for r in a_refs])
        if b_pick is None:
            b = b_fn(*[r[...] for r in b_refs])
        else:
            b = b_pick(b_refs, pl.program_id(1), pl.program_id(2))
        p = _dot(a, b, mode)

        def finish(acc):
            if epi is None:
                o_refs[0][...] = acc.astype(o_refs[0].dtype)
            else:
                epi(acc, [r[...] for r in e_refs], o_refs)

        if nk == 1:
            finish(p)
        else:
            acc_ref = refs[-1]
            k = pl.program_id(2)

            @pl.when(k == 0)
            def _():
                acc_ref[...] = p

            @pl.when((k > 0) & (k < nk - 1))
            def _():
                acc_ref[...] += p

            @pl.when(k == nk - 1)
            def _():
                finish(acc_ref[...] + p)

    ops = list(a_ops) + list(b_ops) + list(e_ops)
    if n_outer:
        def flip(spec):
            return pl.BlockSpec(spec.block_shape, lambda j, i, k, f=spec.index_map: f(i, j, k))

        grid = (nj, ni, nk)
        ops = [(a, flip(s)) for a, s in ops]
        out_specs = [flip(s) for s in out_specs] if multi else flip(out_specs)
    return pl.pallas_call(
        body, name=name, grid=grid,
        in_specs=[s for _, s in ops],
        out_specs=out_specs, out_shape=out_shape,
        scratch_shapes=[pltpu.VMEM(acc_shape, F32)] if nk > 1 else [],
        compiler_params=_params(*(("arbitrary",) * 3 if m_carried else ("parallel", "parallel", "arbitrary"))),
    )(*[a for a, _ in ops])


def _whole(a):
    return pl.BlockSpec(a.shape, lambda i, j, k: (0,) * a.ndim, pipeline_mode=pl.Buffered(1))


def _norm_bwd_epi(acc, e, o):
    xf, dres, g = e
    r = lax.rsqrt(jnp.mean(xf * xf, axis=-1, keepdims=True) + EPS)
    xhat = xf * r
    dgx = acc * g
    m = jnp.mean(dgx * xhat, axis=-1, keepdims=True)
    o[0][...] = dres + r * (dgx - xhat * m)
    part = jnp.sum(acc * xhat, axis=0, keepdims=True)
    i = pl.program_id(0)

    @pl.when(i == 0)
    def _():
        o[1][...] = part

    @pl.when(i > 0)
    def _():
        o[1][...] += part


def _norm_bwd_ops(x, dres, gain, tm):
    t, d = x.shape
    row = pl.BlockSpec((tm, d), lambda i, j, k: (i, 0))
    vec = pl.BlockSpec((1, d), lambda i, j, k: (0, 0))
    return ([(x, row), (dres, row), (gain.reshape(1, d), vec)],
            [jax.ShapeDtypeStruct((t, d), F32), jax.ShapeDtypeStruct((1, d), F32)], [row, vec])


def _rows(t):
    return _tile(t, 512)


def _rmsnorm_fwd(name, x, gain):
    t, d = x.shape
    tr = _rows(t)

    def body(x_ref, g_ref, h_ref):
        xf = x_ref[...]
        r = lax.rsqrt(jnp.mean(xf * xf, axis=-1, keepdims=True) + EPS)
        h_ref[...] = ((xf * r) * g_ref[...]).astype(h_ref.dtype)

    return pl.pallas_call(
        body, name=name, grid=(t // tr,),
        in_specs=[pl.BlockSpec((tr, d), lambda i: (i, 0)), pl.BlockSpec((1, d), lambda i: (0, 0))],
        out_specs=pl.BlockSpec((tr, d), lambda i: (i, 0)),
        out_shape=jax.ShapeDtypeStruct((t, d), _ACT),
        compiler_params=_params("parallel"),
    )(x, gain.reshape(1, d))


def _rmsnorm_bwd(name, dh, x, gain, dres):
    t, d = x.shape
    tr = _rows(t)

    def body(dh_ref, x_ref, g_ref, dres_ref, dx_ref, dg_ref):
        i = pl.program_id(0)
        xf = x_ref[...]
        r = lax.rsqrt(jnp.mean(xf * xf, axis=-1, keepdims=True) + EPS)
        xhat = xf * r
        dy = dh_ref[...].astype(F32)
        dgx = dy * g_ref[...]
        m = jnp.mean(dgx * xhat, axis=-1, keepdims=True)
        dx_ref[...] = dres_ref[...] + r * (dgx - xhat * m)
        part = jnp.sum(dy * xhat, axis=0, keepdims=True)

        @pl.when(i == 0)
        def _():
            dg_ref[...] = part

        @pl.when(i > 0)
        def _():
            dg_ref[...] += part

    row = pl.BlockSpec((tr, d), lambda i: (i, 0))
    vec = pl.BlockSpec((1, d), lambda i: (0, 0))
    dx, dg = pl.pallas_call(
        body, name=name, grid=(t // tr,),
        in_specs=[row, row, vec, row],
        out_specs=(row, vec),
        out_shape=(jax.ShapeDtypeStruct((t, d), F32), jax.ShapeDtypeStruct((1, d), F32)),
        compiler_params=_params("arbitrary"),
    )(dh, x, gain.reshape(1, d), dres)
    return dx, dg.reshape(d)


def _final_loss(name, x, gain, target):
    t, d = x.shape
    tr = _rows(t)

    def body(x_ref, g_ref, t_ref, dx_ref, dg_ref, loss_ref):
        i = pl.program_id(0)
        xf = x_ref[...]
        r = lax.rsqrt(jnp.mean(xf * xf, axis=-1, keepdims=True) + EPS)
        xhat = xf * r
        g = g_ref[...]
        err = xhat * g - t_ref[...]
        lpart = 0.5 * jnp.sum(jnp.mean(err * err, axis=-1, keepdims=True), axis=0, keepdims=True)
        dy = err * (1.0 / d)
        dgx = dy * g
        m = jnp.mean(dgx * xhat, axis=-1, keepdims=True)
        dx_ref[...] = r * (dgx - xhat * m)
        part = jnp.sum(dy * xhat, axis=0, keepdims=True)
        lrow = jnp.broadcast_to(lpart, (1, LANES))

        @pl.when(i == 0)
        def _():
            dg_ref[...] = part
            loss_ref[...] = lrow

        @pl.when(i > 0)
        def _():
            dg_ref[...] += part
            loss_ref[...] += lrow

    row = pl.BlockSpec((tr, d), lambda i: (i, 0))
    vec = pl.BlockSpec((1, d), lambda i: (0, 0))
    dx, dg, loss = pl.pallas_call(
        body, name=name, grid=(t // tr,),
        in_specs=[row, vec, row],
        out_specs=(row, vec, pl.BlockSpec((1, LANES), lambda i: (0, 0))),
        out_shape=(jax.ShapeDtypeStruct((t, d), F32), jax.ShapeDtypeStruct((1, d), F32),
                   jax.ShapeDtypeStruct((1, LANES), F32)),
        compiler_params=_params("arbitrary"),
    )(x, gain.reshape(1, d), target)
    return loss[0, 0], dx, dg.reshape(d)


_FFN_TILES = dict(in_tm=1024, in_n_outer=False, out_tm=512, dact_tm=512, dwout_tk=1024, dh_tm=512, dwin_tk=2048)


def _ffn_tiles(layer, which):
    return _FFN_TILES


def _ffn_fwd(tag, x, gain, w_in4, w_out, cfg):
    t, d = x.shape
    fs = w_in4.shape[2]
    f = 2 * fs
    tm = _tile(t, cfg["in_tm"])
    gain = gain.reshape(1, d)
    gu = _mm(
        tag + "_in", (t // tm, N_CHIPS, 1), "nn",
        [(x, pl.BlockSpec((tm, d), lambda i, j, k: (i, 0))), (gain, pl.BlockSpec((1, d), lambda i, j, k: (0, 0)))],
        [(w_in4, _whole(w_in4))], [],
        jax.ShapeDtypeStruct((2, t, f), _ACT),
        pl.BlockSpec((None, tm, fs), lambda i, j, k: (j // 2, i, j % 2)), None, a_fn=_norm_rows,
        b_pick=lambda b, j, k: b[0][j])
    tm2 = _tile(t, cfg["out_tm"])

    def epi(acc, e, o):
        o[0][...] = e[0] + 0.5 * acc

    x_out = _mm(
        tag + "_out", (t // tm2, 1, 1), "nn",
        [(gu, pl.BlockSpec((None, tm2, f), lambda i, j, k: (0, i, 0))),
         (gu, pl.BlockSpec((None, tm2, f), lambda i, j, k: (1, i, 0)))],
        [(w_out, pl.BlockSpec((f, d), lambda i, j, k: (0, 0)))],
        [(x, pl.BlockSpec((tm2, d), lambda i, j, k: (i, 0)))],
        jax.ShapeDtypeStruct((t, d), F32),
        pl.BlockSpec((tm2, d), lambda i, j, k: (i, 0)), None,
        a_fn=_swiglu, epi=epi)
    return x_out, (x, gu)


def _ffn_bwd(tag, dxo, saved, gain, w_in4, w_out, cfg, mid, ready):
    x, gu = saved
    t, d = x.shape
    fs = w_in4.shape[2]
    f = 2 * fs
    tm = _tile(t, cfg["dact_tm"])
    tk = _tile(t, cfg["dwout_tk"])

    def epi_act(acc, e, o):
        g, u = e
        da = (0.5 * acc).astype(g.dtype)
        sig = _sigmoid(g)
        silu = g * sig
        o[0][0] = (da * u * (sig + silu * (1.0 - sig))).astype(o[0].dtype)
        o[0][1] = (da * silu).astype(o[0].dtype)

    dgu = _mm(
        tag + "_dact", (t // tm, 2, 1), "nt",
        [(dxo, pl.BlockSpec((tm, d), lambda i, j, k: (i, 0)))],
        [(w_out, _whole(w_out))],
        [(gu, pl.BlockSpec((None, tm, fs), lambda i, j, k: (0, i, j))),
         (gu, pl.BlockSpec((None, tm, fs), lambda i, j, k: (1, i, j)))],
        jax.ShapeDtypeStruct((2, t, f), _ACT),
        pl.BlockSpec((2, tm, fs), lambda i, j, k: (0, i, j)), None, epi=epi_act,
        b_pick=lambda b, j, k: b[0][pl.ds(pl.multiple_of(j * fs, LANES), fs), :])
    gain = gain + mid(dgu)

    def epi_half(acc, e, o):
        o[0][...] = (0.5 * acc).astype(o[0].dtype)

    dw_out = _mm(
        tag + "_dwout", (2, 1, t // tk), "tn",
        [(gu, pl.BlockSpec((None, tk, fs), lambda i, j, k: (0, k, i))),
         (gu, pl.BlockSpec((None, tk, fs), lambda i, j, k: (1, k, i)))],
        [(dxo, pl.BlockSpec((tk, d), lambda i, j, k: (k, 0)))], [],
        jax.ShapeDtypeStruct((f, d), _ACT),
        pl.BlockSpec((fs, d), lambda i, j, k: (i, 0)), (fs, d),
        a_fn=_swiglu, epi=epi_half)
    tk = _tile(t, cfg["dwin_tk"])
    dw_in4 = _mm(
        tag + "_dwin", (1, N_CHIPS, t // tk), "tn",
        [(x, pl.BlockSpec((tk, d), lambda i, j, k: (k, 0))),
         (gain.reshape(1, d), pl.BlockSpec((1, d), lambda i, j, k: (0, 0)))],
        [(dgu, pl.BlockSpec((None, tk, fs), lambda i, j, k: (j // 2, k, j % 2)))], [],
        jax.ShapeDtypeStruct((N_CHIPS, d, fs), _ACT),
        pl.BlockSpec((None, d, fs), lambda i, j, k: (j, 0, 0)), (d, fs), a_fn=_norm_rows)
    gain = gain + ready(dict(w_in4=dw_in4, w_out=dw_out), dw_in4)
    tm = _tile(t, cfg["dh_tm"])
    e_ops, shapes, specs = _norm_bwd_ops(x, dxo, gain, tm)
    dx, dgain = _mm(
        tag + "_dh", (t // tm, 1, 2), "nt",
        [(dgu, pl.BlockSpec((None, tm, f), lambda i, j, k: (k, i, 0)))],
        [(w_in4, _whole(w_in4))], e_ops, shapes, specs, (tm, d), epi=_norm_bwd_epi, m_carried=True,
        b_pick=lambda b, j, k: jnp.concatenate([b[0][2 * k], b[0][2 * k + 1]], axis=1))
    return dx, dgain.reshape(d)


_MIX_TILES = dict(tm=1024, dwout_tk=2048, dwin_tk=1024)


def _proj_in(tag, h, w_in):
    t, d = h.shape
    n = w_in.shape[1]
    tm = _tile(t, _MIX_TILES["tm"])
    return _mm(
        tag + "_in", (t // tm, 1, 1), "nn",
        [(h, pl.BlockSpec((tm, d), lambda i, j, k: (i, 0)))],
        [(w_in, pl.BlockSpec((d, n), lambda i, j, k: (0, 0)))], [],
        jax.ShapeDtypeStruct((t, n), _ACT),
        pl.BlockSpec((tm, n), lambda i, j, k: (i, 0)), None)


def _proj_out(tag, x, parts, w_out):
    t, d = x.shape
    tm = _tile(t, _MIX_TILES["tm"])

    def epi(acc, e, o):
        o[0][...] = e[0] + acc

    return _mm(
        tag + "_out", (t // tm, 1, 1), "nn",
        [(p, pl.BlockSpec((tm, p.shape[1]), lambda i, j, k: (i, 0))) for p in parts],
        [(w_out, pl.BlockSpec(w_out.shape, lambda i, j, k: (0, 0)))],
        [(x, pl.BlockSpec((tm, d), lambda i, j, k: (i, 0)))],
        jax.ShapeDtypeStruct((t, d), F32),
        pl.BlockSpec((tm, d), lambda i, j, k: (i, 0)), None, epi=epi)


def _proj_out_bwd(tag, dxo, parts, w_out):
    t, d = dxo.shape
    mix = w_out.shape[0]
    tm = _tile(t, _MIX_TILES["tm"])
    tk = _tile(t, _MIX_TILES["dwout_tk"])
    d_mix = _mm(
        tag + "_dmix", (t // tm, 1, 1), "nt",
        [(dxo, pl.BlockSpec((tm, d), lambda i, j, k: (i, 0)))],
        [(w_out, pl.BlockSpec((mix, d), lambda i, j, k: (0, 0)))], [],
        jax.ShapeDtypeStruct((t, mix), F32),
        pl.BlockSpec((tm, mix), lambda i, j, k: (i, 0)), None)
    dw_out = _mm(
        tag + "_dwout", (1, 1, t // tk), "tn",
        [(p, pl.BlockSpec((tk, p.shape[1]), lambda i, j, k: (k, 0))) for p in parts],
        [(dxo, pl.BlockSpec((tk, d), lambda i, j, k: (k, 0)))], [],
        jax.ShapeDtypeStruct((mix, d), _ACT),
        pl.BlockSpec((mix, d), lambda i, j, k: (0, 0)), (mix, d))
    return d_mix, dw_out


def _proj_in_bwd(tag, h, dparts, w_in, x, dres, gain, ready):
    t, d = h.shape
    n = w_in.shape[1]
    tm = _tile(t, _MIX_TILES["tm"])
    tk = _tile(t, _MIX_TILES["dwin_tk"])
    dw_in = _mm(
        tag + "_dwin", (1, 1, t // tk), "tn",
        [(h, pl.BlockSpec((tk, d), lambda i, j, k: (k, 0)))],
        [(p, pl.BlockSpec((tk, p.shape[1]), lambda i, j, k: (k, 0))) for p in dparts], [],
        jax.ShapeDtypeStruct((d, n), _ACT),
        pl.BlockSpec((d, n), lambda i, j, k: (0, 0)), (d, n))
    e_ops, shapes, specs = _norm_bwd_ops(x, dres, gain + ready(dw_in), tm)
    dx, dgain = _mm(
        tag + "_dh", (t // tm, 1, 1), "nt",
        [(p, pl.BlockSpec((tm, p.shape[1]), lambda i, j, k: (i, 0))) for p in dparts],
        [(w_in, pl.BlockSpec((d, n), lambda i, j, k: (0, 0)))], e_ops, shapes, specs, None,
        epi=_norm_bwd_epi, m_carried=True)
    return dx, dgain.reshape(d)


def _shifted(pad_ref, val, s):
    pad_ref[pl.ds(HALO, s), :] = val
    return pad_ref[pl.ds(HALO - 1, s), :], pad_ref[pl.ds(HALO + 1, s), :]


def _zero_halo(pad_ref, s):
    z = jnp.zeros((HALO, pad_ref.shape[1]), F32)
    pad_ref[pl.ds(0, HALO), :] = z
    pad_ref[pl.ds(HALO + s, HALO), :] = z


def _conv_fwd(tag, proj, conv_w, nb, s):
    t = proj.shape[0]
    ncb = CONV_WIDTH // LANES

    def body(gb_ref, gc_ref, hc_ref, w_ref, a_ref, pad_ref):
        _zero_halo(pad_ref, s)
        cg = gc_ref[...].astype(F32) * hc_ref[...].astype(F32)
        prev, nxt = _shifted(pad_ref, cg, s)
        w = w_ref[...]
        conv = prev * w[0:1, :] + cg * w[1:2, :] + nxt * w[2:3, :]
        a_ref[...] = (gb_ref[...].astype(F32) * conv).astype(a_ref.dtype)

    def col(off):
        return pl.BlockSpec((s, LANES), lambda b, c: (b, off + c))

    return pl.pallas_call(
        body, name=tag + "_conv", grid=(nb, ncb),
        in_specs=[col(0), col(ncb), col(2 * ncb), pl.BlockSpec((3, LANES), lambda b, c: (0, c))],
        out_specs=col(0),
        out_shape=jax.ShapeDtypeStruct((t, CONV_WIDTH), _ACT),
        scratch_shapes=[pltpu.VMEM((s + 2 * HALO, LANES), F32)],
        compiler_params=_params("parallel", "parallel"),
    )(proj, proj, proj, conv_w)


def _conv_bwd(tag, proj, conv_w, d_mix, nb, s):
    t = proj.shape[0]
    ncb = CONV_WIDTH // LANES

    def body(gb_ref, gc_ref, hc_ref, w_ref, da_ref, dgb_ref, dgc_ref, dhc_ref, dw_ref, pad_ref):
        b = pl.program_id(1)
        _zero_halo(pad_ref, s)
        gb = gb_ref[...].astype(F32)
        gc = gc_ref[...].astype(F32)
        hc = hc_ref[...].astype(F32)
        w = w_ref[...]
        da = da_ref[...]
        cg = gc * hc
        prev, nxt = _shifted(pad_ref, cg, s)
        conv = prev * w[0:1, :] + cg * w[1:2, :] + nxt * w[2:3, :]
        dgb_ref[...] = (da * conv).astype(dgb_ref.dtype)
        dconv = da * gb
        dw = jnp.concatenate([
            jnp.sum(dconv * prev, axis=0, keepdims=True),
            jnp.sum(dconv * cg, axis=0, keepdims=True),
            jnp.sum(dconv * nxt, axis=0, keepdims=True)], axis=0)
        dprev, dnxt = _shifted(pad_ref, dconv, s)
        dcg = dnxt * w[0:1, :] + dconv * w[1:2, :] + dprev * w[2:3, :]
        dgc_ref[...] = (dcg * hc).astype(dgc_ref.dtype)
        dhc_ref[...] = (dcg * gc).astype(dhc_ref.dtype)

        @pl.when(b == 0)
        def _():
            dw_ref[...] = dw

        @pl.when(b > 0)
        def _():
            dw_ref[...] += dw

    def col(off):
        return pl.BlockSpec((s, LANES), lambda c, b: (b, off + c))

    wspec = pl.BlockSpec((3, LANES), lambda c, b: (0, c))
    act = jax.ShapeDtypeStruct((t, CONV_WIDTH), _ACT)
    return pl.pallas_call(
        body, name=tag + "_dconv", grid=(ncb, nb),
        in_specs=[col(0), col(ncb), col(2 * ncb), wspec, col(0)],
        out_specs=(col(0), col(0), col(0), wspec),
        out_shape=(act, act, act, jax.ShapeDtypeStruct((3, CONV_WIDTH), F32)),
        scratch_shapes=[pltpu.VMEM((s + 2 * HALO, LANES), F32)],
        compiler_params=_params("parallel", "arbitrary"),
    )(proj, proj, proj, conv_w, d_mix)


def _rope_tables(s):
    rows = s // GRID_W
    r_idx, c_idx = jnp.meshgrid(jnp.arange(rows), jnp.arange(GRID_W), indexing="ij")
    r_idx = r_idx.reshape(-1).astype(F32)
    c_idx = c_idx.reshape(-1).astype(F32)
    n_freq = HEAD_DIM // 4
    inv = ROPE_THETA ** (-jnp.arange(n_freq, dtype=F32) / n_freq)
    ang = jnp.concatenate([r_idx[:, None] * inv, c_idx[:, None] * inv], axis=-1)
    cos = jnp.repeat(jnp.cos(ang), 2, axis=1)
    sin = jnp.repeat(jnp.sin(ang), 2, axis=1)
    sign = jnp.where(jnp.arange(HEAD_DIM) % 2 == 0, -1.0, 1.0).astype(F32)
    return jnp.tile(cos, (1, LANES // HEAD_DIM)), jnp.tile(sin * sign, (1, LANES // HEAD_DIM))


def _head_ones():
    i = jnp.arange(LANES) // HEAD_DIM
    return (i[:, None] == i[None, :]).astype(jnp.bfloat16)


def _head_sum(v, ones):
    outs = []
    for j in range(v.shape[1] // LANES):
        c = v[:, j * LANES:(j + 1) * LANES]
        hi = c.astype(jnp.bfloat16)
        lo = (c - hi.astype(F32)).astype(jnp.bfloat16)
        outs.append(jnp.dot(hi, ones, preferred_element_type=F32) + jnp.dot(lo, ones, preferred_element_type=F32))
    return outs[0] if len(outs) == 1 else jnp.concatenate(outs, axis=1)


def _pair_swap(v):
    outs = []
    for j in range(v.shape[1] // LANES):
        c = v[:, j * LANES:(j + 1) * LANES]
        lane = lax.broadcasted_iota(jnp.int32, c.shape, 1)
        outs.append(jnp.where(lane % 2 == 0, pltpu.roll(c, LANES - 1, 1), pltpu.roll(c, 1, 1)))
    return outs[0] if len(outs) == 1 else jnp.concatenate(outs, axis=1)


def _wide(tab, width):
    return tab if width == LANES else jnp.concatenate([tab] * (width // LANES), axis=1)


_QK_SCALE = HEAD_DIM ** -0.5


def _qk_fwd(tag, proj, q_gain, k_gain, cos, sin, nb, s):
    t = proj.shape[0]
    tr = _tile(s, 512)
    ns = s // tr
    q_off = 3 * CONV_WIDTH // ATTN_WIDTH
    k_off = (3 * CONV_WIDTH + ATTN_WIDTH) // KV_WIDTH

    def body(q_ref, k_ref, qg_ref, kg_ref, cos_ref, sin_ref, ones_ref, qo_ref, ko_ref):
        ones = ones_ref[...]
        for src, g_ref, dst, mult in ((q_ref, qg_ref, qo_ref, _QK_SCALE), (k_ref, kg_ref, ko_ref, 1.0)):
            v = src[...].astype(F32)
            w = v.shape[1]
            r = lax.rsqrt(_head_sum(v * v, ones) * (1.0 / HEAD_DIM) + EPS)
            vn = (v * r) * g_ref[...]
            rot = vn * _wide(cos_ref[...], w) + _pair_swap(vn) * _wide(sin_ref[...], w)
            dst[...] = (rot * mult).astype(dst.dtype)

    tab = pl.BlockSpec((tr, LANES), lambda i: (i % ns, 0))
    return pl.pallas_call(
        body, name=tag + "_qk", grid=(t // tr,),
        in_specs=[pl.BlockSpec((tr, ATTN_WIDTH), lambda i: (i, q_off)),
                  pl.BlockSpec((tr, KV_WIDTH), lambda i: (i, k_off)),
                  pl.BlockSpec((1, ATTN_WIDTH), lambda i: (0, 0)),
                  pl.BlockSpec((1, KV_WIDTH), lambda i: (0, 0)),
                  tab, tab, pl.BlockSpec((LANES, LANES), lambda i: (0, 0))],
        out_specs=(pl.BlockSpec((tr, ATTN_WIDTH), lambda i: (i, 0)),
                   pl.BlockSpec((tr, KV_WIDTH), lambda i: (i, 0))),
        out_shape=(jax.ShapeDtypeStruct((t, ATTN_WIDTH), _ACT), jax.ShapeDtypeStruct((t, KV_WIDTH), _ACT)),
        compiler_params=_params("parallel"),
    )(proj, proj, jnp.tile(q_gain, N_Q_HEADS).reshape(1, ATTN_WIDTH),
      jnp.tile(k_gain, N_KV_HEADS).reshape(1, KV_WIDTH), cos, sin, _head_ones())


def _qk_bwd(tag, proj, q_gain, k_gain, cos, sin, dq_rot, dk_rot, nb, s):
    t = proj.shape[0]
    tr = _tile(s, 512)
    ns = s // tr
    q_off = 3 * CONV_WIDTH // ATTN_WIDTH
    k_off = (3 * CONV_WIDTH + ATTN_WIDTH) // KV_WIDTH

    def body(q_ref, k_ref, qg_ref, kg_ref, cos_ref, sin_ref, ones_ref, dqr_ref, dkr_ref,
             dq_ref, dk_ref, dqg_ref, dkg_ref):
        i = pl.program_id(0)
        ones = ones_ref[...]
        for src, g_ref, dr_ref, dst, dg_ref, mult in ((q_ref, qg_ref, dqr_ref, dq_ref, dqg_ref, _QK_SCALE),
                                                      (k_ref, kg_ref, dkr_ref, dk_ref, dkg_ref, 1.0)):
            v = src[...].astype(F32)
            w = v.shape[1]
            r = lax.rsqrt(_head_sum(v * v, ones) * (1.0 / HEAD_DIM) + EPS)
            xhat = v * r
            dr = dr_ref[...] * mult
            dvn = dr * _wide(cos_ref[...], w) + _pair_swap(dr * _wide(sin_ref[...], w))
            dgx = dvn * g_ref[...]
            m = _head_sum(dgx * xhat, ones) * (1.0 / HEAD_DIM)
            dst[...] = (r * (dgx - xhat * m)).astype(dst.dtype)
            part = jnp.sum(dvn * xhat, axis=0, keepdims=True)
            fold = part[:, 0:HEAD_DIM]
            for hh in range(1, w // HEAD_DIM):
                fold = fold + part[:, hh * HEAD_DIM:(hh + 1) * HEAD_DIM]

            @pl.when(i == 0)
            def _():
                dg_ref[...] = fold

            @pl.when(i > 0)
            def _():
                dg_ref[...] += fold

    tab = pl.BlockSpec((tr, LANES), lambda i: (i % ns, 0))
    qrow = pl.BlockSpec((tr, ATTN_WIDTH), lambda i: (i, 0))
    krow = pl.BlockSpec((tr, KV_WIDTH), lambda i: (i, 0))
    gvec = pl.BlockSpec((1, HEAD_DIM), lambda i: (0, 0))
    dq, dk, dqg, dkg = pl.pallas_call(
        body, name=tag + "_dqk", grid=(t // tr,),
        in_specs=[pl.BlockSpec((tr, ATTN_WIDTH), lambda i: (i, q_off)),
                  pl.BlockSpec((tr, KV_WIDTH), lambda i: (i, k_off)),
                  pl.BlockSpec((1, ATTN_WIDTH), lambda i: (0, 0)),
                  pl.BlockSpec((1, KV_WIDTH), lambda i: (0, 0)),
                  tab, tab, pl.BlockSpec((LANES, LANES), lambda i: (0, 0)), qrow, krow],
        out_specs=(qrow, krow, gvec, gvec),
        out_shape=(jax.ShapeDtypeStruct((t, ATTN_WIDTH), _ACT), jax.ShapeDtypeStruct((t, KV_WIDTH), _ACT),
                   jax.ShapeDtypeStruct((1, HEAD_DIM), F32), jax.ShapeDtypeStruct((1, HEAD_DIM), F32)),
        compiler_params=_params("arbitrary"),
    )(proj, proj, jnp.tile(q_gain, N_Q_HEADS).reshape(1, ATTN_WIDTH),
      jnp.tile(k_gain, N_KV_HEADS).reshape(1, KV_WIDTH), cos, sin, _head_ones(), dq_rot, dk_rot)
    return dq, dk, dqg.reshape(HEAD_DIM), dkg.reshape(HEAD_DIM)


def _head(v, h):
    return v[:, h * HEAD_DIM:(h + 1) * HEAD_DIM]


def _attn_fwd(tag, q, k, proj, nb, s):
    t = q.shape[0]
    tq = _tile(s, 256)
    nq = s // tq
    v_off = (3 * CONV_WIDTH + ATTN_WIDTH + KV_WIDTH) // KV_WIDTH

    def body(q_ref, k_ref, v_ref, o_ref, lse_ref):
        qv = q_ref[...]
        kv = k_ref[...]
        vv = v_ref[...]
        for h in range(N_Q_HEADS):
            j = h // Q_PER_KV
            sc = _dot(_head(qv, h), _head(kv, j), "nt")
            m = jnp.max(sc, axis=-1, keepdims=True)
            e = jnp.exp(sc - m)
            l = jnp.sum(e, axis=-1, keepdims=True)
            o = _dot(e, _head(vv, j)) * (1.0 / l)
            o_ref[:, h * HEAD_DIM:(h + 1) * HEAD_DIM] = o.astype(o_ref.dtype)
            lse_ref[:, h:h + 1] = m + jnp.log(l)

    return pl.pallas_call(
        body, name=tag + "_attn", grid=(nb, nq),
        in_specs=[pl.BlockSpec((tq, ATTN_WIDTH), lambda b, i: (b * nq + i, 0)),
                  pl.BlockSpec((s, KV_WIDTH), lambda b, i: (b, 0)),
                  pl.BlockSpec((s, KV_WIDTH), lambda b, i: (b, v_off))],
        out_specs=(pl.BlockSpec((tq, ATTN_WIDTH), lambda b, i: (b * nq + i, 0)),
                   pl.BlockSpec((tq, N_Q_HEADS), lambda b, i: (b * nq + i, 0))),
        out_shape=(jax.ShapeDtypeStruct((t, ATTN_WIDTH), _ACT), jax.ShapeDtypeStruct((t, N_Q_HEADS), F32)),
        compiler_params=_params("parallel", "parallel"),
    )(q, k, proj)


def _attn_bwd(tag, q, k, proj, o, lse, d_mix, nb, s):
    t = q.shape[0]
    tq = _tile(s, 256)
    nq = s // tq
    v_off = (3 * CONV_WIDTH + ATTN_WIDTH + KV_WIDTH) // KV_WIDTH

    def body(q_ref, k_ref, v_ref, o_ref, lse_ref, do_ref, dq_ref, dk_ref, dv_ref):
        i = pl.program_id(1)

        @pl.when(i == 0)
        def _():
            dk_ref[...] = jnp.zeros_like(dk_ref)
            dv_ref[...] = jnp.zeros_like(dv_ref)

        qv = q_ref[...]
        kv = k_ref[...]
        vv = v_ref[...]
        ov = o_ref[...].astype(F32)
        dov = do_ref[...]
        lse = lse_ref[...]
        for h in range(N_Q_HEADS):
            j = h // Q_PER_KV
            cols = slice(j * HEAD_DIM, (j + 1) * HEAD_DIM)
            qh = _head(qv, h)
            kj = _head(kv, j)
            doh = _head(dov, h)
            sc = _dot(qh, kj, "nt")
            p = jnp.exp(sc - lse[:, h:h + 1])
            dp = _dot(doh, _head(vv, j), "nt")
            delta = jnp.sum(doh * _head(ov, h), axis=-1, keepdims=True)
            ds = p * (dp - delta)
            dv_ref[:, cols] += _dot(p, doh, "tn")
            dk_ref[:, cols] += _dot(ds, qh, "tn")
            dq_ref[:, h * HEAD_DIM:(h + 1) * HEAD_DIM] = _dot(ds, kj)

    qrow = pl.BlockSpec((tq, ATTN_WIDTH), lambda b, i: (b * nq + i, 0))
    kvrow = pl.BlockSpec((s, KV_WIDTH), lambda b, i: (b, 0))
    return pl.pallas_call(
        body, name=tag + "_dattn", grid=(nb, nq),
        in_specs=[qrow, kvrow, pl.BlockSpec((s, KV_WIDTH), lambda b, i: (b, v_off)), qrow,
                  pl.BlockSpec((tq, N_Q_HEADS), lambda b, i: (b * nq + i, 0)),
                  pl.BlockSpec((tq, ATTN_WIDTH), lambda b, i: (b * nq + i, 1))],
        out_specs=(qrow, kvrow, kvrow),
        out_shape=(jax.ShapeDtypeStruct((t, ATTN_WIDTH), F32), jax.ShapeDtypeStruct((t, KV_WIDTH), F32),
                   jax.ShapeDtypeStruct((t, KV_WIDTH), F32)),
        compiler_params=_params("parallel", "arbitrary"),
    )(q, k, proj, o, lse, d_mix)


def _even_fwd(tag, x, p, cos, sin, nb, s):
    h = _rmsnorm_fwd(tag + "_norm", x, p["norm"])
    proj = _proj_in(tag, h, p["w_in"])
    a = _conv_fwd(tag, proj, p["conv_w"], nb, s)
    q, k = _qk_fwd(tag, proj, p["q_gain"], p["k_gain"], cos, sin, nb, s)
    o, lse = _attn_fwd(tag, q, k, proj, nb, s)
    x_out = _proj_out(tag, x, [a, o], p["w_out"])
    return x_out, (x, h, proj, a, q, k, o, lse)


def _even_bwd(tag, dxo, saved, p, cos, sin, nb, s, mid, ready):
    x, h, proj, a, q, k, o, lse = saved
    d_mix, dw_out = _proj_out_bwd(tag, dxo, [a, o], p["w_out"])
    dgb, dgc, dhc, dconv_w = _conv_bwd(tag, proj, p["conv_w"] + mid(d_mix), d_mix, nb, s)
    dq_rot, dk_rot, dv = _attn_bwd(tag, q, k, proj, o, lse, d_mix, nb, s)
    dq, dk, dq_gain, dk_gain = _qk_bwd(tag, proj, p["q_gain"], p["k_gain"], cos, sin, dq_rot, dk_rot, nb, s)
    dx, dnorm = _proj_in_bwd(tag, h, [dgb, dgc, dhc, dq, dk, dv], p["w_in"], x, dxo, p["norm"],
                             lambda dw_in: ready(dict(w_in=dw_in, w_out=dw_out), dw_in))
    return dx, dict(norm=dnorm, conv_w=dconv_w, q_gain=dq_gain, k_gain=dk_gain)


def _window(pad_ref, val, r, s):
    pad_ref[pl.ds(HALO, s), :] = val
    acc = val
    for d in range(1, r + 1):
        acc = acc + pad_ref[pl.ds(HALO - d, s), :] + pad_ref[pl.ds(HALO + d, s), :]
    return acc


def _count(r, s):
    t = lax.broadcasted_iota(jnp.int32, (s, 1), 0)
    return (jnp.minimum(t + r, s - 1) - jnp.maximum(t - r, 0) + 1).astype(F32)


def _sgu_chunk(u_ref, v_ref, norm, ws_ref, bt, rows):
    uu = u_ref[rows, :].astype(F32)
    vv = v_ref[rows, :].astype(F32)
    gu = _gelu(uu)
    gv = _gelu(vv)
    r = lax.rsqrt(jnp.mean(gv * gv, axis=-1, keepdims=True) + EPS)
    xhat = gv * r
    vn = xhat * norm
    mixed = []
    for g in range(N_GROUPS):
        cols = slice(g * SGU_GROUP, (g + 1) * SGU_GROUP)
        mixed.append(_dot(ws_ref[g], vn[:, cols]) + bt[:, g:g + 1])
    return uu, vv, gu, r, xhat, vn, mixed


def _odd_core_fwd(tag, proj, p, nb, s):
    t = proj.shape[0]
    nchunk = s // SGU_CHUNK

    def body(p_ref, u_ref, v_ref, pw_ref, ps_ref, sn_ref, ws_ref, bt_ref, mix_ref, pad_ref):
        _zero_halo(pad_ref, s)
        for g, r in enumerate(POOL_RADII):
            cols = slice(g * POOL_GROUP, (g + 1) * POOL_GROUP)
            pg = p_ref[:, cols].astype(F32)
            pooled = _window(pad_ref, pg, r, s) / _count(r, s) - pg
            mix_ref[:, cols] = (_dot(pooled, pw_ref[g]) * ps_ref[:, cols]).astype(mix_ref.dtype)
        norm = sn_ref[...]
        bt = bt_ref[...]

        def chunk(n, carry):
            rows = pl.ds(pl.multiple_of(n * SGU_CHUNK, SGU_CHUNK), SGU_CHUNK)
            _, _, gu, _, _, _, mixed = _sgu_chunk(u_ref, v_ref, norm, ws_ref, bt, rows)
            for g in range(N_GROUPS):
                cols = slice(g * SGU_GROUP, (g + 1) * SGU_GROUP)
                mix_ref[rows, HALF + g * SGU_GROUP:HALF + (g + 1) * SGU_GROUP] = (
                    gu[:, cols] * mixed[g]).astype(mix_ref.dtype)
            return carry

        lax.fori_loop(0, nchunk, chunk, 0)

    def col(j):
        return pl.BlockSpec((s, HALF), lambda b: (b, j))

    def whole(a):
        return pl.BlockSpec(a.shape, lambda b: (0,) * a.ndim)

    consts = [p["pool_w"], p["pool_scale"].reshape(1, HALF), p["sgu_norm"].reshape(1, HALF),
              p["sgu_w"], p["sgu_b"].T]
    return pl.pallas_call(
        body, name=tag + "_core", grid=(nb,),
        in_specs=[col(0), col(1), col(2)] + [whole(a) for a in consts],
        out_specs=pl.BlockSpec((s, D_MODEL), lambda b: (b, 0)),
        out_shape=jax.ShapeDtypeStruct((t, D_MODEL), _ACT),
        scratch_shapes=[pltpu.VMEM((s + 2 * HALO, POOL_GROUP), F32)],
        compiler_params=_params("parallel"),
    )(proj, proj, proj, *consts)


def _odd_core_bwd(tag, proj, p, d_mix, nb, s):
    t = proj.shape[0]
    nchunk = s // SGU_CHUNK

    def body(p_ref, u_ref, v_ref, pw_ref, ps_ref, sn_ref, ws_ref, bt_ref, dm_ref,
             dproj_ref, dpw_ref, dps_ref, dsn_ref, dws_ref, dbt_ref, pad_ref):
        b = pl.program_id(0)

        @pl.when(b == 0)
        def _():
            dpw_ref[...] = jnp.zeros_like(dpw_ref)
            dps_ref[...] = jnp.zeros_like(dps_ref)
            dsn_ref[...] = jnp.zeros_like(dsn_ref)
            dws_ref[...] = jnp.zeros_like(dws_ref)
            dbt_ref[...] = jnp.zeros_like(dbt_ref)

        _zero_halo(pad_ref, s)
        for g, r in enumerate(POOL_RADII):
            cols = slice(g * POOL_GROUP, (g + 1) * POOL_GROUP)
            pg = p_ref[:, cols].astype(F32)
            cnt = _count(r, s)
            pooled = _window(pad_ref, pg, r, s) / cnt - pg
            c_pre = _dot(pooled, pw_ref[g])
            dc = dm_ref[:, cols]
            dps_ref[:, cols] += jnp.sum(dc * c_pre, axis=0, keepdims=True)
            dcp = dc * ps_ref[:, cols]
            dpw_ref[g] += _dot(pooled, dcp, "tn")
            dpooled = _dot(dcp, pw_ref[g], "nt")
            dproj_ref[:, cols] = (_window(pad_ref, dpooled / cnt, r, s) - dpooled).astype(dproj_ref.dtype)
        norm = sn_ref[...]
        bt = bt_ref[...]

        def chunk(n, carry):
            rows = pl.ds(pl.multiple_of(n * SGU_CHUNK, SGU_CHUNK), SGU_CHUNK)
            uu, vv, gu, r, xhat, vn, mixed = _sgu_chunk(u_ref, v_ref, norm, ws_ref, bt, rows)
            dd = dm_ref[rows, HALF:D_MODEL]
            dgu, dvn = [], []
            for g in range(N_GROUPS):
                cols = slice(g * SGU_GROUP, (g + 1) * SGU_GROUP)
                dgu.append(dd[:, cols] * mixed[g])
                dmx = dd[:, cols] * gu[:, cols]
                dbt_ref[:, g:g + 1] += jnp.sum(dmx, axis=-1, keepdims=True)
                dws_ref[g] += _dot(dmx, vn[:, cols], "nt")
                dvn.append(_dot(ws_ref[g], dmx, "tn"))
            dgu = jnp.concatenate(dgu, axis=1)
            dvn = jnp.concatenate(dvn, axis=1)
            dsn_ref[...] += jnp.sum(dvn * xhat, axis=0, keepdims=True)
            dgx = dvn * norm
            m = jnp.mean(dgx * xhat, axis=-1, keepdims=True)
            dgv = r * (dgx - xhat * m)
            dproj_ref[rows, HALF:2 * HALF] = (dgu * _gelu_grad(uu)).astype(dproj_ref.dtype)
            dproj_ref[rows, 2 * HALF:3 * HALF] = (dgv * _gelu_grad(vv)).astype(dproj_ref.dtype)
            return carry

        lax.fori_loop(0, nchunk, chunk, 0)

    def col(j):
        return pl.BlockSpec((s, HALF), lambda b: (b, j))

    def whole(a):
        return pl.BlockSpec(a.shape, lambda b: (0,) * a.ndim)

    consts = [p["pool_w"], p["pool_scale"].reshape(1, HALF), p["sgu_norm"].reshape(1, HALF),
              p["sgu_w"], p["sgu_b"].T]
    gshapes = [jax.ShapeDtypeStruct(a.shape, F32) for a in consts]
    dproj, dpw, dps, dsn, dws, dbt = pl.pallas_call(
        body, name=tag + "_dcore", grid=(nb,),
        in_specs=[col(0), col(1), col(2)] + [whole(a) for a in consts]
        + [pl.BlockSpec((s, D_MODEL), lambda b: (b, 0))],
        out_specs=[pl.BlockSpec((s, 3 * HALF), lambda b: (b, 0))] + [whole(a) for a in consts],
        out_shape=[jax.ShapeDtypeStruct((t, 3 * HALF), _ACT)] + gshapes,
        scratch_shapes=[pltpu.VMEM((s + 2 * HALO, POOL_GROUP), F32)],
        compiler_params=_params("arbitrary"),
    )(proj, proj, proj, *consts, d_mix)
    return dproj, dict(pool_w=dpw, pool_scale=dps.reshape(HALF), sgu_norm=dsn.reshape(HALF), sgu_w=dws, sgu_b=dbt.T)


def _odd_fwd(tag, x, p, nb, s):
    h = _rmsnorm_fwd(tag + "_norm", x, p["norm"])
    proj = _proj_in(tag, h, p["w_in"])
    mix = _odd_core_fwd(tag, proj, p, nb, s)
    x_out = _proj_out(tag, x, [mix], p["w_out"])
    return x_out, (x, h, proj, mix)


def _odd_bwd(tag, dxo, saved, p, nb, s, mid, ready):
    x, h, proj, mix = saved
    d_mix, dw_out = _proj_out_bwd(tag, dxo, [mix], p["w_out"])
    p = dict(p, pool_scale=p["pool_scale"] + mid(d_mix))
    dproj, grads = _odd_core_bwd(tag, proj, p, d_mix, nb, s)
    dx, dnorm = _proj_in_bwd(tag, h, [dproj], p["w_in"], x, dxo, p["norm"],
                             lambda dw_in: ready(dict(w_in=dw_in, w_out=dw_out), dw_in))
    grads.update(norm=dnorm)
    return dx, grads


def _local_step(x3, target3, depth, weights_of, final_norm, mid, grads_ready, small_done):
    nb, s, d = x3.shape
    t = nb * s
    x = x3.reshape(t, d)
    target = target3.reshape(t, d)
    cos, sin = _rope_tables(s)
    saved, ws = [], []
    for l in range(depth):
        w1 = weights_of(l, "ffn1", x)
        x, s1 = _ffn_fwd(f"l{l}_ffn1", x, w1["norm"], w1["w_in4"], w1["w_out"], _ffn_tiles(l, 1))
        wm = weights_of(l, "mix", x)
        if l % 2 == 0:
            x, s2 = _even_fwd(f"l{l}_ev", x, wm, cos, sin, nb, s)
        else:
            x, s2 = _odd_fwd(f"l{l}_od", x, wm, nb, s)
        w2 = weights_of(l, "ffn2", x)
        x, s3 = _ffn_fwd(f"l{l}_ffn2", x, w2["norm"], w2["w_in4"], w2["w_out"], _ffn_tiles(l, 2))
        saved.append((s1, s2, s3))
        ws.append((w1, wm, w2))
    loss, dx, dfinal = _final_loss("final_loss", x, final_norm, target)
    for l in reversed(range(depth)):
        s1, s2, s3 = saved[l]
        w1, wm, w2 = ws[l]

        def ready(block):
            return lambda grads, a: grads_ready(l, block, grads, a)

        dx, dn = _ffn_bwd(f"l{l}_ffn2", dx, s3, w2["norm"], w2["w_in4"], w2["w_out"], _ffn_tiles(l, 2), mid, ready("ffn2"))
        small_done(l, "ffn2", dict(norm=dn))
        if l % 2 == 0:
            dx, gm = _even_bwd(f"l{l}_ev", dx, s2, wm, cos, sin, nb, s, mid, ready("mix"))
        else:
            dx, gm = _odd_bwd(f"l{l}_od", dx, s2, wm, nb, s, mid, ready("mix"))
        small_done(l, "mix", gm)
        dx, dn = _ffn_bwd(f"l{l}_ffn1", dx, s1, w1["norm"], w1["w_in4"], w1["w_out"], _ffn_tiles(l, 1), mid, ready("ffn1"))
        small_done(l, "ffn1", dict(norm=dn))
    return loss, dx.reshape(nb, s, d), dfinal


_HBM = pl.BlockSpec(memory_space=pltpu.HBM)


def _place():
    x, y, c = lax.axis_index("x"), lax.axis_index("y"), lax.axis_index("c")
    chips = [(1 - x, y), (x, 1 - y), (1 - x, 1 - y)]
    return x, y, c, chips


def _remote(src, dst, send_sem, recv_sem, to):
    return pltpu.make_async_remote_copy(src_ref=src, dst_ref=dst, send_sem=send_sem, recv_sem=recv_sem,
                                        device_id=to, device_id_type=_MESH)


def _gather_shards(arrs, small):
    n = len(arrs)
    own = 6

    def body(*refs):
        ins, sm_in = refs[:n], refs[n]
        outs, sm_out = refs[n + 1:2 * n + 1], refs[2 * n + 1]
        send, recv = refs[2 * n + 2:]
        x, y, c, chips = _place()
        k = 2 * x + y
        sib = (x, y, 1 - c)
        started = []
        for a in range(n + 1):
            src, dst = (ins[a], outs[a]) if a < n else (sm_in, sm_out)
            cp = _remote(src, dst.at[k], send.at[a, own], recv.at[a, own], sib)
            cp.start()
            started.append(cp)
            if a < n:
                h = src.shape[0] // 2
                mine = pl.ds(c * h, h)
                src_part, dst_part = src.at[mine], dst.at[k, mine]
            else:
                src_part, dst_part = src, dst.at[k]
            for j, chip in enumerate(chips):
                cp = _remote(src_part, dst_part, send.at[a, j], recv.at[a, j], (*chip, c))
                cp.start()
                started.append(cp)
        for a in range(n):
            h = ins[a].shape[0] // 2
            mine = pl.ds(c * h, h)
            for j, (px, py) in enumerate(chips):
                landed = outs[a].at[2 * px + py, mine]
                _remote(landed, landed, send.at[a, j], recv.at[a, j], (px, py, c)).wait_recv()
                cp = _remote(landed, landed, send.at[a, 3 + j], recv.at[a, 3 + j], sib)
                cp.start()
                started.append(cp)
        for a in range(n):
            h = ins[a].shape[0] // 2
            other = pl.ds((1 - c) * h, h)
            for j, (px, py) in enumerate(chips):
                passed = outs[a].at[2 * px + py, other]
                _remote(passed, passed, send.at[a, 3 + j], recv.at[a, 3 + j], sib).wait_recv()
        for j, (px, py) in enumerate(chips):
            landed = sm_out.at[2 * px + py]
            _remote(landed, landed, send.at[n, j], recv.at[n, j], (px, py, c)).wait_recv()
        for a in range(n + 1):
            filled = (outs[a] if a < n else sm_out).at[k]
            _remote(filled, filled, send.at[a, own], recv.at[a, own], sib).wait_recv()
        for cp in started:
            cp.wait_send()

    outs = pl.pallas_call(
        body, name="gather_shards",
        in_specs=[_HBM] * (n + 1), out_specs=[_HBM] * (n + 1),
        out_shape=[jax.ShapeDtypeStruct((N_CHIPS,) + a.shape, a.dtype) for a in list(arrs) + [small]],
        scratch_shapes=[pltpu.SemaphoreType.DMA((n + 1, 7)), pltpu.SemaphoreType.DMA((n + 1, 7))],
    )(*arrs, small)
    return outs[:n], outs[n]


_SEM = pl.BlockSpec(memory_space=pltpu.SEMAPHORE)
_EFFECT = pltpu.SideEffectType.DATAFLOW_SIDE_EFFECTING


def _gather_plan(i, src, land, k, c, chips, sib):
    mine = pl.ds(c * (src.shape[0] // 2), src.shape[0] // 2)
    plan = [(src.at[mine], land.at[k, mine], (px, py, c), land.at[2 * px + py, mine]) for px, py in chips]
    return plan + [(src, land.at[k], sib, land.at[k])]


def _pass_plan(i, src, land, k, c, chips, sib):
    h = src.shape[1] // 2
    mine, theirs = pl.ds(c * h, h), pl.ds((1 - c) * h, h)
    return [(src.at[2 * px + py, mine], land.at[2 * px + py, mine], sib, land.at[2 * px + py, theirs]) for px, py in chips]


def _swap_plan(i, src, land, k, c, chips, sib):
    h = src.shape[1] // 2
    return [(src.at[:, pl.ds((1 - c) * h, h)], land, sib, land)]


def _scatter_plan(i, src, land, k, c, chips, sib):
    return [(src.at[2 * px + py], land.at[k], (px, py, c), land.at[2 * px + py]) for px, py in chips]


def _join_plan(layers):
    def plan(i, src, land, k, c, chips, sib):
        h = src.shape[1] // 2
        mine, theirs = pl.ds(c * h, h), pl.ds((1 - c) * h, h)
        return [(src.at[layers[i], mine], land.at[layers[i], mine], sib, land.at[layers[i], theirs])]
    return plan


def _split_start(name, plan, ncopy, srcs, after, land_shapes=None):
    n = len(srcs)
    if land_shapes is None:
        land_shapes = [(N_CHIPS,) + a.shape[-2:] for a in srcs]
    in_place = land_shapes == "self"
    lands = [] if in_place else [pltpu.with_memory_space_constraint(lax.empty(shape, a.dtype), pltpu.HBM)
                                 for shape, a in zip(land_shapes, srcs)]
    nbuf = n + len(lands)

    def body(*refs):
        src_refs = refs[1:1 + n]
        land_refs = src_refs if in_place else refs[1 + n:1 + nbuf]
        send, recv, token = refs[1 + nbuf], refs[2 + nbuf], refs[-1]
        x, y, c, chips = _place()
        for i in range(n):
            for j, (src, dst, peer, _) in enumerate(plan(i, src_refs[i], land_refs[i], 2 * x + y, c, chips, (x, y, 1 - c))):
                _remote(src, dst, send.at[i * ncopy + j], recv.at[i * ncopy + j], peer).start()
        token[...] = jnp.zeros_like(token)

    outs = pl.pallas_call(
        body, name=name,
        in_specs=[_ANY] + [_HBM] * nbuf,
        out_specs=[_SEM, _SEM] + [_HBM] * nbuf + [_VMEM],
        out_shape=[pltpu.SemaphoreType.DMA((n * ncopy,)), pltpu.SemaphoreType.DMA((n * ncopy,))]
        + [pltpu.HBM(a.shape, a.dtype) for a in list(srcs) + lands] + [jax.ShapeDtypeStruct((8, LANES), F32)],
        input_output_aliases={1 + i: 2 + i for i in range(nbuf)},
        compiler_params=pltpu.CompilerParams(has_side_effects=_EFFECT),
    )(after, *[pltpu.with_memory_space_constraint(a, pltpu.HBM) for a in srcs], *lands)
    return outs[0], outs[1], outs[2:2 + n], None if in_place else outs[2 + n:2 + nbuf], outs[-1]


def _split_wait(name, plan, started, after):
    send, recv, srcs, lands = started
    n = len(srcs)
    ncopy = send.shape[0] // n
    in_place = lands is None
    bufs = list(srcs) + ([] if in_place else list(lands))
    nbuf = len(bufs)

    def body(*refs):
        src_refs = refs[:n]
        land_refs = src_refs if in_place else refs[n:nbuf]
        send, recv = refs[nbuf], refs[nbuf + 1]
        x, y, c, chips = _place()
        for i in range(n):
            for j, (src, _, peer, landed) in enumerate(plan(i, src_refs[i], land_refs[i], 2 * x + y, c, chips, (x, y, 1 - c))):
                cp = _remote(src, landed, send.at[i * ncopy + j], recv.at[i * ncopy + j], peer)
                cp.wait_send()
                cp.wait_recv()

    outs = pl.pallas_call(
        body, name=name,
        in_specs=[_HBM] * nbuf + [_SEM, _SEM, _ANY],
        out_specs=[_HBM] * nbuf,
        out_shape=[pltpu.HBM(a.shape, a.dtype) for a in bufs],
        input_output_aliases={i: i for i in range(nbuf)},
        compiler_params=pltpu.CompilerParams(has_side_effects=_EFFECT),
    )(*bufs, send, recv, after)
    return outs[:n], outs[:n] if in_place else outs[n:]


def _allreduce_small(buf, after):
    rows = buf.shape[0]
    piece = rows // N_DEV

    def body(in_ref, after_ref, out_ref, land_ref, send, recv):
        x, y, c, _ = _place()
        me = 4 * x + 2 * y + c
        peers = [(1 - x if r & 4 else x, 1 - y if r & 2 else y, 1 - c if r & 1 else c) for r in range(1, N_DEV)]

        def rows_of(dev):
            return pl.ds(pl.multiple_of(dev * piece, 8), piece)

        first, second = [], []
        for r, (px, py, pc) in enumerate(peers):
            cp = _remote(in_ref.at[rows_of(4 * px + 2 * py + pc)], land_ref.at[me], send.at[0, r], recv.at[0, r], (px, py, pc))
            cp.start()
            first.append(cp)
        land_ref[me] = in_ref[rows_of(me), :]
        for r, (px, py, pc) in enumerate(peers):
            landed = land_ref.at[4 * px + 2 * py + pc]
            _remote(landed, landed, send.at[0, r], recv.at[0, r], (px, py, pc)).wait_recv()
        acc = land_ref[0]
        for d in range(1, N_DEV):
            acc = acc + land_ref[d]
        out_ref[rows_of(me), :] = acc
        for r, peer in enumerate(peers):
            cp = _remote(out_ref.at[rows_of(me)], out_ref.at[rows_of(me)], send.at[1, r], recv.at[1, r], peer)
            cp.start()
            second.append(cp)
        for r, (px, py, pc) in enumerate(peers):
            landed = out_ref.at[rows_of(4 * px + 2 * py + pc)]
            _remote(landed, landed, send.at[1, r], recv.at[1, r], (px, py, pc)).wait_recv()
        for cp in first + second:
            cp.wait_send()

    return pl.pallas_call(
        body, name="allreduce_small",
        in_specs=[_VMEM, _ANY], out_specs=_VMEM,
        out_shape=jax.ShapeDtypeStruct(buf.shape, F32),
        scratch_shapes=[pltpu.VMEM((N_DEV, piece, LANES), F32), pltpu.SemaphoreType.DMA((2, N_DEV - 1)),
                        pltpu.SemaphoreType.DMA((2, N_DEV - 1))],
        compiler_params=pltpu.CompilerParams(vmem_limit_bytes=_VMEM_LIMIT),
    )(buf, after)


def _div_tile(n, cap, mult):
    best = None
    for d in range(mult, min(n, cap) + 1, mult):
        if n % d == 0:
            best = d
    return best if best is not None else n


def _add_sibling(name, grad, got, c):
    nk, hr, cc = got.shape
    tr = _div_tile(hr, 512, 16)
    nt = hr // tr

    def body(c_ref, g_ref, o_ref, s_ref):
        s_ref[...] = (g_ref[...].astype(F32) + o_ref[...].astype(F32)).astype(s_ref.dtype)

    blk = (None, tr, cc)
    return pl.pallas_call(
        body, name=name,
        grid_spec=pltpu.PrefetchScalarGridSpec(
            num_scalar_prefetch=1, grid=(nk, nt),
            in_specs=[pl.BlockSpec(blk, lambda i, q, c_ref: (i, c_ref[0] * nt + q, 0)),
                      pl.BlockSpec(blk, lambda i, q, c_ref: (i, q, 0))],
            out_specs=pl.BlockSpec(blk, lambda i, q, c_ref: (i, q, 0))),
        out_shape=jax.ShapeDtypeStruct(got.shape, got.dtype),
        compiler_params=_params("parallel", "parallel"),
    )(c, grad, got)


def _add_chips(name, mine, got, place, buf, l):
    nk, hr, cc = got.shape
    tr = _div_tile(hr, 512, 16)
    nt = hr // tr

    def body(*refs):
        acc = refs[1][...].astype(F32)
        for q in range(1, nk):
            acc = acc + refs[1 + q][...].astype(F32)
        refs[2 + nk][...] = acc

    def part(q):
        return pl.BlockSpec((None, tr, cc), lambda i, p_ref: ((p_ref[0] + q) % nk, i, 0))

    return pl.pallas_call(
        body, name=name,
        grid_spec=pltpu.PrefetchScalarGridSpec(
            num_scalar_prefetch=1, grid=(nt,),
            in_specs=[part(q) for q in range(nk)] + [_ANY],
            out_specs=pl.BlockSpec((None, tr, cc), lambda i, p_ref: (l, p_ref[1] * nt + i, 0))),
        out_shape=jax.ShapeDtypeStruct(buf.shape, F32),
        input_output_aliases={1 + nk: 0},
        compiler_params=_params("parallel"),
    )(place, mine, *([got] * (nk - 1)), buf)


def _adamw(name, w, g, m, v, after=None):
    shape = w.shape
    cols = shape[-1]
    rows = w.size // cols
    tr = rows if rows * cols <= 2 ** 18 else _div_tile(rows, max(8, 2 ** 18 // cols), 8)
    c1 = 1.0 - ADAM_B1 ** ADAM_STEP
    c2 = 1.0 - ADAM_B2 ** ADAM_STEP
    extra = [] if after is None else [after]

    def body(*refs):
        w_ref, g_ref, m_ref, v_ref = refs[:4]
        d_ref, mo_ref, vo_ref, go_ref = refs[4 + len(extra):]
        gg = g_ref[...]
        mn = ADAM_B1 * m_ref[...] + (1.0 - ADAM_B1) * gg
        vn = ADAM_B2 * v_ref[...] + (1.0 - ADAM_B2) * (gg * gg)
        d_ref[...] = -ADAM_LR * ((mn / c1) / (jnp.sqrt(vn / c2) + ADAM_EPS) + ADAM_WD * w_ref[...])
        mo_ref[...] = mn
        vo_ref[...] = vn
        go_ref[...] = gg

    blk = pl.BlockSpec((tr, cols), lambda i: (i, 0))
    sds = jax.ShapeDtypeStruct((rows, cols), F32)
    outs = pl.pallas_call(
        body, name=name, grid=(rows // tr,),
        in_specs=[blk] * 4 + [_ANY] * len(extra), out_specs=(blk,) * 4, out_shape=(sds,) * 4,
        compiler_params=_params("parallel"),
    )(*[a.reshape(rows, cols) for a in (w, g, m, v)], *extra)
    return [o.reshape(shape) for o in outs]


_WEIGHTS = ["ffn1_norm", "ffn1_w_in", "ffn1_w_out", "mix_norm", "ffn2_norm", "ffn2_w_in", "ffn2_w_out",
            "ev_w_in", "ev_conv_w", "ev_q_norm", "ev_k_norm", "ev_w_out", "od_w_in", "od_pool_w",
            "od_pool_scale", "od_sgu_norm", "od_sgu_w", "od_sgu_b", "od_w_out", "final_norm"]
_BIG = ["ffn1_w_in", "ffn1_w_out", "ffn2_w_in", "ffn2_w_out", "ev_w_in", "ev_w_out", "od_w_in", "od_w_out"]
_SMALL_SHARDED = ["ev_conv_w", "od_pool_scale", "od_sgu_norm"]


def _pad_rows(a, mult=8):
    pad = (-a.shape[0]) % mult
    return a if pad == 0 else jnp.concatenate([a, jnp.zeros((pad,) + a.shape[1:], a.dtype)], axis=0)


def _join_cols(g):
    return g.transpose(1, 0, 2).reshape(g.shape[1], N_CHIPS * g.shape[2])


def _split_cols(w):
    return w.reshape(w.shape[0], N_CHIPS, w.shape[1] // N_CHIPS).transpose(1, 0, 2)


def kernel(x, ffn1_norm, ffn1_w_in, ffn1_w_out, mix_norm, ffn2_norm, ffn2_w_in, ffn2_w_out, ev_w_in, ev_conv_w,
           ev_q_norm, ev_k_norm, ev_w_out, od_w_in, od_pool_w, od_pool_scale, od_sgu_norm, od_sgu_w, od_sgu_b,
           od_w_out, final_norm, loss_target, m_ffn1_norm, m_ffn1_w_in, m_ffn1_w_out, m_mix_norm, m_ffn2_norm,
           m_ffn2_w_in, m_ffn2_w_out, m_ev_w_in, m_ev_conv_w, m_ev_q_norm, m_ev_k_norm, m_ev_w_out, m_od_w_in,
           m_od_pool_w, m_od_pool_scale, m_od_sgu_norm, m_od_sgu_w, m_od_sgu_b, m_od_w_out, m_final_norm, v_ffn1_norm,
           v_ffn1_w_in, v_ffn1_w_out, v_mix_norm, v_ffn2_norm, v_ffn2_w_in, v_ffn2_w_out, v_ev_w_in, v_ev_conv_w,
           v_ev_q_norm, v_ev_k_norm, v_ev_w_out, v_od_w_in, v_od_pool_w, v_od_pool_scale, v_od_sgu_norm, v_od_sgu_w,
           v_od_sgu_b, v_od_w_out, v_final_norm):
    return _step(x, ffn1_norm, ffn1_w_in, ffn1_w_out, mix_norm, ffn2_norm, ffn2_w_in, ffn2_w_out, ev_w_in, ev_conv_w,
                 ev_q_norm, ev_k_norm, ev_w_out, od_w_in, od_pool_w, od_pool_scale, od_sgu_norm, od_sgu_w, od_sgu_b,
                 od_w_out, final_norm, loss_target, m_ffn1_norm, m_ffn1_w_in, m_ffn1_w_out, m_mix_norm, m_ffn2_norm,
                 m_ffn2_w_in, m_ffn2_w_out, m_ev_w_in, m_ev_conv_w, m_ev_q_norm, m_ev_k_norm, m_ev_w_out, m_od_w_in,
                 m_od_pool_w, m_od_pool_scale, m_od_sgu_norm, m_od_sgu_w, m_od_sgu_b, m_od_w_out, m_final_norm,
                 v_ffn1_norm, v_ffn1_w_in, v_ffn1_w_out, v_mix_norm, v_ffn2_norm, v_ffn2_w_in, v_ffn2_w_out,
                 v_ev_w_in, v_ev_conv_w, v_ev_q_norm, v_ev_k_norm, v_ev_w_out, v_od_w_in, v_od_pool_w,
                 v_od_pool_scale, v_od_sgu_norm, v_od_sgu_w, v_od_sgu_b, v_od_w_out, v_final_norm)


def _step(*args):
    nw = len(_WEIGHTS)
    x = args[0]
    w = dict(zip(_WEIGHTS, args[1:1 + nw]))
    target = args[1 + nw]
    m = dict(zip(_WEIGHTS, args[2 + nw:2 + 2 * nw]))
    v = dict(zip(_WEIGHTS, args[2 + 2 * nw:2 + 3 * nw]))
    depth = w["ffn1_norm"].shape[0]
    n_even, n_odd = w["ev_w_in"].shape[0], w["od_w_in"].shape[0]
    chip = 2 * lax.axis_index("x") + lax.axis_index("y")
    place = jnp.stack([chip, lax.axis_index("c")]).astype(jnp.int32)
    core = place[1:2]

    def sharded(l, block):
        if block == "mix":
            block = "ev" if l % 2 == 0 else "od"
            return [(block + "_w_in", l // 2), (block + "_w_out", l // 2)]
        return [(block + "_w_in", l), (block + "_w_out", l)]

    def shards(group):
        return [w[n][i].astype(_ACT) for l, block in group for n, i in sharded(l, block)]

    small_rows = [w["ev_conv_w"].reshape(3 * n_even, LANES), w["od_pool_scale"], w["od_sgu_norm"]]
    first, small = _gather_shards(shards([(0, "ffn1")]), _pad_rows(jnp.concatenate(small_rows, axis=0)))
    conv_w = small[:, :3 * n_even].reshape(N_CHIPS, n_even, 3, LANES).transpose(1, 2, 0, 3).reshape(n_even, 3, CONV_WIDTH)
    pool_scale = small[:, 3 * n_even:3 * n_even + n_odd].transpose(1, 0, 2).reshape(n_odd, HALF)
    sgu_norm = small[:, 3 * n_even + n_odd:3 * n_even + 2 * n_odd].transpose(1, 0, 2).reshape(n_odd, HALF)
    later = [[(0, "mix"), (0, "ffn2")]] + [[(l, "ffn1"), (l, "mix"), (l, "ffn2")] for l in range(1, depth)]
    gathering, after = [], small
    for i, group in enumerate(later):
        gathering.append(_split_start(f"gather_start{i}", _gather_plan, N_CHIPS, shards(group), after))
        after = gathering[-1][4]
    gathered = {(0, "ffn1"): first}

    def rows(g):
        return g.reshape(N_CHIPS * g.shape[1], g.shape[2])

    passing = {}

    def fetch(i, x_in):
        got = _split_wait(f"gather_wait{i}", _gather_plan, gathering[i][:4], x_in)[1]
        passing[i] = _split_start(f"pass_start{i}", _pass_plan, N_CHIPS - 1, got, x_in, "self")
        return passing[i][4][0, 0]

    def weights_of(l, block, x_in):
        zero = after[0, 0] if (l, block) == (0, "ffn1") else 0.0
        if (l, block) not in gathered:
            i = next(i for i, group in enumerate(later) if (l, block) in group)
            if i not in passing:
                zero = zero + fetch(i, x_in)
            got = _split_wait(f"pass_wait{i}", _pass_plan, passing.pop(i)[:4], x_in)[0]
            for n, key in enumerate(later[i]):
                gathered[key] = got[2 * n:2 * n + 2]
        if block == "ffn2" and l + 1 < depth:
            zero = zero + fetch(next(i for i, group in enumerate(later) if (l + 1, "ffn1") in group), x_in)
        w_in, w_out = gathered[(l, block)]
        if block != "mix":
            return dict(norm=w[block + "_norm"][l] + zero, w_in4=w_in, w_out=rows(w_out))
        j = l // 2
        if l % 2 == 0:
            mix = dict(conv_w=conv_w[j], q_gain=w["ev_q_norm"][j], k_gain=w["ev_k_norm"][j])
        else:
            mix = dict(pool_w=w["od_pool_w"][j], pool_scale=pool_scale[j], sgu_norm=sgu_norm[j],
                       sgu_w=w["od_sgu_w"][j], sgu_b=w["od_sgu_b"][j])
        return dict(mix, norm=w["mix_norm"][l] + zero, w_in=_join_cols(w_in), w_out=rows(w_out))

    def by_chip(dw):
        return dw.reshape(N_CHIPS, dw.shape[0] // N_CHIPS, dw.shape[1])

    bufs = {n: lax.empty(w[n].shape, F32) for n in _BIG}
    small_grads = {n: [None] * w[n].shape[0] for n in _WEIGHTS if n not in _BIG and n != "final_norm"}
    swapping, scattering, joining, group = [], [], [], []

    def finish_swap(after):
        tag, names, started = swapping.pop()
        local, from_sibling = _split_wait(f"swap_wait{tag}", _swap_plan, started[:4], after)
        halves = [_add_sibling(f"add_sibling{tag}_{n}", a, b, core) for (n, _), a, b in zip(names, local, from_sibling)]
        scattering.append((tag, names, _split_start(f"scatter_start{tag}", _scatter_plan, N_CHIPS - 1, halves, after)))
        return scattering[-1][2][4][0, 0]

    def finish_scatter(after):
        tag, names, started = scattering.pop(0)
        halves, got = _split_wait(f"scatter_wait{tag}", _scatter_plan, started[:4], after)
        for i, (n, j) in enumerate(names):
            bufs[n] = _add_chips(f"add_chips{tag}_{n}", halves[i], got[i], place, bufs[n], j)
        layers = [j for _, j in names]
        started = _split_start(f"join_start{tag}", _join_plan(layers), 1, [bufs[n] for n, _ in names], after, "self")
        for (n, _), b in zip(names, started[2]):
            bufs[n] = b
        joining.append((tag, names, layers, started))
        return started[4]

    def finish_join(after):
        tag, names, layers, started = joining.pop(0)
        joined = _split_wait(f"join_wait{tag}", _join_plan(layers), (started[0], started[1], [bufs[n] for n, _ in names], None),
                             after)[0]
        for (n, _), b in zip(names, joined):
            bufs[n] = b

    def small_done(l, block, g):
        if block == "mix":
            renamed = (dict(conv_w="ev_conv_w", q_gain="ev_q_norm", k_gain="ev_k_norm") if l % 2 == 0 else
                       dict(pool_w="od_pool_w", pool_scale="od_pool_scale", sgu_norm="od_sgu_norm", sgu_w="od_sgu_w",
                            sgu_b="od_sgu_b"))
            for key, n in renamed.items():
                small_grads[n][l // 2] = g[key]
        small_grads[block + "_norm"][l] = g["norm"]

    def mid(a):
        zero = 0.0
        if swapping:
            if scattering:
                zero = zero + finish_scatter(a)[0, 0]
            zero = zero + finish_swap(a)
        return zero

    def grads_ready(l, block, g, a):
        local = [_split_cols(g["w_in"]) if block == "mix" else g["w_in4"], by_chip(g["w_out"])]
        group.extend(zip(sharded(l, block), local))
        if block == "ffn2" or (block == "mix" and l > 0):
            return 0.0
        tag = f"{l}_{block}"
        names, local = [n for n, _ in group], [b for _, b in group]
        group.clear()
        shapes = [(N_CHIPS, b.shape[1] // 2, b.shape[2]) for b in local]
        swapping.append((tag, names, _split_start(f"swap_start{tag}", _swap_plan, 1, local, a, shapes)))
        return swapping[-1][2][4][0, 0]

    loss_part, grad_x, dfinal = _local_step(x, target, depth, weights_of, w["final_norm"], mid, grads_ready, small_done)
    loss = lax.psum(loss_part, ("x", "y", "c"))

    grads, updates = {}, {}

    def update(n, after):
        updates[n] = _adamw("adamw_" + n, w[n], grads[n] if n in grads else bufs[n], m[n], v[n], after)
        return updates[n][1]

    finish_swap(grad_x)
    behind = scattering[-1][2][4]
    while len(joining) > 0:
        finish_join(behind)
    behind = finish_scatter(behind)
    for n in ("od_w_in", "od_w_out"):
        behind = update(n, behind)
    finish_join(behind)
    for n in ("ffn2_w_in", "ffn2_w_out", "ev_w_in", "ev_w_out"):
        behind = update(n, behind)
    behind = finish_scatter(behind)
    small_grads = {n: jnp.stack(parts) for n, parts in small_grads.items()}
    small_grads["final_norm"] = dfinal
    names = list(small_grads)
    flat = jnp.concatenate([small_grads[n].reshape(-1) for n in names])
    total = flat.shape[0]
    flat = jnp.concatenate([flat, jnp.zeros((-total) % (N_DEV * 8 * LANES), F32)])
    summed = _allreduce_small(flat.reshape(-1, LANES), behind).reshape(-1)
    finish_join(summed)
    for n in ("ffn1_w_in", "ffn1_w_out"):
        behind = update(n, behind)
    off = 0
    for n in names:
        size = small_grads[n].size
        full_grad = summed[off:off + size].reshape(small_grads[n].shape)
        off += size
        if n in _SMALL_SHARDED:
            full_grad = lax.dynamic_slice_in_dim(full_grad, chip * LANES, LANES, axis=full_grad.ndim - 1)
        grads[n] = full_grad
    for n in _WEIGHTS:
        if n not in updates:
            behind = update(n, behind)
    return (loss, grad_x, *[updates[n][3] for n in _WEIGHTS], *[updates[n][0] for n in _WEIGHTS],
            *[updates[n][1] for n in _WEIGHTS], *[updates[n][2] for n in _WEIGHTS])
```

```python
import jax
import jax.numpy as jnp
from jax import lax
from jax.experimental import pallas as pl
from jax.experimental.pallas import tpu as pltpu

F32 = jnp.float32
_MXU = jnp.bfloat16
_ACT = jnp.bfloat16

D_MODEL = 1024
GRID_W = 64
HEAD_DIM = 64
N_Q_HEADS = 8
N_KV_HEADS = 2
Q_PER_KV = N_Q_HEADS // N_KV_HEADS
ATTN_WIDTH = N_Q_HEADS * HEAD_DIM
KV_WIDTH = N_KV_HEADS * HEAD_DIM
ROPE_THETA = 10000.0
CONV_WIDTH = D_MODEL // 2
POOL_RADII = (1, 2, 4, 8)
POOL_GROUP = 128
SGU_GROUP = 128
SGU_CHUNK = 128
N_GROUPS = 4
HALF = D_MODEL // 2
EPS = 1e-6
HALO = 8
LANES = 128
N_CHIPS = 4
N_DEV = 8

ADAM_LR = 0.001
ADAM_B1 = 0.9
ADAM_B2 = 0.999
ADAM_EPS = 1e-08
ADAM_WD = 0.01
ADAM_STEP = 10

_VMEM_LIMIT = 56 * 2 ** 20
_MESH = pl.DeviceIdType.MESH
_ANY = pl.BlockSpec(memory_space=pl.ANY)
_VMEM = pl.BlockSpec(memory_space=pltpu.VMEM)

_DN = {
    "nn": (((1,), (0,)), ((), ())),
    "nt": (((1,), (1,)), ((), ())),
    "tn": (((0,), (0,)), ((), ())),
}


def _params(*sem):
    return pltpu.CompilerParams(dimension_semantics=sem, vmem_limit_bytes=_VMEM_LIMIT)


def _tile(n, cap):
    best = None
    d = LANES
    while d <= min(n, cap):
        if n % d == 0:
            best = d
        d += LANES
    return best if best is not None else n


def _dot(a, b, mode="nn"):
    return lax.dot_general(a.astype(_MXU), b.astype(_MXU), _DN[mode], preferred_element_type=F32)


def _cat(*vals):
    vals = [v.astype(_MXU) for v in vals]
    return vals[0] if len(vals) == 1 else jnp.concatenate(vals, axis=1)


def _sigmoid(g):
    return 1.0 / (1.0 + jnp.exp(-g))


def _norm_rows(x, g):
    r = lax.rsqrt(jnp.mean(x * x, axis=-1, keepdims=True) + EPS)
    return (x * r) * g


def _swiglu(g, u):
    return (g * _sigmoid(g)) * u


_GELU_C = 0.7978845608028654


def _gelu(x):
    return 0.5 * x * (1.0 + jnp.tanh(_GELU_C * (x + 0.044715 * (x * x * x))))


def _gelu_grad(x):
    t = jnp.tanh(_GELU_C * (x + 0.044715 * (x * x * x)))
    return 0.5 * (1.0 + t) + 0.5 * x * (1.0 - t * t) * (_GELU_C * (1.0 + 3.0 * 0.044715 * (x * x)))


def _mm(name, grid, mode, a_ops, b_ops, e_ops, out_shape, out_specs, acc_shape, a_fn=_cat, b_fn=_cat, epi=None,
        n_outer=False, m_carried=False, b_pick=None):
    ni, nj, nk = grid
    na, nb, ne = len(a_ops), len(b_ops), len(e_ops)
    multi = isinstance(out_shape, (list, tuple))
    no = len(out_shape) if multi else 1

    def body(*refs):
        a_refs = refs[:na]
        b_refs = refs[na:na + nb]
        e_refs = refs[na + nb:na + nb + ne]
        o_refs = refs[na + nb + ne:na + nb + ne + no]
        a = a_fn(*[r[...] for r in a_refs])
        if b_pick is None:
            b = b_fn(*[r[...] for r in b_refs])
        else:
            b = b_pick(b_refs, pl.program_id(1), pl.program_id(2))
        p = _dot(a, b, mode)

        def finish(acc):
            if epi is None:
                o_refs[0][...] = acc.astype(o_refs[0].dtype)
            else:
                epi(acc, [r[...] for r in e_refs], o_refs)

        if nk == 1:
            finish(p)
        else:
            acc_ref = refs[-1]
            k = pl.program_id(2)

            @pl.when(k == 0)
            def _():
                acc_ref[...] = p

            @pl.when((k > 0) & (k < nk - 1))
            def _():
                acc_ref[...] += p

            @pl.when(k == nk - 1)
            def _():
                finish(acc_ref[...] + p)

    ops = list(a_ops) + list(b_ops) + list(e_ops)
    if n_outer:
        def flip(spec):
            return pl.BlockSpec(spec.block_shape, lambda j, i, k, f=spec.index_map: f(i, j, k))

        grid = (nj, ni, nk)
        ops = [(a, flip(s)) for a, s in ops]
        out_specs = [flip(s) for s in out_specs] if multi else flip(out_specs)
    return pl.pallas_call(
        body, name=name, grid=grid,
        in_specs=[s for _, s in ops],
        out_specs=out_specs, out_shape=out_shape,
        scratch_shapes=[pltpu.VMEM(acc_shape, F32)] if nk > 1 else [],
        compiler_params=_params(*(("arbitrary",) * 3 if m_carried else ("parallel", "parallel", "arbitrary"))),
    )(*[a for a, _ in ops])


def _whole(a):
    return pl.BlockSpec(a.shape, lambda i, j, k: (0,) * a.ndim, pipeline_mode=pl.Buffered(1))


def _norm_bwd_epi(acc, e, o):
    xf, dres, g = e
    r = lax.rsqrt(jnp.mean(xf * xf, axis=-1, keepdims=True) + EPS)
    xhat = xf * r
    dgx = acc * g
    m = jnp.mean(dgx * xhat, axis=-1, keepdims=True)
    o[0][...] = dres + r * (dgx - xhat * m)
    part = jnp.sum(acc * xhat, axis=0, keepdims=True)
    i = pl.program_id(0)

    @pl.when(i == 0)
    def _():
        o[1][...] = part

    @pl.when(i > 0)
    def _():
        o[1][...] += part


def _norm_bwd_ops(x, dres, gain, tm):
    t, d = x.shape
    row = pl.BlockSpec((tm, d), lambda i, j, k: (i, 0))
    vec = pl.BlockSpec((1, d), lambda i, j, k: (0, 0))
    return ([(x, row), (dres, row), (gain.reshape(1, d), vec)],
            [jax.ShapeDtypeStruct((t, d), F32), jax.ShapeDtypeStruct((1, d), F32)], [row, vec])


def _rows(t):
    return _tile(t, 512)


def _rmsnorm_fwd(name, x, gain):
    t, d = x.shape
    tr = _rows(t)

    def body(x_ref, g_ref, h_ref):
        xf = x_ref[...]
        r = lax.rsqrt(jnp.mean(xf * xf, axis=-1, keepdims=True) + EPS)
        h_ref[...] = ((xf * r) * g_ref[...]).astype(h_ref.dtype)

    return pl.pallas_call(
        body, name=name, grid=(t // tr,),
        in_specs=[pl.BlockSpec((tr, d), lambda i: (i, 0)), pl.BlockSpec((1, d), lambda i: (0, 0))],
        out_specs=pl.BlockSpec((tr, d), lambda i: (i, 0)),
        out_shape=jax.ShapeDtypeStruct((t, d), _ACT),
        compiler_params=_params("parallel"),
    )(x, gain.reshape(1, d))


def _rmsnorm_bwd(name, dh, x, gain, dres):
    t, d = x.shape
    tr = _rows(t)

    def body(dh_ref, x_ref, g_ref, dres_ref, dx_ref, dg_ref):
        i = pl.program_id(0)
        xf = x_ref[...]
        r = lax.rsqrt(jnp.mean(xf * xf, axis=-1, keepdims=True) + EPS)
        xhat = xf * r
        dy = dh_ref[...].astype(F32)
        dgx = dy * g_ref[...]
        m = jnp.mean(dgx * xhat, axis=-1, keepdims=True)
        dx_ref[...] = dres_ref[...] + r * (dgx - xhat * m)
        part = jnp.sum(dy * xhat, axis=0, keepdims=True)

        @pl.when(i == 0)
        def _():
            dg_ref[...] = part

        @pl.when(i > 0)
        def _():
            dg_ref[...] += part

    row = pl.BlockSpec((tr, d), lambda i: (i, 0))
    vec = pl.BlockSpec((1, d), lambda i: (0, 0))
    dx, dg = pl.pallas_call(
        body, name=name, grid=(t // tr,),
        in_specs=[row, row, vec, row],
        out_specs=(row, vec),
        out_shape=(jax.ShapeDtypeStruct((t, d), F32), jax.ShapeDtypeStruct((1, d), F32)),
        compiler_params=_params("arbitrary"),
    )(dh, x, gain.reshape(1, d), dres)
    return dx, dg.reshape(d)


def _final_loss(name, x, gain, target):
    t, d = x.shape
    tr = _rows(t)

    def body(x_ref, g_ref, t_ref, dx_ref, dg_ref, loss_ref):
        i = pl.program_id(0)
        xf = x_ref[...]
        r = lax.rsqrt(jnp.mean(xf * xf, axis=-1, keepdims=True) + EPS)
        xhat = xf * r
        g = g_ref[...]
        err = xhat * g - t_ref[...]
        lpart = 0.5 * jnp.sum(jnp.mean(err * err, axis=-1, keepdims=True), axis=0, keepdims=True)
        dy = err * (1.0 / d)
        dgx = dy * g
        m = jnp.mean(dgx * xhat, axis=-1, keepdims=True)
        dx_ref[...] = r * (dgx - xhat * m)
        part = jnp.sum(dy * xhat, axis=0, keepdims=True)
        lrow = jnp.broadcast_to(lpart, (1, LANES))

        @pl.when(i == 0)
        def _():
            dg_ref[...] = part
            loss_ref[...] = lrow

        @pl.when(i > 0)
        def _():
            dg_ref[...] += part
            loss_ref[...] += lrow

    row = pl.BlockSpec((tr, d), lambda i: (i, 0))
    vec = pl.BlockSpec((1, d), lambda i: (0, 0))
    dx, dg, loss = pl.pallas_call(
        body, name=name, grid=(t // tr,),
        in_specs=[row, vec, row],
        out_specs=(row, vec, pl.BlockSpec((1, LANES), lambda i: (0, 0))),
        out_shape=(jax.ShapeDtypeStruct((t, d), F32), jax.ShapeDtypeStruct((1, d), F32),
                   jax.ShapeDtypeStruct((1, LANES), F32)),
        compiler_params=_params("arbitrary"),
    )(x, gain.reshape(1, d), target)
    return loss[0, 0], dx, dg.reshape(d)


_FFN_TILES = dict(in_tm=1024, out_tm=512, dact_tm=512, dwout_tk=1024, dh_tm=512, dwin_tk=2048)


def _ffn_tiles(layer, which):
    return _FFN_TILES


def _ffn_fwd(tag, x, gain, w_in4, w_out, cfg):
    t, d = x.shape
    fs = w_in4.shape[2]
    f = 2 * fs
    tm = _tile(t, cfg["in_tm"])
    gain = gain.reshape(1, d)
    gu = _mm(
        tag + "_in", (t // tm, N_CHIPS, 1), "nn",
        [(x, pl.BlockSpec((tm, d), lambda i, j, k: (i, 0))), (gain, pl.BlockSpec((1, d), lambda i, j, k: (0, 0)))],
        [(w_in4, pl.BlockSpec((None, d, fs), lambda i, j, k: (j, 0, 0)))], [],
        jax.ShapeDtypeStruct((2, t, f), _ACT),
        pl.BlockSpec((None, tm, fs), lambda i, j, k: (j // 2, i, j % 2)), None, a_fn=_norm_rows)
    tm2 = _tile(t, cfg["out_tm"])

    def epi(acc, e, o):
        o[0][...] = e[0] + 0.5 * acc

    x_out = _mm(
        tag + "_out", (t // tm2, 1, 1), "nn",
        [(gu, pl.BlockSpec((None, tm2, f), lambda i, j, k: (0, i, 0))),
         (gu, pl.BlockSpec((None, tm2, f), lambda i, j, k: (1, i, 0)))],
        [(w_out, pl.BlockSpec((f, d), lambda i, j, k: (0, 0)))],
        [(x, pl.BlockSpec((tm2, d), lambda i, j, k: (i, 0)))],
        jax.ShapeDtypeStruct((t, d), F32),
        pl.BlockSpec((tm2, d), lambda i, j, k: (i, 0)), None,
        a_fn=_swiglu, epi=epi)
    return x_out, (x, gu)


def _ffn_bwd(tag, dxo, saved, gain, w_in4, w_out, cfg, mid, ready):
    x, gu = saved
    t, d = x.shape
    fs = w_in4.shape[2]
    f = 2 * fs
    tm = _tile(t, cfg["dact_tm"])
    tk = _tile(t, cfg["dwout_tk"])

    def epi_act(acc, e, o):
        g, u = e
        da = (0.5 * acc).astype(g.dtype)
        sig = _sigmoid(g)
        silu = g * sig
        o[0][0] = (da * u * (sig + silu * (1.0 - sig))).astype(o[0].dtype)
        o[0][1] = (da * silu).astype(o[0].dtype)

    dgu = _mm(
        tag + "_dact", (t // tm, 2, 1), "nt",
        [(dxo, pl.BlockSpec((tm, d), lambda i, j, k: (i, 0)))],
        [(w_out, _whole(w_out))],
        [(gu, pl.BlockSpec((None, tm, fs), lambda i, j, k: (0, i, j))),
         (gu, pl.BlockSpec((None, tm, fs), lambda i, j, k: (1, i, j)))],
        jax.ShapeDtypeStruct((2, t, f), _ACT),
        pl.BlockSpec((2, tm, fs), lambda i, j, k: (0, i, j)), None, epi=epi_act,
        b_pick=lambda b, j, k: b[0][pl.ds(pl.multiple_of(j * fs, LANES), fs), :])
    gain = gain + mid(dgu)

    def epi_half(acc, e, o):
        o[0][...] = (0.5 * acc).astype(o[0].dtype)

    dw_out = _mm(
        tag + "_dwout", (2, 1, t // tk), "tn",
        [(gu, pl.BlockSpec((None, tk, fs), lambda i, j, k: (0, k, i))),
         (gu, pl.BlockSpec((None, tk, fs), lambda i, j, k: (1, k, i)))],
        [(dxo, pl.BlockSpec((tk, d), lambda i, j, k: (k, 0)))], [],
        jax.ShapeDtypeStruct((f, d), _ACT),
        pl.BlockSpec((fs, d), lambda i, j, k: (i, 0)), (fs, d),
        a_fn=_swiglu, epi=epi_half)
    tk = _tile(t, cfg["dwin_tk"])
    dw_in4 = _mm(
        tag + "_dwin", (1, N_CHIPS, t // tk), "tn",
        [(x, pl.BlockSpec((tk, d), lambda i, j, k: (k, 0))),
         (gain.reshape(1, d), pl.BlockSpec((1, d), lambda i, j, k: (0, 0)))],
        [(dgu, pl.BlockSpec((None, tk, fs), lambda i, j, k: (j // 2, k, j % 2)))], [],
        jax.ShapeDtypeStruct((N_CHIPS, d, fs), _ACT),
        pl.BlockSpec((None, d, fs), lambda i, j, k: (j, 0, 0)), (d, fs), a_fn=_norm_rows)
    gain = gain + ready(dict(w_in4=dw_in4, w_out=dw_out), dw_in4)
    tm = _tile(t, cfg["dh_tm"])
    e_ops, shapes, specs = _norm_bwd_ops(x, dxo, gain, tm)
    dx, dgain = _mm(
        tag + "_dh", (t // tm, 1, 2), "nt",
        [(dgu, pl.BlockSpec((None, tm, f), lambda i, j, k: (k, i, 0)))],
        [(w_in4, _whole(w_in4))], e_ops, shapes, specs, (tm, d), epi=_norm_bwd_epi, m_carried=True,
        b_pick=lambda b, j, k: jnp.concatenate([b[0][2 * k], b[0][2 * k + 1]], axis=1))
    return dx, dgain.reshape(d)


_MIX_TILES = dict(tm=1024, dwout_tk=2048, dwin_tk=1024)


def _proj_in(tag, h, w_in):
    t, d = h.shape
    n = w_in.shape[1]
    tm = _tile(t, _MIX_TILES["tm"])
    return _mm(
        tag + "_in", (t // tm, 1, 1), "nn",
        [(h, pl.BlockSpec((tm, d), lambda i, j, k: (i, 0)))],
        [(w_in, pl.BlockSpec((d, n), lambda i, j, k: (0, 0)))], [],
        jax.ShapeDtypeStruct((t, n), _ACT),
        pl.BlockSpec((tm, n), lambda i, j, k: (i, 0)), None)


def _proj_out(tag, x, parts, w_out):
    t, d = x.shape
    tm = _tile(t, _MIX_TILES["tm"])

    def epi(acc, e, o):
        o[0][...] = e[0] + acc

    return _mm(
        tag + "_out", (t // tm, 1, 1), "nn",
        [(p, pl.BlockSpec((tm, p.shape[1]), lambda i, j, k: (i, 0))) for p in parts],
        [(w_out, pl.BlockSpec(w_out.shape, lambda i, j, k: (0, 0)))],
        [(x, pl.BlockSpec((tm, d), lambda i, j, k: (i, 0)))],
        jax.ShapeDtypeStruct((t, d), F32),
        pl.BlockSpec((tm, d), lambda i, j, k: (i, 0)), None, epi=epi)


def _proj_out_bwd(tag, dxo, parts, w_out):
    t, d = dxo.shape
    mix = w_out.shape[0]
    tm = _tile(t, _MIX_TILES["tm"])
    tk = _tile(t, _MIX_TILES["dwout_tk"])
    d_mix = _mm(
        tag + "_dmix", (t // tm, 1, 1), "nt",
        [(dxo, pl.BlockSpec((tm, d), lambda i, j, k: (i, 0)))],
        [(w_out, pl.BlockSpec((mix, d), lambda i, j, k: (0, 0)))], [],
        jax.ShapeDtypeStruct((t, mix), F32),
        pl.BlockSpec((tm, mix), lambda i, j, k: (i, 0)), None)
    dw_out = _mm(
        tag + "_dwout", (1, 1, t // tk), "tn",
        [(p, pl.BlockSpec((tk, p.shape[1]), lambda i, j, k: (k, 0))) for p in parts],
        [(dxo, pl.BlockSpec((tk, d), lambda i, j, k: (k, 0)))], [],
        jax.ShapeDtypeStruct((mix, d), _ACT),
        pl.BlockSpec((mix, d), lambda i, j, k: (0, 0)), (mix, d))
    return d_mix, dw_out


def _proj_in_bwd(tag, h, dparts, w_in, x, dres, gain, ready):
    t, d = h.shape
    n = w_in.shape[1]
    tm = _tile(t, _MIX_TILES["tm"])
    tk = _tile(t, _MIX_TILES["dwin_tk"])
    dw_in = _mm(
        tag + "_dwin", (1, 1, t // tk), "tn",
        [(h, pl.BlockSpec((tk, d), lambda i, j, k: (k, 0)))],
        [(p, pl.BlockSpec((tk, p.shape[1]), lambda i, j, k: (k, 0))) for p in dparts], [],
        jax.ShapeDtypeStruct((d, n), _ACT),
        pl.BlockSpec((d, n), lambda i, j, k: (0, 0)), (d, n))
    e_ops, shapes, specs = _norm_bwd_ops(x, dres, gain + ready(dw_in), tm)
    dx, dgain = _mm(
        tag + "_dh", (t // tm, 1, 1), "nt",
        [(p, pl.BlockSpec((tm, p.shape[1]), lambda i, j, k: (i, 0))) for p in dparts],
        [(w_in, pl.BlockSpec((d, n), lambda i, j, k: (0, 0)))], e_ops, shapes, specs, None,
        epi=_norm_bwd_epi, m_carried=True)
    return dx, dgain.reshape(d)


def _shifted(pad_ref, val, s):
    pad_ref[pl.ds(HALO, s), :] = val
    return pad_ref[pl.ds(HALO - 1, s), :], pad_ref[pl.ds(HALO + 1, s), :]


def _zero_halo(pad_ref, s):
    z = jnp.zeros((HALO, pad_ref.shape[1]), F32)
    pad_ref[pl.ds(0, HALO), :] = z
    pad_ref[pl.ds(HALO + s, HALO), :] = z


def _conv_fwd(tag, proj, conv_w, nb, s):
    t = proj.shape[0]
    ncb = CONV_WIDTH // LANES

    def body(gb_ref, gc_ref, hc_ref, w_ref, a_ref, pad_ref):
        _zero_halo(pad_ref, s)
        cg = gc_ref[...].astype(F32) * hc_ref[...].astype(F32)
        prev, nxt = _shifted(pad_ref, cg, s)
        w = w_ref[...]
        conv = prev * w[0:1, :] + cg * w[1:2, :] + nxt * w[2:3, :]
        a_ref[...] = (gb_ref[...].astype(F32) * conv).astype(a_ref.dtype)

    def col(off):
        return pl.BlockSpec((s, LANES), lambda b, c: (b, off + c))

    return pl.pallas_call(
        body, name=tag + "_conv", grid=(nb, ncb),
        in_specs=[col(0), col(ncb), col(2 * ncb), pl.BlockSpec((3, LANES), lambda b, c: (0, c))],
        out_specs=col(0),
        out_shape=jax.ShapeDtypeStruct((t, CONV_WIDTH), _ACT),
        scratch_shapes=[pltpu.VMEM((s + 2 * HALO, LANES), F32)],
        compiler_params=_params("parallel", "parallel"),
    )(proj, proj, proj, conv_w)


def _conv_bwd(tag, proj, conv_w, d_mix, nb, s):
    t = proj.shape[0]
    ncb = CONV_WIDTH // LANES

    def body(gb_ref, gc_ref, hc_ref, w_ref, da_ref, dgb_ref, dgc_ref, dhc_ref, dw_ref, pad_ref):
        b = pl.program_id(1)
        _zero_halo(pad_ref, s)
        gb = gb_ref[...].astype(F32)
        gc = gc_ref[...].astype(F32)
        hc = hc_ref[...].astype(F32)
        w = w_ref[...]
        da = da_ref[...]
        cg = gc * hc
        prev, nxt = _shifted(pad_ref, cg, s)
        conv = prev * w[0:1, :] + cg * w[1:2, :] + nxt * w[2:3, :]
        dgb_ref[...] = (da * conv).astype(dgb_ref.dtype)
        dconv = da * gb
        dw = jnp.concatenate([
            jnp.sum(dconv * prev, axis=0, keepdims=True),
            jnp.sum(dconv * cg, axis=0, keepdims=True),
            jnp.sum(dconv * nxt, axis=0, keepdims=True)], axis=0)
        dprev, dnxt = _shifted(pad_ref, dconv, s)
        dcg = dnxt * w[0:1, :] + dconv * w[1:2, :] + dprev * w[2:3, :]
        dgc_ref[...] = (dcg * hc).astype(dgc_ref.dtype)
        dhc_ref[...] = (dcg * gc).astype(dhc_ref.dtype)

        @pl.when(b == 0)
        def _():
            dw_ref[...] = dw

        @pl.when(b > 0)
        def _():
            dw_ref[...] += dw

    def col(off):
        return pl.BlockSpec((s, LANES), lambda c, b: (b, off + c))

    wspec = pl.BlockSpec((3, LANES), lambda c, b: (0, c))
    act = jax.ShapeDtypeStruct((t, CONV_WIDTH), _ACT)
    return pl.pallas_call(
        body, name=tag + "_dconv", grid=(ncb, nb),
        in_specs=[col(0), col(ncb), col(2 * ncb), wspec, col(0)],
        out_specs=(col(0), col(0), col(0), wspec),
        out_shape=(act, act, act, jax.ShapeDtypeStruct((3, CONV_WIDTH), F32)),
        scratch_shapes=[pltpu.VMEM((s + 2 * HALO, LANES), F32)],
        compiler_params=_params("parallel", "arbitrary"),
    )(proj, proj, proj, conv_w, d_mix)


def _rope_tables(s):
    rows = s // GRID_W
    r_idx, c_idx = jnp.meshgrid(jnp.arange(rows), jnp.arange(GRID_W), indexing="ij")
    r_idx = r_idx.reshape(-1).astype(F32)
    c_idx = c_idx.reshape(-1).astype(F32)
    n_freq = HEAD_DIM // 4
    inv = ROPE_THETA ** (-jnp.arange(n_freq, dtype=F32) / n_freq)
    ang = jnp.concatenate([r_idx[:, None] * inv, c_idx[:, None] * inv], axis=-1)
    cos = jnp.repeat(jnp.cos(ang), 2, axis=1)
    sin = jnp.repeat(jnp.sin(ang), 2, axis=1)
    sign = jnp.where(jnp.arange(HEAD_DIM) % 2 == 0, -1.0, 1.0).astype(F32)
    return jnp.tile(cos, (1, LANES // HEAD_DIM)), jnp.tile(sin * sign, (1, LANES // HEAD_DIM))


def _head_ones():
    i = jnp.arange(LANES) // HEAD_DIM
    return (i[:, None] == i[None, :]).astype(jnp.bfloat16)


def _head_sum(v, ones):
    outs = []
    for j in range(v.shape[1] // LANES):
        c = v[:, j * LANES:(j + 1) * LANES]
        hi = c.astype(jnp.bfloat16)
        lo = (c - hi.astype(F32)).astype(jnp.bfloat16)
        outs.append(jnp.dot(hi, ones, preferred_element_type=F32) + jnp.dot(lo, ones, preferred_element_type=F32))
    return outs[0] if len(outs) == 1 else jnp.concatenate(outs, axis=1)


def _pair_swap(v):
    outs = []
    for j in range(v.shape[1] // LANES):
        c = v[:, j * LANES:(j + 1) * LANES]
        lane = lax.broadcasted_iota(jnp.int32, c.shape, 1)
        outs.append(jnp.where(lane % 2 == 0, pltpu.roll(c, LANES - 1, 1), pltpu.roll(c, 1, 1)))
    return outs[0] if len(outs) == 1 else jnp.concatenate(outs, axis=1)


def _wide(tab, width):
    return tab if width == LANES else jnp.concatenate([tab] * (width // LANES), axis=1)


_QK_SCALE = HEAD_DIM ** -0.5


def _qk_fwd(tag, proj, q_gain, k_gain, cos, sin, nb, s):
    t = proj.shape[0]
    tr = _tile(s, 512)
    ns = s // tr
    q_off = 3 * CONV_WIDTH // ATTN_WIDTH
    k_off = (3 * CONV_WIDTH + ATTN_WIDTH) // KV_WIDTH

    def body(q_ref, k_ref, qg_ref, kg_ref, cos_ref, sin_ref, ones_ref, qo_ref, ko_ref):
        ones = ones_ref[...]
        for src, g_ref, dst, mult in ((q_ref, qg_ref, qo_ref, _QK_SCALE), (k_ref, kg_ref, ko_ref, 1.0)):
            v = src[...].astype(F32)
            w = v.shape[1]
            r = lax.rsqrt(_head_sum(v * v, ones) * (1.0 / HEAD_DIM) + EPS)
            vn = (v * r) * g_ref[...]
            rot = vn * _wide(cos_ref[...], w) + _pair_swap(vn) * _wide(sin_ref[...], w)
            dst[...] = (rot * mult).astype(dst.dtype)

    tab = pl.BlockSpec((tr, LANES), lambda i: (i % ns, 0))
    return pl.pallas_call(
        body, name=tag + "_qk", grid=(t // tr,),
        in_specs=[pl.BlockSpec((tr, ATTN_WIDTH), lambda i: (i, q_off)),
                  pl.BlockSpec((tr, KV_WIDTH), lambda i: (i, k_off)),
                  pl.BlockSpec((1, ATTN_WIDTH), lambda i: (0, 0)),
                  pl.BlockSpec((1, KV_WIDTH), lambda i: (0, 0)),
                  tab, tab, pl.BlockSpec((LANES, LANES), lambda i: (0, 0))],
        out_specs=(pl.BlockSpec((tr, ATTN_WIDTH), lambda i: (i, 0)),
                   pl.BlockSpec((tr, KV_WIDTH), lambda i: (i, 0))),
        out_shape=(jax.ShapeDtypeStruct((t, ATTN_WIDTH), _ACT), jax.ShapeDtypeStruct((t, KV_WIDTH), _ACT)),
        compiler_params=_params("parallel"),
    )(proj, proj, jnp.tile(q_gain, N_Q_HEADS).reshape(1, ATTN_WIDTH),
      jnp.tile(k_gain, N_KV_HEADS).reshape(1, KV_WIDTH), cos, sin, _head_ones())


def _qk_bwd(tag, proj, q_gain, k_gain, cos, sin, dq_rot, dk_rot, nb, s):
    t = proj.shape[0]
    tr = _tile(s, 512)
    ns = s // tr
    q_off = 3 * CONV_WIDTH // ATTN_WIDTH
    k_off = (3 * CONV_WIDTH + ATTN_WIDTH) // KV_WIDTH

    def body(q_ref, k_ref, qg_ref, kg_ref, cos_ref, sin_ref, ones_ref, dqr_ref, dkr_ref,
             dq_ref, dk_ref, dqg_ref, dkg_ref):
        i = pl.program_id(0)
        ones = ones_ref[...]
        for src, g_ref, dr_ref, dst, dg_ref, mult in ((q_ref, qg_ref, dqr_ref, dq_ref, dqg_ref, _QK_SCALE),
                                                      (k_ref, kg_ref, dkr_ref, dk_ref, dkg_ref, 1.0)):
            v = src[...].astype(F32)
            w = v.shape[1]
            r = lax.rsqrt(_head_sum(v * v, ones) * (1.0 / HEAD_DIM) + EPS)
            xhat = v * r
            dr = dr_ref[...] * mult
            dvn = dr * _wide(cos_ref[...], w) + _pair_swap(dr * _wide(sin_ref[...], w))
            dgx = dvn * g_ref[...]
            m = _head_sum(dgx * xhat, ones) * (1.0 / HEAD_DIM)
            dst[...] = (r * (dgx - xhat * m)).astype(dst.dtype)
            part = jnp.sum(dvn * xhat, axis=0, keepdims=True)
            fold = part[:, 0:HEAD_DIM]
            for hh in range(1, w // HEAD_DIM):
                fold = fold + part[:, hh * HEAD_DIM:(hh + 1) * HEAD_DIM]

            @pl.when(i == 0)
            def _():
                dg_ref[...] = fold

            @pl.when(i > 0)
            def _():
                dg_ref[...] += fold

    tab = pl.BlockSpec((tr, LANES), lambda i: (i % ns, 0))
    qrow = pl.BlockSpec((tr, ATTN_WIDTH), lambda i: (i, 0))
    krow = pl.BlockSpec((tr, KV_WIDTH), lambda i: (i, 0))
    gvec = pl.BlockSpec((1, HEAD_DIM), lambda i: (0, 0))
    dq, dk, dqg, dkg = pl.pallas_call(
        body, name=tag + "_dqk", grid=(t // tr,),
        in_specs=[pl.BlockSpec((tr, ATTN_WIDTH), lambda i: (i, q_off)),
                  pl.BlockSpec((tr, KV_WIDTH), lambda i: (i, k_off)),
                  pl.BlockSpec((1, ATTN_WIDTH), lambda i: (0, 0)),
                  pl.BlockSpec((1, KV_WIDTH), lambda i: (0, 0)),
                  tab, tab, pl.BlockSpec((LANES, LANES), lambda i: (0, 0)), qrow, krow],
        out_specs=(qrow, krow, gvec, gvec),
        out_shape=(jax.ShapeDtypeStruct((t, ATTN_WIDTH), _ACT), jax.ShapeDtypeStruct((t, KV_WIDTH), _ACT),
                   jax.ShapeDtypeStruct((1, HEAD_DIM), F32), jax.ShapeDtypeStruct((1, HEAD_DIM), F32)),
        compiler_params=_params("arbitrary"),
    )(proj, proj, jnp.tile(q_gain, N_Q_HEADS).reshape(1, ATTN_WIDTH),
      jnp.tile(k_gain, N_KV_HEADS).reshape(1, KV_WIDTH), cos, sin, _head_ones(), dq_rot, dk_rot)
    return dq, dk, dqg.reshape(HEAD_DIM), dkg.reshape(HEAD_DIM)


def _head(v, h):
    return v[:, h * HEAD_DIM:(h + 1) * HEAD_DIM]


def _attn_fwd(tag, q, k, proj, nb, s):
    t = q.shape[0]
    tq = _tile(s, 256)
    nq = s // tq
    v_off = (3 * CONV_WIDTH + ATTN_WIDTH + KV_WIDTH) // KV_WIDTH

    def body(q_ref, k_ref, v_ref, o_ref, lse_ref):
        qv = q_ref[...]
        kv = k_ref[...]
        vv = v_ref[...]
        for h in range(N_Q_HEADS):
            j = h // Q_PER_KV
            sc = _dot(_head(qv, h), _head(kv, j), "nt")
            m = jnp.max(sc, axis=-1, keepdims=True)
            e = jnp.exp(sc - m)
            l = jnp.sum(e, axis=-1, keepdims=True)
            o = _dot(e, _head(vv, j)) * (1.0 / l)
            o_ref[:, h * HEAD_DIM:(h + 1) * HEAD_DIM] = o.astype(o_ref.dtype)
            lse_ref[:, h:h + 1] = m + jnp.log(l)

    return pl.pallas_call(
        body, name=tag + "_attn", grid=(nb, nq),
        in_specs=[pl.BlockSpec((tq, ATTN_WIDTH), lambda b, i: (b * nq + i, 0)),
                  pl.BlockSpec((s, KV_WIDTH), lambda b, i: (b, 0)),
                  pl.BlockSpec((s, KV_WIDTH), lambda b, i: (b, v_off))],
        out_specs=(pl.BlockSpec((tq, ATTN_WIDTH), lambda b, i: (b * nq + i, 0)),
                   pl.BlockSpec((tq, N_Q_HEADS), lambda b, i: (b * nq + i, 0))),
        out_shape=(jax.ShapeDtypeStruct((t, ATTN_WIDTH), _ACT), jax.ShapeDtypeStruct((t, N_Q_HEADS), F32)),
        compiler_params=_params("parallel", "parallel"),
    )(q, k, proj)


def _attn_bwd(tag, q, k, proj, o, lse, d_mix, nb, s):
    t = q.shape[0]
    tq = _tile(s, 256)
    nq = s // tq
    v_off = (3 * CONV_WIDTH + ATTN_WIDTH + KV_WIDTH) // KV_WIDTH

    def body(q_ref, k_ref, v_ref, o_ref, lse_ref, do_ref, dq_ref, dk_ref, dv_ref):
        i = pl.program_id(1)

        @pl.when(i == 0)
        def _():
            dk_ref[...] = jnp.zeros_like(dk_ref)
            dv_ref[...] = jnp.zeros_like(dv_ref)

        qv = q_ref[...]
        kv = k_ref[...]
        vv = v_ref[...]
        ov = o_ref[...].astype(F32)
        dov = do_ref[...]
        lse = lse_ref[...]
        for h in range(N_Q_HEADS):
            j = h // Q_PER_KV
            cols = slice(j * HEAD_DIM, (j + 1) * HEAD_DIM)
            qh = _head(qv, h)
            kj = _head(kv, j)
            doh = _head(dov, h)
            sc = _dot(qh, kj, "nt")
            p = jnp.exp(sc - lse[:, h:h + 1])
            dp = _dot(doh, _head(vv, j), "nt")
            delta = jnp.sum(doh * _head(ov, h), axis=-1, keepdims=True)
            ds = p * (dp - delta)
            dv_ref[:, cols] += _dot(p, doh, "tn")
            dk_ref[:, cols] += _dot(ds, qh, "tn")
            dq_ref[:, h * HEAD_DIM:(h + 1) * HEAD_DIM] = _dot(ds, kj)

    qrow = pl.BlockSpec((tq, ATTN_WIDTH), lambda b, i: (b * nq + i, 0))
    kvrow = pl.BlockSpec((s, KV_WIDTH), lambda b, i: (b, 0))
    return pl.pallas_call(
        body, name=tag + "_dattn", grid=(nb, nq),
        in_specs=[qrow, kvrow, pl.BlockSpec((s, KV_WIDTH), lambda b, i: (b, v_off)), qrow,
                  pl.BlockSpec((tq, N_Q_HEADS), lambda b, i: (b * nq + i, 0)),
                  pl.BlockSpec((tq, ATTN_WIDTH), lambda b, i: (b * nq + i, 1))],
        out_specs=(qrow, kvrow, kvrow),
        out_shape=(jax.ShapeDtypeStruct((t, ATTN_WIDTH), F32), jax.ShapeDtypeStruct((t, KV_WIDTH), F32),
                   jax.ShapeDtypeStruct((t, KV_WIDTH), F32)),
        compiler_params=_params("parallel", "arbitrary"),
    )(q, k, proj, o, lse, d_mix)


def _even_fwd(tag, x, p, cos, sin, nb, s):
    h = _rmsnorm_fwd(tag + "_norm", x, p["norm"])
    proj = _proj_in(tag, h, p["w_in"])
    a = _conv_fwd(tag, proj, p["conv_w"], nb, s)
    q, k = _qk_fwd(tag, proj, p["q_gain"], p["k_gain"], cos, sin, nb, s)
    o, lse = _attn_fwd(tag, q, k, proj, nb, s)
    x_out = _proj_out(tag, x, [a, o], p["w_out"])
    return x_out, (x, h, proj, a, q, k, o, lse)


def _even_bwd(tag, dxo, saved, p, cos, sin, nb, s, mid, ready):
    x, h, proj, a, q, k, o, lse = saved
    d_mix, dw_out = _proj_out_bwd(tag, dxo, [a, o], p["w_out"])
    dgb, dgc, dhc, dconv_w = _conv_bwd(tag, proj, p["conv_w"] + mid(d_mix), d_mix, nb, s)
    dq_rot, dk_rot, dv = _attn_bwd(tag, q, k, proj, o, lse, d_mix, nb, s)
    dq, dk, dq_gain, dk_gain = _qk_bwd(tag, proj, p["q_gain"], p["k_gain"], cos, sin, dq_rot, dk_rot, nb, s)
    dx, dnorm = _proj_in_bwd(tag, h, [dgb, dgc, dhc, dq, dk, dv], p["w_in"], x, dxo, p["norm"],
                             lambda dw_in: ready(dict(w_in=dw_in, w_out=dw_out), dw_in))
    return dx, dict(norm=dnorm, conv_w=dconv_w, q_gain=dq_gain, k_gain=dk_gain)


def _window(pad_ref, val, r, s):
    pad_ref[pl.ds(HALO, s), :] = val
    acc = val
    for d in range(1, r + 1):
        acc = acc + pad_ref[pl.ds(HALO - d, s), :] + pad_ref[pl.ds(HALO + d, s), :]
    return acc


def _count(r, s):
    t = lax.broadcasted_iota(jnp.int32, (s, 1), 0)
    return (jnp.minimum(t + r, s - 1) - jnp.maximum(t - r, 0) + 1).astype(F32)


def _sgu_chunk(u_ref, v_ref, norm, ws_ref, bt, rows):
    uu = u_ref[rows, :].astype(F32)
    vv = v_ref[rows, :].astype(F32)
    gu = _gelu(uu)
    gv = _gelu(vv)
    r = lax.rsqrt(jnp.mean(gv * gv, axis=-1, keepdims=True) + EPS)
    xhat = gv * r
    vn = xhat * norm
    mixed = []
    for g in range(N_GROUPS):
        cols = slice(g * SGU_GROUP, (g + 1) * SGU_GROUP)
        mixed.append(_dot(ws_ref[g], vn[:, cols]) + bt[:, g:g + 1])
    return uu, vv, gu, r, xhat, vn, mixed


def _odd_core_fwd(tag, proj, p, nb, s):
    t = proj.shape[0]
    nchunk = s // SGU_CHUNK

    def body(p_ref, u_ref, v_ref, pw_ref, ps_ref, sn_ref, ws_ref, bt_ref, mix_ref, pad_ref):
        _zero_halo(pad_ref, s)
        for g, r in enumerate(POOL_RADII):
            cols = slice(g * POOL_GROUP, (g + 1) * POOL_GROUP)
            pg = p_ref[:, cols].astype(F32)
            pooled = _window(pad_ref, pg, r, s) / _count(r, s) - pg
            mix_ref[:, cols] = (_dot(pooled, pw_ref[g]) * ps_ref[:, cols]).astype(mix_ref.dtype)
        norm = sn_ref[...]
        bt = bt_ref[...]

        def chunk(n, carry):
            rows = pl.ds(pl.multiple_of(n * SGU_CHUNK, SGU_CHUNK), SGU_CHUNK)
            _, _, gu, _, _, _, mixed = _sgu_chunk(u_ref, v_ref, norm, ws_ref, bt, rows)
            for g in range(N_GROUPS):
                cols = slice(g * SGU_GROUP, (g + 1) * SGU_GROUP)
                mix_ref[rows, HALF + g * SGU_GROUP:HALF + (g + 1) * SGU_GROUP] = (
                    gu[:, cols] * mixed[g]).astype(mix_ref.dtype)
            return carry

        lax.fori_loop(0, nchunk, chunk, 0)

    def col(j):
        return pl.BlockSpec((s, HALF), lambda b: (b, j))

    def whole(a):
        return pl.BlockSpec(a.shape, lambda b: (0,) * a.ndim)

    consts = [p["pool_w"], p["pool_scale"].reshape(1, HALF), p["sgu_norm"].reshape(1, HALF),
              p["sgu_w"], p["sgu_b"].T]
    return pl.pallas_call(
        body, name=tag + "_core", grid=(nb,),
        in_specs=[col(0), col(1), col(2)] + [whole(a) for a in consts],
        out_specs=pl.BlockSpec((s, D_MODEL), lambda b: (b, 0)),
        out_shape=jax.ShapeDtypeStruct((t, D_MODEL), _ACT),
        scratch_shapes=[pltpu.VMEM((s + 2 * HALO, POOL_GROUP), F32)],
        compiler_params=_params("parallel"),
    )(proj, proj, proj, *consts)


def _odd_core_bwd(tag, proj, p, d_mix, nb, s):
    t = proj.shape[0]
    nchunk = s // SGU_CHUNK

    def body(p_ref, u_ref, v_ref, pw_ref, ps_ref, sn_ref, ws_ref, bt_ref, dm_ref,
             dproj_ref, dpw_ref, dps_ref, dsn_ref, dws_ref, dbt_ref, pad_ref):
        b = pl.program_id(0)

        @pl.when(b == 0)
        def _():
            dpw_ref[...] = jnp.zeros_like(dpw_ref)
            dps_ref[...] = jnp.zeros_like(dps_ref)
            dsn_ref[...] = jnp.zeros_like(dsn_ref)
            dws_ref[...] = jnp.zeros_like(dws_ref)
            dbt_ref[...] = jnp.zeros_like(dbt_ref)

        _zero_halo(pad_ref, s)
        for g, r in enumerate(POOL_RADII):
            cols = slice(g * POOL_GROUP, (g + 1) * POOL_GROUP)
            pg = p_ref[:, cols].astype(F32)
            cnt = _count(r, s)
            pooled = _window(pad_ref, pg, r, s) / cnt - pg
            c_pre = _dot(pooled, pw_ref[g])
            dc = dm_ref[:, cols]
            dps_ref[:, cols] += jnp.sum(dc * c_pre, axis=0, keepdims=True)
            dcp = dc * ps_ref[:, cols]
            dpw_ref[g] += _dot(pooled, dcp, "tn")
            dpooled = _dot(dcp, pw_ref[g], "nt")
            dproj_ref[:, cols] = (_window(pad_ref, dpooled / cnt, r, s) - dpooled).astype(dproj_ref.dtype)
        norm = sn_ref[...]
        bt = bt_ref[...]

        def chunk(n, carry):
            rows = pl.ds(pl.multiple_of(n * SGU_CHUNK, SGU_CHUNK), SGU_CHUNK)
            uu, vv, gu, r, xhat, vn, mixed = _sgu_chunk(u_ref, v_ref, norm, ws_ref, bt, rows)
            dd = dm_ref[rows, HALF:D_MODEL]
            dgu, dvn = [], []
            for g in range(N_GROUPS):
                cols = slice(g * SGU_GROUP, (g + 1) * SGU_GROUP)
                dgu.append(dd[:, cols] * mixed[g])
                dmx = dd[:, cols] * gu[:, cols]
                dbt_ref[:, g:g + 1] += jnp.sum(dmx, axis=-1, keepdims=True)
                dws_ref[g] += _dot(dmx, vn[:, cols], "nt")
                dvn.append(_dot(ws_ref[g], dmx, "tn"))
            dgu = jnp.concatenate(dgu, axis=1)
            dvn = jnp.concatenate(dvn, axis=1)
            dsn_ref[...] += jnp.sum(dvn * xhat, axis=0, keepdims=True)
            dgx = dvn * norm
            m = jnp.mean(dgx * xhat, axis=-1, keepdims=True)
            dgv = r * (dgx - xhat * m)
            dproj_ref[rows, HALF:2 * HALF] = (dgu * _gelu_grad(uu)).astype(dproj_ref.dtype)
            dproj_ref[rows, 2 * HALF:3 * HALF] = (dgv * _gelu_grad(vv)).astype(dproj_ref.dtype)
            return carry

        lax.fori_loop(0, nchunk, chunk, 0)

    def col(j):
        return pl.BlockSpec((s, HALF), lambda b: (b, j))

    def whole(a):
        return pl.BlockSpec(a.shape, lambda b: (0,) * a.ndim)

    consts = [p["pool_w"], p["pool_scale"].reshape(1, HALF), p["sgu_norm"].reshape(1, HALF),
              p["sgu_w"], p["sgu_b"].T]
    gshapes = [jax.ShapeDtypeStruct(a.shape, F32) for a in consts]
    dproj, dpw, dps, dsn, dws, dbt = pl.pallas_call(
        body, name=tag + "_dcore", grid=(nb,),
        in_specs=[col(0), col(1), col(2)] + [whole(a) for a in consts]
        + [pl.BlockSpec((s, D_MODEL), lambda b: (b, 0))],
        out_specs=[pl.BlockSpec((s, 3 * HALF), lambda b: (b, 0))] + [whole(a) for a in consts],
        out_shape=[jax.ShapeDtypeStruct((t, 3 * HALF), _ACT)] + gshapes,
        scratch_shapes=[pltpu.VMEM((s + 2 * HALO, POOL_GROUP), F32)],
        compiler_params=_params("arbitrary"),
    )(proj, proj, proj, *consts, d_mix)
    return dproj, dict(pool_w=dpw, pool_scale=dps.reshape(HALF), sgu_norm=dsn.reshape(HALF), sgu_w=dws, sgu_b=dbt.T)


def _odd_fwd(tag, x, p, nb, s):
    h = _rmsnorm_fwd(tag + "_norm", x, p["norm"])
    proj = _proj_in(tag, h, p["w_in"])
    mix = _odd_core_fwd(tag, proj, p, nb, s)
    x_out = _proj_out(tag, x, [mix], p["w_out"])
    return x_out, (x, h, proj, mix)


def _odd_bwd(tag, dxo, saved, p, nb, s, mid, ready):
    x, h, proj, mix = saved
    d_mix, dw_out = _proj_out_bwd(tag, dxo, [mix], p["w_out"])
    p = dict(p, pool_scale=p["pool_scale"] + mid(d_mix))
    dproj, grads = _odd_core_bwd(tag, proj, p, d_mix, nb, s)
    dx, dnorm = _proj_in_bwd(tag, h, [dproj], p["w_in"], x, dxo, p["norm"],
                             lambda dw_in: ready(dict(w_in=dw_in, w_out=dw_out), dw_in))
    grads.update(norm=dnorm)
    return dx, grads


def _local_step(x3, target3, depth, weights_of, final_norm, mid, grads_ready, small_done):
    nb, s, d = x3.shape
    t = nb * s
    x = x3.reshape(t, d)
    target = target3.reshape(t, d)
    cos, sin = _rope_tables(s)
    saved, ws = [], []
    for l in range(depth):
        w1 = weights_of(l, "ffn1", x)
        x, s1 = _ffn_fwd(f"l{l}_ffn1", x, w1["norm"], w1["w_in4"], w1["w_out"], _ffn_tiles(l, 1))
        wm = weights_of(l, "mix", x)
        if l % 2 == 0:
            x, s2 = _even_fwd(f"l{l}_ev", x, wm, cos, sin, nb, s)
        else:
            x, s2 = _odd_fwd(f"l{l}_od", x, wm, nb, s)
        w2 = weights_of(l, "ffn2", x)
        x, s3 = _ffn_fwd(f"l{l}_ffn2", x, w2["norm"], w2["w_in4"], w2["w_out"], _ffn_tiles(l, 2))
        saved.append((s1, s2, s3))
        ws.append((w1, wm, w2))
    loss, dx, dfinal = _final_loss("final_loss", x, final_norm, target)
    for l in reversed(range(depth)):
        s1, s2, s3 = saved[l]
        w1, wm, w2 = ws[l]

        def ready(block):
            return lambda grads, a: grads_ready(l, block, grads, a)

        dx, dn = _ffn_bwd(f"l{l}_ffn2", dx, s3, w2["norm"], w2["w_in4"], w2["w_out"], _ffn_tiles(l, 2), mid, ready("ffn2"))
        small_done(l, "ffn2", dict(norm=dn))
        if l % 2 == 0:
            dx, gm = _even_bwd(f"l{l}_ev", dx, s2, wm, cos, sin, nb, s, mid, ready("mix"))
        else:
            dx, gm = _odd_bwd(f"l{l}_od", dx, s2, wm, nb, s, mid, ready("mix"))
        small_done(l, "mix", gm)
        dx, dn = _ffn_bwd(f"l{l}_ffn1", dx, s1, w1["norm"], w1["w_in4"], w1["w_out"], _ffn_tiles(l, 1), mid, ready("ffn1"))
        small_done(l, "ffn1", dict(norm=dn))
    return loss, dx.reshape(nb, s, d), dfinal


_HBM = pl.BlockSpec(memory_space=pltpu.HBM)


def _place():
    x, y, c = lax.axis_index("x"), lax.axis_index("y"), lax.axis_index("c")
    chips = [(1 - x, y), (x, 1 - y), (1 - x, 1 - y)]
    return x, y, c, chips


def _remote(src, dst, send_sem, recv_sem, to):
    return pltpu.make_async_remote_copy(src_ref=src, dst_ref=dst, send_sem=send_sem, recv_sem=recv_sem,
                                        device_id=to, device_id_type=_MESH)


def _gather_shards(arrs, small):
    n = len(arrs)
    own = 6

    def body(*refs):
        ins, sm_in = refs[:n], refs[n]
        outs, sm_out = refs[n + 1:2 * n + 1], refs[2 * n + 1]
        send, recv = refs[2 * n + 2:]
        x, y, c, chips = _place()
        k = 2 * x + y
        sib = (x, y, 1 - c)
        started = []
        for a in range(n + 1):
            src, dst = (ins[a], outs[a]) if a < n else (sm_in, sm_out)
            cp = _remote(src, dst.at[k], send.at[a, own], recv.at[a, own], sib)
            cp.start()
            started.append(cp)
            if a < n:
                h = src.shape[0] // 2
                mine = pl.ds(c * h, h)
                src_part, dst_part = src.at[mine], dst.at[k, mine]
            else:
                src_part, dst_part = src, dst.at[k]
            for j, chip in enumerate(chips):
                cp = _remote(src_part, dst_part, send.at[a, j], recv.at[a, j], (*chip, c))
                cp.start()
                started.append(cp)
        for a in range(n):
            h = ins[a].shape[0] // 2
            mine = pl.ds(c * h, h)
            for j, (px, py) in enumerate(chips):
                landed = outs[a].at[2 * px + py, mine]
                _remote(landed, landed, send.at[a, j], recv.at[a, j], (px, py, c)).wait_recv()
                cp = _remote(landed, landed, send.at[a, 3 + j], recv.at[a, 3 + j], sib)
                cp.start()
                started.append(cp)
        for a in range(n):
            h = ins[a].shape[0] // 2
            other = pl.ds((1 - c) * h, h)
            for j, (px, py) in enumerate(chips):
                passed = outs[a].at[2 * px + py, other]
                _remote(passed, passed, send.at[a, 3 + j], recv.at[a, 3 + j], sib).wait_recv()
        for j, (px, py) in enumerate(chips):
            landed = sm_out.at[2 * px + py]
            _remote(landed, landed, send.at[n, j], recv.at[n, j], (px, py, c)).wait_recv()
        for a in range(n + 1):
            filled = (outs[a] if a < n else sm_out).at[k]
            _remote(filled, filled, send.at[a, own], recv.at[a, own], sib).wait_recv()
        for cp in started:
            cp.wait_send()

    outs = pl.pallas_call(
        body, name="gather_shards",
        in_specs=[_HBM] * (n + 1), out_specs=[_HBM] * (n + 1),
        out_shape=[jax.ShapeDtypeStruct((N_CHIPS,) + a.shape, a.dtype) for a in list(arrs) + [small]],
        scratch_shapes=[pltpu.SemaphoreType.DMA((n + 1, 7)), pltpu.SemaphoreType.DMA((n + 1, 7))],
    )(*arrs, small)
    return outs[:n], outs[n]


_SEM = pl.BlockSpec(memory_space=pltpu.SEMAPHORE)
_EFFECT = pltpu.SideEffectType.DATAFLOW_SIDE_EFFECTING


def _gather_plan(i, src, land, k, c, chips, sib):
    mine = pl.ds(c * (src.shape[0] // 2), src.shape[0] // 2)
    plan = [(src.at[mine], land.at[k, mine], (px, py, c), land.at[2 * px + py, mine]) for px, py in chips]
    return plan + [(src, land.at[k], sib, land.at[k])]


def _pass_plan(i, src, land, k, c, chips, sib):
    h = src.shape[1] // 2
    mine, theirs = pl.ds(c * h, h), pl.ds((1 - c) * h, h)
    return [(src.at[2 * px + py, mine], land.at[2 * px + py, mine], sib, land.at[2 * px + py, theirs]) for px, py in chips]


def _swap_plan(i, src, land, k, c, chips, sib):
    h = src.shape[1] // 2
    return [(src.at[:, pl.ds((1 - c) * h, h)], land, sib, land)]


def _scatter_plan(i, src, land, k, c, chips, sib):
    return [(src.at[2 * px + py], land.at[k], (px, py, c), land.at[2 * px + py]) for px, py in chips]


def _join_plan(layers):
    def plan(i, src, land, k, c, chips, sib):
        h = src.shape[1] // 2
        mine, theirs = pl.ds(c * h, h), pl.ds((1 - c) * h, h)
        return [(src.at[layers[i], mine], land.at[layers[i], mine], sib, land.at[layers[i], theirs])]
    return plan


def _split_start(name, plan, ncopy, srcs, after, land_shapes=None):
    n = len(srcs)
    if land_shapes is None:
        land_shapes = [(N_CHIPS,) + a.shape[-2:] for a in srcs]
    in_place = land_shapes == "self"
    lands = [] if in_place else [pltpu.with_memory_space_constraint(lax.empty(shape, a.dtype), pltpu.HBM)
                                 for shape, a in zip(land_shapes, srcs)]
    nbuf = n + len(lands)

    def body(*refs):
        src_refs = refs[1:1 + n]
        land_refs = src_refs if in_place else refs[1 + n:1 + nbuf]
        send, recv, token = refs[1 + nbuf], refs[2 + nbuf], refs[-1]
        x, y, c, chips = _place()
        for i in range(n):
            for j, (src, dst, peer, _) in enumerate(plan(i, src_refs[i], land_refs[i], 2 * x + y, c, chips, (x, y, 1 - c))):
                _remote(src, dst, send.at[i * ncopy + j], recv.at[i * ncopy + j], peer).start()
        token[...] = jnp.zeros_like(token)

    outs = pl.pallas_call(
        body, name=name,
        in_specs=[_ANY] + [_HBM] * nbuf,
        out_specs=[_SEM, _SEM] + [_HBM] * nbuf + [_VMEM],
        out_shape=[pltpu.SemaphoreType.DMA((n * ncopy,)), pltpu.SemaphoreType.DMA((n * ncopy,))]
        + [pltpu.HBM(a.shape, a.dtype) for a in list(srcs) + lands] + [jax.ShapeDtypeStruct((8, LANES), F32)],
        input_output_aliases={1 + i: 2 + i for i in range(nbuf)},
        compiler_params=pltpu.CompilerParams(has_side_effects=_EFFECT),
    )(after, *[pltpu.with_memory_space_constraint(a, pltpu.HBM) for a in srcs], *lands)
    return outs[0], outs[1], outs[2:2 + n], None if in_place else outs[2 + n:2 + nbuf], outs[-1]


def _split_wait(name, plan, started, after):
    send, recv, srcs, lands = started
    n = len(srcs)
    ncopy = send.shape[0] // n
    in_place = lands is None
    bufs = list(srcs) + ([] if in_place else list(lands))
    nbuf = len(bufs)

    def body(*refs):
        src_refs = refs[:n]
        land_refs = src_refs if in_place else refs[n:nbuf]
        send, recv = refs[nbuf], refs[nbuf + 1]
        x, y, c, chips = _place()
        for i in range(n):
            for j, (src, _, peer, landed) in enumerate(plan(i, src_refs[i], land_refs[i], 2 * x + y, c, chips, (x, y, 1 - c))):
                cp = _remote(src, landed, send.at[i * ncopy + j], recv.at[i * ncopy + j], peer)
                cp.wait_send()
                cp.wait_recv()

    outs = pl.pallas_call(
        body, name=name,
        in_specs=[_HBM] * nbuf + [_SEM, _SEM, _ANY],
        out_specs=[_HBM] * nbuf,
        out_shape=[pltpu.HBM(a.shape, a.dtype) for a in bufs],
        input_output_aliases={i: i for i in range(nbuf)},
        compiler_params=pltpu.CompilerParams(has_side_effects=_EFFECT),
    )(*bufs, send, recv, after)
    return outs[:n], outs[:n] if in_place else outs[n:]


def _allreduce_small(buf, after):
    rows = buf.shape[0]
    piece = rows // N_DEV

    def body(in_ref, after_ref, out_ref, land_ref, send, recv):
        x, y, c, _ = _place()
        me = 4 * x + 2 * y + c
        peers = [(1 - x if r & 4 else x, 1 - y if r & 2 else y, 1 - c if r & 1 else c) for r in range(1, N_DEV)]

        def rows_of(dev):
            return pl.ds(pl.multiple_of(dev * piece, 8), piece)

        first, second = [], []
        for r, (px, py, pc) in enumerate(peers):
            cp = _remote(in_ref.at[rows_of(4 * px + 2 * py + pc)], land_ref.at[me], send.at[0, r], recv.at[0, r], (px, py, pc))
            cp.start()
            first.append(cp)
        land_ref[me] = in_ref[rows_of(me), :]
        for r, (px, py, pc) in enumerate(peers):
            landed = land_ref.at[4 * px + 2 * py + pc]
            _remote(landed, landed, send.at[0, r], recv.at[0, r], (px, py, pc)).wait_recv()
        acc = land_ref[0]
        for d in range(1, N_DEV):
            acc = acc + land_ref[d]
        out_ref[rows_of(me), :] = acc
        for r, peer in enumerate(peers):
            cp = _remote(out_ref.at[rows_of(me)], out_ref.at[rows_of(me)], send.at[1, r], recv.at[1, r], peer)
            cp.start()
            second.append(cp)
        for r, (px, py, pc) in enumerate(peers):
            landed = out_ref.at[rows_of(4 * px + 2 * py + pc)]
            _remote(landed, landed, send.at[1, r], recv.at[1, r], (px, py, pc)).wait_recv()
        for cp in first + second:
            cp.wait_send()

    return pl.pallas_call(
        body, name="allreduce_small",
        in_specs=[_VMEM, _ANY], out_specs=_VMEM,
        out_shape=jax.ShapeDtypeStruct(buf.shape, F32),
        scratch_shapes=[pltpu.VMEM((N_DEV, piece, LANES), F32), pltpu.SemaphoreType.DMA((2, N_DEV - 1)),
                        pltpu.SemaphoreType.DMA((2, N_DEV - 1))],
        compiler_params=pltpu.CompilerParams(vmem_limit_bytes=_VMEM_LIMIT),
    )(buf, after)


def _div_tile(n, cap, mult):
    best = None
    for d in range(mult, min(n, cap) + 1, mult):
        if n % d == 0:
            best = d
    return best if best is not None else n


def _add_sibling(name, grad, got, c):
    nk, hr, cc = got.shape
    tr = _div_tile(hr, 512, 16)
    nt = hr // tr

    def body(c_ref, g_ref, o_ref, s_ref):
        s_ref[...] = (g_ref[...].astype(F32) + o_ref[...].astype(F32)).astype(s_ref.dtype)

    blk = (None, tr, cc)
    return pl.pallas_call(
        body, name=name,
        grid_spec=pltpu.PrefetchScalarGridSpec(
            num_scalar_prefetch=1, grid=(nk, nt),
            in_specs=[pl.BlockSpec(blk, lambda i, q, c_ref: (i, c_ref[0] * nt + q, 0)),
                      pl.BlockSpec(blk, lambda i, q, c_ref: (i, q, 0))],
            out_specs=pl.BlockSpec(blk, lambda i, q, c_ref: (i, q, 0))),
        out_shape=jax.ShapeDtypeStruct(got.shape, got.dtype),
        compiler_params=_params("parallel", "parallel"),
    )(c, grad, got)


def _add_chips(name, mine, got, place, buf, l):
    nk, hr, cc = got.shape
    tr = _div_tile(hr, 512, 16)
    nt = hr // tr

    def body(*refs):
        acc = refs[1][...].astype(F32)
        for q in range(1, nk):
            acc = acc + refs[1 + q][...].astype(F32)
        refs[2 + nk][...] = acc

    def part(q):
        return pl.BlockSpec((None, tr, cc), lambda i, p_ref: ((p_ref[0] + q) % nk, i, 0))

    return pl.pallas_call(
        body, name=name,
        grid_spec=pltpu.PrefetchScalarGridSpec(
            num_scalar_prefetch=1, grid=(nt,),
            in_specs=[part(q) for q in range(nk)] + [_ANY],
            out_specs=pl.BlockSpec((None, tr, cc), lambda i, p_ref: (l, p_ref[1] * nt + i, 0))),
        out_shape=jax.ShapeDtypeStruct(buf.shape, F32),
        input_output_aliases={1 + nk: 0},
        compiler_params=_params("parallel"),
    )(place, mine, *([got] * (nk - 1)), buf)


def _adamw(name, w, g, m, v, after=None):
    shape = w.shape
    cols = shape[-1]
    rows = w.size // cols
    tr = rows if rows * cols <= 2 ** 18 else _div_tile(rows, max(8, 2 ** 18 // cols), 8)
    c1 = 1.0 - ADAM_B1 ** ADAM_STEP
    c2 = 1.0 - ADAM_B2 ** ADAM_STEP
    extra = [] if after is None else [after]

    def body(*refs):
        w_ref, g_ref, m_ref, v_ref = refs[:4]
        d_ref, mo_ref, vo_ref, go_ref = refs[4 + len(extra):]
        gg = g_ref[...]
        mn = ADAM_B1 * m_ref[...] + (1.0 - ADAM_B1) * gg
        vn = ADAM_B2 * v_ref[...] + (1.0 - ADAM_B2) * (gg * gg)
        d_ref[...] = -ADAM_LR * ((mn / c1) / (jnp.sqrt(vn / c2) + ADAM_EPS) + ADAM_WD * w_ref[...])
        mo_ref[...] = mn
        vo_ref[...] = vn
        go_ref[...] = gg

    blk = pl.BlockSpec((tr, cols), lambda i: (i, 0))
    sds = jax.ShapeDtypeStruct((rows, cols), F32)
    outs = pl.pallas_call(
        body, name=name, grid=(rows // tr,),
        in_specs=[blk] * 4 + [_ANY] * len(extra), out_specs=(blk,) * 4, out_shape=(sds,) * 4,
        compiler_params=_params("parallel"),
    )(*[a.reshape(rows, cols) for a in (w, g, m, v)], *extra)
    return [o.reshape(shape) for o in outs]


_WEIGHTS = ["ffn1_norm", "ffn1_w_in", "ffn1_w_out", "mix_norm", "ffn2_norm", "ffn2_w_in", "ffn2_w_out",
            "ev_w_in", "ev_conv_w", "ev_q_norm", "ev_k_norm", "ev_w_out", "od_w_in", "od_pool_w",
            "od_pool_scale", "od_sgu_norm", "od_sgu_w", "od_sgu_b", "od_w_out", "final_norm"]
_BIG = ["ffn1_w_in", "ffn1_w_out", "ffn2_w_in", "ffn2_w_out", "ev_w_in", "ev_w_out", "od_w_in", "od_w_out"]
_SMALL_SHARDED = ["ev_conv_w", "od_pool_scale", "od_sgu_norm"]


def _pad_rows(a, mult=8):
    pad = (-a.shape[0]) % mult
    return a if pad == 0 else jnp.concatenate([a, jnp.zeros((pad,) + a.shape[1:], a.dtype)], axis=0)


def _join_cols(g):
    return g.transpose(1, 0, 2).reshape(g.shape[1], N_CHIPS * g.shape[2])


def _split_cols(w):
    return w.reshape(w.shape[0], N_CHIPS, w.shape[1] // N_CHIPS).transpose(1, 0, 2)


def kernel(x, ffn1_norm, ffn1_w_in, ffn1_w_out, mix_norm, ffn2_norm, ffn2_w_in, ffn2_w_out, ev_w_in, ev_conv_w,
           ev_q_norm, ev_k_norm, ev_w_out, od_w_in, od_pool_w, od_pool_scale, od_sgu_norm, od_sgu_w, od_sgu_b,
           od_w_out, final_norm, loss_target, m_ffn1_norm, m_ffn1_w_in, m_ffn1_w_out, m_mix_norm, m_ffn2_norm,
           m_ffn2_w_in, m_ffn2_w_out, m_ev_w_in, m_ev_conv_w, m_ev_q_norm, m_ev_k_norm, m_ev_w_out, m_od_w_in,
           m_od_pool_w, m_od_pool_scale, m_od_sgu_norm, m_od_sgu_w, m_od_sgu_b, m_od_w_out, m_final_norm, v_ffn1_norm,
           v_ffn1_w_in, v_ffn1_w_out, v_mix_norm, v_ffn2_norm, v_ffn2_w_in, v_ffn2_w_out, v_ev_w_in, v_ev_conv_w,
           v_ev_q_norm, v_ev_k_norm, v_ev_w_out, v_od_w_in, v_od_pool_w, v_od_pool_scale, v_od_sgu_norm, v_od_sgu_w,
           v_od_sgu_b, v_od_w_out, v_final_norm):
    return _step(x, ffn1_norm, ffn1_w_in, ffn1_w_out, mix_norm, ffn2_norm, ffn2_w_in, ffn2_w_out, ev_w_in, ev_conv_w,
                 ev_q_norm, ev_k_norm, ev_w_out, od_w_in, od_pool_w, od_pool_scale, od_sgu_norm, od_sgu_w, od_sgu_b,
                 od_w_out, final_norm, loss_target, m_ffn1_norm, m_ffn1_w_in, m_ffn1_w_out, m_mix_norm, m_ffn2_norm,
                 m_ffn2_w_in, m_ffn2_w_out, m_ev_w_in, m_ev_conv_w, m_ev_q_norm, m_ev_k_norm, m_ev_w_out, m_od_w_in,
                 m_od_pool_w, m_od_pool_scale, m_od_sgu_norm, m_od_sgu_w, m_od_sgu_b, m_od_w_out, m_final_norm,
                 v_ffn1_norm, v_ffn1_w_in, v_ffn1_w_out, v_mix_norm, v_ffn2_norm, v_ffn2_w_in, v_ffn2_w_out,
                 v_ev_w_in, v_ev_conv_w, v_ev_q_norm, v_ev_k_norm, v_ev_w_out, v_od_w_in, v_od_pool_w,
                 v_od_pool_scale, v_od_sgu_norm, v_od_sgu_w, v_od_sgu_b, v_od_w_out, v_final_norm)


def _step(*args):
    nw = len(_WEIGHTS)
    x = args[0]
    w = dict(zip(_WEIGHTS, args[1:1 + nw]))
    target = args[1 + nw]
    m = dict(zip(_WEIGHTS, args[2 + nw:2 + 2 * nw]))
    v = dict(zip(_WEIGHTS, args[2 + 2 * nw:2 + 3 * nw]))
    depth = w["ffn1_norm"].shape[0]
    n_even, n_odd = w["ev_w_in"].shape[0], w["od_w_in"].shape[0]
    chip = 2 * lax.axis_index("x") + lax.axis_index("y")
    place = jnp.stack([chip, lax.axis_index("c")]).astype(jnp.int32)
    core = place[1:2]

    def sharded(l, block):
        if block == "mix":
            block = "ev" if l % 2 == 0 else "od"
            return [(block + "_w_in", l // 2), (block + "_w_out", l // 2)]
        return [(block + "_w_in", l), (block + "_w_out", l)]

    def shards(group, zero):
        return [(w[n][i] + zero).astype(_ACT) for l, block in group for n, i in sharded(l, block)]

    later = ([[(0, "ffn1")], [(0, "mix"), (0, "ffn2")]]
             + [[(l, "ffn1"), (l, "mix"), (l, "ffn2")] for l in range(1, depth)])
    gathering, after, zero = [], core, 0.0
    for i, group in enumerate(later):
        gathering.append(_split_start(f"gather_start{i}", _gather_plan, N_CHIPS, shards(group, zero), after))
        after = gathering[-1][4]
        zero = after[0, 0]
    small_rows = [w["ev_conv_w"].reshape(3 * n_even, LANES), w["od_pool_scale"], w["od_sgu_norm"]]
    _, small = _gather_shards([], _pad_rows(jnp.concatenate(small_rows, axis=0)))
    conv_w = small[:, :3 * n_even].reshape(N_CHIPS, n_even, 3, LANES).transpose(1, 2, 0, 3).reshape(n_even, 3, CONV_WIDTH)
    pool_scale = small[:, 3 * n_even:3 * n_even + n_odd].transpose(1, 0, 2).reshape(n_odd, HALF)
    sgu_norm = small[:, 3 * n_even + n_odd:3 * n_even + 2 * n_odd].transpose(1, 0, 2).reshape(n_odd, HALF)
    gathered = {}

    def rows(g):
        return g.reshape(N_CHIPS * g.shape[1], g.shape[2])

    passing = {}

    def fetch(i, x_in):
        got = _split_wait(f"gather_wait{i}", _gather_plan, gathering[i][:4], x_in)[1]
        passing[i] = _split_start(f"pass_start{i}", _pass_plan, N_CHIPS - 1, got, x_in, "self")
        return passing[i][4][0, 0]

    def weights_of(l, block, x_in):
        zero = after[0, 0] if (l, block) == (0, "ffn1") else 0.0
        if (l, block) not in gathered:
            i = next(i for i, group in enumerate(later) if (l, block) in group)
            if i not in passing:
                zero = zero + fetch(i, x_in)
            got = _split_wait(f"pass_wait{i}", _pass_plan, passing.pop(i)[:4], x_in)[0]
            for n, key in enumerate(later[i]):
                gathered[key] = got[2 * n:2 * n + 2]
        if block == "ffn2" and l + 1 < depth:
            zero = zero + fetch(next(i for i, group in enumerate(later) if (l + 1, "ffn1") in group), x_in)
        w_in, w_out = gathered[(l, block)]
        if block != "mix":
            return dict(norm=w[block + "_norm"][l] + zero, w_in4=w_in, w_out=rows(w_out))
        j = l // 2
        if l % 2 == 0:
            mix = dict(conv_w=conv_w[j], q_gain=w["ev_q_norm"][j], k_gain=w["ev_k_norm"][j])
        else:
            mix = dict(pool_w=w["od_pool_w"][j], pool_scale=pool_scale[j], sgu_norm=sgu_norm[j],
                       sgu_w=w["od_sgu_w"][j], sgu_b=w["od_sgu_b"][j])
        return dict(mix, norm=w["mix_norm"][l] + zero, w_in=_join_cols(w_in), w_out=rows(w_out))

    def by_chip(dw):
        return dw.reshape(N_CHIPS, dw.shape[0] // N_CHIPS, dw.shape[1])

    bufs = {n: lax.empty(w[n].shape, F32) for n in _BIG}
    small_grads = {n: [None] * w[n].shape[0] for n in _WEIGHTS if n not in _BIG and n != "final_norm"}
    swapping, scattering, joining, group = [], [], [], []

    def finish_swap(after):
        tag, names, started = swapping.pop()
        local, from_sibling = _split_wait(f"swap_wait{tag}", _swap_plan, started[:4], after)
        halves = [_add_sibling(f"add_sibling{tag}_{n}", a, b, core) for (n, _), a, b in zip(names, local, from_sibling)]
        scattering.append((tag, names, _split_start(f"scatter_start{tag}", _scatter_plan, N_CHIPS - 1, halves, after)))
        return scattering[-1][2][4][0, 0]

    def finish_scatter(after):
        tag, names, started = scattering.pop(0)
        halves, got = _split_wait(f"scatter_wait{tag}", _scatter_plan, started[:4], after)
        for i, (n, j) in enumerate(names):
            bufs[n] = _add_chips(f"add_chips{tag}_{n}", halves[i], got[i], place, bufs[n], j)
        layers = [j for _, j in names]
        started = _split_start(f"join_start{tag}", _join_plan(layers), 1, [bufs[n] for n, _ in names], after, "self")
        for (n, _), b in zip(names, started[2]):
            bufs[n] = b
        joining.append((tag, names, layers, started))
        return started[4]

    def finish_join(after):
        tag, names, layers, started = joining.pop(0)
        joined = _split_wait(f"join_wait{tag}", _join_plan(layers), (started[0], started[1], [bufs[n] for n, _ in names], None),
                             after)[0]
        for (n, _), b in zip(names, joined):
            bufs[n] = b

    def small_done(l, block, g):
        if block == "mix":
            renamed = (dict(conv_w="ev_conv_w", q_gain="ev_q_norm", k_gain="ev_k_norm") if l % 2 == 0 else
                       dict(pool_w="od_pool_w", pool_scale="od_pool_scale", sgu_norm="od_sgu_norm", sgu_w="od_sgu_w",
                            sgu_b="od_sgu_b"))
            for key, n in renamed.items():
                small_grads[n][l // 2] = g[key]
        small_grads[block + "_norm"][l] = g["norm"]

    def mid(a):
        zero = 0.0
        if swapping:
            if scattering:
                zero = zero + finish_scatter(a)[0, 0]
            zero = zero + finish_swap(a)
        return zero

    def grads_ready(l, block, g, a):
        local = [_split_cols(g["w_in"]) if block == "mix" else g["w_in4"], by_chip(g["w_out"])]
        group.extend(zip(sharded(l, block), local))
        if block == "ffn2" or (block == "mix" and l > 0):
            return 0.0
        tag = f"{l}_{block}"
        names, local = [n for n, _ in group], [b for _, b in group]
        group.clear()
        shapes = [(N_CHIPS, b.shape[1] // 2, b.shape[2]) for b in local]
        swapping.append((tag, names, _split_start(f"swap_start{tag}", _swap_plan, 1, local, core, shapes)))
        return swapping[-1][2][4][0, 0]

    loss_part, grad_x, dfinal = _local_step(x, target, depth, weights_of, w["final_norm"], mid, grads_ready, small_done)
    loss = lax.psum(loss_part, ("x", "y", "c"))

    grads, updates = {}, {}

    def update(n, after):
        updates[n] = _adamw("adamw_" + n, w[n], grads[n] if n in grads else bufs[n], m[n], v[n], after)
        return updates[n][1]

    finish_swap(grad_x)
    behind = scattering[-1][2][4]
    while len(joining) > 0:
        finish_join(behind)
    behind = finish_scatter(behind)
    for n in ("od_w_in", "od_w_out"):
        behind = update(n, behind)
    finish_join(behind)
    for n in ("ffn2_w_in", "ffn2_w_out", "ev_w_in", "ev_w_out"):
        behind = update(n, behind)
    behind = finish_scatter(behind)
    small_grads = {n: jnp.stack(parts) for n, parts in small_grads.items()}
    small_grads["final_norm"] = dfinal
    names = list(small_grads)
    flat = jnp.concatenate([small_grads[n].reshape(-1) for n in names])
    total = flat.shape[0]
    flat = jnp.concatenate([flat, jnp.zeros((-total) % (N_DEV * 8 * LANES), F32)])
    summed = _allreduce_small(flat.reshape(-1, LANES), behind).reshape(-1)
    finish_join(summed)
    for n in ("ffn1_w_in", "ffn1_w_out"):
        behind = update(n, behind)
    off = 0
    for n in names:
        size = small_grads[n].size
        full_grad = summed[off:off + size].reshape(small_grads[n].shape)
        off += size
        if n in _SMALL_SHARDED:
            full_grad = lax.dynamic_slice_in_dim(full_grad, chip * LANES, LANES, axis=full_grad.ndim - 1)
        grads[n] = full_grad
    for n in _WEIGHTS:
        if n not in updates:
            behind = update(n, behind)
    return (loss, grad_x, *[updates[n][3] for n in _WEIGHTS], *[updates[n][0] for n in _WEIGHTS],
            *[updates[n][1] for n in _WEIGHTS], *[updates[n][2] for n in _WEIGHTS])
```

```python
import jax
import jax.numpy as jnp
from jax import lax
from jax.experimental import pallas as pl
from jax.experimental.pallas import tpu as pltpu

F32 = jnp.float32
_MXU = jnp.bfloat16
_ACT = jnp.bfloat16

D_MODEL = 1024
GRID_W = 64
HEAD_DIM = 64
N_Q_HEADS = 8
N_KV_HEADS = 2
Q_PER_KV = N_Q_HEADS // N_KV_HEADS
ATTN_WIDTH = N_Q_HEADS * HEAD_DIM
KV_WIDTH = N_KV_HEADS * HEAD_DIM
ROPE_THETA = 10000.0
CONV_WIDTH = D_MODEL // 2
POOL_RADII = (1, 2, 4, 8)
POOL_GROUP = 128
SGU_GROUP = 128
SGU_CHUNK = 128
N_GROUPS = 4
HALF = D_MODEL // 2
EPS = 1e-6
HALO = 8
LANES = 128
N_CHIPS = 4
N_DEV = 8

ADAM_LR = 0.001
ADAM_B1 = 0.9
ADAM_B2 = 0.999
ADAM_EPS = 1e-08
ADAM_WD = 0.01
ADAM_STEP = 10

_VMEM_LIMIT = 56 * 2 ** 20
_MESH = pl.DeviceIdType.MESH
_ANY = pl.BlockSpec(memory_space=pl.ANY)
_VMEM = pl.BlockSpec(memory_space=pltpu.VMEM)

_DN = {
    "nn": (((1,), (0,)), ((), ())),
    "nt": (((1,), (1,)), ((), ())),
    "tn": (((0,), (0,)), ((), ())),
}


def _params(*sem):
    return pltpu.CompilerParams(dimension_semantics=sem, vmem_limit_bytes=_VMEM_LIMIT)


def _tile(n, cap):
    best = None
    d = LANES
    while d <= min(n, cap):
        if n % d == 0:
            best = d
        d += LANES
    return best if best is not None else n


def _dot(a, b, mode="nn"):
    return lax.dot_general(a.astype(_MXU), b.astype(_MXU), _DN[mode], preferred_element_type=F32)


def _cat(*vals):
    vals = [v.astype(_MXU) for v in vals]
    return vals[0] if len(vals) == 1 else jnp.concatenate(vals, axis=1)


def _sigmoid(g):
    return 1.0 / (1.0 + jnp.exp(-g))


def _norm_rows(x, g):
    r = lax.rsqrt(jnp.mean(x * x, axis=-1, keepdims=True) + EPS)
    return (x * r) * g


def _swiglu(g, u):
    return (g * _sigmoid(g)) * u


_GELU_C = 0.7978845608028654


def _gelu(x):
    return 0.5 * x * (1.0 + jnp.tanh(_GELU_C * (x + 0.044715 * (x * x * x))))


def _gelu_grad(x):
    t = jnp.tanh(_GELU_C * (x + 0.044715 * (x * x * x)))
    return 0.5 * (1.0 + t) + 0.5 * x * (1.0 - t * t) * (_GELU_C * (1.0 + 3.0 * 0.044715 * (x * x)))


def _mm(name, grid, mode, a_ops, b_ops, e_ops, out_shape, out_specs, acc_shape, a_fn=_cat, b_fn=_cat, epi=None,
        n_outer=False, m_carried=False, b_pick=None):
    ni, nj, nk = grid
    na, nb, ne = len(a_ops), len(b_ops), len(e_ops)
    multi = isinstance(out_shape, (list, tuple))
    no = len(out_shape) if multi else 1

    def body(*refs):
        a_refs = refs[:na]
        b_refs = refs[na:na + nb]
        e_refs = refs[na + nb:na + nb + ne]
        o_refs = refs[na + nb + ne:na + nb + ne + no]
        a = a_fn(*[r[...] for r in a_refs])
        if b_pick is None:
            b = b_fn(*[r[...] for r in b_refs])
        else:
            b = b_pick(b_refs, pl.program_id(1), pl.program_id(2))
        p = _dot(a, b, mode)

        def finish(acc):
            if epi is None:
                o_refs[0][...] = acc.astype(o_refs[0].dtype)
            else:
                epi(acc, [r[...] for r in e_refs], o_refs)

        if nk == 1:
            finish(p)
        else:
            acc_ref = refs[-1]
            k = pl.program_id(2)

            @pl.when(k == 0)
            def _():
                acc_ref[...] = p

            @pl.when((k > 0) & (k < nk - 1))
            def _():
                acc_ref[...] += p

            @pl.when(k == nk - 1)
            def _():
                finish(acc_ref[...] + p)

    ops = list(a_ops) + list(b_ops) + list(e_ops)
    if n_outer:
        def flip(spec):
            return pl.BlockSpec(spec.block_shape, lambda j, i, k, f=spec.index_map: f(i, j, k))

        grid = (nj, ni, nk)
        ops = [(a, flip(s)) for a, s in ops]
        out_specs = [flip(s) for s in out_specs] if multi else flip(out_specs)
    return pl.pallas_call(
        body, name=name, grid=grid,
        in_specs=[s for _, s in ops],
        out_specs=out_specs, out_shape=out_shape,
        scratch_shapes=[pltpu.VMEM(acc_shape, F32)] if nk > 1 else [],
        compiler_params=_params(*(("arbitrary",) * 3 if m_carried else ("parallel", "parallel", "arbitrary"))),
    )(*[a for a, _ in ops])


def _whole(a):
    return pl.BlockSpec(a.shape, lambda i, j, k: (0,) * a.ndim, pipeline_mode=pl.Buffered(1))


def _norm_bwd_epi(acc, e, o):
    xf, dres, g = e
    r = lax.rsqrt(jnp.mean(xf * xf, axis=-1, keepdims=True) + EPS)
    xhat = xf * r
    dgx = acc * g
    m = jnp.mean(dgx * xhat, axis=-1, keepdims=True)
    o[0][...] = dres + r * (dgx - xhat * m)
    part = jnp.sum(acc * xhat, axis=0, keepdims=True)
    i = pl.program_id(0)

    @pl.when(i == 0)
    def _():
        o[1][...] = part

    @pl.when(i > 0)
    def _():
        o[1][...] += part


def _norm_bwd_ops(x, dres, gain, tm):
    t, d = x.shape
    row = pl.BlockSpec((tm, d), lambda i, j, k: (i, 0))
    vec = pl.BlockSpec((1, d), lambda i, j, k: (0, 0))
    return ([(x, row), (dres, row), (gain.reshape(1, d), vec)],
            [jax.ShapeDtypeStruct((t, d), F32), jax.ShapeDtypeStruct((1, d), F32)], [row, vec])


def _rows(t):
    return _tile(t, 512)


def _rmsnorm_fwd(name, x, gain):
    t, d = x.shape
    tr = _rows(t)

    def body(x_ref, g_ref, h_ref):
        xf = x_ref[...]
        r = lax.rsqrt(jnp.mean(xf * xf, axis=-1, keepdims=True) + EPS)
        h_ref[...] = ((xf * r) * g_ref[...]).astype(h_ref.dtype)

    return pl.pallas_call(
        body, name=name, grid=(t // tr,),
        in_specs=[pl.BlockSpec((tr, d), lambda i: (i, 0)), pl.BlockSpec((1, d), lambda i: (0, 0))],
        out_specs=pl.BlockSpec((tr, d), lambda i: (i, 0)),
        out_shape=jax.ShapeDtypeStruct((t, d), _ACT),
        compiler_params=_params("parallel"),
    )(x, gain.reshape(1, d))


def _rmsnorm_bwd(name, dh, x, gain, dres):
    t, d = x.shape
    tr = _rows(t)

    def body(dh_ref, x_ref, g_ref, dres_ref, dx_ref, dg_ref):
        i = pl.program_id(0)
        xf = x_ref[...]
        r = lax.rsqrt(jnp.mean(xf * xf, axis=-1, keepdims=True) + EPS)
        xhat = xf * r
        dy = dh_ref[...].astype(F32)
        dgx = dy * g_ref[...]
        m = jnp.mean(dgx * xhat, axis=-1, keepdims=True)
        dx_ref[...] = dres_ref[...] + r * (dgx - xhat * m)
        part = jnp.sum(dy * xhat, axis=0, keepdims=True)

        @pl.when(i == 0)
        def _():
            dg_ref[...] = part

        @pl.when(i > 0)
        def _():
            dg_ref[...] += part

    row = pl.BlockSpec((tr, d), lambda i: (i, 0))
    vec = pl.BlockSpec((1, d), lambda i: (0, 0))
    dx, dg = pl.pallas_call(
        body, name=name, grid=(t // tr,),
        in_specs=[row, row, vec, row],
        out_specs=(row, vec),
        out_shape=(jax.ShapeDtypeStruct((t, d), F32), jax.ShapeDtypeStruct((1, d), F32)),
        compiler_params=_params("arbitrary"),
    )(dh, x, gain.reshape(1, d), dres)
    return dx, dg.reshape(d)


def _final_loss(name, x, gain, target):
    t, d = x.shape
    tr = _rows(t)

    def body(x_ref, g_ref, t_ref, dx_ref, dg_ref, loss_ref):
        i = pl.program_id(0)
        xf = x_ref[...]
        r = lax.rsqrt(jnp.mean(xf * xf, axis=-1, keepdims=True) + EPS)
        xhat = xf * r
        g = g_ref[...]
        err = xhat * g - t_ref[...]
        lpart = 0.5 * jnp.sum(jnp.mean(err * err, axis=-1, keepdims=True), axis=0, keepdims=True)
        dy = err * (1.0 / d)
        dgx = dy * g
        m = jnp.mean(dgx * xhat, axis=-1, keepdims=True)
        dx_ref[...] = r * (dgx - xhat * m)
        part = jnp.sum(dy * xhat, axis=0, keepdims=True)
        lrow = jnp.broadcast_to(lpart, (1, LANES))

        @pl.when(i == 0)
        def _():
            dg_ref[...] = part
            loss_ref[...] = lrow

        @pl.when(i > 0)
        def _():
            dg_ref[...] += part
            loss_ref[...] += lrow

    row = pl.BlockSpec((tr, d), lambda i: (i, 0))
    vec = pl.BlockSpec((1, d), lambda i: (0, 0))
    dx, dg, loss = pl.pallas_call(
        body, name=name, grid=(t // tr,),
        in_specs=[row, vec, row],
        out_specs=(row, vec, pl.BlockSpec((1, LANES), lambda i: (0, 0))),
        out_shape=(jax.ShapeDtypeStruct((t, d), F32), jax.ShapeDtypeStruct((1, d), F32),
                   jax.ShapeDtypeStruct((1, LANES), F32)),
        compiler_params=_params("arbitrary"),
    )(x, gain.reshape(1, d), target)
    return loss[0, 0], dx, dg.reshape(d)


_FFN_TILES = dict(in_tm=1024, out_tm=512, dact_tm=512, dwout_tk=1024, dh_tm=512, dwin_tk=2048)


def _ffn_tiles(layer, which):
    return _FFN_TILES


def _ffn_fwd(tag, x, gain, w_in4, w_out, cfg):
    t, d = x.shape
    fs = w_in4.shape[2]
    f = 2 * fs
    tm = _tile(t, cfg["in_tm"])
    gain = gain.reshape(1, d)
    gu = _mm(
        tag + "_in", (t // tm, N_CHIPS, 1), "nn",
        [(x, pl.BlockSpec((tm, d), lambda i, j, k: (i, 0))), (gain, pl.BlockSpec((1, d), lambda i, j, k: (0, 0)))],
        [(w_in4, pl.BlockSpec((None, d, fs), lambda i, j, k: (j, 0, 0)))], [],
        jax.ShapeDtypeStruct((2, t, f), _ACT),
        pl.BlockSpec((None, tm, fs), lambda i, j, k: (j // 2, i, j % 2)), None, a_fn=_norm_rows)
    tm2 = _tile(t, cfg["out_tm"])

    def epi(acc, e, o):
        o[0][...] = e[0] + 0.5 * acc

    x_out = _mm(
        tag + "_out", (t // tm2, 1, 1), "nn",
        [(gu, pl.BlockSpec((None, tm2, f), lambda i, j, k: (0, i, 0))),
         (gu, pl.BlockSpec((None, tm2, f), lambda i, j, k: (1, i, 0)))],
        [(w_out, pl.BlockSpec((f, d), lambda i, j, k: (0, 0)))],
        [(x, pl.BlockSpec((tm2, d), lambda i, j, k: (i, 0)))],
        jax.ShapeDtypeStruct((t, d), F32),
        pl.BlockSpec((tm2, d), lambda i, j, k: (i, 0)), None,
        a_fn=_swiglu, epi=epi)
    return x_out, (x, gu)


def _ffn_bwd(tag, dxo, saved, gain, w_in4, w_out, cfg, mid, ready):
    x, gu = saved
    t, d = x.shape
    fs = w_in4.shape[2]
    f = 2 * fs
    tm = _tile(t, cfg["dact_tm"])
    tk = _tile(t, cfg["dwout_tk"])

    def epi_act(acc, e, o):
        g, u = e
        da = (0.5 * acc).astype(g.dtype)
        sig = _sigmoid(g)
        silu = g * sig
        o[0][0] = (da * u * (sig + silu * (1.0 - sig))).astype(o[0].dtype)
        o[0][1] = (da * silu).astype(o[0].dtype)

    dgu = _mm(
        tag + "_dact", (t // tm, 2, 1), "nt",
        [(dxo, pl.BlockSpec((tm, d), lambda i, j, k: (i, 0)))],
        [(w_out, _whole(w_out))],
        [(gu, pl.BlockSpec((None, tm, fs), lambda i, j, k: (0, i, j))),
         (gu, pl.BlockSpec((None, tm, fs), lambda i, j, k: (1, i, j)))],
        jax.ShapeDtypeStruct((2, t, f), _ACT),
        pl.BlockSpec((2, tm, fs), lambda i, j, k: (0, i, j)), None, epi=epi_act,
        b_pick=lambda b, j, k: b[0][pl.ds(pl.multiple_of(j * fs, LANES), fs), :])
    gain = gain + mid(dgu)

    def epi_half(acc, e, o):
        o[0][...] = (0.5 * acc).astype(o[0].dtype)

    dw_out = _mm(
        tag + "_dwout", (2, 1, t // tk), "tn",
        [(gu, pl.BlockSpec((None, tk, fs), lambda i, j, k: (0, k, i))),
         (gu, pl.BlockSpec((None, tk, fs), lambda i, j, k: (1, k, i)))],
        [(dxo, pl.BlockSpec((tk, d), lambda i, j, k: (k, 0)))], [],
        jax.ShapeDtypeStruct((f, d), _ACT),
        pl.BlockSpec((fs, d), lambda i, j, k: (i, 0)), (fs, d),
        a_fn=_swiglu, epi=epi_half)
    tk = _tile(t, cfg["dwin_tk"])
    dw_in4 = _mm(
        tag + "_dwin", (1, N_CHIPS, t // tk), "tn",
        [(x, pl.BlockSpec((tk, d), lambda i, j, k: (k, 0))),
         (gain.reshape(1, d), pl.BlockSpec((1, d), lambda i, j, k: (0, 0)))],
        [(dgu, pl.BlockSpec((None, tk, fs), lambda i, j, k: (j // 2, k, j % 2)))], [],
        jax.ShapeDtypeStruct((N_CHIPS, d, fs), _ACT),
        pl.BlockSpec((None, d, fs), lambda i, j, k: (j, 0, 0)), (d, fs), a_fn=_norm_rows)
    gain = gain + ready(dict(w_in4=dw_in4, w_out=dw_out), dw_in4)
    tm = _tile(t, cfg["dh_tm"])
    e_ops, shapes, specs = _norm_bwd_ops(x, dxo, gain, tm)
    dx, dgain = _mm(
        tag + "_dh", (t // tm, 1, 2), "nt",
        [(dgu, pl.BlockSpec((None, tm, f), lambda i, j, k: (k, i, 0)))],
        [(w_in4, _whole(w_in4))], e_ops, shapes, specs, (tm, d), epi=_norm_bwd_epi, m_carried=True,
        b_pick=lambda b, j, k: jnp.concatenate([b[0][2 * k], b[0][2 * k + 1]], axis=1))
    return dx, dgain.reshape(d)


_MIX_TILES = dict(tm=1024, dwout_tk=2048, dwin_tk=1024)


def _proj_in(tag, h, w_in):
    t, d = h.shape
    n = w_in.shape[1]
    tm = _tile(t, _MIX_TILES["tm"])
    return _mm(
        tag + "_in", (t // tm, 1, 1), "nn",
        [(h, pl.BlockSpec((tm, d), lambda i, j, k: (i, 0)))],
        [(w_in, pl.BlockSpec((d, n), lambda i, j, k: (0, 0)))], [],
        jax.ShapeDtypeStruct((t, n), _ACT),
        pl.BlockSpec((tm, n), lambda i, j, k: (i, 0)), None)


def _proj_out(tag, x, parts, w_out):
    t, d = x.shape
    tm = _tile(t, _MIX_TILES["tm"])

    def epi(acc, e, o):
        o[0][...] = e[0] + acc

    return _mm(
        tag + "_out", (t // tm, 1, 1), "nn",
        [(p, pl.BlockSpec((tm, p.shape[1]), lambda i, j, k: (i, 0))) for p in parts],
        [(w_out, pl.BlockSpec(w_out.shape, lambda i, j, k: (0, 0)))],
        [(x, pl.BlockSpec((tm, d), lambda i, j, k: (i, 0)))],
        jax.ShapeDtypeStruct((t, d), F32),
        pl.BlockSpec((tm, d), lambda i, j, k: (i, 0)), None, epi=epi)


def _proj_out_bwd(tag, dxo, parts, w_out):
    t, d = dxo.shape
    mix = w_out.shape[0]
    tm = _tile(t, _MIX_TILES["tm"])
    tk = _tile(t, _MIX_TILES["dwout_tk"])
    d_mix = _mm(
        tag + "_dmix", (t // tm, 1, 1), "nt",
        [(dxo, pl.BlockSpec((tm, d), lambda i, j, k: (i, 0)))],
        [(w_out, pl.BlockSpec((mix, d), lambda i, j, k: (0, 0)))], [],
        jax.ShapeDtypeStruct((t, mix), F32),
        pl.BlockSpec((tm, mix), lambda i, j, k: (i, 0)), None)
    dw_out = _mm(
        tag + "_dwout", (1, 1, t // tk), "tn",
        [(p, pl.BlockSpec((tk, p.shape[1]), lambda i, j, k: (k, 0))) for p in parts],
        [(dxo, pl.BlockSpec((tk, d), lambda i, j, k: (k, 0)))], [],
        jax.ShapeDtypeStruct((mix, d), _ACT),
        pl.BlockSpec((mix, d), lambda i, j, k: (0, 0)), (mix, d))
    return d_mix, dw_out


def _proj_in_bwd(tag, h, dparts, w_in, x, dres, gain, ready):
    t, d = h.shape
    n = w_in.shape[1]
    tm = _tile(t, _MIX_TILES["tm"])
    tk = _tile(t, _MIX_TILES["dwin_tk"])
    dw_in = _mm(
        tag + "_dwin", (1, 1, t // tk), "tn",
        [(h, pl.BlockSpec((tk, d), lambda i, j, k: (k, 0)))],
        [(p, pl.BlockSpec((tk, p.shape[1]), lambda i, j, k: (k, 0))) for p in dparts], [],
        jax.ShapeDtypeStruct((d, n), _ACT),
        pl.BlockSpec((d, n), lambda i, j, k: (0, 0)), (d, n))
    e_ops, shapes, specs = _norm_bwd_ops(x, dres, gain + ready(dw_in), tm)
    dx, dgain = _mm(
        tag + "_dh", (t // tm, 1, 1), "nt",
        [(p, pl.BlockSpec((tm, p.shape[1]), lambda i, j, k: (i, 0))) for p in dparts],
        [(w_in, pl.BlockSpec((d, n), lambda i, j, k: (0, 0)))], e_ops, shapes, specs, None,
        epi=_norm_bwd_epi, m_carried=True)
    return dx, dgain.reshape(d)


def _shifted(pad_ref, val, s):
    pad_ref[pl.ds(HALO, s), :] = val
    return pad_ref[pl.ds(HALO - 1, s), :], pad_ref[pl.ds(HALO + 1, s), :]


def _zero_halo(pad_ref, s):
    z = jnp.zeros((HALO, pad_ref.shape[1]), F32)
    pad_ref[pl.ds(0, HALO), :] = z
    pad_ref[pl.ds(HALO + s, HALO), :] = z


def _conv_fwd(tag, proj, conv_w, nb, s):
    t = proj.shape[0]
    ncb = CONV_WIDTH // LANES

    def body(gb_ref, gc_ref, hc_ref, w_ref, a_ref, pad_ref):
        _zero_halo(pad_ref, s)
        cg = gc_ref[...].astype(F32) * hc_ref[...].astype(F32)
        prev, nxt = _shifted(pad_ref, cg, s)
        w = w_ref[...]
        conv = prev * w[0:1, :] + cg * w[1:2, :] + nxt * w[2:3, :]
        a_ref[...] = (gb_ref[...].astype(F32) * conv).astype(a_ref.dtype)

    def col(off):
        return pl.BlockSpec((s, LANES), lambda b, c: (b, off + c))

    return pl.pallas_call(
        body, name=tag + "_conv", grid=(nb, ncb),
        in_specs=[col(0), col(ncb), col(2 * ncb), pl.BlockSpec((3, LANES), lambda b, c: (0, c))],
        out_specs=col(0),
        out_shape=jax.ShapeDtypeStruct((t, CONV_WIDTH), _ACT),
        scratch_shapes=[pltpu.VMEM((s + 2 * HALO, LANES), F32)],
        compiler_params=_params("parallel", "parallel"),
    )(proj, proj, proj, conv_w)


def _conv_bwd(tag, proj, conv_w, d_mix, nb, s):
    t = proj.shape[0]
    ncb = CONV_WIDTH // LANES

    def body(gb_ref, gc_ref, hc_ref, w_ref, da_ref, dgb_ref, dgc_ref, dhc_ref, dw_ref, pad_ref):
        b = pl.program_id(1)
        _zero_halo(pad_ref, s)
        gb = gb_ref[...].astype(F32)
        gc = gc_ref[...].astype(F32)
        hc = hc_ref[...].astype(F32)
        w = w_ref[...]
        da = da_ref[...]
        cg = gc * hc
        prev, nxt = _shifted(pad_ref, cg, s)
        conv = prev * w[0:1, :] + cg * w[1:2, :] + nxt * w[2:3, :]
        dgb_ref[...] = (da * conv).astype(dgb_ref.dtype)
        dconv = da * gb
        dw = jnp.concatenate([
            jnp.sum(dconv * prev, axis=0, keepdims=True),
            jnp.sum(dconv * cg, axis=0, keepdims=True),
            jnp.sum(dconv * nxt, axis=0, keepdims=True)], axis=0)
        dprev, dnxt = _shifted(pad_ref, dconv, s)
        dcg = dnxt * w[0:1, :] + dconv * w[1:2, :] + dprev * w[2:3, :]
        dgc_ref[...] = (dcg * hc).astype(dgc_ref.dtype)
        dhc_ref[...] = (dcg * gc).astype(dhc_ref.dtype)

        @pl.when(b == 0)
        def _():
            dw_ref[...] = dw

        @pl.when(b > 0)
        def _():
            dw_ref[...] += dw

    def col(off):
        return pl.BlockSpec((s, LANES), lambda c, b: (b, off + c))

    wspec = pl.BlockSpec((3, LANES), lambda c, b: (0, c))
    act = jax.ShapeDtypeStruct((t, CONV_WIDTH), _ACT)
    return pl.pallas_call(
        body, name=tag + "_dconv", grid=(ncb, nb),
        in_specs=[col(0), col(ncb), col(2 * ncb), wspec, col(0)],
        out_specs=(col(0), col(0), col(0), wspec),
        out_shape=(act, act, act, jax.ShapeDtypeStruct((3, CONV_WIDTH), F32)),
        scratch_shapes=[pltpu.VMEM((s + 2 * HALO, LANES), F32)],
        compiler_params=_params("parallel", "arbitrary"),
    )(proj, proj, proj, conv_w, d_mix)


def _rope_tables(s):
    rows = s // GRID_W
    r_idx, c_idx = jnp.meshgrid(jnp.arange(rows), jnp.arange(GRID_W), indexing="ij")
    r_idx = r_idx.reshape(-1).astype(F32)
    c_idx = c_idx.reshape(-1).astype(F32)
    n_freq = HEAD_DIM // 4
    inv = ROPE_THETA ** (-jnp.arange(n_freq, dtype=F32) / n_freq)
    ang = jnp.concatenate([r_idx[:, None] * inv, c_idx[:, None] * inv], axis=-1)
    cos = jnp.repeat(jnp.cos(ang), 2, axis=1)
    sin = jnp.repeat(jnp.sin(ang), 2, axis=1)
    sign = jnp.where(jnp.arange(HEAD_DIM) % 2 == 0, -1.0, 1.0).astype(F32)
    return jnp.tile(cos, (1, LANES // HEAD_DIM)), jnp.tile(sin * sign, (1, LANES // HEAD_DIM))


def _head_ones():
    i = jnp.arange(LANES) // HEAD_DIM
    return (i[:, None] == i[None, :]).astype(jnp.bfloat16)


def _head_sum(v, ones):
    outs = []
    for j in range(v.shape[1] // LANES):
        c = v[:, j * LANES:(j + 1) * LANES]
        hi = c.astype(jnp.bfloat16)
        lo = (c - hi.astype(F32)).astype(jnp.bfloat16)
        outs.append(jnp.dot(hi, ones, preferred_element_type=F32) + jnp.dot(lo, ones, preferred_element_type=F32))
    return outs[0] if len(outs) == 1 else jnp.concatenate(outs, axis=1)


def _pair_swap(v):
    outs = []
    for j in range(v.shape[1] // LANES):
        c = v[:, j * LANES:(j + 1) * LANES]
        lane = lax.broadcasted_iota(jnp.int32, c.shape, 1)
        outs.append(jnp.where(lane % 2 == 0, pltpu.roll(c, LANES - 1, 1), pltpu.roll(c, 1, 1)))
    return outs[0] if len(outs) == 1 else jnp.concatenate(outs, axis=1)


def _wide(tab, width):
    return tab if width == LANES else jnp.concatenate([tab] * (width // LANES), axis=1)


_QK_SCALE = HEAD_DIM ** -0.5


def _qk_fwd(tag, proj, q_gain, k_gain, cos, sin, nb, s):
    t = proj.shape[0]
    tr = _tile(s, 512)
    ns = s // tr
    q_off = 3 * CONV_WIDTH // ATTN_WIDTH
    k_off = (3 * CONV_WIDTH + ATTN_WIDTH) // KV_WIDTH

    def body(q_ref, k_ref, qg_ref, kg_ref, cos_ref, sin_ref, ones_ref, qo_ref, ko_ref):
        ones = ones_ref[...]
        for src, g_ref, dst, mult in ((q_ref, qg_ref, qo_ref, _QK_SCALE), (k_ref, kg_ref, ko_ref, 1.0)):
            v = src[...].astype(F32)
            w = v.shape[1]
            r = lax.rsqrt(_head_sum(v * v, ones) * (1.0 / HEAD_DIM) + EPS)
            vn = (v * r) * g_ref[...]
            rot = vn * _wide(cos_ref[...], w) + _pair_swap(vn) * _wide(sin_ref[...], w)
            dst[...] = (rot * mult).astype(dst.dtype)

    tab = pl.BlockSpec((tr, LANES), lambda i: (i % ns, 0))
    return pl.pallas_call(
        body, name=tag + "_qk", grid=(t // tr,),
        in_specs=[pl.BlockSpec((tr, ATTN_WIDTH), lambda i: (i, q_off)),
                  pl.BlockSpec((tr, KV_WIDTH), lambda i: (i, k_off)),
                  pl.BlockSpec((1, ATTN_WIDTH), lambda i: (0, 0)),
                  pl.BlockSpec((1, KV_WIDTH), lambda i: (0, 0)),
                  tab, tab, pl.BlockSpec((LANES, LANES), lambda i: (0, 0))],
        out_specs=(pl.BlockSpec((tr, ATTN_WIDTH), lambda i: (i, 0)),
                   pl.BlockSpec((tr, KV_WIDTH), lambda i: (i, 0))),
        out_shape=(jax.ShapeDtypeStruct((t, ATTN_WIDTH), _ACT), jax.ShapeDtypeStruct((t, KV_WIDTH), _ACT)),
        compiler_params=_params("parallel"),
    )(proj, proj, jnp.tile(q_gain, N_Q_HEADS).reshape(1, ATTN_WIDTH),
      jnp.tile(k_gain, N_KV_HEADS).reshape(1, KV_WIDTH), cos, sin, _head_ones())


def _qk_bwd(tag, proj, q_gain, k_gain, cos, sin, dq_rot, dk_rot, nb, s):
    t = proj.shape[0]
    tr = _tile(s, 512)
    ns = s // tr
    q_off = 3 * CONV_WIDTH // ATTN_WIDTH
    k_off = (3 * CONV_WIDTH + ATTN_WIDTH) // KV_WIDTH

    def body(q_ref, k_ref, qg_ref, kg_ref, cos_ref, sin_ref, ones_ref, dqr_ref, dkr_ref,
             dq_ref, dk_ref, dqg_ref, dkg_ref):
        i = pl.program_id(0)
        ones = ones_ref[...]
        for src, g_ref, dr_ref, dst, dg_ref, mult in ((q_ref, qg_ref, dqr_ref, dq_ref, dqg_ref, _QK_SCALE),
                                                      (k_ref, kg_ref, dkr_ref, dk_ref, dkg_ref, 1.0)):
            v = src[...].astype(F32)
            w = v.shape[1]
            r = lax.rsqrt(_head_sum(v * v, ones) * (1.0 / HEAD_DIM) + EPS)
            xhat = v * r
            dr = dr_ref[...] * mult
            dvn = dr * _wide(cos_ref[...], w) + _pair_swap(dr * _wide(sin_ref[...], w))
            dgx = dvn * g_ref[...]
            m = _head_sum(dgx * xhat, ones) * (1.0 / HEAD_DIM)
            dst[...] = (r * (dgx - xhat * m)).astype(dst.dtype)
            part = jnp.sum(dvn * xhat, axis=0, keepdims=True)
            fold = part[:, 0:HEAD_DIM]
            for hh in range(1, w // HEAD_DIM):
                fold = fold + part[:, hh * HEAD_DIM:(hh + 1) * HEAD_DIM]

            @pl.when(i == 0)
            def _():
                dg_ref[...] = fold

            @pl.when(i > 0)
            def _():
                dg_ref[...] += fold

    tab = pl.BlockSpec((tr, LANES), lambda i: (i % ns, 0))
    qrow = pl.BlockSpec((tr, ATTN_WIDTH), lambda i: (i, 0))
    krow = pl.BlockSpec((tr, KV_WIDTH), lambda i: (i, 0))
    gvec = pl.BlockSpec((1, HEAD_DIM), lambda i: (0, 0))
    dq, dk, dqg, dkg = pl.pallas_call(
        body, name=tag + "_dqk", grid=(t // tr,),
        in_specs=[pl.BlockSpec((tr, ATTN_WIDTH), lambda i: (i, q_off)),
                  pl.BlockSpec((tr, KV_WIDTH), lambda i: (i, k_off)),
                  pl.BlockSpec((1, ATTN_WIDTH), lambda i: (0, 0)),
                  pl.BlockSpec((1, KV_WIDTH), lambda i: (0, 0)),
                  tab, tab, pl.BlockSpec((LANES, LANES), lambda i: (0, 0)), qrow, krow],
        out_specs=(qrow, krow, gvec, gvec),
        out_shape=(jax.ShapeDtypeStruct((t, ATTN_WIDTH), _ACT), jax.ShapeDtypeStruct((t, KV_WIDTH), _ACT),
                   jax.ShapeDtypeStruct((1, HEAD_DIM), F32), jax.ShapeDtypeStruct((1, HEAD_DIM), F32)),
        compiler_params=_params("arbitrary"),
    )(proj, proj, jnp.tile(q_gain, N_Q_HEADS).reshape(1, ATTN_WIDTH),
      jnp.tile(k_gain, N_KV_HEADS).reshape(1, KV_WIDTH), cos, sin, _head_ones(), dq_rot, dk_rot)
    return dq, dk, dqg.reshape(HEAD_DIM), dkg.reshape(HEAD_DIM)


def _head(v, h):
    return v[:, h * HEAD_DIM:(h + 1) * HEAD_DIM]


def _attn_fwd(tag, q, k, proj, nb, s):
    t = q.shape[0]
    tq = _tile(s, 256)
    nq = s // tq
    v_off = (3 * CONV_WIDTH + ATTN_WIDTH + KV_WIDTH) // KV_WIDTH

    def body(q_ref, k_ref, v_ref, o_ref, lse_ref):
        qv = q_ref[...]
        kv = k_ref[...]
        vv = v_ref[...]
        for h in range(N_Q_HEADS):
            j = h // Q_PER_KV
            sc = _dot(_head(qv, h), _head(kv, j), "nt")
            m = jnp.max(sc, axis=-1, keepdims=True)
            e = jnp.exp(sc - m)
            l = jnp.sum(e, axis=-1, keepdims=True)
            o = _dot(e, _head(vv, j)) * (1.0 / l)
            o_ref[:, h * HEAD_DIM:(h + 1) * HEAD_DIM] = o.astype(o_ref.dtype)
            lse_ref[:, h:h + 1] = m + jnp.log(l)

    return pl.pallas_call(
        body, name=tag + "_attn", grid=(nb, nq),
        in_specs=[pl.BlockSpec((tq, ATTN_WIDTH), lambda b, i: (b * nq + i, 0)),
                  pl.BlockSpec((s, KV_WIDTH), lambda b, i: (b, 0)),
                  pl.BlockSpec((s, KV_WIDTH), lambda b, i: (b, v_off))],
        out_specs=(pl.BlockSpec((tq, ATTN_WIDTH), lambda b, i: (b * nq + i, 0)),
                   pl.BlockSpec((tq, N_Q_HEADS), lambda b, i: (b * nq + i, 0))),
        out_shape=(jax.ShapeDtypeStruct((t, ATTN_WIDTH), _ACT), jax.ShapeDtypeStruct((t, N_Q_HEADS), F32)),
        compiler_params=_params("parallel", "parallel"),
    )(q, k, proj)


def _attn_bwd(tag, q, k, proj, o, lse, d_mix, nb, s):
    t = q.shape[0]
    tq = _tile(s, 256)
    nq = s // tq
    v_off = (3 * CONV_WIDTH + ATTN_WIDTH + KV_WIDTH) // KV_WIDTH

    def body(q_ref, k_ref, v_ref, o_ref, lse_ref, do_ref, dq_ref, dk_ref, dv_ref):
        i = pl.program_id(1)

        @pl.when(i == 0)
        def _():
            dk_ref[...] = jnp.zeros_like(dk_ref)
            dv_ref[...] = jnp.zeros_like(dv_ref)

        qv = q_ref[...]
        kv = k_ref[...]
        vv = v_ref[...]
        ov = o_ref[...].astype(F32)
        dov = do_ref[...]
        lse = lse_ref[...]
        for h in range(N_Q_HEADS):
            j = h // Q_PER_KV
            cols = slice(j * HEAD_DIM, (j + 1) * HEAD_DIM)
            qh = _head(qv, h)
            kj = _head(kv, j)
            doh = _head(dov, h)
            sc = _dot(qh, kj, "nt")
            p = jnp.exp(sc - lse[:, h:h + 1])
            dp = _dot(doh, _head(vv, j), "nt")
            delta = jnp.sum(doh * _head(ov, h), axis=-1, keepdims=True)
            ds = p * (dp - delta)
            dv_ref[:, cols] += _dot(p, doh, "tn")
            dk_ref[:, cols] += _dot(ds, qh, "tn")
            dq_ref[:, h * HEAD_DIM:(h + 1) * HEAD_DIM] = _dot(ds, kj)

    qrow = pl.BlockSpec((tq, ATTN_WIDTH), lambda b, i: (b * nq + i, 0))
    kvrow = pl.BlockSpec((s, KV_WIDTH), lambda b, i: (b, 0))
    return pl.pallas_call(
        body, name=tag + "_dattn", grid=(nb, nq),
        in_specs=[qrow, kvrow, pl.BlockSpec((s, KV_WIDTH), lambda b, i: (b, v_off)), qrow,
                  pl.BlockSpec((tq, N_Q_HEADS), lambda b, i: (b * nq + i, 0)),
                  pl.BlockSpec((tq, ATTN_WIDTH), lambda b, i: (b * nq + i, 1))],
        out_specs=(qrow, kvrow, kvrow),
        out_shape=(jax.ShapeDtypeStruct((t, ATTN_WIDTH), F32), jax.ShapeDtypeStruct((t, KV_WIDTH), F32),
                   jax.ShapeDtypeStruct((t, KV_WIDTH), F32)),
        compiler_params=_params("parallel", "arbitrary"),
    )(q, k, proj, o, lse, d_mix)


def _even_fwd(tag, x, p, cos, sin, nb, s):
    h = _rmsnorm_fwd(tag + "_norm", x, p["norm"])
    proj = _proj_in(tag, h, p["w_in"])
    a = _conv_fwd(tag, proj, p["conv_w"], nb, s)
    q, k = _qk_fwd(tag, proj, p["q_gain"], p["k_gain"], cos, sin, nb, s)
    o, lse = _attn_fwd(tag, q, k, proj, nb, s)
    x_out = _proj_out(tag, x, [a, o], p["w_out"])
    return x_out, (x, h, proj, a, q, k, o, lse)


def _even_bwd(tag, dxo, saved, p, cos, sin, nb, s, mid, ready):
    x, h, proj, a, q, k, o, lse = saved
    d_mix, dw_out = _proj_out_bwd(tag, dxo, [a, o], p["w_out"])
    dgb, dgc, dhc, dconv_w = _conv_bwd(tag, proj, p["conv_w"] + mid(d_mix), d_mix, nb, s)
    dq_rot, dk_rot, dv = _attn_bwd(tag, q, k, proj, o, lse, d_mix, nb, s)
    dq, dk, dq_gain, dk_gain = _qk_bwd(tag, proj, p["q_gain"], p["k_gain"], cos, sin, dq_rot, dk_rot, nb, s)
    dx, dnorm = _proj_in_bwd(tag, h, [dgb, dgc, dhc, dq, dk, dv], p["w_in"], x, dxo, p["norm"],
                             lambda dw_in: ready(dict(w_in=dw_in, w_out=dw_out), dw_in))
    return dx, dict(norm=dnorm, conv_w=dconv_w, q_gain=dq_gain, k_gain=dk_gain)


def _window(pad_ref, val, r, s):
    pad_ref[pl.ds(HALO, s), :] = val
    acc = val
    for d in range(1, r + 1):
        acc = acc + pad_ref[pl.ds(HALO - d, s), :] + pad_ref[pl.ds(HALO + d, s), :]
    return acc


def _count(r, s):
    t = lax.broadcasted_iota(jnp.int32, (s, 1), 0)
    return (jnp.minimum(t + r, s - 1) - jnp.maximum(t - r, 0) + 1).astype(F32)


def _sgu_chunk(u_ref, v_ref, norm, ws_ref, bt, rows):
    uu = u_ref[rows, :].astype(F32)
    vv = v_ref[rows, :].astype(F32)
    gu = _gelu(uu)
    gv = _gelu(vv)
    r = lax.rsqrt(jnp.mean(gv * gv, axis=-1, keepdims=True) + EPS)
    xhat = gv * r
    vn = xhat * norm
    mixed = []
    for g in range(N_GROUPS):
        cols = slice(g * SGU_GROUP, (g + 1) * SGU_GROUP)
        mixed.append(_dot(ws_ref[g], vn[:, cols]) + bt[:, g:g + 1])
    return uu, vv, gu, r, xhat, vn, mixed


def _odd_core_fwd(tag, proj, p, nb, s):
    t = proj.shape[0]
    nchunk = s // SGU_CHUNK

    def body(p_ref, u_ref, v_ref, pw_ref, ps_ref, sn_ref, ws_ref, bt_ref, mix_ref, pad_ref):
        _zero_halo(pad_ref, s)
        for g, r in enumerate(POOL_RADII):
            cols = slice(g * POOL_GROUP, (g + 1) * POOL_GROUP)
            pg = p_ref[:, cols].astype(F32)
            pooled = _window(pad_ref, pg, r, s) / _count(r, s) - pg
            mix_ref[:, cols] = (_dot(pooled, pw_ref[g]) * ps_ref[:, cols]).astype(mix_ref.dtype)
        norm = sn_ref[...]
        bt = bt_ref[...]

        def chunk(n, carry):
            rows = pl.ds(pl.multiple_of(n * SGU_CHUNK, SGU_CHUNK), SGU_CHUNK)
            _, _, gu, _, _, _, mixed = _sgu_chunk(u_ref, v_ref, norm, ws_ref, bt, rows)
            for g in range(N_GROUPS):
                cols = slice(g * SGU_GROUP, (g + 1) * SGU_GROUP)
                mix_ref[rows, HALF + g * SGU_GROUP:HALF + (g + 1) * SGU_GROUP] = (
                    gu[:, cols] * mixed[g]).astype(mix_ref.dtype)
            return carry

        lax.fori_loop(0, nchunk, chunk, 0)

    def col(j):
        return pl.BlockSpec((s, HALF), lambda b: (b, j))

    def whole(a):
        return pl.BlockSpec(a.shape, lambda b: (0,) * a.ndim)

    consts = [p["pool_w"], p["pool_scale"].reshape(1, HALF), p["sgu_norm"].reshape(1, HALF),
              p["sgu_w"], p["sgu_b"].T]
    return pl.pallas_call(
        body, name=tag + "_core", grid=(nb,),
        in_specs=[col(0), col(1), col(2)] + [whole(a) for a in consts],
        out_specs=pl.BlockSpec((s, D_MODEL), lambda b: (b, 0)),
        out_shape=jax.ShapeDtypeStruct((t, D_MODEL), _ACT),
        scratch_shapes=[pltpu.VMEM((s + 2 * HALO, POOL_GROUP), F32)],
        compiler_params=_params("parallel"),
    )(proj, proj, proj, *consts)


def _odd_core_bwd(tag, proj, p, d_mix, nb, s):
    t = proj.shape[0]
    nchunk = s // SGU_CHUNK

    def body(p_ref, u_ref, v_ref, pw_ref, ps_ref, sn_ref, ws_ref, bt_ref, dm_ref,
             dproj_ref, dpw_ref, dps_ref, dsn_ref, dws_ref, dbt_ref, pad_ref):
        b = pl.program_id(0)

        @pl.when(b == 0)
        def _():
            dpw_ref[...] = jnp.zeros_like(dpw_ref)
            dps_ref[...] = jnp.zeros_like(dps_ref)
            dsn_ref[...] = jnp.zeros_like(dsn_ref)
            dws_ref[...] = jnp.zeros_like(dws_ref)
            dbt_ref[...] = jnp.zeros_like(dbt_ref)

        _zero_halo(pad_ref, s)
        for g, r in enumerate(POOL_RADII):
            cols = slice(g * POOL_GROUP, (g + 1) * POOL_GROUP)
            pg = p_ref[:, cols].astype(F32)
            cnt = _count(r, s)
            pooled = _window(pad_ref, pg, r, s) / cnt - pg
            c_pre = _dot(pooled, pw_ref[g])
            dc = dm_ref[:, cols]
            dps_ref[:, cols] += jnp.sum(dc * c_pre, axis=0, keepdims=True)
            dcp = dc * ps_ref[:, cols]
            dpw_ref[g] += _dot(pooled, dcp, "tn")
            dpooled = _dot(dcp, pw_ref[g], "nt")
            dproj_ref[:, cols] = (_window(pad_ref, dpooled / cnt, r, s) - dpooled).astype(dproj_ref.dtype)
        norm = sn_ref[...]
        bt = bt_ref[...]

        def chunk(n, carry):
            rows = pl.ds(pl.multiple_of(n * SGU_CHUNK, SGU_CHUNK), SGU_CHUNK)
            uu, vv, gu, r, xhat, vn, mixed = _sgu_chunk(u_ref, v_ref, norm, ws_ref, bt, rows)
            dd = dm_ref[rows, HALF:D_MODEL]
            dgu, dvn = [], []
            for g in range(N_GROUPS):
                cols = slice(g * SGU_GROUP, (g + 1) * SGU_GROUP)
                dgu.append(dd[:, cols] * mixed[g])
                dmx = dd[:, cols] * gu[:, cols]
                dbt_ref[:, g:g + 1] += jnp.sum(dmx, axis=-1, keepdims=True)
                dws_ref[g] += _dot(dmx, vn[:, cols], "nt")
                dvn.append(_dot(ws_ref[g], dmx, "tn"))
            dgu = jnp.concatenate(dgu, axis=1)
            dvn = jnp.concatenate(dvn, axis=1)
            dsn_ref[...] += jnp.sum(dvn * xhat, axis=0, keepdims=True)
            dgx = dvn * norm
            m = jnp.mean(dgx * xhat, axis=-1, keepdims=True)
            dgv = r * (dgx - xhat * m)
            dproj_ref[rows, HALF:2 * HALF] = (dgu * _gelu_grad(uu)).astype(dproj_ref.dtype)
            dproj_ref[rows, 2 * HALF:3 * HALF] = (dgv * _gelu_grad(vv)).astype(dproj_ref.dtype)
            return carry

        lax.fori_loop(0, nchunk, chunk, 0)

    def col(j):
        return pl.BlockSpec((s, HALF), lambda b: (b, j))

    def whole(a):
        return pl.BlockSpec(a.shape, lambda b: (0,) * a.ndim)

    consts = [p["pool_w"], p["pool_scale"].reshape(1, HALF), p["sgu_norm"].reshape(1, HALF),
              p["sgu_w"], p["sgu_b"].T]
    gshapes = [jax.ShapeDtypeStruct(a.shape, F32) for a in consts]
    dproj, dpw, dps, dsn, dws, dbt = pl.pallas_call(
        body, name=tag + "_dcore", grid=(nb,),
        in_specs=[col(0), col(1), col(2)] + [whole(a) for a in consts]
        + [pl.BlockSpec((s, D_MODEL), lambda b: (b, 0))],
        out_specs=[pl.BlockSpec((s, 3 * HALF), lambda b: (b, 0))] + [whole(a) for a in consts],
        out_shape=[jax.ShapeDtypeStruct((t, 3 * HALF), _ACT)] + gshapes,
        scratch_shapes=[pltpu.VMEM((s + 2 * HALO, POOL_GROUP), F32)],
        compiler_params=_params("arbitrary"),
    )(proj, proj, proj, *consts, d_mix)
    return dproj, dict(pool_w=dpw, pool_scale=dps.reshape(HALF), sgu_norm=dsn.reshape(HALF), sgu_w=dws, sgu_b=dbt.T)


def _odd_fwd(tag, x, p, nb, s):
    h = _rmsnorm_fwd(tag + "_norm", x, p["norm"])
    proj = _proj_in(tag, h, p["w_in"])
    mix = _odd_core_fwd(tag, proj, p, nb, s)
    x_out = _proj_out(tag, x, [mix], p["w_out"])
    return x_out, (x, h, proj, mix)


def _odd_bwd(tag, dxo, saved, p, nb, s, mid, ready):
    x, h, proj, mix = saved
    d_mix, dw_out = _proj_out_bwd(tag, dxo, [mix], p["w_out"])
    p = dict(p, pool_scale=p["pool_scale"] + mid(d_mix))
    dproj, grads = _odd_core_bwd(tag, proj, p, d_mix, nb, s)
    dx, dnorm = _proj_in_bwd(tag, h, [dproj], p["w_in"], x, dxo, p["norm"],
                             lambda dw_in: ready(dict(w_in=dw_in, w_out=dw_out), dw_in))
    grads.update(norm=dnorm)
    return dx, grads


def _local_step(x3, target3, depth, weights_of, final_norm, mid, grads_ready, small_done):
    nb, s, d = x3.shape
    t = nb * s
    x = x3.reshape(t, d)
    target = target3.reshape(t, d)
    cos, sin = _rope_tables(s)
    saved, ws = [], []
    for l in range(depth):
        w1 = weights_of(l, "ffn1", x)
        x, s1 = _ffn_fwd(f"l{l}_ffn1", x, w1["norm"], w1["w_in4"], w1["w_out"], _ffn_tiles(l, 1))
        wm = weights_of(l, "mix", x)
        if l % 2 == 0:
            x, s2 = _even_fwd(f"l{l}_ev", x, wm, cos, sin, nb, s)
        else:
            x, s2 = _odd_fwd(f"l{l}_od", x, wm, nb, s)
        w2 = weights_of(l, "ffn2", x)
        x, s3 = _ffn_fwd(f"l{l}_ffn2", x, w2["norm"], w2["w_in4"], w2["w_out"], _ffn_tiles(l, 2))
        saved.append((s1, s2, s3))
        ws.append((w1, wm, w2))
    loss, dx, dfinal = _final_loss("final_loss", x, final_norm, target)
    for l in reversed(range(depth)):
        s1, s2, s3 = saved[l]
        w1, wm, w2 = ws[l]

        def ready(block):
            return lambda grads, a: grads_ready(l, block, grads, a)

        dx, dn = _ffn_bwd(f"l{l}_ffn2", dx, s3, w2["norm"], w2["w_in4"], w2["w_out"], _ffn_tiles(l, 2), mid, ready("ffn2"))
        small_done(l, "ffn2", dict(norm=dn))
        if l % 2 == 0:
            dx, gm = _even_bwd(f"l{l}_ev", dx, s2, wm, cos, sin, nb, s, mid, ready("mix"))
        else:
            dx, gm = _odd_bwd(f"l{l}_od", dx, s2, wm, nb, s, mid, ready("mix"))
        small_done(l, "mix", gm)
        dx, dn = _ffn_bwd(f"l{l}_ffn1", dx, s1, w1["norm"], w1["w_in4"], w1["w_out"], _ffn_tiles(l, 1), mid, ready("ffn1"))
        small_done(l, "ffn1", dict(norm=dn))
    return loss, dx.reshape(nb, s, d), dfinal


_HBM = pl.BlockSpec(memory_space=pltpu.HBM)


def _place():
    x, y, c = lax.axis_index("x"), lax.axis_index("y"), lax.axis_index("c")
    chips = [(1 - x, y), (x, 1 - y), (1 - x, 1 - y)]
    return x, y, c, chips


def _remote(src, dst, send_sem, recv_sem, to):
    return pltpu.make_async_remote_copy(src_ref=src, dst_ref=dst, send_sem=send_sem, recv_sem=recv_sem,
                                        device_id=to, device_id_type=_MESH)


def _gather_shards(arrs, small):
    n = len(arrs)
    own = 6

    def body(*refs):
        ins, sm_in = refs[:n], refs[n]
        outs, sm_out = refs[n + 1:2 * n + 1], refs[2 * n + 1]
        send, recv = refs[2 * n + 2:]
        x, y, c, chips = _place()
        k = 2 * x + y
        sib = (x, y, 1 - c)
        started = []
        for a in range(n + 1):
            src, dst = (ins[a], outs[a]) if a < n else (sm_in, sm_out)
            cp = _remote(src, dst.at[k], send.at[a, own], recv.at[a, own], sib)
            cp.start()
            started.append(cp)
            if a < n:
                h = src.shape[0] // 2
                mine = pl.ds(c * h, h)
                src_part, dst_part = src.at[mine], dst.at[k, mine]
            else:
                src_part, dst_part = src, dst.at[k]
            for j, chip in enumerate(chips):
                cp = _remote(src_part, dst_part, send.at[a, j], recv.at[a, j], (*chip, c))
                cp.start()
                started.append(cp)
        for a in range(n):
            h = ins[a].shape[0] // 2
            mine = pl.ds(c * h, h)
            for j, (px, py) in enumerate(chips):
                landed = outs[a].at[2 * px + py, mine]
                _remote(landed, landed, send.at[a, j], recv.at[a, j], (px, py, c)).wait_recv()
                cp = _remote(landed, landed, send.at[a, 3 + j], recv.at[a, 3 + j], sib)
                cp.start()
                started.append(cp)
        for a in range(n):
            h = ins[a].shape[0] // 2
            other = pl.ds((1 - c) * h, h)
            for j, (px, py) in enumerate(chips):
                passed = outs[a].at[2 * px + py, other]
                _remote(passed, passed, send.at[a, 3 + j], recv.at[a, 3 + j], sib).wait_recv()
        for j, (px, py) in enumerate(chips):
            landed = sm_out.at[2 * px + py]
            _remote(landed, landed, send.at[n, j], recv.at[n, j], (px, py, c)).wait_recv()
        for a in range(n + 1):
            filled = (outs[a] if a < n else sm_out).at[k]
            _remote(filled, filled, send.at[a, own], recv.at[a, own], sib).wait_recv()
        for cp in started:
            cp.wait_send()

    outs = pl.pallas_call(
        body, name="gather_shards",
        in_specs=[_HBM] * (n + 1), out_specs=[_HBM] * (n + 1),
        out_shape=[jax.ShapeDtypeStruct((N_CHIPS,) + a.shape, a.dtype) for a in list(arrs) + [small]],
        scratch_shapes=[pltpu.SemaphoreType.DMA((n + 1, 7)), pltpu.SemaphoreType.DMA((n + 1, 7))],
    )(*arrs, small)
    return outs[:n], outs[n]


_SEM = pl.BlockSpec(memory_space=pltpu.SEMAPHORE)
_EFFECT = pltpu.SideEffectType.DATAFLOW_SIDE_EFFECTING


def _gather_plan(i, src, land, k, c, chips, sib):
    mine = pl.ds(c * (src.shape[0] // 2), src.shape[0] // 2)
    plan = [(src.at[mine], land.at[k, mine], (px, py, c), land.at[2 * px + py, mine]) for px, py in chips]
    return plan + [(src, land.at[k], sib, land.at[k])]


def _pass_plan(i, src, land, k, c, chips, sib):
    h = src.shape[1] // 2
    mine, theirs = pl.ds(c * h, h), pl.ds((1 - c) * h, h)
    return [(src.at[2 * px + py, mine], land.at[2 * px + py, mine], sib, land.at[2 * px + py, theirs]) for px, py in chips]


def _swap_plan(i, src, land, k, c, chips, sib):
    h = src.shape[1] // 2
    return [(src.at[:, pl.ds((1 - c) * h, h)], land, sib, land)]


def _scatter_plan(i, src, land, k, c, chips, sib):
    return [(src.at[2 * px + py], land.at[k], (px, py, c), land.at[2 * px + py]) for px, py in chips]


def _join_plan(layers):
    def plan(i, src, land, k, c, chips, sib):
        h = src.shape[1] // 2
        mine, theirs = pl.ds(c * h, h), pl.ds((1 - c) * h, h)
        return [(src.at[layers[i], mine], land.at[layers[i], mine], sib, land.at[layers[i], theirs])]
    return plan


def _split_start(name, plan, ncopy, srcs, after, land_shapes=None):
    n = len(srcs)
    if land_shapes is None:
        land_shapes = [(N_CHIPS,) + a.shape[-2:] for a in srcs]
    in_place = land_shapes == "self"
    lands = [] if in_place else [pltpu.with_memory_space_constraint(lax.empty(shape, a.dtype), pltpu.HBM)
                                 for shape, a in zip(land_shapes, srcs)]
    nbuf = n + len(lands)

    def body(*refs):
        src_refs = refs[1:1 + n]
        land_refs = src_refs if in_place else refs[1 + n:1 + nbuf]
        send, recv, token = refs[1 + nbuf], refs[2 + nbuf], refs[-1]
        x, y, c, chips = _place()
        for i in range(n):
            for j, (src, dst, peer, _) in enumerate(plan(i, src_refs[i], land_refs[i], 2 * x + y, c, chips, (x, y, 1 - c))):
                _remote(src, dst, send.at[i * ncopy + j], recv.at[i * ncopy + j], peer).start()
        token[...] = jnp.zeros_like(token)

    outs = pl.pallas_call(
        body, name=name,
        in_specs=[_ANY] + [_HBM] * nbuf,
        out_specs=[_SEM, _SEM] + [_HBM] * nbuf + [_VMEM],
        out_shape=[pltpu.SemaphoreType.DMA((n * ncopy,)), pltpu.SemaphoreType.DMA((n * ncopy,))]
        + [pltpu.HBM(a.shape, a.dtype) for a in list(srcs) + lands] + [jax.ShapeDtypeStruct((8, LANES), F32)],
        input_output_aliases={1 + i: 2 + i for i in range(nbuf)},
        compiler_params=pltpu.CompilerParams(has_side_effects=_EFFECT),
    )(after, *[pltpu.with_memory_space_constraint(a, pltpu.HBM) for a in srcs], *lands)
    return outs[0], outs[1], outs[2:2 + n], None if in_place else outs[2 + n:2 + nbuf], outs[-1]


def _split_wait(name, plan, started, after):
    send, recv, srcs, lands = started
    n = len(srcs)
    ncopy = send.shape[0] // n
    in_place = lands is None
    bufs = list(srcs) + ([] if in_place else list(lands))
    nbuf = len(bufs)

    def body(*refs):
        src_refs = refs[:n]
        land_refs = src_refs if in_place else refs[n:nbuf]
        send, recv = refs[nbuf], refs[nbuf + 1]
        x, y, c, chips = _place()
        for i in range(n):
            for j, (src, _, peer, landed) in enumerate(plan(i, src_refs[i], land_refs[i], 2 * x + y, c, chips, (x, y, 1 - c))):
                cp = _remote(src, landed, send.at[i * ncopy + j], recv.at[i * ncopy + j], peer)
                cp.wait_send()
                cp.wait_recv()

    outs = pl.pallas_call(
        body, name=name,
        in_specs=[_HBM] * nbuf + [_SEM, _SEM, _ANY],
        out_specs=[_HBM] * nbuf,
        out_shape=[pltpu.HBM(a.shape, a.dtype) for a in bufs],
        input_output_aliases={i: i for i in range(nbuf)},
        compiler_params=pltpu.CompilerParams(has_side_effects=_EFFECT),
    )(*bufs, send, recv, after)
    return outs[:n], outs[:n] if in_place else outs[n:]


def _allreduce_small(buf, after):
    rows = buf.shape[0]
    piece = rows // N_DEV

    def body(in_ref, after_ref, out_ref, land_ref, send, recv):
        x, y, c, _ = _place()
        me = 4 * x + 2 * y + c
        peers = [(1 - x if r & 4 else x, 1 - y if r & 2 else y, 1 - c if r & 1 else c) for r in range(1, N_DEV)]

        def rows_of(dev):
            return pl.ds(pl.multiple_of(dev * piece, 8), piece)

        first, second = [], []
        for r, (px, py, pc) in enumerate(peers):
            cp = _remote(in_ref.at[rows_of(4 * px + 2 * py + pc)], land_ref.at[me], send.at[0, r], recv.at[0, r], (px, py, pc))
            cp.start()
            first.append(cp)
        land_ref[me] = in_ref[rows_of(me), :]
        for r, (px, py, pc) in enumerate(peers):
            landed = land_ref.at[4 * px + 2 * py + pc]
            _remote(landed, landed, send.at[0, r], recv.at[0, r], (px, py, pc)).wait_recv()
        acc = land_ref[0]
        for d in range(1, N_DEV):
            acc = acc + land_ref[d]
        out_ref[rows_of(me), :] = acc
        for r, peer in enumerate(peers):
            cp = _remote(out_ref.at[rows_of(me)], out_ref.at[rows_of(me)], send.at[1, r], recv.at[1, r], peer)
            cp.start()
            second.append(cp)
        for r, (px, py, pc) in enumerate(peers):
            landed = out_ref.at[rows_of(4 * px + 2 * py + pc)]
            _remote(landed, landed, send.at[1, r], recv.at[1, r], (px, py, pc)).wait_recv()
        for cp in first + second:
            cp.wait_send()

    return pl.pallas_call(
        body, name="allreduce_small",
        in_specs=[_VMEM, _ANY], out_specs=_VMEM,
        out_shape=jax.ShapeDtypeStruct(buf.shape, F32),
        scratch_shapes=[pltpu.VMEM((N_DEV, piece, LANES), F32), pltpu.SemaphoreType.DMA((2, N_DEV - 1)),
                        pltpu.SemaphoreType.DMA((2, N_DEV - 1))],
        compiler_params=pltpu.CompilerParams(vmem_limit_bytes=_VMEM_LIMIT),
    )(buf, after)


def _div_tile(n, cap, mult):
    best = None
    for d in range(mult, min(n, cap) + 1, mult):
        if n % d == 0:
            best = d
    return best if best is not None else n


def _add_sibling(name, grad, got, c):
    nk, hr, cc = got.shape
    tr = _div_tile(hr, 512, 16)
    nt = hr // tr

    def body(c_ref, g_ref, o_ref, s_ref):
        s_ref[...] = (g_ref[...].astype(F32) + o_ref[...].astype(F32)).astype(s_ref.dtype)

    blk = (None, tr, cc)
    return pl.pallas_call(
        body, name=name,
        grid_spec=pltpu.PrefetchScalarGridSpec(
            num_scalar_prefetch=1, grid=(nk, nt),
            in_specs=[pl.BlockSpec(blk, lambda i, q, c_ref: (i, c_ref[0] * nt + q, 0)),
                      pl.BlockSpec(blk, lambda i, q, c_ref: (i, q, 0))],
            out_specs=pl.BlockSpec(blk, lambda i, q, c_ref: (i, q, 0))),
        out_shape=jax.ShapeDtypeStruct(got.shape, got.dtype),
        compiler_params=_params("parallel", "parallel"),
    )(c, grad, got)


def _add_chips(name, mine, got, place, buf, l):
    nk, hr, cc = got.shape
    tr = _div_tile(hr, 512, 16)
    nt = hr // tr

    def body(*refs):
        acc = refs[1][...].astype(F32)
        for q in range(1, nk):
            acc = acc + refs[1 + q][...].astype(F32)
        refs[2 + nk][...] = acc

    def part(q):
        return pl.BlockSpec((None, tr, cc), lambda i, p_ref: ((p_ref[0] + q) % nk, i, 0))

    return pl.pallas_call(
        body, name=name,
        grid_spec=pltpu.PrefetchScalarGridSpec(
            num_scalar_prefetch=1, grid=(nt,),
            in_specs=[part(q) for q in range(nk)] + [_ANY],
            out_specs=pl.BlockSpec((None, tr, cc), lambda i, p_ref: (l, p_ref[1] * nt + i, 0))),
        out_shape=jax.ShapeDtypeStruct(buf.shape, F32),
        input_output_aliases={1 + nk: 0},
        compiler_params=_params("parallel"),
    )(place, mine, *([got] * (nk - 1)), buf)


def _adamw(name, w, g, m, v, after=None):
    shape = w.shape
    cols = shape[-1]
    rows = w.size // cols
    tr = rows if rows * cols <= 2 ** 18 else _div_tile(rows, max(8, 2 ** 18 // cols), 8)
    c1 = 1.0 - ADAM_B1 ** ADAM_STEP
    c2 = 1.0 - ADAM_B2 ** ADAM_STEP
    extra = [] if after is None else [after]

    def body(*refs):
        w_ref, g_ref, m_ref, v_ref = refs[:4]
        d_ref, mo_ref, vo_ref, go_ref = refs[4 + len(extra):]
        gg = g_ref[...]
        mn = ADAM_B1 * m_ref[...] + (1.0 - ADAM_B1) * gg
        vn = ADAM_B2 * v_ref[...] + (1.0 - ADAM_B2) * (gg * gg)
        d_ref[...] = -ADAM_LR * ((mn / c1) / (jnp.sqrt(vn / c2) + ADAM_EPS) + ADAM_WD * w_ref[...])
        mo_ref[...] = mn
        vo_ref[...] = vn
        go_ref[...] = gg

    blk = pl.BlockSpec((tr, cols), lambda i: (i, 0))
    sds = jax.ShapeDtypeStruct((rows, cols), F32)
    outs = pl.pallas_call(
        body, name=name, grid=(rows // tr,),
        in_specs=[blk] * 4 + [_ANY] * len(extra), out_specs=(blk,) * 4, out_shape=(sds,) * 4,
        compiler_params=_params("parallel"),
    )(*[a.reshape(rows, cols) for a in (w, g, m, v)], *extra)
    return [o.reshape(shape) for o in outs]


_WEIGHTS = ["ffn1_norm", "ffn1_w_in", "ffn1_w_out", "mix_norm", "ffn2_norm", "ffn2_w_in", "ffn2_w_out",
            "ev_w_in", "ev_conv_w", "ev_q_norm", "ev_k_norm", "ev_w_out", "od_w_in", "od_pool_w",
            "od_pool_scale", "od_sgu_norm", "od_sgu_w", "od_sgu_b", "od_w_out", "final_norm"]
_BIG = ["ffn1_w_in", "ffn1_w_out", "ffn2_w_in", "ffn2_w_out", "ev_w_in", "ev_w_out", "od_w_in", "od_w_out"]
_SMALL_SHARDED = ["ev_conv_w", "od_pool_scale", "od_sgu_norm"]


def _pad_rows(a, mult=8):
    pad = (-a.shape[0]) % mult
    return a if pad == 0 else jnp.concatenate([a, jnp.zeros((pad,) + a.shape[1:], a.dtype)], axis=0)


def _join_cols(g):
    return g.transpose(1, 0, 2).reshape(g.shape[1], N_CHIPS * g.shape[2])


def _split_cols(w):
    return w.reshape(w.shape[0], N_CHIPS, w.shape[1] // N_CHIPS).transpose(1, 0, 2)


def kernel(x, ffn1_norm, ffn1_w_in, ffn1_w_out, mix_norm, ffn2_norm, ffn2_w_in, ffn2_w_out, ev_w_in, ev_conv_w,
           ev_q_norm, ev_k_norm, ev_w_out, od_w_in, od_pool_w, od_pool_scale, od_sgu_norm, od_sgu_w, od_sgu_b,
           od_w_out, final_norm, loss_target, m_ffn1_norm, m_ffn1_w_in, m_ffn1_w_out, m_mix_norm, m_ffn2_norm,
           m_ffn2_w_in, m_ffn2_w_out, m_ev_w_in, m_ev_conv_w, m_ev_q_norm, m_ev_k_norm, m_ev_w_out, m_od_w_in,
           m_od_pool_w, m_od_pool_scale, m_od_sgu_norm, m_od_sgu_w, m_od_sgu_b, m_od_w_out, m_final_norm, v_ffn1_norm,
           v_ffn1_w_in, v_ffn1_w_out, v_mix_norm, v_ffn2_norm, v_ffn2_w_in, v_ffn2_w_out, v_ev_w_in, v_ev_conv_w,
           v_ev_q_norm, v_ev_k_norm, v_ev_w_out, v_od_w_in, v_od_pool_w, v_od_pool_scale, v_od_sgu_norm, v_od_sgu_w,
           v_od_sgu_b, v_od_w_out, v_final_norm):
    return _step(x, ffn1_norm, ffn1_w_in, ffn1_w_out, mix_norm, ffn2_norm, ffn2_w_in, ffn2_w_out, ev_w_in, ev_conv_w,
                 ev_q_norm, ev_k_norm, ev_w_out, od_w_in, od_pool_w, od_pool_scale, od_sgu_norm, od_sgu_w, od_sgu_b,
                 od_w_out, final_norm, loss_target, m_ffn1_norm, m_ffn1_w_in, m_ffn1_w_out, m_mix_norm, m_ffn2_norm,
                 m_ffn2_w_in, m_ffn2_w_out, m_ev_w_in, m_ev_conv_w, m_ev_q_norm, m_ev_k_norm, m_ev_w_out, m_od_w_in,
                 m_od_pool_w, m_od_pool_scale, m_od_sgu_norm, m_od_sgu_w, m_od_sgu_b, m_od_w_out, m_final_norm,
                 v_ffn1_norm, v_ffn1_w_in, v_ffn1_w_out, v_mix_norm, v_ffn2_norm, v_ffn2_w_in, v_ffn2_w_out,
                 v_ev_w_in, v_ev_conv_w, v_ev_q_norm, v_ev_k_norm, v_ev_w_out, v_od_w_in, v_od_pool_w,
                 v_od_pool_scale, v_od_sgu_norm, v_od_sgu_w, v_od_sgu_b, v_od_w_out, v_final_norm)


def _step(*args):
    nw = len(_WEIGHTS)
    x = args[0]
    w = dict(zip(_WEIGHTS, args[1:1 + nw]))
    target = args[1 + nw]
    m = dict(zip(_WEIGHTS, args[2 + nw:2 + 2 * nw]))
    v = dict(zip(_WEIGHTS, args[2 + 2 * nw:2 + 3 * nw]))
    depth = w["ffn1_norm"].shape[0]
    n_even, n_odd = w["ev_w_in"].shape[0], w["od_w_in"].shape[0]
    chip = 2 * lax.axis_index("x") + lax.axis_index("y")
    place = jnp.stack([chip, lax.axis_index("c")]).astype(jnp.int32)
    core = place[1:2]

    def sharded(l, block):
        if block == "mix":
            block = "ev" if l % 2 == 0 else "od"
            return [(block + "_w_in", l // 2), (block + "_w_out", l // 2)]
        return [(block + "_w_in", l), (block + "_w_out", l)]

    def shards(group, zero):
        return [(w[n][i] + zero).astype(_ACT) for l, block in group for n, i in sharded(l, block)]

    later = ([[(0, "ffn1")], [(0, "mix"), (0, "ffn2")]]
             + [[(l, "ffn1"), (l, "mix"), (l, "ffn2")] for l in range(1, depth)])
    small_rows = [w["ev_conv_w"].reshape(3 * n_even, LANES), w["od_pool_scale"], w["od_sgu_norm"]]
    _, small = _gather_shards([], _pad_rows(jnp.concatenate(small_rows, axis=0)))
    conv_w = small[:, :3 * n_even].reshape(N_CHIPS, n_even, 3, LANES).transpose(1, 2, 0, 3).reshape(n_even, 3, CONV_WIDTH)
    pool_scale = small[:, 3 * n_even:3 * n_even + n_odd].transpose(1, 0, 2).reshape(n_odd, HALF)
    sgu_norm = small[:, 3 * n_even + n_odd:3 * n_even + 2 * n_odd].transpose(1, 0, 2).reshape(n_odd, HALF)
    gathering, after, zero = [], small, 0.0
    for i, group in enumerate(later):
        gathering.append(_split_start(f"gather_start{i}", _gather_plan, N_CHIPS, shards(group, zero), after))
        after = gathering[-1][4]
        zero = after[0, 0]
    gathered = {}

    def rows(g):
        return g.reshape(N_CHIPS * g.shape[1], g.shape[2])

    passing = {}

    def fetch(i, x_in):
        got = _split_wait(f"gather_wait{i}", _gather_plan, gathering[i][:4], x_in)[1]
        passing[i] = _split_start(f"pass_start{i}", _pass_plan, N_CHIPS - 1, got, x_in, "self")
        return passing[i][4][0, 0]

    def weights_of(l, block, x_in):
        zero = after[0, 0] if (l, block) == (0, "ffn1") else 0.0
        if (l, block) not in gathered:
            i = next(i for i, group in enumerate(later) if (l, block) in group)
            if i not in passing:
                zero = zero + fetch(i, x_in)
            got = _split_wait(f"pass_wait{i}", _pass_plan, passing.pop(i)[:4], x_in)[0]
            for n, key in enumerate(later[i]):
                gathered[key] = got[2 * n:2 * n + 2]
        if block == "ffn2" and l + 1 < depth:
            zero = zero + fetch(next(i for i, group in enumerate(later) if (l + 1, "ffn1") in group), x_in)
        w_in, w_out = gathered[(l, block)]
        if block != "mix":
            return dict(norm=w[block + "_norm"][l] + zero, w_in4=w_in, w_out=rows(w_out))
        j = l // 2
        if l % 2 == 0:
            mix = dict(conv_w=conv_w[j], q_gain=w["ev_q_norm"][j], k_gain=w["ev_k_norm"][j])
        else:
            mix = dict(pool_w=w["od_pool_w"][j], pool_scale=pool_scale[j], sgu_norm=sgu_norm[j],
                       sgu_w=w["od_sgu_w"][j], sgu_b=w["od_sgu_b"][j])
        return dict(mix, norm=w["mix_norm"][l] + zero, w_in=_join_cols(w_in), w_out=rows(w_out))

    def by_chip(dw):
        return dw.reshape(N_CHIPS, dw.shape[0] // N_CHIPS, dw.shape[1])

    bufs = {n: lax.empty(w[n].shape, F32) for n in _BIG}
    small_grads = {n: [None] * w[n].shape[0] for n in _WEIGHTS if n not in _BIG and n != "final_norm"}
    swapping, scattering, joining, group = [], [], [], []

    def finish_swap(after):
        tag, names, started = swapping.pop()
        local, from_sibling = _split_wait(f"swap_wait{tag}", _swap_plan, started[:4], after)
        halves = [_add_sibling(f"add_sibling{tag}_{n}", a, b, core) for (n, _), a, b in zip(names, local, from_sibling)]
        scattering.append((tag, names, _split_start(f"scatter_start{tag}", _scatter_plan, N_CHIPS - 1, halves, after)))
        return scattering[-1][2][4][0, 0]

    def finish_scatter(after):
        tag, names, started = scattering.pop(0)
        halves, got = _split_wait(f"scatter_wait{tag}", _scatter_plan, started[:4], after)
        for i, (n, j) in enumerate(names):
            bufs[n] = _add_chips(f"add_chips{tag}_{n}", halves[i], got[i], place, bufs[n], j)
        layers = [j for _, j in names]
        started = _split_start(f"join_start{tag}", _join_plan(layers), 1, [bufs[n] for n, _ in names], after, "self")
        for (n, _), b in zip(names, started[2]):
            bufs[n] = b
        joining.append((tag, names, layers, started))
        return started[4]

    def finish_join(after):
        tag, names, layers, started = joining.pop(0)
        joined = _split_wait(f"join_wait{tag}", _join_plan(layers), (started[0], started[1], [bufs[n] for n, _ in names], None),
                             after)[0]
        for (n, _), b in zip(names, joined):
            bufs[n] = b

    def small_done(l, block, g):
        if block == "mix":
            renamed = (dict(conv_w="ev_conv_w", q_gain="ev_q_norm", k_gain="ev_k_norm") if l % 2 == 0 else
                       dict(pool_w="od_pool_w", pool_scale="od_pool_scale", sgu_norm="od_sgu_norm", sgu_w="od_sgu_w",
                            sgu_b="od_sgu_b"))
            for key, n in renamed.items():
                small_grads[n][l // 2] = g[key]
        small_grads[block + "_norm"][l] = g["norm"]

    def mid(a):
        zero = 0.0
        if swapping:
            if scattering:
                zero = zero + finish_scatter(a)[0, 0]
            zero = zero + finish_swap(a)
        return zero

    def grads_ready(l, block, g, a):
        local = [_split_cols(g["w_in"]) if block == "mix" else g["w_in4"], by_chip(g["w_out"])]
        group.extend(zip(sharded(l, block), local))
        if block == "ffn2" or (block == "mix" and l > 0):
            return 0.0
        tag = f"{l}_{block}"
        names, local = [n for n, _ in group], [b for _, b in group]
        group.clear()
        shapes = [(N_CHIPS, b.shape[1] // 2, b.shape[2]) for b in local]
        swapping.append((tag, names, _split_start(f"swap_start{tag}", _swap_plan, 1, local, core, shapes)))
        return swapping[-1][2][4][0, 0]

    loss_part, grad_x, dfinal = _local_step(x, target, depth, weights_of, w["final_norm"], mid, grads_ready, small_done)
    loss = lax.psum(loss_part, ("x", "y", "c"))

    grads, updates = {}, {}

    def update(n, after):
        updates[n] = _adamw("adamw_" + n, w[n], grads[n] if n in grads else bufs[n], m[n], v[n], after)
        return updates[n][1]

    finish_swap(grad_x)
    behind = scattering[-1][2][4]
    while len(joining) > 0:
        finish_join(behind)
    behind = finish_scatter(behind)
    for n in ("od_w_in", "od_w_out"):
        behind = update(n, behind)
    finish_join(behind)
    for n in ("ffn2_w_in", "ffn2_w_out", "ev_w_in", "ev_w_out"):
        behind = update(n, behind)
    behind = finish_scatter(behind)
    small_grads = {n: jnp.stack(parts) for n, parts in small_grads.items()}
    small_grads["final_norm"] = dfinal
    names = list(small_grads)
    flat = jnp.concatenate([small_grads[n].reshape(-1) for n in names])
    total = flat.shape[0]
    flat = jnp.concatenate([flat, jnp.zeros((-total) % (N_DEV * 8 * LANES), F32)])
    summed = _allreduce_small(flat.reshape(-1, LANES), behind).reshape(-1)
    finish_join(summed)
    for n in ("ffn1_w_in", "ffn1_w_out"):
        behind = update(n, behind)
    off = 0
    for n in names:
        size = small_grads[n].size
        full_grad = summed[off:off + size].reshape(small_grads[n].shape)
        off += size
        if n in _SMALL_SHARDED:
            full_grad = lax.dynamic_slice_in_dim(full_grad, chip * LANES, LANES, axis=full_grad.ndim - 1)
        grads[n] = full_grad
    for n in _WEIGHTS:
        if n not in updates:
            behind = update(n, behind)
    return (loss, grad_x, *[updates[n][3] for n in _WEIGHTS], *[updates[n][0] for n in _WEIGHTS],
            *[updates[n][1] for n in _WEIGHTS], *[updates[n][2] for n in _WEIGHTS])
```

```python
import jax
import jax.numpy as jnp
from jax import lax
from jax.experimental import pallas as pl
from jax.experimental.pallas import tpu as pltpu

F32 = jnp.float32
_MXU = jnp.bfloat16
_ACT = jnp.bfloat16

D_MODEL = 1024
GRID_W = 64
HEAD_DIM = 64
N_Q_HEADS = 8
N_KV_HEADS = 2
Q_PER_KV = N_Q_HEADS // N_KV_HEADS
ATTN_WIDTH = N_Q_HEADS * HEAD_DIM
KV_WIDTH = N_KV_HEADS * HEAD_DIM
ROPE_THETA = 10000.0
CONV_WIDTH = D_MODEL // 2
POOL_RADII = (1, 2, 4, 8)
POOL_GROUP = 128
SGU_GROUP = 128
SGU_CHUNK = 128
N_GROUPS = 4
HALF = D_MODEL // 2
EPS = 1e-6
HALO = 8
LANES = 128
N_CHIPS = 4
N_DEV = 8

ADAM_LR = 0.001
ADAM_B1 = 0.9
ADAM_B2 = 0.999
ADAM_EPS = 1e-08
ADAM_WD = 0.01
ADAM_STEP = 10

_VMEM_LIMIT = 56 * 2 ** 20
_MESH = pl.DeviceIdType.MESH
_ANY = pl.BlockSpec(memory_space=pl.ANY)
_VMEM = pl.BlockSpec(memory_space=pltpu.VMEM)

_DN = {
    "nn": (((1,), (0,)), ((), ())),
    "nt": (((1,), (1,)), ((), ())),
    "tn": (((0,), (0,)), ((), ())),
}


def _params(*sem):
    return pltpu.CompilerParams(dimension_semantics=sem, vmem_limit_bytes=_VMEM_LIMIT)


def _tile(n, cap):
    best = None
    d = LANES
    while d <= min(n, cap):
        if n % d == 0:
            best = d
        d += LANES
    return best if best is not None else n


def _dot(a, b, mode="nn"):
    return lax.dot_general(a.astype(_MXU), b.astype(_MXU), _DN[mode], preferred_element_type=F32)


def _cat(*vals):
    vals = [v.astype(_MXU) for v in vals]
    return vals[0] if len(vals) == 1 else jnp.concatenate(vals, axis=1)


def _sigmoid(g):
    return 1.0 / (1.0 + jnp.exp(-g))


def _norm_rows(x, g):
    r = lax.rsqrt(jnp.mean(x * x, axis=-1, keepdims=True) + EPS)
    return (x * r) * g


def _swiglu(g, u):
    return (g * _sigmoid(g)) * u


_GELU_C = 0.7978845608028654


def _gelu(x):
    return 0.5 * x * (1.0 + jnp.tanh(_GELU_C * (x + 0.044715 * (x * x * x))))


def _gelu_grad(x):
    t = jnp.tanh(_GELU_C * (x + 0.044715 * (x * x * x)))
    return 0.5 * (1.0 + t) + 0.5 * x * (1.0 - t * t) * (_GELU_C * (1.0 + 3.0 * 0.044715 * (x * x)))


def _mm(name, grid, mode, a_ops, b_ops, e_ops, out_shape, out_specs, acc_shape, a_fn=_cat, b_fn=_cat, epi=None,
        n_outer=False, m_carried=False, b_pick=None, a_out=False):
    ni, nj, nk = grid
    na, nb, ne = len(a_ops), len(b_ops), len(e_ops)
    multi = isinstance(out_shape, (list, tuple))
    no = len(out_shape) if multi else 1

    def body(*refs):
        a_refs = refs[:na]
        b_refs = refs[na:na + nb]
        e_refs = refs[na + nb:na + nb + ne]
        o_refs = refs[na + nb + ne:na + nb + ne + no]
        a = a_fn(*[r[...] for r in a_refs])
        if a_out:
            @pl.when((pl.program_id(1) == 0) & (pl.program_id(2) == 0))
            def _():
                o_refs[-1][...] = a.astype(o_refs[-1].dtype)
        if b_pick is None:
            b = b_fn(*[r[...] for r in b_refs])
        else:
            b = b_pick(b_refs, pl.program_id(1), pl.program_id(2))
        p = _dot(a, b, mode)

        def finish(acc):
            if epi is None:
                o_refs[0][...] = acc.astype(o_refs[0].dtype)
            else:
                epi(acc, [r[...] for r in e_refs], o_refs)

        if nk == 1:
            finish(p)
        else:
            acc_ref = refs[-1]
            k = pl.program_id(2)

            @pl.when(k == 0)
            def _():
                acc_ref[...] = p

            @pl.when((k > 0) & (k < nk - 1))
            def _():
                acc_ref[...] += p

            @pl.when(k == nk - 1)
            def _():
                finish(acc_ref[...] + p)

    ops = list(a_ops) + list(b_ops) + list(e_ops)
    if n_outer:
        def flip(spec):
            return pl.BlockSpec(spec.block_shape, lambda j, i, k, f=spec.index_map: f(i, j, k))

        grid = (nj, ni, nk)
        ops = [(a, flip(s)) for a, s in ops]
        out_specs = [flip(s) for s in out_specs] if multi else flip(out_specs)
    return pl.pallas_call(
        body, name=name, grid=grid,
        in_specs=[s for _, s in ops],
        out_specs=out_specs, out_shape=out_shape,
        scratch_shapes=[pltpu.VMEM(acc_shape, F32)] if nk > 1 else [],
        compiler_params=_params(*(("arbitrary",) * 3 if m_carried else ("parallel", "parallel", "arbitrary"))),
    )(*[a for a, _ in ops])


def _whole(a):
    return pl.BlockSpec(a.shape, lambda i, j, k: (0,) * a.ndim, pipeline_mode=pl.Buffered(1))


def _norm_bwd_epi(acc, e, o):
    xf, dres, g = e
    r = lax.rsqrt(jnp.mean(xf * xf, axis=-1, keepdims=True) + EPS)
    xhat = xf * r
    dgx = acc * g
    m = jnp.mean(dgx * xhat, axis=-1, keepdims=True)
    dx = dres + r * (dgx - xhat * m)
    o[0][...] = dx
    o[2][...] = dx.astype(o[2].dtype)
    part = jnp.sum(acc * xhat, axis=0, keepdims=True)
    i = pl.program_id(0)

    @pl.when(i == 0)
    def _():
        o[1][...] = part

    @pl.when(i > 0)
    def _():
        o[1][...] += part


def _norm_bwd_ops(x, dres, gain, tm):
    t, d = x.shape
    row = pl.BlockSpec((tm, d), lambda i, j, k: (i, 0))
    vec = pl.BlockSpec((1, d), lambda i, j, k: (0, 0))
    return ([(x, row), (dres, row), (gain.reshape(1, d), vec)],
            [jax.ShapeDtypeStruct((t, d), F32), jax.ShapeDtypeStruct((1, d), F32), jax.ShapeDtypeStruct((t, d), _ACT)],
            [row, vec, row])


def _rows(t):
    return _tile(t, 512)


def _rmsnorm_fwd(name, x, gain):
    t, d = x.shape
    tr = _rows(t)

    def body(x_ref, g_ref, h_ref):
        xf = x_ref[...]
        r = lax.rsqrt(jnp.mean(xf * xf, axis=-1, keepdims=True) + EPS)
        h_ref[...] = ((xf * r) * g_ref[...]).astype(h_ref.dtype)

    return pl.pallas_call(
        body, name=name, grid=(t // tr,),
        in_specs=[pl.BlockSpec((tr, d), lambda i: (i, 0)), pl.BlockSpec((1, d), lambda i: (0, 0))],
        out_specs=pl.BlockSpec((tr, d), lambda i: (i, 0)),
        out_shape=jax.ShapeDtypeStruct((t, d), _ACT),
        compiler_params=_params("parallel"),
    )(x, gain.reshape(1, d))


def _rmsnorm_bwd(name, dh, x, gain, dres):
    t, d = x.shape
    tr = _rows(t)

    def body(dh_ref, x_ref, g_ref, dres_ref, dx_ref, dg_ref):
        i = pl.program_id(0)
        xf = x_ref[...]
        r = lax.rsqrt(jnp.mean(xf * xf, axis=-1, keepdims=True) + EPS)
        xhat = xf * r
        dy = dh_ref[...].astype(F32)
        dgx = dy * g_ref[...]
        m = jnp.mean(dgx * xhat, axis=-1, keepdims=True)
        dx_ref[...] = dres_ref[...] + r * (dgx - xhat * m)
        part = jnp.sum(dy * xhat, axis=0, keepdims=True)

        @pl.when(i == 0)
        def _():
            dg_ref[...] = part

        @pl.when(i > 0)
        def _():
            dg_ref[...] += part

    row = pl.BlockSpec((tr, d), lambda i: (i, 0))
    vec = pl.BlockSpec((1, d), lambda i: (0, 0))
    dx, dg = pl.pallas_call(
        body, name=name, grid=(t // tr,),
        in_specs=[row, row, vec, row],
        out_specs=(row, vec),
        out_shape=(jax.ShapeDtypeStruct((t, d), F32), jax.ShapeDtypeStruct((1, d), F32)),
        compiler_params=_params("arbitrary"),
    )(dh, x, gain.reshape(1, d), dres)
    return dx, dg.reshape(d)


def _final_loss(name, x, gain, target):
    t, d = x.shape
    tr = _rows(t)

    def body(x_ref, g_ref, t_ref, dx_ref, dg_ref, loss_ref, lo_ref):
        i = pl.program_id(0)
        xf = x_ref[...]
        r = lax.rsqrt(jnp.mean(xf * xf, axis=-1, keepdims=True) + EPS)
        xhat = xf * r
        g = g_ref[...]
        err = xhat * g - t_ref[...]
        lpart = 0.5 * jnp.sum(jnp.mean(err * err, axis=-1, keepdims=True), axis=0, keepdims=True)
        dy = err * (1.0 / d)
        dgx = dy * g
        m = jnp.mean(dgx * xhat, axis=-1, keepdims=True)
        dx = r * (dgx - xhat * m)
        dx_ref[...] = dx
        lo_ref[...] = dx.astype(lo_ref.dtype)
        part = jnp.sum(dy * xhat, axis=0, keepdims=True)
        lrow = jnp.broadcast_to(lpart, (1, LANES))

        @pl.when(i == 0)
        def _():
            dg_ref[...] = part
            loss_ref[...] = lrow

        @pl.when(i > 0)
        def _():
            dg_ref[...] += part
            loss_ref[...] += lrow

    row = pl.BlockSpec((tr, d), lambda i: (i, 0))
    vec = pl.BlockSpec((1, d), lambda i: (0, 0))
    dx, dg, loss, dx_lo = pl.pallas_call(
        body, name=name, grid=(t // tr,),
        in_specs=[row, vec, row],
        out_specs=(row, vec, pl.BlockSpec((1, LANES), lambda i: (0, 0)), row),
        out_shape=(jax.ShapeDtypeStruct((t, d), F32), jax.ShapeDtypeStruct((1, d), F32),
                   jax.ShapeDtypeStruct((1, LANES), F32), jax.ShapeDtypeStruct((t, d), _ACT)),
        compiler_params=_params("arbitrary"),
    )(x, gain.reshape(1, d), target)
    return loss[0, 0], dx, dx_lo, dg.reshape(d)


_FFN_TILES = dict(in_tm=1024, out_tm=512, dact_tm=512, dwout_tk=1024, dh_tm=512, dwin_tk=2048)


def _ffn_tiles(layer, which):
    return _FFN_TILES


def _ffn_fwd(tag, x, gain, w_in4, w_out, cfg):
    t, d = x.shape
    fs = w_in4.shape[2]
    f = 2 * fs
    tm = _tile(t, cfg["in_tm"])
    gain = gain.reshape(1, d)
    gu, h = _mm(
        tag + "_in", (t // tm, N_CHIPS, 1), "nn",
        [(x, pl.BlockSpec((tm, d), lambda i, j, k: (i, 0))), (gain, pl.BlockSpec((1, d), lambda i, j, k: (0, 0)))],
        [(w_in4, pl.BlockSpec((None, d, fs), lambda i, j, k: (j, 0, 0)))], [],
        [jax.ShapeDtypeStruct((2, t, f), _ACT), jax.ShapeDtypeStruct((t, d), _ACT)],
        [pl.BlockSpec((None, tm, fs), lambda i, j, k: (j // 2, i, j % 2)), pl.BlockSpec((tm, d), lambda i, j, k: (i, 0))],
        None, a_fn=_norm_rows, m_carried=True, a_out=True)
    tm2 = _tile(t, cfg["out_tm"])

    def epi(acc, e, o):
        o[0][...] = e[0] + 0.5 * acc

    x_out = _mm(
        tag + "_out", (t // tm2, 1, 1), "nn",
        [(gu, pl.BlockSpec((None, tm2, f), lambda i, j, k: (0, i, 0))),
         (gu, pl.BlockSpec((None, tm2, f), lambda i, j, k: (1, i, 0)))],
        [(w_out, pl.BlockSpec((f, d), lambda i, j, k: (0, 0)))],
        [(x, pl.BlockSpec((tm2, d), lambda i, j, k: (i, 0)))],
        jax.ShapeDtypeStruct((t, d), F32),
        pl.BlockSpec((tm2, d), lambda i, j, k: (i, 0)), None,
        a_fn=_swiglu, epi=epi)
    return x_out, (x, h, gu)


def _ffn_bwd(tag, dxo, dxo_lo, saved, gain, w_in4, w_out, cfg, mid, ready):
    x, h, gu = saved
    t, d = x.shape
    fs = w_in4.shape[2]
    f = 2 * fs
    tm = _tile(t, cfg["dact_tm"])
    tk = _tile(t, cfg["dwout_tk"])

    def epi_act(acc, e, o):
        g, u = e
        da = (0.5 * acc).astype(g.dtype)
        sig = _sigmoid(g)
        silu = g * sig
        o[0][0] = (da * u * (sig + silu * (1.0 - sig))).astype(o[0].dtype)
        o[0][1] = (da * silu).astype(o[0].dtype)

    dgu = _mm(
        tag + "_dact", (t // tm, 2, 1), "nt",
        [(dxo_lo, pl.BlockSpec((tm, d), lambda i, j, k: (i, 0)))],
        [(w_out, _whole(w_out))],
        [(gu, pl.BlockSpec((None, tm, fs), lambda i, j, k: (0, i, j))),
         (gu, pl.BlockSpec((None, tm, fs), lambda i, j, k: (1, i, j)))],
        jax.ShapeDtypeStruct((2, t, f), _ACT),
        pl.BlockSpec((2, tm, fs), lambda i, j, k: (0, i, j)), None, epi=epi_act,
        b_pick=lambda b, j, k: b[0][pl.ds(pl.multiple_of(j * fs, LANES), fs), :])
    gain = gain + mid(dgu)

    def epi_half(acc, e, o):
        o[0][...] = (0.5 * acc).astype(o[0].dtype)

    dw_out = _mm(
        tag + "_dwout", (2, 1, t // tk), "tn",
        [(gu, pl.BlockSpec((None, tk, fs), lambda i, j, k: (0, k, i))),
         (gu, pl.BlockSpec((None, tk, fs), lambda i, j, k: (1, k, i)))],
        [(dxo_lo, pl.BlockSpec((tk, d), lambda i, j, k: (k, 0)))], [],
        jax.ShapeDtypeStruct((f, d), _ACT),
        pl.BlockSpec((fs, d), lambda i, j, k: (i, 0)), (fs, d),
        a_fn=_swiglu, epi=epi_half)
    tk = _tile(t, cfg["dwin_tk"])
    dw_in4 = _mm(
        tag + "_dwin", (1, N_CHIPS, t // tk), "tn",
        [(h, pl.BlockSpec((tk, d), lambda i, j, k: (k, 0)))],
        [(dgu, pl.BlockSpec((None, tk, fs), lambda i, j, k: (j // 2, k, j % 2)))], [],
        jax.ShapeDtypeStruct((N_CHIPS, d, fs), _ACT),
        pl.BlockSpec((None, d, fs), lambda i, j, k: (j, 0, 0)), (d, fs))
    gain = gain + ready(dict(w_in4=dw_in4, w_out=dw_out), dw_in4)
    tm = _tile(t, cfg["dh_tm"])
    e_ops, shapes, specs = _norm_bwd_ops(x, dxo, gain, tm)
    dx, dgain, dx_lo = _mm(
        tag + "_dh", (t // tm, 1, 2), "nt",
        [(dgu, pl.BlockSpec((None, tm, f), lambda i, j, k: (k, i, 0)))],
        [(w_in4, _whole(w_in4))], e_ops, shapes, specs, (tm, d), epi=_norm_bwd_epi, m_carried=True,
        b_pick=lambda b, j, k: jnp.concatenate([b[0][2 * k], b[0][2 * k + 1]], axis=1))
    return dx, dx_lo, dgain.reshape(d)


_MIX_TILES = dict(tm=1024, dwout_tk=2048, dwin_tk=1024)


def _proj_in(tag, h, w_in):
    t, d = h.shape
    n = w_in.shape[1]
    tm = _tile(t, _MIX_TILES["tm"])
    return _mm(
        tag + "_in", (t // tm, 1, 1), "nn",
        [(h, pl.BlockSpec((tm, d), lambda i, j, k: (i, 0)))],
        [(w_in, pl.BlockSpec((d, n), lambda i, j, k: (0, 0)))], [],
        jax.ShapeDtypeStruct((t, n), _ACT),
        pl.BlockSpec((tm, n), lambda i, j, k: (i, 0)), None)


def _proj_out(tag, x, parts, w_out):
    t, d = x.shape
    tm = _tile(t, _MIX_TILES["tm"])

    def epi(acc, e, o):
        o[0][...] = e[0] + acc

    return _mm(
        tag + "_out", (t // tm, 1, 1), "nn",
        [(p, pl.BlockSpec((tm, p.shape[1]), lambda i, j, k: (i, 0))) for p in parts],
        [(w_out, pl.BlockSpec(w_out.shape, lambda i, j, k: (0, 0)))],
        [(x, pl.BlockSpec((tm, d), lambda i, j, k: (i, 0)))],
        jax.ShapeDtypeStruct((t, d), F32),
        pl.BlockSpec((tm, d), lambda i, j, k: (i, 0)), None, epi=epi)


def _proj_out_bwd(tag, dxo, parts, w_out):
    t, d = dxo.shape
    mix = w_out.shape[0]
    tm = _tile(t, _MIX_TILES["tm"])
    tk = _tile(t, _MIX_TILES["dwout_tk"])
    d_mix = _mm(
        tag + "_dmix", (t // tm, 1, 1), "nt",
        [(dxo, pl.BlockSpec((tm, d), lambda i, j, k: (i, 0)))],
        [(w_out, pl.BlockSpec((mix, d), lambda i, j, k: (0, 0)))], [],
        jax.ShapeDtypeStruct((t, mix), F32),
        pl.BlockSpec((tm, mix), lambda i, j, k: (i, 0)), None)
    dw_out = _mm(
        tag + "_dwout", (1, 1, t // tk), "tn",
        [(p, pl.BlockSpec((tk, p.shape[1]), lambda i, j, k: (k, 0))) for p in parts],
        [(dxo, pl.BlockSpec((tk, d), lambda i, j, k: (k, 0)))], [],
        jax.ShapeDtypeStruct((mix, d), _ACT),
        pl.BlockSpec((mix, d), lambda i, j, k: (0, 0)), (mix, d))
    return d_mix, dw_out


def _proj_in_bwd(tag, h, dparts, w_in, x, dres, gain, ready):
    t, d = h.shape
    n = w_in.shape[1]
    tm = _tile(t, _MIX_TILES["tm"])
    tk = _tile(t, _MIX_TILES["dwin_tk"])
    dw_in = _mm(
        tag + "_dwin", (1, 1, t // tk), "tn",
        [(h, pl.BlockSpec((tk, d), lambda i, j, k: (k, 0)))],
        [(p, pl.BlockSpec((tk, p.shape[1]), lambda i, j, k: (k, 0))) for p in dparts], [],
        jax.ShapeDtypeStruct((d, n), _ACT),
        pl.BlockSpec((d, n), lambda i, j, k: (0, 0)), (d, n))
    e_ops, shapes, specs = _norm_bwd_ops(x, dres, gain + ready(dw_in), tm)
    dx, dgain, dx_lo = _mm(
        tag + "_dh", (t // tm, 1, 1), "nt",
        [(p, pl.BlockSpec((tm, p.shape[1]), lambda i, j, k: (i, 0))) for p in dparts],
        [(w_in, pl.BlockSpec((d, n), lambda i, j, k: (0, 0)))], e_ops, shapes, specs, None,
        epi=_norm_bwd_epi, m_carried=True)
    return dx, dx_lo, dgain.reshape(d)


def _shifted(pad_ref, val, s):
    pad_ref[pl.ds(HALO, s), :] = val
    return pad_ref[pl.ds(HALO - 1, s), :], pad_ref[pl.ds(HALO + 1, s), :]


def _zero_halo(pad_ref, s):
    z = jnp.zeros((HALO, pad_ref.shape[1]), F32)
    pad_ref[pl.ds(0, HALO), :] = z
    pad_ref[pl.ds(HALO + s, HALO), :] = z


def _conv_fwd(tag, proj, conv_w, nb, s):
    t = proj.shape[0]
    ncb = CONV_WIDTH // LANES

    def body(gb_ref, gc_ref, hc_ref, w_ref, a_ref, pad_ref):
        _zero_halo(pad_ref, s)
        cg = gc_ref[...].astype(F32) * hc_ref[...].astype(F32)
        prev, nxt = _shifted(pad_ref, cg, s)
        w = w_ref[...]
        conv = prev * w[0:1, :] + cg * w[1:2, :] + nxt * w[2:3, :]
        a_ref[...] = (gb_ref[...].astype(F32) * conv).astype(a_ref.dtype)

    def col(off):
        return pl.BlockSpec((s, LANES), lambda b, c: (b, off + c))

    return pl.pallas_call(
        body, name=tag + "_conv", grid=(nb, ncb),
        in_specs=[col(0), col(ncb), col(2 * ncb), pl.BlockSpec((3, LANES), lambda b, c: (0, c))],
        out_specs=col(0),
        out_shape=jax.ShapeDtypeStruct((t, CONV_WIDTH), _ACT),
        scratch_shapes=[pltpu.VMEM((s + 2 * HALO, LANES), F32)],
        compiler_params=_params("parallel", "parallel"),
    )(proj, proj, proj, conv_w)


def _conv_bwd(tag, proj, conv_w, d_mix, nb, s):
    t = proj.shape[0]
    ncb = CONV_WIDTH // LANES

    def body(gb_ref, gc_ref, hc_ref, w_ref, da_ref, dgb_ref, dgc_ref, dhc_ref, dw_ref, pad_ref):
        b = pl.program_id(1)
        _zero_halo(pad_ref, s)
        gb = gb_ref[...].astype(F32)
        gc = gc_ref[...].astype(F32)
        hc = hc_ref[...].astype(F32)
        w = w_ref[...]
        da = da_ref[...]
        cg = gc * hc
        prev, nxt = _shifted(pad_ref, cg, s)
        conv = prev * w[0:1, :] + cg * w[1:2, :] + nxt * w[2:3, :]
        dgb_ref[...] = (da * conv).astype(dgb_ref.dtype)
        dconv = da * gb
        dw = jnp.concatenate([
            jnp.sum(dconv * prev, axis=0, keepdims=True),
            jnp.sum(dconv * cg, axis=0, keepdims=True),
            jnp.sum(dconv * nxt, axis=0, keepdims=True)], axis=0)
        dprev, dnxt = _shifted(pad_ref, dconv, s)
        dcg = dnxt * w[0:1, :] + dconv * w[1:2, :] + dprev * w[2:3, :]
        dgc_ref[...] = (dcg * hc).astype(dgc_ref.dtype)
        dhc_ref[...] = (dcg * gc).astype(dhc_ref.dtype)

        @pl.when(b == 0)
        def _():
            dw_ref[...] = dw

        @pl.when(b > 0)
        def _():
            dw_ref[...] += dw

    def col(off):
        return pl.BlockSpec((s, LANES), lambda c, b: (b, off + c))

    wspec = pl.BlockSpec((3, LANES), lambda c, b: (0, c))
    act = jax.ShapeDtypeStruct((t, CONV_WIDTH), _ACT)
    return pl.pallas_call(
        body, name=tag + "_dconv", grid=(ncb, nb),
        in_specs=[col(0), col(ncb), col(2 * ncb), wspec, col(0)],
        out_specs=(col(0), col(0), col(0), wspec),
        out_shape=(act, act, act, jax.ShapeDtypeStruct((3, CONV_WIDTH), F32)),
        scratch_shapes=[pltpu.VMEM((s + 2 * HALO, LANES), F32)],
        compiler_params=_params("parallel", "arbitrary"),
    )(proj, proj, proj, conv_w, d_mix)


def _rope_tables(s):
    rows = s // GRID_W
    r_idx, c_idx = jnp.meshgrid(jnp.arange(rows), jnp.arange(GRID_W), indexing="ij")
    r_idx = r_idx.reshape(-1).astype(F32)
    c_idx = c_idx.reshape(-1).astype(F32)
    n_freq = HEAD_DIM // 4
    inv = ROPE_THETA ** (-jnp.arange(n_freq, dtype=F32) / n_freq)
    ang = jnp.concatenate([r_idx[:, None] * inv, c_idx[:, None] * inv], axis=-1)
    cos = jnp.repeat(jnp.cos(ang), 2, axis=1)
    sin = jnp.repeat(jnp.sin(ang), 2, axis=1)
    sign = jnp.where(jnp.arange(HEAD_DIM) % 2 == 0, -1.0, 1.0).astype(F32)
    return jnp.tile(cos, (1, LANES // HEAD_DIM)), jnp.tile(sin * sign, (1, LANES // HEAD_DIM))


def _head_ones():
    i = jnp.arange(LANES) // HEAD_DIM
    return (i[:, None] == i[None, :]).astype(jnp.bfloat16)


def _head_sum(v, ones):
    outs = []
    for j in range(v.shape[1] // LANES):
        c = v[:, j * LANES:(j + 1) * LANES]
        hi = c.astype(jnp.bfloat16)
        lo = (c - hi.astype(F32)).astype(jnp.bfloat16)
        outs.append(jnp.dot(hi, ones, preferred_element_type=F32) + jnp.dot(lo, ones, preferred_element_type=F32))
    return outs[0] if len(outs) == 1 else jnp.concatenate(outs, axis=1)


def _pair_swap(v):
    outs = []
    for j in range(v.shape[1] // LANES):
        c = v[:, j * LANES:(j + 1) * LANES]
        lane = lax.broadcasted_iota(jnp.int32, c.shape, 1)
        outs.append(jnp.where(lane % 2 == 0, pltpu.roll(c, LANES - 1, 1), pltpu.roll(c, 1, 1)))
    return outs[0] if len(outs) == 1 else jnp.concatenate(outs, axis=1)


def _wide(tab, width):
    return tab if width == LANES else jnp.concatenate([tab] * (width // LANES), axis=1)


_QK_SCALE = HEAD_DIM ** -0.5


def _qk_fwd(tag, proj, q_gain, k_gain, cos, sin, nb, s):
    t = proj.shape[0]
    tr = _tile(s, 512)
    ns = s // tr
    q_off = 3 * CONV_WIDTH // ATTN_WIDTH
    k_off = (3 * CONV_WIDTH + ATTN_WIDTH) // KV_WIDTH

    def body(q_ref, k_ref, qg_ref, kg_ref, cos_ref, sin_ref, ones_ref, qo_ref, ko_ref):
        ones = ones_ref[...]
        for src, g_ref, dst, mult in ((q_ref, qg_ref, qo_ref, _QK_SCALE), (k_ref, kg_ref, ko_ref, 1.0)):
            v = src[...].astype(F32)
            w = v.shape[1]
            r = lax.rsqrt(_head_sum(v * v, ones) * (1.0 / HEAD_DIM) + EPS)
            vn = (v * r) * g_ref[...]
            rot = vn * _wide(cos_ref[...], w) + _pair_swap(vn) * _wide(sin_ref[...], w)
            dst[...] = (rot * mult).astype(dst.dtype)

    tab = pl.BlockSpec((tr, LANES), lambda i: (i % ns, 0))
    return pl.pallas_call(
        body, name=tag + "_qk", grid=(t // tr,),
        in_specs=[pl.BlockSpec((tr, ATTN_WIDTH), lambda i: (i, q_off)),
                  pl.BlockSpec((tr, KV_WIDTH), lambda i: (i, k_off)),
                  pl.BlockSpec((1, ATTN_WIDTH), lambda i: (0, 0)),
                  pl.BlockSpec((1, KV_WIDTH), lambda i: (0, 0)),
                  tab, tab, pl.BlockSpec((LANES, LANES), lambda i: (0, 0))],
        out_specs=(pl.BlockSpec((tr, ATTN_WIDTH), lambda i: (i, 0)),
                   pl.BlockSpec((tr, KV_WIDTH), lambda i: (i, 0))),
        out_shape=(jax.ShapeDtypeStruct((t, ATTN_WIDTH), _ACT), jax.ShapeDtypeStruct((t, KV_WIDTH), _ACT)),
        compiler_params=_params("parallel"),
    )(proj, proj, jnp.tile(q_gain, N_Q_HEADS).reshape(1, ATTN_WIDTH),
      jnp.tile(k_gain, N_KV_HEADS).reshape(1, KV_WIDTH), cos, sin, _head_ones())


def _qk_bwd(tag, proj, q_gain, k_gain, cos, sin, dq_rot, dk_rot, nb, s):
    t = proj.shape[0]
    tr = _tile(s, 512)
    ns = s // tr
    q_off = 3 * CONV_WIDTH // ATTN_WIDTH
    k_off = (3 * CONV_WIDTH + ATTN_WIDTH) // KV_WIDTH

    def body(q_ref, k_ref, qg_ref, kg_ref, cos_ref, sin_ref, ones_ref, dqr_ref, dkr_ref,
             dq_ref, dk_ref, dqg_ref, dkg_ref):
        i = pl.program_id(0)
        ones = ones_ref[...]
        for src, g_ref, dr_ref, dst, dg_ref, mult in ((q_ref, qg_ref, dqr_ref, dq_ref, dqg_ref, _QK_SCALE),
                                                      (k_ref, kg_ref, dkr_ref, dk_ref, dkg_ref, 1.0)):
            v = src[...].astype(F32)
            w = v.shape[1]
            r = lax.rsqrt(_head_sum(v * v, ones) * (1.0 / HEAD_DIM) + EPS)
            xhat = v * r
            dr = dr_ref[...] * mult
            dvn = dr * _wide(cos_ref[...], w) + _pair_swap(dr * _wide(sin_ref[...], w))
            dgx = dvn * g_ref[...]
            m = _head_sum(dgx * xhat, ones) * (1.0 / HEAD_DIM)
            dst[...] = (r * (dgx - xhat * m)).astype(dst.dtype)
            part = jnp.sum(dvn * xhat, axis=0, keepdims=True)
            fold = part[:, 0:HEAD_DIM]
            for hh in range(1, w // HEAD_DIM):
                fold = fold + part[:, hh * HEAD_DIM:(hh + 1) * HEAD_DIM]

            @pl.when(i == 0)
            def _():
                dg_ref[...] = fold

            @pl.when(i > 0)
            def _():
                dg_ref[...] += fold

    tab = pl.BlockSpec((tr, LANES), lambda i: (i % ns, 0))
    qrow = pl.BlockSpec((tr, ATTN_WIDTH), lambda i: (i, 0))
    krow = pl.BlockSpec((tr, KV_WIDTH), lambda i: (i, 0))
    gvec = pl.BlockSpec((1, HEAD_DIM), lambda i: (0, 0))
    dq, dk, dqg, dkg = pl.pallas_call(
        body, name=tag + "_dqk", grid=(t // tr,),
        in_specs=[pl.BlockSpec((tr, ATTN_WIDTH), lambda i: (i, q_off)),
                  pl.BlockSpec((tr, KV_WIDTH), lambda i: (i, k_off)),
                  pl.BlockSpec((1, ATTN_WIDTH), lambda i: (0, 0)),
                  pl.BlockSpec((1, KV_WIDTH), lambda i: (0, 0)),
                  tab, tab, pl.BlockSpec((LANES, LANES), lambda i: (0, 0)), qrow, krow],
        out_specs=(qrow, krow, gvec, gvec),
        out_shape=(jax.ShapeDtypeStruct((t, ATTN_WIDTH), _ACT), jax.ShapeDtypeStruct((t, KV_WIDTH), _ACT),
                   jax.ShapeDtypeStruct((1, HEAD_DIM), F32), jax.ShapeDtypeStruct((1, HEAD_DIM), F32)),
        compiler_params=_params("arbitrary"),
    )(proj, proj, jnp.tile(q_gain, N_Q_HEADS).reshape(1, ATTN_WIDTH),
      jnp.tile(k_gain, N_KV_HEADS).reshape(1, KV_WIDTH), cos, sin, _head_ones(), dq_rot, dk_rot)
    return dq, dk, dqg.reshape(HEAD_DIM), dkg.reshape(HEAD_DIM)


def _head(v, h):
    return v[:, h * HEAD_DIM:(h + 1) * HEAD_DIM]


def _attn_fwd(tag, q, k, proj, nb, s):
    t = q.shape[0]
    tq = _tile(s, 256)
    nq = s // tq
    v_off = (3 * CONV_WIDTH + ATTN_WIDTH + KV_WIDTH) // KV_WIDTH

    def body(q_ref, k_ref, v_ref, o_ref, lse_ref):
        qv = q_ref[...]
        kv = k_ref[...]
        vv = v_ref[...]
        for h in range(N_Q_HEADS):
            j = h // Q_PER_KV
            sc = _dot(_head(qv, h), _head(kv, j), "nt")
            m = jnp.max(sc, axis=-1, keepdims=True)
            e = jnp.exp(sc - m)
            l = jnp.sum(e, axis=-1, keepdims=True)
            o = _dot(e, _head(vv, j)) * (1.0 / l)
            o_ref[:, h * HEAD_DIM:(h + 1) * HEAD_DIM] = o.astype(o_ref.dtype)
            lse_ref[:, h:h + 1] = m + jnp.log(l)

    return pl.pallas_call(
        body, name=tag + "_attn", grid=(nb, nq),
        in_specs=[pl.BlockSpec((tq, ATTN_WIDTH), lambda b, i: (b * nq + i, 0)),
                  pl.BlockSpec((s, KV_WIDTH), lambda b, i: (b, 0)),
                  pl.BlockSpec((s, KV_WIDTH), lambda b, i: (b, v_off))],
        out_specs=(pl.BlockSpec((tq, ATTN_WIDTH), lambda b, i: (b * nq + i, 0)),
                   pl.BlockSpec((tq, N_Q_HEADS), lambda b, i: (b * nq + i, 0))),
        out_shape=(jax.ShapeDtypeStruct((t, ATTN_WIDTH), _ACT), jax.ShapeDtypeStruct((t, N_Q_HEADS), F32)),
        compiler_params=_params("parallel", "parallel"),
    )(q, k, proj)


def _attn_bwd(tag, q, k, proj, o, lse, d_mix, nb, s):
    t = q.shape[0]
    tq = _tile(s, 256)
    nq = s // tq
    v_off = (3 * CONV_WIDTH + ATTN_WIDTH + KV_WIDTH) // KV_WIDTH

    def body(q_ref, k_ref, v_ref, o_ref, lse_ref, do_ref, dq_ref, dk_ref, dv_ref):
        i = pl.program_id(1)

        @pl.when(i == 0)
        def _():
            dk_ref[...] = jnp.zeros_like(dk_ref)
            dv_ref[...] = jnp.zeros_like(dv_ref)

        qv = q_ref[...]
        kv = k_ref[...]
        vv = v_ref[...]
        ov = o_ref[...].astype(F32)
        dov = do_ref[...]
        lse = lse_ref[...]
        for h in range(N_Q_HEADS):
            j = h // Q_PER_KV
            cols = slice(j * HEAD_DIM, (j + 1) * HEAD_DIM)
            qh = _head(qv, h)
            kj = _head(kv, j)
            doh = _head(dov, h)
            sc = _dot(qh, kj, "nt")
            p = jnp.exp(sc - lse[:, h:h + 1])
            dp = _dot(doh, _head(vv, j), "nt")
            delta = jnp.sum(doh * _head(ov, h), axis=-1, keepdims=True)
            ds = p * (dp - delta)
            dv_ref[:, cols] += _dot(p, doh, "tn")
            dk_ref[:, cols] += _dot(ds, qh, "tn")
            dq_ref[:, h * HEAD_DIM:(h + 1) * HEAD_DIM] = _dot(ds, kj)

    qrow = pl.BlockSpec((tq, ATTN_WIDTH), lambda b, i: (b * nq + i, 0))
    kvrow = pl.BlockSpec((s, KV_WIDTH), lambda b, i: (b, 0))
    return pl.pallas_call(
        body, name=tag + "_dattn", grid=(nb, nq),
        in_specs=[qrow, kvrow, pl.BlockSpec((s, KV_WIDTH), lambda b, i: (b, v_off)), qrow,
                  pl.BlockSpec((tq, N_Q_HEADS), lambda b, i: (b * nq + i, 0)),
                  pl.BlockSpec((tq, ATTN_WIDTH), lambda b, i: (b * nq + i, 1))],
        out_specs=(qrow, kvrow, kvrow),
        out_shape=(jax.ShapeDtypeStruct((t, ATTN_WIDTH), F32), jax.ShapeDtypeStruct((t, KV_WIDTH), F32),
                   jax.ShapeDtypeStruct((t, KV_WIDTH), F32)),
        compiler_params=_params("parallel", "arbitrary"),
    )(q, k, proj, o, lse, d_mix)


def _even_fwd(tag, x, p, cos, sin, nb, s):
    h = _rmsnorm_fwd(tag + "_norm", x, p["norm"])
    proj = _proj_in(tag, h, p["w_in"])
    a = _conv_fwd(tag, proj, p["conv_w"], nb, s)
    q, k = _qk_fwd(tag, proj, p["q_gain"], p["k_gain"], cos, sin, nb, s)
    o, lse = _attn_fwd(tag, q, k, proj, nb, s)
    x_out = _proj_out(tag, x, [a, o], p["w_out"])
    return x_out, (x, h, proj, a, q, k, o, lse)


def _even_bwd(tag, dxo, dxo_lo, saved, p, cos, sin, nb, s, mid, ready):
    x, h, proj, a, q, k, o, lse = saved
    d_mix, dw_out = _proj_out_bwd(tag, dxo_lo, [a, o], p["w_out"])
    dgb, dgc, dhc, dconv_w = _conv_bwd(tag, proj, p["conv_w"] + mid(d_mix), d_mix, nb, s)
    dq_rot, dk_rot, dv = _attn_bwd(tag, q, k, proj, o, lse, d_mix, nb, s)
    dq, dk, dq_gain, dk_gain = _qk_bwd(tag, proj, p["q_gain"], p["k_gain"], cos, sin, dq_rot, dk_rot, nb, s)
    dx, dx_lo, dnorm = _proj_in_bwd(tag, h, [dgb, dgc, dhc, dq, dk, dv], p["w_in"], x, dxo, p["norm"],
                                    lambda dw_in: ready(dict(w_in=dw_in, w_out=dw_out), dw_in))
    return dx, dx_lo, dict(norm=dnorm, conv_w=dconv_w, q_gain=dq_gain, k_gain=dk_gain)


def _window(pad_ref, val, r, s):
    pad_ref[pl.ds(HALO, s), :] = val
    acc = val
    for d in range(1, r + 1):
        acc = acc + pad_ref[pl.ds(HALO - d, s), :] + pad_ref[pl.ds(HALO + d, s), :]
    return acc


def _count(r, s):
    t = lax.broadcasted_iota(jnp.int32, (s, 1), 0)
    return (jnp.minimum(t + r, s - 1) - jnp.maximum(t - r, 0) + 1).astype(F32)


def _sgu_chunk(u_ref, v_ref, norm, ws_ref, bt, rows):
    uu = u_ref[rows, :].astype(F32)
    vv = v_ref[rows, :].astype(F32)
    gu = _gelu(uu)
    gv = _gelu(vv)
    r = lax.rsqrt(jnp.mean(gv * gv, axis=-1, keepdims=True) + EPS)
    xhat = gv * r
    vn = xhat * norm
    mixed = []
    for g in range(N_GROUPS):
        cols = slice(g * SGU_GROUP, (g + 1) * SGU_GROUP)
        mixed.append(_dot(ws_ref[g], vn[:, cols]) + bt[:, g:g + 1])
    return uu, vv, gu, r, xhat, vn, mixed


def _odd_core_fwd(tag, proj, p, nb, s):
    t = proj.shape[0]
    nchunk = s // SGU_CHUNK

    def body(p_ref, u_ref, v_ref, pw_ref, ps_ref, sn_ref, ws_ref, bt_ref, mix_ref, pad_ref):
        _zero_halo(pad_ref, s)
        for g, r in enumerate(POOL_RADII):
            cols = slice(g * POOL_GROUP, (g + 1) * POOL_GROUP)
            pg = p_ref[:, cols].astype(F32)
            pooled = _window(pad_ref, pg, r, s) / _count(r, s) - pg
            mix_ref[:, cols] = (_dot(pooled, pw_ref[g]) * ps_ref[:, cols]).astype(mix_ref.dtype)
        norm = sn_ref[...]
        bt = bt_ref[...]

        def chunk(n, carry):
            rows = pl.ds(pl.multiple_of(n * SGU_CHUNK, SGU_CHUNK), SGU_CHUNK)
            _, _, gu, _, _, _, mixed = _sgu_chunk(u_ref, v_ref, norm, ws_ref, bt, rows)
            for g in range(N_GROUPS):
                cols = slice(g * SGU_GROUP, (g + 1) * SGU_GROUP)
                mix_ref[rows, HALF + g * SGU_GROUP:HALF + (g + 1) * SGU_GROUP] = (
                    gu[:, cols] * mixed[g]).astype(mix_ref.dtype)
            return carry

        lax.fori_loop(0, nchunk, chunk, 0)

    def col(j):
        return pl.BlockSpec((s, HALF), lambda b: (b, j))

    def whole(a):
        return pl.BlockSpec(a.shape, lambda b: (0,) * a.ndim)

    consts = [p["pool_w"], p["pool_scale"].reshape(1, HALF), p["sgu_norm"].reshape(1, HALF),
              p["sgu_w"], p["sgu_b"].T]
    return pl.pallas_call(
        body, name=tag + "_core", grid=(nb,),
        in_specs=[col(0), col(1), col(2)] + [whole(a) for a in consts],
        out_specs=pl.BlockSpec((s, D_MODEL), lambda b: (b, 0)),
        out_shape=jax.ShapeDtypeStruct((t, D_MODEL), _ACT),
        scratch_shapes=[pltpu.VMEM((s + 2 * HALO, POOL_GROUP), F32)],
        compiler_params=_params("parallel"),
    )(proj, proj, proj, *consts)


def _odd_core_bwd(tag, proj, p, d_mix, nb, s):
    t = proj.shape[0]
    nchunk = s // SGU_CHUNK

    def body(p_ref, u_ref, v_ref, pw_ref, ps_ref, sn_ref, ws_ref, bt_ref, dm_ref,
             dproj_ref, dpw_ref, dps_ref, dsn_ref, dws_ref, dbt_ref, pad_ref):
        b = pl.program_id(0)

        @pl.when(b == 0)
        def _():
            dpw_ref[...] = jnp.zeros_like(dpw_ref)
            dps_ref[...] = jnp.zeros_like(dps_ref)
            dsn_ref[...] = jnp.zeros_like(dsn_ref)
            dws_ref[...] = jnp.zeros_like(dws_ref)
            dbt_ref[...] = jnp.zeros_like(dbt_ref)

        _zero_halo(pad_ref, s)
        for g, r in enumerate(POOL_RADII):
            cols = slice(g * POOL_GROUP, (g + 1) * POOL_GROUP)
            pg = p_ref[:, cols].astype(F32)
            cnt = _count(r, s)
            pooled = _window(pad_ref, pg, r, s) / cnt - pg
            c_pre = _dot(pooled, pw_ref[g])
            dc = dm_ref[:, cols]
            dps_ref[:, cols] += jnp.sum(dc * c_pre, axis=0, keepdims=True)
            dcp = dc * ps_ref[:, cols]
            dpw_ref[g] += _dot(pooled, dcp, "tn")
            dpooled = _dot(dcp, pw_ref[g], "nt")
            dproj_ref[:, cols] = (_window(pad_ref, dpooled / cnt, r, s) - dpooled).astype(dproj_ref.dtype)
        norm = sn_ref[...]
        bt = bt_ref[...]

        def chunk(n, carry):
            rows = pl.ds(pl.multiple_of(n * SGU_CHUNK, SGU_CHUNK), SGU_CHUNK)
            uu, vv, gu, r, xhat, vn, mixed = _sgu_chunk(u_ref, v_ref, norm, ws_ref, bt, rows)
            dd = dm_ref[rows, HALF:D_MODEL]
            dgu, dvn = [], []
            for g in range(N_GROUPS):
                cols = slice(g * SGU_GROUP, (g + 1) * SGU_GROUP)
                dgu.append(dd[:, cols] * mixed[g])
                dmx = dd[:, cols] * gu[:, cols]
                dbt_ref[:, g:g + 1] += jnp.sum(dmx, axis=-1, keepdims=True)
                dws_ref[g] += _dot(dmx, vn[:, cols], "nt")
                dvn.append(_dot(ws_ref[g], dmx, "tn"))
            dgu = jnp.concatenate(dgu, axis=1)
            dvn = jnp.concatenate(dvn, axis=1)
            dsn_ref[...] += jnp.sum(dvn * xhat, axis=0, keepdims=True)
            dgx = dvn * norm
            m = jnp.mean(dgx * xhat, axis=-1, keepdims=True)
            dgv = r * (dgx - xhat * m)
            dproj_ref[rows, HALF:2 * HALF] = (dgu * _gelu_grad(uu)).astype(dproj_ref.dtype)
            dproj_ref[rows, 2 * HALF:3 * HALF] = (dgv * _gelu_grad(vv)).astype(dproj_ref.dtype)
            return carry

        lax.fori_loop(0, nchunk, chunk, 0)

    def col(j):
        return pl.BlockSpec((s, HALF), lambda b: (b, j))

    def whole(a):
        return pl.BlockSpec(a.shape, lambda b: (0,) * a.ndim)

    consts = [p["pool_w"], p["pool_scale"].reshape(1, HALF), p["sgu_norm"].reshape(1, HALF),
              p["sgu_w"], p["sgu_b"].T]
    gshapes = [jax.ShapeDtypeStruct(a.shape, F32) for a in consts]
    dproj, dpw, dps, dsn, dws, dbt = pl.pallas_call(
        body, name=tag + "_dcore", grid=(nb,),
        in_specs=[col(0), col(1), col(2)] + [whole(a) for a in consts]
        + [pl.BlockSpec((s, D_MODEL), lambda b: (b, 0))],
        out_specs=[pl.BlockSpec((s, 3 * HALF), lambda b: (b, 0))] + [whole(a) for a in consts],
        out_shape=[jax.ShapeDtypeStruct((t, 3 * HALF), _ACT)] + gshapes,
        scratch_shapes=[pltpu.VMEM((s + 2 * HALO, POOL_GROUP), F32)],
        compiler_params=_params("arbitrary"),
    )(proj, proj, proj, *consts, d_mix)
    return dproj, dict(pool_w=dpw, pool_scale=dps.reshape(HALF), sgu_norm=dsn.reshape(HALF), sgu_w=dws, sgu_b=dbt.T)


def _odd_fwd(tag, x, p, nb, s):
    h = _rmsnorm_fwd(tag + "_norm", x, p["norm"])
    proj = _proj_in(tag, h, p["w_in"])
    mix = _odd_core_fwd(tag, proj, p, nb, s)
    x_out = _proj_out(tag, x, [mix], p["w_out"])
    return x_out, (x, h, proj, mix)


def _odd_bwd(tag, dxo, dxo_lo, saved, p, nb, s, mid, ready):
    x, h, proj, mix = saved
    d_mix, dw_out = _proj_out_bwd(tag, dxo_lo, [mix], p["w_out"])
    p = dict(p, pool_scale=p["pool_scale"] + mid(d_mix))
    dproj, grads = _odd_core_bwd(tag, proj, p, d_mix, nb, s)
    dx, dx_lo, dnorm = _proj_in_bwd(tag, h, [dproj], p["w_in"], x, dxo, p["norm"],
                                    lambda dw_in: ready(dict(w_in=dw_in, w_out=dw_out), dw_in))
    grads.update(norm=dnorm)
    return dx, dx_lo, grads


def _local_step(x3, target3, depth, weights_of, final_norm, mid, grads_ready, small_done):
    nb, s, d = x3.shape
    t = nb * s
    x = x3.reshape(t, d)
    target = target3.reshape(t, d)
    cos, sin = _rope_tables(s)
    saved, ws = [], []
    for l in range(depth):
        w1 = weights_of(l, "ffn1", x)
        x, s1 = _ffn_fwd(f"l{l}_ffn1", x, w1["norm"], w1["w_in4"], w1["w_out"], _ffn_tiles(l, 1))
        wm = weights_of(l, "mix", x)
        if l % 2 == 0:
            x, s2 = _even_fwd(f"l{l}_ev", x, wm, cos, sin, nb, s)
        else:
            x, s2 = _odd_fwd(f"l{l}_od", x, wm, nb, s)
        w2 = weights_of(l, "ffn2", x)
        x, s3 = _ffn_fwd(f"l{l}_ffn2", x, w2["norm"], w2["w_in4"], w2["w_out"], _ffn_tiles(l, 2))
        saved.append((s1, s2, s3))
        ws.append((w1, wm, w2))
    loss, dx, dx_lo, dfinal = _final_loss("final_loss", x, final_norm, target)
    for l in reversed(range(depth)):
        s1, s2, s3 = saved[l]
        w1, wm, w2 = ws[l]

        def ready(block):
            return lambda grads, a: grads_ready(l, block, grads, a)

        dx, dx_lo, dn = _ffn_bwd(f"l{l}_ffn2", dx, dx_lo, s3, w2["norm"], w2["w_in4"], w2["w_out"], _ffn_tiles(l, 2), mid,
                                 ready("ffn2"))
        small_done(l, "ffn2", dict(norm=dn))
        if l % 2 == 0:
            dx, dx_lo, gm = _even_bwd(f"l{l}_ev", dx, dx_lo, s2, wm, cos, sin, nb, s, mid, ready("mix"))
        else:
            dx, dx_lo, gm = _odd_bwd(f"l{l}_od", dx, dx_lo, s2, wm, nb, s, mid, ready("mix"))
        small_done(l, "mix", gm)
        dx, dx_lo, dn = _ffn_bwd(f"l{l}_ffn1", dx, dx_lo, s1, w1["norm"], w1["w_in4"], w1["w_out"], _ffn_tiles(l, 1), mid,
                                 ready("ffn1"))
        small_done(l, "ffn1", dict(norm=dn))
    return loss, dx.reshape(nb, s, d), dfinal


_HBM = pl.BlockSpec(memory_space=pltpu.HBM)


def _place():
    x, y, c = lax.axis_index("x"), lax.axis_index("y"), lax.axis_index("c")
    chips = [(1 - x, y), (x, 1 - y), (1 - x, 1 - y)]
    return x, y, c, chips


def _remote(src, dst, send_sem, recv_sem, to):
    return pltpu.make_async_remote_copy(src_ref=src, dst_ref=dst, send_sem=send_sem, recv_sem=recv_sem,
                                        device_id=to, device_id_type=_MESH)


def _gather_shards(arrs, small):
    n = len(arrs)
    own = 6

    def body(*refs):
        ins, sm_in = refs[:n], refs[n]
        outs, sm_out = refs[n + 1:2 * n + 1], refs[2 * n + 1]
        send, recv = refs[2 * n + 2:]
        x, y, c, chips = _place()
        k = 2 * x + y
        sib = (x, y, 1 - c)
        started = []
        for a in range(n + 1):
            src, dst = (ins[a], outs[a]) if a < n else (sm_in, sm_out)
            cp = _remote(src, dst.at[k], send.at[a, own], recv.at[a, own], sib)
            cp.start()
            started.append(cp)
            if a < n:
                h = src.shape[0] // 2
                mine = pl.ds(c * h, h)
                src_part, dst_part = src.at[mine], dst.at[k, mine]
            else:
                src_part, dst_part = src, dst.at[k]
            for j, chip in enumerate(chips):
                cp = _remote(src_part, dst_part, send.at[a, j], recv.at[a, j], (*chip, c))
                cp.start()
                started.append(cp)
        for a in range(n):
            h = ins[a].shape[0] // 2
            mine = pl.ds(c * h, h)
            for j, (px, py) in enumerate(chips):
                landed = outs[a].at[2 * px + py, mine]
                _remote(landed, landed, send.at[a, j], recv.at[a, j], (px, py, c)).wait_recv()
                cp = _remote(landed, landed, send.at[a, 3 + j], recv.at[a, 3 + j], sib)
                cp.start()
                started.append(cp)
        for a in range(n):
            h = ins[a].shape[0] // 2
            other = pl.ds((1 - c) * h, h)
            for j, (px, py) in enumerate(chips):
                passed = outs[a].at[2 * px + py, other]
                _remote(passed, passed, send.at[a, 3 + j], recv.at[a, 3 + j], sib).wait_recv()
        for j, (px, py) in enumerate(chips):
            landed = sm_out.at[2 * px + py]
            _remote(landed, landed, send.at[n, j], recv.at[n, j], (px, py, c)).wait_recv()
        for a in range(n + 1):
            filled = (outs[a] if a < n else sm_out).at[k]
            _remote(filled, filled, send.at[a, own], recv.at[a, own], sib).wait_recv()
        for cp in started:
            cp.wait_send()

    outs = pl.pallas_call(
        body, name="gather_shards",
        in_specs=[_HBM] * (n + 1), out_specs=[_HBM] * (n + 1),
        out_shape=[jax.ShapeDtypeStruct((N_CHIPS,) + a.shape, a.dtype) for a in list(arrs) + [small]],
        scratch_shapes=[pltpu.SemaphoreType.DMA((n + 1, 7)), pltpu.SemaphoreType.DMA((n + 1, 7))],
    )(*arrs, small)
    return outs[:n], outs[n]


_SEM = pl.BlockSpec(memory_space=pltpu.SEMAPHORE)
_EFFECT = pltpu.SideEffectType.DATAFLOW_SIDE_EFFECTING


def _gather_plan(i, src, land, k, c, chips, sib):
    mine = pl.ds(c * (src.shape[0] // 2), src.shape[0] // 2)
    plan = [(src.at[mine], land.at[k, mine], (px, py, c), land.at[2 * px + py, mine]) for px, py in chips]
    return plan + [(src, land.at[k], sib, land.at[k])]


def _pass_plan(i, src, land, k, c, chips, sib):
    h = src.shape[1] // 2
    mine, theirs = pl.ds(c * h, h), pl.ds((1 - c) * h, h)
    return [(src.at[2 * px + py, mine], land.at[2 * px + py, mine], sib, land.at[2 * px + py, theirs]) for px, py in chips]


def _swap_plan(i, src, land, k, c, chips, sib):
    h = src.shape[1] // 2
    return [(src.at[:, pl.ds((1 - c) * h, h)], land, sib, land)]


def _scatter_plan(i, src, land, k, c, chips, sib):
    return [(src.at[2 * px + py], land.at[k], (px, py, c), land.at[2 * px + py]) for px, py in chips]


def _join_plan(layers):
    def plan(i, src, land, k, c, chips, sib):
        h = src.shape[1] // 2
        mine, theirs = pl.ds(c * h, h), pl.ds((1 - c) * h, h)
        return [(src.at[layers[i], mine], land.at[layers[i], mine], sib, land.at[layers[i], theirs])]
    return plan


def _split_start(name, plan, ncopy, srcs, after, land_shapes=None):
    n = len(srcs)
    if land_shapes is None:
        land_shapes = [(N_CHIPS,) + a.shape[-2:] for a in srcs]
    in_place = land_shapes == "self"
    lands = [] if in_place else [pltpu.with_memory_space_constraint(lax.empty(shape, a.dtype), pltpu.HBM)
                                 for shape, a in zip(land_shapes, srcs)]
    nbuf = n + len(lands)

    def body(*refs):
        src_refs = refs[1:1 + n]
        land_refs = src_refs if in_place else refs[1 + n:1 + nbuf]
        send, recv, token = refs[1 + nbuf], refs[2 + nbuf], refs[-1]
        x, y, c, chips = _place()
        for i in range(n):
            for j, (src, dst, peer, _) in enumerate(plan(i, src_refs[i], land_refs[i], 2 * x + y, c, chips, (x, y, 1 - c))):
                _remote(src, dst, send.at[i * ncopy + j], recv.at[i * ncopy + j], peer).start()
        token[...] = jnp.zeros_like(token)

    outs = pl.pallas_call(
        body, name=name,
        in_specs=[_ANY] + [_HBM] * nbuf,
        out_specs=[_SEM, _SEM] + [_HBM] * nbuf + [_VMEM],
        out_shape=[pltpu.SemaphoreType.DMA((n * ncopy,)), pltpu.SemaphoreType.DMA((n * ncopy,))]
        + [pltpu.HBM(a.shape, a.dtype) for a in list(srcs) + lands] + [jax.ShapeDtypeStruct((8, LANES), F32)],
        input_output_aliases={1 + i: 2 + i for i in range(nbuf)},
        compiler_params=pltpu.CompilerParams(has_side_effects=_EFFECT),
    )(after, *[pltpu.with_memory_space_constraint(a, pltpu.HBM) for a in srcs], *lands)
    return outs[0], outs[1], outs[2:2 + n], None if in_place else outs[2 + n:2 + nbuf], outs[-1]


def _split_wait(name, plan, started, after):
    send, recv, srcs, lands = started
    n = len(srcs)
    ncopy = send.shape[0] // n
    in_place = lands is None
    bufs = list(srcs) + ([] if in_place else list(lands))
    nbuf = len(bufs)

    def body(*refs):
        src_refs = refs[:n]
        land_refs = src_refs if in_place else refs[n:nbuf]
        send, recv = refs[nbuf], refs[nbuf + 1]
        x, y, c, chips = _place()
        for i in range(n):
            for j, (src, _, peer, landed) in enumerate(plan(i, src_refs[i], land_refs[i], 2 * x + y, c, chips, (x, y, 1 - c))):
                cp = _remote(src, landed, send.at[i * ncopy + j], recv.at[i * ncopy + j], peer)
                cp.wait_send()
                cp.wait_recv()

    outs = pl.pallas_call(
        body, name=name,
        in_specs=[_HBM] * nbuf + [_SEM, _SEM, _ANY],
        out_specs=[_HBM] * nbuf,
        out_shape=[pltpu.HBM(a.shape, a.dtype) for a in bufs],
        input_output_aliases={i: i for i in range(nbuf)},
        compiler_params=pltpu.CompilerParams(has_side_effects=_EFFECT),
    )(*bufs, send, recv, after)
    return outs[:n], outs[:n] if in_place else outs[n:]


def _allreduce_small(buf, after):
    rows = buf.shape[0]
    piece = rows // N_DEV

    def body(in_ref, after_ref, out_ref, land_ref, send, recv):
        x, y, c, _ = _place()
        me = 4 * x + 2 * y + c
        peers = [(1 - x if r & 4 else x, 1 - y if r & 2 else y, 1 - c if r & 1 else c) for r in range(1, N_DEV)]

        def rows_of(dev):
            return pl.ds(pl.multiple_of(dev * piece, 8), piece)

        first, second = [], []
        for r, (px, py, pc) in enumerate(peers):
            cp = _remote(in_ref.at[rows_of(4 * px + 2 * py + pc)], land_ref.at[me], send.at[0, r], recv.at[0, r], (px, py, pc))
            cp.start()
            first.append(cp)
        land_ref[me] = in_ref[rows_of(me), :]
        for r, (px, py, pc) in enumerate(peers):
            landed = land_ref.at[4 * px + 2 * py + pc]
            _remote(landed, landed, send.at[0, r], recv.at[0, r], (px, py, pc)).wait_recv()
        acc = land_ref[0]
        for d in range(1, N_DEV):
            acc = acc + land_ref[d]
        out_ref[rows_of(me), :] = acc
        for r, peer in enumerate(peers):
            cp = _remote(out_ref.at[rows_of(me)], out_ref.at[rows_of(me)], send.at[1, r], recv.at[1, r], peer)
            cp.start()
            second.append(cp)
        for r, (px, py, pc) in enumerate(peers):
            landed = out_ref.at[rows_of(4 * px + 2 * py + pc)]
            _remote(landed, landed, send.at[1, r], recv.at[1, r], (px, py, pc)).wait_recv()
        for cp in first + second:
            cp.wait_send()

    return pl.pallas_call(
        body, name="allreduce_small",
        in_specs=[_VMEM, _ANY], out_specs=_VMEM,
        out_shape=jax.ShapeDtypeStruct(buf.shape, F32),
        scratch_shapes=[pltpu.VMEM((N_DEV, piece, LANES), F32), pltpu.SemaphoreType.DMA((2, N_DEV - 1)),
                        pltpu.SemaphoreType.DMA((2, N_DEV - 1))],
        compiler_params=pltpu.CompilerParams(vmem_limit_bytes=_VMEM_LIMIT),
    )(buf, after)


def _div_tile(n, cap, mult):
    best = None
    for d in range(mult, min(n, cap) + 1, mult):
        if n % d == 0:
            best = d
    return best if best is not None else n


def _add_sibling(name, grad, got, c):
    nk, hr, cc = got.shape
    tr = _div_tile(hr, 512, 16)
    nt = hr // tr

    def body(c_ref, g_ref, o_ref, s_ref):
        s_ref[...] = (g_ref[...].astype(F32) + o_ref[...].astype(F32)).astype(s_ref.dtype)

    blk = (None, tr, cc)
    return pl.pallas_call(
        body, name=name,
        grid_spec=pltpu.PrefetchScalarGridSpec(
            num_scalar_prefetch=1, grid=(nk, nt),
            in_specs=[pl.BlockSpec(blk, lambda i, q, c_ref: (i, c_ref[0] * nt + q, 0)),
                      pl.BlockSpec(blk, lambda i, q, c_ref: (i, q, 0))],
            out_specs=pl.BlockSpec(blk, lambda i, q, c_ref: (i, q, 0))),
        out_shape=jax.ShapeDtypeStruct(got.shape, got.dtype),
        compiler_params=_params("parallel", "parallel"),
    )(c, grad, got)


def _add_chips(name, mine, got, place, buf, l):
    nk, hr, cc = got.shape
    tr = _div_tile(hr, 512, 16)
    nt = hr // tr

    def body(*refs):
        acc = refs[1][...].astype(F32)
        for q in range(1, nk):
            acc = acc + refs[1 + q][...].astype(F32)
        refs[2 + nk][...] = acc

    def part(q):
        return pl.BlockSpec((None, tr, cc), lambda i, p_ref: ((p_ref[0] + q) % nk, i, 0))

    return pl.pallas_call(
        body, name=name,
        grid_spec=pltpu.PrefetchScalarGridSpec(
            num_scalar_prefetch=1, grid=(nt,),
            in_specs=[part(q) for q in range(nk)] + [_ANY],
            out_specs=pl.BlockSpec((None, tr, cc), lambda i, p_ref: (l, p_ref[1] * nt + i, 0))),
        out_shape=jax.ShapeDtypeStruct(buf.shape, F32),
        input_output_aliases={1 + nk: 0},
        compiler_params=_params("parallel"),
    )(place, mine, *([got] * (nk - 1)), buf)


def _adamw(name, w, g, m, v, after=None):
    shape = w.shape
    cols = shape[-1]
    rows = w.size // cols
    tr = rows if rows * cols <= 2 ** 18 else _div_tile(rows, max(8, 2 ** 18 // cols), 8)
    c1 = 1.0 - ADAM_B1 ** ADAM_STEP
    c2 = 1.0 - ADAM_B2 ** ADAM_STEP
    extra = [] if after is None else [after]

    def body(*refs):
        w_ref, g_ref, m_ref, v_ref = refs[:4]
        d_ref, mo_ref, vo_ref, go_ref = refs[4 + len(extra):]
        gg = g_ref[...]
        mn = ADAM_B1 * m_ref[...] + (1.0 - ADAM_B1) * gg
        vn = ADAM_B2 * v_ref[...] + (1.0 - ADAM_B2) * (gg * gg)
        d_ref[...] = -ADAM_LR * ((mn / c1) / (jnp.sqrt(vn / c2) + ADAM_EPS) + ADAM_WD * w_ref[...])
        mo_ref[...] = mn
        vo_ref[...] = vn
        go_ref[...] = gg

    blk = pl.BlockSpec((tr, cols), lambda i: (i, 0))
    sds = jax.ShapeDtypeStruct((rows, cols), F32)
    outs = pl.pallas_call(
        body, name=name, grid=(rows // tr,),
        in_specs=[blk] * 4 + [_ANY] * len(extra), out_specs=(blk,) * 4, out_shape=(sds,) * 4,
        compiler_params=_params("parallel"),
    )(*[a.reshape(rows, cols) for a in (w, g, m, v)], *extra)
    return [o.reshape(shape) for o in outs]


_WEIGHTS = ["ffn1_norm", "ffn1_w_in", "ffn1_w_out", "mix_norm", "ffn2_norm", "ffn2_w_in", "ffn2_w_out",
            "ev_w_in", "ev_conv_w", "ev_q_norm", "ev_k_norm", "ev_w_out", "od_w_in", "od_pool_w",
            "od_pool_scale", "od_sgu_norm", "od_sgu_w", "od_sgu_b", "od_w_out", "final_norm"]
_BIG = ["ffn1_w_in", "ffn1_w_out", "ffn2_w_in", "ffn2_w_out", "ev_w_in", "ev_w_out", "od_w_in", "od_w_out"]
_SMALL_SHARDED = ["ev_conv_w", "od_pool_scale", "od_sgu_norm"]


def _pad_rows(a, mult=8):
    pad = (-a.shape[0]) % mult
    return a if pad == 0 else jnp.concatenate([a, jnp.zeros((pad,) + a.shape[1:], a.dtype)], axis=0)


def _join_cols(g):
    return g.transpose(1, 0, 2).reshape(g.shape[1], N_CHIPS * g.shape[2])


def _split_cols(w):
    return w.reshape(w.shape[0], N_CHIPS, w.shape[1] // N_CHIPS).transpose(1, 0, 2)


def kernel(x, ffn1_norm, ffn1_w_in, ffn1_w_out, mix_norm, ffn2_norm, ffn2_w_in, ffn2_w_out, ev_w_in, ev_conv_w,
           ev_q_norm, ev_k_norm, ev_w_out, od_w_in, od_pool_w, od_pool_scale, od_sgu_norm, od_sgu_w, od_sgu_b,
           od_w_out, final_norm, loss_target, m_ffn1_norm, m_ffn1_w_in, m_ffn1_w_out, m_mix_norm, m_ffn2_norm,
           m_ffn2_w_in, m_ffn2_w_out, m_ev_w_in, m_ev_conv_w, m_ev_q_norm, m_ev_k_norm, m_ev_w_out, m_od_w_in,
           m_od_pool_w, m_od_pool_scale, m_od_sgu_norm, m_od_sgu_w, m_od_sgu_b, m_od_w_out, m_final_norm, v_ffn1_norm,
           v_ffn1_w_in, v_ffn1_w_out, v_mix_norm, v_ffn2_norm, v_ffn2_w_in, v_ffn2_w_out, v_ev_w_in, v_ev_conv_w,
           v_ev_q_norm, v_ev_k_norm, v_ev_w_out, v_od_w_in, v_od_pool_w, v_od_pool_scale, v_od_sgu_norm, v_od_sgu_w,
           v_od_sgu_b, v_od_w_out, v_final_norm):
    return _step(x, ffn1_norm, ffn1_w_in, ffn1_w_out, mix_norm, ffn2_norm, ffn2_w_in, ffn2_w_out, ev_w_in, ev_conv_w,
                 ev_q_norm, ev_k_norm, ev_w_out, od_w_in, od_pool_w, od_pool_scale, od_sgu_norm, od_sgu_w, od_sgu_b,
                 od_w_out, final_norm, loss_target, m_ffn1_norm, m_ffn1_w_in, m_ffn1_w_out, m_mix_norm, m_ffn2_norm,
                 m_ffn2_w_in, m_ffn2_w_out, m_ev_w_in, m_ev_conv_w, m_ev_q_norm, m_ev_k_norm, m_ev_w_out, m_od_w_in,
                 m_od_pool_w, m_od_pool_scale, m_od_sgu_norm, m_od_sgu_w, m_od_sgu_b, m_od_w_out, m_final_norm,
                 v_ffn1_norm, v_ffn1_w_in, v_ffn1_w_out, v_mix_norm, v_ffn2_norm, v_ffn2_w_in, v_ffn2_w_out,
                 v_ev_w_in, v_ev_conv_w, v_ev_q_norm, v_ev_k_norm, v_ev_w_out, v_od_w_in, v_od_pool_w,
                 v_od_pool_scale, v_od_sgu_norm, v_od_sgu_w, v_od_sgu_b, v_od_w_out, v_final_norm)


def _step(*args):
    nw = len(_WEIGHTS)
    x = args[0]
    w = dict(zip(_WEIGHTS, args[1:1 + nw]))
    target = args[1 + nw]
    m = dict(zip(_WEIGHTS, args[2 + nw:2 + 2 * nw]))
    v = dict(zip(_WEIGHTS, args[2 + 2 * nw:2 + 3 * nw]))
    depth = w["ffn1_norm"].shape[0]
    n_even, n_odd = w["ev_w_in"].shape[0], w["od_w_in"].shape[0]
    chip = 2 * lax.axis_index("x") + lax.axis_index("y")
    place = jnp.stack([chip, lax.axis_index("c")]).astype(jnp.int32)
    core = place[1:2]

    def sharded(l, block):
        if block == "mix":
            block = "ev" if l % 2 == 0 else "od"
            return [(block + "_w_in", l // 2), (block + "_w_out", l // 2)]
        return [(block + "_w_in", l), (block + "_w_out", l)]

    def shards(group, zero):
        return [(w[n][i] + zero).astype(_ACT) for l, block in group for n, i in sharded(l, block)]

    later = ([[(0, "ffn1")], [(0, "mix"), (0, "ffn2")]]
             + [[(l, "ffn1"), (l, "mix"), (l, "ffn2")] for l in range(1, depth)])
    small_rows = [w["ev_conv_w"].reshape(3 * n_even, LANES), w["od_pool_scale"], w["od_sgu_norm"]]
    _, small = _gather_shards([], _pad_rows(jnp.concatenate(small_rows, axis=0)))
    conv_w = small[:, :3 * n_even].reshape(N_CHIPS, n_even, 3, LANES).transpose(1, 2, 0, 3).reshape(n_even, 3, CONV_WIDTH)
    pool_scale = small[:, 3 * n_even:3 * n_even + n_odd].transpose(1, 0, 2).reshape(n_odd, HALF)
    sgu_norm = small[:, 3 * n_even + n_odd:3 * n_even + 2 * n_odd].transpose(1, 0, 2).reshape(n_odd, HALF)
    gathering, after, zero = [], small, 0.0
    for i, group in enumerate(later):
        gathering.append(_split_start(f"gather_start{i}", _gather_plan, N_CHIPS, shards(group, zero), after))
        after = gathering[-1][4]
        zero = after[0, 0]
    gathered = {}

    def rows(g):
        return g.reshape(N_CHIPS * g.shape[1], g.shape[2])

    passing = {}

    def fetch(i, x_in):
        got = _split_wait(f"gather_wait{i}", _gather_plan, gathering[i][:4], x_in)[1]
        passing[i] = _split_start(f"pass_start{i}", _pass_plan, N_CHIPS - 1, got, x_in, "self")
        return passing[i][4][0, 0]

    def weights_of(l, block, x_in):
        zero = after[0, 0] if (l, block) == (0, "ffn1") else 0.0
        if (l, block) not in gathered:
            i = next(i for i, group in enumerate(later) if (l, block) in group)
            if i not in passing:
                zero = zero + fetch(i, x_in)
            got = _split_wait(f"pass_wait{i}", _pass_plan, passing.pop(i)[:4], x_in)[0]
            for n, key in enumerate(later[i]):
                gathered[key] = got[2 * n:2 * n + 2]
        if block == "ffn2" and l + 1 < depth:
            zero = zero + fetch(next(i for i, group in enumerate(later) if (l + 1, "ffn1") in group), x_in)
        w_in, w_out = gathered[(l, block)]
        if block != "mix":
            return dict(norm=w[block + "_norm"][l] + zero, w_in4=w_in, w_out=rows(w_out))
        j = l // 2
        if l % 2 == 0:
            mix = dict(conv_w=conv_w[j], q_gain=w["ev_q_norm"][j], k_gain=w["ev_k_norm"][j])
        else:
            mix = dict(pool_w=w["od_pool_w"][j], pool_scale=pool_scale[j], sgu_norm=sgu_norm[j],
                       sgu_w=w["od_sgu_w"][j], sgu_b=w["od_sgu_b"][j])
        return dict(mix, norm=w["mix_norm"][l] + zero, w_in=_join_cols(w_in), w_out=rows(w_out))

    def by_chip(dw):
        return dw.reshape(N_CHIPS, dw.shape[0] // N_CHIPS, dw.shape[1])

    bufs = {n: lax.empty(w[n].shape, F32) for n in _BIG}
    small_grads = {n: [None] * w[n].shape[0] for n in _WEIGHTS if n not in _BIG and n != "final_norm"}
    swapping, scattering, joining, group = [], [], [], []

    def finish_swap(after):
        tag, names, started = swapping.pop()
        local, from_sibling = _split_wait(f"swap_wait{tag}", _swap_plan, started[:4], after)
        halves = [_add_sibling(f"add_sibling{tag}_{n}", a, b, core) for (n, _), a, b in zip(names, local, from_sibling)]
        scattering.append((tag, names, _split_start(f"scatter_start{tag}", _scatter_plan, N_CHIPS - 1, halves, after)))
        return scattering[-1][2][4][0, 0]

    def finish_scatter(after):
        tag, names, started = scattering.pop(0)
        halves, got = _split_wait(f"scatter_wait{tag}", _scatter_plan, started[:4], after)
        for i, (n, j) in enumerate(names):
            bufs[n] = _add_chips(f"add_chips{tag}_{n}", halves[i], got[i], place, bufs[n], j)
        layers = [j for _, j in names]
        started = _split_start(f"join_start{tag}", _join_plan(layers), 1, [bufs[n] for n, _ in names], after, "self")
        for (n, _), b in zip(names, started[2]):
            bufs[n] = b
        joining.append((tag, names, layers, started))
        return started[4]

    def finish_join(after):
        tag, names, layers, started = joining.pop(0)
        joined = _split_wait(f"join_wait{tag}", _join_plan(layers), (started[0], started[1], [bufs[n] for n, _ in names], None),
                             after)[0]
        for (n, _), b in zip(names, joined):
            bufs[n] = b

    def small_done(l, block, g):
        if block == "mix":
            renamed = (dict(conv_w="ev_conv_w", q_gain="ev_q_norm", k_gain="ev_k_norm") if l % 2 == 0 else
                       dict(pool_w="od_pool_w", pool_scale="od_pool_scale", sgu_norm="od_sgu_norm", sgu_w="od_sgu_w",
                            sgu_b="od_sgu_b"))
            for key, n in renamed.items():
                small_grads[n][l // 2] = g[key]
        small_grads[block + "_norm"][l] = g["norm"]

    def mid(a):
        zero = 0.0
        if swapping:
            if scattering:
                zero = zero + finish_scatter(a)[0, 0]
            zero = zero + finish_swap(a)
        return zero

    def grads_ready(l, block, g, a):
        local = [_split_cols(g["w_in"]) if block == "mix" else g["w_in4"], by_chip(g["w_out"])]
        group.extend(zip(sharded(l, block), local))
        if block == "ffn2" or (block == "mix" and l > 0):
            return 0.0
        tag = f"{l}_{block}"
        names, local = [n for n, _ in group], [b for _, b in group]
        group.clear()
        shapes = [(N_CHIPS, b.shape[1] // 2, b.shape[2]) for b in local]
        swapping.append((tag, names, _split_start(f"swap_start{tag}", _swap_plan, 1, local, core, shapes)))
        return swapping[-1][2][4][0, 0]

    loss_part, grad_x, dfinal = _local_step(x, target, depth, weights_of, w["final_norm"], mid, grads_ready, small_done)
    loss = lax.psum(loss_part, ("x", "y", "c"))

    grads, updates = {}, {}

    def update(n, after):
        updates[n] = _adamw("adamw_" + n, w[n], grads[n] if n in grads else bufs[n], m[n], v[n], after)
        return updates[n][1]

    finish_swap(grad_x)
    behind = scattering[-1][2][4]
    while len(joining) > 0:
        finish_join(behind)
    behind = finish_scatter(behind)
    for n in ("od_w_in", "od_w_out"):
        behind = update(n, behind)
    finish_join(behind)
    for n in ("ffn2_w_in", "ffn2_w_out", "ev_w_in", "ev_w_out"):
        behind = update(n, behind)
    behind = finish_scatter(behind)
    small_grads = {n: jnp.stack(parts) for n, parts in small_grads.items()}
    small_grads["final_norm"] = dfinal
    names = list(small_grads)
    flat = jnp.concatenate([small_grads[n].reshape(-1) for n in names])
    total = flat.shape[0]
    flat = jnp.concatenate([flat, jnp.zeros((-total) % (N_DEV * 8 * LANES), F32)])
    summed = _allreduce_small(flat.reshape(-1, LANES), behind).reshape(-1)
    finish_join(summed)
    for n in ("ffn1_w_in", "ffn1_w_out"):
        behind = update(n, behind)
    off = 0
    for n in names:
        size = small_grads[n].size
        full_grad = summed[off:off + size].reshape(small_grads[n].shape)
        off += size
        if n in _SMALL_SHARDED:
            full_grad = lax.dynamic_slice_in_dim(full_grad, chip * LANES, LANES, axis=full_grad.ndim - 1)
        grads[n] = full_grad
    for n in _WEIGHTS:
        if n not in updates:
            behind = update(n, behind)
    return (loss, grad_x, *[updates[n][3] for n in _WEIGHTS], *[updates[n][0] for n in _WEIGHTS],
            *[updates[n][1] for n in _WEIGHTS], *[updates[n][2] for n in _WEIGHTS])
```

```python
import jax
import jax.numpy as jnp
from jax import lax
from jax.experimental import pallas as pl
from jax.experimental.pallas import tpu as pltpu

F32 = jnp.float32
_MXU = jnp.bfloat16
_ACT = jnp.bfloat16

D_MODEL = 1024
GRID_W = 64
HEAD_DIM = 64
N_Q_HEADS = 8
N_KV_HEADS = 2
Q_PER_KV = N_Q_HEADS // N_KV_HEADS
ATTN_WIDTH = N_Q_HEADS * HEAD_DIM
KV_WIDTH = N_KV_HEADS * HEAD_DIM
ROPE_THETA = 10000.0
CONV_WIDTH = D_MODEL // 2
POOL_RADII = (1, 2, 4, 8)
POOL_GROUP = 128
SGU_GROUP = 128
SGU_CHUNK = 128
N_GROUPS = 4
HALF = D_MODEL // 2
EPS = 1e-6
HALO = 8
LANES = 128
N_CHIPS = 4
N_DEV = 8

ADAM_LR = 0.001
ADAM_B1 = 0.9
ADAM_B2 = 0.999
ADAM_EPS = 1e-08
ADAM_WD = 0.01
ADAM_STEP = 10

_VMEM_LIMIT = 56 * 2 ** 20
_MESH = pl.DeviceIdType.MESH
_ANY = pl.BlockSpec(memory_space=pl.ANY)
_VMEM = pl.BlockSpec(memory_space=pltpu.VMEM)

_DN = {
    "nn": (((1,), (0,)), ((), ())),
    "nt": (((1,), (1,)), ((), ())),
    "tn": (((0,), (0,)), ((), ())),
}


def _params(*sem):
    return pltpu.CompilerParams(dimension_semantics=sem, vmem_limit_bytes=_VMEM_LIMIT)


def _tile(n, cap):
    best = None
    d = LANES
    while d <= min(n, cap):
        if n % d == 0:
            best = d
        d += LANES
    return best if best is not None else n


def _dot(a, b, mode="nn"):
    return lax.dot_general(a.astype(_MXU), b.astype(_MXU), _DN[mode], preferred_element_type=F32)


def _cat(*vals):
    vals = [v.astype(_MXU) for v in vals]
    return vals[0] if len(vals) == 1 else jnp.concatenate(vals, axis=1)


def _sigmoid(g):
    return 1.0 / (1.0 + jnp.exp(-g))


def _norm_rows(x, g):
    r = lax.rsqrt(jnp.mean(x * x, axis=-1, keepdims=True) + EPS)
    return (x * r) * g


def _swiglu(g, u):
    return (g * _sigmoid(g)) * u


_GELU_C = 0.7978845608028654


def _gelu(x):
    return 0.5 * x * (1.0 + jnp.tanh(_GELU_C * (x + 0.044715 * (x * x * x))))


def _gelu_grad(x):
    t = jnp.tanh(_GELU_C * (x + 0.044715 * (x * x * x)))
    return 0.5 * (1.0 + t) + 0.5 * x * (1.0 - t * t) * (_GELU_C * (1.0 + 3.0 * 0.044715 * (x * x)))


def _mm(name, grid, mode, a_ops, b_ops, e_ops, out_shape, out_specs, acc_shape, a_fn=_cat, b_fn=_cat, epi=None,
        n_outer=False, m_carried=False, b_pick=None, a_out=False):
    ni, nj, nk = grid
    na, nb, ne = len(a_ops), len(b_ops), len(e_ops)
    multi = isinstance(out_shape, (list, tuple))
    no = len(out_shape) if multi else 1

    def body(*refs):
        a_refs = refs[:na]
        b_refs = refs[na:na + nb]
        e_refs = refs[na + nb:na + nb + ne]
        o_refs = refs[na + nb + ne:na + nb + ne + no]
        a = a_fn(*[r[...] for r in a_refs])
        if a_out:
            o_refs[-1][...] = a.astype(o_refs[-1].dtype)
        if b_pick is None:
            b = b_fn(*[r[...] for r in b_refs])
        else:
            b = b_pick(b_refs, pl.program_id(1), pl.program_id(2))
        p = _dot(a, b, mode)

        def finish(acc):
            if epi is None:
                o_refs[0][...] = acc.astype(o_refs[0].dtype)
            else:
                epi(acc, [r[...] for r in e_refs], o_refs)

        if nk == 1:
            finish(p)
        else:
            acc_ref = refs[-1]
            k = pl.program_id(2)

            @pl.when(k == 0)
            def _():
                acc_ref[...] = p

            @pl.when((k > 0) & (k < nk - 1))
            def _():
                acc_ref[...] += p

            @pl.when(k == nk - 1)
            def _():
                finish(acc_ref[...] + p)

    ops = list(a_ops) + list(b_ops) + list(e_ops)
    if n_outer:
        def flip(spec):
            return pl.BlockSpec(spec.block_shape, lambda j, i, k, f=spec.index_map: f(i, j, k))

        grid = (nj, ni, nk)
        ops = [(a, flip(s)) for a, s in ops]
        out_specs = [flip(s) for s in out_specs] if multi else flip(out_specs)
    return pl.pallas_call(
        body, name=name, grid=grid,
        in_specs=[s for _, s in ops],
        out_specs=out_specs, out_shape=out_shape,
        scratch_shapes=[pltpu.VMEM(acc_shape, F32)] if nk > 1 else [],
        compiler_params=_params(*(("arbitrary",) * 3 if m_carried else ("parallel", "parallel", "arbitrary"))),
    )(*[a for a, _ in ops])


def _whole(a):
    return pl.BlockSpec(a.shape, lambda i, j, k: (0,) * a.ndim, pipeline_mode=pl.Buffered(1))


def _norm_bwd_epi(acc, e, o):
    xf, dres, g = e
    r = lax.rsqrt(jnp.mean(xf * xf, axis=-1, keepdims=True) + EPS)
    xhat = xf * r
    dgx = acc * g
    m = jnp.mean(dgx * xhat, axis=-1, keepdims=True)
    dx = dres + r * (dgx - xhat * m)
    o[0][...] = dx
    o[2][...] = dx.astype(o[2].dtype)
    part = jnp.sum(acc * xhat, axis=0, keepdims=True)
    i = pl.program_id(0)

    @pl.when(i == 0)
    def _():
        o[1][...] = part

    @pl.when(i > 0)
    def _():
        o[1][...] += part


def _norm_bwd_ops(x, dres, gain, tm):
    t, d = x.shape
    row = pl.BlockSpec((tm, d), lambda i, j, k: (i, 0))
    vec = pl.BlockSpec((1, d), lambda i, j, k: (0, 0))
    return ([(x, row), (dres, row), (gain.reshape(1, d), vec)],
            [jax.ShapeDtypeStruct((t, d), F32), jax.ShapeDtypeStruct((1, d), F32), jax.ShapeDtypeStruct((t, d), _ACT)],
            [row, vec, row])


def _rows(t):
    return _tile(t, 512)


def _rmsnorm_fwd(name, x, gain):
    t, d = x.shape
    tr = _rows(t)

    def body(x_ref, g_ref, h_ref):
        xf = x_ref[...]
        r = lax.rsqrt(jnp.mean(xf * xf, axis=-1, keepdims=True) + EPS)
        h_ref[...] = ((xf * r) * g_ref[...]).astype(h_ref.dtype)

    return pl.pallas_call(
        body, name=name, grid=(t // tr,),
        in_specs=[pl.BlockSpec((tr, d), lambda i: (i, 0)), pl.BlockSpec((1, d), lambda i: (0, 0))],
        out_specs=pl.BlockSpec((tr, d), lambda i: (i, 0)),
        out_shape=jax.ShapeDtypeStruct((t, d), _ACT),
        compiler_params=_params("parallel"),
    )(x, gain.reshape(1, d))


def _rmsnorm_bwd(name, dh, x, gain, dres):
    t, d = x.shape
    tr = _rows(t)

    def body(dh_ref, x_ref, g_ref, dres_ref, dx_ref, dg_ref):
        i = pl.program_id(0)
        xf = x_ref[...]
        r = lax.rsqrt(jnp.mean(xf * xf, axis=-1, keepdims=True) + EPS)
        xhat = xf * r
        dy = dh_ref[...].astype(F32)
        dgx = dy * g_ref[...]
        m = jnp.mean(dgx * xhat, axis=-1, keepdims=True)
        dx_ref[...] = dres_ref[...] + r * (dgx - xhat * m)
        part = jnp.sum(dy * xhat, axis=0, keepdims=True)

        @pl.when(i == 0)
        def _():
            dg_ref[...] = part

        @pl.when(i > 0)
        def _():
            dg_ref[...] += part

    row = pl.BlockSpec((tr, d), lambda i: (i, 0))
    vec = pl.BlockSpec((1, d), lambda i: (0, 0))
    dx, dg = pl.pallas_call(
        body, name=name, grid=(t // tr,),
        in_specs=[row, row, vec, row],
        out_specs=(row, vec),
        out_shape=(jax.ShapeDtypeStruct((t, d), F32), jax.ShapeDtypeStruct((1, d), F32)),
        compiler_params=_params("arbitrary"),
    )(dh, x, gain.reshape(1, d), dres)
    return dx, dg.reshape(d)


def _final_loss(name, x, gain, target):
    t, d = x.shape
    tr = _rows(t)

    def body(x_ref, g_ref, t_ref, dx_ref, dg_ref, loss_ref, lo_ref):
        i = pl.program_id(0)
        xf = x_ref[...]
        r = lax.rsqrt(jnp.mean(xf * xf, axis=-1, keepdims=True) + EPS)
        xhat = xf * r
        g = g_ref[...]
        err = xhat * g - t_ref[...]
        lpart = 0.5 * jnp.sum(jnp.mean(err * err, axis=-1, keepdims=True), axis=0, keepdims=True)
        dy = err * (1.0 / d)
        dgx = dy * g
        m = jnp.mean(dgx * xhat, axis=-1, keepdims=True)
        dx = r * (dgx - xhat * m)
        dx_ref[...] = dx
        lo_ref[...] = dx.astype(lo_ref.dtype)
        part = jnp.sum(dy * xhat, axis=0, keepdims=True)
        lrow = jnp.broadcast_to(lpart, (1, LANES))

        @pl.when(i == 0)
        def _():
            dg_ref[...] = part
            loss_ref[...] = lrow

        @pl.when(i > 0)
        def _():
            dg_ref[...] += part
            loss_ref[...] += lrow

    row = pl.BlockSpec((tr, d), lambda i: (i, 0))
    vec = pl.BlockSpec((1, d), lambda i: (0, 0))
    dx, dg, loss, dx_lo = pl.pallas_call(
        body, name=name, grid=(t // tr,),
        in_specs=[row, vec, row],
        out_specs=(row, vec, pl.BlockSpec((1, LANES), lambda i: (0, 0)), row),
        out_shape=(jax.ShapeDtypeStruct((t, d), F32), jax.ShapeDtypeStruct((1, d), F32),
                   jax.ShapeDtypeStruct((1, LANES), F32), jax.ShapeDtypeStruct((t, d), _ACT)),
        compiler_params=_params("arbitrary"),
    )(x, gain.reshape(1, d), target)
    return loss[0, 0], dx, dx_lo, dg.reshape(d)


_FFN_TILES = dict(in_tm=1024, out_tm=512, dact_tm=1024, dwout_tk=1024, dh_tm=512, dwin_tk=2048)


def _ffn_tiles(layer, which):
    return _FFN_TILES


def _ffn_fwd(tag, x, gain, w_in4, w_out, cfg):
    t, d = x.shape
    fs = w_in4.shape[2]
    f = 2 * fs
    tm = _tile(t, cfg["in_tm"])
    gain = gain.reshape(1, d)
    gu, h = _mm(
        tag + "_in", (t // tm, N_CHIPS, 1), "nn",
        [(x, pl.BlockSpec((tm, d), lambda i, j, k: (i, 0))), (gain, pl.BlockSpec((1, d), lambda i, j, k: (0, 0)))],
        [(w_in4, pl.BlockSpec((None, d, fs), lambda i, j, k: (j, 0, 0)))], [],
        [jax.ShapeDtypeStruct((2, t, f), _ACT), jax.ShapeDtypeStruct((t, d), _ACT)],
        [pl.BlockSpec((None, tm, fs), lambda i, j, k: (j // 2, i, j % 2)), pl.BlockSpec((tm, d), lambda i, j, k: (i, 0))],
        None, a_fn=_norm_rows, m_carried=True, a_out=True)
    tm2 = _tile(t, cfg["out_tm"])

    def epi(acc, e, o):
        o[0][...] = e[0] + 0.5 * acc

    x_out = _mm(
        tag + "_out", (t // tm2, 1, 1), "nn",
        [(gu, pl.BlockSpec((None, tm2, f), lambda i, j, k: (0, i, 0))),
         (gu, pl.BlockSpec((None, tm2, f), lambda i, j, k: (1, i, 0)))],
        [(w_out, pl.BlockSpec((f, d), lambda i, j, k: (0, 0)))],
        [(x, pl.BlockSpec((tm2, d), lambda i, j, k: (i, 0)))],
        jax.ShapeDtypeStruct((t, d), F32),
        pl.BlockSpec((tm2, d), lambda i, j, k: (i, 0)), None,
        a_fn=_swiglu, epi=epi)
    return x_out, (x, h, gu)


def _ffn_bwd(tag, dxo, dxo_lo, saved, gain, w_in4, w_out, cfg, mid, ready):
    x, h, gu = saved
    t, d = x.shape
    fs = w_in4.shape[2]
    f = 2 * fs
    tm = _tile(t, cfg["dact_tm"])
    tk = _tile(t, cfg["dwout_tk"])

    def epi_act(acc, e, o):
        g, u = e
        da = (0.5 * acc).astype(g.dtype)
        sig = _sigmoid(g)
        silu = g * sig
        o[0][0] = (da * u * (sig + silu * (1.0 - sig))).astype(o[0].dtype)
        o[0][1] = (da * silu).astype(o[0].dtype)

    dgu = _mm(
        tag + "_dact", (t // tm, 2, 1), "nt",
        [(dxo_lo, pl.BlockSpec((tm, d), lambda i, j, k: (i, 0)))],
        [(w_out, _whole(w_out))],
        [(gu, pl.BlockSpec((None, tm, fs), lambda i, j, k: (0, i, j))),
         (gu, pl.BlockSpec((None, tm, fs), lambda i, j, k: (1, i, j)))],
        jax.ShapeDtypeStruct((2, t, f), _ACT),
        pl.BlockSpec((2, tm, fs), lambda i, j, k: (0, i, j)), None, epi=epi_act,
        b_pick=lambda b, j, k: b[0][pl.ds(pl.multiple_of(j * fs, LANES), fs), :])
    gain = gain + mid(dgu)

    def epi_half(acc, e, o):
        o[0][...] = (0.5 * acc).astype(o[0].dtype)

    dw_out = _mm(
        tag + "_dwout", (2, 1, t // tk), "tn",
        [(gu, pl.BlockSpec((None, tk, fs), lambda i, j, k: (0, k, i))),
         (gu, pl.BlockSpec((None, tk, fs), lambda i, j, k: (1, k, i)))],
        [(dxo_lo, pl.BlockSpec((tk, d), lambda i, j, k: (k, 0)))], [],
        jax.ShapeDtypeStruct((f, d), _ACT),
        pl.BlockSpec((fs, d), lambda i, j, k: (i, 0)), (fs, d),
        a_fn=_swiglu, epi=epi_half)
    tk = _tile(t, cfg["dwin_tk"])
    dw_in4 = _mm(
        tag + "_dwin", (1, N_CHIPS, t // tk), "tn",
        [(h, pl.BlockSpec((tk, d), lambda i, j, k: (k, 0)))],
        [(dgu, pl.BlockSpec((None, tk, fs), lambda i, j, k: (j // 2, k, j % 2)))], [],
        jax.ShapeDtypeStruct((N_CHIPS, d, fs), _ACT),
        pl.BlockSpec((None, d, fs), lambda i, j, k: (j, 0, 0)), (d, fs))
    gain = gain + ready(dict(w_in4=dw_in4, w_out=dw_out), dw_in4)
    tm = _tile(t, cfg["dh_tm"])
    e_ops, shapes, specs = _norm_bwd_ops(x, dxo, gain, tm)
    dx, dgain, dx_lo = _mm(
        tag + "_dh", (t // tm, 1, 2), "nt",
        [(dgu, pl.BlockSpec((None, tm, f), lambda i, j, k: (k, i, 0)))],
        [(w_in4, _whole(w_in4))], e_ops, shapes, specs, (tm, d), epi=_norm_bwd_epi, m_carried=True,
        b_pick=lambda b, j, k: jnp.concatenate([b[0][2 * k], b[0][2 * k + 1]], axis=1))
    return dx, dx_lo, dgain.reshape(d)


_MIX_TILES = dict(tm=1024, dwout_tk=2048, dwin_tk=1024)


def _proj_in(tag, h, w_in):
    t, d = h.shape
    n = w_in.shape[1]
    tm = _tile(t, _MIX_TILES["tm"])
    return _mm(
        tag + "_in", (t // tm, 1, 1), "nn",
        [(h, pl.BlockSpec((tm, d), lambda i, j, k: (i, 0)))],
        [(w_in, pl.BlockSpec((d, n), lambda i, j, k: (0, 0)))], [],
        jax.ShapeDtypeStruct((t, n), _ACT),
        pl.BlockSpec((tm, n), lambda i, j, k: (i, 0)), None)


def _proj_out(tag, x, parts, w_out):
    t, d = x.shape
    tm = _tile(t, _MIX_TILES["tm"])

    def epi(acc, e, o):
        o[0][...] = e[0] + acc

    return _mm(
        tag + "_out", (t // tm, 1, 1), "nn",
        [(p, pl.BlockSpec((tm, p.shape[1]), lambda i, j, k: (i, 0))) for p in parts],
        [(w_out, pl.BlockSpec(w_out.shape, lambda i, j, k: (0, 0)))],
        [(x, pl.BlockSpec((tm, d), lambda i, j, k: (i, 0)))],
        jax.ShapeDtypeStruct((t, d), F32),
        pl.BlockSpec((tm, d), lambda i, j, k: (i, 0)), None, epi=epi)


def _proj_out_bwd(tag, dxo, parts, w_out):
    t, d = dxo.shape
    mix = w_out.shape[0]
    tm = _tile(t, _MIX_TILES["tm"])
    tk = _tile(t, _MIX_TILES["dwout_tk"])
    d_mix = _mm(
        tag + "_dmix", (t // tm, 1, 1), "nt",
        [(dxo, pl.BlockSpec((tm, d), lambda i, j, k: (i, 0)))],
        [(w_out, pl.BlockSpec((mix, d), lambda i, j, k: (0, 0)))], [],
        jax.ShapeDtypeStruct((t, mix), F32),
        pl.BlockSpec((tm, mix), lambda i, j, k: (i, 0)), None)
    dw_out = _mm(
        tag + "_dwout", (1, 1, t // tk), "tn",
        [(p, pl.BlockSpec((tk, p.shape[1]), lambda i, j, k: (k, 0))) for p in parts],
        [(dxo, pl.BlockSpec((tk, d), lambda i, j, k: (k, 0)))], [],
        jax.ShapeDtypeStruct((mix, d), _ACT),
        pl.BlockSpec((mix, d), lambda i, j, k: (0, 0)), (mix, d))
    return d_mix, dw_out


def _proj_in_bwd(tag, h, dparts, w_in, x, dres, gain, ready):
    t, d = h.shape
    n = w_in.shape[1]
    tm = _tile(t, _MIX_TILES["tm"])
    tk = _tile(t, _MIX_TILES["dwin_tk"])
    dw_in = _mm(
        tag + "_dwin", (1, 1, t // tk), "tn",
        [(h, pl.BlockSpec((tk, d), lambda i, j, k: (k, 0)))],
        [(p, pl.BlockSpec((tk, p.shape[1]), lambda i, j, k: (k, 0))) for p in dparts], [],
        jax.ShapeDtypeStruct((d, n), _ACT),
        pl.BlockSpec((d, n), lambda i, j, k: (0, 0)), (d, n))
    e_ops, shapes, specs = _norm_bwd_ops(x, dres, gain + ready(dw_in), tm)
    dx, dgain, dx_lo = _mm(
        tag + "_dh", (t // tm, 1, 1), "nt",
        [(p, pl.BlockSpec((tm, p.shape[1]), lambda i, j, k: (i, 0))) for p in dparts],
        [(w_in, pl.BlockSpec((d, n), lambda i, j, k: (0, 0)))], e_ops, shapes, specs, None,
        epi=_norm_bwd_epi, m_carried=True)
    return dx, dx_lo, dgain.reshape(d)


def _shifted(pad_ref, val, s):
    pad_ref[pl.ds(HALO, s), :] = val
    return pad_ref[pl.ds(HALO - 1, s), :], pad_ref[pl.ds(HALO + 1, s), :]


def _zero_halo(pad_ref, s):
    z = jnp.zeros((HALO, pad_ref.shape[1]), F32)
    pad_ref[pl.ds(0, HALO), :] = z
    pad_ref[pl.ds(HALO + s, HALO), :] = z


def _conv_fwd(tag, proj, conv_w, nb, s):
    t = proj.shape[0]
    ncb = CONV_WIDTH // LANES

    def body(gb_ref, gc_ref, hc_ref, w_ref, a_ref, pad_ref):
        _zero_halo(pad_ref, s)
        cg = gc_ref[...].astype(F32) * hc_ref[...].astype(F32)
        prev, nxt = _shifted(pad_ref, cg, s)
        w = w_ref[...]
        conv = prev * w[0:1, :] + cg * w[1:2, :] + nxt * w[2:3, :]
        a_ref[...] = (gb_ref[...].astype(F32) * conv).astype(a_ref.dtype)

    def col(off):
        return pl.BlockSpec((s, LANES), lambda b, c: (b, off + c))

    return pl.pallas_call(
        body, name=tag + "_conv", grid=(nb, ncb),
        in_specs=[col(0), col(ncb), col(2 * ncb), pl.BlockSpec((3, LANES), lambda b, c: (0, c))],
        out_specs=col(0),
        out_shape=jax.ShapeDtypeStruct((t, CONV_WIDTH), _ACT),
        scratch_shapes=[pltpu.VMEM((s + 2 * HALO, LANES), F32)],
        compiler_params=_params("parallel", "parallel"),
    )(proj, proj, proj, conv_w)


def _conv_bwd(tag, proj, conv_w, d_mix, nb, s):
    t = proj.shape[0]
    ncb = CONV_WIDTH // LANES

    def body(gb_ref, gc_ref, hc_ref, w_ref, da_ref, dgb_ref, dgc_ref, dhc_ref, dw_ref, pad_ref):
        b = pl.program_id(1)
        _zero_halo(pad_ref, s)
        gb = gb_ref[...].astype(F32)
        gc = gc_ref[...].astype(F32)
        hc = hc_ref[...].astype(F32)
        w = w_ref[...]
        da = da_ref[...]
        cg = gc * hc
        prev, nxt = _shifted(pad_ref, cg, s)
        conv = prev * w[0:1, :] + cg * w[1:2, :] + nxt * w[2:3, :]
        dgb_ref[...] = (da * conv).astype(dgb_ref.dtype)
        dconv = da * gb
        dw = jnp.concatenate([
            jnp.sum(dconv * prev, axis=0, keepdims=True),
            jnp.sum(dconv * cg, axis=0, keepdims=True),
            jnp.sum(dconv * nxt, axis=0, keepdims=True)], axis=0)
        dprev, dnxt = _shifted(pad_ref, dconv, s)
        dcg = dnxt * w[0:1, :] + dconv * w[1:2, :] + dprev * w[2:3, :]
        dgc_ref[...] = (dcg * hc).astype(dgc_ref.dtype)
        dhc_ref[...] = (dcg * gc).astype(dhc_ref.dtype)

        @pl.when(b == 0)
        def _():
            dw_ref[...] = dw

        @pl.when(b > 0)
        def _():
            dw_ref[...] += dw

    def col(off):
        return pl.BlockSpec((s, LANES), lambda c, b: (b, off + c))

    wspec = pl.BlockSpec((3, LANES), lambda c, b: (0, c))
    act = jax.ShapeDtypeStruct((t, CONV_WIDTH), _ACT)
    return pl.pallas_call(
        body, name=tag + "_dconv", grid=(ncb, nb),
        in_specs=[col(0), col(ncb), col(2 * ncb), wspec, col(0)],
        out_specs=(col(0), col(0), col(0), wspec),
        out_shape=(act, act, act, jax.ShapeDtypeStruct((3, CONV_WIDTH), F32)),
        scratch_shapes=[pltpu.VMEM((s + 2 * HALO, LANES), F32)],
        compiler_params=_params("parallel", "arbitrary"),
    )(proj, proj, proj, conv_w, d_mix)


def _rope_tables(s):
    rows = s // GRID_W
    r_idx, c_idx = jnp.meshgrid(jnp.arange(rows), jnp.arange(GRID_W), indexing="ij")
    r_idx = r_idx.reshape(-1).astype(F32)
    c_idx = c_idx.reshape(-1).astype(F32)
    n_freq = HEAD_DIM // 4
    inv = ROPE_THETA ** (-jnp.arange(n_freq, dtype=F32) / n_freq)
    ang = jnp.concatenate([r_idx[:, None] * inv, c_idx[:, None] * inv], axis=-1)
    cos = jnp.repeat(jnp.cos(ang), 2, axis=1)
    sin = jnp.repeat(jnp.sin(ang), 2, axis=1)
    sign = jnp.where(jnp.arange(HEAD_DIM) % 2 == 0, -1.0, 1.0).astype(F32)
    return jnp.tile(cos, (1, LANES // HEAD_DIM)), jnp.tile(sin * sign, (1, LANES // HEAD_DIM))


def _head_ones():
    i = jnp.arange(LANES) // HEAD_DIM
    return (i[:, None] == i[None, :]).astype(jnp.bfloat16)


def _head_sum(v, ones):
    outs = []
    for j in range(v.shape[1] // LANES):
        c = v[:, j * LANES:(j + 1) * LANES]
        hi = c.astype(jnp.bfloat16)
        lo = (c - hi.astype(F32)).astype(jnp.bfloat16)
        outs.append(jnp.dot(hi, ones, preferred_element_type=F32) + jnp.dot(lo, ones, preferred_element_type=F32))
    return outs[0] if len(outs) == 1 else jnp.concatenate(outs, axis=1)


def _pair_swap(v):
    outs = []
    for j in range(v.shape[1] // LANES):
        c = v[:, j * LANES:(j + 1) * LANES]
        lane = lax.broadcasted_iota(jnp.int32, c.shape, 1)
        outs.append(jnp.where(lane % 2 == 0, pltpu.roll(c, LANES - 1, 1), pltpu.roll(c, 1, 1)))
    return outs[0] if len(outs) == 1 else jnp.concatenate(outs, axis=1)


def _wide(tab, width):
    return tab if width == LANES else jnp.concatenate([tab] * (width // LANES), axis=1)


_QK_SCALE = HEAD_DIM ** -0.5


def _qk_fwd(tag, proj, q_gain, k_gain, cos, sin, nb, s):
    t = proj.shape[0]
    tr = _tile(s, 512)
    ns = s // tr
    q_off = 3 * CONV_WIDTH // ATTN_WIDTH
    k_off = (3 * CONV_WIDTH + ATTN_WIDTH) // KV_WIDTH

    def body(q_ref, k_ref, qg_ref, kg_ref, cos_ref, sin_ref, ones_ref, qo_ref, ko_ref):
        ones = ones_ref[...]
        for src, g_ref, dst, mult in ((q_ref, qg_ref, qo_ref, _QK_SCALE), (k_ref, kg_ref, ko_ref, 1.0)):
            v = src[...].astype(F32)
            w = v.shape[1]
            r = lax.rsqrt(_head_sum(v * v, ones) * (1.0 / HEAD_DIM) + EPS)
            vn = (v * r) * g_ref[...]
            rot = vn * _wide(cos_ref[...], w) + _pair_swap(vn) * _wide(sin_ref[...], w)
            dst[...] = (rot * mult).astype(dst.dtype)

    tab = pl.BlockSpec((tr, LANES), lambda i: (i % ns, 0))
    return pl.pallas_call(
        body, name=tag + "_qk", grid=(t // tr,),
        in_specs=[pl.BlockSpec((tr, ATTN_WIDTH), lambda i: (i, q_off)),
                  pl.BlockSpec((tr, KV_WIDTH), lambda i: (i, k_off)),
                  pl.BlockSpec((1, ATTN_WIDTH), lambda i: (0, 0)),
                  pl.BlockSpec((1, KV_WIDTH), lambda i: (0, 0)),
                  tab, tab, pl.BlockSpec((LANES, LANES), lambda i: (0, 0))],
        out_specs=(pl.BlockSpec((tr, ATTN_WIDTH), lambda i: (i, 0)),
                   pl.BlockSpec((tr, KV_WIDTH), lambda i: (i, 0))),
        out_shape=(jax.ShapeDtypeStruct((t, ATTN_WIDTH), _ACT), jax.ShapeDtypeStruct((t, KV_WIDTH), _ACT)),
        compiler_params=_params("parallel"),
    )(proj, proj, jnp.tile(q_gain, N_Q_HEADS).reshape(1, ATTN_WIDTH),
      jnp.tile(k_gain, N_KV_HEADS).reshape(1, KV_WIDTH), cos, sin, _head_ones())


def _qk_bwd(tag, proj, q_gain, k_gain, cos, sin, dq_rot, dk_rot, nb, s):
    t = proj.shape[0]
    tr = _tile(s, 512)
    ns = s // tr
    q_off = 3 * CONV_WIDTH // ATTN_WIDTH
    k_off = (3 * CONV_WIDTH + ATTN_WIDTH) // KV_WIDTH

    def body(q_ref, k_ref, qg_ref, kg_ref, cos_ref, sin_ref, ones_ref, dqr_ref, dkr_ref,
             dq_ref, dk_ref, dqg_ref, dkg_ref):
        i = pl.program_id(0)
        ones = ones_ref[...]
        for src, g_ref, dr_ref, dst, dg_ref, mult in ((q_ref, qg_ref, dqr_ref, dq_ref, dqg_ref, _QK_SCALE),
                                                      (k_ref, kg_ref, dkr_ref, dk_ref, dkg_ref, 1.0)):
            v = src[...].astype(F32)
            w = v.shape[1]
            r = lax.rsqrt(_head_sum(v * v, ones) * (1.0 / HEAD_DIM) + EPS)
            xhat = v * r
            dr = dr_ref[...] * mult
            dvn = dr * _wide(cos_ref[...], w) + _pair_swap(dr * _wide(sin_ref[...], w))
            dgx = dvn * g_ref[...]
            m = _head_sum(dgx * xhat, ones) * (1.0 / HEAD_DIM)
            dst[...] = (r * (dgx - xhat * m)).astype(dst.dtype)
            part = jnp.sum(dvn * xhat, axis=0, keepdims=True)
            fold = part[:, 0:HEAD_DIM]
            for hh in range(1, w // HEAD_DIM):
                fold = fold + part[:, hh * HEAD_DIM:(hh + 1) * HEAD_DIM]

            @pl.when(i == 0)
            def _():
                dg_ref[...] = fold

            @pl.when(i > 0)
            def _():
                dg_ref[...] += fold

    tab = pl.BlockSpec((tr, LANES), lambda i: (i % ns, 0))
    qrow = pl.BlockSpec((tr, ATTN_WIDTH), lambda i: (i, 0))
    krow = pl.BlockSpec((tr, KV_WIDTH), lambda i: (i, 0))
    gvec = pl.BlockSpec((1, HEAD_DIM), lambda i: (0, 0))
    dq, dk, dqg, dkg = pl.pallas_call(
        body, name=tag + "_dqk", grid=(t // tr,),
        in_specs=[pl.BlockSpec((tr, ATTN_WIDTH), lambda i: (i, q_off)),
                  pl.BlockSpec((tr, KV_WIDTH), lambda i: (i, k_off)),
                  pl.BlockSpec((1, ATTN_WIDTH), lambda i: (0, 0)),
                  pl.BlockSpec((1, KV_WIDTH), lambda i: (0, 0)),
                  tab, tab, pl.BlockSpec((LANES, LANES), lambda i: (0, 0)), qrow, krow],
        out_specs=(qrow, krow, gvec, gvec),
        out_shape=(jax.ShapeDtypeStruct((t, ATTN_WIDTH), _ACT), jax.ShapeDtypeStruct((t, KV_WIDTH), _ACT),
                   jax.ShapeDtypeStruct((1, HEAD_DIM), F32), jax.ShapeDtypeStruct((1, HEAD_DIM), F32)),
        compiler_params=_params("arbitrary"),
    )(proj, proj, jnp.tile(q_gain, N_Q_HEADS).reshape(1, ATTN_WIDTH),
      jnp.tile(k_gain, N_KV_HEADS).reshape(1, KV_WIDTH), cos, sin, _head_ones(), dq_rot, dk_rot)
    return dq, dk, dqg.reshape(HEAD_DIM), dkg.reshape(HEAD_DIM)


def _head(v, h):
    return v[:, h * HEAD_DIM:(h + 1) * HEAD_DIM]


def _attn_fwd(tag, q, k, proj, nb, s):
    t = q.shape[0]
    tq = _tile(s, 256)
    nq = s // tq
    v_off = (3 * CONV_WIDTH + ATTN_WIDTH + KV_WIDTH) // KV_WIDTH

    def body(q_ref, k_ref, v_ref, o_ref, lse_ref):
        qv = q_ref[...]
        kv = k_ref[...]
        vv = v_ref[...]
        for h in range(N_Q_HEADS):
            j = h // Q_PER_KV
            sc = _dot(_head(qv, h), _head(kv, j), "nt")
            m = jnp.max(sc, axis=-1, keepdims=True)
            e = jnp.exp(sc - m)
            l = jnp.sum(e, axis=-1, keepdims=True)
            o = _dot(e, _head(vv, j)) * (1.0 / l)
            o_ref[:, h * HEAD_DIM:(h + 1) * HEAD_DIM] = o.astype(o_ref.dtype)
            lse_ref[:, h:h + 1] = m + jnp.log(l)

    return pl.pallas_call(
        body, name=tag + "_attn", grid=(nb, nq),
        in_specs=[pl.BlockSpec((tq, ATTN_WIDTH), lambda b, i: (b * nq + i, 0)),
                  pl.BlockSpec((s, KV_WIDTH), lambda b, i: (b, 0)),
                  pl.BlockSpec((s, KV_WIDTH), lambda b, i: (b, v_off))],
        out_specs=(pl.BlockSpec((tq, ATTN_WIDTH), lambda b, i: (b * nq + i, 0)),
                   pl.BlockSpec((tq, N_Q_HEADS), lambda b, i: (b * nq + i, 0))),
        out_shape=(jax.ShapeDtypeStruct((t, ATTN_WIDTH), _ACT), jax.ShapeDtypeStruct((t, N_Q_HEADS), F32)),
        compiler_params=_params("parallel", "parallel"),
    )(q, k, proj)


def _attn_bwd(tag, q, k, proj, o, lse, d_mix, nb, s):
    t = q.shape[0]
    tq = _tile(s, 256)
    nq = s // tq
    v_off = (3 * CONV_WIDTH + ATTN_WIDTH + KV_WIDTH) // KV_WIDTH

    def body(q_ref, k_ref, v_ref, o_ref, lse_ref, do_ref, dq_ref, dk_ref, dv_ref):
        i = pl.program_id(1)

        @pl.when(i == 0)
        def _():
            dk_ref[...] = jnp.zeros_like(dk_ref)
            dv_ref[...] = jnp.zeros_like(dv_ref)

        qv = q_ref[...]
        kv = k_ref[...]
        vv = v_ref[...]
        ov = o_ref[...].astype(F32)
        dov = do_ref[...]
        lse = lse_ref[...]
        for h in range(N_Q_HEADS):
            j = h // Q_PER_KV
            cols = slice(j * HEAD_DIM, (j + 1) * HEAD_DIM)
            qh = _head(qv, h)
            kj = _head(kv, j)
            doh = _head(dov, h)
            sc = _dot(qh, kj, "nt")
            p = jnp.exp(sc - lse[:, h:h + 1])
            dp = _dot(doh, _head(vv, j), "nt")
            delta = jnp.sum(doh * _head(ov, h), axis=-1, keepdims=True)
            ds = p * (dp - delta)
            dv_ref[:, cols] += _dot(p, doh, "tn")
            dk_ref[:, cols] += _dot(ds, qh, "tn")
            dq_ref[:, h * HEAD_DIM:(h + 1) * HEAD_DIM] = _dot(ds, kj)

    qrow = pl.BlockSpec((tq, ATTN_WIDTH), lambda b, i: (b * nq + i, 0))
    kvrow = pl.BlockSpec((s, KV_WIDTH), lambda b, i: (b, 0))
    return pl.pallas_call(
        body, name=tag + "_dattn", grid=(nb, nq),
        in_specs=[qrow, kvrow, pl.BlockSpec((s, KV_WIDTH), lambda b, i: (b, v_off)), qrow,
                  pl.BlockSpec((tq, N_Q_HEADS), lambda b, i: (b * nq + i, 0)),
                  pl.BlockSpec((tq, ATTN_WIDTH), lambda b, i: (b * nq + i, 1))],
        out_specs=(qrow, kvrow, kvrow),
        out_shape=(jax.ShapeDtypeStruct((t, ATTN_WIDTH), F32), jax.ShapeDtypeStruct((t, KV_WIDTH), F32),
                   jax.ShapeDtypeStruct((t, KV_WIDTH), F32)),
        compiler_params=_params("parallel", "arbitrary"),
    )(q, k, proj, o, lse, d_mix)


def _even_fwd(tag, x, p, cos, sin, nb, s):
    h = _rmsnorm_fwd(tag + "_norm", x, p["norm"])
    proj = _proj_in(tag, h, p["w_in"])
    a = _conv_fwd(tag, proj, p["conv_w"], nb, s)
    q, k = _qk_fwd(tag, proj, p["q_gain"], p["k_gain"], cos, sin, nb, s)
    o, lse = _attn_fwd(tag, q, k, proj, nb, s)
    x_out = _proj_out(tag, x, [a, o], p["w_out"])
    return x_out, (x, h, proj, a, q, k, o, lse)


def _even_bwd(tag, dxo, dxo_lo, saved, p, cos, sin, nb, s, mid, ready):
    x, h, proj, a, q, k, o, lse = saved
    d_mix, dw_out = _proj_out_bwd(tag, dxo_lo, [a, o], p["w_out"])
    dgb, dgc, dhc, dconv_w = _conv_bwd(tag, proj, p["conv_w"] + mid(d_mix), d_mix, nb, s)
    dq_rot, dk_rot, dv = _attn_bwd(tag, q, k, proj, o, lse, d_mix, nb, s)
    dq, dk, dq_gain, dk_gain = _qk_bwd(tag, proj, p["q_gain"], p["k_gain"], cos, sin, dq_rot, dk_rot, nb, s)
    dx, dx_lo, dnorm = _proj_in_bwd(tag, h, [dgb, dgc, dhc, dq, dk, dv], p["w_in"], x, dxo, p["norm"],
                                    lambda dw_in: ready(dict(w_in=dw_in, w_out=dw_out), dw_in))
    return dx, dx_lo, dict(norm=dnorm, conv_w=dconv_w, q_gain=dq_gain, k_gain=dk_gain)


def _window(pad_ref, val, r, s):
    pad_ref[pl.ds(HALO, s), :] = val
    acc = val
    for d in range(1, r + 1):
        acc = acc + pad_ref[pl.ds(HALO - d, s), :] + pad_ref[pl.ds(HALO + d, s), :]
    return acc


def _count(r, s):
    t = lax.broadcasted_iota(jnp.int32, (s, 1), 0)
    return (jnp.minimum(t + r, s - 1) - jnp.maximum(t - r, 0) + 1).astype(F32)


def _sgu_chunk(u_ref, v_ref, norm, ws_ref, bt, rows):
    uu = u_ref[rows, :].astype(F32)
    vv = v_ref[rows, :].astype(F32)
    gu = _gelu(uu)
    gv = _gelu(vv)
    r = lax.rsqrt(jnp.mean(gv * gv, axis=-1, keepdims=True) + EPS)
    xhat = gv * r
    vn = xhat * norm
    mixed = []
    for g in range(N_GROUPS):
        cols = slice(g * SGU_GROUP, (g + 1) * SGU_GROUP)
        mixed.append(_dot(ws_ref[g], vn[:, cols]) + bt[:, g:g + 1])
    return uu, vv, gu, r, xhat, vn, mixed


def _odd_core_fwd(tag, proj, p, nb, s):
    t = proj.shape[0]
    nchunk = s // SGU_CHUNK

    def body(p_ref, u_ref, v_ref, pw_ref, ps_ref, sn_ref, ws_ref, bt_ref, mix_ref, pad_ref):
        _zero_halo(pad_ref, s)
        for g, r in enumerate(POOL_RADII):
            cols = slice(g * POOL_GROUP, (g + 1) * POOL_GROUP)
            pg = p_ref[:, cols].astype(F32)
            pooled = _window(pad_ref, pg, r, s) / _count(r, s) - pg
            mix_ref[:, cols] = (_dot(pooled, pw_ref[g]) * ps_ref[:, cols]).astype(mix_ref.dtype)
        norm = sn_ref[...]
        bt = bt_ref[...]

        def chunk(n, carry):
            rows = pl.ds(pl.multiple_of(n * SGU_CHUNK, SGU_CHUNK), SGU_CHUNK)
            _, _, gu, _, _, _, mixed = _sgu_chunk(u_ref, v_ref, norm, ws_ref, bt, rows)
            for g in range(N_GROUPS):
                cols = slice(g * SGU_GROUP, (g + 1) * SGU_GROUP)
                mix_ref[rows, HALF + g * SGU_GROUP:HALF + (g + 1) * SGU_GROUP] = (
                    gu[:, cols] * mixed[g]).astype(mix_ref.dtype)
            return carry

        lax.fori_loop(0, nchunk, chunk, 0)

    def col(j):
        return pl.BlockSpec((s, HALF), lambda b: (b, j))

    def whole(a):
        return pl.BlockSpec(a.shape, lambda b: (0,) * a.ndim)

    consts = [p["pool_w"], p["pool_scale"].reshape(1, HALF), p["sgu_norm"].reshape(1, HALF),
              p["sgu_w"], p["sgu_b"].T]
    return pl.pallas_call(
        body, name=tag + "_core", grid=(nb,),
        in_specs=[col(0), col(1), col(2)] + [whole(a) for a in consts],
        out_specs=pl.BlockSpec((s, D_MODEL), lambda b: (b, 0)),
        out_shape=jax.ShapeDtypeStruct((t, D_MODEL), _ACT),
        scratch_shapes=[pltpu.VMEM((s + 2 * HALO, POOL_GROUP), F32)],
        compiler_params=_params("parallel"),
    )(proj, proj, proj, *consts)


def _odd_core_bwd(tag, proj, p, d_mix, nb, s):
    t = proj.shape[0]
    nchunk = s // SGU_CHUNK

    def body(p_ref, u_ref, v_ref, pw_ref, ps_ref, sn_ref, ws_ref, bt_ref, dm_ref,
             dproj_ref, dpw_ref, dps_ref, dsn_ref, dws_ref, dbt_ref, pad_ref):
        b = pl.program_id(0)

        @pl.when(b == 0)
        def _():
            dpw_ref[...] = jnp.zeros_like(dpw_ref)
            dps_ref[...] = jnp.zeros_like(dps_ref)
            dsn_ref[...] = jnp.zeros_like(dsn_ref)
            dws_ref[...] = jnp.zeros_like(dws_ref)
            dbt_ref[...] = jnp.zeros_like(dbt_ref)

        _zero_halo(pad_ref, s)
        for g, r in enumerate(POOL_RADII):
            cols = slice(g * POOL_GROUP, (g + 1) * POOL_GROUP)
            pg = p_ref[:, cols].astype(F32)
            cnt = _count(r, s)
            pooled = _window(pad_ref, pg, r, s) / cnt - pg
            c_pre = _dot(pooled, pw_ref[g])
            dc = dm_ref[:, cols]
            dps_ref[:, cols] += jnp.sum(dc * c_pre, axis=0, keepdims=True)
            dcp = dc * ps_ref[:, cols]
            dpw_ref[g] += _dot(pooled, dcp, "tn")
            dpooled = _dot(dcp, pw_ref[g], "nt")
            dproj_ref[:, cols] = (_window(pad_ref, dpooled / cnt, r, s) - dpooled).astype(dproj_ref.dtype)
        norm = sn_ref[...]
        bt = bt_ref[...]

        def chunk(n, carry):
            rows = pl.ds(pl.multiple_of(n * SGU_CHUNK, SGU_CHUNK), SGU_CHUNK)
            uu, vv, gu, r, xhat, vn, mixed = _sgu_chunk(u_ref, v_ref, norm, ws_ref, bt, rows)
            dd = dm_ref[rows, HALF:D_MODEL]
            dgu, dvn = [], []
            for g in range(N_GROUPS):
                cols = slice(g * SGU_GROUP, (g + 1) * SGU_GROUP)
                dgu.append(dd[:, cols] * mixed[g])
                dmx = dd[:, cols] * gu[:, cols]
                dbt_ref[:, g:g + 1] += jnp.sum(dmx, axis=-1, keepdims=True)
                dws_ref[g] += _dot(dmx, vn[:, cols], "nt")
                dvn.append(_dot(ws_ref[g], dmx, "tn"))
            dgu = jnp.concatenate(dgu, axis=1)
            dvn = jnp.concatenate(dvn, axis=1)
            dsn_ref[...] += jnp.sum(dvn * xhat, axis=0, keepdims=True)
            dgx = dvn * norm
            m = jnp.mean(dgx * xhat, axis=-1, keepdims=True)
            dgv = r * (dgx - xhat * m)
            dproj_ref[rows, HALF:2 * HALF] = (dgu * _gelu_grad(uu)).astype(dproj_ref.dtype)
            dproj_ref[rows, 2 * HALF:3 * HALF] = (dgv * _gelu_grad(vv)).astype(dproj_ref.dtype)
            return carry

        lax.fori_loop(0, nchunk, chunk, 0)

    def col(j):
        return pl.BlockSpec((s, HALF), lambda b: (b, j))

    def whole(a):
        return pl.BlockSpec(a.shape, lambda b: (0,) * a.ndim)

    consts = [p["pool_w"], p["pool_scale"].reshape(1, HALF), p["sgu_norm"].reshape(1, HALF),
              p["sgu_w"], p["sgu_b"].T]
    gshapes = [jax.ShapeDtypeStruct(a.shape, F32) for a in consts]
    dproj, dpw, dps, dsn, dws, dbt = pl.pallas_call(
        body, name=tag + "_dcore", grid=(nb,),
        in_specs=[col(0), col(1), col(2)] + [whole(a) for a in consts]
        + [pl.BlockSpec((s, D_MODEL), lambda b: (b, 0))],
        out_specs=[pl.BlockSpec((s, 3 * HALF), lambda b: (b, 0))] + [whole(a) for a in consts],
        out_shape=[jax.ShapeDtypeStruct((t, 3 * HALF), _ACT)] + gshapes,
        scratch_shapes=[pltpu.VMEM((s + 2 * HALO, POOL_GROUP), F32)],
        compiler_params=_params("arbitrary"),
    )(proj, proj, proj, *consts, d_mix)
    return dproj, dict(pool_w=dpw, pool_scale=dps.reshape(HALF), sgu_norm=dsn.reshape(HALF), sgu_w=dws, sgu_b=dbt.T)


def _odd_fwd(tag, x, p, nb, s):
    h = _rmsnorm_fwd(tag + "_norm", x, p["norm"])
    proj = _proj_in(tag, h, p["w_in"])
    mix = _odd_core_fwd(tag, proj, p, nb, s)
    x_out = _proj_out(tag, x, [mix], p["w_out"])
    return x_out, (x, h, proj, mix)


def _odd_bwd(tag, dxo, dxo_lo, saved, p, nb, s, mid, ready):
    x, h, proj, mix = saved
    d_mix, dw_out = _proj_out_bwd(tag, dxo_lo, [mix], p["w_out"])
    p = dict(p, pool_scale=p["pool_scale"] + mid(d_mix))
    dproj, grads = _odd_core_bwd(tag, proj, p, d_mix, nb, s)
    dx, dx_lo, dnorm = _proj_in_bwd(tag, h, [dproj], p["w_in"], x, dxo, p["norm"],
                                    lambda dw_in: ready(dict(w_in=dw_in, w_out=dw_out), dw_in))
    grads.update(norm=dnorm)
    return dx, dx_lo, grads


def _local_step(x3, target3, depth, weights_of, final_norm, mid, grads_ready, small_done):
    nb, s, d = x3.shape
    t = nb * s
    x = x3.reshape(t, d)
    target = target3.reshape(t, d)
    cos, sin = _rope_tables(s)
    saved, ws = [], []
    for l in range(depth):
        w1 = weights_of(l, "ffn1", x)
        x, s1 = _ffn_fwd(f"l{l}_ffn1", x, w1["norm"], w1["w_in4"], w1["w_out"], _ffn_tiles(l, 1))
        wm = weights_of(l, "mix", x)
        if l % 2 == 0:
            x, s2 = _even_fwd(f"l{l}_ev", x, wm, cos, sin, nb, s)
        else:
            x, s2 = _odd_fwd(f"l{l}_od", x, wm, nb, s)
        w2 = weights_of(l, "ffn2", x)
        x, s3 = _ffn_fwd(f"l{l}_ffn2", x, w2["norm"], w2["w_in4"], w2["w_out"], _ffn_tiles(l, 2))
        saved.append((s1, s2, s3))
        ws.append((w1, wm, w2))
    loss, dx, dx_lo, dfinal = _final_loss("final_loss", x, final_norm, target)
    for l in reversed(range(depth)):
        s1, s2, s3 = saved[l]
        w1, wm, w2 = ws[l]

        def ready(block):
            return lambda grads, a: grads_ready(l, block, grads, a)

        dx, dx_lo, dn = _ffn_bwd(f"l{l}_ffn2", dx, dx_lo, s3, w2["norm"], w2["w_in4"], w2["w_out"], _ffn_tiles(l, 2), mid,
                                 ready("ffn2"))
        small_done(l, "ffn2", dict(norm=dn))
        if l % 2 == 0:
            dx, dx_lo, gm = _even_bwd(f"l{l}_ev", dx, dx_lo, s2, wm, cos, sin, nb, s, mid, ready("mix"))
        else:
            dx, dx_lo, gm = _odd_bwd(f"l{l}_od", dx, dx_lo, s2, wm, nb, s, mid, ready("mix"))
        small_done(l, "mix", gm)
        dx, dx_lo, dn = _ffn_bwd(f"l{l}_ffn1", dx, dx_lo, s1, w1["norm"], w1["w_in4"], w1["w_out"], _ffn_tiles(l, 1), mid,
                                 ready("ffn1"))
        small_done(l, "ffn1", dict(norm=dn))
    return loss, dx.reshape(nb, s, d), dfinal


_HBM = pl.BlockSpec(memory_space=pltpu.HBM)


def _place():
    x, y, c = lax.axis_index("x"), lax.axis_index("y"), lax.axis_index("c")
    chips = [(1 - x, y), (x, 1 - y), (1 - x, 1 - y)]
    return x, y, c, chips


def _remote(src, dst, send_sem, recv_sem, to):
    return pltpu.make_async_remote_copy(src_ref=src, dst_ref=dst, send_sem=send_sem, recv_sem=recv_sem,
                                        device_id=to, device_id_type=_MESH)


def _gather_shards(arrs, small):
    n = len(arrs)
    own = 6

    def body(*refs):
        ins, sm_in = refs[:n], refs[n]
        outs, sm_out = refs[n + 1:2 * n + 1], refs[2 * n + 1]
        send, recv = refs[2 * n + 2:]
        x, y, c, chips = _place()
        k = 2 * x + y
        sib = (x, y, 1 - c)
        started = []
        for a in range(n + 1):
            src, dst = (ins[a], outs[a]) if a < n else (sm_in, sm_out)
            cp = _remote(src, dst.at[k], send.at[a, own], recv.at[a, own], sib)
            cp.start()
            started.append(cp)
            if a < n:
                h = src.shape[0] // 2
                mine = pl.ds(c * h, h)
                src_part, dst_part = src.at[mine], dst.at[k, mine]
            else:
                src_part, dst_part = src, dst.at[k]
            for j, chip in enumerate(chips):
                cp = _remote(src_part, dst_part, send.at[a, j], recv.at[a, j], (*chip, c))
                cp.start()
                started.append(cp)
        for a in range(n):
            h = ins[a].shape[0] // 2
            mine = pl.ds(c * h, h)
            for j, (px, py) in enumerate(chips):
                landed = outs[a].at[2 * px + py, mine]
                _remote(landed, landed, send.at[a, j], recv.at[a, j], (px, py, c)).wait_recv()
                cp = _remote(landed, landed, send.at[a, 3 + j], recv.at[a, 3 + j], sib)
                cp.start()
                started.append(cp)
        for a in range(n):
            h = ins[a].shape[0] // 2
            other = pl.ds((1 - c) * h, h)
            for j, (px, py) in enumerate(chips):
                passed = outs[a].at[2 * px + py, other]
                _remote(passed, passed, send.at[a, 3 + j], recv.at[a, 3 + j], sib).wait_recv()
        for j, (px, py) in enumerate(chips):
            landed = sm_out.at[2 * px + py]
            _remote(landed, landed, send.at[n, j], recv.at[n, j], (px, py, c)).wait_recv()
        for a in range(n + 1):
            filled = (outs[a] if a < n else sm_out).at[k]
            _remote(filled, filled, send.at[a, own], recv.at[a, own], sib).wait_recv()
        for cp in started:
            cp.wait_send()

    outs = pl.pallas_call(
        body, name="gather_shards",
        in_specs=[_HBM] * (n + 1), out_specs=[_HBM] * (n + 1),
        out_shape=[jax.ShapeDtypeStruct((N_CHIPS,) + a.shape, a.dtype) for a in list(arrs) + [small]],
        scratch_shapes=[pltpu.SemaphoreType.DMA((n + 1, 7)), pltpu.SemaphoreType.DMA((n + 1, 7))],
    )(*arrs, small)
    return outs[:n], outs[n]


_SEM = pl.BlockSpec(memory_space=pltpu.SEMAPHORE)
_EFFECT = pltpu.SideEffectType.DATAFLOW_SIDE_EFFECTING


def _gather_plan(i, src, land, k, c, chips, sib):
    mine = pl.ds(c * (src.shape[0] // 2), src.shape[0] // 2)
    plan = [(src.at[mine], land.at[k, mine], (px, py, c), land.at[2 * px + py, mine]) for px, py in chips]
    return plan + [(src, land.at[k], sib, land.at[k])]


def _pass_plan(i, src, land, k, c, chips, sib):
    h = src.shape[1] // 2
    mine, theirs = pl.ds(c * h, h), pl.ds((1 - c) * h, h)
    return [(src.at[2 * px + py, mine], land.at[2 * px + py, mine], sib, land.at[2 * px + py, theirs]) for px, py in chips]


def _swap_plan(i, src, land, k, c, chips, sib):
    h = src.shape[1] // 2
    return [(src.at[:, pl.ds((1 - c) * h, h)], land, sib, land)]


def _scatter_plan(i, src, land, k, c, chips, sib):
    return [(src.at[2 * px + py], land.at[k], (px, py, c), land.at[2 * px + py]) for px, py in chips]


def _join_plan(layers):
    def plan(i, src, land, k, c, chips, sib):
        h = src.shape[1] // 2
        mine, theirs = pl.ds(c * h, h), pl.ds((1 - c) * h, h)
        return [(src.at[layers[i], mine], land.at[layers[i], mine], sib, land.at[layers[i], theirs])]
    return plan


def _split_start(name, plan, ncopy, srcs, after, land_shapes=None):
    n = len(srcs)
    if land_shapes is None:
        land_shapes = [(N_CHIPS,) + a.shape[-2:] for a in srcs]
    in_place = land_shapes == "self"
    lands = [] if in_place else [pltpu.with_memory_space_constraint(lax.empty(shape, a.dtype), pltpu.HBM)
                                 for shape, a in zip(land_shapes, srcs)]
    nbuf = n + len(lands)

    def body(*refs):
        src_refs = refs[1:1 + n]
        land_refs = src_refs if in_place else refs[1 + n:1 + nbuf]
        send, recv, token = refs[1 + nbuf], refs[2 + nbuf], refs[-1]
        x, y, c, chips = _place()
        for i in range(n):
            for j, (src, dst, peer, _) in enumerate(plan(i, src_refs[i], land_refs[i], 2 * x + y, c, chips, (x, y, 1 - c))):
                _remote(src, dst, send.at[i * ncopy + j], recv.at[i * ncopy + j], peer).start()
        token[...] = jnp.zeros_like(token)

    outs = pl.pallas_call(
        body, name=name,
        in_specs=[_ANY] + [_HBM] * nbuf,
        out_specs=[_SEM, _SEM] + [_HBM] * nbuf + [_VMEM],
        out_shape=[pltpu.SemaphoreType.DMA((n * ncopy,)), pltpu.SemaphoreType.DMA((n * ncopy,))]
        + [pltpu.HBM(a.shape, a.dtype) for a in list(srcs) + lands] + [jax.ShapeDtypeStruct((8, LANES), F32)],
        input_output_aliases={1 + i: 2 + i for i in range(nbuf)},
        compiler_params=pltpu.CompilerParams(has_side_effects=_EFFECT),
    )(after, *[pltpu.with_memory_space_constraint(a, pltpu.HBM) for a in srcs], *lands)
    return outs[0], outs[1], outs[2:2 + n], None if in_place else outs[2 + n:2 + nbuf], outs[-1]


def _split_wait(name, plan, started, after):
    send, recv, srcs, lands = started
    n = len(srcs)
    ncopy = send.shape[0] // n
    in_place = lands is None
    bufs = list(srcs) + ([] if in_place else list(lands))
    nbuf = len(bufs)

    def body(*refs):
        src_refs = refs[:n]
        land_refs = src_refs if in_place else refs[n:nbuf]
        send, recv = refs[nbuf], refs[nbuf + 1]
        x, y, c, chips = _place()
        for i in range(n):
            for j, (src, _, peer, landed) in enumerate(plan(i, src_refs[i], land_refs[i], 2 * x + y, c, chips, (x, y, 1 - c))):
                cp = _remote(src, landed, send.at[i * ncopy + j], recv.at[i * ncopy + j], peer)
                cp.wait_send()
                cp.wait_recv()

    outs = pl.pallas_call(
        body, name=name,
        in_specs=[_HBM] * nbuf + [_SEM, _SEM, _ANY],
        out_specs=[_HBM] * nbuf,
        out_shape=[pltpu.HBM(a.shape, a.dtype) for a in bufs],
        input_output_aliases={i: i for i in range(nbuf)},
        compiler_params=pltpu.CompilerParams(has_side_effects=_EFFECT),
    )(*bufs, send, recv, after)
    return outs[:n], outs[:n] if in_place else outs[n:]


def _allreduce_small(buf, after):
    rows = buf.shape[0]
    piece = rows // N_DEV

    def body(in_ref, after_ref, out_ref, land_ref, send, recv):
        x, y, c, _ = _place()
        me = 4 * x + 2 * y + c
        peers = [(1 - x if r & 4 else x, 1 - y if r & 2 else y, 1 - c if r & 1 else c) for r in range(1, N_DEV)]

        def rows_of(dev):
            return pl.ds(pl.multiple_of(dev * piece, 8), piece)

        first, second = [], []
        for r, (px, py, pc) in enumerate(peers):
            cp = _remote(in_ref.at[rows_of(4 * px + 2 * py + pc)], land_ref.at[me], send.at[0, r], recv.at[0, r], (px, py, pc))
            cp.start()
            first.append(cp)
        land_ref[me] = in_ref[rows_of(me), :]
        for r, (px, py, pc) in enumerate(peers):
            landed = land_ref.at[4 * px + 2 * py + pc]
            _remote(landed, landed, send.at[0, r], recv.at[0, r], (px, py, pc)).wait_recv()
        acc = land_ref[0]
        for d in range(1, N_DEV):
            acc = acc + land_ref[d]
        out_ref[rows_of(me), :] = acc
        for r, peer in enumerate(peers):
            cp = _remote(out_ref.at[rows_of(me)], out_ref.at[rows_of(me)], send.at[1, r], recv.at[1, r], peer)
            cp.start()
            second.append(cp)
        for r, (px, py, pc) in enumerate(peers):
            landed = out_ref.at[rows_of(4 * px + 2 * py + pc)]
            _remote(landed, landed, send.at[1, r], recv.at[1, r], (px, py, pc)).wait_recv()
        for cp in first + second:
            cp.wait_send()

    return pl.pallas_call(
        body, name="allreduce_small",
        in_specs=[_VMEM, _ANY], out_specs=_VMEM,
        out_shape=jax.ShapeDtypeStruct(buf.shape, F32),
        scratch_shapes=[pltpu.VMEM((N_DEV, piece, LANES), F32), pltpu.SemaphoreType.DMA((2, N_DEV - 1)),
                        pltpu.SemaphoreType.DMA((2, N_DEV - 1))],
        compiler_params=pltpu.CompilerParams(vmem_limit_bytes=_VMEM_LIMIT),
    )(buf, after)


def _div_tile(n, cap, mult):
    best = None
    for d in range(mult, min(n, cap) + 1, mult):
        if n % d == 0:
            best = d
    return best if best is not None else n


def _add_sibling(name, grad, got, c):
    nk, hr, cc = got.shape
    tr = _div_tile(hr, 512, 16)
    nt = hr // tr

    def body(c_ref, g_ref, o_ref, s_ref):
        s_ref[...] = (g_ref[...].astype(F32) + o_ref[...].astype(F32)).astype(s_ref.dtype)

    blk = (None, tr, cc)
    return pl.pallas_call(
        body, name=name,
        grid_spec=pltpu.PrefetchScalarGridSpec(
            num_scalar_prefetch=1, grid=(nk, nt),
            in_specs=[pl.BlockSpec(blk, lambda i, q, c_ref: (i, c_ref[0] * nt + q, 0)),
                      pl.BlockSpec(blk, lambda i, q, c_ref: (i, q, 0))],
            out_specs=pl.BlockSpec(blk, lambda i, q, c_ref: (i, q, 0))),
        out_shape=jax.ShapeDtypeStruct(got.shape, got.dtype),
        compiler_params=_params("parallel", "parallel"),
    )(c, grad, got)


def _add_chips(name, mine, got, place, buf, l):
    nk, hr, cc = got.shape
    tr = _div_tile(hr, 512, 16)
    nt = hr // tr

    def body(*refs):
        acc = refs[1][...].astype(F32)
        for q in range(1, nk):
            acc = acc + refs[1 + q][...].astype(F32)
        refs[2 + nk][...] = acc

    def part(q):
        return pl.BlockSpec((None, tr, cc), lambda i, p_ref: ((p_ref[0] + q) % nk, i, 0))

    return pl.pallas_call(
        body, name=name,
        grid_spec=pltpu.PrefetchScalarGridSpec(
            num_scalar_prefetch=1, grid=(nt,),
            in_specs=[part(q) for q in range(nk)] + [_ANY],
            out_specs=pl.BlockSpec((None, tr, cc), lambda i, p_ref: (l, p_ref[1] * nt + i, 0))),
        out_shape=jax.ShapeDtypeStruct(buf.shape, F32),
        input_output_aliases={1 + nk: 0},
        compiler_params=_params("parallel"),
    )(place, mine, *([got] * (nk - 1)), buf)


def _adamw(name, w, g, m, v, after=None):
    shape = w.shape
    cols = shape[-1]
    rows = w.size // cols
    tr = rows if rows * cols <= 2 ** 18 else _div_tile(rows, max(8, 2 ** 18 // cols), 8)
    c1 = 1.0 - ADAM_B1 ** ADAM_STEP
    c2 = 1.0 - ADAM_B2 ** ADAM_STEP
    extra = [] if after is None else [after]

    def body(*refs):
        w_ref, g_ref, m_ref, v_ref = refs[:4]
        d_ref, mo_ref, vo_ref, go_ref = refs[4 + len(extra):]
        gg = g_ref[...]
        mn = ADAM_B1 * m_ref[...] + (1.0 - ADAM_B1) * gg
        vn = ADAM_B2 * v_ref[...] + (1.0 - ADAM_B2) * (gg * gg)
        d_ref[...] = -ADAM_LR * ((mn / c1) / (jnp.sqrt(vn / c2) + ADAM_EPS) + ADAM_WD * w_ref[...])
        mo_ref[...] = mn
        vo_ref[...] = vn
        go_ref[...] = gg

    blk = pl.BlockSpec((tr, cols), lambda i: (i, 0))
    sds = jax.ShapeDtypeStruct((rows, cols), F32)
    outs = pl.pallas_call(
        body, name=name, grid=(rows // tr,),
        in_specs=[blk] * 4 + [_ANY] * len(extra), out_specs=(blk,) * 4, out_shape=(sds,) * 4,
        compiler_params=_params("parallel"),
    )(*[a.reshape(rows, cols) for a in (w, g, m, v)], *extra)
    return [o.reshape(shape) for o in outs]


_WEIGHTS = ["ffn1_norm", "ffn1_w_in", "ffn1_w_out", "mix_norm", "ffn2_norm", "ffn2_w_in", "ffn2_w_out",
            "ev_w_in", "ev_conv_w", "ev_q_norm", "ev_k_norm", "ev_w_out", "od_w_in", "od_pool_w",
            "od_pool_scale", "od_sgu_norm", "od_sgu_w", "od_sgu_b", "od_w_out", "final_norm"]
_BIG = ["ffn1_w_in", "ffn1_w_out", "ffn2_w_in", "ffn2_w_out", "ev_w_in", "ev_w_out", "od_w_in", "od_w_out"]
_SMALL_SHARDED = ["ev_conv_w", "od_pool_scale", "od_sgu_norm"]


def _pad_rows(a, mult=8):
    pad = (-a.shape[0]) % mult
    return a if pad == 0 else jnp.concatenate([a, jnp.zeros((pad,) + a.shape[1:], a.dtype)], axis=0)


def _join_cols(g):
    return g.transpose(1, 0, 2).reshape(g.shape[1], N_CHIPS * g.shape[2])


def _split_cols(w):
    return w.reshape(w.shape[0], N_CHIPS, w.shape[1] // N_CHIPS).transpose(1, 0, 2)


def kernel(x, ffn1_norm, ffn1_w_in, ffn1_w_out, mix_norm, ffn2_norm, ffn2_w_in, ffn2_w_out, ev_w_in, ev_conv_w,
           ev_q_norm, ev_k_norm, ev_w_out, od_w_in, od_pool_w, od_pool_scale, od_sgu_norm, od_sgu_w, od_sgu_b,
           od_w_out, final_norm, loss_target, m_ffn1_norm, m_ffn1_w_in, m_ffn1_w_out, m_mix_norm, m_ffn2_norm,
           m_ffn2_w_in, m_ffn2_w_out, m_ev_w_in, m_ev_conv_w, m_ev_q_norm, m_ev_k_norm, m_ev_w_out, m_od_w_in,
           m_od_pool_w, m_od_pool_scale, m_od_sgu_norm, m_od_sgu_w, m_od_sgu_b, m_od_w_out, m_final_norm, v_ffn1_norm,
           v_ffn1_w_in, v_ffn1_w_out, v_mix_norm, v_ffn2_norm, v_ffn2_w_in, v_ffn2_w_out, v_ev_w_in, v_ev_conv_w,
           v_ev_q_norm, v_ev_k_norm, v_ev_w_out, v_od_w_in, v_od_pool_w, v_od_pool_scale, v_od_sgu_norm, v_od_sgu_w,
           v_od_sgu_b, v_od_w_out, v_final_norm):
    return _step(x, ffn1_norm, ffn1_w_in, ffn1_w_out, mix_norm, ffn2_norm, ffn2_w_in, ffn2_w_out, ev_w_in, ev_conv_w,
                 ev_q_norm, ev_k_norm, ev_w_out, od_w_in, od_pool_w, od_pool_scale, od_sgu_norm, od_sgu_w, od_sgu_b,
                 od_w_out, final_norm, loss_target, m_ffn1_norm, m_ffn1_w_in, m_ffn1_w_out, m_mix_norm, m_ffn2_norm,
                 m_ffn2_w_in, m_ffn2_w_out, m_ev_w_in, m_ev_conv_w, m_ev_q_norm, m_ev_k_norm, m_ev_w_out, m_od_w_in,
                 m_od_pool_w, m_od_pool_scale, m_od_sgu_norm, m_od_sgu_w, m_od_sgu_b, m_od_w_out, m_final_norm,
                 v_ffn1_norm, v_ffn1_w_in, v_ffn1_w_out, v_mix_norm, v_ffn2_norm, v_ffn2_w_in, v_ffn2_w_out,
                 v_ev_w_in, v_ev_conv_w, v_ev_q_norm, v_ev_k_norm, v_ev_w_out, v_od_w_in, v_od_pool_w,
                 v_od_pool_scale, v_od_sgu_norm, v_od_sgu_w, v_od_sgu_b, v_od_w_out, v_final_norm)


def _step(*args):
    nw = len(_WEIGHTS)
    x = args[0]
    w = dict(zip(_WEIGHTS, args[1:1 + nw]))
    target = args[1 + nw]
    m = dict(zip(_WEIGHTS, args[2 + nw:2 + 2 * nw]))
    v = dict(zip(_WEIGHTS, args[2 + 2 * nw:2 + 3 * nw]))
    depth = w["ffn1_norm"].shape[0]
    n_even, n_odd = w["ev_w_in"].shape[0], w["od_w_in"].shape[0]
    chip = 2 * lax.axis_index("x") + lax.axis_index("y")
    place = jnp.stack([chip, lax.axis_index("c")]).astype(jnp.int32)
    core = place[1:2]

    def sharded(l, block):
        if block == "mix":
            block = "ev" if l % 2 == 0 else "od"
            return [(block + "_w_in", l // 2), (block + "_w_out", l // 2)]
        return [(block + "_w_in", l), (block + "_w_out", l)]

    def shards(group, zero):
        return [(w[n][i] + zero).astype(_ACT) for l, block in group for n, i in sharded(l, block)]

    later = ([[(0, "ffn1")], [(0, "mix"), (0, "ffn2")]]
             + [[(l, "ffn1"), (l, "mix"), (l, "ffn2")] for l in range(1, depth)])
    small_rows = [w["ev_conv_w"].reshape(3 * n_even, LANES), w["od_pool_scale"], w["od_sgu_norm"]]
    _, small = _gather_shards([], _pad_rows(jnp.concatenate(small_rows, axis=0)))
    conv_w = small[:, :3 * n_even].reshape(N_CHIPS, n_even, 3, LANES).transpose(1, 2, 0, 3).reshape(n_even, 3, CONV_WIDTH)
    pool_scale = small[:, 3 * n_even:3 * n_even + n_odd].transpose(1, 0, 2).reshape(n_odd, HALF)
    sgu_norm = small[:, 3 * n_even + n_odd:3 * n_even + 2 * n_odd].transpose(1, 0, 2).reshape(n_odd, HALF)
    gathering, after, zero = [], small, 0.0
    for i, group in enumerate(later):
        gathering.append(_split_start(f"gather_start{i}", _gather_plan, N_CHIPS, shards(group, zero), after))
        after = gathering[-1][4]
        zero = after[0, 0]
    gathered = {}

    def rows(g):
        return g.reshape(N_CHIPS * g.shape[1], g.shape[2])

    passing = {}

    def fetch(i, x_in):
        got = _split_wait(f"gather_wait{i}", _gather_plan, gathering[i][:4], x_in)[1]
        passing[i] = _split_start(f"pass_start{i}", _pass_plan, N_CHIPS - 1, got, x_in, "self")
        return passing[i][4][0, 0]

    def weights_of(l, block, x_in):
        zero = after[0, 0] if (l, block) == (0, "ffn1") else 0.0
        if (l, block) not in gathered:
            i = next(i for i, group in enumerate(later) if (l, block) in group)
            if i not in passing:
                zero = zero + fetch(i, x_in)
            got = _split_wait(f"pass_wait{i}", _pass_plan, passing.pop(i)[:4], x_in)[0]
            for n, key in enumerate(later[i]):
                gathered[key] = got[2 * n:2 * n + 2]
        if block == "ffn2" and l + 1 < depth:
            zero = zero + fetch(next(i for i, group in enumerate(later) if (l + 1, "ffn1") in group), x_in)
        w_in, w_out = gathered[(l, block)]
        if block != "mix":
            return dict(norm=w[block + "_norm"][l] + zero, w_in4=w_in, w_out=rows(w_out))
        j = l // 2
        if l % 2 == 0:
            mix = dict(conv_w=conv_w[j], q_gain=w["ev_q_norm"][j], k_gain=w["ev_k_norm"][j])
        else:
            mix = dict(pool_w=w["od_pool_w"][j], pool_scale=pool_scale[j], sgu_norm=sgu_norm[j],
                       sgu_w=w["od_sgu_w"][j], sgu_b=w["od_sgu_b"][j])
        return dict(mix, norm=w["mix_norm"][l] + zero, w_in=_join_cols(w_in), w_out=rows(w_out))

    def by_chip(dw):
        return dw.reshape(N_CHIPS, dw.shape[0] // N_CHIPS, dw.shape[1])

    bufs = {n: lax.empty(w[n].shape, F32) for n in _BIG}
    small_grads = {n: [None] * w[n].shape[0] for n in _WEIGHTS if n not in _BIG and n != "final_norm"}
    swapping, scattering, joining, group = [], [], [], []

    def finish_swap(after):
        tag, names, started = swapping.pop()
        local, from_sibling = _split_wait(f"swap_wait{tag}", _swap_plan, started[:4], after)
        halves = [_add_sibling(f"add_sibling{tag}_{n}", a, b, core) for (n, _), a, b in zip(names, local, from_sibling)]
        scattering.append((tag, names, _split_start(f"scatter_start{tag}", _scatter_plan, N_CHIPS - 1, halves, after)))
        return scattering[-1][2][4][0, 0]

    def finish_scatter(after):
        tag, names, started = scattering.pop(0)
        halves, got = _split_wait(f"scatter_wait{tag}", _scatter_plan, started[:4], after)
        for i, (n, j) in enumerate(names):
            bufs[n] = _add_chips(f"add_chips{tag}_{n}", halves[i], got[i], place, bufs[n], j)
        layers = [j for _, j in names]
        started = _split_start(f"join_start{tag}", _join_plan(layers), 1, [bufs[n] for n, _ in names], after, "self")
        for (n, _), b in zip(names, started[2]):
            bufs[n] = b
        joining.append((tag, names, layers, started))
        return started[4]

    def finish_join(after):
        tag, names, layers, started = joining.pop(0)
        joined = _split_wait(f"join_wait{tag}", _join_plan(layers), (started[0], started[1], [bufs[n] for n, _ in names], None),
                             after)[0]
        for (n, _), b in zip(names, joined):
            bufs[n] = b

    def small_done(l, block, g):
        if block == "mix":
            renamed = (dict(conv_w="ev_conv_w", q_gain="ev_q_norm", k_gain="ev_k_norm") if l % 2 == 0 else
                       dict(pool_w="od_pool_w", pool_scale="od_pool_scale", sgu_norm="od_sgu_norm", sgu_w="od_sgu_w",
                            sgu_b="od_sgu_b"))
            for key, n in renamed.items():
                small_grads[n][l // 2] = g[key]
        small_grads[block + "_norm"][l] = g["norm"]

    def mid(a):
        zero = 0.0
        if swapping:
            if scattering:
                zero = zero + finish_scatter(a)[0, 0]
            zero = zero + finish_swap(a)
        return zero

    def grads_ready(l, block, g, a):
        local = [_split_cols(g["w_in"]) if block == "mix" else g["w_in4"], by_chip(g["w_out"])]
        group.extend(zip(sharded(l, block), local))
        if block == "ffn2" or (block == "mix" and l > 0):
            return 0.0
        tag = f"{l}_{block}"
        names, local = [n for n, _ in group], [b for _, b in group]
        group.clear()
        shapes = [(N_CHIPS, b.shape[1] // 2, b.shape[2]) for b in local]
        swapping.append((tag, names, _split_start(f"swap_start{tag}", _swap_plan, 1, local, core, shapes)))
        return swapping[-1][2][4][0, 0]

    loss_part, grad_x, dfinal = _local_step(x, target, depth, weights_of, w["final_norm"], mid, grads_ready, small_done)
    loss = lax.psum(loss_part, ("x", "y", "c"))

    grads, updates = {}, {}

    def update(n, after):
        updates[n] = _adamw("adamw_" + n, w[n], grads[n] if n in grads else bufs[n], m[n], v[n], after)
        return updates[n][1]

    finish_swap(grad_x)
    behind = scattering[-1][2][4]
    while len(joining) > 0:
        finish_join(behind)
    behind = finish_scatter(behind)
    for n in ("od_w_in", "od_w_out"):
        behind = update(n, behind)
    finish_join(behind)
    for n in ("ffn2_w_in", "ffn2_w_out", "ev_w_in", "ev_w_out"):
        behind = update(n, behind)
    behind = finish_scatter(behind)
    small_grads = {n: jnp.stack(parts) for n, parts in small_grads.items()}
    small_grads["final_norm"] = dfinal
    names = list(small_grads)
    flat = jnp.concatenate([small_grads[n].reshape(-1) for n in names])
    total = flat.shape[0]
    flat = jnp.concatenate([flat, jnp.zeros((-total) % (N_DEV * 8 * LANES), F32)])
    summed = _allreduce_small(flat.reshape(-1, LANES), behind).reshape(-1)
    finish_join(summed)
    for n in ("ffn1_w_in", "ffn1_w_out"):
        behind = update(n, behind)
    off = 0
    for n in names:
        size = small_grads[n].size
        full_grad = summed[off:off + size].reshape(small_grads[n].shape)
        off += size
        if n in _SMALL_SHARDED:
            full_grad = lax.dynamic_slice_in_dim(full_grad, chip * LANES, LANES, axis=full_grad.ndim - 1)
        grads[n] = full_grad
    for n in _WEIGHTS:
        if n not in updates:
            behind = update(n, behind)
    return (loss, grad_x, *[updates[n][3] for n in _WEIGHTS], *[updates[n][0] for n in _WEIGHTS],
            *[updates[n][1] for n in _WEIGHTS], *[updates[n][2] for n in _WEIGHTS])
```

```python
import jax
import jax.numpy as jnp
from jax import lax
from jax.experimental import pallas as pl
from jax.experimental.pallas import tpu as pltpu

F32 = jnp.float32
_MXU = jnp.bfloat16
_ACT = jnp.bfloat16

D_MODEL = 1024
GRID_W = 64
HEAD_DIM = 64
N_Q_HEADS = 8
N_KV_HEADS = 2
Q_PER_KV = N_Q_HEADS // N_KV_HEADS
ATTN_WIDTH = N_Q_HEADS * HEAD_DIM
KV_WIDTH = N_KV_HEADS * HEAD_DIM
ROPE_THETA = 10000.0
CONV_WIDTH = D_MODEL // 2
POOL_RADII = (1, 2, 4, 8)
POOL_GROUP = 128
SGU_GROUP = 128
SGU_CHUNK = 128
N_GROUPS = 4
HALF = D_MODEL // 2
EPS = 1e-6
HALO = 8
LANES = 128
N_CHIPS = 4
N_DEV = 8

ADAM_LR = 0.001
ADAM_B1 = 0.9
ADAM_B2 = 0.999
ADAM_EPS = 1e-08
ADAM_WD = 0.01
ADAM_STEP = 10

_VMEM_LIMIT = 56 * 2 ** 20
_MESH = pl.DeviceIdType.MESH
_ANY = pl.BlockSpec(memory_space=pl.ANY)
_VMEM = pl.BlockSpec(memory_space=pltpu.VMEM)

_DN = {
    "nn": (((1,), (0,)), ((), ())),
    "nt": (((1,), (1,)), ((), ())),
    "tn": (((0,), (0,)), ((), ())),
}


def _params(*sem):
    return pltpu.CompilerParams(dimension_semantics=sem, vmem_limit_bytes=_VMEM_LIMIT)


def _tile(n, cap):
    best = None
    d = LANES
    while d <= min(n, cap):
        if n % d == 0:
            best = d
        d += LANES
    return best if best is not None else n


def _dot(a, b, mode="nn"):
    return lax.dot_general(a.astype(_MXU), b.astype(_MXU), _DN[mode], preferred_element_type=F32)


def _cat(*vals):
    vals = [v.astype(_MXU) for v in vals]
    return vals[0] if len(vals) == 1 else jnp.concatenate(vals, axis=1)


def _sigmoid(g):
    return 1.0 / (1.0 + jnp.exp(-g))


def _norm_rows(x, g):
    r = lax.rsqrt(jnp.mean(x * x, axis=-1, keepdims=True) + EPS)
    return (x * r) * g


def _swiglu(g, u):
    return (g * _sigmoid(g)) * u


_GELU_C = 0.7978845608028654


def _gelu(x):
    return 0.5 * x * (1.0 + jnp.tanh(_GELU_C * (x + 0.044715 * (x * x * x))))


def _gelu_grad(x):
    t = jnp.tanh(_GELU_C * (x + 0.044715 * (x * x * x)))
    return 0.5 * (1.0 + t) + 0.5 * x * (1.0 - t * t) * (_GELU_C * (1.0 + 3.0 * 0.044715 * (x * x)))


def _mm(name, grid, mode, a_ops, b_ops, e_ops, out_shape, out_specs, acc_shape, a_fn=_cat, b_fn=_cat, epi=None,
        n_outer=False, m_carried=False, b_pick=None, a_out=False):
    ni, nj, nk = grid
    na, nb, ne = len(a_ops), len(b_ops), len(e_ops)
    multi = isinstance(out_shape, (list, tuple))
    no = len(out_shape) if multi else 1

    def body(*refs):
        a_refs = refs[:na]
        b_refs = refs[na:na + nb]
        e_refs = refs[na + nb:na + nb + ne]
        o_refs = refs[na + nb + ne:na + nb + ne + no]
        a = a_fn(*[r[...] for r in a_refs])
        if a_out:
            o_refs[-1][...] = a.astype(o_refs[-1].dtype)
        if b_pick is None:
            b = b_fn(*[r[...] for r in b_refs])
        else:
            b = b_pick(b_refs, pl.program_id(1), pl.program_id(2))
        p = _dot(a, b, mode)

        def finish(acc):
            if epi is None:
                o_refs[0][...] = acc.astype(o_refs[0].dtype)
            else:
                epi(acc, [r[...] for r in e_refs], o_refs)

        if nk == 1:
            finish(p)
        else:
            acc_ref = refs[-1]
            k = pl.program_id(2)

            @pl.when(k == 0)
            def _():
                acc_ref[...] = p

            @pl.when((k > 0) & (k < nk - 1))
            def _():
                acc_ref[...] += p

            @pl.when(k == nk - 1)
            def _():
                finish(acc_ref[...] + p)

    ops = list(a_ops) + list(b_ops) + list(e_ops)
    if n_outer:
        def flip(spec):
            return pl.BlockSpec(spec.block_shape, lambda j, i, k, f=spec.index_map: f(i, j, k))

        grid = (nj, ni, nk)
        ops = [(a, flip(s)) for a, s in ops]
        out_specs = [flip(s) for s in out_specs] if multi else flip(out_specs)
    return pl.pallas_call(
        body, name=name, grid=grid,
        in_specs=[s for _, s in ops],
        out_specs=out_specs, out_shape=out_shape,
        scratch_shapes=[pltpu.VMEM(acc_shape, F32)] if nk > 1 else [],
        compiler_params=_params(*(("arbitrary",) * 3 if m_carried else ("parallel", "parallel", "arbitrary"))),
    )(*[a for a, _ in ops])


def _whole(a):
    return pl.BlockSpec(a.shape, lambda i, j, k: (0,) * a.ndim, pipeline_mode=pl.Buffered(1))


def _norm_bwd_epi(acc, e, o):
    xf, dres, g = e
    r = lax.rsqrt(jnp.mean(xf * xf, axis=-1, keepdims=True) + EPS)
    xhat = xf * r
    dgx = acc * g
    m = jnp.mean(dgx * xhat, axis=-1, keepdims=True)
    dx = dres + r * (dgx - xhat * m)
    o[0][...] = dx
    o[2][...] = dx.astype(o[2].dtype)
    part = jnp.sum(acc * xhat, axis=0, keepdims=True)
    i = pl.program_id(0)

    @pl.when(i == 0)
    def _():
        o[1][...] = part

    @pl.when(i > 0)
    def _():
        o[1][...] += part


def _norm_bwd_ops(x, dres, gain, tm):
    t, d = x.shape
    row = pl.BlockSpec((tm, d), lambda i, j, k: (i, 0))
    vec = pl.BlockSpec((1, d), lambda i, j, k: (0, 0))
    return ([(x, row), (dres, row), (gain.reshape(1, d), vec)],
            [jax.ShapeDtypeStruct((t, d), F32), jax.ShapeDtypeStruct((1, d), F32), jax.ShapeDtypeStruct((t, d), _ACT)],
            [row, vec, row])


def _rows(t):
    return _tile(t, 512)


def _rmsnorm_fwd(name, x, gain):
    t, d = x.shape
    tr = _rows(t)

    def body(x_ref, g_ref, h_ref):
        xf = x_ref[...]
        r = lax.rsqrt(jnp.mean(xf * xf, axis=-1, keepdims=True) + EPS)
        h_ref[...] = ((xf * r) * g_ref[...]).astype(h_ref.dtype)

    return pl.pallas_call(
        body, name=name, grid=(t // tr,),
        in_specs=[pl.BlockSpec((tr, d), lambda i: (i, 0)), pl.BlockSpec((1, d), lambda i: (0, 0))],
        out_specs=pl.BlockSpec((tr, d), lambda i: (i, 0)),
        out_shape=jax.ShapeDtypeStruct((t, d), _ACT),
        compiler_params=_params("parallel"),
    )(x, gain.reshape(1, d))


def _rmsnorm_bwd(name, dh, x, gain, dres):
    t, d = x.shape
    tr = _rows(t)

    def body(dh_ref, x_ref, g_ref, dres_ref, dx_ref, dg_ref):
        i = pl.program_id(0)
        xf = x_ref[...]
        r = lax.rsqrt(jnp.mean(xf * xf, axis=-1, keepdims=True) + EPS)
        xhat = xf * r
        dy = dh_ref[...].astype(F32)
        dgx = dy * g_ref[...]
        m = jnp.mean(dgx * xhat, axis=-1, keepdims=True)
        dx_ref[...] = dres_ref[...] + r * (dgx - xhat * m)
        part = jnp.sum(dy * xhat, axis=0, keepdims=True)

        @pl.when(i == 0)
        def _():
            dg_ref[...] = part

        @pl.when(i > 0)
        def _():
            dg_ref[...] += part

    row = pl.BlockSpec((tr, d), lambda i: (i, 0))
    vec = pl.BlockSpec((1, d), lambda i: (0, 0))
    dx, dg = pl.pallas_call(
        body, name=name, grid=(t // tr,),
        in_specs=[row, row, vec, row],
        out_specs=(row, vec),
        out_shape=(jax.ShapeDtypeStruct((t, d), F32), jax.ShapeDtypeStruct((1, d), F32)),
        compiler_params=_params("arbitrary"),
    )(dh, x, gain.reshape(1, d), dres)
    return dx, dg.reshape(d)


def _final_loss(name, x, gain, target):
    t, d = x.shape
    tr = _rows(t)

    def body(x_ref, g_ref, t_ref, dx_ref, dg_ref, loss_ref, lo_ref):
        i = pl.program_id(0)
        xf = x_ref[...]
        r = lax.rsqrt(jnp.mean(xf * xf, axis=-1, keepdims=True) + EPS)
        xhat = xf * r
        g = g_ref[...]
        err = xhat * g - t_ref[...]
        lpart = 0.5 * jnp.sum(jnp.mean(err * err, axis=-1, keepdims=True), axis=0, keepdims=True)
        dy = err * (1.0 / d)
        dgx = dy * g
        m = jnp.mean(dgx * xhat, axis=-1, keepdims=True)
        dx = r * (dgx - xhat * m)
        dx_ref[...] = dx
        lo_ref[...] = dx.astype(lo_ref.dtype)
        part = jnp.sum(dy * xhat, axis=0, keepdims=True)
        lrow = jnp.broadcast_to(lpart, (1, LANES))

        @pl.when(i == 0)
        def _():
            dg_ref[...] = part
            loss_ref[...] = lrow

        @pl.when(i > 0)
        def _():
            dg_ref[...] += part
            loss_ref[...] += lrow

    row = pl.BlockSpec((tr, d), lambda i: (i, 0))
    vec = pl.BlockSpec((1, d), lambda i: (0, 0))
    dx, dg, loss, dx_lo = pl.pallas_call(
        body, name=name, grid=(t // tr,),
        in_specs=[row, vec, row],
        out_specs=(row, vec, pl.BlockSpec((1, LANES), lambda i: (0, 0)), row),
        out_shape=(jax.ShapeDtypeStruct((t, d), F32), jax.ShapeDtypeStruct((1, d), F32),
                   jax.ShapeDtypeStruct((1, LANES), F32), jax.ShapeDtypeStruct((t, d), _ACT)),
        compiler_params=_params("arbitrary"),
    )(x, gain.reshape(1, d), target)
    return loss[0, 0], dx, dx_lo, dg.reshape(d)


_FFN_TILES = dict(in_tm=1024, out_tm=512, dact_tm=1024, dwout_tk=1024, dh_tm=512, dwin_tk=2048)


def _ffn_tiles(layer, which):
    return _FFN_TILES


def _ffn_fwd(tag, x, gain, w_in4, w_out, cfg):
    t, d = x.shape
    fs = w_in4.shape[2]
    f = 2 * fs
    tm = _tile(t, cfg["in_tm"])
    gain = gain.reshape(1, d)
    gu, h = _mm(
        tag + "_in", (t // tm, N_CHIPS, 1), "nn",
        [(x, pl.BlockSpec((tm, d), lambda i, j, k: (i, 0))), (gain, pl.BlockSpec((1, d), lambda i, j, k: (0, 0)))],
        [(w_in4, pl.BlockSpec((None, d, fs), lambda i, j, k: (j, 0, 0)))], [],
        [jax.ShapeDtypeStruct((2, t, f), _ACT), jax.ShapeDtypeStruct((t, d), _ACT)],
        [pl.BlockSpec((None, tm, fs), lambda i, j, k: (j // 2, i, j % 2)), pl.BlockSpec((tm, d), lambda i, j, k: (i, 0))],
        None, a_fn=_norm_rows, m_carried=True, a_out=True)
    tm2 = _tile(t, cfg["out_tm"])

    def epi(acc, e, o):
        o[0][...] = e[0] + 0.5 * acc

    x_out = _mm(
        tag + "_out", (t // tm2, 1, 1), "nn",
        [(gu, pl.BlockSpec((None, tm2, f), lambda i, j, k: (0, i, 0))),
         (gu, pl.BlockSpec((None, tm2, f), lambda i, j, k: (1, i, 0)))],
        [(w_out, pl.BlockSpec((f, d), lambda i, j, k: (0, 0)))],
        [(x, pl.BlockSpec((tm2, d), lambda i, j, k: (i, 0)))],
        jax.ShapeDtypeStruct((t, d), F32),
        pl.BlockSpec((tm2, d), lambda i, j, k: (i, 0)), None,
        a_fn=_swiglu, epi=epi)
    return x_out, (x, h, gu)


def _ffn_bwd(tag, dxo, dxo_lo, saved, gain, w_in4, w_out, cfg, mid, ready):
    x, h, gu = saved
    t, d = x.shape
    fs = w_in4.shape[2]
    f = 2 * fs
    tm = _tile(t, cfg["dact_tm"])
    tk = _tile(t, cfg["dwout_tk"])

    def epi_act(acc, e, o):
        g, u = e
        da = (0.5 * acc).astype(g.dtype)
        sig = _sigmoid(g)
        silu = g * sig
        o[0][0] = (da * u * (sig + silu * (1.0 - sig))).astype(o[0].dtype)
        o[0][1] = (da * silu).astype(o[0].dtype)

    dgu = _mm(
        tag + "_dact", (t // tm, 2, 1), "nt",
        [(dxo_lo, pl.BlockSpec((tm, d), lambda i, j, k: (i, 0)))],
        [(w_out, _whole(w_out))],
        [(gu, pl.BlockSpec((None, tm, fs), lambda i, j, k: (0, i, j))),
         (gu, pl.BlockSpec((None, tm, fs), lambda i, j, k: (1, i, j)))],
        jax.ShapeDtypeStruct((2, t, f), _ACT),
        pl.BlockSpec((2, tm, fs), lambda i, j, k: (0, i, j)), None, epi=epi_act,
        b_pick=lambda b, j, k: b[0][pl.ds(pl.multiple_of(j * fs, LANES), fs), :])
    gain = gain + mid(dgu)

    def epi_half(acc, e, o):
        o[0][...] = (0.5 * acc).astype(o[0].dtype)

    dw_out = _mm(
        tag + "_dwout", (2, 1, t // tk), "tn",
        [(gu, pl.BlockSpec((None, tk, fs), lambda i, j, k: (0, k, i))),
         (gu, pl.BlockSpec((None, tk, fs), lambda i, j, k: (1, k, i)))],
        [(dxo_lo, pl.BlockSpec((tk, d), lambda i, j, k: (k, 0)))], [],
        jax.ShapeDtypeStruct((f, d), _ACT),
        pl.BlockSpec((fs, d), lambda i, j, k: (i, 0)), (fs, d),
        a_fn=_swiglu, epi=epi_half)
    tk = _tile(t, cfg["dwin_tk"])
    dw_in4 = _mm(
        tag + "_dwin", (1, N_CHIPS, t // tk), "tn",
        [(h, pl.BlockSpec((tk, d), lambda i, j, k: (k, 0)))],
        [(dgu, pl.BlockSpec((None, tk, fs), lambda i, j, k: (j // 2, k, j % 2)))], [],
        jax.ShapeDtypeStruct((N_CHIPS, d, fs), _ACT),
        pl.BlockSpec((None, d, fs), lambda i, j, k: (j, 0, 0)), (d, fs))
    gain = gain + ready(dict(w_in4=dw_in4, w_out=dw_out), dw_in4)
    tm = _tile(t, cfg["dh_tm"])
    e_ops, shapes, specs = _norm_bwd_ops(x, dxo, gain, tm)
    dx, dgain, dx_lo = _mm(
        tag + "_dh", (t // tm, 1, 2), "nt",
        [(dgu, pl.BlockSpec((None, tm, f), lambda i, j, k: (k, i, 0)))],
        [(w_in4, _whole(w_in4))], e_ops, shapes, specs, (tm, d), epi=_norm_bwd_epi, m_carried=True,
        b_pick=lambda b, j, k: jnp.concatenate([b[0][2 * k], b[0][2 * k + 1]], axis=1))
    return dx, dx_lo, dgain.reshape(d)


_MIX_TILES = dict(tm=1024, dwout_tk=2048, dwin_tk=1024)


def _proj_in(tag, h, w_in):
    t, d = h.shape
    n = w_in.shape[1]
    tm = _tile(t, _MIX_TILES["tm"])
    return _mm(
        tag + "_in", (t // tm, 1, 1), "nn",
        [(h, pl.BlockSpec((tm, d), lambda i, j, k: (i, 0)))],
        [(w_in, pl.BlockSpec((d, n), lambda i, j, k: (0, 0)))], [],
        jax.ShapeDtypeStruct((t, n), _ACT),
        pl.BlockSpec((tm, n), lambda i, j, k: (i, 0)), None)


def _proj_out(tag, x, parts, w_out):
    t, d = x.shape
    tm = _tile(t, _MIX_TILES["tm"])

    def epi(acc, e, o):
        o[0][...] = e[0] + acc

    return _mm(
        tag + "_out", (t // tm, 1, 1), "nn",
        [(p, pl.BlockSpec((tm, p.shape[1]), lambda i, j, k: (i, 0))) for p in parts],
        [(w_out, pl.BlockSpec(w_out.shape, lambda i, j, k: (0, 0)))],
        [(x, pl.BlockSpec((tm, d), lambda i, j, k: (i, 0)))],
        jax.ShapeDtypeStruct((t, d), F32),
        pl.BlockSpec((tm, d), lambda i, j, k: (i, 0)), None, epi=epi)


def _proj_out_bwd(tag, dxo, parts, w_out):
    t, d = dxo.shape
    mix = w_out.shape[0]
    tm = _tile(t, _MIX_TILES["tm"])
    tk = _tile(t, _MIX_TILES["dwout_tk"])
    d_mix = _mm(
        tag + "_dmix", (t // tm, 1, 1), "nt",
        [(dxo, pl.BlockSpec((tm, d), lambda i, j, k: (i, 0)))],
        [(w_out, pl.BlockSpec((mix, d), lambda i, j, k: (0, 0)))], [],
        jax.ShapeDtypeStruct((t, mix), F32),
        pl.BlockSpec((tm, mix), lambda i, j, k: (i, 0)), None)
    dw_out = _mm(
        tag + "_dwout", (1, 1, t // tk), "tn",
        [(p, pl.BlockSpec((tk, p.shape[1]), lambda i, j, k: (k, 0))) for p in parts],
        [(dxo, pl.BlockSpec((tk, d), lambda i, j, k: (k, 0)))], [],
        jax.ShapeDtypeStruct((mix, d), _ACT),
        pl.BlockSpec((mix, d), lambda i, j, k: (0, 0)), (mix, d))
    return d_mix, dw_out


def _proj_in_bwd(tag, h, dparts, w_in, x, dres, gain, ready):
    t, d = h.shape
    n = w_in.shape[1]
    tm = _tile(t, _MIX_TILES["tm"])
    tk = _tile(t, _MIX_TILES["dwin_tk"])
    dw_in = _mm(
        tag + "_dwin", (1, 1, t // tk), "tn",
        [(h, pl.BlockSpec((tk, d), lambda i, j, k: (k, 0)))],
        [(p, pl.BlockSpec((tk, p.shape[1]), lambda i, j, k: (k, 0))) for p in dparts], [],
        jax.ShapeDtypeStruct((d, n), _ACT),
        pl.BlockSpec((d, n), lambda i, j, k: (0, 0)), (d, n))
    e_ops, shapes, specs = _norm_bwd_ops(x, dres, gain + ready(dw_in), tm)
    dx, dgain, dx_lo = _mm(
        tag + "_dh", (t // tm, 1, 1), "nt",
        [(p, pl.BlockSpec((tm, p.shape[1]), lambda i, j, k: (i, 0))) for p in dparts],
        [(w_in, pl.BlockSpec((d, n), lambda i, j, k: (0, 0)))], e_ops, shapes, specs, None,
        epi=_norm_bwd_epi, m_carried=True)
    return dx, dx_lo, dgain.reshape(d)


def _shifted(pad_ref, val, s):
    pad_ref[pl.ds(HALO, s), :] = val
    return pad_ref[pl.ds(HALO - 1, s), :], pad_ref[pl.ds(HALO + 1, s), :]


def _zero_halo(pad_ref, s):
    z = jnp.zeros((HALO, pad_ref.shape[1]), F32)
    pad_ref[pl.ds(0, HALO), :] = z
    pad_ref[pl.ds(HALO + s, HALO), :] = z


def _conv_fwd(tag, proj, conv_w, nb, s):
    t = proj.shape[0]
    ncb = CONV_WIDTH // LANES

    def body(gb_ref, gc_ref, hc_ref, w_ref, a_ref, pad_ref):
        _zero_halo(pad_ref, s)
        cg = gc_ref[...].astype(F32) * hc_ref[...].astype(F32)
        prev, nxt = _shifted(pad_ref, cg, s)
        w = w_ref[...]
        conv = prev * w[0:1, :] + cg * w[1:2, :] + nxt * w[2:3, :]
        a_ref[...] = (gb_ref[...].astype(F32) * conv).astype(a_ref.dtype)

    def col(off):
        return pl.BlockSpec((s, LANES), lambda b, c: (b, off + c))

    return pl.pallas_call(
        body, name=tag + "_conv", grid=(nb, ncb),
        in_specs=[col(0), col(ncb), col(2 * ncb), pl.BlockSpec((3, LANES), lambda b, c: (0, c))],
        out_specs=col(0),
        out_shape=jax.ShapeDtypeStruct((t, CONV_WIDTH), _ACT),
        scratch_shapes=[pltpu.VMEM((s + 2 * HALO, LANES), F32)],
        compiler_params=_params("parallel", "parallel"),
    )(proj, proj, proj, conv_w)


def _conv_bwd(tag, proj, conv_w, d_mix, nb, s):
    t = proj.shape[0]
    ncb = CONV_WIDTH // LANES

    def body(gb_ref, gc_ref, hc_ref, w_ref, da_ref, dgb_ref, dgc_ref, dhc_ref, dw_ref, pad_ref):
        b = pl.program_id(1)
        _zero_halo(pad_ref, s)
        gb = gb_ref[...].astype(F32)
        gc = gc_ref[...].astype(F32)
        hc = hc_ref[...].astype(F32)
        w = w_ref[...]
        da = da_ref[...]
        cg = gc * hc
        prev, nxt = _shifted(pad_ref, cg, s)
        conv = prev * w[0:1, :] + cg * w[1:2, :] + nxt * w[2:3, :]
        dgb_ref[...] = (da * conv).astype(dgb_ref.dtype)
        dconv = da * gb
        dw = jnp.concatenate([
            jnp.sum(dconv * prev, axis=0, keepdims=True),
            jnp.sum(dconv * cg, axis=0, keepdims=True),
            jnp.sum(dconv * nxt, axis=0, keepdims=True)], axis=0)
        dprev, dnxt = _shifted(pad_ref, dconv, s)
        dcg = dnxt * w[0:1, :] + dconv * w[1:2, :] + dprev * w[2:3, :]
        dgc_ref[...] = (dcg * hc).astype(dgc_ref.dtype)
        dhc_ref[...] = (dcg * gc).astype(dhc_ref.dtype)

        @pl.when(b == 0)
        def _():
            dw_ref[...] = dw

        @pl.when(b > 0)
        def _():
            dw_ref[...] += dw

    def col(off):
        return pl.BlockSpec((s, LANES), lambda c, b: (b, off + c))

    wspec = pl.BlockSpec((3, LANES), lambda c, b: (0, c))
    act = jax.ShapeDtypeStruct((t, CONV_WIDTH), _ACT)
    return pl.pallas_call(
        body, name=tag + "_dconv", grid=(ncb, nb),
        in_specs=[col(0), col(ncb), col(2 * ncb), wspec, col(0)],
        out_specs=(col(0), col(0), col(0), wspec),
        out_shape=(act, act, act, jax.ShapeDtypeStruct((3, CONV_WIDTH), F32)),
        scratch_shapes=[pltpu.VMEM((s + 2 * HALO, LANES), F32)],
        compiler_params=_params("parallel", "arbitrary"),
    )(proj, proj, proj, conv_w, d_mix)


def _rope_tables(s):
    rows = s // GRID_W
    r_idx, c_idx = jnp.meshgrid(jnp.arange(rows), jnp.arange(GRID_W), indexing="ij")
    r_idx = r_idx.reshape(-1).astype(F32)
    c_idx = c_idx.reshape(-1).astype(F32)
    n_freq = HEAD_DIM // 4
    inv = ROPE_THETA ** (-jnp.arange(n_freq, dtype=F32) / n_freq)
    ang = jnp.concatenate([r_idx[:, None] * inv, c_idx[:, None] * inv], axis=-1)
    cos = jnp.repeat(jnp.cos(ang), 2, axis=1)
    sin = jnp.repeat(jnp.sin(ang), 2, axis=1)
    sign = jnp.where(jnp.arange(HEAD_DIM) % 2 == 0, -1.0, 1.0).astype(F32)
    return jnp.tile(cos, (1, LANES // HEAD_DIM)), jnp.tile(sin * sign, (1, LANES // HEAD_DIM))


def _head_ones():
    i = jnp.arange(LANES) // HEAD_DIM
    return (i[:, None] == i[None, :]).astype(jnp.bfloat16)


def _head_sum(v, ones):
    outs = []
    for j in range(v.shape[1] // LANES):
        c = v[:, j * LANES:(j + 1) * LANES]
        hi = c.astype(jnp.bfloat16)
        lo = (c - hi.astype(F32)).astype(jnp.bfloat16)
        outs.append(jnp.dot(hi, ones, preferred_element_type=F32) + jnp.dot(lo, ones, preferred_element_type=F32))
    return outs[0] if len(outs) == 1 else jnp.concatenate(outs, axis=1)


def _pair_swap(v):
    outs = []
    for j in range(v.shape[1] // LANES):
        c = v[:, j * LANES:(j + 1) * LANES]
        lane = lax.broadcasted_iota(jnp.int32, c.shape, 1)
        outs.append(jnp.where(lane % 2 == 0, pltpu.roll(c, LANES - 1, 1), pltpu.roll(c, 1, 1)))
    return outs[0] if len(outs) == 1 else jnp.concatenate(outs, axis=1)


def _wide(tab, width):
    return tab if width == LANES else jnp.concatenate([tab] * (width // LANES), axis=1)


_QK_SCALE = HEAD_DIM ** -0.5


def _qk_fwd(tag, proj, q_gain, k_gain, cos, sin, nb, s):
    t = proj.shape[0]
    tr = _tile(s, 512)
    ns = s // tr
    q_off = 3 * CONV_WIDTH // ATTN_WIDTH
    k_off = (3 * CONV_WIDTH + ATTN_WIDTH) // KV_WIDTH

    def body(q_ref, k_ref, qg_ref, kg_ref, cos_ref, sin_ref, ones_ref, qo_ref, ko_ref):
        ones = ones_ref[...]
        for src, g_ref, dst, mult in ((q_ref, qg_ref, qo_ref, _QK_SCALE), (k_ref, kg_ref, ko_ref, 1.0)):
            v = src[...].astype(F32)
            w = v.shape[1]
            r = lax.rsqrt(_head_sum(v * v, ones) * (1.0 / HEAD_DIM) + EPS)
            vn = (v * r) * g_ref[...]
            rot = vn * _wide(cos_ref[...], w) + _pair_swap(vn) * _wide(sin_ref[...], w)
            dst[...] = (rot * mult).astype(dst.dtype)

    tab = pl.BlockSpec((tr, LANES), lambda i: (i % ns, 0))
    return pl.pallas_call(
        body, name=tag + "_qk", grid=(t // tr,),
        in_specs=[pl.BlockSpec((tr, ATTN_WIDTH), lambda i: (i, q_off)),
                  pl.BlockSpec((tr, KV_WIDTH), lambda i: (i, k_off)),
                  pl.BlockSpec((1, ATTN_WIDTH), lambda i: (0, 0)),
                  pl.BlockSpec((1, KV_WIDTH), lambda i: (0, 0)),
                  tab, tab, pl.BlockSpec((LANES, LANES), lambda i: (0, 0))],
        out_specs=(pl.BlockSpec((tr, ATTN_WIDTH), lambda i: (i, 0)),
                   pl.BlockSpec((tr, KV_WIDTH), lambda i: (i, 0))),
        out_shape=(jax.ShapeDtypeStruct((t, ATTN_WIDTH), _ACT), jax.ShapeDtypeStruct((t, KV_WIDTH), _ACT)),
        compiler_params=_params("parallel"),
    )(proj, proj, jnp.tile(q_gain, N_Q_HEADS).reshape(1, ATTN_WIDTH),
      jnp.tile(k_gain, N_KV_HEADS).reshape(1, KV_WIDTH), cos, sin, _head_ones())


def _qk_bwd(tag, proj, q_gain, k_gain, cos, sin, dq_rot, dk_rot, nb, s):
    t = proj.shape[0]
    tr = _tile(s, 512)
    ns = s // tr
    q_off = 3 * CONV_WIDTH // ATTN_WIDTH
    k_off = (3 * CONV_WIDTH + ATTN_WIDTH) // KV_WIDTH

    def body(q_ref, k_ref, qg_ref, kg_ref, cos_ref, sin_ref, ones_ref, dqr_ref, dkr_ref,
             dq_ref, dk_ref, dqg_ref, dkg_ref):
        i = pl.program_id(0)
        ones = ones_ref[...]
        for src, g_ref, dr_ref, dst, dg_ref, mult in ((q_ref, qg_ref, dqr_ref, dq_ref, dqg_ref, _QK_SCALE),
                                                      (k_ref, kg_ref, dkr_ref, dk_ref, dkg_ref, 1.0)):
            v = src[...].astype(F32)
            w = v.shape[1]
            r = lax.rsqrt(_head_sum(v * v, ones) * (1.0 / HEAD_DIM) + EPS)
            xhat = v * r
            dr = dr_ref[...] * mult
            dvn = dr * _wide(cos_ref[...], w) + _pair_swap(dr * _wide(sin_ref[...], w))
            dgx = dvn * g_ref[...]
            m = _head_sum(dgx * xhat, ones) * (1.0 / HEAD_DIM)
            dst[...] = (r * (dgx - xhat * m)).astype(dst.dtype)
            part = jnp.sum(dvn * xhat, axis=0, keepdims=True)
            fold = part[:, 0:HEAD_DIM]
            for hh in range(1, w // HEAD_DIM):
                fold = fold + part[:, hh * HEAD_DIM:(hh + 1) * HEAD_DIM]

            @pl.when(i == 0)
            def _():
                dg_ref[...] = fold

            @pl.when(i > 0)
            def _():
                dg_ref[...] += fold

    tab = pl.BlockSpec((tr, LANES), lambda i: (i % ns, 0))
    qrow = pl.BlockSpec((tr, ATTN_WIDTH), lambda i: (i, 0))
    krow = pl.BlockSpec((tr, KV_WIDTH), lambda i: (i, 0))
    gvec = pl.BlockSpec((1, HEAD_DIM), lambda i: (0, 0))
    dq, dk, dqg, dkg = pl.pallas_call(
        body, name=tag + "_dqk", grid=(t // tr,),
        in_specs=[pl.BlockSpec((tr, ATTN_WIDTH), lambda i: (i, q_off)),
                  pl.BlockSpec((tr, KV_WIDTH), lambda i: (i, k_off)),
                  pl.BlockSpec((1, ATTN_WIDTH), lambda i: (0, 0)),
                  pl.BlockSpec((1, KV_WIDTH), lambda i: (0, 0)),
                  tab, tab, pl.BlockSpec((LANES, LANES), lambda i: (0, 0)), qrow, krow],
        out_specs=(qrow, krow, gvec, gvec),
        out_shape=(jax.ShapeDtypeStruct((t, ATTN_WIDTH), _ACT), jax.ShapeDtypeStruct((t, KV_WIDTH), _ACT),
                   jax.ShapeDtypeStruct((1, HEAD_DIM), F32), jax.ShapeDtypeStruct((1, HEAD_DIM), F32)),
        compiler_params=_params("arbitrary"),
    )(proj, proj, jnp.tile(q_gain, N_Q_HEADS).reshape(1, ATTN_WIDTH),
      jnp.tile(k_gain, N_KV_HEADS).reshape(1, KV_WIDTH), cos, sin, _head_ones(), dq_rot, dk_rot)
    return dq, dk, dqg.reshape(HEAD_DIM), dkg.reshape(HEAD_DIM)


def _head(v, h):
    return v[:, h * HEAD_DIM:(h + 1) * HEAD_DIM]


def _attn_fwd(tag, q, k, proj, nb, s):
    t = q.shape[0]
    tq = _tile(s, 512)
    nq = s // tq
    v_off = (3 * CONV_WIDTH + ATTN_WIDTH + KV_WIDTH) // KV_WIDTH

    def body(q_ref, k_ref, v_ref, o_ref, lse_ref):
        qv = q_ref[...]
        kv = k_ref[...]
        vv = v_ref[...]
        for h in range(N_Q_HEADS):
            j = h // Q_PER_KV
            sc = _dot(_head(qv, h), _head(kv, j), "nt")
            m = jnp.max(sc, axis=-1, keepdims=True)
            e = jnp.exp(sc - m)
            l = jnp.sum(e, axis=-1, keepdims=True)
            o = _dot(e, _head(vv, j)) * (1.0 / l)
            o_ref[:, h * HEAD_DIM:(h + 1) * HEAD_DIM] = o.astype(o_ref.dtype)
            lse_ref[:, h:h + 1] = m + jnp.log(l)

    return pl.pallas_call(
        body, name=tag + "_attn", grid=(nb, nq),
        in_specs=[pl.BlockSpec((tq, ATTN_WIDTH), lambda b, i: (b * nq + i, 0)),
                  pl.BlockSpec((s, KV_WIDTH), lambda b, i: (b, 0)),
                  pl.BlockSpec((s, KV_WIDTH), lambda b, i: (b, v_off))],
        out_specs=(pl.BlockSpec((tq, ATTN_WIDTH), lambda b, i: (b * nq + i, 0)),
                   pl.BlockSpec((tq, N_Q_HEADS), lambda b, i: (b * nq + i, 0))),
        out_shape=(jax.ShapeDtypeStruct((t, ATTN_WIDTH), _ACT), jax.ShapeDtypeStruct((t, N_Q_HEADS), F32)),
        compiler_params=_params("parallel", "parallel"),
    )(q, k, proj)


def _attn_bwd(tag, q, k, proj, o, lse, d_mix, nb, s):
    t = q.shape[0]
    tq = _tile(s, 512)
    nq = s // tq
    v_off = (3 * CONV_WIDTH + ATTN_WIDTH + KV_WIDTH) // KV_WIDTH

    def body(q_ref, k_ref, v_ref, o_ref, lse_ref, do_ref, dq_ref, dk_ref, dv_ref):
        i = pl.program_id(1)

        @pl.when(i == 0)
        def _():
            dk_ref[...] = jnp.zeros_like(dk_ref)
            dv_ref[...] = jnp.zeros_like(dv_ref)

        qv = q_ref[...]
        kv = k_ref[...]
        vv = v_ref[...]
        ov = o_ref[...].astype(F32)
        dov = do_ref[...]
        lse = lse_ref[...]
        for h in range(N_Q_HEADS):
            j = h // Q_PER_KV
            cols = slice(j * HEAD_DIM, (j + 1) * HEAD_DIM)
            qh = _head(qv, h)
            kj = _head(kv, j)
            doh = _head(dov, h)
            sc = _dot(qh, kj, "nt")
            p = jnp.exp(sc - lse[:, h:h + 1])
            dp = _dot(doh, _head(vv, j), "nt")
            delta = jnp.sum(doh * _head(ov, h), axis=-1, keepdims=True)
            ds = p * (dp - delta)
            dv_ref[:, cols] += _dot(p, doh, "tn")
            dk_ref[:, cols] += _dot(ds, qh, "tn")
            dq_ref[:, h * HEAD_DIM:(h + 1) * HEAD_DIM] = _dot(ds, kj)

    qrow = pl.BlockSpec((tq, ATTN_WIDTH), lambda b, i: (b * nq + i, 0))
    kvrow = pl.BlockSpec((s, KV_WIDTH), lambda b, i: (b, 0))
    return pl.pallas_call(
        body, name=tag + "_dattn", grid=(nb, nq),
        in_specs=[qrow, kvrow, pl.BlockSpec((s, KV_WIDTH), lambda b, i: (b, v_off)), qrow,
                  pl.BlockSpec((tq, N_Q_HEADS), lambda b, i: (b * nq + i, 0)),
                  pl.BlockSpec((tq, ATTN_WIDTH), lambda b, i: (b * nq + i, 1))],
        out_specs=(qrow, kvrow, kvrow),
        out_shape=(jax.ShapeDtypeStruct((t, ATTN_WIDTH), F32), jax.ShapeDtypeStruct((t, KV_WIDTH), F32),
                   jax.ShapeDtypeStruct((t, KV_WIDTH), F32)),
        compiler_params=_params("parallel", "arbitrary"),
    )(q, k, proj, o, lse, d_mix)


def _even_fwd(tag, x, p, cos, sin, nb, s):
    h = _rmsnorm_fwd(tag + "_norm", x, p["norm"])
    proj = _proj_in(tag, h, p["w_in"])
    a = _conv_fwd(tag, proj, p["conv_w"], nb, s)
    q, k = _qk_fwd(tag, proj, p["q_gain"], p["k_gain"], cos, sin, nb, s)
    o, lse = _attn_fwd(tag, q, k, proj, nb, s)
    x_out = _proj_out(tag, x, [a, o], p["w_out"])
    return x_out, (x, h, proj, a, q, k, o, lse)


def _even_bwd(tag, dxo, dxo_lo, saved, p, cos, sin, nb, s, mid, ready):
    x, h, proj, a, q, k, o, lse = saved
    d_mix, dw_out = _proj_out_bwd(tag, dxo_lo, [a, o], p["w_out"])
    dgb, dgc, dhc, dconv_w = _conv_bwd(tag, proj, p["conv_w"] + mid(d_mix), d_mix, nb, s)
    dq_rot, dk_rot, dv = _attn_bwd(tag, q, k, proj, o, lse, d_mix, nb, s)
    dq, dk, dq_gain, dk_gain = _qk_bwd(tag, proj, p["q_gain"], p["k_gain"], cos, sin, dq_rot, dk_rot, nb, s)
    dx, dx_lo, dnorm = _proj_in_bwd(tag, h, [dgb, dgc, dhc, dq, dk, dv], p["w_in"], x, dxo, p["norm"],
                                    lambda dw_in: ready(dict(w_in=dw_in, w_out=dw_out), dw_in))
    return dx, dx_lo, dict(norm=dnorm, conv_w=dconv_w, q_gain=dq_gain, k_gain=dk_gain)


def _window(pad_ref, val, r, s):
    pad_ref[pl.ds(HALO, s), :] = val
    acc = val
    for d in range(1, r + 1):
        acc = acc + pad_ref[pl.ds(HALO - d, s), :] + pad_ref[pl.ds(HALO + d, s), :]
    return acc


def _count(r, s):
    t = lax.broadcasted_iota(jnp.int32, (s, 1), 0)
    return (jnp.minimum(t + r, s - 1) - jnp.maximum(t - r, 0) + 1).astype(F32)


def _sgu_chunk(u_ref, v_ref, norm, ws_ref, bt, rows):
    uu = u_ref[rows, :].astype(F32)
    vv = v_ref[rows, :].astype(F32)
    gu = _gelu(uu)
    gv = _gelu(vv)
    r = lax.rsqrt(jnp.mean(gv * gv, axis=-1, keepdims=True) + EPS)
    xhat = gv * r
    vn = xhat * norm
    mixed = []
    for g in range(N_GROUPS):
        cols = slice(g * SGU_GROUP, (g + 1) * SGU_GROUP)
        mixed.append(_dot(ws_ref[g], vn[:, cols]) + bt[:, g:g + 1])
    return uu, vv, gu, r, xhat, vn, mixed


def _odd_core_fwd(tag, proj, p, nb, s):
    t = proj.shape[0]
    nchunk = s // SGU_CHUNK

    def body(p_ref, u_ref, v_ref, pw_ref, ps_ref, sn_ref, ws_ref, bt_ref, mix_ref, pad_ref):
        _zero_halo(pad_ref, s)
        for g, r in enumerate(POOL_RADII):
            cols = slice(g * POOL_GROUP, (g + 1) * POOL_GROUP)
            pg = p_ref[:, cols].astype(F32)
            pooled = _window(pad_ref, pg, r, s) / _count(r, s) - pg
            mix_ref[:, cols] = (_dot(pooled, pw_ref[g]) * ps_ref[:, cols]).astype(mix_ref.dtype)
        norm = sn_ref[...]
        bt = bt_ref[...]

        def chunk(n, carry):
            rows = pl.ds(pl.multiple_of(n * SGU_CHUNK, SGU_CHUNK), SGU_CHUNK)
            _, _, gu, _, _, _, mixed = _sgu_chunk(u_ref, v_ref, norm, ws_ref, bt, rows)
            for g in range(N_GROUPS):
                cols = slice(g * SGU_GROUP, (g + 1) * SGU_GROUP)
                mix_ref[rows, HALF + g * SGU_GROUP:HALF + (g + 1) * SGU_GROUP] = (
                    gu[:, cols] * mixed[g]).astype(mix_ref.dtype)
            return carry

        lax.fori_loop(0, nchunk, chunk, 0)

    def col(j):
        return pl.BlockSpec((s, HALF), lambda b: (b, j))

    def whole(a):
        return pl.BlockSpec(a.shape, lambda b: (0,) * a.ndim)

    consts = [p["pool_w"], p["pool_scale"].reshape(1, HALF), p["sgu_norm"].reshape(1, HALF),
              p["sgu_w"], p["sgu_b"].T]
    return pl.pallas_call(
        body, name=tag + "_core", grid=(nb,),
        in_specs=[col(0), col(1), col(2)] + [whole(a) for a in consts],
        out_specs=pl.BlockSpec((s, D_MODEL), lambda b: (b, 0)),
        out_shape=jax.ShapeDtypeStruct((t, D_MODEL), _ACT),
        scratch_shapes=[pltpu.VMEM((s + 2 * HALO, POOL_GROUP), F32)],
        compiler_params=_params("parallel"),
    )(proj, proj, proj, *consts)


def _odd_core_bwd(tag, proj, p, d_mix, nb, s):
    t = proj.shape[0]
    nchunk = s // SGU_CHUNK

    def body(p_ref, u_ref, v_ref, pw_ref, ps_ref, sn_ref, ws_ref, bt_ref, dm_ref,
             dproj_ref, dpw_ref, dps_ref, dsn_ref, dws_ref, dbt_ref, pad_ref):
        b = pl.program_id(0)

        @pl.when(b == 0)
        def _():
            dpw_ref[...] = jnp.zeros_like(dpw_ref)
            dps_ref[...] = jnp.zeros_like(dps_ref)
            dsn_ref[...] = jnp.zeros_like(dsn_ref)
            dws_ref[...] = jnp.zeros_like(dws_ref)
            dbt_ref[...] = jnp.zeros_like(dbt_ref)

        _zero_halo(pad_ref, s)
        for g, r in enumerate(POOL_RADII):
            cols = slice(g * POOL_GROUP, (g + 1) * POOL_GROUP)
            pg = p_ref[:, cols].astype(F32)
            cnt = _count(r, s)
            pooled = _window(pad_ref, pg, r, s) / cnt - pg
            c_pre = _dot(pooled, pw_ref[g])
            dc = dm_ref[:, cols]
            dps_ref[:, cols] += jnp.sum(dc * c_pre, axis=0, keepdims=True)
            dcp = dc * ps_ref[:, cols]
            dpw_ref[g] += _dot(pooled, dcp, "tn")
            dpooled = _dot(dcp, pw_ref[g], "nt")
            dproj_ref[:, cols] = (_window(pad_ref, dpooled / cnt, r, s) - dpooled).astype(dproj_ref.dtype)
        norm = sn_ref[...]
        bt = bt_ref[...]

        def chunk(n, carry):
            rows = pl.ds(pl.multiple_of(n * SGU_CHUNK, SGU_CHUNK), SGU_CHUNK)
            uu, vv, gu, r, xhat, vn, mixed = _sgu_chunk(u_ref, v_ref, norm, ws_ref, bt, rows)
            dd = dm_ref[rows, HALF:D_MODEL]
            dgu, dvn = [], []
            for g in range(N_GROUPS):
                cols = slice(g * SGU_GROUP, (g + 1) * SGU_GROUP)
                dgu.append(dd[:, cols] * mixed[g])
                dmx = dd[:, cols] * gu[:, cols]
                dbt_ref[:, g:g + 1] += jnp.sum(dmx, axis=-1, keepdims=True)
                dws_ref[g] += _dot(dmx, vn[:, cols], "nt")
                dvn.append(_dot(ws_ref[g], dmx, "tn"))
            dgu = jnp.concatenate(dgu, axis=1)
            dvn = jnp.concatenate(dvn, axis=1)
            dsn_ref[...] += jnp.sum(dvn * xhat, axis=0, keepdims=True)
            dgx = dvn * norm
            m = jnp.mean(dgx * xhat, axis=-1, keepdims=True)
            dgv = r * (dgx - xhat * m)
            dproj_ref[rows, HALF:2 * HALF] = (dgu * _gelu_grad(uu)).astype(dproj_ref.dtype)
            dproj_ref[rows, 2 * HALF:3 * HALF] = (dgv * _gelu_grad(vv)).astype(dproj_ref.dtype)
            return carry

        lax.fori_loop(0, nchunk, chunk, 0)

    def col(j):
        return pl.BlockSpec((s, HALF), lambda b: (b, j))

    def whole(a):
        return pl.BlockSpec(a.shape, lambda b: (0,) * a.ndim)

    consts = [p["pool_w"], p["pool_scale"].reshape(1, HALF), p["sgu_norm"].reshape(1, HALF),
              p["sgu_w"], p["sgu_b"].T]
    gshapes = [jax.ShapeDtypeStruct(a.shape, F32) for a in consts]
    dproj, dpw, dps, dsn, dws, dbt = pl.pallas_call(
        body, name=tag + "_dcore", grid=(nb,),
        in_specs=[col(0), col(1), col(2)] + [whole(a) for a in consts]
        + [pl.BlockSpec((s, D_MODEL), lambda b: (b, 0))],
        out_specs=[pl.BlockSpec((s, 3 * HALF), lambda b: (b, 0))] + [whole(a) for a in consts],
        out_shape=[jax.ShapeDtypeStruct((t, 3 * HALF), _ACT)] + gshapes,
        scratch_shapes=[pltpu.VMEM((s + 2 * HALO, POOL_GROUP), F32)],
        compiler_params=_params("arbitrary"),
    )(proj, proj, proj, *consts, d_mix)
    return dproj, dict(pool_w=dpw, pool_scale=dps.reshape(HALF), sgu_norm=dsn.reshape(HALF), sgu_w=dws, sgu_b=dbt.T)


def _odd_fwd(tag, x, p, nb, s):
    h = _rmsnorm_fwd(tag + "_norm", x, p["norm"])
    proj = _proj_in(tag, h, p["w_in"])
    mix = _odd_core_fwd(tag, proj, p, nb, s)
    x_out = _proj_out(tag, x, [mix], p["w_out"])
    return x_out, (x, h, proj, mix)


def _odd_bwd(tag, dxo, dxo_lo, saved, p, nb, s, mid, ready):
    x, h, proj, mix = saved
    d_mix, dw_out = _proj_out_bwd(tag, dxo_lo, [mix], p["w_out"])
    p = dict(p, pool_scale=p["pool_scale"] + mid(d_mix))
    dproj, grads = _odd_core_bwd(tag, proj, p, d_mix, nb, s)
    dx, dx_lo, dnorm = _proj_in_bwd(tag, h, [dproj], p["w_in"], x, dxo, p["norm"],
                                    lambda dw_in: ready(dict(w_in=dw_in, w_out=dw_out), dw_in))
    grads.update(norm=dnorm)
    return dx, dx_lo, grads


def _local_step(x3, target3, depth, weights_of, final_norm, mid, grads_ready, small_done):
    nb, s, d = x3.shape
    t = nb * s
    x = x3.reshape(t, d)
    target = target3.reshape(t, d)
    cos, sin = _rope_tables(s)
    saved, ws = [], []
    for l in range(depth):
        w1 = weights_of(l, "ffn1", x)
        x, s1 = _ffn_fwd(f"l{l}_ffn1", x, w1["norm"], w1["w_in4"], w1["w_out"], _ffn_tiles(l, 1))
        wm = weights_of(l, "mix", x)
        if l % 2 == 0:
            x, s2 = _even_fwd(f"l{l}_ev", x, wm, cos, sin, nb, s)
        else:
            x, s2 = _odd_fwd(f"l{l}_od", x, wm, nb, s)
        w2 = weights_of(l, "ffn2", x)
        x, s3 = _ffn_fwd(f"l{l}_ffn2", x, w2["norm"], w2["w_in4"], w2["w_out"], _ffn_tiles(l, 2))
        saved.append((s1, s2, s3))
        ws.append((w1, wm, w2))
    loss, dx, dx_lo, dfinal = _final_loss("final_loss", x, final_norm, target)
    for l in reversed(range(depth)):
        s1, s2, s3 = saved[l]
        w1, wm, w2 = ws[l]

        def ready(block):
            return lambda grads, a: grads_ready(l, block, grads, a)

        dx, dx_lo, dn = _ffn_bwd(f"l{l}_ffn2", dx, dx_lo, s3, w2["norm"], w2["w_in4"], w2["w_out"], _ffn_tiles(l, 2), mid,
                                 ready("ffn2"))
        small_done(l, "ffn2", dict(norm=dn))
        if l % 2 == 0:
            dx, dx_lo, gm = _even_bwd(f"l{l}_ev", dx, dx_lo, s2, wm, cos, sin, nb, s, mid, ready("mix"))
        else:
            dx, dx_lo, gm = _odd_bwd(f"l{l}_od", dx, dx_lo, s2, wm, nb, s, mid, ready("mix"))
        small_done(l, "mix", gm)
        dx, dx_lo, dn = _ffn_bwd(f"l{l}_ffn1", dx, dx_lo, s1, w1["norm"], w1["w_in4"], w1["w_out"], _ffn_tiles(l, 1), mid,
                                 ready("ffn1"))
        small_done(l, "ffn1", dict(norm=dn))
    return loss, dx.reshape(nb, s, d), dfinal


_HBM = pl.BlockSpec(memory_space=pltpu.HBM)


def _place():
    x, y, c = lax.axis_index("x"), lax.axis_index("y"), lax.axis_index("c")
    chips = [(1 - x, y), (x, 1 - y), (1 - x, 1 - y)]
    return x, y, c, chips


def _remote(src, dst, send_sem, recv_sem, to):
    return pltpu.make_async_remote_copy(src_ref=src, dst_ref=dst, send_sem=send_sem, recv_sem=recv_sem,
                                        device_id=to, device_id_type=_MESH)


def _gather_shards(arrs, small):
    n = len(arrs)
    own = 6

    def body(*refs):
        ins, sm_in = refs[:n], refs[n]
        outs, sm_out = refs[n + 1:2 * n + 1], refs[2 * n + 1]
        send, recv = refs[2 * n + 2:]
        x, y, c, chips = _place()
        k = 2 * x + y
        sib = (x, y, 1 - c)
        started = []
        for a in range(n + 1):
            src, dst = (ins[a], outs[a]) if a < n else (sm_in, sm_out)
            cp = _remote(src, dst.at[k], send.at[a, own], recv.at[a, own], sib)
            cp.start()
            started.append(cp)
            if a < n:
                h = src.shape[0] // 2
                mine = pl.ds(c * h, h)
                src_part, dst_part = src.at[mine], dst.at[k, mine]
            else:
                src_part, dst_part = src, dst.at[k]
            for j, chip in enumerate(chips):
                cp = _remote(src_part, dst_part, send.at[a, j], recv.at[a, j], (*chip, c))
                cp.start()
                started.append(cp)
        for a in range(n):
            h = ins[a].shape[0] // 2
            mine = pl.ds(c * h, h)
            for j, (px, py) in enumerate(chips):
                landed = outs[a].at[2 * px + py, mine]
                _remote(landed, landed, send.at[a, j], recv.at[a, j], (px, py, c)).wait_recv()
                cp = _remote(landed, landed, send.at[a, 3 + j], recv.at[a, 3 + j], sib)
                cp.start()
                started.append(cp)
        for a in range(n):
            h = ins[a].shape[0] // 2
            other = pl.ds((1 - c) * h, h)
            for j, (px, py) in enumerate(chips):
                passed = outs[a].at[2 * px + py, other]
                _remote(passed, passed, send.at[a, 3 + j], recv.at[a, 3 + j], sib).wait_recv()
        for j, (px, py) in enumerate(chips):
            landed = sm_out.at[2 * px + py]
            _remote(landed, landed, send.at[n, j], recv.at[n, j], (px, py, c)).wait_recv()
        for a in range(n + 1):
            filled = (outs[a] if a < n else sm_out).at[k]
            _remote(filled, filled, send.at[a, own], recv.at[a, own], sib).wait_recv()
        for cp in started:
            cp.wait_send()

    outs = pl.pallas_call(
        body, name="gather_shards",
        in_specs=[_HBM] * (n + 1), out_specs=[_HBM] * (n + 1),
        out_shape=[jax.ShapeDtypeStruct((N_CHIPS,) + a.shape, a.dtype) for a in list(arrs) + [small]],
        scratch_shapes=[pltpu.SemaphoreType.DMA((n + 1, 7)), pltpu.SemaphoreType.DMA((n + 1, 7))],
    )(*arrs, small)
    return outs[:n], outs[n]


_SEM = pl.BlockSpec(memory_space=pltpu.SEMAPHORE)
_EFFECT = pltpu.SideEffectType.DATAFLOW_SIDE_EFFECTING


def _gather_plan(i, src, land, k, c, chips, sib):
    mine = pl.ds(c * (src.shape[0] // 2), src.shape[0] // 2)
    plan = [(src.at[mine], land.at[k, mine], (px, py, c), land.at[2 * px + py, mine]) for px, py in chips]
    return plan + [(src, land.at[k], sib, land.at[k])]


def _pass_plan(i, src, land, k, c, chips, sib):
    h = src.shape[1] // 2
    mine, theirs = pl.ds(c * h, h), pl.ds((1 - c) * h, h)
    return [(src.at[2 * px + py, mine], land.at[2 * px + py, mine], sib, land.at[2 * px + py, theirs]) for px, py in chips]


def _swap_plan(i, src, land, k, c, chips, sib):
    h = src.shape[1] // 2
    return [(src.at[:, pl.ds((1 - c) * h, h)], land, sib, land)]


def _scatter_plan(i, src, land, k, c, chips, sib):
    return [(src.at[2 * px + py], land.at[k], (px, py, c), land.at[2 * px + py]) for px, py in chips]


def _join_plan(layers):
    def plan(i, src, land, k, c, chips, sib):
        h = src.shape[1] // 2
        mine, theirs = pl.ds(c * h, h), pl.ds((1 - c) * h, h)
        return [(src.at[layers[i], mine], land.at[layers[i], mine], sib, land.at[layers[i], theirs])]
    return plan


def _split_start(name, plan, ncopy, srcs, after, land_shapes=None):
    n = len(srcs)
    if land_shapes is None:
        land_shapes = [(N_CHIPS,) + a.shape[-2:] for a in srcs]
    in_place = land_shapes == "self"
    lands = [] if in_place else [pltpu.with_memory_space_constraint(lax.empty(shape, a.dtype), pltpu.HBM)
                                 for shape, a in zip(land_shapes, srcs)]
    nbuf = n + len(lands)

    def body(*refs):
        src_refs = refs[1:1 + n]
        land_refs = src_refs if in_place else refs[1 + n:1 + nbuf]
        send, recv, token = refs[1 + nbuf], refs[2 + nbuf], refs[-1]
        x, y, c, chips = _place()
        for i in range(n):
            for j, (src, dst, peer, _) in enumerate(plan(i, src_refs[i], land_refs[i], 2 * x + y, c, chips, (x, y, 1 - c))):
                _remote(src, dst, send.at[i * ncopy + j], recv.at[i * ncopy + j], peer).start()
        token[...] = jnp.zeros_like(token)

    outs = pl.pallas_call(
        body, name=name,
        in_specs=[_ANY] + [_HBM] * nbuf,
        out_specs=[_SEM, _SEM] + [_HBM] * nbuf + [_VMEM],
        out_shape=[pltpu.SemaphoreType.DMA((n * ncopy,)), pltpu.SemaphoreType.DMA((n * ncopy,))]
        + [pltpu.HBM(a.shape, a.dtype) for a in list(srcs) + lands] + [jax.ShapeDtypeStruct((8, LANES), F32)],
        input_output_aliases={1 + i: 2 + i for i in range(nbuf)},
        compiler_params=pltpu.CompilerParams(has_side_effects=_EFFECT),
    )(after, *[pltpu.with_memory_space_constraint(a, pltpu.HBM) for a in srcs], *lands)
    return outs[0], outs[1], outs[2:2 + n], None if in_place else outs[2 + n:2 + nbuf], outs[-1]


def _split_wait(name, plan, started, after):
    send, recv, srcs, lands = started
    n = len(srcs)
    ncopy = send.shape[0] // n
    in_place = lands is None
    bufs = list(srcs) + ([] if in_place else list(lands))
    nbuf = len(bufs)

    def body(*refs):
        src_refs = refs[:n]
        land_refs = src_refs if in_place else refs[n:nbuf]
        send, recv = refs[nbuf], refs[nbuf + 1]
        x, y, c, chips = _place()
        for i in range(n):
            for j, (src, _, peer, landed) in enumerate(plan(i, src_refs[i], land_refs[i], 2 * x + y, c, chips, (x, y, 1 - c))):
                cp = _remote(src, landed, send.at[i * ncopy + j], recv.at[i * ncopy + j], peer)
                cp.wait_send()
                cp.wait_recv()

    outs = pl.pallas_call(
        body, name=name,
        in_specs=[_HBM] * nbuf + [_SEM, _SEM, _ANY],
        out_specs=[_HBM] * nbuf,
        out_shape=[pltpu.HBM(a.shape, a.dtype) for a in bufs],
        input_output_aliases={i: i for i in range(nbuf)},
        compiler_params=pltpu.CompilerParams(has_side_effects=_EFFECT),
    )(*bufs, send, recv, after)
    return outs[:n], outs[:n] if in_place else outs[n:]


def _allreduce_small(buf, after):
    rows = buf.shape[0]
    piece = rows // N_DEV

    def body(in_ref, after_ref, out_ref, land_ref, send, recv):
        x, y, c, _ = _place()
        me = 4 * x + 2 * y + c
        peers = [(1 - x if r & 4 else x, 1 - y if r & 2 else y, 1 - c if r & 1 else c) for r in range(1, N_DEV)]

        def rows_of(dev):
            return pl.ds(pl.multiple_of(dev * piece, 8), piece)

        first, second = [], []
        for r, (px, py, pc) in enumerate(peers):
            cp = _remote(in_ref.at[rows_of(4 * px + 2 * py + pc)], land_ref.at[me], send.at[0, r], recv.at[0, r], (px, py, pc))
            cp.start()
            first.append(cp)
        land_ref[me] = in_ref[rows_of(me), :]
        for r, (px, py, pc) in enumerate(peers):
            landed = land_ref.at[4 * px + 2 * py + pc]
            _remote(landed, landed, send.at[0, r], recv.at[0, r], (px, py, pc)).wait_recv()
        acc = land_ref[0]
        for d in range(1, N_DEV):
            acc = acc + land_ref[d]
        out_ref[rows_of(me), :] = acc
        for r, peer in enumerate(peers):
            cp = _remote(out_ref.at[rows_of(me)], out_ref.at[rows_of(me)], send.at[1, r], recv.at[1, r], peer)
            cp.start()
            second.append(cp)
        for r, (px, py, pc) in enumerate(peers):
            landed = out_ref.at[rows_of(4 * px + 2 * py + pc)]
            _remote(landed, landed, send.at[1, r], recv.at[1, r], (px, py, pc)).wait_recv()
        for cp in first + second:
            cp.wait_send()

    return pl.pallas_call(
        body, name="allreduce_small",
        in_specs=[_VMEM, _ANY], out_specs=_VMEM,
        out_shape=jax.ShapeDtypeStruct(buf.shape, F32),
        scratch_shapes=[pltpu.VMEM((N_DEV, piece, LANES), F32), pltpu.SemaphoreType.DMA((2, N_DEV - 1)),
                        pltpu.SemaphoreType.DMA((2, N_DEV - 1))],
        compiler_params=pltpu.CompilerParams(vmem_limit_bytes=_VMEM_LIMIT),
    )(buf, after)


def _div_tile(n, cap, mult):
    best = None
    for d in range(mult, min(n, cap) + 1, mult):
        if n % d == 0:
            best = d
    return best if best is not None else n


def _add_sibling(name, grad, got, c):
    nk, hr, cc = got.shape
    tr = _div_tile(hr, 512, 16)
    nt = hr // tr

    def body(c_ref, g_ref, o_ref, s_ref):
        s_ref[...] = (g_ref[...].astype(F32) + o_ref[...].astype(F32)).astype(s_ref.dtype)

    blk = (None, tr, cc)
    return pl.pallas_call(
        body, name=name,
        grid_spec=pltpu.PrefetchScalarGridSpec(
            num_scalar_prefetch=1, grid=(nk, nt),
            in_specs=[pl.BlockSpec(blk, lambda i, q, c_ref: (i, c_ref[0] * nt + q, 0)),
                      pl.BlockSpec(blk, lambda i, q, c_ref: (i, q, 0))],
            out_specs=pl.BlockSpec(blk, lambda i, q, c_ref: (i, q, 0))),
        out_shape=jax.ShapeDtypeStruct(got.shape, got.dtype),
        compiler_params=_params("parallel", "parallel"),
    )(c, grad, got)


def _add_chips(name, mine, got, place, buf, l):
    nk, hr, cc = got.shape
    tr = _div_tile(hr, 512, 16)
    nt = hr // tr

    def body(*refs):
        acc = refs[1][...].astype(F32)
        for q in range(1, nk):
            acc = acc + refs[1 + q][...].astype(F32)
        refs[2 + nk][...] = acc

    def part(q):
        return pl.BlockSpec((None, tr, cc), lambda i, p_ref: ((p_ref[0] + q) % nk, i, 0))

    return pl.pallas_call(
        body, name=name,
        grid_spec=pltpu.PrefetchScalarGridSpec(
            num_scalar_prefetch=1, grid=(nt,),
            in_specs=[part(q) for q in range(nk)] + [_ANY],
            out_specs=pl.BlockSpec((None, tr, cc), lambda i, p_ref: (l, p_ref[1] * nt + i, 0))),
        out_shape=jax.ShapeDtypeStruct(buf.shape, F32),
        input_output_aliases={1 + nk: 0},
        compiler_params=_params("parallel"),
    )(place, mine, *([got] * (nk - 1)), buf)


def _adamw(name, w, g, m, v, after=None):
    shape = w.shape
    cols = shape[-1]
    rows = w.size // cols
    tr = rows if rows * cols <= 2 ** 18 else _div_tile(rows, max(8, 2 ** 18 // cols), 8)
    c1 = 1.0 - ADAM_B1 ** ADAM_STEP
    c2 = 1.0 - ADAM_B2 ** ADAM_STEP
    extra = [] if after is None else [after]

    def body(*refs):
        w_ref, g_ref, m_ref, v_ref = refs[:4]
        d_ref, mo_ref, vo_ref, go_ref = refs[4 + len(extra):]
        gg = g_ref[...]
        mn = ADAM_B1 * m_ref[...] + (1.0 - ADAM_B1) * gg
        vn = ADAM_B2 * v_ref[...] + (1.0 - ADAM_B2) * (gg * gg)
        d_ref[...] = -ADAM_LR * ((mn / c1) / (jnp.sqrt(vn / c2) + ADAM_EPS) + ADAM_WD * w_ref[...])
        mo_ref[...] = mn
        vo_ref[...] = vn
        go_ref[...] = gg

    blk = pl.BlockSpec((tr, cols), lambda i: (i, 0))
    sds = jax.ShapeDtypeStruct((rows, cols), F32)
    outs = pl.pallas_call(
        body, name=name, grid=(rows // tr,),
        in_specs=[blk] * 4 + [_ANY] * len(extra), out_specs=(blk,) * 4, out_shape=(sds,) * 4,
        compiler_params=_params("parallel"),
    )(*[a.reshape(rows, cols) for a in (w, g, m, v)], *extra)
    return [o.reshape(shape) for o in outs]


_WEIGHTS = ["ffn1_norm", "ffn1_w_in", "ffn1_w_out", "mix_norm", "ffn2_norm", "ffn2_w_in", "ffn2_w_out",
            "ev_w_in", "ev_conv_w", "ev_q_norm", "ev_k_norm", "ev_w_out", "od_w_in", "od_pool_w",
            "od_pool_scale", "od_sgu_norm", "od_sgu_w", "od_sgu_b", "od_w_out", "final_norm"]
_BIG = ["ffn1_w_in", "ffn1_w_out", "ffn2_w_in", "ffn2_w_out", "ev_w_in", "ev_w_out", "od_w_in", "od_w_out"]
_SMALL_SHARDED = ["ev_conv_w", "od_pool_scale", "od_sgu_norm"]


def _pad_rows(a, mult=8):
    pad = (-a.shape[0]) % mult
    return a if pad == 0 else jnp.concatenate([a, jnp.zeros((pad,) + a.shape[1:], a.dtype)], axis=0)


def _join_cols(g):
    return g.transpose(1, 0, 2).reshape(g.shape[1], N_CHIPS * g.shape[2])


def _split_cols(w):
    return w.reshape(w.shape[0], N_CHIPS, w.shape[1] // N_CHIPS).transpose(1, 0, 2)


def kernel(x, ffn1_norm, ffn1_w_in, ffn1_w_out, mix_norm, ffn2_norm, ffn2_w_in, ffn2_w_out, ev_w_in, ev_conv_w,
           ev_q_norm, ev_k_norm, ev_w_out, od_w_in, od_pool_w, od_pool_scale, od_sgu_norm, od_sgu_w, od_sgu_b,
           od_w_out, final_norm, loss_target, m_ffn1_norm, m_ffn1_w_in, m_ffn1_w_out, m_mix_norm, m_ffn2_norm,
           m_ffn2_w_in, m_ffn2_w_out, m_ev_w_in, m_ev_conv_w, m_ev_q_norm, m_ev_k_norm, m_ev_w_out, m_od_w_in,
           m_od_pool_w, m_od_pool_scale, m_od_sgu_norm, m_od_sgu_w, m_od_sgu_b, m_od_w_out, m_final_norm, v_ffn1_norm,
           v_ffn1_w_in, v_ffn1_w_out, v_mix_norm, v_ffn2_norm, v_ffn2_w_in, v_ffn2_w_out, v_ev_w_in, v_ev_conv_w,
           v_ev_q_norm, v_ev_k_norm, v_ev_w_out, v_od_w_in, v_od_pool_w, v_od_pool_scale, v_od_sgu_norm, v_od_sgu_w,
           v_od_sgu_b, v_od_w_out, v_final_norm):
    return _step(x, ffn1_norm, ffn1_w_in, ffn1_w_out, mix_norm, ffn2_norm, ffn2_w_in, ffn2_w_out, ev_w_in, ev_conv_w,
                 ev_q_norm, ev_k_norm, ev_w_out, od_w_in, od_pool_w, od_pool_scale, od_sgu_norm, od_sgu_w, od_sgu_b,
                 od_w_out, final_norm, loss_target, m_ffn1_norm, m_ffn1_w_in, m_ffn1_w_out, m_mix_norm, m_ffn2_norm,
                 m_ffn2_w_in, m_ffn2_w_out, m_ev_w_in, m_ev_conv_w, m_ev_q_norm, m_ev_k_norm, m_ev_w_out, m_od_w_in,
                 m_od_pool_w, m_od_pool_scale, m_od_sgu_norm, m_od_sgu_w, m_od_sgu_b, m_od_w_out, m_final_norm,
                 v_ffn1_norm, v_ffn1_w_in, v_ffn1_w_out, v_mix_norm, v_ffn2_norm, v_ffn2_w_in, v_ffn2_w_out,
                 v_ev_w_in, v_ev_conv_w, v_ev_q_norm, v_ev_k_norm, v_ev_w_out, v_od_w_in, v_od_pool_w,
                 v_od_pool_scale, v_od_sgu_norm, v_od_sgu_w, v_od_sgu_b, v_od_w_out, v_final_norm)


def _step(*args):
    nw = len(_WEIGHTS)
    x = args[0]
    w = dict(zip(_WEIGHTS, args[1:1 + nw]))
    target = args[1 + nw]
    m = dict(zip(_WEIGHTS, args[2 + nw:2 + 2 * nw]))
    v = dict(zip(_WEIGHTS, args[2 + 2 * nw:2 + 3 * nw]))
    depth = w["ffn1_norm"].shape[0]
    n_even, n_odd = w["ev_w_in"].shape[0], w["od_w_in"].shape[0]
    chip = 2 * lax.axis_index("x") + lax.axis_index("y")
    place = jnp.stack([chip, lax.axis_index("c")]).astype(jnp.int32)
    core = place[1:2]

    def sharded(l, block):
        if block == "mix":
            block = "ev" if l % 2 == 0 else "od"
            return [(block + "_w_in", l // 2), (block + "_w_out", l // 2)]
        return [(block + "_w_in", l), (block + "_w_out", l)]

    def shards(group, zero):
        return [(w[n][i] + zero).astype(_ACT) for l, block in group for n, i in sharded(l, block)]

    later = ([[(0, "ffn1")], [(0, "mix"), (0, "ffn2")]]
             + [[(l, "ffn1"), (l, "mix"), (l, "ffn2")] for l in range(1, depth)])
    small_rows = [w["ev_conv_w"].reshape(3 * n_even, LANES), w["od_pool_scale"], w["od_sgu_norm"]]
    _, small = _gather_shards([], _pad_rows(jnp.concatenate(small_rows, axis=0)))
    conv_w = small[:, :3 * n_even].reshape(N_CHIPS, n_even, 3, LANES).transpose(1, 2, 0, 3).reshape(n_even, 3, CONV_WIDTH)
    pool_scale = small[:, 3 * n_even:3 * n_even + n_odd].transpose(1, 0, 2).reshape(n_odd, HALF)
    sgu_norm = small[:, 3 * n_even + n_odd:3 * n_even + 2 * n_odd].transpose(1, 0, 2).reshape(n_odd, HALF)
    gathering, after, zero = [], small, 0.0
    for i, group in enumerate(later):
        gathering.append(_split_start(f"gather_start{i}", _gather_plan, N_CHIPS, shards(group, zero), after))
        after = gathering[-1][4]
        zero = after[0, 0]
    gathered = {}

    def rows(g):
        return g.reshape(N_CHIPS * g.shape[1], g.shape[2])

    passing = {}

    def fetch(i, x_in):
        got = _split_wait(f"gather_wait{i}", _gather_plan, gathering[i][:4], x_in)[1]
        passing[i] = _split_start(f"pass_start{i}", _pass_plan, N_CHIPS - 1, got, x_in, "self")
        return passing[i][4][0, 0]

    def weights_of(l, block, x_in):
        zero = after[0, 0] if (l, block) == (0, "ffn1") else 0.0
        if (l, block) not in gathered:
            i = next(i for i, group in enumerate(later) if (l, block) in group)
            if i not in passing:
                zero = zero + fetch(i, x_in)
            got = _split_wait(f"pass_wait{i}", _pass_plan, passing.pop(i)[:4], x_in)[0]
            for n, key in enumerate(later[i]):
                gathered[key] = got[2 * n:2 * n + 2]
        if block == "ffn2" and l + 1 < depth:
            zero = zero + fetch(next(i for i, group in enumerate(later) if (l + 1, "ffn1") in group), x_in)
        w_in, w_out = gathered[(l, block)]
        if block != "mix":
            return dict(norm=w[block + "_norm"][l] + zero, w_in4=w_in, w_out=rows(w_out))
        j = l // 2
        if l % 2 == 0:
            mix = dict(conv_w=conv_w[j], q_gain=w["ev_q_norm"][j], k_gain=w["ev_k_norm"][j])
        else:
            mix = dict(pool_w=w["od_pool_w"][j], pool_scale=pool_scale[j], sgu_norm=sgu_norm[j],
                       sgu_w=w["od_sgu_w"][j], sgu_b=w["od_sgu_b"][j])
        return dict(mix, norm=w["mix_norm"][l] + zero, w_in=_join_cols(w_in), w_out=rows(w_out))

    def by_chip(dw):
        return dw.reshape(N_CHIPS, dw.shape[0] // N_CHIPS, dw.shape[1])

    bufs = {n: lax.empty(w[n].shape, F32) for n in _BIG}
    small_grads = {n: [None] * w[n].shape[0] for n in _WEIGHTS if n not in _BIG and n != "final_norm"}
    swapping, scattering, joining, group = [], [], [], []

    def finish_swap(after):
        tag, names, started = swapping.pop()
        local, from_sibling = _split_wait(f"swap_wait{tag}", _swap_plan, started[:4], after)
        halves = [_add_sibling(f"add_sibling{tag}_{n}", a, b, core) for (n, _), a, b in zip(names, local, from_sibling)]
        scattering.append((tag, names, _split_start(f"scatter_start{tag}", _scatter_plan, N_CHIPS - 1, halves, after)))
        return scattering[-1][2][4][0, 0]

    def finish_scatter(after):
        tag, names, started = scattering.pop(0)
        halves, got = _split_wait(f"scatter_wait{tag}", _scatter_plan, started[:4], after)
        for i, (n, j) in enumerate(names):
            bufs[n] = _add_chips(f"add_chips{tag}_{n}", halves[i], got[i], place, bufs[n], j)
        layers = [j for _, j in names]
        started = _split_start(f"join_start{tag}", _join_plan(layers), 1, [bufs[n] for n, _ in names], after, "self")
        for (n, _), b in zip(names, started[2]):
            bufs[n] = b
        joining.append((tag, names, layers, started))
        return started[4]

    def finish_join(after):
        tag, names, layers, started = joining.pop(0)
        joined = _split_wait(f"join_wait{tag}", _join_plan(layers), (started[0], started[1], [bufs[n] for n, _ in names], None),
                             after)[0]
        for (n, _), b in zip(names, joined):
            bufs[n] = b

    def small_done(l, block, g):
        if block == "mix":
            renamed = (dict(conv_w="ev_conv_w", q_gain="ev_q_norm", k_gain="ev_k_norm") if l % 2 == 0 else
                       dict(pool_w="od_pool_w", pool_scale="od_pool_scale", sgu_norm="od_sgu_norm", sgu_w="od_sgu_w",
                            sgu_b="od_sgu_b"))
            for key, n in renamed.items():
                small_grads[n][l // 2] = g[key]
        small_grads[block + "_norm"][l] = g["norm"]

    def mid(a):
        zero = 0.0
        if swapping:
            if scattering:
                zero = zero + finish_scatter(a)[0, 0]
            zero = zero + finish_swap(a)
        return zero

    def grads_ready(l, block, g, a):
        local = [_split_cols(g["w_in"]) if block == "mix" else g["w_in4"], by_chip(g["w_out"])]
        group.extend(zip(sharded(l, block), local))
        if block == "ffn2" or (block == "mix" and l > 0):
            return 0.0
        tag = f"{l}_{block}"
        names, local = [n for n, _ in group], [b for _, b in group]
        group.clear()
        shapes = [(N_CHIPS, b.shape[1] // 2, b.shape[2]) for b in local]
        swapping.append((tag, names, _split_start(f"swap_start{tag}", _swap_plan, 1, local, core, shapes)))
        return swapping[-1][2][4][0, 0]

    loss_part, grad_x, dfinal = _local_step(x, target, depth, weights_of, w["final_norm"], mid, grads_ready, small_done)
    loss = lax.psum(loss_part, ("x", "y", "c"))

    grads, updates = {}, {}

    def update(n, after):
        updates[n] = _adamw("adamw_" + n, w[n], grads[n] if n in grads else bufs[n], m[n], v[n], after)
        return updates[n][1]

    finish_swap(grad_x)
    behind = scattering[-1][2][4]
    while len(joining) > 0:
        finish_join(behind)
    behind = finish_scatter(behind)
    for n in ("od_w_in", "od_w_out"):
        behind = update(n, behind)
    finish_join(behind)
    for n in ("ffn2_w_in", "ffn2_w_out", "ev_w_in", "ev_w_out"):
        behind = update(n, behind)
    behind = finish_scatter(behind)
    small_grads = {n: jnp.stack(parts) for n, parts in small_grads.items()}
    small_grads["final_norm"] = dfinal
    names = list(small_grads)
    flat = jnp.concatenate([small_grads[n].reshape(-1) for n in names])
    total = flat.shape[0]
    flat = jnp.concatenate([flat, jnp.zeros((-total) % (N_DEV * 8 * LANES), F32)])
    summed = _allreduce_small(flat.reshape(-1, LANES), behind).reshape(-1)
    finish_join(summed)
    for n in ("ffn1_w_in", "ffn1_w_out"):
        behind = update(n, behind)
    off = 0
    for n in names:
        size = small_grads[n].size
        full_grad = summed[off:off + size].reshape(small_grads[n].shape)
        off += size
        if n in _SMALL_SHARDED:
            full_grad = lax.dynamic_slice_in_dim(full_grad, chip * LANES, LANES, axis=full_grad.ndim - 1)
        grads[n] = full_grad
    for n in _WEIGHTS:
        if n not in updates:
            behind = update(n, behind)
    return (loss, grad_x, *[updates[n][3] for n in _WEIGHTS], *[updates[n][0] for n in _WEIGHTS],
            *[updates[n][1] for n in _WEIGHTS], *[updates[n][2] for n in _WEIGHTS])
```
